```python
import jax, jax.numpy as jnp
from jax import lax
import numpy as np

D_MODEL = 1024
BATCH = 8
SEQ = 4096
DEPTH = 4

N_A_LAYERS = DEPTH // 2
N_B_LAYERS = DEPTH - N_A_LAYERS
CONV_WIDTH = 3
D_FF = 2816
N_HEADS = 8
QK_NOPE_DIM = 128
QK_ROPE_DIM = 64
V_HEAD_DIM = 128
Q_LORA_RANK = 512
KV_LORA_RANK = 256
ROPE_THETA = 10000.0
RMS_EPS = 1e-6
Q_BLOCK = 128

kernel_name = "yoco_shortconv_mla_convffn_trunk"


def rms_norm(x, g):
    xf = x.astype(jnp.float32)
    y = xf * lax.rsqrt(jnp.mean(xf * xf, axis=-1, keepdims=True) + RMS_EPS)
    return (y * g.astype(jnp.float32)).astype(x.dtype)


def causal_dwconv(x, w):
    c = x.shape[-1]
    return lax.conv_general_dilated(
        x, w[:, None, :].astype(x.dtype), window_strides=(1,),
        padding=((CONV_WIDTH - 1, 0),), dimension_numbers=("NWC", "WIO", "NWC"),
        feature_group_count=c)


def rope_tables(seq, dtype):
    inv = 1.0 / (ROPE_THETA ** (jnp.arange(0, QK_ROPE_DIM, 2, dtype=jnp.float32) / QK_ROPE_DIM))
    ang = jnp.arange(seq, dtype=jnp.float32)[:, None] * inv[None, :]
    return jnp.cos(ang).astype(dtype), jnp.sin(ang).astype(dtype)


def apply_rope(x, cos, sin):
    x1, x2 = jnp.split(x, 2, axis=-1)
    return jnp.concatenate([x1 * cos - x2 * sin, x1 * sin + x2 * cos], axis=-1)


def short_conv_mixer(x_n, w_in, conv_w, w_out):
    b, c, h = jnp.split(x_n @ w_in, 3, axis=-1)
    return (b * causal_dwconv(c * h, conv_w)) @ w_out


def conv_ffn(x_n, w_up, conv_w, w_down):
    g, u = jnp.split(causal_dwconv(x_n @ w_up, conv_w), 2, axis=-1)
    return (jax.nn.silu(g) * u) @ w_down


def shared_latent_kv(h, kv_in_norm, w_dkv, kv_norm, w_ukv, cos, sin):
    bsz, seq, _ = h.shape
    ckv = rms_norm(h, kv_in_norm) @ w_dkv
    c_kv, k_rope = ckv[..., :KV_LORA_RANK], ckv[..., KV_LORA_RANK:]
    k_rope = apply_rope(k_rope, cos, sin)
    kv = (rms_norm(c_kv, kv_norm) @ w_ukv).reshape(bsz, seq, N_HEADS, QK_NOPE_DIM + V_HEAD_DIM)
    return kv[..., :QK_NOPE_DIM], kv[..., QK_NOPE_DIM:], k_rope


def mla_attention(x_n, w_dq, q_norm, w_uq, w_o, k_nope, v, k_rope, cos, sin):
    bsz, seq, _ = x_n.shape
    q = (rms_norm(x_n @ w_dq, q_norm) @ w_uq).reshape(bsz, seq, N_HEADS, QK_NOPE_DIM + QK_ROPE_DIM)
    q_nope = q[..., :QK_NOPE_DIM]
    q_rope = apply_rope(q[..., QK_NOPE_DIM:], cos[:, None, :], sin[:, None, :])
    n_blk = seq // Q_BLOCK

    def to_blocks(t):
        return jnp.moveaxis(t.reshape(bsz, n_blk, Q_BLOCK, *t.shape[2:]), 1, 0)

    k_pos = jnp.arange(seq)
    scale = (QK_NOPE_DIM + QK_ROPE_DIM) ** -0.5

    def attend_block(args):
        qn, qr, blk = args
        s = (jnp.einsum('bqhd,bkhd->bhqk', qn, k_nope)
             + jnp.einsum('bqhr,bkr->bhqk', qr, k_rope)).astype(jnp.float32) * scale
        q_pos = blk * Q_BLOCK + jnp.arange(Q_BLOCK)
        s = jnp.where(k_pos[None, :] <= q_pos[:, None], s, -jnp.inf)
        p = jax.nn.softmax(s, axis=-1).astype(v.dtype)
        return jnp.einsum('bhqk,bkhd->bqhd', p, v)

    o = lax.map(attend_block, (to_blocks(q_nope), to_blocks(q_rope), jnp.arange(n_blk)))
    o = jnp.moveaxis(o, 0, 1).reshape(bsz, seq, N_HEADS * V_HEAD_DIM)
    return o @ w_o


def _fwd_setup_inputs(seed: int = 0) -> dict:
    key = jax.random.key(seed)
    ks = jax.random.split(key, 24)
    f32 = jnp.float32

    def w(k, shape, fan_in):
        return jax.random.normal(k, shape, f32) * (fan_in ** -0.5)

    def gain(k, shape):
        return 1.0 + 0.02 * jax.random.normal(k, shape, f32)

    return {
        "x": jax.random.normal(ks[0], (BATCH, SEQ, D_MODEL), f32),
        "a_mix_norm": gain(ks[1], (N_A_LAYERS, D_MODEL)),
        "a_w_in": w(ks[2], (N_A_LAYERS, D_MODEL, 3 * D_MODEL), D_MODEL),
        "a_conv": w(ks[3], (N_A_LAYERS, CONV_WIDTH, D_MODEL), CONV_WIDTH),
        "a_w_out": w(ks[4], (N_A_LAYERS, D_MODEL, D_MODEL), D_MODEL),
        "b_mix_norm": gain(ks[5], (N_B_LAYERS, D_MODEL)),
        "b_w_dq": w(ks[6], (N_B_LAYERS, D_MODEL, Q_LORA_RANK), D_MODEL),
        "b_q_norm": gain(ks[7], (N_B_LAYERS, Q_LORA_RANK)),
        "b_w_uq": w(ks[8], (N_B_LAYERS, Q_LORA_RANK, N_HEADS * (QK_NOPE_DIM + QK_ROPE_DIM)), Q_LORA_RANK),
        "b_w_o": w(ks[9], (N_B_LAYERS, N_HEADS * V_HEAD_DIM, D_MODEL), N_HEADS * V_HEAD_DIM),
        "kv_in_norm": gain(ks[10], (D_MODEL,)),
        "w_dkv": w(ks[11], (D_MODEL, KV_LORA_RANK + QK_ROPE_DIM), D_MODEL),
        "kv_norm": gain(ks[12], (KV_LORA_RANK,)),
        "w_ukv": w(ks[13], (KV_LORA_RANK, N_HEADS * (QK_NOPE_DIM + V_HEAD_DIM)), KV_LORA_RANK),
        "ffn_norm": gain(ks[14], (DEPTH, D_MODEL)),
        "ffn_w_up": w(ks[15], (DEPTH, D_MODEL, 2 * D_FF), D_MODEL),
        "ffn_conv": w(ks[16], (DEPTH, CONV_WIDTH, 2 * D_FF), CONV_WIDTH),
        "ffn_w_down": w(ks[17], (DEPTH, D_FF, D_MODEL), D_FF),
        "final_norm": gain(ks[18], (D_MODEL,)),
    }


def _fwd_reference(x, a_mix_norm, a_w_in, a_conv, a_w_out, b_mix_norm, b_w_dq, b_q_norm, b_w_uq, b_w_o,
              kv_in_norm, w_dkv, kv_norm, w_ukv, ffn_norm, ffn_w_up, ffn_conv, ffn_w_down, final_norm):
    cos, sin = rope_tables(x.shape[1], x.dtype)
    h = x
    shared = None
    for layer in range(DEPTH):
        if layer < N_A_LAYERS:
            h = h + short_conv_mixer(rms_norm(h, a_mix_norm[layer]), a_w_in[layer],
                                     a_conv[layer], a_w_out[layer])
        else:
            j = layer - N_A_LAYERS
            k_nope, v, k_rope = shared
            h = h + mla_attention(rms_norm(h, b_mix_norm[j]), b_w_dq[j], b_q_norm[j], b_w_uq[j],
                                  b_w_o[j], k_nope, v, k_rope, cos, sin)
        h = h + conv_ffn(rms_norm(h, ffn_norm[layer]), ffn_w_up[layer], ffn_conv[layer], ffn_w_down[layer])
        if layer == N_A_LAYERS - 1:
            shared = shared_latent_kv(h, kv_in_norm, w_dkv, kv_norm, w_ukv, cos, sin)
    return rms_norm(h, final_norm)


import jax as _jax
import jax.numpy as _jnp

TWIN_FORMAT = 'train_step'
FWD_PARAMS = ['x', 'a_mix_norm', 'a_w_in', 'a_conv', 'a_w_out', 'b_mix_norm', 'b_w_dq', 'b_q_norm', 'b_w_uq', 'b_w_o', 'kv_in_norm', 'w_dkv', 'kv_norm', 'w_ukv', 'ffn_norm', 'ffn_w_up', 'ffn_conv', 'ffn_w_down', 'final_norm']
TWIN_WEIGHTS = ['a_mix_norm', 'a_w_in', 'a_conv', 'a_w_out', 'b_mix_norm', 'b_w_dq', 'b_q_norm', 'b_w_uq', 'b_w_o', 'kv_in_norm', 'w_dkv', 'kv_norm', 'w_ukv', 'ffn_norm', 'ffn_w_up', 'ffn_conv', 'ffn_w_down', 'final_norm']
TWIN_DIFF_INPUT = 'x'
TWIN_INPUTS = ['x', 'a_mix_norm', 'a_w_in', 'a_conv', 'a_w_out', 'b_mix_norm', 'b_w_dq', 'b_q_norm', 'b_w_uq', 'b_w_o', 'kv_in_norm', 'w_dkv', 'kv_norm', 'w_ukv', 'ffn_norm', 'ffn_w_up', 'ffn_conv', 'ffn_w_down', 'final_norm', 'loss_target', 'm_a_mix_norm', 'm_a_w_in', 'm_a_conv', 'm_a_w_out', 'm_b_mix_norm', 'm_b_w_dq', 'm_b_q_norm', 'm_b_w_uq', 'm_b_w_o', 'm_kv_in_norm', 'm_w_dkv', 'm_kv_norm', 'm_w_ukv', 'm_ffn_norm', 'm_ffn_w_up', 'm_ffn_conv', 'm_ffn_w_down', 'm_final_norm', 'v_a_mix_norm', 'v_a_w_in', 'v_a_conv', 'v_a_w_out', 'v_b_mix_norm', 'v_b_w_dq', 'v_b_q_norm', 'v_b_w_uq', 'v_b_w_o', 'v_kv_in_norm', 'v_w_dkv', 'v_kv_norm', 'v_w_ukv', 'v_ffn_norm', 'v_ffn_w_up', 'v_ffn_conv', 'v_ffn_w_down', 'v_final_norm']
TWIN_OUTPUTS = ['loss', 'grad_x', 'grad_a_mix_norm', 'grad_a_w_in', 'grad_a_conv', 'grad_a_w_out', 'grad_b_mix_norm', 'grad_b_w_dq', 'grad_b_q_norm', 'grad_b_w_uq', 'grad_b_w_o', 'grad_kv_in_norm', 'grad_w_dkv', 'grad_kv_norm', 'grad_w_ukv', 'grad_ffn_norm', 'grad_ffn_w_up', 'grad_ffn_conv', 'grad_ffn_w_down', 'grad_final_norm', 'delta_a_mix_norm', 'delta_a_w_in', 'delta_a_conv', 'delta_a_w_out', 'delta_b_mix_norm', 'delta_b_w_dq', 'delta_b_q_norm', 'delta_b_w_uq', 'delta_b_w_o', 'delta_kv_in_norm', 'delta_w_dkv', 'delta_kv_norm', 'delta_w_ukv', 'delta_ffn_norm', 'delta_ffn_w_up', 'delta_ffn_conv', 'delta_ffn_w_down', 'delta_final_norm', 'new_m_a_mix_norm', 'new_m_a_w_in', 'new_m_a_conv', 'new_m_a_w_out', 'new_m_b_mix_norm', 'new_m_b_w_dq', 'new_m_b_q_norm', 'new_m_b_w_uq', 'new_m_b_w_o', 'new_m_kv_in_norm', 'new_m_w_dkv', 'new_m_kv_norm', 'new_m_w_ukv', 'new_m_ffn_norm', 'new_m_ffn_w_up', 'new_m_ffn_conv', 'new_m_ffn_w_down', 'new_m_final_norm', 'new_v_a_mix_norm', 'new_v_a_w_in', 'new_v_a_conv', 'new_v_a_w_out', 'new_v_b_mix_norm', 'new_v_b_w_dq', 'new_v_b_q_norm', 'new_v_b_w_uq', 'new_v_b_w_o', 'new_v_kv_in_norm', 'new_v_w_dkv', 'new_v_kv_norm', 'new_v_w_ukv', 'new_v_ffn_norm', 'new_v_ffn_w_up', 'new_v_ffn_conv', 'new_v_ffn_w_down', 'new_v_final_norm']
TWIN_LEAF_KINDS = {'loss': 'loss', 'grad_x': 'grad_x', 'grad_a_mix_norm': 'grad_w', 'grad_a_w_in': 'grad_w', 'grad_a_conv': 'grad_w', 'grad_a_w_out': 'grad_w', 'grad_b_mix_norm': 'grad_w', 'grad_b_w_dq': 'grad_w', 'grad_b_q_norm': 'grad_w', 'grad_b_w_uq': 'grad_w', 'grad_b_w_o': 'grad_w', 'grad_kv_in_norm': 'grad_w', 'grad_w_dkv': 'grad_w', 'grad_kv_norm': 'grad_w', 'grad_w_ukv': 'grad_w', 'grad_ffn_norm': 'grad_w', 'grad_ffn_w_up': 'grad_w', 'grad_ffn_conv': 'grad_w', 'grad_ffn_w_down': 'grad_w', 'grad_final_norm': 'grad_w', 'delta_a_mix_norm': 'delta_w', 'delta_a_w_in': 'delta_w', 'delta_a_conv': 'delta_w', 'delta_a_w_out': 'delta_w', 'delta_b_mix_norm': 'delta_w', 'delta_b_w_dq': 'delta_w', 'delta_b_q_norm': 'delta_w', 'delta_b_w_uq': 'delta_w', 'delta_b_w_o': 'delta_w', 'delta_kv_in_norm': 'delta_w', 'delta_w_dkv': 'delta_w', 'delta_kv_norm': 'delta_w', 'delta_w_ukv': 'delta_w', 'delta_ffn_norm': 'delta_w', 'delta_ffn_w_up': 'delta_w', 'delta_ffn_conv': 'delta_w', 'delta_ffn_w_down': 'delta_w', 'delta_final_norm': 'delta_w', 'new_m_a_mix_norm': 'new_m', 'new_m_a_w_in': 'new_m', 'new_m_a_conv': 'new_m', 'new_m_a_w_out': 'new_m', 'new_m_b_mix_norm': 'new_m', 'new_m_b_w_dq': 'new_m', 'new_m_b_q_norm': 'new_m', 'new_m_b_w_uq': 'new_m', 'new_m_b_w_o': 'new_m', 'new_m_kv_in_norm': 'new_m', 'new_m_w_dkv': 'new_m', 'new_m_kv_norm': 'new_m', 'new_m_w_ukv': 'new_m', 'new_m_ffn_norm': 'new_m', 'new_m_ffn_w_up': 'new_m', 'new_m_ffn_conv': 'new_m', 'new_m_ffn_w_down': 'new_m', 'new_m_final_norm': 'new_m', 'new_v_a_mix_norm': 'new_v', 'new_v_a_w_in': 'new_v', 'new_v_a_conv': 'new_v', 'new_v_a_w_out': 'new_v', 'new_v_b_mix_norm': 'new_v', 'new_v_b_w_dq': 'new_v', 'new_v_b_q_norm': 'new_v', 'new_v_b_w_uq': 'new_v', 'new_v_b_w_o': 'new_v', 'new_v_kv_in_norm': 'new_v', 'new_v_w_dkv': 'new_v', 'new_v_kv_norm': 'new_v', 'new_v_w_ukv': 'new_v', 'new_v_ffn_norm': 'new_v', 'new_v_ffn_w_up': 'new_v', 'new_v_ffn_conv': 'new_v', 'new_v_ffn_w_down': 'new_v', 'new_v_final_norm': 'new_v'}


def _forward(args):
    return _fwd_reference(*[args[k] for k in FWD_PARAMS])


def _output_shape():
    out = _jax.eval_shape(lambda: _forward(_fwd_setup_inputs(0)))
    return out.shape, out.dtype

N_MICROBATCH = 1
ADAM_LR = 0.001
ADAM_B1 = 0.9
ADAM_B2 = 0.999
ADAM_EPS = 1e-08
ADAM_WD = 0.01
ADAM_STEP = 10
PER_EXAMPLE_BATCH_AXIS = {'x': 0, 'loss_target': 0}
SHARED_INPUTS = []
_WEIGHT_DTYPES = {'a_mix_norm': _jnp.float32, 'a_w_in': _jnp.float32, 'a_conv': _jnp.float32, 'a_w_out': _jnp.float32, 'b_mix_norm': _jnp.float32, 'b_w_dq': _jnp.float32, 'b_q_norm': _jnp.float32, 'b_w_uq': _jnp.float32, 'b_w_o': _jnp.float32, 'kv_in_norm': _jnp.float32, 'w_dkv': _jnp.float32, 'kv_norm': _jnp.float32, 'w_ukv': _jnp.float32, 'ffn_norm': _jnp.float32, 'ffn_w_up': _jnp.float32, 'ffn_conv': _jnp.float32, 'ffn_w_down': _jnp.float32, 'final_norm': _jnp.float32}
MOMENT_SCALE = {'a_mix_norm': 2.760417e-01, 'a_w_in': 1.615910e-01, 'a_conv': 1.628153e-01, 'a_w_out': 1.610723e-01, 'b_mix_norm': 1.695877e-02, 'b_w_dq': 2.455930e-02, 'b_q_norm': 2.484448e-02, 'b_w_uq': 1.417351e-02, 'b_w_o': 2.103234e-02, 'kv_in_norm': 3.842209e-02, 'w_dkv': 6.736753e-02, 'kv_norm': 7.775696e-02, 'w_ukv': 2.541334e-02, 'ffn_norm': 1.024612e-01, 'ffn_w_up': 4.367996e-02, 'ffn_conv': 4.367459e-02, 'ffn_w_down': 7.130496e-02, 'final_norm': 3.189572e+01}


def _to_microbatches(a, axis):
    t = _jnp.moveaxis(a, axis, 0)
    t = t.reshape((N_MICROBATCH, t.shape[0] // N_MICROBATCH) + t.shape[1:])
    return _jnp.moveaxis(t, 1, axis + 1)


def setup_inputs(seed: int = 0) -> dict:
    inp = _fwd_setup_inputs(seed)
    key = _jax.random.fold_in(_jax.random.key(seed), 7919)
    shape, _ = _output_shape()
    out = dict(inp)
    out["loss_target"] = _jax.random.normal(_jax.random.fold_in(key, 0), shape, _jnp.float32)
    for i, name in enumerate(TWIN_WEIGHTS):
        w = inp[name].astype(_jnp.float32)
        if MOMENT_SCALE is None:
            s = _jnp.sqrt(_jnp.mean(_jnp.square(w)) + 1e-30)
        else:
            s = MOMENT_SCALE[name]
        km, kv = _jax.random.split(_jax.random.fold_in(key, i + 1))
        out[name] = w
        out["m_" + name] = s * _jax.random.normal(km, w.shape, _jnp.float32)
        out["v_" + name] = (s * s) * _jax.random.uniform(kv, w.shape, _jnp.float32, 0.5, 1.5)
    if N_MICROBATCH > 1:
        for name, axis in PER_EXAMPLE_BATCH_AXIS.items():
            out[name] = _to_microbatches(out[name], axis)
    return {'x': out['x'], 'a_mix_norm': out['a_mix_norm'], 'a_w_in': out['a_w_in'], 'a_conv': out['a_conv'], 'a_w_out': out['a_w_out'], 'b_mix_norm': out['b_mix_norm'], 'b_w_dq': out['b_w_dq'], 'b_q_norm': out['b_q_norm'], 'b_w_uq': out['b_w_uq'], 'b_w_o': out['b_w_o'], 'kv_in_norm': out['kv_in_norm'], 'w_dkv': out['w_dkv'], 'kv_norm': out['kv_norm'], 'w_ukv': out['w_ukv'], 'ffn_norm': out['ffn_norm'], 'ffn_w_up': out['ffn_w_up'], 'ffn_conv': out['ffn_conv'], 'ffn_w_down': out['ffn_w_down'], 'final_norm': out['final_norm'], 'loss_target': out['loss_target'], 'm_a_mix_norm': out['m_a_mix_norm'], 'm_a_w_in': out['m_a_w_in'], 'm_a_conv': out['m_a_conv'], 'm_a_w_out': out['m_a_w_out'], 'm_b_mix_norm': out['m_b_mix_norm'], 'm_b_w_dq': out['m_b_w_dq'], 'm_b_q_norm': out['m_b_q_norm'], 'm_b_w_uq': out['m_b_w_uq'], 'm_b_w_o': out['m_b_w_o'], 'm_kv_in_norm': out['m_kv_in_norm'], 'm_w_dkv': out['m_w_dkv'], 'm_kv_norm': out['m_kv_norm'], 'm_w_ukv': out['m_w_ukv'], 'm_ffn_norm': out['m_ffn_norm'], 'm_ffn_w_up': out['m_ffn_w_up'], 'm_ffn_conv': out['m_ffn_conv'], 'm_ffn_w_down': out['m_ffn_w_down'], 'm_final_norm': out['m_final_norm'], 'v_a_mix_norm': out['v_a_mix_norm'], 'v_a_w_in': out['v_a_w_in'], 'v_a_conv': out['v_a_conv'], 'v_a_w_out': out['v_a_w_out'], 'v_b_mix_norm': out['v_b_mix_norm'], 'v_b_w_dq': out['v_b_w_dq'], 'v_b_q_norm': out['v_b_q_norm'], 'v_b_w_uq': out['v_b_w_uq'], 'v_b_w_o': out['v_b_w_o'], 'v_kv_in_norm': out['v_kv_in_norm'], 'v_w_dkv': out['v_w_dkv'], 'v_kv_norm': out['v_kv_norm'], 'v_w_ukv': out['v_w_ukv'], 'v_ffn_norm': out['v_ffn_norm'], 'v_ffn_w_up': out['v_ffn_w_up'], 'v_ffn_conv': out['v_ffn_conv'], 'v_ffn_w_down': out['v_ffn_w_down'], 'v_final_norm': out['v_final_norm']}


def _loss(weights, diff, rest, loss_target):
    with _jax.named_scope("forward"):
        args = {**rest, TWIN_DIFF_INPUT: diff, **{k: w.astype(_WEIGHT_DTYPES[k]) for k, w in weights.items()}}
        y = _forward(args)
    with _jax.named_scope("loss_head"):
        err = _jnp.square(y.astype(_jnp.float32) - loss_target)
        return 0.5 * _jnp.sum(_jnp.mean(err, axis=-1)) if err.ndim else 0.5 * err


def _adamw(w, g, m, v):
    m = ADAM_B1 * m + (1.0 - ADAM_B1) * g
    v = ADAM_B2 * v + (1.0 - ADAM_B2) * _jnp.square(g)
    m_hat = m / (1.0 - ADAM_B1 ** ADAM_STEP)
    v_hat = v / (1.0 - ADAM_B2 ** ADAM_STEP)
    delta = -ADAM_LR * (m_hat / (_jnp.sqrt(v_hat) + ADAM_EPS) + ADAM_WD * w)
    return delta, m, v


def reference(x, a_mix_norm, a_w_in, a_conv, a_w_out, b_mix_norm, b_w_dq, b_q_norm, b_w_uq, b_w_o, kv_in_norm, w_dkv, kv_norm, w_ukv, ffn_norm, ffn_w_up, ffn_conv, ffn_w_down, final_norm, loss_target, m_a_mix_norm, m_a_w_in, m_a_conv, m_a_w_out, m_b_mix_norm, m_b_w_dq, m_b_q_norm, m_b_w_uq, m_b_w_o, m_kv_in_norm, m_w_dkv, m_kv_norm, m_w_ukv, m_ffn_norm, m_ffn_w_up, m_ffn_conv, m_ffn_w_down, m_final_norm, v_a_mix_norm, v_a_w_in, v_a_conv, v_a_w_out, v_b_mix_norm, v_b_w_dq, v_b_q_norm, v_b_w_uq, v_b_w_o, v_kv_in_norm, v_w_dkv, v_kv_norm, v_w_ukv, v_ffn_norm, v_ffn_w_up, v_ffn_conv, v_ffn_w_down, v_final_norm):
    given = dict(x=x, a_mix_norm=a_mix_norm, a_w_in=a_w_in, a_conv=a_conv, a_w_out=a_w_out, b_mix_norm=b_mix_norm, b_w_dq=b_w_dq, b_q_norm=b_q_norm, b_w_uq=b_w_uq, b_w_o=b_w_o, kv_in_norm=kv_in_norm, w_dkv=w_dkv, kv_norm=kv_norm, w_ukv=w_ukv, ffn_norm=ffn_norm, ffn_w_up=ffn_w_up, ffn_conv=ffn_conv, ffn_w_down=ffn_w_down, final_norm=final_norm, loss_target=loss_target, m_a_mix_norm=m_a_mix_norm, m_a_w_in=m_a_w_in, m_a_conv=m_a_conv, m_a_w_out=m_a_w_out, m_b_mix_norm=m_b_mix_norm, m_b_w_dq=m_b_w_dq, m_b_q_norm=m_b_q_norm, m_b_w_uq=m_b_w_uq, m_b_w_o=m_b_w_o, m_kv_in_norm=m_kv_in_norm, m_w_dkv=m_w_dkv, m_kv_norm=m_kv_norm, m_w_ukv=m_w_ukv, m_ffn_norm=m_ffn_norm, m_ffn_w_up=m_ffn_w_up, m_ffn_conv=m_ffn_conv, m_ffn_w_down=m_ffn_w_down, m_final_norm=m_final_norm, v_a_mix_norm=v_a_mix_norm, v_a_w_in=v_a_w_in, v_a_conv=v_a_conv, v_a_w_out=v_a_w_out, v_b_mix_norm=v_b_mix_norm, v_b_w_dq=v_b_w_dq, v_b_q_norm=v_b_q_norm, v_b_w_uq=v_b_w_uq, v_b_w_o=v_b_w_o, v_kv_in_norm=v_kv_in_norm, v_w_dkv=v_w_dkv, v_kv_norm=v_kv_norm, v_w_ukv=v_w_ukv, v_ffn_norm=v_ffn_norm, v_ffn_w_up=v_ffn_w_up, v_ffn_conv=v_ffn_conv, v_ffn_w_down=v_ffn_w_down, v_final_norm=v_final_norm)
    weights = {n: given[n] for n in TWIN_WEIGHTS}
    shared = {n: given[n] for n in SHARED_INPUTS}
    per_example = {n: given[n] for n in ['x']}
    grad_fn = _jax.value_and_grad(_loss, argnums=(0, 1))

    def one_microbatch(ex, loss_target):
        ex = dict(ex)
        diff = ex.pop(TWIN_DIFF_INPUT)
        return grad_fn(weights, diff, {**shared, **ex}, loss_target)

    if N_MICROBATCH == 1:
        loss, (grad_w, grad_x) = one_microbatch(per_example, given["loss_target"])
    else:
        def body(carry, xs):
            loss_sum, grad_sum = carry
            l_k, (gw_k, gx_k) = one_microbatch(xs[0], xs[1])
            with _jax.named_scope("update"):
                return (loss_sum + l_k, _jax.tree.map(_jnp.add, grad_sum, gw_k)), gx_k

        init = (_jnp.zeros((), _jnp.float32), _jax.tree.map(_jnp.zeros_like, weights))
        (loss, grad_w), grad_x = _jax.lax.scan(body, init, (per_example, given["loss_target"]))
    with _jax.named_scope("update"):
        delta_w, new_m, new_v = {}, {}, {}
        for n in TWIN_WEIGHTS:
            delta_w[n], new_m[n], new_v[n] = _adamw(weights[n], grad_w[n], given["m_" + n], given["v_" + n])
    return (loss, grad_x, *[grad_w[n] for n in TWIN_WEIGHTS], *[delta_w[n] for n in TWIN_WEIGHTS],
            *[new_m[n] for n in TWIN_WEIGHTS], *[new_v[n] for n in TWIN_WEIGHTS])
```

```python
import functools

import jax
import jax.numpy as jnp
from jax import lax
from jax.experimental import pallas as pl
from jax.experimental.pallas import tpu as pltpu

F32 = jnp.float32
BF16 = jnp.bfloat16

N_DEV = 8
N_HEADS = 8
NOPE = 128
ROPE = 64
ROPE_PAD = 128
QK = NOPE + ROPE_PAD
VDIM = 128
KV_RANK = 256
ROPE_THETA = 10000.0
RMS_EPS = 1e-6
ATTN_SCALE = (NOPE + ROPE) ** -0.5
N_A = 2
N_B = 2
DEPTH = 4

ADAM_LR = 0.001
ADAM_B1 = 0.9
ADAM_B2 = 0.999
ADAM_EPS = 1e-08
ADAM_WD = 0.01
ADAM_STEP = 10

V7X_VMEM_LIMIT = 56 * 1024 * 1024
BF16_SUBLANES = 16
ROW_TILE = 512
ROW_TILE_SMALL = 256
ATTN_TILE = 512
MIXER_CHUNK = 512
LANES = 128
NEG_BIG = -1e30

MESH_ID = pl.DeviceIdType.MESH
ANY = pl.BlockSpec(memory_space=pl.ANY)


def _params(n_grid):
    return pltpu.CompilerParams(dimension_semantics=("arbitrary",) * n_grid,
                                vmem_limit_bytes=V7X_VMEM_LIMIT)


def _nt(a, b):
    return lax.dot_general(a, b, (((1,), (1,)), ((), ())), preferred_element_type=F32)


def _tn(a, b):
    return lax.dot_general(a, b, (((0,), (0,)), ((), ())), preferred_element_type=F32)


def _nn(a, b):
    return jnp.dot(a, b, preferred_element_type=F32)


def _rms(h, g):
    rstd = lax.rsqrt(jnp.mean(h * h, axis=-1, keepdims=True) + RMS_EPS)
    xhat = h * rstd
    return xhat * g, xhat, rstd


def _rms_bwd(dxn, xhat, rstd, g):
    dxhat = dxn * g
    dh = rstd * (dxhat - xhat * jnp.mean(dxhat * xhat, axis=-1, keepdims=True))
    return dh, dxn * xhat


def _shift_down(x, k, halo_rows):
    r = pltpu.roll(x, k, 0)
    row = lax.broadcasted_iota(jnp.int32, x.shape, 0)
    for t in range(k):
        r = jnp.where(row == t, halo_rows[t], r)
    return r


def _shift_up(x, k, halo_rows):
    n = x.shape[0]
    r = pltpu.roll(x, n - k, 0)
    row = lax.broadcasted_iota(jnp.int32, x.shape, 0)
    for t in range(k):
        r = jnp.where(row == n - k + t, halo_rows[t], r)
    return r


def _conv_taps(w_ref):
    return w_ref[0:1, :], w_ref[1:2, :], w_ref[2:3, :]


def _rope_swap(x):
    lane = lax.broadcasted_iota(jnp.int32, x.shape, 1)
    return jnp.where(lane < ROPE // 2, pltpu.roll(x, ROPE_PAD - ROPE // 2, 1),
                     pltpu.roll(x, ROPE // 2, 1))


def _rope_tables(seq):
    inv = 1.0 / (ROPE_THETA ** (jnp.arange(0, ROPE, 2, dtype=F32) / ROPE))
    ang = jnp.arange(seq, dtype=F32)[:, None] * inv[None, :]
    cos, sin = jnp.cos(ang), jnp.sin(ang)
    zero = jnp.zeros((seq, ROPE_PAD - ROPE), F32)
    return (jnp.concatenate([cos, cos, zero], axis=1),
            jnp.concatenate([-sin, sin, zero], axis=1))


def _row_tile(rows, cap, mult=8):
    best = None
    for t in range(mult, min(rows, cap) + 1, mult):
        if rows % t == 0:
            best = t
    return rows if best is None else best


def all_gather(x, name):
    n_l = x.shape[0]

    def body(x_ref, out_ref, send_sems, recv_sems, local_sem):
        mx, my, mc = lax.axis_index("x"), lax.axis_index("y"), lax.axis_index("c")
        me, sibling = (mx, my, mc), (mx, my, 1 - mc)
        chips = [(1 - mx, my), (mx, 1 - my), (1 - mx, 1 - my)]

        def slot(px, py, pc):
            return out_ref.at[:, 4 * px + 2 * py + pc]

        def copy(k, block, to, src=None):
            return pltpu.make_async_remote_copy(
                src_ref=slot(*block) if src is None else src, dst_ref=slot(*block),
                send_sem=send_sems.at[k], recv_sem=recv_sems.at[k],
                device_id=to, device_id_type=MESH_ID)

        mine = pltpu.make_async_copy(x_ref, slot(*me), local_sem)
        mine.start()
        first = [copy(0, me, sibling, src=x_ref)]
        first += [copy(1 + j, me, (*chip, mc), src=x_ref) for j, chip in enumerate(chips)]
        for cp in first:
            cp.start()
        passed = [copy(4 + j, (*chip, mc), sibling) for j, chip in enumerate(chips)]
        for j, chip in enumerate(chips):
            copy(1 + j, (*chip, mc), me).wait_recv()
            passed[j].start()
        copy(0, sibling, me).wait_recv()
        for j, chip in enumerate(chips):
            copy(4 + j, (*chip, 1 - mc), me).wait_recv()
        for cp in first + passed:
            cp.wait_send()
        mine.wait()

    return pl.pallas_call(
        body, name=name,
        out_shape=jax.ShapeDtypeStruct((n_l, N_DEV) + x.shape[1:], x.dtype),
        in_specs=[ANY], out_specs=ANY,
        scratch_shapes=[pltpu.SemaphoreType.DMA((7,)), pltpu.SemaphoreType.DMA((7,)),
                        pltpu.SemaphoreType.DMA],
    )(x)


def reduce_scatter_exchange(g, name):
    def body(g_ref, out_ref, send_sems, recv_sems, local_sem):
        mx, my, mc = lax.axis_index("x"), lax.axis_index("y"), lax.axis_index("c")
        me = 4 * mx + 2 * my + mc
        mine = pltpu.make_async_copy(g_ref.at[me], out_ref.at[me], local_sem)
        mine.start()
        copies = []
        for k in range(1, N_DEV):
            px, py, pc = mx ^ ((k >> 2) & 1), my ^ ((k >> 1) & 1), mc ^ (k & 1)
            peer = 4 * px + 2 * py + pc
            send = pltpu.make_async_remote_copy(
                src_ref=g_ref.at[peer], dst_ref=out_ref.at[me],
                send_sem=send_sems.at[k - 1], recv_sem=recv_sems.at[k - 1],
                device_id=(px, py, pc), device_id_type=MESH_ID)
            send.start()
            arrive = pltpu.make_async_remote_copy(
                src_ref=g_ref.at[me], dst_ref=out_ref.at[peer],
                send_sem=send_sems.at[k - 1], recv_sem=recv_sems.at[k - 1],
                device_id=(px, py, pc), device_id_type=MESH_ID)
            copies.append((send, arrive))
        for send, arrive in copies:
            arrive.wait_recv()
        for send, arrive in copies:
            send.wait_send()
        mine.wait()

    return pl.pallas_call(
        body, name=name, out_shape=jax.ShapeDtypeStruct(g.shape, g.dtype),
        in_specs=[ANY], out_specs=ANY,
        scratch_shapes=[pltpu.SemaphoreType.DMA((7,)), pltpu.SemaphoreType.DMA((7,)),
                        pltpu.SemaphoreType.DMA],
    )(g)


def _adamw(g, w, m, v):
    m = ADAM_B1 * m + (1.0 - ADAM_B1) * g
    v = ADAM_B2 * v + (1.0 - ADAM_B2) * (g * g)
    m_hat = m / (1.0 - ADAM_B1 ** ADAM_STEP)
    v_hat = v / (1.0 - ADAM_B2 ** ADAM_STEP)
    delta = -ADAM_LR * (m_hat / (jnp.sqrt(v_hat) + ADAM_EPS) + ADAM_WD * w)
    return delta, m, v


def sum_adamw(parts, w, m, v, name):
    n_l, rows, cols = w.shape
    n = parts[0].shape[0]
    mult = BF16_SUBLANES if parts[0].dtype == BF16 else 8
    tr = _row_tile(rows, 128, mult)
    n_i = rows // tr

    def body(*refs):
        part_refs = refs[:n_l]
        w_ref, m_ref, v_ref, g_out, d_out, m_out, v_out = refs[n_l:]
        layer = pl.program_id(0)
        for k in range(n_l):
            @pl.when(layer == k)
            def _(k=k):
                g = part_refs[k][0].astype(F32)
                for s in range(1, n):
                    g = g + part_refs[k][s].astype(F32)
                delta, m_new, v_new = _adamw(g, w_ref[...], m_ref[...], v_ref[...])
                g_out[...] = g
                d_out[...] = delta
                m_out[...] = m_new
                v_out[...] = v_new

    part_specs = [pl.BlockSpec((n, tr, cols), functools.partial(
        lambda l, i, k: (0, jnp.where(l == k, i, 0), 0), k=k)) for k in range(n_l)]
    wspec = pl.BlockSpec((None, tr, cols), lambda l, i: (l, i, 0))
    shape = jax.ShapeDtypeStruct(w.shape, F32)
    return pl.pallas_call(
        body, name=name, grid=(n_l, n_i), out_shape=(shape,) * 4,
        in_specs=part_specs + [wspec] * 3, out_specs=(wspec,) * 4,
        compiler_params=_params(2),
    )(*parts, w, m, v)


def sum_slots(parts, name):
    n, rows, cols = parts.shape

    def body(p_ref, o_ref):
        acc = p_ref[0]
        for s in range(1, n):
            acc = acc + p_ref[s]
        o_ref[...] = acc

    return pl.pallas_call(
        body, name=name, out_shape=jax.ShapeDtypeStruct((rows, cols), F32),
        in_specs=[pl.BlockSpec(memory_space=pltpu.VMEM)],
        out_specs=pl.BlockSpec(memory_space=pltpu.VMEM),
    )(parts)


def proj_residual(a, w, res, name):
    nb, seq, kb = a.shape
    d = w.shape[-1]
    tm = min(ROW_TILE, seq)

    def body(a_ref, w_ref, r_ref, o_ref):
        acc = r_ref[...]
        for b in range(nb):
            acc = acc + _nn(a_ref[b], w_ref[b])
        o_ref[...] = acc

    return pl.pallas_call(
        body, name=name, grid=(seq // tm,), out_shape=jax.ShapeDtypeStruct((seq, d), F32),
        in_specs=[pl.BlockSpec((nb, tm, kb), lambda i: (0, i, 0)),
                  pl.BlockSpec((nb, kb, d), lambda i: (0, 0, 0)),
                  pl.BlockSpec((tm, d), lambda i: (i, 0))],
        out_specs=pl.BlockSpec((tm, d), lambda i: (i, 0)),
        compiler_params=_params(1),
    )(a, w, res)


def proj_t_rms_bwd(du, w, h, g, dres, name):
    nb, seq, wd = du.shape
    k = w.shape[1]
    tm = min(ROW_TILE_SMALL, seq)

    def body(du_ref, w_ref, h_ref, g_ref, dr_ref, dh_ref, dg_ref):
        i = pl.program_id(0)
        dxn = _nt(du_ref[0], w_ref[0])
        for b in range(1, nb):
            dxn = dxn + _nt(du_ref[b], w_ref[b])
        _, xhat, rstd = _rms(h_ref[...], g_ref[...])
        dh, dg_rows = _rms_bwd(dxn, xhat, rstd, g_ref[...])
        dh_ref[...] = dr_ref[...] + dh

        @pl.when(i == 0)
        def _():
            dg_ref[...] = jnp.zeros_like(dg_ref)
        dg_ref[...] += jnp.sum(dg_rows, axis=0, keepdims=True)

    row = pl.BlockSpec((tm, k), lambda i: (i, 0))
    vec = pl.BlockSpec((1, k), lambda i: (0, 0))
    return pl.pallas_call(
        body, name=name, grid=(seq // tm,),
        out_shape=(jax.ShapeDtypeStruct((seq, k), F32), jax.ShapeDtypeStruct((1, k), F32)),
        in_specs=[pl.BlockSpec((nb, tm, wd), lambda i: (0, i, 0)),
                  pl.BlockSpec((nb, k, wd), lambda i: (0, 0, 0)), row, vec, row],
        out_specs=(row, vec), compiler_params=_params(1),
    )(du, w, h, g, dres)


def mixer_fwd(h, g, win3, cw, name):
    seq, d = h.shape
    tm = min(ROW_TILE, seq)
    cc = min(MIXER_CHUNK, d)
    n_c, n_i = d // cc, seq // tm

    def body(h_ref, g_ref, w_ref, cw_ref, u_ref, z_ref, carry):
        i = pl.program_id(1)

        @pl.when(i == 0)
        def _():
            carry[...] = jnp.zeros_like(carry)
        xn = _rms(h_ref[...], g_ref[...])[0].astype(BF16)
        b = _nn(xn, w_ref[0])
        c = _nn(xn, w_ref[1])
        hh = _nn(xn, w_ref[2])
        u_ref[0] = b.astype(BF16)
        u_ref[1] = c.astype(BF16)
        u_ref[2] = hh.astype(BF16)
        p = c * hh
        w0, w1, w2 = _conv_taps(cw_ref)
        p1 = _shift_down(p, 1, [carry[7:8, :]])
        p2 = _shift_down(p, 2, [carry[6:7, :], carry[7:8, :]])
        q = w0 * p2 + w1 * p1 + w2 * p
        carry[...] = p[tm - 8:tm, :]
        z_ref[...] = (b * q).astype(BF16)

    return pl.pallas_call(
        body, name=name, grid=(n_c, n_i),
        out_shape=(jax.ShapeDtypeStruct((3, seq, d), BF16), jax.ShapeDtypeStruct((seq, d), BF16)),
        in_specs=[pl.BlockSpec((tm, d), lambda c, i: (i, 0)),
                  pl.BlockSpec((1, d), lambda c, i: (0, 0)),
                  pl.BlockSpec((3, d, cc), lambda c, i: (0, 0, c)),
                  pl.BlockSpec((3, cc), lambda c, i: (0, c))],
        out_specs=(pl.BlockSpec((3, tm, cc), lambda c, i: (0, i, c)),
                   pl.BlockSpec((tm, cc), lambda c, i: (i, c))),
        scratch_shapes=[pltpu.VMEM((8, cc), F32)],
        compiler_params=_params(2),
    )(h, g, win3, cw)


def mixer_bwd(dh, wout, u3, h, g, cw, name):
    seq, d = h.shape
    tm = min(ROW_TILE, seq)
    cc = min(MIXER_CHUNK, d)
    n_c, n_i = d // cc, seq // tm
    halo_blocks = tm // BF16_SUBLANES

    def body(dh_ref, wout_ref, u_ref, halo_ref, h_ref, g_ref, cw_ref,
             du_ref, dwin_ref, dwout_ref, dcw_ref, acc_in, acc_out, acc_cw, carry):
        i = pl.program_id(1)
        ir = n_i - 1 - i

        @pl.when(i == 0)
        def _():
            acc_in[...] = jnp.zeros_like(acc_in)
            acc_out[...] = jnp.zeros_like(acc_out)
            acc_cw[...] = jnp.zeros_like(acc_cw)
            carry[...] = jnp.zeros_like(carry)
        dhb = dh_ref[...].astype(BF16)
        dz = _nt(dhb, wout_ref[...])
        b = u_ref[0].astype(F32)
        c = u_ref[1].astype(F32)
        hh = u_ref[2].astype(F32)
        p = c * hh
        first = ir == 0
        lo = BF16_SUBLANES - 2
        pprev = halo_ref[1].astype(F32) * halo_ref[2].astype(F32)
        pprev = jnp.where(first, 0.0, pprev)
        w0, w1, w2 = _conv_taps(cw_ref)
        p1 = _shift_down(p, 1, [pprev[lo + 1:lo + 2, :]])
        p2 = _shift_down(p, 2, [pprev[lo:lo + 1, :], pprev[lo + 1:lo + 2, :]])
        q = w0 * p2 + w1 * p1 + w2 * p
        z = (b * q).astype(BF16)
        acc_out[...] += _tn(z, dhb)
        db = dz * q
        dq = dz * b
        dq1 = _shift_up(dq, 1, [carry[0:1, :]])
        dq2 = _shift_up(dq, 2, [carry[0:1, :], carry[1:2, :]])
        dp = w2 * dq + w1 * dq1 + w0 * dq2
        carry[...] = dq[0:8, :]
        acc_cw[0:1, :] += jnp.sum(dq * p2, axis=0, keepdims=True)
        acc_cw[1:2, :] += jnp.sum(dq * p1, axis=0, keepdims=True)
        acc_cw[2:3, :] += jnp.sum(dq * p, axis=0, keepdims=True)
        dbb = db.astype(BF16)
        dcb = (dp * hh).astype(BF16)
        dhhb = (dp * c).astype(BF16)
        du_ref[0] = dbb
        du_ref[1] = dcb
        du_ref[2] = dhhb
        xn = _rms(h_ref[...], g_ref[...])[0].astype(BF16)
        acc_in[0] += _tn(xn, dbb)
        acc_in[1] += _tn(xn, dcb)
        acc_in[2] += _tn(xn, dhhb)

        @pl.when(i == n_i - 1)
        def _():
            dwin_ref[...] = acc_in[...].astype(BF16)
            dwout_ref[...] = acc_out[...].astype(BF16)
            dcw_ref[...] = acc_cw[0:3, :]

    return pl.pallas_call(
        body, name=name, grid=(n_c, n_i),
        out_shape=(jax.ShapeDtypeStruct((3, seq, d), BF16), jax.ShapeDtypeStruct((3, d, d), BF16),
                   jax.ShapeDtypeStruct((d, d), BF16), jax.ShapeDtypeStruct((3, d), F32)),
        in_specs=[pl.BlockSpec((tm, d), lambda c, i: (n_i - 1 - i, 0)),
                  pl.BlockSpec((cc, d), lambda c, i: (c, 0)),
                  pl.BlockSpec((3, tm, cc), lambda c, i: (0, n_i - 1 - i, c)),
                  pl.BlockSpec((3, BF16_SUBLANES, cc),
                               lambda c, i: (0, jnp.maximum((n_i - 1 - i) * halo_blocks - 1, 0), c)),
                  pl.BlockSpec((tm, d), lambda c, i: (n_i - 1 - i, 0)),
                  pl.BlockSpec((1, d), lambda c, i: (0, 0)),
                  pl.BlockSpec((3, cc), lambda c, i: (0, c))],
        out_specs=(pl.BlockSpec((3, tm, cc), lambda c, i: (0, n_i - 1 - i, c)),
                   pl.BlockSpec((3, d, cc), lambda c, i: (0, 0, c)),
                   pl.BlockSpec((cc, d), lambda c, i: (c, 0)),
                   pl.BlockSpec((3, cc), lambda c, i: (0, c))),
        scratch_shapes=[pltpu.VMEM((3, d, cc), F32), pltpu.VMEM((cc, d), F32),
                        pltpu.VMEM((8, cc), F32), pltpu.VMEM((8, cc), F32)],
        compiler_params=_params(2),
    )(dh, wout, u3, u3, h, g, cw)


def _silu_parts(cg):
    sg = 1.0 / (1.0 + jnp.exp(-cg))
    return sg, cg * sg


def ffn_fwd(h, g, wup, fcw, layer, name):
    seq, d = h.shape
    f8 = wup.shape[-1]
    half = N_DEV // 2
    tm = min(ROW_TILE, seq)
    n_i = seq // tm

    def body(h_ref, g_ref, wg_ref, wu_ref, cg_ref, cu_ref, up_ref, a_ref, carry):
        i = pl.program_id(1)

        @pl.when(i == 0)
        def _():
            carry[...] = jnp.zeros_like(carry)
        xn = _rms(h_ref[...], g_ref[...])[0].astype(BF16)
        conv = []
        for s, (w_ref, t_ref) in enumerate(((wg_ref, cg_ref), (wu_ref, cu_ref))):
            u = _nn(xn, w_ref[...])
            up_ref[s] = u.astype(BF16)
            w0, w1, w2 = _conv_taps(t_ref)
            u1 = _shift_down(u, 1, [carry[s, 7:8, :]])
            u2 = _shift_down(u, 2, [carry[s, 6:7, :], carry[s, 7:8, :]])
            conv.append(w0 * u2 + w1 * u1 + w2 * u)
            carry[s] = u[tm - 8:tm, :]
        _, silu = _silu_parts(conv[0])
        a_ref[...] = (silu * conv[1]).astype(BF16)

    return pl.pallas_call(
        body, name=name, grid=(half, n_i),
        out_shape=(jax.ShapeDtypeStruct((2, half, seq, f8), BF16),
                   jax.ShapeDtypeStruct((half, seq, f8), BF16)),
        in_specs=[pl.BlockSpec((tm, d), lambda c, i: (i, 0)),
                  pl.BlockSpec((1, d), lambda c, i: (0, 0)),
                  pl.BlockSpec((None, None, d, f8), lambda c, i: (layer, c, 0, 0)),
                  pl.BlockSpec((None, None, d, f8), lambda c, i: (layer, c + half, 0, 0)),
                  pl.BlockSpec((None, None, 3, f8), lambda c, i: (layer, c, 0, 0)),
                  pl.BlockSpec((None, None, 3, f8), lambda c, i: (layer, c + half, 0, 0))],
        out_specs=(pl.BlockSpec((2, None, tm, f8), lambda c, i: (0, c, i, 0)),
                   pl.BlockSpec((None, tm, f8), lambda c, i: (c, i, 0))),
        scratch_shapes=[pltpu.VMEM((2, 8, f8), F32)],
        compiler_params=_params(2),
    )(h, g, wup, wup, fcw, fcw)


def ffn_bwd(dh, wdown, up2, h, g, fcw, layer, name):
    seq, d = h.shape
    f8 = up2.shape[-1]
    half = N_DEV // 2
    tm = min(ROW_TILE, seq)
    n_i = seq // tm
    halo_blocks = tm // BF16_SUBLANES

    def body(dh_ref, wd_ref, up_ref, halo_ref, h_ref, g_ref, cg_ref, cu_ref,
             dup_ref, dwup_ref, dwd_ref, dcw_ref, acc_up, acc_down, acc_cw, carry):
        i = pl.program_id(1)
        ir = n_i - 1 - i

        @pl.when(i == 0)
        def _():
            acc_up[...] = jnp.zeros_like(acc_up)
            acc_down[...] = jnp.zeros_like(acc_down)
            acc_cw[...] = jnp.zeros_like(acc_cw)
            carry[...] = jnp.zeros_like(carry)
        dhb = dh_ref[...].astype(BF16)
        da = _nt(dhb, wd_ref[...])
        first = ir == 0
        lo = BF16_SUBLANES - 2
        us, u1s, u2s, convs, taps = [], [], [], [], []
        for s, t_ref in enumerate((cg_ref, cu_ref)):
            u = up_ref[s].astype(F32)
            prev = jnp.where(first, 0.0, halo_ref[s].astype(F32))
            w0, w1, w2 = _conv_taps(t_ref)
            u1 = _shift_down(u, 1, [prev[lo + 1:lo + 2, :]])
            u2 = _shift_down(u, 2, [prev[lo:lo + 1, :], prev[lo + 1:lo + 2, :]])
            us.append(u)
            u1s.append(u1)
            u2s.append(u2)
            taps.append((w0, w1, w2))
            convs.append(w0 * u2 + w1 * u1 + w2 * u)
        cg, cu = convs
        sg, silu = _silu_parts(cg)
        act = (silu * cu).astype(BF16)
        acc_down[...] += _tn(act, dhb)
        dcg = da * cu * (sg * (1.0 + cg * (1.0 - sg)))
        dcu = da * silu
        xn = _rms(h_ref[...], g_ref[...])[0].astype(BF16)
        for s, dc in enumerate((dcg, dcu)):
            w0, w1, w2 = taps[s]
            d1 = _shift_up(dc, 1, [carry[s, 0:1, :]])
            d2 = _shift_up(dc, 2, [carry[s, 0:1, :], carry[s, 1:2, :]])
            du = (w2 * dc + w1 * d1 + w0 * d2).astype(BF16)
            carry[s] = dc[0:8, :]
            acc_cw[s, 0:1, :] += jnp.sum(dc * u2s[s], axis=0, keepdims=True)
            acc_cw[s, 1:2, :] += jnp.sum(dc * u1s[s], axis=0, keepdims=True)
            acc_cw[s, 2:3, :] += jnp.sum(dc * us[s], axis=0, keepdims=True)
            dup_ref[s] = du
            acc_up[s] += _tn(xn, du)

        @pl.when(i == n_i - 1)
        def _():
            dwup_ref[...] = acc_up[...].astype(BF16)
            dwd_ref[...] = acc_down[...].astype(BF16)
            dcw_ref[...] = acc_cw[:, 0:3, :]

    fb = wdown.shape[2]
    return pl.pallas_call(
        body, name=name, grid=(half, n_i),
        out_shape=(jax.ShapeDtypeStruct((2, half, seq, f8), BF16),
                   jax.ShapeDtypeStruct((2, half, d, f8), BF16),
                   jax.ShapeDtypeStruct((half, fb, d), BF16),
                   jax.ShapeDtypeStruct((2, half, 3, f8), F32)),
        in_specs=[pl.BlockSpec((tm, d), lambda c, i: (n_i - 1 - i, 0)),
                  pl.BlockSpec((None, None, fb, d), lambda c, i: (layer, c, 0, 0)),
                  pl.BlockSpec((2, None, tm, f8), lambda c, i: (0, c, n_i - 1 - i, 0)),
                  pl.BlockSpec((2, None, BF16_SUBLANES, f8),
                               lambda c, i: (0, c, jnp.maximum((n_i - 1 - i) * halo_blocks - 1, 0), 0)),
                  pl.BlockSpec((tm, d), lambda c, i: (n_i - 1 - i, 0)),
                  pl.BlockSpec((1, d), lambda c, i: (0, 0)),
                  pl.BlockSpec((None, None, 3, f8), lambda c, i: (layer, c, 0, 0)),
                  pl.BlockSpec((None, None, 3, f8), lambda c, i: (layer, c + half, 0, 0))],
        out_specs=(pl.BlockSpec((2, None, tm, f8), lambda c, i: (0, c, n_i - 1 - i, 0)),
                   pl.BlockSpec((2, None, d, f8), lambda c, i: (0, c, 0, 0)),
                   pl.BlockSpec((None, fb, d), lambda c, i: (c, 0, 0)),
                   pl.BlockSpec((2, None, 3, f8), lambda c, i: (0, c, 0, 0))),
        scratch_shapes=[pltpu.VMEM((2, d, f8), F32), pltpu.VMEM((fb, d), F32),
                        pltpu.VMEM((2, 8, f8), F32), pltpu.VMEM((2, 8, f8), F32)],
        compiler_params=_params(2),
    )(dh, wdown, up2, up2, h, g, fcw, fcw)


def _rope_fwd(x, cos, sin):
    return x * cos + _rope_swap(x) * sin


def _rope_bwd(dy, cos, sin):
    return dy * cos - _rope_swap(dy) * sin


def q_fwd(h, g, wdq, gq, wuq, cos, sin, layer, name):
    seq, d = h.shape
    rank = wdq.shape[-1]
    tm = min(ROW_TILE, seq)

    def body(h_ref, g_ref, wdq_ref, gq_ref, wuq_ref, cos_ref, sin_ref, q_ref):
        xn = _rms(h_ref[...], g_ref[...])[0].astype(BF16)
        qc = _nn(xn, wdq_ref[...])
        qn = _rms(qc, gq_ref[...])[0].astype(BF16)
        for hd in range(N_HEADS):
            qh = _nn(qn, wuq_ref[hd])
            qr = _rope_fwd(qh[:, NOPE:QK], cos_ref[...], sin_ref[...])
            q_ref[hd, :, 0:NOPE] = (qh[:, 0:NOPE] * ATTN_SCALE).astype(BF16)
            q_ref[hd, :, NOPE:QK] = (qr * ATTN_SCALE).astype(BF16)

    return pl.pallas_call(
        body, name=name, grid=(seq // tm,),
        out_shape=jax.ShapeDtypeStruct((N_HEADS, seq, QK), BF16),
        in_specs=[pl.BlockSpec((tm, d), lambda i: (i, 0)),
                  pl.BlockSpec((1, d), lambda i: (0, 0)),
                  pl.BlockSpec((None, d, rank), lambda i: (layer, 0, 0)),
                  pl.BlockSpec((1, rank), lambda i: (0, 0)),
                  pl.BlockSpec((None, N_HEADS, rank, QK), lambda i: (layer, 0, 0, 0)),
                  pl.BlockSpec((tm, ROPE_PAD), lambda i: (i, 0)),
                  pl.BlockSpec((tm, ROPE_PAD), lambda i: (i, 0))],
        out_specs=pl.BlockSpec((N_HEADS, tm, QK), lambda i: (0, i, 0)),
        compiler_params=_params(1),
    )(h, g, wdq, gq, wuq, cos, sin)


def q_bwd(dq, h, g, wdq, gq, wuq, cos, sin, layer, name):
    seq, d = h.shape
    rank = wdq.shape[-1]
    tm = min(ROW_TILE_SMALL, seq)
    n_i = seq // tm

    def body(dq_ref, h_ref, g_ref, wdq_ref, gq_ref, wuq_ref, cos_ref, sin_ref,
             dqc_ref, dwuq_ref, dwdq_ref, dgq_ref, acc_uq, acc_dq):
        i = pl.program_id(0)

        @pl.when(i == 0)
        def _():
            acc_uq[...] = jnp.zeros_like(acc_uq)
            acc_dq[...] = jnp.zeros_like(acc_dq)
            dgq_ref[...] = jnp.zeros_like(dgq_ref)
        xn = _rms(h_ref[...], g_ref[...])[0].astype(BF16)
        qc = _nn(xn, wdq_ref[...])
        qn, qhat, qrstd = _rms(qc, gq_ref[...])
        qnb = qn.astype(BF16)
        dqn = jnp.zeros((tm, rank), F32)
        for hd in range(N_HEADS):
            dnope = (dq_ref[hd, :, 0:NOPE] * ATTN_SCALE).astype(BF16)
            drope = _rope_bwd(dq_ref[hd, :, NOPE:QK] * ATTN_SCALE, cos_ref[...], sin_ref[...])
            draw = jnp.concatenate([dnope, drope.astype(BF16)], axis=1)
            dqn = dqn + _nt(draw, wuq_ref[hd])
            acc_uq[hd] += _tn(qnb, draw)
        dqc, dg_rows = _rms_bwd(dqn, qhat, qrstd, gq_ref[...])
        dgq_ref[...] += jnp.sum(dg_rows, axis=0, keepdims=True)
        dqcb = dqc.astype(BF16)
        dqc_ref[0] = dqcb
        acc_dq[...] += _tn(xn, dqcb)

        @pl.when(i == n_i - 1)
        def _():
            dwuq_ref[...] = acc_uq[...].astype(BF16)
            dwdq_ref[...] = acc_dq[...].astype(BF16)

    return pl.pallas_call(
        body, name=name, grid=(n_i,),
        out_shape=(jax.ShapeDtypeStruct((1, seq, rank), BF16),
                   jax.ShapeDtypeStruct((N_HEADS, rank, QK), BF16),
                   jax.ShapeDtypeStruct((d, rank), BF16),
                   jax.ShapeDtypeStruct((1, rank), F32)),
        in_specs=[pl.BlockSpec((N_HEADS, tm, QK), lambda i: (0, i, 0)),
                  pl.BlockSpec((tm, d), lambda i: (i, 0)),
                  pl.BlockSpec((1, d), lambda i: (0, 0)),
                  pl.BlockSpec((None, d, rank), lambda i: (layer, 0, 0)),
                  pl.BlockSpec((1, rank), lambda i: (0, 0)),
                  pl.BlockSpec((None, N_HEADS, rank, QK), lambda i: (layer, 0, 0, 0)),
                  pl.BlockSpec((tm, ROPE_PAD), lambda i: (i, 0)),
                  pl.BlockSpec((tm, ROPE_PAD), lambda i: (i, 0))],
        out_specs=(pl.BlockSpec((1, tm, rank), lambda i: (0, i, 0)),
                   pl.BlockSpec((N_HEADS, rank, QK), lambda i: (0, 0, 0)),
                   pl.BlockSpec((d, rank), lambda i: (0, 0)),
                   pl.BlockSpec((1, rank), lambda i: (0, 0))),
        scratch_shapes=[pltpu.VMEM((N_HEADS, rank, QK), F32), pltpu.VMEM((d, rank), F32)],
        compiler_params=_params(1),
    )(dq, h, g, wdq, gq, wuq, cos, sin)


def kv_fwd(h, g, wdkv, gkv, wukv, cos, sin, name):
    seq, d = h.shape
    tm = min(ROW_TILE, seq)
    wk = KV_RANK + ROPE_PAD

    def body(h_ref, g_ref, wdkv_ref, gkv_ref, wukv_ref, cos_ref, sin_ref, k_ref, v_ref, c_ref):
        xk = _rms(h_ref[...], g_ref[...])[0].astype(BF16)
        ckv = _nn(xk, wdkv_ref[...])
        c_kv = ckv[:, 0:KV_RANK]
        c_ref[...] = c_kv
        kr = _rope_fwd(ckv[:, KV_RANK:wk], cos_ref[...], sin_ref[...]).astype(BF16)
        ckn = _rms(c_kv, gkv_ref[...])[0].astype(BF16)
        for hd in range(N_HEADS):
            kvh = _nn(ckn, wukv_ref[hd])
            k_ref[hd, :, 0:NOPE] = kvh[:, 0:NOPE].astype(BF16)
            k_ref[hd, :, NOPE:QK] = kr
            v_ref[hd] = kvh[:, NOPE:NOPE + VDIM].astype(BF16)

    return pl.pallas_call(
        body, name=name, grid=(seq // tm,),
        out_shape=(jax.ShapeDtypeStruct((N_HEADS, seq, QK), BF16),
                   jax.ShapeDtypeStruct((N_HEADS, seq, VDIM), BF16),
                   jax.ShapeDtypeStruct((seq, KV_RANK), F32)),
        in_specs=[pl.BlockSpec((tm, d), lambda i: (i, 0)),
                  pl.BlockSpec((1, d), lambda i: (0, 0)),
                  pl.BlockSpec((d, wk), lambda i: (0, 0)),
                  pl.BlockSpec((1, KV_RANK), lambda i: (0, 0)),
                  pl.BlockSpec((N_HEADS, KV_RANK, NOPE + VDIM), lambda i: (0, 0, 0)),
                  pl.BlockSpec((tm, ROPE_PAD), lambda i: (i, 0)),
                  pl.BlockSpec((tm, ROPE_PAD), lambda i: (i, 0))],
        out_specs=(pl.BlockSpec((N_HEADS, tm, QK), lambda i: (0, i, 0)),
                   pl.BlockSpec((N_HEADS, tm, VDIM), lambda i: (0, i, 0)),
                   pl.BlockSpec((tm, KV_RANK), lambda i: (i, 0))),
        compiler_params=_params(1),
    )(h, g, wdkv, gkv, wukv, cos, sin)


def kv_bwd(dks, dvs, c_kv, h, g, gkv, wukv, cos, sin, name):
    seq, d = h.shape
    tm = min(ROW_TILE_SMALL, seq)
    n_i = seq // tm
    wk = KV_RANK + ROPE_PAD
    n_b = len(dks)

    def body(*refs):
        dk_refs = refs[:n_b]
        dv_refs = refs[n_b:2 * n_b]
        (c_ref, h_ref, g_ref, gkv_ref, wukv_ref, cos_ref, sin_ref,
         dckv_ref, dwukv_ref, dwdkv_ref, dgkv_ref, acc_ukv, acc_dkv) = refs[2 * n_b:]
        i = pl.program_id(0)

        @pl.when(i == 0)
        def _():
            acc_ukv[...] = jnp.zeros_like(acc_ukv)
            acc_dkv[...] = jnp.zeros_like(acc_dkv)
            dgkv_ref[...] = jnp.zeros_like(dgkv_ref)
        ckn, chat, crstd = _rms(c_ref[...], gkv_ref[...])
        cknb = ckn.astype(BF16)
        dckn = jnp.zeros((tm, KV_RANK), F32)
        dkr = jnp.zeros((tm, ROPE_PAD), F32)
        for hd in range(N_HEADS):
            dk = dk_refs[0][hd]
            dv = dv_refs[0][hd]
            for j in range(1, n_b):
                dk = dk + dk_refs[j][hd]
                dv = dv + dv_refs[j][hd]
            dkr = dkr + dk[:, NOPE:QK]
            dkvh = jnp.concatenate([dk[:, 0:NOPE].astype(BF16), dv.astype(BF16)], axis=1)
            dckn = dckn + _nt(dkvh, wukv_ref[hd])
            acc_ukv[hd] += _tn(cknb, dkvh)
        dc_kv, dg_rows = _rms_bwd(dckn, chat, crstd, gkv_ref[...])
        dgkv_ref[...] += jnp.sum(dg_rows, axis=0, keepdims=True)
        dkr_raw = _rope_bwd(dkr, cos_ref[...], sin_ref[...])
        dckv = jnp.concatenate([dc_kv.astype(BF16), dkr_raw.astype(BF16)], axis=1)
        dckv_ref[0] = dckv
        xk = _rms(h_ref[...], g_ref[...])[0].astype(BF16)
        acc_dkv[...] += _tn(xk, dckv)

        @pl.when(i == n_i - 1)
        def _():
            dwukv_ref[...] = acc_ukv[...].astype(BF16)
            dwdkv_ref[...] = acc_dkv[...].astype(BF16)

    kspec = pl.BlockSpec((N_HEADS, tm, QK), lambda i: (0, i, 0))
    vspec = pl.BlockSpec((N_HEADS, tm, VDIM), lambda i: (0, i, 0))
    return pl.pallas_call(
        body, name=name, grid=(n_i,),
        out_shape=(jax.ShapeDtypeStruct((1, seq, wk), BF16),
                   jax.ShapeDtypeStruct((N_HEADS, KV_RANK, NOPE + VDIM), BF16),
                   jax.ShapeDtypeStruct((d, wk), BF16),
                   jax.ShapeDtypeStruct((1, KV_RANK), F32)),
        in_specs=[kspec] * n_b + [vspec] * n_b + [
            pl.BlockSpec((tm, KV_RANK), lambda i: (i, 0)),
            pl.BlockSpec((tm, d), lambda i: (i, 0)),
            pl.BlockSpec((1, d), lambda i: (0, 0)),
            pl.BlockSpec((1, KV_RANK), lambda i: (0, 0)),
            pl.BlockSpec((N_HEADS, KV_RANK, NOPE + VDIM), lambda i: (0, 0, 0)),
            pl.BlockSpec((tm, ROPE_PAD), lambda i: (i, 0)),
            pl.BlockSpec((tm, ROPE_PAD), lambda i: (i, 0))],
        out_specs=(pl.BlockSpec((1, tm, wk), lambda i: (0, i, 0)),
                   pl.BlockSpec((N_HEADS, KV_RANK, NOPE + VDIM), lambda i: (0, 0, 0)),
                   pl.BlockSpec((d, wk), lambda i: (0, 0)),
                   pl.BlockSpec((1, KV_RANK), lambda i: (0, 0))),
        scratch_shapes=[pltpu.VMEM((N_HEADS, KV_RANK, NOPE + VDIM), F32), pltpu.VMEM((d, wk), F32)],
        compiler_params=_params(1),
    )(*dks, *dvs, c_kv, h, g, gkv, wukv, cos, sin)


def o_bwd(dh, o, wo, name):
    seq, d = dh.shape
    hv = o.shape[1]
    tm = min(ROW_TILE, seq)
    n_i = seq // tm

    def body(dh_ref, o_ref, wo_ref, do_ref, dwo_ref, acc):
        i = pl.program_id(0)

        @pl.when(i == 0)
        def _():
            acc[...] = jnp.zeros_like(acc)
        dhb = dh_ref[...].astype(BF16)
        do_ref[...] = _nt(dhb, wo_ref[...]).astype(BF16)
        acc[...] += _tn(o_ref[...], dhb)

        @pl.when(i == n_i - 1)
        def _():
            dwo_ref[...] = acc[...].astype(BF16)

    return pl.pallas_call(
        body, name=name, grid=(n_i,),
        out_shape=(jax.ShapeDtypeStruct((seq, hv), BF16), jax.ShapeDtypeStruct((hv, d), BF16)),
        in_specs=[pl.BlockSpec((tm, d), lambda i: (i, 0)),
                  pl.BlockSpec((tm, hv), lambda i: (i, 0)),
                  pl.BlockSpec((hv, d), lambda i: (0, 0))],
        out_specs=(pl.BlockSpec((tm, hv), lambda i: (i, 0)),
                   pl.BlockSpec((hv, d), lambda i: (0, 0))),
        scratch_shapes=[pltpu.VMEM((hv, d), F32)],
        compiler_params=_params(1),
    )(dh, o, wo)


def _causal_scores(q, k, qi, kj, tq):
    s = _nt(q, k)
    row = lax.broadcasted_iota(jnp.int32, s.shape, 0) + qi * tq
    col = lax.broadcasted_iota(jnp.int32, s.shape, 1) + kj * tq
    return jnp.where(col <= row, s, NEG_BIG)


def attn_fwd(q, k, v, name):
    _, seq, _ = q.shape
    tq = min(ATTN_TILE, seq)
    n_q = seq // tq

    def body(q_ref, k_ref, v_ref, o_ref, lse_ref, m_sc, l_sc, acc_sc):
        qi, kj = pl.program_id(1), pl.program_id(2)

        @pl.when(kj == 0)
        def _():
            m_sc[...] = jnp.full_like(m_sc, NEG_BIG)
            l_sc[...] = jnp.zeros_like(l_sc)
            acc_sc[...] = jnp.zeros_like(acc_sc)

        @pl.when(kj <= qi)
        def _():
            s = _causal_scores(q_ref[...], k_ref[...], qi, kj, tq)
            m_prev = m_sc[...]
            m_new = jnp.maximum(m_prev, jnp.max(s, axis=1, keepdims=True))
            p = jnp.exp(s - m_new[:, 0:1])
            alpha = jnp.exp(m_prev - m_new)
            l_sc[...] = alpha * l_sc[...] + jnp.sum(p, axis=1, keepdims=True)
            acc_sc[...] = alpha * acc_sc[...] + _nn(p.astype(BF16), v_ref[...])
            m_sc[...] = m_new

        @pl.when(kj == n_q - 1)
        def _():
            o_ref[...] = (acc_sc[...] / l_sc[...]).astype(BF16)
            lse_ref[...] = m_sc[...] + jnp.log(l_sc[...])

    return pl.pallas_call(
        body, name=name, grid=(N_HEADS, n_q, n_q),
        out_shape=(jax.ShapeDtypeStruct((seq, N_HEADS * VDIM), BF16),
                   jax.ShapeDtypeStruct((N_HEADS, seq, LANES), F32)),
        in_specs=[pl.BlockSpec((None, tq, QK), lambda h, i, j: (h, i, 0)),
                  pl.BlockSpec((None, tq, QK), lambda h, i, j: (h, jnp.minimum(j, i), 0)),
                  pl.BlockSpec((None, tq, VDIM), lambda h, i, j: (h, jnp.minimum(j, i), 0))],
        out_specs=(pl.BlockSpec((tq, VDIM), lambda h, i, j: (i, h)),
                   pl.BlockSpec((None, tq, LANES), lambda h, i, j: (h, i, 0))),
        scratch_shapes=[pltpu.VMEM((tq, LANES), F32), pltpu.VMEM((tq, LANES), F32),
                        pltpu.VMEM((tq, VDIM), F32)],
        compiler_params=_params(3),
    )(q, k, v)


def attn_bwd(q, k, v, o, do, lse, name):
    _, seq, _ = q.shape
    tq = min(ATTN_TILE, seq)
    n_q = seq // tq

    def body(q_ref, k_ref, v_ref, o_ref, do_ref, lse_ref, dq_ref, dk_ref, dv_ref):
        kj, qi = pl.program_id(1), pl.program_id(2)

        @pl.when((kj == 0) & (qi == 0))
        def _():
            dq_ref[...] = jnp.zeros_like(dq_ref)

        @pl.when(qi == 0)
        def _():
            dk_ref[...] = jnp.zeros_like(dk_ref)
            dv_ref[...] = jnp.zeros_like(dv_ref)

        @pl.when(qi >= kj)
        def _():
            qb = q_ref[...]
            kb = k_ref[...]
            dob = do_ref[...]
            s = _causal_scores(qb, kb, qi, kj, tq)
            p = jnp.exp(s - lse_ref[:, 0:1])
            dp = _nt(dob, v_ref[...])
            delta = jnp.sum(dob.astype(F32) * o_ref[...].astype(F32), axis=1, keepdims=True)
            ds = (p * (dp - delta)).astype(BF16)
            dv_ref[...] += _tn(p.astype(BF16), dob)
            dk_ref[...] += _tn(ds, qb)
            rows = pl.ds(pl.multiple_of(qi * tq, tq), tq)
            dq_ref[rows, :] += _nn(ds, kb)

    qrow = lambda h, j, i: (h, jnp.maximum(i, j), 0)
    return pl.pallas_call(
        body, name=name, grid=(N_HEADS, n_q, n_q),
        out_shape=(jax.ShapeDtypeStruct((N_HEADS, seq, QK), F32),
                   jax.ShapeDtypeStruct((N_HEADS, seq, QK), F32),
                   jax.ShapeDtypeStruct((N_HEADS, seq, VDIM), F32)),
        in_specs=[pl.BlockSpec((None, tq, QK), qrow),
                  pl.BlockSpec((None, tq, QK), lambda h, j, i: (h, j, 0)),
                  pl.BlockSpec((None, tq, VDIM), lambda h, j, i: (h, j, 0)),
                  pl.BlockSpec((tq, VDIM), lambda h, j, i: (jnp.maximum(i, j), h)),
                  pl.BlockSpec((tq, VDIM), lambda h, j, i: (jnp.maximum(i, j), h)),
                  pl.BlockSpec((None, tq, LANES), qrow)],
        out_specs=(pl.BlockSpec((None, seq, QK), lambda h, j, i: (h, 0, 0)),
                   pl.BlockSpec((None, tq, QK), lambda h, j, i: (h, j, 0)),
                   pl.BlockSpec((None, tq, VDIM), lambda h, j, i: (h, j, 0))),
        compiler_params=_params(3),
    )(q, k, v, o, do, lse)


def loss_head(h, g, target, name):
    seq, d = h.shape
    tm = min(ROW_TILE, seq)

    def body(h_ref, g_ref, t_ref, l_ref, dh_ref, dg_ref):
        i = pl.program_id(0)

        @pl.when(i == 0)
        def _():
            l_ref[...] = jnp.zeros_like(l_ref)
            dg_ref[...] = jnp.zeros_like(dg_ref)
        y, xhat, rstd = _rms(h_ref[...], g_ref[...])
        diff = y - t_ref[...]
        l_ref[...] += jnp.sum(jnp.sum(diff * diff, axis=1, keepdims=True), axis=0, keepdims=True)
        dh, dg_rows = _rms_bwd(diff * (1.0 / d), xhat, rstd, g_ref[...])
        dh_ref[...] = dh
        dg_ref[...] += jnp.sum(dg_rows, axis=0, keepdims=True)

    row = pl.BlockSpec((tm, d), lambda i: (i, 0))
    vec = pl.BlockSpec((1, d), lambda i: (0, 0))
    return pl.pallas_call(
        body, name=name, grid=(seq // tm,),
        out_shape=(jax.ShapeDtypeStruct((1, LANES), F32), jax.ShapeDtypeStruct((seq, d), F32),
                   jax.ShapeDtypeStruct((1, d), F32)),
        in_specs=[row, vec, row],
        out_specs=(pl.BlockSpec((1, LANES), lambda i: (0, 0)), row, vec),
        compiler_params=_params(1),
    )(h, g, target)


def _pack(parts):
    rows = []
    for p in parts:
        flat = p.reshape(-1)
        n_rows = -(-flat.shape[0] // (8 * LANES)) * 8
        flat = jnp.pad(flat, (0, n_rows * LANES - flat.shape[0]))
        rows.append(flat.reshape(n_rows, LANES))
    return jnp.concatenate(rows, axis=0)


def _unpack(packed, shapes):
    lead = packed.shape[:-2]
    out, r0 = [], 0
    for shape in shapes:
        size = 1
        for s in shape:
            size *= s
        n_rows = -(-size // (8 * LANES)) * 8
        part = packed[..., r0:r0 + n_rows, :].reshape(lead + (n_rows * LANES,))
        out.append(part[..., :size].reshape(lead + tuple(shape)))
        r0 += n_rows
    return out


def kernel(x, a_mix_norm, a_w_in, a_conv, a_w_out, b_mix_norm, b_w_dq, b_q_norm, b_w_uq, b_w_o, kv_in_norm, w_dkv, kv_norm, w_ukv, ffn_norm, ffn_w_up, ffn_conv, ffn_w_down, final_norm, loss_target, m_a_mix_norm, m_a_w_in, m_a_conv, m_a_w_out, m_b_mix_norm, m_b_w_dq, m_b_q_norm, m_b_w_uq, m_b_w_o, m_kv_in_norm, m_w_dkv, m_kv_norm, m_w_ukv, m_ffn_norm, m_ffn_w_up, m_ffn_conv, m_ffn_w_down, m_final_norm, v_a_mix_norm, v_a_w_in, v_a_conv, v_a_w_out, v_b_mix_norm, v_b_w_dq, v_b_q_norm, v_b_w_uq, v_b_w_o, v_kv_in_norm, v_w_dkv, v_kv_norm, v_w_ukv, v_ffn_norm, v_ffn_w_up, v_ffn_conv, v_ffn_w_down, v_final_norm):
    seq, d = x.shape[1], x.shape[2]
    me = 4 * lax.axis_index("x") + 2 * lax.axis_index("y") + lax.axis_index("c")
    h0 = x.reshape(seq, d)
    target = loss_target.reshape(seq, d)
    cos, sin = _rope_tables(seq)
    rank = b_w_dq.shape[-1]
    f8 = ffn_w_up.shape[-1]
    dshard = a_w_out.shape[1]

    def gather(w, name):
        w = w if w.ndim == 3 else w[None]
        return all_gather(w.astype(BF16), name)

    g_win = gather(a_w_in, "ag_a_w_in")
    g_wout = gather(a_w_out, "ag_a_w_out")
    g_wdq = gather(b_w_dq, "ag_b_w_dq")
    g_wuq = gather(b_w_uq, "ag_b_w_uq")
    g_wo = gather(b_w_o, "ag_b_w_o")
    g_wdkv = gather(w_dkv, "ag_w_dkv")
    g_wukv = gather(w_ukv, "ag_w_ukv")
    g_wup = gather(ffn_w_up, "ag_ffn_w_up")
    g_wdown = gather(ffn_w_down, "ag_ffn_w_down")

    small_shapes = [a_mix_norm.shape, a_conv.shape, ffn_conv.shape]
    g_small = all_gather(_pack([a_mix_norm, a_conv, ffn_conv])[None], "ag_small")[0]
    s_mix, s_aconv, s_fconv = _unpack(g_small, small_shapes)
    a_gain = jnp.transpose(s_mix, (1, 0, 2)).reshape(N_A, d)
    a_cw = jnp.transpose(s_aconv, (1, 2, 0, 3)).reshape(N_A, 3, d)
    f_cw = jnp.transpose(s_fconv, (1, 0, 2, 3))

    win3 = jnp.transpose(g_win, (0, 2, 1, 3)).reshape(N_A, d, 3, d)
    win3 = jnp.transpose(win3, (0, 2, 1, 3))
    wout = g_wout.reshape(N_A, d, d)
    wdq = g_wdq.reshape(N_B, d, rank)
    wuq = jnp.pad(g_wuq, ((0, 0), (0, 0), (0, 0), (0, QK - NOPE - ROPE)))
    wo = g_wo.reshape(N_B, N_HEADS * VDIM, d)
    wdkv = jnp.pad(g_wdkv.reshape(d, KV_RANK + ROPE), ((0, 0), (0, ROPE_PAD - ROPE)))
    wukv = g_wukv[0]
    fd = g_wdown.shape[2]
    wdown = g_wdown.reshape(DEPTH, N_DEV // 2, 2 * fd, d)

    saved = {}
    h = h0
    kv = None
    for layer in range(DEPTH):
        saved["hm", layer] = h
        if layer < N_A:
            u3, z = mixer_fwd(h, a_gain[layer][None], win3[layer], a_cw[layer], f"mixer_fwd{layer}")
            saved["u3", layer] = u3
            h = proj_residual(z[None], wout[layer][None], h, f"mixer_out{layer}")
        else:
            j = layer - N_A
            q = q_fwd(h, b_mix_norm[j][None], wdq, b_q_norm[j][None], wuq, cos, sin, j, f"q_fwd{j}")
            o, lse = attn_fwd(q, kv[0], kv[1], f"attn_fwd{j}")
            saved["attn", layer] = (q, o, lse)
            h = proj_residual(o[None], wo[j][None], h, f"attn_out{j}")
        saved["hf", layer] = h
        up2, act = ffn_fwd(h, ffn_norm[layer][None], g_wup, f_cw, layer, f"ffn_fwd{layer}")
        saved["up2", layer] = up2
        h = proj_residual(act, wdown[layer], h, f"ffn_out{layer}")
        if layer == N_A - 1:
            kv = kv_fwd(h, kv_in_norm[None], wdkv, kv_norm[None], wukv, cos, sin, "kv_fwd")

    sq_err, dh, d_final = loss_head(h, final_norm[None], target, "loss_head")
    loss = lax.psum(sq_err[0, 0] * (0.5 / d), ("x", "y", "c"))

    grads = {}
    d_ffn_norm = [None] * DEPTH
    d_fconv = [None] * DEPTH
    d_a_gain = [None] * N_A
    d_aconv = [None] * N_A
    d_b_gain = [None] * N_B
    d_q_gain = [None] * N_B
    dks, dvs = [], []
    for layer in reversed(range(DEPTH)):
        if layer == N_A - 1:
            hk = saved["hm", layer + 1]
            dckv, dwukv, dwdkv, d_kv_gain = kv_bwd(dks, dvs, kv[2], hk, kv_in_norm[None],
                                                   kv_norm[None], wukv, cos, sin, "kv_bwd")
            dh, d_kvin_gain = proj_t_rms_bwd(dckv, wdkv[None], hk, kv_in_norm[None], dh, "kv_in_bwd")
            grads["w_ukv", 0] = dwukv
            grads["w_dkv", 0] = dwdkv[:, :KV_RANK + ROPE].reshape(N_DEV, dshard, KV_RANK + ROPE)
        hf = saved["hf", layer]
        dup2, dwup, dwdown, dcw = ffn_bwd(dh, wdown, saved["up2", layer], hf,
                                           ffn_norm[layer][None], f_cw, layer, f"ffn_bwd{layer}")
        dh, d_ffn_norm[layer] = proj_t_rms_bwd(dup2.reshape(N_DEV, seq, f8), g_wup[layer], hf,
                                               ffn_norm[layer][None], dh, f"ffn_in_bwd{layer}")
        grads["ffn_w_up", layer] = dwup.reshape(N_DEV, d, f8)
        grads["ffn_w_down", layer] = dwdown.reshape(N_DEV, fd, d)
        d_fconv[layer] = dcw.reshape(N_DEV, 3, f8)
        hm = saved["hm", layer]
        if layer < N_A:
            du3, dwin3, dwout, dcw = mixer_bwd(dh, wout[layer], saved["u3", layer], hm,
                                               a_gain[layer][None], a_cw[layer], f"mixer_bwd{layer}")
            dh, d_a_gain[layer] = proj_t_rms_bwd(du3, win3[layer], hm, a_gain[layer][None], dh,
                                                 f"mixer_in_bwd{layer}")
            dwin = jnp.transpose(dwin3, (1, 0, 2)).reshape(d, N_DEV, 3 * d // N_DEV)
            grads["a_w_in", layer] = jnp.transpose(dwin, (1, 0, 2))
            grads["a_w_out", layer] = dwout.reshape(N_DEV, dshard, d)
            d_aconv[layer] = dcw
        else:
            j = layer - N_A
            q, o, lse = saved["attn", layer]
            do, dwo = o_bwd(dh, o, wo[j], f"attn_out_bwd{j}")
            dq, dk, dv = attn_bwd(q, kv[0], kv[1], o, do, lse, f"attn_bwd{j}")
            dks.append(dk)
            dvs.append(dv)
            dqc, dwuq, dwdq, d_q_gain[j] = q_bwd(dq, hm, b_mix_norm[j][None], wdq, b_q_norm[j][None],
                                                 wuq, cos, sin, j, f"q_bwd{j}")
            dh, d_b_gain[j] = proj_t_rms_bwd(dqc, wdq[j][None], hm, b_mix_norm[j][None], dh,
                                             f"q_in_bwd{j}")
            grads["b_w_o", j] = dwo.reshape(N_DEV, dshard, d)
            grads["b_w_uq", j] = dwuq[:, :, :NOPE + ROPE]
            grads["b_w_dq", j] = dwdq.reshape(N_DEV, dshard, rank)
    grad_x = dh.reshape(x.shape)

    def update(name, n_layers, w, m, v):
        shard = w.shape if w.ndim == 3 else (1,) + w.shape
        parts = [reduce_scatter_exchange(grads[name, l], f"rs_{name}{l}") for l in range(n_layers)]
        outs = sum_adamw(parts, w.reshape(shard), m.reshape(shard), v.reshape(shard), f"adamw_{name}")
        return [t.reshape(w.shape) for t in outs]

    res = {}
    res["a_w_in"] = update("a_w_in", N_A, a_w_in, m_a_w_in, v_a_w_in)
    res["a_w_out"] = update("a_w_out", N_A, a_w_out, m_a_w_out, v_a_w_out)
    res["b_w_dq"] = update("b_w_dq", N_B, b_w_dq, m_b_w_dq, v_b_w_dq)
    res["b_w_uq"] = update("b_w_uq", N_B, b_w_uq, m_b_w_uq, v_b_w_uq)
    res["b_w_o"] = update("b_w_o", N_B, b_w_o, m_b_w_o, v_b_w_o)
    res["w_dkv"] = update("w_dkv", 1, w_dkv, m_w_dkv, v_w_dkv)
    res["w_ukv"] = update("w_ukv", 1, w_ukv, m_w_ukv, v_w_ukv)
    res["ffn_w_up"] = update("ffn_w_up", DEPTH, ffn_w_up, m_ffn_w_up, v_ffn_w_up)
    res["ffn_w_down"] = update("ffn_w_down", DEPTH, ffn_w_down, m_ffn_w_down, v_ffn_w_down)

    full_small = [
        jnp.concatenate(d_a_gain, axis=0),
        jnp.stack(d_aconv),
        jnp.concatenate(d_b_gain, axis=0),
        jnp.concatenate(d_q_gain, axis=0),
        d_kvin_gain[0],
        d_kv_gain[0],
        jnp.concatenate(d_ffn_norm, axis=0),
        jnp.stack(d_fconv),
        d_final[0],
    ]
    full_shapes = [t.shape for t in full_small]
    g_parts = all_gather(_pack(full_small)[None], "ag_small_grads")[0]
    summed = sum_slots(g_parts, "sum_small_grads")
    (s_a_gain, s_aconv_g, s_b_gain, s_q_gain, s_kvin, s_kvn, s_ffn_gain, s_fconv_g,
     s_final) = _unpack(summed, full_shapes)
    dsl = d // N_DEV
    small = [
        ("a_mix_norm", lax.dynamic_slice_in_dim(s_a_gain, me * dsl, dsl, axis=1), a_mix_norm, m_a_mix_norm, v_a_mix_norm),
        ("a_conv", lax.dynamic_slice_in_dim(s_aconv_g, me * dsl, dsl, axis=2), a_conv, m_a_conv, v_a_conv),
        ("b_mix_norm", s_b_gain, b_mix_norm, m_b_mix_norm, v_b_mix_norm),
        ("b_q_norm", s_q_gain, b_q_norm, m_b_q_norm, v_b_q_norm),
        ("kv_in_norm", s_kvin, kv_in_norm, m_kv_in_norm, v_kv_in_norm),
        ("kv_norm", s_kvn, kv_norm, m_kv_norm, v_kv_norm),
        ("ffn_norm", s_ffn_gain, ffn_norm, m_ffn_norm, v_ffn_norm),
        ("ffn_conv", lax.dynamic_index_in_dim(s_fconv_g, me, axis=1, keepdims=False), ffn_conv, m_ffn_conv, v_ffn_conv),
        ("final_norm", s_final, final_norm, m_final_norm, v_final_norm),
    ]
    shapes = [t[2].shape for t in small]
    packed = [_pack([t[k] for t in small])[None] for k in (1, 2, 3, 4)]
    outs = sum_adamw([packed[0]], packed[1], packed[2], packed[3], "adamw_small")
    unpacked = [_unpack(t[0], shapes) for t in outs]
    for idx, t in enumerate(small):
        res[t[0]] = [unpacked[k][idx] for k in range(4)]

    order = ["a_mix_norm", "a_w_in", "a_conv", "a_w_out", "b_mix_norm", "b_w_dq", "b_q_norm",
             "b_w_uq", "b_w_o", "kv_in_norm", "w_dkv", "kv_norm", "w_ukv", "ffn_norm",
             "ffn_w_up", "ffn_conv", "ffn_w_down", "final_norm"]
    return (loss, grad_x, *[res[n][0] for n in order], *[res[n][1] for n in order],
            *[res[n][2] for n in order], *[res[n][3] for n in order])
```

```python
import functools

import jax
import jax.numpy as jnp
from jax import lax
from jax.experimental import pallas as pl
from jax.experimental.pallas import tpu as pltpu

F32 = jnp.float32
BF16 = jnp.bfloat16

N_DEV = 8
N_HEADS = 8
NOPE = 128
ROPE = 64
ROPE_PAD = 128
QK = NOPE + ROPE_PAD
VDIM = 128
KV_RANK = 256
ROPE_THETA = 10000.0
RMS_EPS = 1e-6
ATTN_SCALE = (NOPE + ROPE) ** -0.5
N_A = 2
N_B = 2
DEPTH = 4

ADAM_LR = 0.001
ADAM_B1 = 0.9
ADAM_B2 = 0.999
ADAM_EPS = 1e-08
ADAM_WD = 0.01
ADAM_STEP = 10

V7X_VMEM_LIMIT = 56 * 1024 * 1024
BF16_SUBLANES = 16
ROW_TILE = 512
ROW_TILE_SMALL = 256
ATTN_TILE = 512
MIXER_CHUNK = 512
LANES = 128
NEG_BIG = -1e30
COPIES_PER_TASK = 7

MESH_ID = pl.DeviceIdType.MESH
ANY = pl.BlockSpec(memory_space=pl.ANY)


def _nt(a, b):
    return lax.dot_general(a, b, (((1,), (1,)), ((), ())), preferred_element_type=F32)


def _tn(a, b):
    return lax.dot_general(a, b, (((0,), (0,)), ((), ())), preferred_element_type=F32)


def _nn(a, b):
    return jnp.dot(a, b, preferred_element_type=F32)


def _rms(h, g):
    rstd = lax.rsqrt(jnp.mean(h * h, axis=-1, keepdims=True) + RMS_EPS)
    xhat = h * rstd
    return xhat * g, xhat, rstd


def _rms_bwd(dxn, xhat, rstd, g):
    dxhat = dxn * g
    dh = rstd * (dxhat - xhat * jnp.mean(dxhat * xhat, axis=-1, keepdims=True))
    return dh, dxn * xhat


def _shift_down(x, k, halo_rows):
    r = pltpu.roll(x, k, 0)
    row = lax.broadcasted_iota(jnp.int32, x.shape, 0)
    for t in range(k):
        r = jnp.where(row == t, halo_rows[t], r)
    return r


def _shift_up(x, k, halo_rows):
    n = x.shape[0]
    r = pltpu.roll(x, n - k, 0)
    row = lax.broadcasted_iota(jnp.int32, x.shape, 0)
    for t in range(k):
        r = jnp.where(row == n - k + t, halo_rows[t], r)
    return r


def _conv_taps(w_ref):
    return w_ref[0:1, :], w_ref[1:2, :], w_ref[2:3, :]


def _rope_swap(x):
    lane = lax.broadcasted_iota(jnp.int32, x.shape, 1)
    return jnp.where(lane < ROPE // 2, pltpu.roll(x, ROPE_PAD - ROPE // 2, 1),
                     pltpu.roll(x, ROPE // 2, 1))


def _rope_fwd(x, cos, sin):
    return x * cos + _rope_swap(x) * sin


def _rope_bwd(dy, cos, sin):
    return dy * cos - _rope_swap(dy) * sin


def _rope_tables(seq):
    inv = 1.0 / (ROPE_THETA ** (jnp.arange(0, ROPE, 2, dtype=F32) / ROPE))
    ang = jnp.arange(seq, dtype=F32)[:, None] * inv[None, :]
    cos, sin = jnp.cos(ang), jnp.sin(ang)
    zero = jnp.zeros((seq, ROPE_PAD - ROPE), F32)
    return (jnp.concatenate([cos, cos, zero], axis=1),
            jnp.concatenate([-sin, sin, zero], axis=1))


def _row_tile(rows, cap, mult=8):
    best = None
    for t in range(mult, min(rows, cap) + 1, mult):
        if rows % t == 0:
            best = t
    return rows if best is None else best


class _AllGatherTask:
    def __init__(self, t, x_ref, out_ref, send_sems, recv_sems, local_sems):
        self.t, self.x_ref, self.out_ref = t, x_ref, out_ref
        self.send_sems, self.recv_sems, self.local_sems = send_sems, recv_sems, local_sems
        mx, my, mc = lax.axis_index("x"), lax.axis_index("y"), lax.axis_index("c")
        self.mc = mc
        self.me, self.sibling = (mx, my, mc), (mx, my, 1 - mc)
        self.chips = [(1 - mx, my), (mx, 1 - my), (1 - mx, 1 - my)]

    def _slot(self, px, py, pc):
        return self.out_ref.at[4 * px + 2 * py + pc]

    def _copy(self, k, block, to, src=None):
        s = COPIES_PER_TASK * self.t + k
        return pltpu.make_async_remote_copy(
            src_ref=self._slot(*block) if src is None else src, dst_ref=self._slot(*block),
            send_sem=self.send_sems.at[s], recv_sem=self.recv_sems.at[s],
            device_id=to, device_id_type=MESH_ID)

    def _mine(self):
        return pltpu.make_async_copy(self.x_ref, self._slot(*self.me), self.local_sems.at[self.t])

    def _first(self):
        out = [self._copy(0, self.me, self.sibling, src=self.x_ref)]
        out += [self._copy(1 + j, self.me, (*chip, self.mc), src=self.x_ref)
                for j, chip in enumerate(self.chips)]
        return out

    def _passed(self):
        return [self._copy(4 + j, (*chip, self.mc), self.sibling) for j, chip in enumerate(self.chips)]

    def start(self):
        self._mine().start()
        for cp in self._first():
            cp.start()

    def forward(self):
        passed = self._passed()
        for j, chip in enumerate(self.chips):
            self._copy(1 + j, (*chip, self.mc), self.me).wait_recv()
            passed[j].start()

    def finish(self):
        self._copy(0, self.sibling, self.me).wait_recv()
        for j, chip in enumerate(self.chips):
            self._copy(4 + j, (*chip, 1 - self.mc), self.me).wait_recv()
        for cp in self._first() + self._passed():
            cp.wait_send()
        self._mine().wait()


class _ReduceScatterTask:
    def __init__(self, t, g_ref, out_ref, send_sems, recv_sems, local_sems):
        self.t, self.g_ref, self.out_ref = t, g_ref, out_ref
        self.send_sems, self.recv_sems, self.local_sems = send_sems, recv_sems, local_sems
        mx, my, mc = lax.axis_index("x"), lax.axis_index("y"), lax.axis_index("c")
        self.me = 4 * mx + 2 * my + mc
        self.peers = []
        for k in range(1, N_DEV):
            px, py, pc = mx ^ ((k >> 2) & 1), my ^ ((k >> 1) & 1), mc ^ (k & 1)
            self.peers.append(((px, py, pc), 4 * px + 2 * py + pc))

    def _mine(self):
        return pltpu.make_async_copy(self.g_ref.at[self.me], self.out_ref.at[self.me],
                                     self.local_sems.at[self.t])

    def _copy(self, k, src_slot, dst_slot):
        s = COPIES_PER_TASK * self.t + k
        return pltpu.make_async_remote_copy(
            src_ref=self.g_ref.at[src_slot], dst_ref=self.out_ref.at[dst_slot],
            send_sem=self.send_sems.at[s], recv_sem=self.recv_sems.at[s],
            device_id=self.peers[k][0], device_id_type=MESH_ID)

    def start(self):
        self._mine().start()
        for k, (_, peer) in enumerate(self.peers):
            self._copy(k, peer, self.me).start()

    def forward(self):
        pass

    def finish(self):
        for k, (_, peer) in enumerate(self.peers):
            self._copy(k, self.me, peer).wait_recv()
        for k, (_, peer) in enumerate(self.peers):
            self._copy(k, peer, self.me).wait_send()
        self._mine().wait()


_TASKS = {"ag": _AllGatherTask, "rs": _ReduceScatterTask}


def _task_shape(kind, arr):
    shape = (N_DEV,) + arr.shape if kind == "ag" else arr.shape
    return jax.ShapeDtypeStruct(shape, arr.dtype)


def _sem_shapes(n_tasks):
    return [pltpu.SemaphoreType.DMA((COPIES_PER_TASK * n_tasks,)),
            pltpu.SemaphoreType.DMA((COPIES_PER_TASK * n_tasks,)),
            pltpu.SemaphoreType.DMA((n_tasks,))]


def _make_tasks(rider, in_refs, out_refs, sems):
    return [_TASKS[kind](t, in_refs[t], out_refs[t], *sems) for t, (kind, _) in enumerate(rider)]


def exchange(rider, name):
    n = len(rider)

    def body(*refs):
        tasks = _make_tasks(rider, refs[:n], refs[n:2 * n], refs[2 * n:])
        for task in tasks:
            task.start()
        for task in tasks:
            task.forward()
        for task in tasks:
            task.finish()

    return list(pl.pallas_call(
        body, name=name, out_shape=tuple(_task_shape(k, a) for k, a in rider),
        in_specs=[ANY] * n, out_specs=(ANY,) * n, scratch_shapes=_sem_shapes(n),
    )(*[a for _, a in rider]))


def _call(body, name, grid, in_specs, out_specs, out_shape, args, scratch=(), rider=()):
    in_specs, out_specs, out_shape = list(in_specs), tuple(out_specs), tuple(out_shape)
    n_in, n_out, n_scr, n_r = len(in_specs), len(out_specs), len(scratch), len(rider)
    if n_r:
        def kern(*refs):
            ins, r_in = refs[:n_in], refs[n_in:n_in + n_r]
            o0 = n_in + n_r
            outs, r_out = refs[o0:o0 + n_out], refs[o0 + n_out:o0 + n_out + n_r]
            s0 = o0 + n_out + n_r
            scr, sems = refs[s0:s0 + n_scr], refs[s0 + n_scr:]
            ids = [pl.program_id(a) for a in range(len(grid))]
            first = functools.reduce(jnp.logical_and, [i == 0 for i in ids])
            last = functools.reduce(jnp.logical_and, [i == n - 1 for i, n in zip(ids, grid)])

            @pl.when(first)
            def _():
                for task in _make_tasks(rider, r_in, r_out, sems):
                    task.start()
            body(*ins, *outs, *scr)

            @pl.when(last)
            def _():
                tasks = _make_tasks(rider, r_in, r_out, sems)
                for task in tasks:
                    task.forward()
                for task in tasks:
                    task.finish()
    else:
        kern = body
    res = pl.pallas_call(
        kern, name=name, grid=grid,
        in_specs=in_specs + [ANY] * n_r, out_specs=out_specs + (ANY,) * n_r,
        out_shape=out_shape + tuple(_task_shape(k, a) for k, a in rider),
        scratch_shapes=list(scratch) + (_sem_shapes(n_r) if n_r else []),
        compiler_params=pltpu.CompilerParams(dimension_semantics=("arbitrary",) * len(grid),
                                             vmem_limit_bytes=V7X_VMEM_LIMIT),
    )(*args, *[a for _, a in rider])
    return list(res[:n_out]), list(res[n_out:])


def _adamw(g, w, m, v):
    m = ADAM_B1 * m + (1.0 - ADAM_B1) * g
    v = ADAM_B2 * v + (1.0 - ADAM_B2) * (g * g)
    m_hat = m / (1.0 - ADAM_B1 ** ADAM_STEP)
    v_hat = v / (1.0 - ADAM_B2 ** ADAM_STEP)
    delta = -ADAM_LR * (m_hat / (jnp.sqrt(v_hat) + ADAM_EPS) + ADAM_WD * w)
    return delta, m, v


def sum_adamw(parts, w, m, v, name, rider=()):
    n_l, rows, cols = w.shape
    n = parts[0].shape[0]
    mult = BF16_SUBLANES if parts[0].dtype == BF16 else 8
    tr = _row_tile(rows, 128, mult)
    n_i = rows // tr

    def body(*refs):
        part_refs = refs[:n_l]
        w_ref, m_ref, v_ref, g_out, d_out, m_out, v_out = refs[n_l:]
        layer = pl.program_id(0)
        for k in range(n_l):
            @pl.when(layer == k)
            def _(k=k):
                g = part_refs[k][0].astype(F32)
                for s in range(1, n):
                    g = g + part_refs[k][s].astype(F32)
                delta, m_new, v_new = _adamw(g, w_ref[...], m_ref[...], v_ref[...])
                g_out[...] = g
                d_out[...] = delta
                m_out[...] = m_new
                v_out[...] = v_new

    part_specs = [pl.BlockSpec((n, tr, cols), functools.partial(
        lambda l, i, k: (0, jnp.where(l == k, i, 0), 0), k=k)) for k in range(n_l)]
    wspec = pl.BlockSpec((None, tr, cols), lambda l, i: (l, i, 0))
    shape = jax.ShapeDtypeStruct(w.shape, F32)
    return _call(body, name, (n_l, n_i), part_specs + [wspec] * 3, (wspec,) * 4, (shape,) * 4,
                 (*parts, w, m, v), rider=rider)


def sum_slots(parts, name):
    n, rows, cols = parts.shape

    def body(p_ref, o_ref):
        acc = p_ref[0]
        for s in range(1, n):
            acc = acc + p_ref[s]
        o_ref[...] = acc

    return pl.pallas_call(
        body, name=name, out_shape=jax.ShapeDtypeStruct((rows, cols), F32),
        in_specs=[pl.BlockSpec(memory_space=pltpu.VMEM)],
        out_specs=pl.BlockSpec(memory_space=pltpu.VMEM),
    )(parts)


def norm_fwd(h, g, name):
    seq, d = h.shape
    tm = min(ROW_TILE, seq)

    def body(h_ref, g_ref, o_ref):
        o_ref[...] = _rms(h_ref[...], g_ref[...])[0].astype(BF16)

    return _call(body, name, (seq // tm,),
                 [pl.BlockSpec((tm, d), lambda i: (i, 0)), pl.BlockSpec((1, d), lambda i: (0, 0))],
                 [pl.BlockSpec((tm, d), lambda i: (i, 0))],
                 [jax.ShapeDtypeStruct((seq, d), BF16)], (h, g))[0][0]


def proj_residual(a, w, res, name, g_next=None, rider=()):
    nb, seq, kb = a.shape
    d = w.shape[-1]
    tm = min(ROW_TILE, seq)
    with_norm = g_next is not None

    def body(a_ref, w_ref, r_ref, *rest):
        acc = r_ref[...]
        for b in range(nb):
            acc = acc + _nn(a_ref[b], w_ref[b])
        if with_norm:
            g_ref, o_ref, xn_ref = rest
            xn_ref[...] = _rms(acc, g_ref[...])[0].astype(BF16)
        else:
            (o_ref,) = rest
        o_ref[...] = acc

    row = pl.BlockSpec((tm, d), lambda i: (i, 0))
    in_specs = [pl.BlockSpec((nb, tm, kb), lambda i: (0, i, 0)),
                pl.BlockSpec((nb, kb, d), lambda i: (0, 0, 0)), row]
    args = [a, w, res]
    out_specs, out_shape = [row], [jax.ShapeDtypeStruct((seq, d), F32)]
    if with_norm:
        in_specs.append(pl.BlockSpec((1, d), lambda i: (0, 0)))
        args.append(g_next)
        out_specs.append(row)
        out_shape.append(jax.ShapeDtypeStruct((seq, d), BF16))
    outs, r_outs = _call(body, name, (seq // tm,), in_specs, out_specs, out_shape, args, rider=rider)
    return (outs[0], outs[1] if with_norm else None), r_outs


def proj_t_rms_bwd(du, w, h, g, dres, name, rider=()):
    nb, seq, wd = du.shape
    k = w.shape[1]
    tm = min(ROW_TILE_SMALL, seq)

    def body(du_ref, w_ref, h_ref, g_ref, dr_ref, dh_ref, dg_ref):
        i = pl.program_id(0)
        dxn = _nt(du_ref[0], w_ref[0])
        for b in range(1, nb):
            dxn = dxn + _nt(du_ref[b], w_ref[b])
        _, xhat, rstd = _rms(h_ref[...], g_ref[...])
        dh, dg_rows = _rms_bwd(dxn, xhat, rstd, g_ref[...])
        dh_ref[...] = dr_ref[...] + dh

        @pl.when(i == 0)
        def _():
            dg_ref[...] = jnp.zeros_like(dg_ref)
        dg_ref[...] += jnp.sum(dg_rows, axis=0, keepdims=True)

    row = pl.BlockSpec((tm, k), lambda i: (i, 0))
    vec = pl.BlockSpec((1, k), lambda i: (0, 0))
    return _call(body, name, (seq // tm,),
                 [pl.BlockSpec((nb, tm, wd), lambda i: (0, i, 0)),
                  pl.BlockSpec((nb, k, wd), lambda i: (0, 0, 0)), row, vec, row],
                 (row, vec),
                 (jax.ShapeDtypeStruct((seq, k), F32), jax.ShapeDtypeStruct((1, k), F32)),
                 (du, w, h, g, dres), rider=rider)


def mixer_fwd(xn, win3, cw, name, rider=()):
    seq, d = xn.shape
    tm = min(ROW_TILE, seq)
    cc = min(MIXER_CHUNK, d)
    n_c, n_i = d // cc, seq // tm

    def body(x_ref, w_ref, cw_ref, u_ref, z_ref, carry):
        i = pl.program_id(1)

        @pl.when(i == 0)
        def _():
            carry[...] = jnp.zeros_like(carry)
        xb = x_ref[...]
        b = _nn(xb, w_ref[0])
        c = _nn(xb, w_ref[1])
        hh = _nn(xb, w_ref[2])
        p = c * hh
        w0, w1, w2 = _conv_taps(cw_ref)
        p1 = _shift_down(p, 1, [carry[7:8, :]])
        p2 = _shift_down(p, 2, [carry[6:7, :], carry[7:8, :]])
        q = w0 * p2 + w1 * p1 + w2 * p
        carry[...] = p[tm - 8:tm, :]
        u_ref[0] = b.astype(BF16)
        u_ref[1] = c.astype(BF16)
        u_ref[2] = hh.astype(BF16)
        u_ref[3] = q.astype(BF16)
        z_ref[...] = (b * q).astype(BF16)

    return _call(body, name, (n_c, n_i),
                 [pl.BlockSpec((tm, d), lambda c, i: (i, 0)),
                  pl.BlockSpec((3, d, cc), lambda c, i: (0, 0, c)),
                  pl.BlockSpec((3, cc), lambda c, i: (0, c))],
                 (pl.BlockSpec((4, tm, cc), lambda c, i: (0, i, c)),
                  pl.BlockSpec((tm, cc), lambda c, i: (i, c))),
                 (jax.ShapeDtypeStruct((4, seq, d), BF16), jax.ShapeDtypeStruct((seq, d), BF16)),
                 (xn, win3, cw), scratch=[pltpu.VMEM((8, cc), F32)], rider=rider)


def mixer_bwd(dh, wout, u4, z, xn, cw, name, rider=()):
    seq, d = xn.shape
    tm = min(ROW_TILE, seq)
    cc = min(MIXER_CHUNK, d)
    n_c, n_i = d // cc, seq // tm

    def body(dh_ref, wout_ref, u_ref, z_ref, x_ref, cw_ref,
             du_ref, dwin_ref, dwout_ref, dcw_ref, acc_in, acc_out, acc_cw, carry):
        i = pl.program_id(1)

        @pl.when(i == 0)
        def _():
            acc_in[...] = jnp.zeros_like(acc_in)
            acc_out[...] = jnp.zeros_like(acc_out)
            acc_cw[...] = jnp.zeros_like(acc_cw)
            carry[...] = jnp.zeros_like(carry)
        dhb = dh_ref[...].astype(BF16)
        dz = _nt(dhb, wout_ref[...])
        acc_out[...] += _tn(z_ref[...], dhb)
        b = u_ref[0].astype(F32)
        c = u_ref[1].astype(F32)
        hh = u_ref[2].astype(F32)
        q = u_ref[3].astype(F32)
        p = c * hh
        db = dz * q
        dq = dz * b
        w0, w1, w2 = _conv_taps(cw_ref)
        dq1 = _shift_up(dq, 1, [carry[0:1, :]])
        dq2 = _shift_up(dq, 2, [carry[0:1, :], carry[1:2, :]])
        dp = w2 * dq + w1 * dq1 + w0 * dq2
        carry[...] = dq[0:8, :]
        acc_cw[0:1, :] += jnp.sum(dq2 * p, axis=0, keepdims=True)
        acc_cw[1:2, :] += jnp.sum(dq1 * p, axis=0, keepdims=True)
        acc_cw[2:3, :] += jnp.sum(dq * p, axis=0, keepdims=True)
        dbb = db.astype(BF16)
        dcb = (dp * hh).astype(BF16)
        dhhb = (dp * c).astype(BF16)
        du_ref[0] = dbb
        du_ref[1] = dcb
        du_ref[2] = dhhb
        xb = x_ref[...]
        acc_in[0] += _tn(xb, dbb)
        acc_in[1] += _tn(xb, dcb)
        acc_in[2] += _tn(xb, dhhb)

        @pl.when(i == n_i - 1)
        def _():
            dwin_ref[...] = acc_in[...].astype(BF16)
            dwout_ref[...] = acc_out[...].astype(BF16)
            dcw_ref[...] = acc_cw[0:3, :]

    rev = lambda c, i: (n_i - 1 - i, 0)
    return _call(body, name, (n_c, n_i),
                 [pl.BlockSpec((tm, d), rev),
                  pl.BlockSpec((cc, d), lambda c, i: (c, 0)),
                  pl.BlockSpec((4, tm, cc), lambda c, i: (0, n_i - 1 - i, c)),
                  pl.BlockSpec((tm, cc), lambda c, i: (n_i - 1 - i, c)),
                  pl.BlockSpec((tm, d), rev),
                  pl.BlockSpec((3, cc), lambda c, i: (0, c))],
                 (pl.BlockSpec((3, tm, cc), lambda c, i: (0, n_i - 1 - i, c)),
                  pl.BlockSpec((3, d, cc), lambda c, i: (0, 0, c)),
                  pl.BlockSpec((cc, d), lambda c, i: (c, 0)),
                  pl.BlockSpec((3, cc), lambda c, i: (0, c))),
                 (jax.ShapeDtypeStruct((3, seq, d), BF16), jax.ShapeDtypeStruct((3, d, d), BF16),
                  jax.ShapeDtypeStruct((d, d), BF16), jax.ShapeDtypeStruct((3, d), F32)),
                 (dh, wout, u4, z, xn, cw),
                 scratch=[pltpu.VMEM((3, d, cc), F32), pltpu.VMEM((cc, d), F32),
                          pltpu.VMEM((8, cc), F32), pltpu.VMEM((8, cc), F32)], rider=rider)


def _silu_parts(cg):
    sg = 1.0 / (1.0 + jnp.exp(-cg))
    return sg, cg * sg


def ffn_fwd(xn, wup, fcw, name, rider=()):
    seq, d = xn.shape
    f8 = wup.shape[-1]
    half = N_DEV // 2
    tm = min(ROW_TILE, seq)
    n_i = seq // tm

    def body(x_ref, wg_ref, wu_ref, cg_ref, cu_ref, up_ref, cv_ref, a_ref, carry):
        i = pl.program_id(1)

        @pl.when(i == 0)
        def _():
            carry[...] = jnp.zeros_like(carry)
        xb = x_ref[...]
        conv = []
        for s, (w_ref, t_ref) in enumerate(((wg_ref, cg_ref), (wu_ref, cu_ref))):
            u = _nn(xb, w_ref[...])
            up_ref[s] = u.astype(BF16)
            w0, w1, w2 = _conv_taps(t_ref)
            u1 = _shift_down(u, 1, [carry[s, 7:8, :]])
            u2 = _shift_down(u, 2, [carry[s, 6:7, :], carry[s, 7:8, :]])
            cv = w0 * u2 + w1 * u1 + w2 * u
            cv_ref[s] = cv.astype(BF16)
            conv.append(cv)
            carry[s] = u[tm - 8:tm, :]
        _, silu = _silu_parts(conv[0])
        a_ref[...] = (silu * conv[1]).astype(BF16)

    blk = pl.BlockSpec((2, None, tm, f8), lambda c, i: (0, c, i, 0))
    big = jax.ShapeDtypeStruct((2, half, seq, f8), BF16)
    return _call(body, name, (half, n_i),
                 [pl.BlockSpec((tm, d), lambda c, i: (i, 0)),
                  pl.BlockSpec((None, d, f8), lambda c, i: (c, 0, 0)),
                  pl.BlockSpec((None, d, f8), lambda c, i: (c + half, 0, 0)),
                  pl.BlockSpec((None, 3, f8), lambda c, i: (c, 0, 0)),
                  pl.BlockSpec((None, 3, f8), lambda c, i: (c + half, 0, 0))],
                 (blk, blk, pl.BlockSpec((None, tm, f8), lambda c, i: (c, i, 0))),
                 (big, big, jax.ShapeDtypeStruct((half, seq, f8), BF16)),
                 (xn, wup, wup, fcw, fcw), scratch=[pltpu.VMEM((2, 8, f8), F32)], rider=rider)


def ffn_bwd(dh, wdown, up2, cv2, act, xn, fcw, name, rider=()):
    seq, d = xn.shape
    f8 = up2.shape[-1]
    fb = wdown.shape[1]
    half = N_DEV // 2
    tm = min(ROW_TILE, seq)
    n_i = seq // tm

    def body(dh_ref, wd_ref, up_ref, cv_ref, a_ref, x_ref, cg_ref, cu_ref,
             dup_ref, dwup_ref, dwd_ref, dcw_ref, acc_up, acc_down, acc_cw, carry):
        i = pl.program_id(1)

        @pl.when(i == 0)
        def _():
            acc_up[...] = jnp.zeros_like(acc_up)
            acc_down[...] = jnp.zeros_like(acc_down)
            acc_cw[...] = jnp.zeros_like(acc_cw)
            carry[...] = jnp.zeros_like(carry)
        dhb = dh_ref[...].astype(BF16)
        da = _nt(dhb, wd_ref[...])
        acc_down[...] += _tn(a_ref[...], dhb)
        cg = cv_ref[0].astype(F32)
        cu = cv_ref[1].astype(F32)
        sg, silu = _silu_parts(cg)
        dcg = da * cu * (sg * (1.0 + cg * (1.0 - sg)))
        dcu = da * silu
        xb = x_ref[...]
        for s, (dc, t_ref) in enumerate(((dcg, cg_ref), (dcu, cu_ref))):
            w0, w1, w2 = _conv_taps(t_ref)
            d1 = _shift_up(dc, 1, [carry[s, 0:1, :]])
            d2 = _shift_up(dc, 2, [carry[s, 0:1, :], carry[s, 1:2, :]])
            du = (w2 * dc + w1 * d1 + w0 * d2).astype(BF16)
            carry[s] = dc[0:8, :]
            u = up_ref[s].astype(F32)
            acc_cw[s, 0:1, :] += jnp.sum(d2 * u, axis=0, keepdims=True)
            acc_cw[s, 1:2, :] += jnp.sum(d1 * u, axis=0, keepdims=True)
            acc_cw[s, 2:3, :] += jnp.sum(dc * u, axis=0, keepdims=True)
            dup_ref[s] = du
            acc_up[s] += _tn(xb, du)

        @pl.when(i == n_i - 1)
        def _():
            dwup_ref[...] = acc_up[...].astype(BF16)
            dwd_ref[...] = acc_down[...].astype(BF16)
            dcw_ref[...] = acc_cw[:, 0:3, :]

    rev = lambda c, i: (n_i - 1 - i, 0)
    blk = pl.BlockSpec((2, None, tm, f8), lambda c, i: (0, c, n_i - 1 - i, 0))
    return _call(body, name, (half, n_i),
                 [pl.BlockSpec((tm, d), rev),
                  pl.BlockSpec((None, fb, d), lambda c, i: (c, 0, 0)),
                  blk, blk,
                  pl.BlockSpec((None, tm, f8), lambda c, i: (c, n_i - 1 - i, 0)),
                  pl.BlockSpec((tm, d), rev),
                  pl.BlockSpec((None, 3, f8), lambda c, i: (c, 0, 0)),
                  pl.BlockSpec((None, 3, f8), lambda c, i: (c + half, 0, 0))],
                 (blk,
                  pl.BlockSpec((2, None, d, f8), lambda c, i: (0, c, 0, 0)),
                  pl.BlockSpec((None, fb, d), lambda c, i: (c, 0, 0)),
                  pl.BlockSpec((2, None, 3, f8), lambda c, i: (0, c, 0, 0))),
                 (jax.ShapeDtypeStruct((2, half, seq, f8), BF16),
                  jax.ShapeDtypeStruct((2, half, d, f8), BF16),
                  jax.ShapeDtypeStruct((half, fb, d), BF16),
                  jax.ShapeDtypeStruct((2, half, 3, f8), F32)),
                 (dh, wdown, up2, cv2, act, xn, fcw, fcw),
                 scratch=[pltpu.VMEM((2, d, f8), F32), pltpu.VMEM((fb, d), F32),
                          pltpu.VMEM((2, 8, f8), F32), pltpu.VMEM((2, 8, f8), F32)], rider=rider)


def q_fwd(xn, wdq, gq, wuq, cos, sin, name, rider=()):
    seq, d = xn.shape
    rank = wdq.shape[-1]
    tm = min(ROW_TILE, seq)

    def body(x_ref, wdq_ref, gq_ref, wuq_ref, cos_ref, sin_ref, q_ref):
        qc = _nn(x_ref[...], wdq_ref[...])
        qn = _rms(qc, gq_ref[...])[0].astype(BF16)
        for hd in range(N_HEADS):
            qh = _nn(qn, wuq_ref[hd])
            qr = _rope_fwd(qh[:, NOPE:QK], cos_ref[...], sin_ref[...])
            q_ref[hd, :, 0:NOPE] = (qh[:, 0:NOPE] * ATTN_SCALE).astype(BF16)
            q_ref[hd, :, NOPE:QK] = (qr * ATTN_SCALE).astype(BF16)

    rope = pl.BlockSpec((tm, ROPE_PAD), lambda i: (i, 0))
    return _call(body, name, (seq // tm,),
                 [pl.BlockSpec((tm, d), lambda i: (i, 0)),
                  pl.BlockSpec((d, rank), lambda i: (0, 0)),
                  pl.BlockSpec((1, rank), lambda i: (0, 0)),
                  pl.BlockSpec((N_HEADS, rank, QK), lambda i: (0, 0, 0)), rope, rope],
                 [pl.BlockSpec((N_HEADS, tm, QK), lambda i: (0, i, 0))],
                 [jax.ShapeDtypeStruct((N_HEADS, seq, QK), BF16)],
                 (xn, wdq, gq, wuq, cos, sin), rider=rider)


def q_bwd(dq, xn, wdq, gq, wuq, cos, sin, name, rider=()):
    seq, d = xn.shape
    rank = wdq.shape[-1]
    tm = min(ROW_TILE_SMALL, seq)
    n_i = seq // tm

    def body(dq_ref, x_ref, wdq_ref, gq_ref, wuq_ref, cos_ref, sin_ref,
             dqc_ref, dwuq_ref, dwdq_ref, dgq_ref, acc_uq, acc_dq):
        i = pl.program_id(0)

        @pl.when(i == 0)
        def _():
            acc_uq[...] = jnp.zeros_like(acc_uq)
            acc_dq[...] = jnp.zeros_like(acc_dq)
            dgq_ref[...] = jnp.zeros_like(dgq_ref)
        xb = x_ref[...]
        qc = _nn(xb, wdq_ref[...])
        qn, qhat, qrstd = _rms(qc, gq_ref[...])
        qnb = qn.astype(BF16)
        dqn = jnp.zeros((tm, rank), F32)
        for hd in range(N_HEADS):
            dnope = (dq_ref[hd, :, 0:NOPE] * ATTN_SCALE).astype(BF16)
            drope = _rope_bwd(dq_ref[hd, :, NOPE:QK] * ATTN_SCALE, cos_ref[...], sin_ref[...])
            draw = jnp.concatenate([dnope, drope.astype(BF16)], axis=1)
            dqn = dqn + _nt(draw, wuq_ref[hd])
            acc_uq[hd] += _tn(qnb, draw)
        dqc, dg_rows = _rms_bwd(dqn, qhat, qrstd, gq_ref[...])
        dgq_ref[...] += jnp.sum(dg_rows, axis=0, keepdims=True)
        dqcb = dqc.astype(BF16)
        dqc_ref[0] = dqcb
        acc_dq[...] += _tn(xb, dqcb)

        @pl.when(i == n_i - 1)
        def _():
            dwuq_ref[...] = acc_uq[...].astype(BF16)
            dwdq_ref[...] = acc_dq[...].astype(BF16)

    rope = pl.BlockSpec((tm, ROPE_PAD), lambda i: (i, 0))
    return _call(body, name, (n_i,),
                 [pl.BlockSpec((N_HEADS, tm, QK), lambda i: (0, i, 0)),
                  pl.BlockSpec((tm, d), lambda i: (i, 0)),
                  pl.BlockSpec((d, rank), lambda i: (0, 0)),
                  pl.BlockSpec((1, rank), lambda i: (0, 0)),
                  pl.BlockSpec((N_HEADS, rank, QK), lambda i: (0, 0, 0)), rope, rope],
                 (pl.BlockSpec((1, tm, rank), lambda i: (0, i, 0)),
                  pl.BlockSpec((N_HEADS, rank, QK), lambda i: (0, 0, 0)),
                  pl.BlockSpec((d, rank), lambda i: (0, 0)),
                  pl.BlockSpec((1, rank), lambda i: (0, 0))),
                 (jax.ShapeDtypeStruct((1, seq, rank), BF16),
                  jax.ShapeDtypeStruct((N_HEADS, rank, QK), BF16),
                  jax.ShapeDtypeStruct((d, rank), BF16),
                  jax.ShapeDtypeStruct((1, rank), F32)),
                 (dq, xn, wdq, gq, wuq, cos, sin),
                 scratch=[pltpu.VMEM((N_HEADS, rank, QK), F32), pltpu.VMEM((d, rank), F32)],
                 rider=rider)


def kv_fwd(h, g, wdkv, gkv, wukv, cos, sin, name, rider=()):
    seq, d = h.shape
    tm = min(ROW_TILE, seq)
    wk = KV_RANK + ROPE_PAD

    def body(h_ref, g_ref, wdkv_ref, gkv_ref, wukv_ref, cos_ref, sin_ref, k_ref, v_ref, c_ref):
        xk = _rms(h_ref[...], g_ref[...])[0].astype(BF16)
        ckv = _nn(xk, wdkv_ref[...])
        c_kv = ckv[:, 0:KV_RANK]
        c_ref[...] = c_kv
        kr = _rope_fwd(ckv[:, KV_RANK:wk], cos_ref[...], sin_ref[...]).astype(BF16)
        ckn = _rms(c_kv, gkv_ref[...])[0].astype(BF16)
        for hd in range(N_HEADS):
            kvh = _nn(ckn, wukv_ref[hd])
            k_ref[hd, :, 0:NOPE] = kvh[:, 0:NOPE].astype(BF16)
            k_ref[hd, :, NOPE:QK] = kr
            v_ref[hd] = kvh[:, NOPE:NOPE + VDIM].astype(BF16)

    rope = pl.BlockSpec((tm, ROPE_PAD), lambda i: (i, 0))
    return _call(body, name, (seq // tm,),
                 [pl.BlockSpec((tm, d), lambda i: (i, 0)),
                  pl.BlockSpec((1, d), lambda i: (0, 0)),
                  pl.BlockSpec((d, wk), lambda i: (0, 0)),
                  pl.BlockSpec((1, KV_RANK), lambda i: (0, 0)),
                  pl.BlockSpec((N_HEADS, KV_RANK, NOPE + VDIM), lambda i: (0, 0, 0)), rope, rope],
                 (pl.BlockSpec((N_HEADS, tm, QK), lambda i: (0, i, 0)),
                  pl.BlockSpec((N_HEADS, tm, VDIM), lambda i: (0, i, 0)),
                  pl.BlockSpec((tm, KV_RANK), lambda i: (i, 0))),
                 (jax.ShapeDtypeStruct((N_HEADS, seq, QK), BF16),
                  jax.ShapeDtypeStruct((N_HEADS, seq, VDIM), BF16),
                  jax.ShapeDtypeStruct((seq, KV_RANK), F32)),
                 (h, g, wdkv, gkv, wukv, cos, sin), rider=rider)


def kv_bwd(dks, dvs, c_kv, h, g, gkv, wukv, cos, sin, name, rider=()):
    seq, d = h.shape
    tm = min(ROW_TILE_SMALL, seq)
    n_i = seq // tm
    wk = KV_RANK + ROPE_PAD
    n_b = len(dks)

    def body(*refs):
        dk_refs = refs[:n_b]
        dv_refs = refs[n_b:2 * n_b]
        (c_ref, h_ref, g_ref, gkv_ref, wukv_ref, cos_ref, sin_ref,
         dckv_ref, dwukv_ref, dwdkv_ref, dgkv_ref, acc_ukv, acc_dkv) = refs[2 * n_b:]
        i = pl.program_id(0)

        @pl.when(i == 0)
        def _():
            acc_ukv[...] = jnp.zeros_like(acc_ukv)
            acc_dkv[...] = jnp.zeros_like(acc_dkv)
            dgkv_ref[...] = jnp.zeros_like(dgkv_ref)
        ckn, chat, crstd = _rms(c_ref[...], gkv_ref[...])
        cknb = ckn.astype(BF16)
        dckn = jnp.zeros((tm, KV_RANK), F32)
        dkr = jnp.zeros((tm, ROPE_PAD), F32)
        for hd in range(N_HEADS):
            dk = dk_refs[0][hd]
            dv = dv_refs[0][hd]
            for j in range(1, n_b):
                dk = dk + dk_refs[j][hd]
                dv = dv + dv_refs[j][hd]
            dkr = dkr + dk[:, NOPE:QK]
            dkvh = jnp.concatenate([dk[:, 0:NOPE].astype(BF16), dv.astype(BF16)], axis=1)
            dckn = dckn + _nt(dkvh, wukv_ref[hd])
            acc_ukv[hd] += _tn(cknb, dkvh)
        dc_kv, dg_rows = _rms_bwd(dckn, chat, crstd, gkv_ref[...])
        dgkv_ref[...] += jnp.sum(dg_rows, axis=0, keepdims=True)
        dkr_raw = _rope_bwd(dkr, cos_ref[...], sin_ref[...])
        dckv = jnp.concatenate([dc_kv.astype(BF16), dkr_raw.astype(BF16)], axis=1)
        dckv_ref[0] = dckv
        xk = _rms(h_ref[...], g_ref[...])[0].astype(BF16)
        acc_dkv[...] += _tn(xk, dckv)

        @pl.when(i == n_i - 1)
        def _():
            dwukv_ref[...] = acc_ukv[...].astype(BF16)
            dwdkv_ref[...] = acc_dkv[...].astype(BF16)

    kspec = pl.BlockSpec((N_HEADS, tm, QK), lambda i: (0, i, 0))
    vspec = pl.BlockSpec((N_HEADS, tm, VDIM), lambda i: (0, i, 0))
    rope = pl.BlockSpec((tm, ROPE_PAD), lambda i: (i, 0))
    return _call(body, name, (n_i,),
                 [kspec] * n_b + [vspec] * n_b + [
                     pl.BlockSpec((tm, KV_RANK), lambda i: (i, 0)),
                     pl.BlockSpec((tm, d), lambda i: (i, 0)),
                     pl.BlockSpec((1, d), lambda i: (0, 0)),
                     pl.BlockSpec((1, KV_RANK), lambda i: (0, 0)),
                     pl.BlockSpec((N_HEADS, KV_RANK, NOPE + VDIM), lambda i: (0, 0, 0)), rope, rope],
                 (pl.BlockSpec((1, tm, wk), lambda i: (0, i, 0)),
                  pl.BlockSpec((N_HEADS, KV_RANK, NOPE + VDIM), lambda i: (0, 0, 0)),
                  pl.BlockSpec((d, wk), lambda i: (0, 0)),
                  pl.BlockSpec((1, KV_RANK), lambda i: (0, 0))),
                 (jax.ShapeDtypeStruct((1, seq, wk), BF16),
                  jax.ShapeDtypeStruct((N_HEADS, KV_RANK, NOPE + VDIM), BF16),
                  jax.ShapeDtypeStruct((d, wk), BF16),
                  jax.ShapeDtypeStruct((1, KV_RANK), F32)),
                 (*dks, *dvs, c_kv, h, g, gkv, wukv, cos, sin),
                 scratch=[pltpu.VMEM((N_HEADS, KV_RANK, NOPE + VDIM), F32), pltpu.VMEM((d, wk), F32)],
                 rider=rider)


def o_bwd(dh, o, wo, name, rider=()):
    seq, d = dh.shape
    hv = o.shape[1]
    tm = min(ROW_TILE, seq)
    n_i = seq // tm

    def body(dh_ref, o_ref, wo_ref, do_ref, dwo_ref, acc):
        i = pl.program_id(0)

        @pl.when(i == 0)
        def _():
            acc[...] = jnp.zeros_like(acc)
        dhb = dh_ref[...].astype(BF16)
        do_ref[...] = _nt(dhb, wo_ref[...]).astype(BF16)
        acc[...] += _tn(o_ref[...], dhb)

        @pl.when(i == n_i - 1)
        def _():
            dwo_ref[...] = acc[...].astype(BF16)

    return _call(body, name, (n_i,),
                 [pl.BlockSpec((tm, d), lambda i: (i, 0)),
                  pl.BlockSpec((tm, hv), lambda i: (i, 0)),
                  pl.BlockSpec((hv, d), lambda i: (0, 0))],
                 (pl.BlockSpec((tm, hv), lambda i: (i, 0)),
                  pl.BlockSpec((hv, d), lambda i: (0, 0))),
                 (jax.ShapeDtypeStruct((seq, hv), BF16), jax.ShapeDtypeStruct((hv, d), BF16)),
                 (dh, o, wo), scratch=[pltpu.VMEM((hv, d), F32)], rider=rider)


def _mask_diagonal(s):
    row = lax.broadcasted_iota(jnp.int32, s.shape, 0)
    col = lax.broadcasted_iota(jnp.int32, s.shape, 1)
    return jnp.where(col <= row, s, NEG_BIG)


def attn_fwd(q, k, v, name, rider=()):
    _, seq, _ = q.shape
    t = min(ATTN_TILE, seq)
    n_q = seq // t

    def body(q_ref, k_ref, v_ref, o_ref, lse_ref):
        qi = pl.program_id(1)
        qb = q_ref[...]

        def block(j, carry, diagonal):
            m, l, acc = carry
            rows = pl.ds(pl.multiple_of(j * t, t), t)
            s = _nt(qb, k_ref[rows, :])
            if diagonal:
                s = _mask_diagonal(s)
            m_new = jnp.maximum(m, jnp.max(s, axis=1, keepdims=True))
            p = jnp.exp(s - m_new)
            alpha = jnp.exp(m - m_new)
            l = alpha * l + jnp.sum(p, axis=1, keepdims=True)
            acc = alpha * acc + _nn(p.astype(BF16), v_ref[rows, :])
            return m_new, l, acc

        init = (jnp.full((t, 1), NEG_BIG, F32), jnp.zeros((t, 1), F32), jnp.zeros((t, VDIM), F32))
        carry = lax.fori_loop(0, qi, lambda j, c: block(j, c, False), init)
        m, l, acc = block(qi, carry, True)
        o_ref[...] = (acc / l).astype(BF16)
        lse_ref[...] = jnp.broadcast_to(m + jnp.log(l), (t, LANES))

    return _call(body, name, (N_HEADS, n_q),
                 [pl.BlockSpec((None, t, QK), lambda h, i: (h, i, 0)),
                  pl.BlockSpec((None, seq, QK), lambda h, i: (h, 0, 0)),
                  pl.BlockSpec((None, seq, VDIM), lambda h, i: (h, 0, 0))],
                 (pl.BlockSpec((t, VDIM), lambda h, i: (i, h)),
                  pl.BlockSpec((None, t, LANES), lambda h, i: (h, i, 0))),
                 (jax.ShapeDtypeStruct((seq, N_HEADS * VDIM), BF16),
                  jax.ShapeDtypeStruct((N_HEADS, seq, LANES), F32)),
                 (q, k, v), rider=rider)


def attn_bwd(q, k, v, o, do, lse, name, rider=()):
    _, seq, _ = q.shape
    t = min(ATTN_TILE, seq)
    n_q = seq // t

    def body(q_ref, k_ref, v_ref, o_ref, do_ref, lse_ref, dq_ref, dk_ref, dv_ref):
        kj = pl.program_id(1)

        @pl.when(kj == 0)
        def _():
            dq_ref[...] = jnp.zeros_like(dq_ref)
        kb = k_ref[...]
        vb = v_ref[...]

        def block(i, carry, diagonal):
            dk, dv = carry
            rows = pl.ds(pl.multiple_of(i * t, t), t)
            qb = q_ref[rows, :]
            dob = do_ref[rows, :]
            s = _nt(qb, kb)
            if diagonal:
                s = _mask_diagonal(s)
            p = jnp.exp(s - lse_ref[rows, 0:1])
            dp = _nt(dob, vb)
            delta = jnp.sum(dob.astype(F32) * o_ref[rows, :].astype(F32), axis=1, keepdims=True)
            ds = (p * (dp - delta)).astype(BF16)
            dv = dv + _tn(p.astype(BF16), dob)
            dk = dk + _tn(ds, qb)
            dq_ref[rows, :] += _nn(ds, kb)
            return dk, dv

        carry = block(kj, (jnp.zeros((t, QK), F32), jnp.zeros((t, VDIM), F32)), True)
        dk, dv = lax.fori_loop(kj + 1, n_q, lambda i, c: block(i, c, False), carry)
        dk_ref[...] = dk
        dv_ref[...] = dv

    head_rows = pl.BlockSpec((seq, VDIM), lambda h, j: (0, h))
    return _call(body, name, (N_HEADS, n_q),
                 [pl.BlockSpec((None, seq, QK), lambda h, j: (h, 0, 0)),
                  pl.BlockSpec((None, t, QK), lambda h, j: (h, j, 0)),
                  pl.BlockSpec((None, t, VDIM), lambda h, j: (h, j, 0)),
                  head_rows, head_rows,
                  pl.BlockSpec((None, seq, LANES), lambda h, j: (h, 0, 0))],
                 (pl.BlockSpec((None, seq, QK), lambda h, j: (h, 0, 0)),
                  pl.BlockSpec((None, t, QK), lambda h, j: (h, j, 0)),
                  pl.BlockSpec((None, t, VDIM), lambda h, j: (h, j, 0))),
                 (jax.ShapeDtypeStruct((N_HEADS, seq, QK), F32),
                  jax.ShapeDtypeStruct((N_HEADS, seq, QK), F32),
                  jax.ShapeDtypeStruct((N_HEADS, seq, VDIM), F32)),
                 (q, k, v, o, do, lse), rider=rider)


def loss_head(h, g, target, name):
    seq, d = h.shape
    tm = min(ROW_TILE, seq)

    def body(h_ref, g_ref, t_ref, l_ref, dh_ref, dg_ref):
        i = pl.program_id(0)

        @pl.when(i == 0)
        def _():
            l_ref[...] = jnp.zeros_like(l_ref)
            dg_ref[...] = jnp.zeros_like(dg_ref)
        y, xhat, rstd = _rms(h_ref[...], g_ref[...])
        diff = y - t_ref[...]
        l_ref[...] += jnp.sum(jnp.sum(diff * diff, axis=1, keepdims=True), axis=0, keepdims=True)
        dh, dg_rows = _rms_bwd(diff * (1.0 / d), xhat, rstd, g_ref[...])
        dh_ref[...] = dh
        dg_ref[...] += jnp.sum(dg_rows, axis=0, keepdims=True)

    row = pl.BlockSpec((tm, d), lambda i: (i, 0))
    vec = pl.BlockSpec((1, d), lambda i: (0, 0))
    return _call(body, name, (seq // tm,), [row, vec, row],
                 (pl.BlockSpec((1, LANES), lambda i: (0, 0)), row, vec),
                 (jax.ShapeDtypeStruct((1, LANES), F32), jax.ShapeDtypeStruct((seq, d), F32),
                  jax.ShapeDtypeStruct((1, d), F32)),
                 (h, g, target))[0]


def _pack(parts):
    rows = []
    for p in parts:
        flat = p.reshape(-1)
        n_rows = -(-flat.shape[0] // (8 * LANES)) * 8
        flat = jnp.pad(flat, (0, n_rows * LANES - flat.shape[0]))
        rows.append(flat.reshape(n_rows, LANES))
    return jnp.concatenate(rows, axis=0)


def _unpack(packed, shapes):
    lead = packed.shape[:-2]
    out, r0 = [], 0
    for shape in shapes:
        size = 1
        for s in shape:
            size *= s
        n_rows = -(-size // (8 * LANES)) * 8
        part = packed[..., r0:r0 + n_rows, :].reshape(lead + (n_rows * LANES,))
        out.append(part[..., :size].reshape(lead + tuple(shape)))
        r0 += n_rows
    return out


FWD_RIDERS = {
    "mixer_fwd0": [("ffn_w_up", 0)],
    "mixer_out0": [("ffn_w_down", 0)],
    "ffn_fwd0": [("a_w_in", 1), ("a_w_out", 1), ("ffn_w_down", 1)],
    "ffn_out0": [("w_dkv", 0), ("w_ukv", 0), ("b_w_dq", 0), ("b_w_uq", 0)],
    "mixer_fwd1": [("ffn_w_up", 1)],
    "mixer_out1": [("b_w_o", 0)],
    "ffn_fwd1": [("ffn_w_up", 2)],
    "ffn_out1": [("ffn_w_down", 2)],
    "kv_fwd": [("b_w_dq", 1), ("b_w_uq", 1)],
    "q_fwd0": [("b_w_o", 1)],
    "attn_fwd0": [("ffn_w_up", 3), ("ffn_w_down", 3)],
}
BWD_RIDERS = {
    "ffn_in_bwd3": [("ffn_w_down", 3)],
    "attn_bwd1": [("ffn_w_up", 3), ("b_w_o", 1)],
    "q_in_bwd1": [("b_w_dq", 1)],
    "ffn_bwd2": [("b_w_uq", 1)],
    "ffn_in_bwd2": [("ffn_w_down", 2)],
    "attn_bwd0": [("ffn_w_up", 2), ("b_w_o", 0)],
    "q_in_bwd0": [("b_w_dq", 0)],
    "kv_bwd": [("b_w_uq", 0)],
    "kv_in_bwd": [("w_ukv", 0)],
    "ffn_bwd1": [("w_dkv", 0)],
    "ffn_in_bwd1": [("ffn_w_down", 1)],
    "mixer_in_bwd1": [("a_w_out", 1)],
    "ffn_bwd0": [("ffn_w_up", 1), ("a_w_in", 1)],
    "ffn_in_bwd0": [("ffn_w_down", 0)],
    "mixer_bwd0": [("ffn_w_up", 0)],
    "mixer_in_bwd0": [("a_w_out", 0)],
    "adamw_ffn_w_down": [("a_w_in", 0)],
}


def kernel(x, a_mix_norm, a_w_in, a_conv, a_w_out, b_mix_norm, b_w_dq, b_q_norm, b_w_uq, b_w_o, kv_in_norm, w_dkv, kv_norm, w_ukv, ffn_norm, ffn_w_up, ffn_conv, ffn_w_down, final_norm, loss_target, m_a_mix_norm, m_a_w_in, m_a_conv, m_a_w_out, m_b_mix_norm, m_b_w_dq, m_b_q_norm, m_b_w_uq, m_b_w_o, m_kv_in_norm, m_w_dkv, m_kv_norm, m_w_ukv, m_ffn_norm, m_ffn_w_up, m_ffn_conv, m_ffn_w_down, m_final_norm, v_a_mix_norm, v_a_w_in, v_a_conv, v_a_w_out, v_b_mix_norm, v_b_w_dq, v_b_q_norm, v_b_w_uq, v_b_w_o, v_kv_in_norm, v_w_dkv, v_kv_norm, v_w_ukv, v_ffn_norm, v_ffn_w_up, v_ffn_conv, v_ffn_w_down, v_final_norm):
    seq, d = x.shape[1], x.shape[2]
    me = 4 * lax.axis_index("x") + 2 * lax.axis_index("y") + lax.axis_index("c")
    h0 = x.reshape(seq, d)
    target = loss_target.reshape(seq, d)
    cos, sin = _rope_tables(seq)
    rank = b_w_dq.shape[-1]
    f8 = ffn_w_up.shape[-1]
    fd = ffn_w_down.shape[1]
    dshard = a_w_out.shape[1]
    hv = N_HEADS * VDIM

    shards = {"a_w_in": a_w_in, "a_w_out": a_w_out, "b_w_dq": b_w_dq, "b_w_uq": b_w_uq,
              "b_w_o": b_w_o, "w_dkv": w_dkv[None], "w_ukv": w_ukv[None],
              "ffn_w_up": ffn_w_up, "ffn_w_down": ffn_w_down}

    def relayout(name, g):
        if name == "a_w_in":
            w = jnp.transpose(g, (1, 0, 2)).reshape(d, 3, d)
            return jnp.transpose(w, (1, 0, 2))
        if name == "a_w_out":
            return g.reshape(d, d)
        if name == "b_w_dq":
            return g.reshape(d, rank)
        if name == "b_w_uq":
            return jnp.pad(g, ((0, 0), (0, 0), (0, QK - NOPE - ROPE)))
        if name == "b_w_o":
            return g.reshape(hv, d)
        if name == "w_dkv":
            return jnp.pad(g.reshape(d, KV_RANK + ROPE), ((0, 0), (0, ROPE_PAD - ROPE)))
        if name == "ffn_w_down":
            return g.reshape(N_DEV // 2, 2 * fd, d)
        return g

    weights = {}

    def ag_rider(host):
        return [("ag", shards[n][l].astype(BF16)) for n, l in FWD_RIDERS.get(host, [])]

    def ag_done(host, outs):
        for (n, l), g in zip(FWD_RIDERS.get(host, []), outs):
            weights[n, l] = relayout(n, g)

    small_shapes = [a_mix_norm.shape, a_conv.shape, ffn_conv.shape]
    first = exchange([("ag", a_w_in[0].astype(BF16)), ("ag", a_w_out[0].astype(BF16)),
                      ("ag", _pack([a_mix_norm, a_conv, ffn_conv]))], "ag_first")
    weights["a_w_in", 0] = relayout("a_w_in", first[0])
    weights["a_w_out", 0] = relayout("a_w_out", first[1])
    s_mix, s_aconv, s_fconv = _unpack(first[2], small_shapes)
    a_gain = jnp.transpose(s_mix, (1, 0, 2)).reshape(N_A, d)
    a_cw = jnp.transpose(s_aconv, (1, 2, 0, 3)).reshape(N_A, 3, d)
    f_cw = jnp.transpose(s_fconv, (1, 0, 2, 3))

    def mixer_gain(layer):
        if layer >= DEPTH:
            return None
        return a_gain[layer][None] if layer < N_A else b_mix_norm[layer - N_A][None]

    saved = {}
    h = h0
    xn = norm_fwd(h, mixer_gain(0), "norm_first")
    kv = None
    for layer in range(DEPTH):
        saved["hm", layer], saved["xm", layer] = h, xn
        if layer < N_A:
            name = f"mixer_fwd{layer}"
            (u4, z), r = mixer_fwd(xn, weights["a_w_in", layer], a_cw[layer], name, rider=ag_rider(name))
            ag_done(name, r)
            saved["mix", layer] = (u4, z)
            name = f"mixer_out{layer}"
            (h, xn), r = proj_residual(z[None], weights["a_w_out", layer][None], h, name,
                                       g_next=ffn_norm[layer][None], rider=ag_rider(name))
            ag_done(name, r)
        else:
            j = layer - N_A
            name = f"q_fwd{j}"
            (q,), r = q_fwd(xn, weights["b_w_dq", j], b_q_norm[j][None], weights["b_w_uq", j],
                            cos, sin, name, rider=ag_rider(name))
            ag_done(name, r)
            name = f"attn_fwd{j}"
            (o, lse), r = attn_fwd(q, kv[0], kv[1], name, rider=ag_rider(name))
            ag_done(name, r)
            saved["attn", layer] = (q, o, lse)
            name = f"attn_out{j}"
            (h, xn), r = proj_residual(o[None], weights["b_w_o", j][None], h, name,
                                       g_next=ffn_norm[layer][None], rider=ag_rider(name))
            ag_done(name, r)
        saved["hf", layer], saved["xf", layer] = h, xn
        name = f"ffn_fwd{layer}"
        (up2, cv2, act), r = ffn_fwd(xn, weights["ffn_w_up", layer], f_cw[layer], name, rider=ag_rider(name))
        ag_done(name, r)
        saved["ffn", layer] = (up2, cv2, act)
        name = f"ffn_out{layer}"
        (h, xn), r = proj_residual(act, weights["ffn_w_down", layer], h, name,
                                   g_next=mixer_gain(layer + 1), rider=ag_rider(name))
        ag_done(name, r)
        if layer == N_A - 1:
            (k_all, v_all, c_kv), r = kv_fwd(h, kv_in_norm[None], weights["w_dkv", 0], kv_norm[None],
                                             weights["w_ukv", 0], cos, sin, "kv_fwd",
                                             rider=ag_rider("kv_fwd"))
            ag_done("kv_fwd", r)
            kv = (k_all, v_all, c_kv)

    sq_err, dh, d_final = loss_head(h, final_norm[None], target, "loss_head")
    loss = lax.psum(sq_err[0, 0] * (0.5 / d), ("x", "y", "c"))

    grads = {}
    parts = {}

    def rs_rider(host):
        return [("rs", grads[key]) for key in BWD_RIDERS.get(host, [])]

    def rs_done(host, outs):
        for key, p in zip(BWD_RIDERS.get(host, []), outs):
            parts[key] = p

    d_ffn_norm = [None] * DEPTH
    d_fconv = [None] * DEPTH
    d_a_gain = [None] * N_A
    d_aconv = [None] * N_A
    d_b_gain = [None] * N_B
    d_q_gain = [None] * N_B
    dks, dvs = [], []
    for layer in reversed(range(DEPTH)):
        if layer == N_A - 1:
            hk = saved["hm", layer + 1]
            (dckv, dwukv, dwdkv, d_kv_gain), r = kv_bwd(
                dks, dvs, kv[2], hk, kv_in_norm[None], kv_norm[None], weights["w_ukv", 0],
                cos, sin, "kv_bwd", rider=rs_rider("kv_bwd"))
            rs_done("kv_bwd", r)
            grads["w_ukv", 0] = dwukv
            grads["w_dkv", 0] = dwdkv[:, :KV_RANK + ROPE].reshape(N_DEV, dshard, KV_RANK + ROPE)
            (dh, d_kvin_gain), r = proj_t_rms_bwd(dckv, weights["w_dkv", 0][None], hk, kv_in_norm[None],
                                                  dh, "kv_in_bwd", rider=rs_rider("kv_in_bwd"))
            rs_done("kv_in_bwd", r)
        up2, cv2, act = saved["ffn", layer]
        name = f"ffn_bwd{layer}"
        (dup2, dwup, dwdown, dcw), r = ffn_bwd(dh, weights["ffn_w_down", layer], up2, cv2, act,
                                               saved["xf", layer], f_cw[layer], name, rider=rs_rider(name))
        rs_done(name, r)
        grads["ffn_w_up", layer] = dwup.reshape(N_DEV, d, f8)
        grads["ffn_w_down", layer] = dwdown.reshape(N_DEV, fd, d)
        d_fconv[layer] = dcw.reshape(N_DEV, 3, f8)
        name = f"ffn_in_bwd{layer}"
        (dh, d_ffn_norm[layer]), r = proj_t_rms_bwd(dup2.reshape(N_DEV, seq, f8), weights["ffn_w_up", layer],
                                                    saved["hf", layer], ffn_norm[layer][None], dh, name,
                                                    rider=rs_rider(name))
        rs_done(name, r)
        hm, xm = saved["hm", layer], saved["xm", layer]
        if layer < N_A:
            u4, z = saved["mix", layer]
            name = f"mixer_bwd{layer}"
            (du3, dwin3, dwout, dcw), r = mixer_bwd(dh, weights["a_w_out", layer], u4, z, xm, a_cw[layer],
                                                    name, rider=rs_rider(name))
            rs_done(name, r)
            dwin = jnp.transpose(dwin3, (1, 0, 2)).reshape(d, N_DEV, 3 * d // N_DEV)
            grads["a_w_in", layer] = jnp.transpose(dwin, (1, 0, 2))
            grads["a_w_out", layer] = dwout.reshape(N_DEV, dshard, d)
            d_aconv[layer] = dcw
            name = f"mixer_in_bwd{layer}"
            (dh, d_a_gain[layer]), r = proj_t_rms_bwd(du3, weights["a_w_in", layer], hm, a_gain[layer][None],
                                                      dh, name, rider=rs_rider(name))
            rs_done(name, r)
        else:
            j = layer - N_A
            q, o, lse = saved["attn", layer]
            name = f"attn_out_bwd{j}"
            (do, dwo), r = o_bwd(dh, o, weights["b_w_o", j], name, rider=rs_rider(name))
            rs_done(name, r)
            grads["b_w_o", j] = dwo.reshape(N_DEV, dshard, d)
            name = f"attn_bwd{j}"
            (dq, dk, dv), r = attn_bwd(q, kv[0], kv[1], o, do, lse, name, rider=rs_rider(name))
            rs_done(name, r)
            dks.append(dk)
            dvs.append(dv)
            name = f"q_bwd{j}"
            (dqc, dwuq, dwdq, d_q_gain[j]), r = q_bwd(dq, xm, weights["b_w_dq", j], b_q_norm[j][None],
                                                      weights["b_w_uq", j], cos, sin, name, rider=rs_rider(name))
            rs_done(name, r)
            grads["b_w_uq", j] = dwuq[:, :, :NOPE + ROPE]
            grads["b_w_dq", j] = dwdq.reshape(N_DEV, dshard, rank)
            name = f"q_in_bwd{j}"
            (dh, d_b_gain[j]), r = proj_t_rms_bwd(dqc, weights["b_w_dq", j][None], hm, b_mix_norm[j][None],
                                                  dh, name, rider=rs_rider(name))
            rs_done(name, r)
    grad_x = dh.reshape(x.shape)

    full_small = [
        jnp.concatenate(d_a_gain, axis=0),
        jnp.stack(d_aconv),
        jnp.concatenate(d_b_gain, axis=0),
        jnp.concatenate(d_q_gain, axis=0),
        d_kvin_gain[0],
        d_kv_gain[0],
        jnp.concatenate(d_ffn_norm, axis=0),
        jnp.stack(d_fconv),
        d_final[0],
    ]
    full_shapes = [t.shape for t in full_small]
    small_pack = _pack(full_small)

    res = {}

    def update(name, n_layers, w, m, v, extra=()):
        shard = w.shape if w.ndim == 3 else (1,) + w.shape
        host = f"adamw_{name}"
        outs, r = sum_adamw([parts[name, l] for l in range(n_layers)], w.reshape(shard),
                            m.reshape(shard), v.reshape(shard), host,
                            rider=rs_rider(host) + list(extra))
        rs_done(host, r)
        res[name] = [t.reshape(w.shape) for t in outs]
        return r[len(BWD_RIDERS.get(host, [])):]

    update("ffn_w_down", DEPTH, ffn_w_down, m_ffn_w_down, v_ffn_w_down)
    (g_parts,) = update("ffn_w_up", DEPTH, ffn_w_up, m_ffn_w_up, v_ffn_w_up, extra=[("ag", small_pack)])
    update("a_w_out", N_A, a_w_out, m_a_w_out, v_a_w_out)
    update("b_w_dq", N_B, b_w_dq, m_b_w_dq, v_b_w_dq)
    update("b_w_uq", N_B, b_w_uq, m_b_w_uq, v_b_w_uq)
    update("b_w_o", N_B, b_w_o, m_b_w_o, v_b_w_o)
    update("w_dkv", 1, w_dkv, m_w_dkv, v_w_dkv)
    update("w_ukv", 1, w_ukv, m_w_ukv, v_w_ukv)
    update("a_w_in", N_A, a_w_in, m_a_w_in, v_a_w_in)

    summed = sum_slots(g_parts, "sum_small_grads")
    (s_a_gain, s_aconv_g, s_b_gain, s_q_gain, s_kvin, s_kvn, s_ffn_gain, s_fconv_g,
     s_final) = _unpack(summed, full_shapes)
    dsl = d // N_DEV
    small = [
        ("a_mix_norm", lax.dynamic_slice_in_dim(s_a_gain, me * dsl, dsl, axis=1), a_mix_norm, m_a_mix_norm, v_a_mix_norm),
        ("a_conv", lax.dynamic_slice_in_dim(s_aconv_g, me * dsl, dsl, axis=2), a_conv, m_a_conv, v_a_conv),
        ("b_mix_norm", s_b_gain, b_mix_norm, m_b_mix_norm, v_b_mix_norm),
        ("b_q_norm", s_q_gain, b_q_norm, m_b_q_norm, v_b_q_norm),
        ("kv_in_norm", s_kvin, kv_in_norm, m_kv_in_norm, v_kv_in_norm),
        ("kv_norm", s_kvn, kv_norm, m_kv_norm, v_kv_norm),
        ("ffn_norm", s_ffn_gain, ffn_norm, m_ffn_norm, v_ffn_norm),
        ("ffn_conv", lax.dynamic_index_in_dim(s_fconv_g, me, axis=1, keepdims=False), ffn_conv, m_ffn_conv, v_ffn_conv),
        ("final_norm", s_final, final_norm, m_final_norm, v_final_norm),
    ]
    shapes = [t[2].shape for t in small]
    packed = [_pack([t[k] for t in small])[None] for k in (1, 2, 3, 4)]
    outs, _ = sum_adamw([packed[0]], packed[1], packed[2], packed[3], "adamw_small")
    unpacked = [_unpack(t[0], shapes) for t in outs]
    for idx, t in enumerate(small):
        res[t[0]] = [unpacked[k][idx] for k in range(4)]

    order = ["a_mix_norm", "a_w_in", "a_conv", "a_w_out", "b_mix_norm", "b_w_dq", "b_q_norm",
             "b_w_uq", "b_w_o", "kv_in_norm", "w_dkv", "kv_norm", "w_ukv", "ffn_norm",
             "ffn_w_up", "ffn_conv", "ffn_w_down", "final_norm"]
    return (loss, grad_x, *[res[n][0] for n in order], *[res[n][1] for n in order],
            *[res[n][2] for n in order], *[res[n][3] for n in order])
```

```python
import functools

import jax
import jax.numpy as jnp
from jax import lax
from jax.experimental import pallas as pl
from jax.experimental.pallas import tpu as pltpu

F32 = jnp.float32
BF16 = jnp.bfloat16

N_DEV = 8
N_HEADS = 8
NOPE = 128
ROPE = 64
ROPE_PAD = 128
QK = NOPE + ROPE_PAD
VDIM = 128
KV_RANK = 256
ROPE_THETA = 10000.0
RMS_EPS = 1e-6
ATTN_SCALE = (NOPE + ROPE) ** -0.5
N_A = 2
N_B = 2
DEPTH = 4

ADAM_LR = 0.001
ADAM_B1 = 0.9
ADAM_B2 = 0.999
ADAM_EPS = 1e-08
ADAM_WD = 0.01
ADAM_STEP = 10

V7X_VMEM_LIMIT = 56 * 1024 * 1024
BF16_SUBLANES = 16
ROW_TILE = 512
ROW_TILE_SMALL = 256
ATTN_TILE = 512
MIXER_CHUNK = 512
LANES = 128
NEG_BIG = -1e30
COPIES_PER_TASK = 7

MESH_ID = pl.DeviceIdType.MESH
ANY = pl.BlockSpec(memory_space=pl.ANY)


def _nt(a, b):
    return lax.dot_general(a, b, (((1,), (1,)), ((), ())), preferred_element_type=F32)


def _tn(a, b):
    return lax.dot_general(a, b, (((0,), (0,)), ((), ())), preferred_element_type=F32)


def _nn(a, b):
    return jnp.dot(a, b, preferred_element_type=F32)


def _rms(h, g):
    rstd = lax.rsqrt(jnp.mean(h * h, axis=-1, keepdims=True) + RMS_EPS)
    xhat = h * rstd
    return xhat * g, xhat, rstd


def _rms_bwd(dxn, xhat, rstd, g):
    dxhat = dxn * g
    dh = rstd * (dxhat - xhat * jnp.mean(dxhat * xhat, axis=-1, keepdims=True))
    return dh, dxn * xhat


def _shift_down(x, k, halo_rows):
    r = pltpu.roll(x, k, 0)
    row = lax.broadcasted_iota(jnp.int32, x.shape, 0)
    for t in range(k):
        r = jnp.where(row == t, halo_rows[t], r)
    return r


def _shift_up(x, k, halo_rows):
    n = x.shape[0]
    r = pltpu.roll(x, n - k, 0)
    row = lax.broadcasted_iota(jnp.int32, x.shape, 0)
    for t in range(k):
        r = jnp.where(row == n - k + t, halo_rows[t], r)
    return r


def _conv_taps(w_ref):
    return w_ref[0:1, :], w_ref[1:2, :], w_ref[2:3, :]


def _rope_swap(x):
    lane = lax.broadcasted_iota(jnp.int32, x.shape, 1)
    return jnp.where(lane < ROPE // 2, pltpu.roll(x, ROPE_PAD - ROPE // 2, 1),
                     pltpu.roll(x, ROPE // 2, 1))


def _rope_fwd(x, cos, sin):
    return x * cos + _rope_swap(x) * sin


def _rope_bwd(dy, cos, sin):
    return dy * cos - _rope_swap(dy) * sin


def _rope_tables(seq):
    inv = 1.0 / (ROPE_THETA ** (jnp.arange(0, ROPE, 2, dtype=F32) / ROPE))
    ang = jnp.arange(seq, dtype=F32)[:, None] * inv[None, :]
    cos, sin = jnp.cos(ang), jnp.sin(ang)
    zero = jnp.zeros((seq, ROPE_PAD - ROPE), F32)
    return (jnp.concatenate([cos, cos, zero], axis=1),
            jnp.concatenate([-sin, sin, zero], axis=1))


def _row_tile(rows, cap, mult=8):
    best = None
    for t in range(mult, min(rows, cap) + 1, mult):
        if rows % t == 0:
            best = t
    return rows if best is None else best


class _AllGatherTask:
    def __init__(self, t, x_ref, out_ref, send_sems, recv_sems, local_sems):
        self.t, self.x_ref, self.out_ref = t, x_ref, out_ref
        self.send_sems, self.recv_sems, self.local_sems = send_sems, recv_sems, local_sems
        mx, my, mc = lax.axis_index("x"), lax.axis_index("y"), lax.axis_index("c")
        self.mc = mc
        self.me, self.sibling = (mx, my, mc), (mx, my, 1 - mc)
        self.chips = [(1 - mx, my), (mx, 1 - my), (1 - mx, 1 - my)]

    def _slot(self, px, py, pc):
        return self.out_ref.at[4 * px + 2 * py + pc]

    def _copy(self, k, block, to, src=None):
        s = COPIES_PER_TASK * self.t + k
        return pltpu.make_async_remote_copy(
            src_ref=self._slot(*block) if src is None else src, dst_ref=self._slot(*block),
            send_sem=self.send_sems.at[s], recv_sem=self.recv_sems.at[s],
            device_id=to, device_id_type=MESH_ID)

    def _mine(self):
        return pltpu.make_async_copy(self.x_ref, self._slot(*self.me), self.local_sems.at[self.t])

    def _first(self):
        out = [self._copy(0, self.me, self.sibling, src=self.x_ref)]
        out += [self._copy(1 + j, self.me, (*chip, self.mc), src=self.x_ref)
                for j, chip in enumerate(self.chips)]
        return out

    def _passed(self):
        return [self._copy(4 + j, (*chip, self.mc), self.sibling) for j, chip in enumerate(self.chips)]

    def start(self):
        self._mine().start()
        for cp in self._first():
            cp.start()

    def forward(self):
        passed = self._passed()
        for j, chip in enumerate(self.chips):
            self._copy(1 + j, (*chip, self.mc), self.me).wait_recv()
            passed[j].start()

    def finish(self):
        self._copy(0, self.sibling, self.me).wait_recv()
        for j, chip in enumerate(self.chips):
            self._copy(4 + j, (*chip, 1 - self.mc), self.me).wait_recv()
        for cp in self._first() + self._passed():
            cp.wait_send()
        self._mine().wait()


class _ReduceScatterTask:
    def __init__(self, t, g_ref, out_ref, send_sems, recv_sems, local_sems):
        self.t, self.g_ref, self.out_ref = t, g_ref, out_ref
        self.send_sems, self.recv_sems, self.local_sems = send_sems, recv_sems, local_sems
        mx, my, mc = lax.axis_index("x"), lax.axis_index("y"), lax.axis_index("c")
        self.me = 4 * mx + 2 * my + mc
        self.peers = []
        for k in range(1, N_DEV):
            px, py, pc = mx ^ ((k >> 2) & 1), my ^ ((k >> 1) & 1), mc ^ (k & 1)
            self.peers.append(((px, py, pc), 4 * px + 2 * py + pc))

    def _mine(self):
        return pltpu.make_async_copy(self.g_ref.at[self.me], self.out_ref.at[self.me],
                                     self.local_sems.at[self.t])

    def _copy(self, k, src_slot, dst_slot):
        s = COPIES_PER_TASK * self.t + k
        return pltpu.make_async_remote_copy(
            src_ref=self.g_ref.at[src_slot], dst_ref=self.out_ref.at[dst_slot],
            send_sem=self.send_sems.at[s], recv_sem=self.recv_sems.at[s],
            device_id=self.peers[k][0], device_id_type=MESH_ID)

    def start(self):
        self._mine().start()
        for k, (_, peer) in enumerate(self.peers):
            self._copy(k, peer, self.me).start()

    def forward(self):
        pass

    def finish(self):
        for k, (_, peer) in enumerate(self.peers):
            self._copy(k, self.me, peer).wait_recv()
        for k, (_, peer) in enumerate(self.peers):
            self._copy(k, peer, self.me).wait_send()
        self._mine().wait()


_TASKS = {"ag": _AllGatherTask, "rs": _ReduceScatterTask}


def _task_shape(kind, arr):
    shape = (N_DEV,) + arr.shape if kind == "ag" else arr.shape
    return jax.ShapeDtypeStruct(shape, arr.dtype)


def _sem_shapes(n_tasks):
    return [pltpu.SemaphoreType.DMA((COPIES_PER_TASK * n_tasks,)),
            pltpu.SemaphoreType.DMA((COPIES_PER_TASK * n_tasks,)),
            pltpu.SemaphoreType.DMA((n_tasks,))]


def _make_tasks(rider, in_refs, out_refs, sems):
    return [_TASKS[kind](t, in_refs[t], out_refs[t], *sems) for t, (kind, _) in enumerate(rider)]


def exchange(rider, name):
    n = len(rider)

    def body(*refs):
        tasks = _make_tasks(rider, refs[:n], refs[n:2 * n], refs[2 * n:])
        for task in tasks:
            task.start()
        for task in tasks:
            task.forward()
        for task in tasks:
            task.finish()

    return list(pl.pallas_call(
        body, name=name, out_shape=tuple(_task_shape(k, a) for k, a in rider),
        in_specs=[ANY] * n, out_specs=(ANY,) * n, scratch_shapes=_sem_shapes(n),
    )(*[a for _, a in rider]))


def _call(body, name, grid, in_specs, out_specs, out_shape, args, scratch=(), rider=()):
    in_specs, out_specs, out_shape = list(in_specs), tuple(out_specs), tuple(out_shape)
    n_in, n_out, n_scr, n_r = len(in_specs), len(out_specs), len(scratch), len(rider)
    if n_r:
        def kern(*refs):
            ins, r_in = refs[:n_in], refs[n_in:n_in + n_r]
            o0 = n_in + n_r
            outs, r_out = refs[o0:o0 + n_out], refs[o0 + n_out:o0 + n_out + n_r]
            s0 = o0 + n_out + n_r
            scr, sems = refs[s0:s0 + n_scr], refs[s0 + n_scr:]
            step = 0
            for a, n in enumerate(grid):
                step = step * n + pl.program_id(a)
            n_steps = 1
            for n in grid:
                n_steps *= n

            @pl.when(step == 0)
            def _():
                for task in _make_tasks(rider, r_in, r_out, sems):
                    task.start()
            body(*ins, *outs, *scr)

            @pl.when(step == (3 * (n_steps - 1)) // 4)
            def _():
                for task in _make_tasks(rider, r_in, r_out, sems):
                    task.forward()

            @pl.when(step == n_steps - 1)
            def _():
                for task in _make_tasks(rider, r_in, r_out, sems):
                    task.finish()
    else:
        kern = body
    res = pl.pallas_call(
        kern, name=name, grid=grid,
        in_specs=in_specs + [ANY] * n_r, out_specs=out_specs + (ANY,) * n_r,
        out_shape=out_shape + tuple(_task_shape(k, a) for k, a in rider),
        scratch_shapes=list(scratch) + (_sem_shapes(n_r) if n_r else []),
        compiler_params=pltpu.CompilerParams(dimension_semantics=("arbitrary",) * len(grid),
                                             vmem_limit_bytes=V7X_VMEM_LIMIT),
    )(*args, *[a for _, a in rider])
    return list(res[:n_out]), list(res[n_out:])


def _adamw(g, w, m, v):
    m = ADAM_B1 * m + (1.0 - ADAM_B1) * g
    v = ADAM_B2 * v + (1.0 - ADAM_B2) * (g * g)
    m_hat = m / (1.0 - ADAM_B1 ** ADAM_STEP)
    v_hat = v / (1.0 - ADAM_B2 ** ADAM_STEP)
    delta = -ADAM_LR * (m_hat / (jnp.sqrt(v_hat) + ADAM_EPS) + ADAM_WD * w)
    return delta, m, v


def sum_adamw(parts, w, m, v, name, rider=()):
    n_l, rows, cols = w.shape
    n = parts[0].shape[0]
    mult = BF16_SUBLANES if parts[0].dtype == BF16 else 8
    tr = _row_tile(rows, 128, mult)
    n_i = rows // tr

    def body(*refs):
        part_refs = refs[:n_l]
        w_ref, m_ref, v_ref, g_out, d_out, m_out, v_out = refs[n_l:]
        layer = pl.program_id(0)
        for k in range(n_l):
            @pl.when(layer == k)
            def _(k=k):
                g = part_refs[k][0].astype(F32)
                for s in range(1, n):
                    g = g + part_refs[k][s].astype(F32)
                delta, m_new, v_new = _adamw(g, w_ref[...], m_ref[...], v_ref[...])
                g_out[...] = g
                d_out[...] = delta
                m_out[...] = m_new
                v_out[...] = v_new

    part_specs = [pl.BlockSpec((n, tr, cols), functools.partial(
        lambda l, i, k: (0, jnp.where(l == k, i, 0), 0), k=k)) for k in range(n_l)]
    wspec = pl.BlockSpec((None, tr, cols), lambda l, i: (l, i, 0))
    shape = jax.ShapeDtypeStruct(w.shape, F32)
    return _call(body, name, (n_l, n_i), part_specs + [wspec] * 3, (wspec,) * 4, (shape,) * 4,
                 (*parts, w, m, v), rider=rider)


def sum_slots(parts, name):
    n, rows, cols = parts.shape

    def body(p_ref, o_ref):
        acc = p_ref[0]
        for s in range(1, n):
            acc = acc + p_ref[s]
        o_ref[...] = acc

    return pl.pallas_call(
        body, name=name, out_shape=jax.ShapeDtypeStruct((rows, cols), F32),
        in_specs=[pl.BlockSpec(memory_space=pltpu.VMEM)],
        out_specs=pl.BlockSpec(memory_space=pltpu.VMEM),
    )(parts)


def norm_fwd(h, g, name):
    seq, d = h.shape
    tm = min(ROW_TILE, seq)

    def body(h_ref, g_ref, o_ref):
        o_ref[...] = _rms(h_ref[...], g_ref[...])[0].astype(BF16)

    return _call(body, name, (seq // tm,),
                 [pl.BlockSpec((tm, d), lambda i: (i, 0)), pl.BlockSpec((1, d), lambda i: (0, 0))],
                 [pl.BlockSpec((tm, d), lambda i: (i, 0))],
                 [jax.ShapeDtypeStruct((seq, d), BF16)], (h, g))[0][0]


def proj_residual(a, w, res, name, g_next=None, rider=()):
    nb, seq, kb = a.shape
    d = w.shape[-1]
    tm = min(ROW_TILE, seq)
    with_norm = g_next is not None

    def body(a_ref, w_ref, r_ref, *rest):
        acc = r_ref[...]
        for b in range(nb):
            acc = acc + _nn(a_ref[b], w_ref[b])
        if with_norm:
            g_ref, o_ref, xn_ref = rest
            xn_ref[...] = _rms(acc, g_ref[...])[0].astype(BF16)
        else:
            (o_ref,) = rest
        o_ref[...] = acc

    row = pl.BlockSpec((tm, d), lambda i: (i, 0))
    in_specs = [pl.BlockSpec((nb, tm, kb), lambda i: (0, i, 0)),
                pl.BlockSpec((nb, kb, d), lambda i: (0, 0, 0)), row]
    args = [a, w, res]
    out_specs, out_shape = [row], [jax.ShapeDtypeStruct((seq, d), F32)]
    if with_norm:
        in_specs.append(pl.BlockSpec((1, d), lambda i: (0, 0)))
        args.append(g_next)
        out_specs.append(row)
        out_shape.append(jax.ShapeDtypeStruct((seq, d), BF16))
    outs, r_outs = _call(body, name, (seq // tm,), in_specs, out_specs, out_shape, args, rider=rider)
    return (outs[0], outs[1] if with_norm else None), r_outs


def proj_t_rms_bwd(du, w, h, g, dres, name, rider=()):
    nb, seq, wd = du.shape
    k = w.shape[1]
    tm = min(ROW_TILE_SMALL, seq)

    def body(du_ref, w_ref, h_ref, g_ref, dr_ref, dh_ref, dg_ref):
        i = pl.program_id(0)
        dxn = _nt(du_ref[0], w_ref[0])
        for b in range(1, nb):
            dxn = dxn + _nt(du_ref[b], w_ref[b])
        _, xhat, rstd = _rms(h_ref[...], g_ref[...])
        dh, dg_rows = _rms_bwd(dxn, xhat, rstd, g_ref[...])
        dh_ref[...] = dr_ref[...] + dh

        @pl.when(i == 0)
        def _():
            dg_ref[...] = jnp.zeros_like(dg_ref)
        dg_ref[...] += jnp.sum(dg_rows, axis=0, keepdims=True)

    row = pl.BlockSpec((tm, k), lambda i: (i, 0))
    vec = pl.BlockSpec((1, k), lambda i: (0, 0))
    return _call(body, name, (seq // tm,),
                 [pl.BlockSpec((nb, tm, wd), lambda i: (0, i, 0)),
                  pl.BlockSpec((nb, k, wd), lambda i: (0, 0, 0)), row, vec, row],
                 (row, vec),
                 (jax.ShapeDtypeStruct((seq, k), F32), jax.ShapeDtypeStruct((1, k), F32)),
                 (du, w, h, g, dres), rider=rider)


def mixer_fwd(xn, win3, cw, name, rider=()):
    seq, d = xn.shape
    tm = min(ROW_TILE, seq)
    cc = min(MIXER_CHUNK, d)
    n_c, n_i = d // cc, seq // tm

    def body(x_ref, w_ref, cw_ref, u_ref, z_ref, carry):
        i = pl.program_id(1)

        @pl.when(i == 0)
        def _():
            carry[...] = jnp.zeros_like(carry)
        xb = x_ref[...]
        b = _nn(xb, w_ref[0])
        c = _nn(xb, w_ref[1])
        hh = _nn(xb, w_ref[2])
        p = c * hh
        w0, w1, w2 = _conv_taps(cw_ref)
        p1 = _shift_down(p, 1, [carry[7:8, :]])
        p2 = _shift_down(p, 2, [carry[6:7, :], carry[7:8, :]])
        q = w0 * p2 + w1 * p1 + w2 * p
        carry[...] = p[tm - 8:tm, :]
        u_ref[0] = b.astype(BF16)
        u_ref[1] = c.astype(BF16)
        u_ref[2] = hh.astype(BF16)
        u_ref[3] = q.astype(BF16)
        z_ref[...] = (b * q).astype(BF16)

    return _call(body, name, (n_c, n_i),
                 [pl.BlockSpec((tm, d), lambda c, i: (i, 0)),
                  pl.BlockSpec((3, d, cc), lambda c, i: (0, 0, c)),
                  pl.BlockSpec((3, cc), lambda c, i: (0, c))],
                 (pl.BlockSpec((4, tm, cc), lambda c, i: (0, i, c)),
                  pl.BlockSpec((tm, cc), lambda c, i: (i, c))),
                 (jax.ShapeDtypeStruct((4, seq, d), BF16), jax.ShapeDtypeStruct((seq, d), BF16)),
                 (xn, win3, cw), scratch=[pltpu.VMEM((8, cc), F32)], rider=rider)


def mixer_bwd(dh, wout, u4, z, xn, cw, name, rider=()):
    seq, d = xn.shape
    tm = min(ROW_TILE, seq)
    cc = min(MIXER_CHUNK, d)
    n_c, n_i = d // cc, seq // tm

    def body(dh_ref, wout_ref, u_ref, z_ref, x_ref, cw_ref,
             du_ref, dwin_ref, dwout_ref, dcw_ref, acc_in, acc_out, acc_cw, carry):
        i = pl.program_id(1)

        @pl.when(i == 0)
        def _():
            acc_in[...] = jnp.zeros_like(acc_in)
            acc_out[...] = jnp.zeros_like(acc_out)
            acc_cw[...] = jnp.zeros_like(acc_cw)
            carry[...] = jnp.zeros_like(carry)
        dhb = dh_ref[...].astype(BF16)
        dz = _nt(dhb, wout_ref[...])
        acc_out[...] += _tn(z_ref[...], dhb)
        b = u_ref[0].astype(F32)
        c = u_ref[1].astype(F32)
        hh = u_ref[2].astype(F32)
        q = u_ref[3].astype(F32)
        p = c * hh
        db = dz * q
        dq = dz * b
        w0, w1, w2 = _conv_taps(cw_ref)
        dq1 = _shift_up(dq, 1, [carry[0:1, :]])
        dq2 = _shift_up(dq, 2, [carry[0:1, :], carry[1:2, :]])
        dp = w2 * dq + w1 * dq1 + w0 * dq2
        carry[...] = dq[0:8, :]
        acc_cw[0:1, :] += jnp.sum(dq2 * p, axis=0, keepdims=True)
        acc_cw[1:2, :] += jnp.sum(dq1 * p, axis=0, keepdims=True)
        acc_cw[2:3, :] += jnp.sum(dq * p, axis=0, keepdims=True)
        dbb = db.astype(BF16)
        dcb = (dp * hh).astype(BF16)
        dhhb = (dp * c).astype(BF16)
        du_ref[0] = dbb
        du_ref[1] = dcb
        du_ref[2] = dhhb
        xb = x_ref[...]
        acc_in[0] += _tn(xb, dbb)
        acc_in[1] += _tn(xb, dcb)
        acc_in[2] += _tn(xb, dhhb)

        @pl.when(i == n_i - 1)
        def _():
            dwin_ref[...] = acc_in[...].astype(BF16)
            dwout_ref[...] = acc_out[...].astype(BF16)
            dcw_ref[...] = acc_cw[0:3, :]

    rev = lambda c, i: (n_i - 1 - i, 0)
    return _call(body, name, (n_c, n_i),
                 [pl.BlockSpec((tm, d), rev),
                  pl.BlockSpec((cc, d), lambda c, i: (c, 0)),
                  pl.BlockSpec((4, tm, cc), lambda c, i: (0, n_i - 1 - i, c)),
                  pl.BlockSpec((tm, cc), lambda c, i: (n_i - 1 - i, c)),
                  pl.BlockSpec((tm, d), rev),
                  pl.BlockSpec((3, cc), lambda c, i: (0, c))],
                 (pl.BlockSpec((3, tm, cc), lambda c, i: (0, n_i - 1 - i, c)),
                  pl.BlockSpec((3, d, cc), lambda c, i: (0, 0, c)),
                  pl.BlockSpec((cc, d), lambda c, i: (c, 0)),
                  pl.BlockSpec((3, cc), lambda c, i: (0, c))),
                 (jax.ShapeDtypeStruct((3, seq, d), BF16), jax.ShapeDtypeStruct((3, d, d), BF16),
                  jax.ShapeDtypeStruct((d, d), BF16), jax.ShapeDtypeStruct((3, d), F32)),
                 (dh, wout, u4, z, xn, cw),
                 scratch=[pltpu.VMEM((3, d, cc), F32), pltpu.VMEM((cc, d), F32),
                          pltpu.VMEM((8, cc), F32), pltpu.VMEM((8, cc), F32)], rider=rider)


def _silu_parts(cg):
    sg = 1.0 / (1.0 + jnp.exp(-cg))
    return sg, cg * sg


def ffn_fwd(xn, wup, fcw, name, rider=()):
    seq, d = xn.shape
    f8 = wup.shape[-1]
    half = N_DEV // 2
    tm = min(ROW_TILE, seq)
    n_i = seq // tm

    def body(x_ref, wg_ref, wu_ref, cg_ref, cu_ref, up_ref, cv_ref, a_ref, carry):
        i = pl.program_id(1)

        @pl.when(i == 0)
        def _():
            carry[...] = jnp.zeros_like(carry)
        xb = x_ref[...]
        conv = []
        for s, (w_ref, t_ref) in enumerate(((wg_ref, cg_ref), (wu_ref, cu_ref))):
            u = _nn(xb, w_ref[...])
            up_ref[s] = u.astype(BF16)
            w0, w1, w2 = _conv_taps(t_ref)
            u1 = _shift_down(u, 1, [carry[s, 7:8, :]])
            u2 = _shift_down(u, 2, [carry[s, 6:7, :], carry[s, 7:8, :]])
            cv = w0 * u2 + w1 * u1 + w2 * u
            cv_ref[s] = cv.astype(BF16)
            conv.append(cv)
            carry[s] = u[tm - 8:tm, :]
        _, silu = _silu_parts(conv[0])
        a_ref[...] = (silu * conv[1]).astype(BF16)

    blk = pl.BlockSpec((2, None, tm, f8), lambda c, i: (0, c, i, 0))
    big = jax.ShapeDtypeStruct((2, half, seq, f8), BF16)
    return _call(body, name, (half, n_i),
                 [pl.BlockSpec((tm, d), lambda c, i: (i, 0)),
                  pl.BlockSpec((None, d, f8), lambda c, i: (c, 0, 0)),
                  pl.BlockSpec((None, d, f8), lambda c, i: (c + half, 0, 0)),
                  pl.BlockSpec((None, 3, f8), lambda c, i: (c, 0, 0)),
                  pl.BlockSpec((None, 3, f8), lambda c, i: (c + half, 0, 0))],
                 (blk, blk, pl.BlockSpec((None, tm, f8), lambda c, i: (c, i, 0))),
                 (big, big, jax.ShapeDtypeStruct((half, seq, f8), BF16)),
                 (xn, wup, wup, fcw, fcw), scratch=[pltpu.VMEM((2, 8, f8), F32)], rider=rider)


def ffn_bwd(dh, wdown, up2, cv2, act, xn, fcw, name, rider=()):
    seq, d = xn.shape
    f8 = up2.shape[-1]
    fb = wdown.shape[1]
    half = N_DEV // 2
    tm = min(ROW_TILE, seq)
    n_i = seq // tm

    def body(dh_ref, wd_ref, up_ref, cv_ref, a_ref, x_ref, cg_ref, cu_ref,
             dup_ref, dwup_ref, dwd_ref, dcw_ref, acc_up, acc_down, acc_cw, carry):
        i = pl.program_id(1)

        @pl.when(i == 0)
        def _():
            acc_up[...] = jnp.zeros_like(acc_up)
            acc_down[...] = jnp.zeros_like(acc_down)
            acc_cw[...] = jnp.zeros_like(acc_cw)
            carry[...] = jnp.zeros_like(carry)
        dhb = dh_ref[...].astype(BF16)
        da = _nt(dhb, wd_ref[...])
        acc_down[...] += _tn(a_ref[...], dhb)
        cg = cv_ref[0].astype(F32)
        cu = cv_ref[1].astype(F32)
        sg, silu = _silu_parts(cg)
        dcg = da * cu * (sg * (1.0 + cg * (1.0 - sg)))
        dcu = da * silu
        xb = x_ref[...]
        for s, (dc, t_ref) in enumerate(((dcg, cg_ref), (dcu, cu_ref))):
            w0, w1, w2 = _conv_taps(t_ref)
            d1 = _shift_up(dc, 1, [carry[s, 0:1, :]])
            d2 = _shift_up(dc, 2, [carry[s, 0:1, :], carry[s, 1:2, :]])
            du = (w2 * dc + w1 * d1 + w0 * d2).astype(BF16)
            carry[s] = dc[0:8, :]
            u = up_ref[s].astype(F32)
            acc_cw[s, 0:1, :] += jnp.sum(d2 * u, axis=0, keepdims=True)
            acc_cw[s, 1:2, :] += jnp.sum(d1 * u, axis=0, keepdims=True)
            acc_cw[s, 2:3, :] += jnp.sum(dc * u, axis=0, keepdims=True)
            dup_ref[s] = du
            acc_up[s] += _tn(xb, du)

        @pl.when(i == n_i - 1)
        def _():
            dwup_ref[...] = acc_up[...].astype(BF16)
            dwd_ref[...] = acc_down[...].astype(BF16)
            dcw_ref[...] = acc_cw[:, 0:3, :]

    rev = lambda c, i: (n_i - 1 - i, 0)
    blk = pl.BlockSpec((2, None, tm, f8), lambda c, i: (0, c, n_i - 1 - i, 0))
    return _call(body, name, (half, n_i),
                 [pl.BlockSpec((tm, d), rev),
                  pl.BlockSpec((None, fb, d), lambda c, i: (c, 0, 0)),
                  blk, blk,
                  pl.BlockSpec((None, tm, f8), lambda c, i: (c, n_i - 1 - i, 0)),
                  pl.BlockSpec((tm, d), rev),
                  pl.BlockSpec((None, 3, f8), lambda c, i: (c, 0, 0)),
                  pl.BlockSpec((None, 3, f8), lambda c, i: (c + half, 0, 0))],
                 (blk,
                  pl.BlockSpec((2, None, d, f8), lambda c, i: (0, c, 0, 0)),
                  pl.BlockSpec((None, fb, d), lambda c, i: (c, 0, 0)),
                  pl.BlockSpec((2, None, 3, f8), lambda c, i: (0, c, 0, 0))),
                 (jax.ShapeDtypeStruct((2, half, seq, f8), BF16),
                  jax.ShapeDtypeStruct((2, half, d, f8), BF16),
                  jax.ShapeDtypeStruct((half, fb, d), BF16),
                  jax.ShapeDtypeStruct((2, half, 3, f8), F32)),
                 (dh, wdown, up2, cv2, act, xn, fcw, fcw),
                 scratch=[pltpu.VMEM((2, d, f8), F32), pltpu.VMEM((fb, d), F32),
                          pltpu.VMEM((2, 8, f8), F32), pltpu.VMEM((2, 8, f8), F32)], rider=rider)


def q_fwd(xn, wdq, gq, wuq, cos, sin, name, rider=()):
    seq, d = xn.shape
    rank = wdq.shape[-1]
    tm = min(ROW_TILE, seq)

    def body(x_ref, wdq_ref, gq_ref, wuq_ref, cos_ref, sin_ref, q_ref):
        qc = _nn(x_ref[...], wdq_ref[...])
        qn = _rms(qc, gq_ref[...])[0].astype(BF16)
        for hd in range(N_HEADS):
            qh = _nn(qn, wuq_ref[hd])
            qr = _rope_fwd(qh[:, NOPE:QK], cos_ref[...], sin_ref[...])
            q_ref[hd, :, 0:NOPE] = (qh[:, 0:NOPE] * ATTN_SCALE).astype(BF16)
            q_ref[hd, :, NOPE:QK] = (qr * ATTN_SCALE).astype(BF16)

    rope = pl.BlockSpec((tm, ROPE_PAD), lambda i: (i, 0))
    return _call(body, name, (seq // tm,),
                 [pl.BlockSpec((tm, d), lambda i: (i, 0)),
                  pl.BlockSpec((d, rank), lambda i: (0, 0)),
                  pl.BlockSpec((1, rank), lambda i: (0, 0)),
                  pl.BlockSpec((N_HEADS, rank, QK), lambda i: (0, 0, 0)), rope, rope],
                 [pl.BlockSpec((N_HEADS, tm, QK), lambda i: (0, i, 0))],
                 [jax.ShapeDtypeStruct((N_HEADS, seq, QK), BF16)],
                 (xn, wdq, gq, wuq, cos, sin), rider=rider)


def q_bwd(dq, xn, wdq, gq, wuq, cos, sin, name, rider=()):
    seq, d = xn.shape
    rank = wdq.shape[-1]
    tm = min(ROW_TILE_SMALL, seq)
    n_i = seq // tm

    def body(dq_ref, x_ref, wdq_ref, gq_ref, wuq_ref, cos_ref, sin_ref,
             dqc_ref, dwuq_ref, dwdq_ref, dgq_ref, acc_uq, acc_dq):
        i = pl.program_id(0)

        @pl.when(i == 0)
        def _():
            acc_uq[...] = jnp.zeros_like(acc_uq)
            acc_dq[...] = jnp.zeros_like(acc_dq)
            dgq_ref[...] = jnp.zeros_like(dgq_ref)
        xb = x_ref[...]
        qc = _nn(xb, wdq_ref[...])
        qn, qhat, qrstd = _rms(qc, gq_ref[...])
        qnb = qn.astype(BF16)
        dqn = jnp.zeros((tm, rank), F32)
        for hd in range(N_HEADS):
            dnope = (dq_ref[hd, :, 0:NOPE] * ATTN_SCALE).astype(BF16)
            drope = _rope_bwd(dq_ref[hd, :, NOPE:QK] * ATTN_SCALE, cos_ref[...], sin_ref[...])
            draw = jnp.concatenate([dnope, drope.astype(BF16)], axis=1)
            dqn = dqn + _nt(draw, wuq_ref[hd])
            acc_uq[hd] += _tn(qnb, draw)
        dqc, dg_rows = _rms_bwd(dqn, qhat, qrstd, gq_ref[...])
        dgq_ref[...] += jnp.sum(dg_rows, axis=0, keepdims=True)
        dqcb = dqc.astype(BF16)
        dqc_ref[0] = dqcb
        acc_dq[...] += _tn(xb, dqcb)

        @pl.when(i == n_i - 1)
        def _():
            dwuq_ref[...] = acc_uq[...].astype(BF16)
            dwdq_ref[...] = acc_dq[...].astype(BF16)

    rope = pl.BlockSpec((tm, ROPE_PAD), lambda i: (i, 0))
    return _call(body, name, (n_i,),
                 [pl.BlockSpec((N_HEADS, tm, QK), lambda i: (0, i, 0)),
                  pl.BlockSpec((tm, d), lambda i: (i, 0)),
                  pl.BlockSpec((d, rank), lambda i: (0, 0)),
                  pl.BlockSpec((1, rank), lambda i: (0, 0)),
                  pl.BlockSpec((N_HEADS, rank, QK), lambda i: (0, 0, 0)), rope, rope],
                 (pl.BlockSpec((1, tm, rank), lambda i: (0, i, 0)),
                  pl.BlockSpec((N_HEADS, rank, QK), lambda i: (0, 0, 0)),
                  pl.BlockSpec((d, rank), lambda i: (0, 0)),
                  pl.BlockSpec((1, rank), lambda i: (0, 0))),
                 (jax.ShapeDtypeStruct((1, seq, rank), BF16),
                  jax.ShapeDtypeStruct((N_HEADS, rank, QK), BF16),
                  jax.ShapeDtypeStruct((d, rank), BF16),
                  jax.ShapeDtypeStruct((1, rank), F32)),
                 (dq, xn, wdq, gq, wuq, cos, sin),
                 scratch=[pltpu.VMEM((N_HEADS, rank, QK), F32), pltpu.VMEM((d, rank), F32)],
                 rider=rider)


def kv_fwd(h, g, wdkv, gkv, wukv, cos, sin, name, rider=()):
    seq, d = h.shape
    tm = min(ROW_TILE, seq)
    wk = KV_RANK + ROPE_PAD

    def body(h_ref, g_ref, wdkv_ref, gkv_ref, wukv_ref, cos_ref, sin_ref, k_ref, v_ref, c_ref):
        xk = _rms(h_ref[...], g_ref[...])[0].astype(BF16)
        ckv = _nn(xk, wdkv_ref[...])
        c_kv = ckv[:, 0:KV_RANK]
        c_ref[...] = c_kv
        kr = _rope_fwd(ckv[:, KV_RANK:wk], cos_ref[...], sin_ref[...]).astype(BF16)
        ckn = _rms(c_kv, gkv_ref[...])[0].astype(BF16)
        for hd in range(N_HEADS):
            kvh = _nn(ckn, wukv_ref[hd])
            k_ref[hd, :, 0:NOPE] = kvh[:, 0:NOPE].astype(BF16)
            k_ref[hd, :, NOPE:QK] = kr
            v_ref[hd] = kvh[:, NOPE:NOPE + VDIM].astype(BF16)

    rope = pl.BlockSpec((tm, ROPE_PAD), lambda i: (i, 0))
    return _call(body, name, (seq // tm,),
                 [pl.BlockSpec((tm, d), lambda i: (i, 0)),
                  pl.BlockSpec((1, d), lambda i: (0, 0)),
                  pl.BlockSpec((d, wk), lambda i: (0, 0)),
                  pl.BlockSpec((1, KV_RANK), lambda i: (0, 0)),
                  pl.BlockSpec((N_HEADS, KV_RANK, NOPE + VDIM), lambda i: (0, 0, 0)), rope, rope],
                 (pl.BlockSpec((N_HEADS, tm, QK), lambda i: (0, i, 0)),
                  pl.BlockSpec((N_HEADS, tm, VDIM), lambda i: (0, i, 0)),
                  pl.BlockSpec((tm, KV_RANK), lambda i: (i, 0))),
                 (jax.ShapeDtypeStruct((N_HEADS, seq, QK), BF16),
                  jax.ShapeDtypeStruct((N_HEADS, seq, VDIM), BF16),
                  jax.ShapeDtypeStruct((seq, KV_RANK), F32)),
                 (h, g, wdkv, gkv, wukv, cos, sin), rider=rider)


def kv_bwd(dks, dvs, c_kv, h, g, gkv, wukv, cos, sin, name, rider=()):
    seq, d = h.shape
    tm = min(ROW_TILE_SMALL, seq)
    n_i = seq // tm
    wk = KV_RANK + ROPE_PAD
    n_b = len(dks)

    def body(*refs):
        dk_refs = refs[:n_b]
        dv_refs = refs[n_b:2 * n_b]
        (c_ref, h_ref, g_ref, gkv_ref, wukv_ref, cos_ref, sin_ref,
         dckv_ref, dwukv_ref, dwdkv_ref, dgkv_ref, acc_ukv, acc_dkv) = refs[2 * n_b:]
        i = pl.program_id(0)

        @pl.when(i == 0)
        def _():
            acc_ukv[...] = jnp.zeros_like(acc_ukv)
            acc_dkv[...] = jnp.zeros_like(acc_dkv)
            dgkv_ref[...] = jnp.zeros_like(dgkv_ref)
        ckn, chat, crstd = _rms(c_ref[...], gkv_ref[...])
        cknb = ckn.astype(BF16)
        dckn = jnp.zeros((tm, KV_RANK), F32)
        dkr = jnp.zeros((tm, ROPE_PAD), F32)
        for hd in range(N_HEADS):
            dk = dk_refs[0][hd]
            dv = dv_refs[0][hd]
            for j in range(1, n_b):
                dk = dk + dk_refs[j][hd]
                dv = dv + dv_refs[j][hd]
            dkr = dkr + dk[:, NOPE:QK]
            dkvh = jnp.concatenate([dk[:, 0:NOPE].astype(BF16), dv.astype(BF16)], axis=1)
            dckn = dckn + _nt(dkvh, wukv_ref[hd])
            acc_ukv[hd] += _tn(cknb, dkvh)
        dc_kv, dg_rows = _rms_bwd(dckn, chat, crstd, gkv_ref[...])
        dgkv_ref[...] += jnp.sum(dg_rows, axis=0, keepdims=True)
        dkr_raw = _rope_bwd(dkr, cos_ref[...], sin_ref[...])
        dckv = jnp.concatenate([dc_kv.astype(BF16), dkr_raw.astype(BF16)], axis=1)
        dckv_ref[0] = dckv
        xk = _rms(h_ref[...], g_ref[...])[0].astype(BF16)
        acc_dkv[...] += _tn(xk, dckv)

        @pl.when(i == n_i - 1)
        def _():
            dwukv_ref[...] = acc_ukv[...].astype(BF16)
            dwdkv_ref[...] = acc_dkv[...].astype(BF16)

    kspec = pl.BlockSpec((N_HEADS, tm, QK), lambda i: (0, i, 0))
    vspec = pl.BlockSpec((N_HEADS, tm, VDIM), lambda i: (0, i, 0))
    rope = pl.BlockSpec((tm, ROPE_PAD), lambda i: (i, 0))
    return _call(body, name, (n_i,),
                 [kspec] * n_b + [vspec] * n_b + [
                     pl.BlockSpec((tm, KV_RANK), lambda i: (i, 0)),
                     pl.BlockSpec((tm, d), lambda i: (i, 0)),
                     pl.BlockSpec((1, d), lambda i: (0, 0)),
                     pl.BlockSpec((1, KV_RANK), lambda i: (0, 0)),
                     pl.BlockSpec((N_HEADS, KV_RANK, NOPE + VDIM), lambda i: (0, 0, 0)), rope, rope],
                 (pl.BlockSpec((1, tm, wk), lambda i: (0, i, 0)),
                  pl.BlockSpec((N_HEADS, KV_RANK, NOPE + VDIM), lambda i: (0, 0, 0)),
                  pl.BlockSpec((d, wk), lambda i: (0, 0)),
                  pl.BlockSpec((1, KV_RANK), lambda i: (0, 0))),
                 (jax.ShapeDtypeStruct((1, seq, wk), BF16),
                  jax.ShapeDtypeStruct((N_HEADS, KV_RANK, NOPE + VDIM), BF16),
                  jax.ShapeDtypeStruct((d, wk), BF16),
                  jax.ShapeDtypeStruct((1, KV_RANK), F32)),
                 (*dks, *dvs, c_kv, h, g, gkv, wukv, cos, sin),
                 scratch=[pltpu.VMEM((N_HEADS, KV_RANK, NOPE + VDIM), F32), pltpu.VMEM((d, wk), F32)],
                 rider=rider)


def o_bwd(dh, o, wo, name, rider=()):
    seq, d = dh.shape
    hv = o.shape[1]
    tm = min(ROW_TILE, seq)
    n_i = seq // tm

    def body(dh_ref, o_ref, wo_ref, do_ref, dwo_ref, acc):
        i = pl.program_id(0)

        @pl.when(i == 0)
        def _():
            acc[...] = jnp.zeros_like(acc)
        dhb = dh_ref[...].astype(BF16)
        do_ref[...] = _nt(dhb, wo_ref[...]).astype(BF16)
        acc[...] += _tn(o_ref[...], dhb)

        @pl.when(i == n_i - 1)
        def _():
            dwo_ref[...] = acc[...].astype(BF16)

    return _call(body, name, (n_i,),
                 [pl.BlockSpec((tm, d), lambda i: (i, 0)),
                  pl.BlockSpec((tm, hv), lambda i: (i, 0)),
                  pl.BlockSpec((hv, d), lambda i: (0, 0))],
                 (pl.BlockSpec((tm, hv), lambda i: (i, 0)),
                  pl.BlockSpec((hv, d), lambda i: (0, 0))),
                 (jax.ShapeDtypeStruct((seq, hv), BF16), jax.ShapeDtypeStruct((hv, d), BF16)),
                 (dh, o, wo), scratch=[pltpu.VMEM((hv, d), F32)], rider=rider)


def _mask_diagonal(s):
    row = lax.broadcasted_iota(jnp.int32, s.shape, 0)
    col = lax.broadcasted_iota(jnp.int32, s.shape, 1)
    return jnp.where(col <= row, s, NEG_BIG)


def attn_fwd(q, k, v, name, rider=()):
    _, seq, _ = q.shape
    t = min(ATTN_TILE, seq)
    n_q = seq // t

    def body(q_ref, k_ref, v_ref, o_ref, lse_ref):
        qi = pl.program_id(1)
        qb = q_ref[...]

        def rows(j):
            return pl.ds(pl.multiple_of(j * t, t), t)

        def scores(j):
            return _nt(qb, k_ref[rows(j), :])

        def update(j, s, m, l, acc):
            m_new = jnp.maximum(m, jnp.max(s, axis=1, keepdims=True))
            p = jnp.exp(s - m_new)
            alpha = jnp.exp(m - m_new)
            l = alpha * l + jnp.sum(p, axis=1, keepdims=True)
            acc = alpha * acc + _nn(p.astype(BF16), v_ref[rows(j), :])
            return m_new, l, acc

        def step(j, carry):
            s, m, l, acc = carry
            s_next = scores(j + 1)
            return (s_next,) + update(j, s, m, l, acc)

        init = (scores(0), jnp.full((t, 1), NEG_BIG, F32), jnp.zeros((t, 1), F32),
                jnp.zeros((t, VDIM), F32))
        s, m, l, acc = lax.fori_loop(0, qi, step, init)
        m, l, acc = update(qi, _mask_diagonal(s), m, l, acc)
        o_ref[...] = (acc / l).astype(BF16)
        lse_ref[...] = jnp.broadcast_to(m + jnp.log(l), (t, LANES))

    return _call(body, name, (N_HEADS, n_q),
                 [pl.BlockSpec((None, t, QK), lambda h, i: (h, i, 0)),
                  pl.BlockSpec((None, seq, QK), lambda h, i: (h, 0, 0)),
                  pl.BlockSpec((None, seq, VDIM), lambda h, i: (h, 0, 0))],
                 (pl.BlockSpec((t, VDIM), lambda h, i: (i, h)),
                  pl.BlockSpec((None, t, LANES), lambda h, i: (h, i, 0))),
                 (jax.ShapeDtypeStruct((seq, N_HEADS * VDIM), BF16),
                  jax.ShapeDtypeStruct((N_HEADS, seq, LANES), F32)),
                 (q, k, v), rider=rider)


def attn_bwd(q, k, v, o, do, lse, name, rider=()):
    _, seq, _ = q.shape
    t = min(ATTN_TILE, seq)
    n_q = seq // t

    def body(q_ref, k_ref, v_ref, o_ref, do_ref, lse_ref, dq_ref, dk_ref, dv_ref):
        kj = pl.program_id(1)

        @pl.when(kj == 0)
        def _():
            dq_ref[...] = jnp.zeros_like(dq_ref)
        kb = k_ref[...]
        vb = v_ref[...]

        def rows(i):
            return pl.ds(pl.multiple_of(i * t, t), t)

        def products(i):
            return _nt(q_ref[rows(i), :], kb), _nt(do_ref[rows(i), :], vb)

        def consume(i, s, dp, dk, dv):
            r = rows(i)
            qb = q_ref[r, :]
            dob = do_ref[r, :]
            row = lax.broadcasted_iota(jnp.int32, s.shape, 0)
            col = lax.broadcasted_iota(jnp.int32, s.shape, 1)
            s = jnp.where(jnp.logical_or(col <= row, i != kj), s, NEG_BIG)
            p = jnp.exp(s - lse_ref[r, 0:1])
            delta = jnp.sum(dob.astype(F32) * o_ref[r, :].astype(F32), axis=1, keepdims=True)
            ds = (p * (dp - delta)).astype(BF16)
            dv = dv + _tn(p.astype(BF16), dob)
            dk = dk + _tn(ds, qb)
            dq_ref[r, :] += _nn(ds, kb)
            return dk, dv

        def step(i, carry):
            s, dp, dk, dv = carry
            s_next, dp_next = products(i + 1)
            return (s_next, dp_next) + consume(i, s, dp, dk, dv)

        init = products(kj) + (jnp.zeros((t, QK), F32), jnp.zeros((t, VDIM), F32))
        s, dp, dk, dv = lax.fori_loop(kj, n_q - 1, step, init)
        dk, dv = consume(n_q - 1, s, dp, dk, dv)
        dk_ref[...] = dk
        dv_ref[...] = dv

    head_rows = pl.BlockSpec((seq, VDIM), lambda h, j: (0, h))
    return _call(body, name, (N_HEADS, n_q),
                 [pl.BlockSpec((None, seq, QK), lambda h, j: (h, 0, 0)),
                  pl.BlockSpec((None, t, QK), lambda h, j: (h, j, 0)),
                  pl.BlockSpec((None, t, VDIM), lambda h, j: (h, j, 0)),
                  head_rows, head_rows,
                  pl.BlockSpec((None, seq, LANES), lambda h, j: (h, 0, 0))],
                 (pl.BlockSpec((None, seq, QK), lambda h, j: (h, 0, 0)),
                  pl.BlockSpec((None, t, QK), lambda h, j: (h, j, 0)),
                  pl.BlockSpec((None, t, VDIM), lambda h, j: (h, j, 0))),
                 (jax.ShapeDtypeStruct((N_HEADS, seq, QK), F32),
                  jax.ShapeDtypeStruct((N_HEADS, seq, QK), F32),
                  jax.ShapeDtypeStruct((N_HEADS, seq, VDIM), F32)),
                 (q, k, v, o, do, lse), rider=rider)


def loss_head(h, g, target, name):
    seq, d = h.shape
    tm = min(ROW_TILE, seq)

    def body(h_ref, g_ref, t_ref, l_ref, dh_ref, dg_ref):
        i = pl.program_id(0)

        @pl.when(i == 0)
        def _():
            l_ref[...] = jnp.zeros_like(l_ref)
            dg_ref[...] = jnp.zeros_like(dg_ref)
        y, xhat, rstd = _rms(h_ref[...], g_ref[...])
        diff = y - t_ref[...]
        l_ref[...] += jnp.sum(jnp.sum(diff * diff, axis=1, keepdims=True), axis=0, keepdims=True)
        dh, dg_rows = _rms_bwd(diff * (1.0 / d), xhat, rstd, g_ref[...])
        dh_ref[...] = dh
        dg_ref[...] += jnp.sum(dg_rows, axis=0, keepdims=True)

    row = pl.BlockSpec((tm, d), lambda i: (i, 0))
    vec = pl.BlockSpec((1, d), lambda i: (0, 0))
    return _call(body, name, (seq // tm,), [row, vec, row],
                 (pl.BlockSpec((1, LANES), lambda i: (0, 0)), row, vec),
                 (jax.ShapeDtypeStruct((1, LANES), F32), jax.ShapeDtypeStruct((seq, d), F32),
                  jax.ShapeDtypeStruct((1, d), F32)),
                 (h, g, target))[0]


def _pack(parts):
    rows = []
    for p in parts:
        flat = p.reshape(-1)
        n_rows = -(-flat.shape[0] // (8 * LANES)) * 8
        flat = jnp.pad(flat, (0, n_rows * LANES - flat.shape[0]))
        rows.append(flat.reshape(n_rows, LANES))
    return jnp.concatenate(rows, axis=0)


def _unpack(packed, shapes):
    lead = packed.shape[:-2]
    out, r0 = [], 0
    for shape in shapes:
        size = 1
        for s in shape:
            size *= s
        n_rows = -(-size // (8 * LANES)) * 8
        part = packed[..., r0:r0 + n_rows, :].reshape(lead + (n_rows * LANES,))
        out.append(part[..., :size].reshape(lead + tuple(shape)))
        r0 += n_rows
    return out


FWD_RIDERS = {
    "mixer_fwd0": [("ffn_w_up", 0)],
    "mixer_out0": [("w_dkv", 0), ("w_ukv", 0), ("b_w_dq", 0), ("b_w_uq", 0)],
    "ffn_fwd0": [("ffn_w_down", 0), ("a_w_in", 1), ("a_w_out", 1)],
    "ffn_out0": [("ffn_w_down", 1)],
    "mixer_fwd1": [("ffn_w_up", 1)],
    "mixer_out1": [("b_w_o", 0)],
    "ffn_fwd1": [("ffn_w_up", 2)],
    "ffn_out1": [("ffn_w_down", 2)],
    "kv_fwd": [("b_w_dq", 1), ("b_w_uq", 1)],
    "q_fwd0": [("b_w_o", 1)],
    "attn_fwd0": [("ffn_w_up", 3), ("ffn_w_down", 3)],
}
BWD_RIDERS = {
    "ffn_in_bwd3": [("ffn_w_down", 3)],
    "attn_bwd1": [("ffn_w_up", 3), ("b_w_o", 1)],
    "q_in_bwd1": [("b_w_dq", 1)],
    "ffn_bwd2": [("b_w_uq", 1)],
    "ffn_in_bwd2": [("ffn_w_down", 2)],
    "attn_bwd0": [("ffn_w_up", 2), ("b_w_o", 0)],
    "q_in_bwd0": [("b_w_dq", 0)],
    "kv_bwd": [("b_w_uq", 0)],
    "kv_in_bwd": [("w_ukv", 0)],
    "ffn_bwd1": [("w_dkv", 0)],
    "ffn_in_bwd1": [("ffn_w_down", 1)],
    "mixer_in_bwd1": [("a_w_out", 1)],
    "ffn_bwd0": [("ffn_w_up", 1), ("a_w_in", 1)],
    "ffn_in_bwd0": [("ffn_w_down", 0)],
    "mixer_bwd0": [("ffn_w_up", 0)],
    "mixer_in_bwd0": [("a_w_out", 0), ("a_w_in", 0)],
}


def kernel(x, a_mix_norm, a_w_in, a_conv, a_w_out, b_mix_norm, b_w_dq, b_q_norm, b_w_uq, b_w_o, kv_in_norm, w_dkv, kv_norm, w_ukv, ffn_norm, ffn_w_up, ffn_conv, ffn_w_down, final_norm, loss_target, m_a_mix_norm, m_a_w_in, m_a_conv, m_a_w_out, m_b_mix_norm, m_b_w_dq, m_b_q_norm, m_b_w_uq, m_b_w_o, m_kv_in_norm, m_w_dkv, m_kv_norm, m_w_ukv, m_ffn_norm, m_ffn_w_up, m_ffn_conv, m_ffn_w_down, m_final_norm, v_a_mix_norm, v_a_w_in, v_a_conv, v_a_w_out, v_b_mix_norm, v_b_w_dq, v_b_q_norm, v_b_w_uq, v_b_w_o, v_kv_in_norm, v_w_dkv, v_kv_norm, v_w_ukv, v_ffn_norm, v_ffn_w_up, v_ffn_conv, v_ffn_w_down, v_final_norm):
    seq, d = x.shape[1], x.shape[2]
    me = 4 * lax.axis_index("x") + 2 * lax.axis_index("y") + lax.axis_index("c")
    h0 = x.reshape(seq, d)
    target = loss_target.reshape(seq, d)
    cos, sin = _rope_tables(seq)
    rank = b_w_dq.shape[-1]
    f8 = ffn_w_up.shape[-1]
    fd = ffn_w_down.shape[1]
    dshard = a_w_out.shape[1]
    hv = N_HEADS * VDIM

    shards = {"a_w_in": a_w_in, "a_w_out": a_w_out, "b_w_dq": b_w_dq, "b_w_uq": b_w_uq,
              "b_w_o": b_w_o, "w_dkv": w_dkv[None], "w_ukv": w_ukv[None],
              "ffn_w_up": ffn_w_up, "ffn_w_down": ffn_w_down}

    def relayout(name, g):
        if name == "a_w_in":
            w = jnp.transpose(g, (1, 0, 2)).reshape(d, 3, d)
            return jnp.transpose(w, (1, 0, 2))
        if name == "a_w_out":
            return g.reshape(d, d)
        if name == "b_w_dq":
            return g.reshape(d, rank)
        if name == "b_w_uq":
            return jnp.pad(g, ((0, 0), (0, 0), (0, QK - NOPE - ROPE)))
        if name == "b_w_o":
            return g.reshape(hv, d)
        if name == "w_dkv":
            return jnp.pad(g.reshape(d, KV_RANK + ROPE), ((0, 0), (0, ROPE_PAD - ROPE)))
        if name == "ffn_w_down":
            return g.reshape(N_DEV // 2, 2 * fd, d)
        return g

    weights = {}

    def ag_rider(host):
        return [("ag", shards[n][l].astype(BF16)) for n, l in FWD_RIDERS.get(host, [])]

    def ag_done(host, outs):
        for (n, l), g in zip(FWD_RIDERS.get(host, []), outs):
            weights[n, l] = relayout(n, g)

    small_shapes = [a_mix_norm.shape, a_conv.shape, ffn_conv.shape]
    first = exchange([("ag", a_w_in[0].astype(BF16)), ("ag", a_w_out[0].astype(BF16)),
                      ("ag", _pack([a_mix_norm, a_conv, ffn_conv]))], "ag_first")
    weights["a_w_in", 0] = relayout("a_w_in", first[0])
    weights["a_w_out", 0] = relayout("a_w_out", first[1])
    s_mix, s_aconv, s_fconv = _unpack(first[2], small_shapes)
    a_gain = jnp.transpose(s_mix, (1, 0, 2)).reshape(N_A, d)
    a_cw = jnp.transpose(s_aconv, (1, 2, 0, 3)).reshape(N_A, 3, d)
    f_cw = jnp.transpose(s_fconv, (1, 0, 2, 3))

    def mixer_gain(layer):
        if layer >= DEPTH:
            return None
        return a_gain[layer][None] if layer < N_A else b_mix_norm[layer - N_A][None]

    saved = {}
    h = h0
    xn = norm_fwd(h, mixer_gain(0), "norm_first")
    kv = None
    for layer in range(DEPTH):
        saved["hm", layer], saved["xm", layer] = h, xn
        if layer < N_A:
            name = f"mixer_fwd{layer}"
            (u4, z), r = mixer_fwd(xn, weights["a_w_in", layer], a_cw[layer], name, rider=ag_rider(name))
            ag_done(name, r)
            saved["mix", layer] = (u4, z)
            name = f"mixer_out{layer}"
            (h, xn), r = proj_residual(z[None], weights["a_w_out", layer][None], h, name,
                                       g_next=ffn_norm[layer][None], rider=ag_rider(name))
            ag_done(name, r)
        else:
            j = layer - N_A
            name = f"q_fwd{j}"
            (q,), r = q_fwd(xn, weights["b_w_dq", j], b_q_norm[j][None], weights["b_w_uq", j],
                            cos, sin, name, rider=ag_rider(name))
            ag_done(name, r)
            name = f"attn_fwd{j}"
            (o, lse), r = attn_fwd(q, kv[0], kv[1], name, rider=ag_rider(name))
            ag_done(name, r)
            saved["attn", layer] = (q, o, lse)
            name = f"attn_out{j}"
            (h, xn), r = proj_residual(o[None], weights["b_w_o", j][None], h, name,
                                       g_next=ffn_norm[layer][None], rider=ag_rider(name))
            ag_done(name, r)
        saved["hf", layer], saved["xf", layer] = h, xn
        name = f"ffn_fwd{layer}"
        (up2, cv2, act), r = ffn_fwd(xn, weights["ffn_w_up", layer], f_cw[layer], name, rider=ag_rider(name))
        ag_done(name, r)
        saved["ffn", layer] = (up2, cv2, act)
        name = f"ffn_out{layer}"
        (h, xn), r = proj_residual(act, weights["ffn_w_down", layer], h, name,
                                   g_next=mixer_gain(layer + 1), rider=ag_rider(name))
        ag_done(name, r)
        if layer == N_A - 1:
            (k_all, v_all, c_kv), r = kv_fwd(h, kv_in_norm[None], weights["w_dkv", 0], kv_norm[None],
                                             weights["w_ukv", 0], cos, sin, "kv_fwd",
                                             rider=ag_rider("kv_fwd"))
            ag_done("kv_fwd", r)
            kv = (k_all, v_all, c_kv)

    sq_err, dh, d_final = loss_head(h, final_norm[None], target, "loss_head")
    loss = lax.psum(sq_err[0, 0] * (0.5 / d), ("x", "y", "c"))

    grads = {}
    parts = {}

    def rs_rider(host):
        return [("rs", grads[key]) for key in BWD_RIDERS.get(host, [])]

    def rs_done(host, outs):
        for key, p in zip(BWD_RIDERS.get(host, []), outs):
            parts[key] = p

    d_ffn_norm = [None] * DEPTH
    d_fconv = [None] * DEPTH
    d_a_gain = [None] * N_A
    d_aconv = [None] * N_A
    d_b_gain = [None] * N_B
    d_q_gain = [None] * N_B
    dks, dvs = [], []
    for layer in reversed(range(DEPTH)):
        if layer == N_A - 1:
            hk = saved["hm", layer + 1]
            (dckv, dwukv, dwdkv, d_kv_gain), r = kv_bwd(
                dks, dvs, kv[2], hk, kv_in_norm[None], kv_norm[None], weights["w_ukv", 0],
                cos, sin, "kv_bwd", rider=rs_rider("kv_bwd"))
            rs_done("kv_bwd", r)
            grads["w_ukv", 0] = dwukv
            grads["w_dkv", 0] = dwdkv[:, :KV_RANK + ROPE].reshape(N_DEV, dshard, KV_RANK + ROPE)
            (dh, d_kvin_gain), r = proj_t_rms_bwd(dckv, weights["w_dkv", 0][None], hk, kv_in_norm[None],
                                                  dh, "kv_in_bwd", rider=rs_rider("kv_in_bwd"))
            rs_done("kv_in_bwd", r)
        up2, cv2, act = saved["ffn", layer]
        name = f"ffn_bwd{layer}"
        (dup2, dwup, dwdown, dcw), r = ffn_bwd(dh, weights["ffn_w_down", layer], up2, cv2, act,
                                               saved["xf", layer], f_cw[layer], name, rider=rs_rider(name))
        rs_done(name, r)
        grads["ffn_w_up", layer] = dwup.reshape(N_DEV, d, f8)
        grads["ffn_w_down", layer] = dwdown.reshape(N_DEV, fd, d)
        d_fconv[layer] = dcw.reshape(N_DEV, 3, f8)
        name = f"ffn_in_bwd{layer}"
        (dh, d_ffn_norm[layer]), r = proj_t_rms_bwd(dup2.reshape(N_DEV, seq, f8), weights["ffn_w_up", layer],
                                                    saved["hf", layer], ffn_norm[layer][None], dh, name,
                                                    rider=rs_rider(name))
        rs_done(name, r)
        hm, xm = saved["hm", layer], saved["xm", layer]
        if layer < N_A:
            u4, z = saved["mix", layer]
            name = f"mixer_bwd{layer}"
            (du3, dwin3, dwout, dcw), r = mixer_bwd(dh, weights["a_w_out", layer], u4, z, xm, a_cw[layer],
                                                    name, rider=rs_rider(name))
            rs_done(name, r)
            dwin = jnp.transpose(dwin3, (1, 0, 2)).reshape(d, N_DEV, 3 * d // N_DEV)
            grads["a_w_in", layer] = jnp.transpose(dwin, (1, 0, 2))
            grads["a_w_out", layer] = dwout.reshape(N_DEV, dshard, d)
            d_aconv[layer] = dcw
            name = f"mixer_in_bwd{layer}"
            (dh, d_a_gain[layer]), r = proj_t_rms_bwd(du3, weights["a_w_in", layer], hm, a_gain[layer][None],
                                                      dh, name, rider=rs_rider(name))
            rs_done(name, r)
        else:
            j = layer - N_A
            q, o, lse = saved["attn", layer]
            name = f"attn_out_bwd{j}"
            (do, dwo), r = o_bwd(dh, o, weights["b_w_o", j], name, rider=rs_rider(name))
            rs_done(name, r)
            grads["b_w_o", j] = dwo.reshape(N_DEV, dshard, d)
            name = f"attn_bwd{j}"
            (dq, dk, dv), r = attn_bwd(q, kv[0], kv[1], o, do, lse, name, rider=rs_rider(name))
            rs_done(name, r)
            dks.append(dk)
            dvs.append(dv)
            name = f"q_bwd{j}"
            (dqc, dwuq, dwdq, d_q_gain[j]), r = q_bwd(dq, xm, weights["b_w_dq", j], b_q_norm[j][None],
                                                      weights["b_w_uq", j], cos, sin, name, rider=rs_rider(name))
            rs_done(name, r)
            grads["b_w_uq", j] = dwuq[:, :, :NOPE + ROPE]
            grads["b_w_dq", j] = dwdq.reshape(N_DEV, dshard, rank)
            name = f"q_in_bwd{j}"
            (dh, d_b_gain[j]), r = proj_t_rms_bwd(dqc, weights["b_w_dq", j][None], hm, b_mix_norm[j][None],
                                                  dh, name, rider=rs_rider(name))
            rs_done(name, r)
    grad_x = dh.reshape(x.shape)

    full_small = [
        jnp.concatenate(d_a_gain, axis=0),
        jnp.stack(d_aconv),
        jnp.concatenate(d_b_gain, axis=0),
        jnp.concatenate(d_q_gain, axis=0),
        d_kvin_gain[0],
        d_kv_gain[0],
        jnp.concatenate(d_ffn_norm, axis=0),
        jnp.stack(d_fconv),
        d_final[0],
    ]
    full_shapes = [t.shape for t in full_small]
    small_pack = _pack(full_small)

    res = {}

    def update(name, n_layers, w, m, v, extra=()):
        shard = w.shape if w.ndim == 3 else (1,) + w.shape
        host = f"adamw_{name}"
        outs, r = sum_adamw([parts[name, l] for l in range(n_layers)], w.reshape(shard),
                            m.reshape(shard), v.reshape(shard), host,
                            rider=rs_rider(host) + list(extra))
        rs_done(host, r)
        res[name] = [t.reshape(w.shape) for t in outs]
        return r[len(BWD_RIDERS.get(host, [])):]

    (g_parts,) = update("a_w_out", N_A, a_w_out, m_a_w_out, v_a_w_out, extra=[("ag", small_pack)])
    update("ffn_w_down", DEPTH, ffn_w_down, m_ffn_w_down, v_ffn_w_down)
    update("ffn_w_up", DEPTH, ffn_w_up, m_ffn_w_up, v_ffn_w_up)
    update("b_w_dq", N_B, b_w_dq, m_b_w_dq, v_b_w_dq)
    update("b_w_uq", N_B, b_w_uq, m_b_w_uq, v_b_w_uq)
    update("b_w_o", N_B, b_w_o, m_b_w_o, v_b_w_o)
    update("w_dkv", 1, w_dkv, m_w_dkv, v_w_dkv)
    update("w_ukv", 1, w_ukv, m_w_ukv, v_w_ukv)
    update("a_w_in", N_A, a_w_in, m_a_w_in, v_a_w_in)

    summed = sum_slots(g_parts, "sum_small_grads")
    (s_a_gain, s_aconv_g, s_b_gain, s_q_gain, s_kvin, s_kvn, s_ffn_gain, s_fconv_g,
     s_final) = _unpack(summed, full_shapes)
    dsl = d // N_DEV
    small = [
        ("a_mix_norm", lax.dynamic_slice_in_dim(s_a_gain, me * dsl, dsl, axis=1), a_mix_norm, m_a_mix_norm, v_a_mix_norm),
        ("a_conv", lax.dynamic_slice_in_dim(s_aconv_g, me * dsl, dsl, axis=2), a_conv, m_a_conv, v_a_conv),
        ("b_mix_norm", s_b_gain, b_mix_norm, m_b_mix_norm, v_b_mix_norm),
        ("b_q_norm", s_q_gain, b_q_norm, m_b_q_norm, v_b_q_norm),
        ("kv_in_norm", s_kvin, kv_in_norm, m_kv_in_norm, v_kv_in_norm),
        ("kv_norm", s_kvn, kv_norm, m_kv_norm, v_kv_norm),
        ("ffn_norm", s_ffn_gain, ffn_norm, m_ffn_norm, v_ffn_norm),
        ("ffn_conv", lax.dynamic_index_in_dim(s_fconv_g, me, axis=1, keepdims=False), ffn_conv, m_ffn_conv, v_ffn_conv),
        ("final_norm", s_final, final_norm, m_final_norm, v_final_norm),
    ]
    shapes = [t[2].shape for t in small]
    packed = [_pack([t[k] for t in small])[None] for k in (1, 2, 3, 4)]
    outs, _ = sum_adamw([packed[0]], packed[1], packed[2], packed[3], "adamw_small")
    unpacked = [_unpack(t[0], shapes) for t in outs]
    for idx, t in enumerate(small):
        res[t[0]] = [unpacked[k][idx] for k in range(4)]

    order = ["a_mix_norm", "a_w_in", "a_conv", "a_w_out", "b_mix_norm", "b_w_dq", "b_q_norm",
             "b_w_uq", "b_w_o", "kv_in_norm", "w_dkv", "kv_norm", "w_ukv", "ffn_norm",
             "ffn_w_up", "ffn_conv", "ffn_w_down", "final_norm"]
    return (loss, grad_x, *[res[n][0] for n in order], *[res[n][1] for n in order],
            *[res[n][2] for n in order], *[res[n][3] for n in order])
```

```python
import functools

import jax
import jax.numpy as jnp
from jax import lax
from jax.experimental import pallas as pl
from jax.experimental.pallas import tpu as pltpu

F32 = jnp.float32
BF16 = jnp.bfloat16

N_DEV = 8
N_HEADS = 8
NOPE = 128
ROPE = 64
ROPE_PAD = 128
QK = NOPE + ROPE_PAD
VDIM = 128
KV_RANK = 256
ROPE_THETA = 10000.0
RMS_EPS = 1e-6
ATTN_SCALE = (NOPE + ROPE) ** -0.5
N_A = 2
N_B = 2
DEPTH = 4

ADAM_LR = 0.001
ADAM_B1 = 0.9
ADAM_B2 = 0.999
ADAM_EPS = 1e-08
ADAM_WD = 0.01
ADAM_STEP = 10

V7X_VMEM_LIMIT = 56 * 1024 * 1024
BF16_SUBLANES = 16
ROW_TILE = 512
ROW_TILE_SMALL = 256
ATTN_TILE = 512
MIXER_CHUNK = 512
LANES = 128
NEG_BIG = -1e30
COPIES_PER_TASK = 7

MESH_ID = pl.DeviceIdType.MESH
ANY = pl.BlockSpec(memory_space=pl.ANY)


def _nt(a, b):
    return lax.dot_general(a, b, (((1,), (1,)), ((), ())), preferred_element_type=F32)


def _tn(a, b):
    return lax.dot_general(a, b, (((0,), (0,)), ((), ())), preferred_element_type=F32)


def _nn(a, b):
    return jnp.dot(a, b, preferred_element_type=F32)


def _rms(h, g):
    rstd = lax.rsqrt(jnp.mean(h * h, axis=-1, keepdims=True) + RMS_EPS)
    xhat = h * rstd
    return xhat * g, xhat, rstd


def _rms_bwd(dxn, xhat, rstd, g):
    dxhat = dxn * g
    dh = rstd * (dxhat - xhat * jnp.mean(dxhat * xhat, axis=-1, keepdims=True))
    return dh, dxn * xhat


def _shift_down(x, k, halo_rows):
    r = pltpu.roll(x, k, 0)
    row = lax.broadcasted_iota(jnp.int32, x.shape, 0)
    for t in range(k):
        r = jnp.where(row == t, halo_rows[t], r)
    return r


def _shift_up(x, k, halo_rows):
    n = x.shape[0]
    r = pltpu.roll(x, n - k, 0)
    row = lax.broadcasted_iota(jnp.int32, x.shape, 0)
    for t in range(k):
        r = jnp.where(row == n - k + t, halo_rows[t], r)
    return r


def _conv_taps(w_ref):
    return w_ref[0:1, :], w_ref[1:2, :], w_ref[2:3, :]


def _rope_swap(x):
    lane = lax.broadcasted_iota(jnp.int32, x.shape, 1)
    return jnp.where(lane < ROPE // 2, pltpu.roll(x, ROPE_PAD - ROPE // 2, 1),
                     pltpu.roll(x, ROPE // 2, 1))


def _rope_fwd(x, cos, sin):
    return x * cos + _rope_swap(x) * sin


def _rope_bwd(dy, cos, sin):
    return dy * cos - _rope_swap(dy) * sin


def _rope_tables(seq):
    inv = 1.0 / (ROPE_THETA ** (jnp.arange(0, ROPE, 2, dtype=F32) / ROPE))
    ang = jnp.arange(seq, dtype=F32)[:, None] * inv[None, :]
    cos, sin = jnp.cos(ang), jnp.sin(ang)
    zero = jnp.zeros((seq, ROPE_PAD - ROPE), F32)
    return (jnp.concatenate([cos, cos, zero], axis=1),
            jnp.concatenate([-sin, sin, zero], axis=1))


def _row_tile(rows, cap, mult=8):
    best = None
    for t in range(mult, min(rows, cap) + 1, mult):
        if rows % t == 0:
            best = t
    return rows if best is None else best


class _AllGatherTask:
    def __init__(self, t, x_ref, out_ref, send_sems, recv_sems, local_sems):
        self.t, self.x_ref, self.out_ref = t, x_ref, out_ref
        self.send_sems, self.recv_sems, self.local_sems = send_sems, recv_sems, local_sems
        mx, my, mc = lax.axis_index("x"), lax.axis_index("y"), lax.axis_index("c")
        self.mc = mc
        self.me, self.sibling = (mx, my, mc), (mx, my, 1 - mc)
        self.chips = [(1 - mx, my), (mx, 1 - my), (1 - mx, 1 - my)]

    def _slot(self, px, py, pc):
        return self.out_ref.at[4 * px + 2 * py + pc]

    def _copy(self, k, block, to, src=None):
        s = COPIES_PER_TASK * self.t + k
        return pltpu.make_async_remote_copy(
            src_ref=self._slot(*block) if src is None else src, dst_ref=self._slot(*block),
            send_sem=self.send_sems.at[s], recv_sem=self.recv_sems.at[s],
            device_id=to, device_id_type=MESH_ID)

    def _mine(self):
        return pltpu.make_async_copy(self.x_ref, self._slot(*self.me), self.local_sems.at[self.t])

    def _first(self):
        out = [self._copy(0, self.me, self.sibling, src=self.x_ref)]
        out += [self._copy(1 + j, self.me, (*chip, self.mc), src=self.x_ref)
                for j, chip in enumerate(self.chips)]
        return out

    def _passed(self):
        return [self._copy(4 + j, (*chip, self.mc), self.sibling) for j, chip in enumerate(self.chips)]

    def start(self):
        self._mine().start()
        for cp in self._first():
            cp.start()

    def forward(self):
        passed = self._passed()
        for j, chip in enumerate(self.chips):
            self._copy(1 + j, (*chip, self.mc), self.me).wait_recv()
            passed[j].start()

    def finish(self):
        self._copy(0, self.sibling, self.me).wait_recv()
        for j, chip in enumerate(self.chips):
            self._copy(4 + j, (*chip, 1 - self.mc), self.me).wait_recv()
        for cp in self._first() + self._passed():
            cp.wait_send()
        self._mine().wait()


class _ReduceScatterTask:
    def __init__(self, t, g_ref, out_ref, send_sems, recv_sems, local_sems):
        self.t, self.g_ref, self.out_ref = t, g_ref, out_ref
        self.send_sems, self.recv_sems, self.local_sems = send_sems, recv_sems, local_sems
        mx, my, mc = lax.axis_index("x"), lax.axis_index("y"), lax.axis_index("c")
        self.me = 4 * mx + 2 * my + mc
        self.peers = []
        for k in range(1, N_DEV):
            px, py, pc = mx ^ ((k >> 2) & 1), my ^ ((k >> 1) & 1), mc ^ (k & 1)
            self.peers.append(((px, py, pc), 4 * px + 2 * py + pc))

    def _mine(self):
        return pltpu.make_async_copy(self.g_ref.at[self.me], self.out_ref.at[self.me],
                                     self.local_sems.at[self.t])

    def _copy(self, k, src_slot, dst_slot):
        s = COPIES_PER_TASK * self.t + k
        return pltpu.make_async_remote_copy(
            src_ref=self.g_ref.at[src_slot], dst_ref=self.out_ref.at[dst_slot],
            send_sem=self.send_sems.at[s], recv_sem=self.recv_sems.at[s],
            device_id=self.peers[k][0], device_id_type=MESH_ID)

    def start(self):
        self._mine().start()
        for k, (_, peer) in enumerate(self.peers):
            self._copy(k, peer, self.me).start()

    def forward(self):
        pass

    def finish(self):
        for k, (_, peer) in enumerate(self.peers):
            self._copy(k, self.me, peer).wait_recv()
        for k, (_, peer) in enumerate(self.peers):
            self._copy(k, peer, self.me).wait_send()
        self._mine().wait()


_TASKS = {"ag": _AllGatherTask, "rs": _ReduceScatterTask}


def _task_shape(kind, arr):
    shape = (N_DEV,) + arr.shape if kind == "ag" else arr.shape
    return jax.ShapeDtypeStruct(shape, arr.dtype)


def _sem_shapes(n_tasks):
    return [pltpu.SemaphoreType.DMA((COPIES_PER_TASK * n_tasks,)),
            pltpu.SemaphoreType.DMA((COPIES_PER_TASK * n_tasks,)),
            pltpu.SemaphoreType.DMA((n_tasks,))]


def _make_tasks(rider, in_refs, out_refs, sems):
    return [_TASKS[kind](t, in_refs[t], out_refs[t], *sems) for t, (kind, _) in enumerate(rider)]


def exchange(rider, name):
    n = len(rider)

    def body(*refs):
        tasks = _make_tasks(rider, refs[:n], refs[n:2 * n], refs[2 * n:])
        for task in tasks:
            task.start()
        for task in tasks:
            task.forward()
        for task in tasks:
            task.finish()

    return list(pl.pallas_call(
        body, name=name, out_shape=tuple(_task_shape(k, a) for k, a in rider),
        in_specs=[ANY] * n, out_specs=(ANY,) * n, scratch_shapes=_sem_shapes(n),
    )(*[a for _, a in rider]))


def _call(body, name, grid, in_specs, out_specs, out_shape, args, scratch=(), rider=()):
    in_specs, out_specs, out_shape = list(in_specs), tuple(out_specs), tuple(out_shape)
    n_in, n_out, n_scr, n_r = len(in_specs), len(out_specs), len(scratch), len(rider)
    if n_r:
        def kern(*refs):
            ins, r_in = refs[:n_in], refs[n_in:n_in + n_r]
            o0 = n_in + n_r
            outs, r_out = refs[o0:o0 + n_out], refs[o0 + n_out:o0 + n_out + n_r]
            s0 = o0 + n_out + n_r
            scr, sems = refs[s0:s0 + n_scr], refs[s0 + n_scr:]
            step = 0
            for a, n in enumerate(grid):
                step = step * n + pl.program_id(a)
            n_steps = 1
            for n in grid:
                n_steps *= n

            @pl.when(step == 0)
            def _():
                for task in _make_tasks(rider, r_in, r_out, sems):
                    task.start()
            body(*ins, *outs, *scr)

            @pl.when(step == (3 * (n_steps - 1)) // 4)
            def _():
                for task in _make_tasks(rider, r_in, r_out, sems):
                    task.forward()

            @pl.when(step == n_steps - 1)
            def _():
                for task in _make_tasks(rider, r_in, r_out, sems):
                    task.finish()
    else:
        kern = body
    res = pl.pallas_call(
        kern, name=name, grid=grid,
        in_specs=in_specs + [ANY] * n_r, out_specs=out_specs + (ANY,) * n_r,
        out_shape=out_shape + tuple(_task_shape(k, a) for k, a in rider),
        scratch_shapes=list(scratch) + (_sem_shapes(n_r) if n_r else []),
        compiler_params=pltpu.CompilerParams(dimension_semantics=("arbitrary",) * len(grid),
                                             vmem_limit_bytes=V7X_VMEM_LIMIT),
    )(*args, *[a for _, a in rider])
    return list(res[:n_out]), list(res[n_out:])


def _adamw(g, w, m, v):
    m = ADAM_B1 * m + (1.0 - ADAM_B1) * g
    v = ADAM_B2 * v + (1.0 - ADAM_B2) * (g * g)
    m_hat = m / (1.0 - ADAM_B1 ** ADAM_STEP)
    v_hat = v / (1.0 - ADAM_B2 ** ADAM_STEP)
    delta = -ADAM_LR * (m_hat / (jnp.sqrt(v_hat) + ADAM_EPS) + ADAM_WD * w)
    return delta, m, v


def sum_adamw(parts, w, m, v, name, rider=()):
    n_l, rows, cols = w.shape
    n = parts[0].shape[0]
    mult = BF16_SUBLANES if parts[0].dtype == BF16 else 8
    tr = _row_tile(rows, 128, mult)
    n_i = rows // tr

    def body(*refs):
        part_refs = refs[:n_l]
        w_ref, m_ref, v_ref, g_out, d_out, m_out, v_out = refs[n_l:]
        layer = pl.program_id(0)
        for k in range(n_l):
            @pl.when(layer == k)
            def _(k=k):
                g = part_refs[k][0].astype(F32)
                for s in range(1, n):
                    g = g + part_refs[k][s].astype(F32)
                delta, m_new, v_new = _adamw(g, w_ref[...], m_ref[...], v_ref[...])
                g_out[...] = g
                d_out[...] = delta
                m_out[...] = m_new
                v_out[...] = v_new

    part_specs = [pl.BlockSpec((n, tr, cols), functools.partial(
        lambda l, i, k: (0, jnp.where(l == k, i, 0), 0), k=k)) for k in range(n_l)]
    wspec = pl.BlockSpec((None, tr, cols), lambda l, i: (l, i, 0))
    shape = jax.ShapeDtypeStruct(w.shape, F32)
    return _call(body, name, (n_l, n_i), part_specs + [wspec] * 3, (wspec,) * 4, (shape,) * 4,
                 (*parts, w, m, v), rider=rider)


def sum_slots(parts, name):
    n, rows, cols = parts.shape

    def body(p_ref, o_ref):
        acc = p_ref[0]
        for s in range(1, n):
            acc = acc + p_ref[s]
        o_ref[...] = acc

    return pl.pallas_call(
        body, name=name, out_shape=jax.ShapeDtypeStruct((rows, cols), F32),
        in_specs=[pl.BlockSpec(memory_space=pltpu.VMEM)],
        out_specs=pl.BlockSpec(memory_space=pltpu.VMEM),
    )(parts)


def norm_fwd(h, g, name):
    seq, d = h.shape
    tm = min(ROW_TILE, seq)

    def body(h_ref, g_ref, o_ref):
        o_ref[...] = _rms(h_ref[...], g_ref[...])[0].astype(BF16)

    return _call(body, name, (seq // tm,),
                 [pl.BlockSpec((tm, d), lambda i: (i, 0)), pl.BlockSpec((1, d), lambda i: (0, 0))],
                 [pl.BlockSpec((tm, d), lambda i: (i, 0))],
                 [jax.ShapeDtypeStruct((seq, d), BF16)], (h, g))[0][0]


def proj_residual(a, w, res, name, g_next=None, rider=()):
    nb, seq, kb = a.shape
    d = w.shape[-1]
    tm = min(ROW_TILE, seq)
    with_norm = g_next is not None

    def body(a_ref, w_ref, r_ref, *rest):
        acc = r_ref[...]
        for b in range(nb):
            acc = acc + _nn(a_ref[b], w_ref[b])
        if with_norm:
            g_ref, o_ref, xn_ref = rest
            xn_ref[...] = _rms(acc, g_ref[...])[0].astype(BF16)
        else:
            (o_ref,) = rest
        o_ref[...] = acc

    row = pl.BlockSpec((tm, d), lambda i: (i, 0))
    in_specs = [pl.BlockSpec((nb, tm, kb), lambda i: (0, i, 0)),
                pl.BlockSpec((nb, kb, d), lambda i: (0, 0, 0)), row]
    args = [a, w, res]
    out_specs, out_shape = [row], [jax.ShapeDtypeStruct((seq, d), F32)]
    if with_norm:
        in_specs.append(pl.BlockSpec((1, d), lambda i: (0, 0)))
        args.append(g_next)
        out_specs.append(row)
        out_shape.append(jax.ShapeDtypeStruct((seq, d), BF16))
    outs, r_outs = _call(body, name, (seq // tm,), in_specs, out_specs, out_shape, args, rider=rider)
    return (outs[0], outs[1] if with_norm else None), r_outs


def proj_t_rms_bwd(du, w, h, g, dres, name, rider=()):
    nb, seq, wd = du.shape
    k = w.shape[1]
    tm = min(ROW_TILE_SMALL, seq)

    def body(du_ref, w_ref, h_ref, g_ref, dr_ref, dh_ref, dg_ref):
        i = pl.program_id(0)
        dxn = _nt(du_ref[0], w_ref[0])
        for b in range(1, nb):
            dxn = dxn + _nt(du_ref[b], w_ref[b])
        _, xhat, rstd = _rms(h_ref[...], g_ref[...])
        dh, dg_rows = _rms_bwd(dxn, xhat, rstd, g_ref[...])
        dh_ref[...] = dr_ref[...] + dh

        @pl.when(i == 0)
        def _():
            dg_ref[...] = jnp.zeros_like(dg_ref)
        dg_ref[...] += jnp.sum(dg_rows, axis=0, keepdims=True)

    row = pl.BlockSpec((tm, k), lambda i: (i, 0))
    vec = pl.BlockSpec((1, k), lambda i: (0, 0))
    return _call(body, name, (seq // tm,),
                 [pl.BlockSpec((nb, tm, wd), lambda i: (0, i, 0)),
                  pl.BlockSpec((nb, k, wd), lambda i: (0, 0, 0)), row, vec, row],
                 (row, vec),
                 (jax.ShapeDtypeStruct((seq, k), F32), jax.ShapeDtypeStruct((1, k), F32)),
                 (du, w, h, g, dres), rider=rider)


def mixer_fwd(xn, win3, cw, name, rider=()):
    seq, d = xn.shape
    tm = min(ROW_TILE, seq)
    cc = min(MIXER_CHUNK, d)
    n_c, n_i = d // cc, seq // tm

    def body(x_ref, w_ref, cw_ref, u_ref, z_ref, carry):
        i = pl.program_id(1)

        @pl.when(i == 0)
        def _():
            carry[...] = jnp.zeros_like(carry)
        xb = x_ref[...]
        b = _nn(xb, w_ref[0])
        c = _nn(xb, w_ref[1])
        hh = _nn(xb, w_ref[2])
        p = c * hh
        w0, w1, w2 = _conv_taps(cw_ref)
        p1 = _shift_down(p, 1, [carry[7:8, :]])
        p2 = _shift_down(p, 2, [carry[6:7, :], carry[7:8, :]])
        q = w0 * p2 + w1 * p1 + w2 * p
        carry[...] = p[tm - 8:tm, :]
        u_ref[0] = b.astype(BF16)
        u_ref[1] = c.astype(BF16)
        u_ref[2] = hh.astype(BF16)
        u_ref[3] = q.astype(BF16)
        z_ref[...] = (b * q).astype(BF16)

    return _call(body, name, (n_c, n_i),
                 [pl.BlockSpec((tm, d), lambda c, i: (i, 0)),
                  pl.BlockSpec((3, d, cc), lambda c, i: (0, 0, c)),
                  pl.BlockSpec((3, cc), lambda c, i: (0, c))],
                 (pl.BlockSpec((4, tm, cc), lambda c, i: (0, i, c)),
                  pl.BlockSpec((tm, cc), lambda c, i: (i, c))),
                 (jax.ShapeDtypeStruct((4, seq, d), BF16), jax.ShapeDtypeStruct((seq, d), BF16)),
                 (xn, win3, cw), scratch=[pltpu.VMEM((8, cc), F32)], rider=rider)


def mixer_bwd(dh, wout, u4, z, xn, cw, name, rider=()):
    seq, d = xn.shape
    tm = min(ROW_TILE, seq)
    cc = min(MIXER_CHUNK, d)
    n_c, n_i = d // cc, seq // tm

    def body(dh_ref, wout_ref, u_ref, z_ref, x_ref, cw_ref,
             du_ref, dwin_ref, dwout_ref, dcw_ref, acc_in, acc_out, acc_cw, carry):
        i = pl.program_id(1)

        @pl.when(i == 0)
        def _():
            acc_in[...] = jnp.zeros_like(acc_in)
            acc_out[...] = jnp.zeros_like(acc_out)
            acc_cw[...] = jnp.zeros_like(acc_cw)
            carry[...] = jnp.zeros_like(carry)
        dhb = dh_ref[...].astype(BF16)
        dz = _nt(dhb, wout_ref[...])
        acc_out[...] += _tn(z_ref[...], dhb)
        b = u_ref[0].astype(F32)
        c = u_ref[1].astype(F32)
        hh = u_ref[2].astype(F32)
        q = u_ref[3].astype(F32)
        p = c * hh
        db = dz * q
        dq = dz * b
        w0, w1, w2 = _conv_taps(cw_ref)
        dq1 = _shift_up(dq, 1, [carry[0:1, :]])
        dq2 = _shift_up(dq, 2, [carry[0:1, :], carry[1:2, :]])
        dp = w2 * dq + w1 * dq1 + w0 * dq2
        carry[...] = dq[0:8, :]
        acc_cw[0:1, :] += jnp.sum(dq2 * p, axis=0, keepdims=True)
        acc_cw[1:2, :] += jnp.sum(dq1 * p, axis=0, keepdims=True)
        acc_cw[2:3, :] += jnp.sum(dq * p, axis=0, keepdims=True)
        dbb = db.astype(BF16)
        dcb = (dp * hh).astype(BF16)
        dhhb = (dp * c).astype(BF16)
        du_ref[0] = dbb
        du_ref[1] = dcb
        du_ref[2] = dhhb
        xb = x_ref[...]
        acc_in[0] += _tn(xb, dbb)
        acc_in[1] += _tn(xb, dcb)
        acc_in[2] += _tn(xb, dhhb)

        @pl.when(i == n_i - 1)
        def _():
            dwin_ref[...] = acc_in[...].astype(BF16)
            dwout_ref[...] = acc_out[...].astype(BF16)
            dcw_ref[...] = acc_cw[0:3, :]

    rev = lambda c, i: (n_i - 1 - i, 0)
    return _call(body, name, (n_c, n_i),
                 [pl.BlockSpec((tm, d), rev),
                  pl.BlockSpec((cc, d), lambda c, i: (c, 0)),
                  pl.BlockSpec((4, tm, cc), lambda c, i: (0, n_i - 1 - i, c)),
                  pl.BlockSpec((tm, cc), lambda c, i: (n_i - 1 - i, c)),
                  pl.BlockSpec((tm, d), rev),
                  pl.BlockSpec((3, cc), lambda c, i: (0, c))],
                 (pl.BlockSpec((3, tm, cc), lambda c, i: (0, n_i - 1 - i, c)),
                  pl.BlockSpec((3, d, cc), lambda c, i: (0, 0, c)),
                  pl.BlockSpec((cc, d), lambda c, i: (c, 0)),
                  pl.BlockSpec((3, cc), lambda c, i: (0, c))),
                 (jax.ShapeDtypeStruct((3, seq, d), BF16), jax.ShapeDtypeStruct((3, d, d), BF16),
                  jax.ShapeDtypeStruct((d, d), BF16), jax.ShapeDtypeStruct((3, d), F32)),
                 (dh, wout, u4, z, xn, cw),
                 scratch=[pltpu.VMEM((3, d, cc), F32), pltpu.VMEM((cc, d), F32),
                          pltpu.VMEM((8, cc), F32), pltpu.VMEM((8, cc), F32)], rider=rider)


def _silu_parts(cg):
    sg = 1.0 / (1.0 + jnp.exp(-cg))
    return sg, cg * sg


def ffn_fwd(xn, wup, fcw, name, rider=()):
    seq, d = xn.shape
    f8 = wup.shape[-1]
    half = N_DEV // 2
    tm = min(ROW_TILE, seq)
    n_i = seq // tm

    def body(x_ref, wg_ref, wu_ref, cg_ref, cu_ref, up_ref, cv_ref, a_ref, carry):
        i = pl.program_id(1)

        @pl.when(i == 0)
        def _():
            carry[...] = jnp.zeros_like(carry)
        xb = x_ref[...]
        conv = []
        for s, (w_ref, t_ref) in enumerate(((wg_ref, cg_ref), (wu_ref, cu_ref))):
            u = _nn(xb, w_ref[...])
            up_ref[s] = u.astype(BF16)
            w0, w1, w2 = _conv_taps(t_ref)
            u1 = _shift_down(u, 1, [carry[s, 7:8, :]])
            u2 = _shift_down(u, 2, [carry[s, 6:7, :], carry[s, 7:8, :]])
            cv = w0 * u2 + w1 * u1 + w2 * u
            cv_ref[s] = cv.astype(BF16)
            conv.append(cv)
            carry[s] = u[tm - 8:tm, :]
        _, silu = _silu_parts(conv[0])
        a_ref[...] = (silu * conv[1]).astype(BF16)

    blk = pl.BlockSpec((2, None, tm, f8), lambda c, i: (0, c, i, 0))
    big = jax.ShapeDtypeStruct((2, half, seq, f8), BF16)
    return _call(body, name, (half, n_i),
                 [pl.BlockSpec((tm, d), lambda c, i: (i, 0)),
                  pl.BlockSpec((None, d, f8), lambda c, i: (c, 0, 0)),
                  pl.BlockSpec((None, d, f8), lambda c, i: (c + half, 0, 0)),
                  pl.BlockSpec((None, 3, f8), lambda c, i: (c, 0, 0)),
                  pl.BlockSpec((None, 3, f8), lambda c, i: (c + half, 0, 0))],
                 (blk, blk, pl.BlockSpec((None, tm, f8), lambda c, i: (c, i, 0))),
                 (big, big, jax.ShapeDtypeStruct((half, seq, f8), BF16)),
                 (xn, wup, wup, fcw, fcw), scratch=[pltpu.VMEM((2, 8, f8), F32)], rider=rider)


def ffn_bwd(dh, wdown, up2, cv2, act, xn, fcw, name, rider=()):
    seq, d = xn.shape
    f8 = up2.shape[-1]
    fb = wdown.shape[1]
    half = N_DEV // 2
    tm = min(ROW_TILE, seq)
    n_i = seq // tm

    def body(dh_ref, wd_ref, up_ref, cv_ref, a_ref, x_ref, cg_ref, cu_ref,
             dup_ref, dwup_ref, dwd_ref, dcw_ref, acc_up, acc_down, acc_cw, carry):
        i = pl.program_id(1)

        @pl.when(i == 0)
        def _():
            acc_up[...] = jnp.zeros_like(acc_up)
            acc_down[...] = jnp.zeros_like(acc_down)
            acc_cw[...] = jnp.zeros_like(acc_cw)
            carry[...] = jnp.zeros_like(carry)
        dhb = dh_ref[...].astype(BF16)
        da = _nt(dhb, wd_ref[...])
        acc_down[...] += _tn(a_ref[...], dhb)
        cg = cv_ref[0].astype(F32)
        cu = cv_ref[1].astype(F32)
        sg, silu = _silu_parts(cg)
        dcg = da * cu * (sg * (1.0 + cg * (1.0 - sg)))
        dcu = da * silu
        xb = x_ref[...]
        for s, (dc, t_ref) in enumerate(((dcg, cg_ref), (dcu, cu_ref))):
            w0, w1, w2 = _conv_taps(t_ref)
            d1 = _shift_up(dc, 1, [carry[s, 0:1, :]])
            d2 = _shift_up(dc, 2, [carry[s, 0:1, :], carry[s, 1:2, :]])
            du = (w2 * dc + w1 * d1 + w0 * d2).astype(BF16)
            carry[s] = dc[0:8, :]
            u = up_ref[s].astype(F32)
            acc_cw[s, 0:1, :] += jnp.sum(d2 * u, axis=0, keepdims=True)
            acc_cw[s, 1:2, :] += jnp.sum(d1 * u, axis=0, keepdims=True)
            acc_cw[s, 2:3, :] += jnp.sum(dc * u, axis=0, keepdims=True)
            dup_ref[s] = du
            acc_up[s] += _tn(xb, du)

        @pl.when(i == n_i - 1)
        def _():
            dwup_ref[...] = acc_up[...].astype(BF16)
            dwd_ref[...] = acc_down[...].astype(BF16)
            dcw_ref[...] = acc_cw[:, 0:3, :]

    rev = lambda c, i: (n_i - 1 - i, 0)
    blk = pl.BlockSpec((2, None, tm, f8), lambda c, i: (0, c, n_i - 1 - i, 0))
    return _call(body, name, (half, n_i),
                 [pl.BlockSpec((tm, d), rev),
                  pl.BlockSpec((None, fb, d), lambda c, i: (c, 0, 0)),
                  blk, blk,
                  pl.BlockSpec((None, tm, f8), lambda c, i: (c, n_i - 1 - i, 0)),
                  pl.BlockSpec((tm, d), rev),
                  pl.BlockSpec((None, 3, f8), lambda c, i: (c, 0, 0)),
                  pl.BlockSpec((None, 3, f8), lambda c, i: (c + half, 0, 0))],
                 (blk,
                  pl.BlockSpec((2, None, d, f8), lambda c, i: (0, c, 0, 0)),
                  pl.BlockSpec((None, fb, d), lambda c, i: (c, 0, 0)),
                  pl.BlockSpec((2, None, 3, f8), lambda c, i: (0, c, 0, 0))),
                 (jax.ShapeDtypeStruct((2, half, seq, f8), BF16),
                  jax.ShapeDtypeStruct((2, half, d, f8), BF16),
                  jax.ShapeDtypeStruct((half, fb, d), BF16),
                  jax.ShapeDtypeStruct((2, half, 3, f8), F32)),
                 (dh, wdown, up2, cv2, act, xn, fcw, fcw),
                 scratch=[pltpu.VMEM((2, d, f8), F32), pltpu.VMEM((fb, d), F32),
                          pltpu.VMEM((2, 8, f8), F32), pltpu.VMEM((2, 8, f8), F32)], rider=rider)


def q_fwd(xn, wdq, gq, wuq, cos, sin, name, rider=()):
    seq, d = xn.shape
    rank = wdq.shape[-1]
    tm = min(ROW_TILE, seq)

    def body(x_ref, wdq_ref, gq_ref, wuq_ref, cos_ref, sin_ref, q_ref):
        qc = _nn(x_ref[...], wdq_ref[...])
        qn = _rms(qc, gq_ref[...])[0].astype(BF16)
        for hd in range(N_HEADS):
            qh = _nn(qn, wuq_ref[hd])
            qr = _rope_fwd(qh[:, NOPE:QK], cos_ref[...], sin_ref[...])
            q_ref[hd, :, 0:NOPE] = (qh[:, 0:NOPE] * ATTN_SCALE).astype(BF16)
            q_ref[hd, :, NOPE:QK] = (qr * ATTN_SCALE).astype(BF16)

    rope = pl.BlockSpec((tm, ROPE_PAD), lambda i: (i, 0))
    return _call(body, name, (seq // tm,),
                 [pl.BlockSpec((tm, d), lambda i: (i, 0)),
                  pl.BlockSpec((d, rank), lambda i: (0, 0)),
                  pl.BlockSpec((1, rank), lambda i: (0, 0)),
                  pl.BlockSpec((N_HEADS, rank, QK), lambda i: (0, 0, 0)), rope, rope],
                 [pl.BlockSpec((N_HEADS, tm, QK), lambda i: (0, i, 0))],
                 [jax.ShapeDtypeStruct((N_HEADS, seq, QK), BF16)],
                 (xn, wdq, gq, wuq, cos, sin), rider=rider)


def q_bwd(dq, xn, wdq, gq, wuq, cos, sin, name, rider=()):
    seq, d = xn.shape
    rank = wdq.shape[-1]
    tm = min(ROW_TILE_SMALL, seq)
    n_i = seq // tm

    def body(dq_ref, x_ref, wdq_ref, gq_ref, wuq_ref, cos_ref, sin_ref,
             dqc_ref, dwuq_ref, dwdq_ref, dgq_ref, acc_uq, acc_dq):
        i = pl.program_id(0)

        @pl.when(i == 0)
        def _():
            acc_uq[...] = jnp.zeros_like(acc_uq)
            acc_dq[...] = jnp.zeros_like(acc_dq)
            dgq_ref[...] = jnp.zeros_like(dgq_ref)
        xb = x_ref[...]
        qc = _nn(xb, wdq_ref[...])
        qn, qhat, qrstd = _rms(qc, gq_ref[...])
        qnb = qn.astype(BF16)
        dqn = jnp.zeros((tm, rank), F32)
        for hd in range(N_HEADS):
            dnope = (dq_ref[hd, :, 0:NOPE] * ATTN_SCALE).astype(BF16)
            drope = _rope_bwd(dq_ref[hd, :, NOPE:QK] * ATTN_SCALE, cos_ref[...], sin_ref[...])
            draw = jnp.concatenate([dnope, drope.astype(BF16)], axis=1)
            dqn = dqn + _nt(draw, wuq_ref[hd])
            acc_uq[hd] += _tn(qnb, draw)
        dqc, dg_rows = _rms_bwd(dqn, qhat, qrstd, gq_ref[...])
        dgq_ref[...] += jnp.sum(dg_rows, axis=0, keepdims=True)
        dqcb = dqc.astype(BF16)
        dqc_ref[0] = dqcb
        acc_dq[...] += _tn(xb, dqcb)

        @pl.when(i == n_i - 1)
        def _():
            dwuq_ref[...] = acc_uq[...].astype(BF16)
            dwdq_ref[...] = acc_dq[...].astype(BF16)

    rope = pl.BlockSpec((tm, ROPE_PAD), lambda i: (i, 0))
    return _call(body, name, (n_i,),
                 [pl.BlockSpec((N_HEADS, tm, QK), lambda i: (0, i, 0)),
                  pl.BlockSpec((tm, d), lambda i: (i, 0)),
                  pl.BlockSpec((d, rank), lambda i: (0, 0)),
                  pl.BlockSpec((1, rank), lambda i: (0, 0)),
                  pl.BlockSpec((N_HEADS, rank, QK), lambda i: (0, 0, 0)), rope, rope],
                 (pl.BlockSpec((1, tm, rank), lambda i: (0, i, 0)),
                  pl.BlockSpec((N_HEADS, rank, QK), lambda i: (0, 0, 0)),
                  pl.BlockSpec((d, rank), lambda i: (0, 0)),
                  pl.BlockSpec((1, rank), lambda i: (0, 0))),
                 (jax.ShapeDtypeStruct((1, seq, rank), BF16),
                  jax.ShapeDtypeStruct((N_HEADS, rank, QK), BF16),
                  jax.ShapeDtypeStruct((d, rank), BF16),
                  jax.ShapeDtypeStruct((1, rank), F32)),
                 (dq, xn, wdq, gq, wuq, cos, sin),
                 scratch=[pltpu.VMEM((N_HEADS, rank, QK), F32), pltpu.VMEM((d, rank), F32)],
                 rider=rider)


def kv_fwd(h, g, wdkv, gkv, wukv, cos, sin, name, rider=()):
    seq, d = h.shape
    tm = min(ROW_TILE, seq)
    wk = KV_RANK + ROPE_PAD

    def body(h_ref, g_ref, wdkv_ref, gkv_ref, wukv_ref, cos_ref, sin_ref, k_ref, v_ref, c_ref):
        xk = _rms(h_ref[...], g_ref[...])[0].astype(BF16)
        ckv = _nn(xk, wdkv_ref[...])
        c_kv = ckv[:, 0:KV_RANK]
        c_ref[...] = c_kv
        kr = _rope_fwd(ckv[:, KV_RANK:wk], cos_ref[...], sin_ref[...]).astype(BF16)
        ckn = _rms(c_kv, gkv_ref[...])[0].astype(BF16)
        for hd in range(N_HEADS):
            kvh = _nn(ckn, wukv_ref[hd])
            k_ref[hd, :, 0:NOPE] = kvh[:, 0:NOPE].astype(BF16)
            k_ref[hd, :, NOPE:QK] = kr
            v_ref[hd] = kvh[:, NOPE:NOPE + VDIM].astype(BF16)

    rope = pl.BlockSpec((tm, ROPE_PAD), lambda i: (i, 0))
    return _call(body, name, (seq // tm,),
                 [pl.BlockSpec((tm, d), lambda i: (i, 0)),
                  pl.BlockSpec((1, d), lambda i: (0, 0)),
                  pl.BlockSpec((d, wk), lambda i: (0, 0)),
                  pl.BlockSpec((1, KV_RANK), lambda i: (0, 0)),
                  pl.BlockSpec((N_HEADS, KV_RANK, NOPE + VDIM), lambda i: (0, 0, 0)), rope, rope],
                 (pl.BlockSpec((N_HEADS, tm, QK), lambda i: (0, i, 0)),
                  pl.BlockSpec((N_HEADS, tm, VDIM), lambda i: (0, i, 0)),
                  pl.BlockSpec((tm, KV_RANK), lambda i: (i, 0))),
                 (jax.ShapeDtypeStruct((N_HEADS, seq, QK), BF16),
                  jax.ShapeDtypeStruct((N_HEADS, seq, VDIM), BF16),
                  jax.ShapeDtypeStruct((seq, KV_RANK), F32)),
                 (h, g, wdkv, gkv, wukv, cos, sin), rider=rider)


def kv_bwd(dks, dvs, c_kv, h, g, gkv, wukv, cos, sin, name, rider=()):
    seq, d = h.shape
    tm = min(ROW_TILE_SMALL, seq)
    n_i = seq // tm
    wk = KV_RANK + ROPE_PAD
    n_b = len(dks)

    def body(*refs):
        dk_refs = refs[:n_b]
        dv_refs = refs[n_b:2 * n_b]
        (c_ref, h_ref, g_ref, gkv_ref, wukv_ref, cos_ref, sin_ref,
         dckv_ref, dwukv_ref, dwdkv_ref, dgkv_ref, acc_ukv, acc_dkv) = refs[2 * n_b:]
        i = pl.program_id(0)

        @pl.when(i == 0)
        def _():
            acc_ukv[...] = jnp.zeros_like(acc_ukv)
            acc_dkv[...] = jnp.zeros_like(acc_dkv)
            dgkv_ref[...] = jnp.zeros_like(dgkv_ref)
        ckn, chat, crstd = _rms(c_ref[...], gkv_ref[...])
        cknb = ckn.astype(BF16)
        dckn = jnp.zeros((tm, KV_RANK), F32)
        dkr = jnp.zeros((tm, ROPE_PAD), F32)
        for hd in range(N_HEADS):
            dk = dk_refs[0][hd]
            dv = dv_refs[0][hd]
            for j in range(1, n_b):
                dk = dk + dk_refs[j][hd]
                dv = dv + dv_refs[j][hd]
            dkr = dkr + dk[:, NOPE:QK]
            dkvh = jnp.concatenate([dk[:, 0:NOPE].astype(BF16), dv.astype(BF16)], axis=1)
            dckn = dckn + _nt(dkvh, wukv_ref[hd])
            acc_ukv[hd] += _tn(cknb, dkvh)
        dc_kv, dg_rows = _rms_bwd(dckn, chat, crstd, gkv_ref[...])
        dgkv_ref[...] += jnp.sum(dg_rows, axis=0, keepdims=True)
        dkr_raw = _rope_bwd(dkr, cos_ref[...], sin_ref[...])
        dckv = jnp.concatenate([dc_kv.astype(BF16), dkr_raw.astype(BF16)], axis=1)
        dckv_ref[0] = dckv
        xk = _rms(h_ref[...], g_ref[...])[0].astype(BF16)
        acc_dkv[...] += _tn(xk, dckv)

        @pl.when(i == n_i - 1)
        def _():
            dwukv_ref[...] = acc_ukv[...].astype(BF16)
            dwdkv_ref[...] = acc_dkv[...].astype(BF16)

    kspec = pl.BlockSpec((N_HEADS, tm, QK), lambda i: (0, i, 0))
    vspec = pl.BlockSpec((N_HEADS, tm, VDIM), lambda i: (0, i, 0))
    rope = pl.BlockSpec((tm, ROPE_PAD), lambda i: (i, 0))
    return _call(body, name, (n_i,),
                 [kspec] * n_b + [vspec] * n_b + [
                     pl.BlockSpec((tm, KV_RANK), lambda i: (i, 0)),
                     pl.BlockSpec((tm, d), lambda i: (i, 0)),
                     pl.BlockSpec((1, d), lambda i: (0, 0)),
                     pl.BlockSpec((1, KV_RANK), lambda i: (0, 0)),
                     pl.BlockSpec((N_HEADS, KV_RANK, NOPE + VDIM), lambda i: (0, 0, 0)), rope, rope],
                 (pl.BlockSpec((1, tm, wk), lambda i: (0, i, 0)),
                  pl.BlockSpec((N_HEADS, KV_RANK, NOPE + VDIM), lambda i: (0, 0, 0)),
                  pl.BlockSpec((d, wk), lambda i: (0, 0)),
                  pl.BlockSpec((1, KV_RANK), lambda i: (0, 0))),
                 (jax.ShapeDtypeStruct((1, seq, wk), BF16),
                  jax.ShapeDtypeStruct((N_HEADS, KV_RANK, NOPE + VDIM), BF16),
                  jax.ShapeDtypeStruct((d, wk), BF16),
                  jax.ShapeDtypeStruct((1, KV_RANK), F32)),
                 (*dks, *dvs, c_kv, h, g, gkv, wukv, cos, sin),
                 scratch=[pltpu.VMEM((N_HEADS, KV_RANK, NOPE + VDIM), F32), pltpu.VMEM((d, wk), F32)],
                 rider=rider)


def o_bwd(dh, o, wo, name, rider=()):
    seq, d = dh.shape
    hv = o.shape[1]
    tm = min(ROW_TILE, seq)
    n_i = seq // tm

    def body(dh_ref, o_ref, wo_ref, do_ref, dwo_ref, acc):
        i = pl.program_id(0)

        @pl.when(i == 0)
        def _():
            acc[...] = jnp.zeros_like(acc)
        dhb = dh_ref[...].astype(BF16)
        do_ref[...] = _nt(dhb, wo_ref[...]).astype(BF16)
        acc[...] += _tn(o_ref[...], dhb)

        @pl.when(i == n_i - 1)
        def _():
            dwo_ref[...] = acc[...].astype(BF16)

    return _call(body, name, (n_i,),
                 [pl.BlockSpec((tm, d), lambda i: (i, 0)),
                  pl.BlockSpec((tm, hv), lambda i: (i, 0)),
                  pl.BlockSpec((hv, d), lambda i: (0, 0))],
                 (pl.BlockSpec((tm, hv), lambda i: (i, 0)),
                  pl.BlockSpec((hv, d), lambda i: (0, 0))),
                 (jax.ShapeDtypeStruct((seq, hv), BF16), jax.ShapeDtypeStruct((hv, d), BF16)),
                 (dh, o, wo), scratch=[pltpu.VMEM((hv, d), F32)], rider=rider)


def _mask_diagonal(s):
    row = lax.broadcasted_iota(jnp.int32, s.shape, 0)
    col = lax.broadcasted_iota(jnp.int32, s.shape, 1)
    return jnp.where(col <= row, s, NEG_BIG)


def attn_fwd(q, k, v, name, rider=()):
    _, seq, _ = q.shape
    t = min(ATTN_TILE, seq // 2)
    n_pair = seq // (2 * t)

    def body(q_ref, k_ref, v_ref, o_ref, lse_ref):
        qi = pl.program_id(1)
        q_a = q_ref[0:t, :]
        q_b = q_ref[t:2 * t, :]

        def rows(j):
            return pl.ds(pl.multiple_of(j * t, t), t)

        def update(qx, kb, vb, state, diagonal=False):
            m, l, acc = state
            s = _nt(qx, kb)
            if diagonal:
                s = _mask_diagonal(s)
            m_new = jnp.maximum(m, jnp.max(s, axis=1, keepdims=True))
            p = jnp.exp(s - m_new)
            alpha = jnp.exp(m - m_new)
            l = alpha * l + jnp.sum(p, axis=1, keepdims=True)
            acc = alpha * acc + _nn(p.astype(BF16), vb)
            return m_new, l, acc

        def step(j, carry):
            kb, vb = k_ref[rows(j), :], v_ref[rows(j), :]
            return update(q_a, kb, vb, carry[0:3]) + update(q_b, kb, vb, carry[3:6])

        init = (jnp.full((t, 1), NEG_BIG, F32), jnp.zeros((t, 1), F32), jnp.zeros((t, VDIM), F32))
        carry = lax.fori_loop(0, 2 * qi, step, init + init)
        k0, v0 = k_ref[rows(2 * qi), :], v_ref[rows(2 * qi), :]
        k1, v1 = k_ref[rows(2 * qi + 1), :], v_ref[rows(2 * qi + 1), :]
        state_a = update(q_a, k0, v0, carry[0:3], diagonal=True)
        state_b = update(q_b, k1, v1, update(q_b, k0, v0, carry[3:6]), diagonal=True)
        for half, (m, l, acc) in enumerate((state_a, state_b)):
            o_ref[half * t:(half + 1) * t, :] = (acc / l).astype(BF16)
            lse_ref[half * t:(half + 1) * t, :] = jnp.broadcast_to(m + jnp.log(l), (t, LANES))

    return _call(body, name, (N_HEADS, n_pair),
                 [pl.BlockSpec((None, 2 * t, QK), lambda h, i: (h, i, 0)),
                  pl.BlockSpec((None, seq, QK), lambda h, i: (h, 0, 0)),
                  pl.BlockSpec((None, seq, VDIM), lambda h, i: (h, 0, 0))],
                 (pl.BlockSpec((2 * t, VDIM), lambda h, i: (i, h)),
                  pl.BlockSpec((None, 2 * t, LANES), lambda h, i: (h, i, 0))),
                 (jax.ShapeDtypeStruct((seq, N_HEADS * VDIM), BF16),
                  jax.ShapeDtypeStruct((N_HEADS, seq, LANES), F32)),
                 (q, k, v), rider=rider)


def attn_bwd(q, k, v, o, do, lse, name, rider=()):
    _, seq, _ = q.shape
    t = min(ATTN_TILE, seq // 2)
    n_q = seq // t
    n_pair = n_q // 2

    def body(q_ref, k_ref, v_ref, o_ref, do_ref, lse_ref, dq_ref, dk_ref, dv_ref):
        kj = pl.program_id(1)

        @pl.when(kj == 0)
        def _():
            dq_ref[...] = jnp.zeros_like(dq_ref)
        dk_ref[...] = jnp.zeros_like(dk_ref)
        dv_ref[...] = jnp.zeros_like(dv_ref)
        halves = (slice(0, t), slice(t, 2 * t))

        def block(i, masks):
            rows = pl.ds(pl.multiple_of(i * t, t), t)
            qb = q_ref[rows, :]
            dob = do_ref[rows, :]
            lse_col = lse_ref[rows, 0:1]
            delta = jnp.sum(dob.astype(F32) * o_ref[rows, :].astype(F32), axis=1, keepdims=True)
            dq = None
            for x, diagonal in enumerate(masks):
                if diagonal is None:
                    continue
                kb, vb = k_ref[halves[x], :], v_ref[halves[x], :]
                s = _nt(qb, kb)
                if diagonal:
                    s = _mask_diagonal(s)
                p = jnp.exp(s - lse_col)
                ds = (p * (_nt(dob, vb) - delta)).astype(BF16)
                dv_ref[halves[x], :] += _tn(p.astype(BF16), dob)
                dk_ref[halves[x], :] += _tn(ds, qb)
                part = _nn(ds, kb)
                dq = part if dq is None else dq + part
            dq_ref[rows, :] += dq

        block(2 * kj, (True, None))
        block(2 * kj + 1, (False, True))

        def step(i, carry):
            block(i, (False, False))
            return carry

        lax.fori_loop(2 * kj + 2, n_q, step, 0)

    head_rows = pl.BlockSpec((seq, VDIM), lambda h, j: (0, h))
    return _call(body, name, (N_HEADS, n_pair),
                 [pl.BlockSpec((None, seq, QK), lambda h, j: (h, 0, 0)),
                  pl.BlockSpec((None, 2 * t, QK), lambda h, j: (h, j, 0)),
                  pl.BlockSpec((None, 2 * t, VDIM), lambda h, j: (h, j, 0)),
                  head_rows, head_rows,
                  pl.BlockSpec((None, seq, LANES), lambda h, j: (h, 0, 0))],
                 (pl.BlockSpec((None, seq, QK), lambda h, j: (h, 0, 0)),
                  pl.BlockSpec((None, 2 * t, QK), lambda h, j: (h, j, 0)),
                  pl.BlockSpec((None, 2 * t, VDIM), lambda h, j: (h, j, 0))),
                 (jax.ShapeDtypeStruct((N_HEADS, seq, QK), F32),
                  jax.ShapeDtypeStruct((N_HEADS, seq, QK), F32),
                  jax.ShapeDtypeStruct((N_HEADS, seq, VDIM), F32)),
                 (q, k, v, o, do, lse), rider=rider)


def loss_head(h, g, target, name):
    seq, d = h.shape
    tm = min(ROW_TILE, seq)

    def body(h_ref, g_ref, t_ref, l_ref, dh_ref, dg_ref):
        i = pl.program_id(0)

        @pl.when(i == 0)
        def _():
            l_ref[...] = jnp.zeros_like(l_ref)
            dg_ref[...] = jnp.zeros_like(dg_ref)
        y, xhat, rstd = _rms(h_ref[...], g_ref[...])
        diff = y - t_ref[...]
        l_ref[...] += jnp.sum(jnp.sum(diff * diff, axis=1, keepdims=True), axis=0, keepdims=True)
        dh, dg_rows = _rms_bwd(diff * (1.0 / d), xhat, rstd, g_ref[...])
        dh_ref[...] = dh
        dg_ref[...] += jnp.sum(dg_rows, axis=0, keepdims=True)

    row = pl.BlockSpec((tm, d), lambda i: (i, 0))
    vec = pl.BlockSpec((1, d), lambda i: (0, 0))
    return _call(body, name, (seq // tm,), [row, vec, row],
                 (pl.BlockSpec((1, LANES), lambda i: (0, 0)), row, vec),
                 (jax.ShapeDtypeStruct((1, LANES), F32), jax.ShapeDtypeStruct((seq, d), F32),
                  jax.ShapeDtypeStruct((1, d), F32)),
                 (h, g, target))[0]


def _pack(parts):
    rows = []
    for p in parts:
        flat = p.reshape(-1)
        n_rows = -(-flat.shape[0] // (8 * LANES)) * 8
        flat = jnp.pad(flat, (0, n_rows * LANES - flat.shape[0]))
        rows.append(flat.reshape(n_rows, LANES))
    return jnp.concatenate(rows, axis=0)


def _unpack(packed, shapes):
    lead = packed.shape[:-2]
    out, r0 = [], 0
    for shape in shapes:
        size = 1
        for s in shape:
            size *= s
        n_rows = -(-size // (8 * LANES)) * 8
        part = packed[..., r0:r0 + n_rows, :].reshape(lead + (n_rows * LANES,))
        out.append(part[..., :size].reshape(lead + tuple(shape)))
        r0 += n_rows
    return out


FWD_RIDERS = {
    "mixer_fwd0": [("ffn_w_up", 0)],
    "mixer_out0": [("w_dkv", 0), ("w_ukv", 0), ("b_w_dq", 0), ("b_w_uq", 0)],
    "ffn_fwd0": [("ffn_w_down", 0), ("a_w_in", 1), ("a_w_out", 1)],
    "ffn_out0": [("ffn_w_down", 1)],
    "mixer_fwd1": [("ffn_w_up", 1)],
    "mixer_out1": [("b_w_o", 0)],
    "ffn_fwd1": [("ffn_w_up", 2)],
    "ffn_out1": [("ffn_w_down", 2)],
    "kv_fwd": [("b_w_dq", 1), ("b_w_uq", 1)],
    "q_fwd0": [("b_w_o", 1)],
    "attn_fwd0": [("ffn_w_up", 3), ("ffn_w_down", 3)],
}
BWD_RIDERS = {
    "ffn_in_bwd3": [("ffn_w_down", 3)],
    "attn_bwd1": [("ffn_w_up", 3), ("b_w_o", 1)],
    "q_in_bwd1": [("b_w_dq", 1)],
    "ffn_bwd2": [("b_w_uq", 1)],
    "ffn_in_bwd2": [("ffn_w_down", 2)],
    "attn_bwd0": [("ffn_w_up", 2), ("b_w_o", 0)],
    "q_in_bwd0": [("b_w_dq", 0)],
    "kv_bwd": [("b_w_uq", 0)],
    "kv_in_bwd": [("w_ukv", 0)],
    "ffn_bwd1": [("w_dkv", 0)],
    "ffn_in_bwd1": [("ffn_w_down", 1)],
    "mixer_in_bwd1": [("a_w_out", 1)],
    "ffn_bwd0": [("ffn_w_up", 1), ("a_w_in", 1)],
    "ffn_in_bwd0": [("ffn_w_down", 0)],
    "mixer_bwd0": [("ffn_w_up", 0)],
    "mixer_in_bwd0": [("a_w_out", 0)],
    "adamw_ffn_w_down": [("a_w_in", 0)],
}


def kernel(x, a_mix_norm, a_w_in, a_conv, a_w_out, b_mix_norm, b_w_dq, b_q_norm, b_w_uq, b_w_o, kv_in_norm, w_dkv, kv_norm, w_ukv, ffn_norm, ffn_w_up, ffn_conv, ffn_w_down, final_norm, loss_target, m_a_mix_norm, m_a_w_in, m_a_conv, m_a_w_out, m_b_mix_norm, m_b_w_dq, m_b_q_norm, m_b_w_uq, m_b_w_o, m_kv_in_norm, m_w_dkv, m_kv_norm, m_w_ukv, m_ffn_norm, m_ffn_w_up, m_ffn_conv, m_ffn_w_down, m_final_norm, v_a_mix_norm, v_a_w_in, v_a_conv, v_a_w_out, v_b_mix_norm, v_b_w_dq, v_b_q_norm, v_b_w_uq, v_b_w_o, v_kv_in_norm, v_w_dkv, v_kv_norm, v_w_ukv, v_ffn_norm, v_ffn_w_up, v_ffn_conv, v_ffn_w_down, v_final_norm):
    seq, d = x.shape[1], x.shape[2]
    me = 4 * lax.axis_index("x") + 2 * lax.axis_index("y") + lax.axis_index("c")
    h0 = x.reshape(seq, d)
    target = loss_target.reshape(seq, d)
    cos, sin = _rope_tables(seq)
    rank = b_w_dq.shape[-1]
    f8 = ffn_w_up.shape[-1]
    fd = ffn_w_down.shape[1]
    dshard = a_w_out.shape[1]
    hv = N_HEADS * VDIM

    shards = {"a_w_in": a_w_in, "a_w_out": a_w_out, "b_w_dq": b_w_dq, "b_w_uq": b_w_uq,
              "b_w_o": b_w_o, "w_dkv": w_dkv[None], "w_ukv": w_ukv[None],
              "ffn_w_up": ffn_w_up, "ffn_w_down": ffn_w_down}

    def relayout(name, g):
        if name == "a_w_in":
            w = jnp.transpose(g, (1, 0, 2)).reshape(d, 3, d)
            return jnp.transpose(w, (1, 0, 2))
        if name == "a_w_out":
            return g.reshape(d, d)
        if name == "b_w_dq":
            return g.reshape(d, rank)
        if name == "b_w_uq":
            return jnp.pad(g, ((0, 0), (0, 0), (0, QK - NOPE - ROPE)))
        if name == "b_w_o":
            return g.reshape(hv, d)
        if name == "w_dkv":
            return jnp.pad(g.reshape(d, KV_RANK + ROPE), ((0, 0), (0, ROPE_PAD - ROPE)))
        if name == "ffn_w_down":
            return g.reshape(N_DEV // 2, 2 * fd, d)
        return g

    weights = {}

    def ag_rider(host):
        return [("ag", shards[n][l].astype(BF16)) for n, l in FWD_RIDERS.get(host, [])]

    def ag_done(host, outs):
        for (n, l), g in zip(FWD_RIDERS.get(host, []), outs):
            weights[n, l] = relayout(n, g)

    small_shapes = [a_mix_norm.shape, a_conv.shape, ffn_conv.shape]
    first = exchange([("ag", a_w_in[0].astype(BF16)), ("ag", a_w_out[0].astype(BF16)),
                      ("ag", _pack([a_mix_norm, a_conv, ffn_conv]))], "ag_first")
    weights["a_w_in", 0] = relayout("a_w_in", first[0])
    weights["a_w_out", 0] = relayout("a_w_out", first[1])
    s_mix, s_aconv, s_fconv = _unpack(first[2], small_shapes)
    a_gain = jnp.transpose(s_mix, (1, 0, 2)).reshape(N_A, d)
    a_cw = jnp.transpose(s_aconv, (1, 2, 0, 3)).reshape(N_A, 3, d)
    f_cw = jnp.transpose(s_fconv, (1, 0, 2, 3))

    def mixer_gain(layer):
        if layer >= DEPTH:
            return None
        return a_gain[layer][None] if layer < N_A else b_mix_norm[layer - N_A][None]

    saved = {}
    h = h0
    xn = norm_fwd(h, mixer_gain(0), "norm_first")
    kv = None
    for layer in range(DEPTH):
        saved["hm", layer], saved["xm", layer] = h, xn
        if layer < N_A:
            name = f"mixer_fwd{layer}"
            (u4, z), r = mixer_fwd(xn, weights["a_w_in", layer], a_cw[layer], name, rider=ag_rider(name))
            ag_done(name, r)
            saved["mix", layer] = (u4, z)
            name = f"mixer_out{layer}"
            (h, xn), r = proj_residual(z[None], weights["a_w_out", layer][None], h, name,
                                       g_next=ffn_norm[layer][None], rider=ag_rider(name))
            ag_done(name, r)
        else:
            j = layer - N_A
            name = f"q_fwd{j}"
            (q,), r = q_fwd(xn, weights["b_w_dq", j], b_q_norm[j][None], weights["b_w_uq", j],
                            cos, sin, name, rider=ag_rider(name))
            ag_done(name, r)
            name = f"attn_fwd{j}"
            (o, lse), r = attn_fwd(q, kv[0], kv[1], name, rider=ag_rider(name))
            ag_done(name, r)
            saved["attn", layer] = (q, o, lse)
            name = f"attn_out{j}"
            (h, xn), r = proj_residual(o[None], weights["b_w_o", j][None], h, name,
                                       g_next=ffn_norm[layer][None], rider=ag_rider(name))
            ag_done(name, r)
        saved["hf", layer], saved["xf", layer] = h, xn
        name = f"ffn_fwd{layer}"
        (up2, cv2, act), r = ffn_fwd(xn, weights["ffn_w_up", layer], f_cw[layer], name, rider=ag_rider(name))
        ag_done(name, r)
        saved["ffn", layer] = (up2, cv2, act)
        name = f"ffn_out{layer}"
        (h, xn), r = proj_residual(act, weights["ffn_w_down", layer], h, name,
                                   g_next=mixer_gain(layer + 1), rider=ag_rider(name))
        ag_done(name, r)
        if layer == N_A - 1:
            (k_all, v_all, c_kv), r = kv_fwd(h, kv_in_norm[None], weights["w_dkv", 0], kv_norm[None],
                                             weights["w_ukv", 0], cos, sin, "kv_fwd",
                                             rider=ag_rider("kv_fwd"))
            ag_done("kv_fwd", r)
            kv = (k_all, v_all, c_kv)

    sq_err, dh, d_final = loss_head(h, final_norm[None], target, "loss_head")
    loss = lax.psum(sq_err[0, 0] * (0.5 / d), ("x", "y", "c"))

    grads = {}
    parts = {}

    def rs_rider(host):
        return [("rs", grads[key]) for key in BWD_RIDERS.get(host, [])]

    def rs_done(host, outs):
        for key, p in zip(BWD_RIDERS.get(host, []), outs):
            parts[key] = p

    d_ffn_norm = [None] * DEPTH
    d_fconv = [None] * DEPTH
    d_a_gain = [None] * N_A
    d_aconv = [None] * N_A
    d_b_gain = [None] * N_B
    d_q_gain = [None] * N_B
    dks, dvs = [], []
    for layer in reversed(range(DEPTH)):
        if layer == N_A - 1:
            hk = saved["hm", layer + 1]
            (dckv, dwukv, dwdkv, d_kv_gain), r = kv_bwd(
                dks, dvs, kv[2], hk, kv_in_norm[None], kv_norm[None], weights["w_ukv", 0],
                cos, sin, "kv_bwd", rider=rs_rider("kv_bwd"))
            rs_done("kv_bwd", r)
            grads["w_ukv", 0] = dwukv
            grads["w_dkv", 0] = dwdkv[:, :KV_RANK + ROPE].reshape(N_DEV, dshard, KV_RANK + ROPE)
            (dh, d_kvin_gain), r = proj_t_rms_bwd(dckv, weights["w_dkv", 0][None], hk, kv_in_norm[None],
                                                  dh, "kv_in_bwd", rider=rs_rider("kv_in_bwd"))
            rs_done("kv_in_bwd", r)
        up2, cv2, act = saved["ffn", layer]
        name = f"ffn_bwd{layer}"
        (dup2, dwup, dwdown, dcw), r = ffn_bwd(dh, weights["ffn_w_down", layer], up2, cv2, act,
                                               saved["xf", layer], f_cw[layer], name, rider=rs_rider(name))
        rs_done(name, r)
        grads["ffn_w_up", layer] = jnp.transpose(dwup.reshape(N_DEV, d, f8), (0, 2, 1))
        grads["ffn_w_down", layer] = dwdown.reshape(N_DEV, fd, d)
        d_fconv[layer] = dcw.reshape(N_DEV, 3, f8)
        name = f"ffn_in_bwd{layer}"
        (dh, d_ffn_norm[layer]), r = proj_t_rms_bwd(dup2.reshape(N_DEV, seq, f8), weights["ffn_w_up", layer],
                                                    saved["hf", layer], ffn_norm[layer][None], dh, name,
                                                    rider=rs_rider(name))
        rs_done(name, r)
        hm, xm = saved["hm", layer], saved["xm", layer]
        if layer < N_A:
            u4, z = saved["mix", layer]
            name = f"mixer_bwd{layer}"
            (du3, dwin3, dwout, dcw), r = mixer_bwd(dh, weights["a_w_out", layer], u4, z, xm, a_cw[layer],
                                                    name, rider=rs_rider(name))
            rs_done(name, r)
            dwin = jnp.transpose(dwin3, (1, 0, 2)).reshape(d, N_DEV, 3 * d // N_DEV)
            grads["a_w_in", layer] = jnp.transpose(dwin, (1, 0, 2))
            grads["a_w_out", layer] = dwout.reshape(N_DEV, dshard, d)
            d_aconv[layer] = dcw
            name = f"mixer_in_bwd{layer}"
            (dh, d_a_gain[layer]), r = proj_t_rms_bwd(du3, weights["a_w_in", layer], hm, a_gain[layer][None],
                                                      dh, name, rider=rs_rider(name))
            rs_done(name, r)
        else:
            j = layer - N_A
            q, o, lse = saved["attn", layer]
            name = f"attn_out_bwd{j}"
            (do, dwo), r = o_bwd(dh, o, weights["b_w_o", j], name, rider=rs_rider(name))
            rs_done(name, r)
            grads["b_w_o", j] = dwo.reshape(N_DEV, dshard, d)
            name = f"attn_bwd{j}"
            (dq, dk, dv), r = attn_bwd(q, kv[0], kv[1], o, do, lse, name, rider=rs_rider(name))
            rs_done(name, r)
            dks.append(dk)
            dvs.append(dv)
            name = f"q_bwd{j}"
            (dqc, dwuq, dwdq, d_q_gain[j]), r = q_bwd(dq, xm, weights["b_w_dq", j], b_q_norm[j][None],
                                                      weights["b_w_uq", j], cos, sin, name, rider=rs_rider(name))
            rs_done(name, r)
            grads["b_w_uq", j] = dwuq[:, :, :NOPE + ROPE]
            grads["b_w_dq", j] = dwdq.reshape(N_DEV, dshard, rank)
            name = f"q_in_bwd{j}"
            (dh, d_b_gain[j]), r = proj_t_rms_bwd(dqc, weights["b_w_dq", j][None], hm, b_mix_norm[j][None],
                                                  dh, name, rider=rs_rider(name))
            rs_done(name, r)
    grad_x = dh.reshape(x.shape)

    full_small = [
        jnp.concatenate(d_a_gain, axis=0),
        jnp.stack(d_aconv),
        jnp.concatenate(d_b_gain, axis=0),
        jnp.concatenate(d_q_gain, axis=0),
        d_kvin_gain[0],
        d_kv_gain[0],
        jnp.concatenate(d_ffn_norm, axis=0),
        jnp.stack(d_fconv),
        d_final[0],
    ]
    full_shapes = [t.shape for t in full_small]
    small_pack = _pack(full_small)

    res = {}

    def update(name, n_layers, w, m, v, extra=(), transposed=False):
        view = (lambda t: jnp.transpose(t, (0, 2, 1))) if transposed else (lambda t: t)
        shard = w.shape if w.ndim == 3 else (1,) + w.shape
        host = f"adamw_{name}"
        outs, r = sum_adamw([parts[name, l] for l in range(n_layers)], view(w.reshape(shard)),
                            view(m.reshape(shard)), view(v.reshape(shard)), host,
                            rider=rs_rider(host) + list(extra))
        rs_done(host, r)
        res[name] = [view(t).reshape(w.shape) for t in outs]
        return r[len(BWD_RIDERS.get(host, [])):]

    update("ffn_w_down", DEPTH, ffn_w_down, m_ffn_w_down, v_ffn_w_down)
    (g_parts,) = update("ffn_w_up", DEPTH, ffn_w_up, m_ffn_w_up, v_ffn_w_up,
                        extra=[("ag", small_pack)], transposed=True)
    update("a_w_out", N_A, a_w_out, m_a_w_out, v_a_w_out)
    update("b_w_dq", N_B, b_w_dq, m_b_w_dq, v_b_w_dq)
    update("b_w_uq", N_B, b_w_uq, m_b_w_uq, v_b_w_uq)
    update("b_w_o", N_B, b_w_o, m_b_w_o, v_b_w_o)
    update("w_dkv", 1, w_dkv, m_w_dkv, v_w_dkv)
    update("w_ukv", 1, w_ukv, m_w_ukv, v_w_ukv)
    update("a_w_in", N_A, a_w_in, m_a_w_in, v_a_w_in)

    summed = sum_slots(g_parts, "sum_small_grads")
    (s_a_gain, s_aconv_g, s_b_gain, s_q_gain, s_kvin, s_kvn, s_ffn_gain, s_fconv_g,
     s_final) = _unpack(summed, full_shapes)
    dsl = d // N_DEV
    small = [
        ("a_mix_norm", lax.dynamic_slice_in_dim(s_a_gain, me * dsl, dsl, axis=1), a_mix_norm, m_a_mix_norm, v_a_mix_norm),
        ("a_conv", lax.dynamic_slice_in_dim(s_aconv_g, me * dsl, dsl, axis=2), a_conv, m_a_conv, v_a_conv),
        ("b_mix_norm", s_b_gain, b_mix_norm, m_b_mix_norm, v_b_mix_norm),
        ("b_q_norm", s_q_gain, b_q_norm, m_b_q_norm, v_b_q_norm),
        ("kv_in_norm", s_kvin, kv_in_norm, m_kv_in_norm, v_kv_in_norm),
        ("kv_norm", s_kvn, kv_norm, m_kv_norm, v_kv_norm),
        ("ffn_norm", s_ffn_gain, ffn_norm, m_ffn_norm, v_ffn_norm),
        ("ffn_conv", lax.dynamic_index_in_dim(s_fconv_g, me, axis=1, keepdims=False), ffn_conv, m_ffn_conv, v_ffn_conv),
        ("final_norm", s_final, final_norm, m_final_norm, v_final_norm),
    ]
    shapes = [t[2].shape for t in small]
    packed = [_pack([t[k] for t in small])[None] for k in (1, 2, 3, 4)]
    outs, _ = sum_adamw([packed[0]], packed[1], packed[2], packed[3], "adamw_small")
    unpacked = [_unpack(t[0], shapes) for t in outs]
    for idx, t in enumerate(small):
        res[t[0]] = [unpacked[k][idx] for k in range(4)]

    order = ["a_mix_norm", "a_w_in", "a_conv", "a_w_out", "b_mix_norm", "b_w_dq", "b_q_norm",
             "b_w_uq", "b_w_o", "kv_in_norm", "w_dkv", "kv_norm", "w_ukv", "ffn_norm",
             "ffn_w_up", "ffn_conv", "ffn_w_down", "final_norm"]
    return (loss, grad_x, *[res[n][0] for n in order], *[res[n][1] for n in order],
            *[res[n][2] for n in order], *[res[n][3] for n in order])
```

```python
import functools

import jax
import jax.numpy as jnp
from jax import lax
from jax.experimental import pallas as pl
from jax.experimental.pallas import tpu as pltpu

F32 = jnp.float32
BF16 = jnp.bfloat16

N_DEV = 8
N_HEADS = 8
NOPE = 128
ROPE = 64
ROPE_PAD = 128
QK = NOPE + ROPE_PAD
VDIM = 128
KV_RANK = 256
ROPE_THETA = 10000.0
RMS_EPS = 1e-6
ATTN_SCALE = (NOPE + ROPE) ** -0.5
N_A = 2
N_B = 2
DEPTH = 4

ADAM_LR = 0.001
ADAM_B1 = 0.9
ADAM_B2 = 0.999
ADAM_EPS = 1e-08
ADAM_WD = 0.01
ADAM_STEP = 10

V7X_VMEM_LIMIT = 56 * 1024 * 1024
BF16_SUBLANES = 16
ROW_TILE = 512
ROW_TILE_SMALL = 256
ATTN_TILE = 512
MIXER_CHUNK = 512
LANES = 128
NEG_BIG = -1e30
COPIES_PER_TASK = 7

MESH_ID = pl.DeviceIdType.MESH
ANY = pl.BlockSpec(memory_space=pl.ANY)


def _nt(a, b):
    return lax.dot_general(a, b, (((1,), (1,)), ((), ())), preferred_element_type=F32)


def _tn(a, b):
    return lax.dot_general(a, b, (((0,), (0,)), ((), ())), preferred_element_type=F32)


def _nn(a, b):
    return jnp.dot(a, b, preferred_element_type=F32)


def _rms(h, g):
    rstd = lax.rsqrt(jnp.mean(h * h, axis=-1, keepdims=True) + RMS_EPS)
    xhat = h * rstd
    return xhat * g, xhat, rstd


def _rms_bwd(dxn, xhat, rstd, g):
    dxhat = dxn * g
    dh = rstd * (dxhat - xhat * jnp.mean(dxhat * xhat, axis=-1, keepdims=True))
    return dh, dxn * xhat


def _shift_down(x, k, halo_rows):
    r = pltpu.roll(x, k, 0)
    row = lax.broadcasted_iota(jnp.int32, x.shape, 0)
    for t in range(k):
        r = jnp.where(row == t, halo_rows[t], r)
    return r


def _shift_up(x, k, halo_rows):
    n = x.shape[0]
    r = pltpu.roll(x, n - k, 0)
    row = lax.broadcasted_iota(jnp.int32, x.shape, 0)
    for t in range(k):
        r = jnp.where(row == n - k + t, halo_rows[t], r)
    return r


def _conv_taps(w_ref):
    return w_ref[0:1, :], w_ref[1:2, :], w_ref[2:3, :]


def _rope_swap(x):
    lane = lax.broadcasted_iota(jnp.int32, x.shape, 1)
    return jnp.where(lane < ROPE // 2, pltpu.roll(x, ROPE_PAD - ROPE // 2, 1),
                     pltpu.roll(x, ROPE // 2, 1))


def _rope_fwd(x, cos, sin):
    return x * cos + _rope_swap(x) * sin


def _rope_bwd(dy, cos, sin):
    return dy * cos - _rope_swap(dy) * sin


def _rope_tables(seq):
    inv = 1.0 / (ROPE_THETA ** (jnp.arange(0, ROPE, 2, dtype=F32) / ROPE))
    ang = jnp.arange(seq, dtype=F32)[:, None] * inv[None, :]
    cos, sin = jnp.cos(ang), jnp.sin(ang)
    zero = jnp.zeros((seq, ROPE_PAD - ROPE), F32)
    return (jnp.concatenate([cos, cos, zero], axis=1),
            jnp.concatenate([-sin, sin, zero], axis=1))


def _row_tile(rows, cap, mult=8):
    best = None
    for t in range(mult, min(rows, cap) + 1, mult):
        if rows % t == 0:
            best = t
    return rows if best is None else best


class _AllGatherTask:
    def __init__(self, t, x_ref, out_ref, send_sems, recv_sems, local_sems):
        self.t, self.x_ref, self.out_ref = t, x_ref, out_ref
        self.send_sems, self.recv_sems, self.local_sems = send_sems, recv_sems, local_sems
        mx, my, mc = lax.axis_index("x"), lax.axis_index("y"), lax.axis_index("c")
        self.mc = mc
        self.me, self.sibling = (mx, my, mc), (mx, my, 1 - mc)
        self.chips = [(1 - mx, my), (mx, 1 - my), (1 - mx, 1 - my)]

    def _slot(self, px, py, pc):
        return self.out_ref.at[4 * px + 2 * py + pc]

    def _copy(self, k, block, to, src=None):
        s = COPIES_PER_TASK * self.t + k
        return pltpu.make_async_remote_copy(
            src_ref=self._slot(*block) if src is None else src, dst_ref=self._slot(*block),
            send_sem=self.send_sems.at[s], recv_sem=self.recv_sems.at[s],
            device_id=to, device_id_type=MESH_ID)

    def _mine(self):
        return pltpu.make_async_copy(self.x_ref, self._slot(*self.me), self.local_sems.at[self.t])

    def _first(self):
        out = [self._copy(0, self.me, self.sibling, src=self.x_ref)]
        out += [self._copy(1 + j, self.me, (*chip, self.mc), src=self.x_ref)
                for j, chip in enumerate(self.chips)]
        return out

    def _passed(self):
        return [self._copy(4 + j, (*chip, self.mc), self.sibling) for j, chip in enumerate(self.chips)]

    def start(self):
        self._mine().start()
        for cp in self._first():
            cp.start()

    def forward(self):
        passed = self._passed()
        for j, chip in enumerate(self.chips):
            self._copy(1 + j, (*chip, self.mc), self.me).wait_recv()
            passed[j].start()

    def finish(self):
        self._copy(0, self.sibling, self.me).wait_recv()
        for j, chip in enumerate(self.chips):
            self._copy(4 + j, (*chip, 1 - self.mc), self.me).wait_recv()
        for cp in self._first() + self._passed():
            cp.wait_send()
        self._mine().wait()


class _ReduceScatterTask:
    def __init__(self, t, g_ref, out_ref, send_sems, recv_sems, local_sems):
        self.t, self.g_ref, self.out_ref = t, g_ref, out_ref
        self.send_sems, self.recv_sems, self.local_sems = send_sems, recv_sems, local_sems
        mx, my, mc = lax.axis_index("x"), lax.axis_index("y"), lax.axis_index("c")
        self.me = 4 * mx + 2 * my + mc
        self.peers = []
        for k in range(1, N_DEV):
            px, py, pc = mx ^ ((k >> 2) & 1), my ^ ((k >> 1) & 1), mc ^ (k & 1)
            self.peers.append(((px, py, pc), 4 * px + 2 * py + pc))

    def _mine(self):
        return pltpu.make_async_copy(self.g_ref.at[self.me], self.out_ref.at[self.me],
                                     self.local_sems.at[self.t])

    def _copy(self, k, src_slot, dst_slot):
        s = COPIES_PER_TASK * self.t + k
        return pltpu.make_async_remote_copy(
            src_ref=self.g_ref.at[src_slot], dst_ref=self.out_ref.at[dst_slot],
            send_sem=self.send_sems.at[s], recv_sem=self.recv_sems.at[s],
            device_id=self.peers[k][0], device_id_type=MESH_ID)

    def start(self):
        self._mine().start()
        for k, (_, peer) in enumerate(self.peers):
            self._copy(k, peer, self.me).start()

    def forward(self):
        pass

    def finish(self):
        for k, (_, peer) in enumerate(self.peers):
            self._copy(k, self.me, peer).wait_recv()
        for k, (_, peer) in enumerate(self.peers):
            self._copy(k, peer, self.me).wait_send()
        self._mine().wait()


_TASKS = {"ag": _AllGatherTask, "rs": _ReduceScatterTask}


def _task_shape(kind, arr):
    shape = (N_DEV,) + arr.shape if kind == "ag" else arr.shape
    return jax.ShapeDtypeStruct(shape, arr.dtype)


def _sem_shapes(n_tasks):
    return [pltpu.SemaphoreType.DMA((COPIES_PER_TASK * n_tasks,)),
            pltpu.SemaphoreType.DMA((COPIES_PER_TASK * n_tasks,)),
            pltpu.SemaphoreType.DMA((n_tasks,))]


def _make_tasks(rider, in_refs, out_refs, sems):
    return [_TASKS[kind](t, in_refs[t], out_refs[t], *sems) for t, (kind, _) in enumerate(rider)]


def exchange(rider, name):
    n = len(rider)

    def body(*refs):
        tasks = _make_tasks(rider, refs[:n], refs[n:2 * n], refs[2 * n:])
        for task in tasks:
            task.start()
        for task in tasks:
            task.forward()
        for task in tasks:
            task.finish()

    return list(pl.pallas_call(
        body, name=name, out_shape=tuple(_task_shape(k, a) for k, a in rider),
        in_specs=[ANY] * n, out_specs=(ANY,) * n, scratch_shapes=_sem_shapes(n),
    )(*[a for _, a in rider]))


def _call(body, name, grid, in_specs, out_specs, out_shape, args, scratch=(), rider=()):
    in_specs, out_specs, out_shape = list(in_specs), tuple(out_specs), tuple(out_shape)
    n_in, n_out, n_scr, n_r = len(in_specs), len(out_specs), len(scratch), len(rider)
    if n_r:
        def kern(*refs):
            ins, r_in = refs[:n_in], refs[n_in:n_in + n_r]
            o0 = n_in + n_r
            outs, r_out = refs[o0:o0 + n_out], refs[o0 + n_out:o0 + n_out + n_r]
            s0 = o0 + n_out + n_r
            scr, sems = refs[s0:s0 + n_scr], refs[s0 + n_scr:]
            step = 0
            for a, n in enumerate(grid):
                step = step * n + pl.program_id(a)
            n_steps = 1
            for n in grid:
                n_steps *= n

            @pl.when(step == 0)
            def _():
                for task in _make_tasks(rider, r_in, r_out, sems):
                    task.start()
            body(*ins, *outs, *scr)

            @pl.when(step == (3 * (n_steps - 1)) // 4)
            def _():
                for task in _make_tasks(rider, r_in, r_out, sems):
                    task.forward()

            @pl.when(step == n_steps - 1)
            def _():
                for task in _make_tasks(rider, r_in, r_out, sems):
                    task.finish()
    else:
        kern = body
    res = pl.pallas_call(
        kern, name=name, grid=grid,
        in_specs=in_specs + [ANY] * n_r, out_specs=out_specs + (ANY,) * n_r,
        out_shape=out_shape + tuple(_task_shape(k, a) for k, a in rider),
        scratch_shapes=list(scratch) + (_sem_shapes(n_r) if n_r else []),
        compiler_params=pltpu.CompilerParams(dimension_semantics=("arbitrary",) * len(grid),
                                             vmem_limit_bytes=V7X_VMEM_LIMIT),
    )(*args, *[a for _, a in rider])
    return list(res[:n_out]), list(res[n_out:])


def _adamw(g, w, m, v):
    m = ADAM_B1 * m + (1.0 - ADAM_B1) * g
    v = ADAM_B2 * v + (1.0 - ADAM_B2) * (g * g)
    m_hat = m / (1.0 - ADAM_B1 ** ADAM_STEP)
    v_hat = v / (1.0 - ADAM_B2 ** ADAM_STEP)
    delta = -ADAM_LR * (m_hat / (jnp.sqrt(v_hat) + ADAM_EPS) + ADAM_WD * w)
    return delta, m, v


def sum_adamw(parts, w, m, v, name, rider=()):
    n_l, rows, cols = w.shape
    n = parts[0].shape[0]
    mult = BF16_SUBLANES if parts[0].dtype == BF16 else 8
    tr = _row_tile(rows, 128, mult)
    n_i = rows // tr

    def body(*refs):
        part_refs = refs[:n_l]
        w_ref, m_ref, v_ref, g_out, d_out, m_out, v_out = refs[n_l:]
        layer = pl.program_id(0)
        for k in range(n_l):
            @pl.when(layer == k)
            def _(k=k):
                g = part_refs[k][0].astype(F32)
                for s in range(1, n):
                    g = g + part_refs[k][s].astype(F32)
                delta, m_new, v_new = _adamw(g, w_ref[...], m_ref[...], v_ref[...])
                g_out[...] = g
                d_out[...] = delta
                m_out[...] = m_new
                v_out[...] = v_new

    part_specs = [pl.BlockSpec((n, tr, cols), functools.partial(
        lambda l, i, k: (0, jnp.where(l == k, i, 0), 0), k=k)) for k in range(n_l)]
    wspec = pl.BlockSpec((None, tr, cols), lambda l, i: (l, i, 0))
    shape = jax.ShapeDtypeStruct(w.shape, F32)
    return _call(body, name, (n_l, n_i), part_specs + [wspec] * 3, (wspec,) * 4, (shape,) * 4,
                 (*parts, w, m, v), rider=rider)


def sum_adamw_transposed(parts, w_t, m_t, v_t, name, rider=()):
    n_l, cols, rows = w_t.shape
    n = parts[0].shape[0]
    tr = LANES
    n_i = rows // tr
    starts = list(range(0, cols - LANES + 1, LANES))
    if starts[-1] + LANES < cols:
        starts.append(cols - LANES)

    def body(*refs):
        part_refs = refs[:n_l]
        w_ref, m_ref, v_ref, g_out, d_out, m_out, v_out = refs[n_l:]
        layer = pl.program_id(0)
        for k in range(n_l):
            @pl.when(layer == k)
            def _(k=k):
                for c0 in starts:
                    piece = pl.ds(c0, LANES)
                    g = part_refs[k][0, :, piece].astype(F32)
                    for s in range(1, n):
                        g = g + part_refs[k][s, :, piece].astype(F32)
                    g = g.T
                    delta, m_new, v_new = _adamw(g, w_ref[piece, :], m_ref[piece, :], v_ref[piece, :])
                    g_out[piece, :] = g
                    d_out[piece, :] = delta
                    m_out[piece, :] = m_new
                    v_out[piece, :] = v_new

    part_specs = [pl.BlockSpec((n, tr, cols), functools.partial(
        lambda l, i, k: (0, jnp.where(l == k, i, 0), 0), k=k)) for k in range(n_l)]
    wspec = pl.BlockSpec((None, cols, tr), lambda l, i: (l, 0, i))
    shape = jax.ShapeDtypeStruct(w_t.shape, F32)
    return _call(body, name, (n_l, n_i), part_specs + [wspec] * 3, (wspec,) * 4, (shape,) * 4,
                 (*parts, w_t, m_t, v_t), rider=rider)


def sum_slots(parts, name):
    n, rows, cols = parts.shape

    def body(p_ref, o_ref):
        acc = p_ref[0]
        for s in range(1, n):
            acc = acc + p_ref[s]
        o_ref[...] = acc

    return pl.pallas_call(
        body, name=name, out_shape=jax.ShapeDtypeStruct((rows, cols), F32),
        in_specs=[pl.BlockSpec(memory_space=pltpu.VMEM)],
        out_specs=pl.BlockSpec(memory_space=pltpu.VMEM),
    )(parts)


def norm_fwd(h, g, name):
    seq, d = h.shape
    tm = min(ROW_TILE, seq)

    def body(h_ref, g_ref, o_ref):
        o_ref[...] = _rms(h_ref[...], g_ref[...])[0].astype(BF16)

    return _call(body, name, (seq // tm,),
                 [pl.BlockSpec((tm, d), lambda i: (i, 0)), pl.BlockSpec((1, d), lambda i: (0, 0))],
                 [pl.BlockSpec((tm, d), lambda i: (i, 0))],
                 [jax.ShapeDtypeStruct((seq, d), BF16)], (h, g))[0][0]


def proj_residual(a, w, res, name, g_next=None, rider=()):
    nb, seq, kb = a.shape
    d = w.shape[-1]
    tm = min(ROW_TILE, seq)
    with_norm = g_next is not None

    def body(a_ref, w_ref, r_ref, *rest):
        acc = r_ref[...]
        for b in range(nb):
            acc = acc + _nn(a_ref[b], w_ref[b])
        if with_norm:
            g_ref, o_ref, xn_ref = rest
            xn_ref[...] = _rms(acc, g_ref[...])[0].astype(BF16)
        else:
            (o_ref,) = rest
        o_ref[...] = acc

    row = pl.BlockSpec((tm, d), lambda i: (i, 0))
    in_specs = [pl.BlockSpec((nb, tm, kb), lambda i: (0, i, 0)),
                pl.BlockSpec((nb, kb, d), lambda i: (0, 0, 0)), row]
    args = [a, w, res]
    out_specs, out_shape = [row], [jax.ShapeDtypeStruct((seq, d), F32)]
    if with_norm:
        in_specs.append(pl.BlockSpec((1, d), lambda i: (0, 0)))
        args.append(g_next)
        out_specs.append(row)
        out_shape.append(jax.ShapeDtypeStruct((seq, d), BF16))
    outs, r_outs = _call(body, name, (seq // tm,), in_specs, out_specs, out_shape, args, rider=rider)
    return (outs[0], outs[1] if with_norm else None), r_outs


def proj_t_rms_bwd(du, w, h, g, dres, name, rider=()):
    nb, seq, wd = du.shape
    k = w.shape[1]
    tm = min(ROW_TILE_SMALL, seq)

    def body(du_ref, w_ref, h_ref, g_ref, dr_ref, dh_ref, dg_ref):
        i = pl.program_id(0)
        dxn = _nt(du_ref[0], w_ref[0])
        for b in range(1, nb):
            dxn = dxn + _nt(du_ref[b], w_ref[b])
        _, xhat, rstd = _rms(h_ref[...], g_ref[...])
        dh, dg_rows = _rms_bwd(dxn, xhat, rstd, g_ref[...])
        dh_ref[...] = dr_ref[...] + dh

        @pl.when(i == 0)
        def _():
            dg_ref[...] = jnp.zeros_like(dg_ref)
        dg_ref[...] += jnp.sum(dg_rows, axis=0, keepdims=True)

    row = pl.BlockSpec((tm, k), lambda i: (i, 0))
    vec = pl.BlockSpec((1, k), lambda i: (0, 0))
    return _call(body, name, (seq // tm,),
                 [pl.BlockSpec((nb, tm, wd), lambda i: (0, i, 0)),
                  pl.BlockSpec((nb, k, wd), lambda i: (0, 0, 0)), row, vec, row],
                 (row, vec),
                 (jax.ShapeDtypeStruct((seq, k), F32), jax.ShapeDtypeStruct((1, k), F32)),
                 (du, w, h, g, dres), rider=rider)


def mixer_fwd(xn, win3, cw, name, rider=()):
    seq, d = xn.shape
    tm = min(ROW_TILE, seq)
    cc = min(MIXER_CHUNK, d)
    n_c, n_i = d // cc, seq // tm

    def body(x_ref, w_ref, cw_ref, u_ref, z_ref, carry):
        i = pl.program_id(1)

        @pl.when(i == 0)
        def _():
            carry[...] = jnp.zeros_like(carry)
        xb = x_ref[...]
        b = _nn(xb, w_ref[0])
        c = _nn(xb, w_ref[1])
        hh = _nn(xb, w_ref[2])
        p = c * hh
        w0, w1, w2 = _conv_taps(cw_ref)
        p1 = _shift_down(p, 1, [carry[7:8, :]])
        p2 = _shift_down(p, 2, [carry[6:7, :], carry[7:8, :]])
        q = w0 * p2 + w1 * p1 + w2 * p
        carry[...] = p[tm - 8:tm, :]
        u_ref[0] = b.astype(BF16)
        u_ref[1] = c.astype(BF16)
        u_ref[2] = hh.astype(BF16)
        u_ref[3] = q.astype(BF16)
        z_ref[...] = (b * q).astype(BF16)

    return _call(body, name, (n_c, n_i),
                 [pl.BlockSpec((tm, d), lambda c, i: (i, 0)),
                  pl.BlockSpec((3, d, cc), lambda c, i: (0, 0, c)),
                  pl.BlockSpec((3, cc), lambda c, i: (0, c))],
                 (pl.BlockSpec((4, tm, cc), lambda c, i: (0, i, c)),
                  pl.BlockSpec((tm, cc), lambda c, i: (i, c))),
                 (jax.ShapeDtypeStruct((4, seq, d), BF16), jax.ShapeDtypeStruct((seq, d), BF16)),
                 (xn, win3, cw), scratch=[pltpu.VMEM((8, cc), F32)], rider=rider)


def mixer_bwd(dh, wout, u4, z, xn, cw, name, rider=()):
    seq, d = xn.shape
    tm = min(ROW_TILE, seq)
    cc = min(MIXER_CHUNK, d)
    n_c, n_i = d // cc, seq // tm

    def body(dh_ref, wout_ref, u_ref, z_ref, x_ref, cw_ref,
             du_ref, dwin_ref, dwout_ref, dcw_ref, acc_in, acc_out, acc_cw, carry):
        i = pl.program_id(1)

        @pl.when(i == 0)
        def _():
            acc_in[...] = jnp.zeros_like(acc_in)
            acc_out[...] = jnp.zeros_like(acc_out)
            acc_cw[...] = jnp.zeros_like(acc_cw)
            carry[...] = jnp.zeros_like(carry)
        dhb = dh_ref[...].astype(BF16)
        dz = _nt(dhb, wout_ref[...])
        acc_out[...] += _tn(z_ref[...], dhb)
        b = u_ref[0].astype(F32)
        c = u_ref[1].astype(F32)
        hh = u_ref[2].astype(F32)
        q = u_ref[3].astype(F32)
        p = c * hh
        db = dz * q
        dq = dz * b
        w0, w1, w2 = _conv_taps(cw_ref)
        dq1 = _shift_up(dq, 1, [carry[0:1, :]])
        dq2 = _shift_up(dq, 2, [carry[0:1, :], carry[1:2, :]])
        dp = w2 * dq + w1 * dq1 + w0 * dq2
        carry[...] = dq[0:8, :]
        acc_cw[0:1, :] += jnp.sum(dq2 * p, axis=0, keepdims=True)
        acc_cw[1:2, :] += jnp.sum(dq1 * p, axis=0, keepdims=True)
        acc_cw[2:3, :] += jnp.sum(dq * p, axis=0, keepdims=True)
        dbb = db.astype(BF16)
        dcb = (dp * hh).astype(BF16)
        dhhb = (dp * c).astype(BF16)
        du_ref[0] = dbb
        du_ref[1] = dcb
        du_ref[2] = dhhb
        xb = x_ref[...]
        acc_in[0] += _tn(xb, dbb)
        acc_in[1] += _tn(xb, dcb)
        acc_in[2] += _tn(xb, dhhb)

        @pl.when(i == n_i - 1)
        def _():
            dwin_ref[...] = acc_in[...].astype(BF16)
            dwout_ref[...] = acc_out[...].astype(BF16)
            dcw_ref[...] = acc_cw[0:3, :]

    rev = lambda c, i: (n_i - 1 - i, 0)
    return _call(body, name, (n_c, n_i),
                 [pl.BlockSpec((tm, d), rev),
                  pl.BlockSpec((cc, d), lambda c, i: (c, 0)),
                  pl.BlockSpec((4, tm, cc), lambda c, i: (0, n_i - 1 - i, c)),
                  pl.BlockSpec((tm, cc), lambda c, i: (n_i - 1 - i, c)),
                  pl.BlockSpec((tm, d), rev),
                  pl.BlockSpec((3, cc), lambda c, i: (0, c))],
                 (pl.BlockSpec((3, tm, cc), lambda c, i: (0, n_i - 1 - i, c)),
                  pl.BlockSpec((3, d, cc), lambda c, i: (0, 0, c)),
                  pl.BlockSpec((cc, d), lambda c, i: (c, 0)),
                  pl.BlockSpec((3, cc), lambda c, i: (0, c))),
                 (jax.ShapeDtypeStruct((3, seq, d), BF16), jax.ShapeDtypeStruct((3, d, d), BF16),
                  jax.ShapeDtypeStruct((d, d), BF16), jax.ShapeDtypeStruct((3, d), F32)),
                 (dh, wout, u4, z, xn, cw),
                 scratch=[pltpu.VMEM((3, d, cc), F32), pltpu.VMEM((cc, d), F32),
                          pltpu.VMEM((8, cc), F32), pltpu.VMEM((8, cc), F32)], rider=rider)


def _silu_parts(cg):
    sg = 1.0 / (1.0 + jnp.exp(-cg))
    return sg, cg * sg


def ffn_fwd(xn, wup, fcw, name, rider=()):
    seq, d = xn.shape
    f8 = wup.shape[-1]
    half = N_DEV // 2
    tm = min(ROW_TILE, seq)
    n_i = seq // tm

    def body(x_ref, wg_ref, wu_ref, cg_ref, cu_ref, up_ref, cv_ref, a_ref, carry):
        i = pl.program_id(1)

        @pl.when(i == 0)
        def _():
            carry[...] = jnp.zeros_like(carry)
        xb = x_ref[...]
        conv = []
        for s, (w_ref, t_ref) in enumerate(((wg_ref, cg_ref), (wu_ref, cu_ref))):
            u = _nn(xb, w_ref[...])
            up_ref[s] = u.astype(BF16)
            w0, w1, w2 = _conv_taps(t_ref)
            u1 = _shift_down(u, 1, [carry[s, 7:8, :]])
            u2 = _shift_down(u, 2, [carry[s, 6:7, :], carry[s, 7:8, :]])
            cv = w0 * u2 + w1 * u1 + w2 * u
            cv_ref[s] = cv.astype(BF16)
            conv.append(cv)
            carry[s] = u[tm - 8:tm, :]
        _, silu = _silu_parts(conv[0])
        a_ref[...] = (silu * conv[1]).astype(BF16)

    blk = pl.BlockSpec((2, None, tm, f8), lambda c, i: (0, c, i, 0))
    big = jax.ShapeDtypeStruct((2, half, seq, f8), BF16)
    return _call(body, name, (half, n_i),
                 [pl.BlockSpec((tm, d), lambda c, i: (i, 0)),
                  pl.BlockSpec((None, d, f8), lambda c, i: (c, 0, 0)),
                  pl.BlockSpec((None, d, f8), lambda c, i: (c + half, 0, 0)),
                  pl.BlockSpec((None, 3, f8), lambda c, i: (c, 0, 0)),
                  pl.BlockSpec((None, 3, f8), lambda c, i: (c + half, 0, 0))],
                 (blk, blk, pl.BlockSpec((None, tm, f8), lambda c, i: (c, i, 0))),
                 (big, big, jax.ShapeDtypeStruct((half, seq, f8), BF16)),
                 (xn, wup, wup, fcw, fcw), scratch=[pltpu.VMEM((2, 8, f8), F32)], rider=rider)


def ffn_bwd(dh, wdown, up2, cv2, act, xn, fcw, name, rider=()):
    seq, d = xn.shape
    f8 = up2.shape[-1]
    fb = wdown.shape[1]
    half = N_DEV // 2
    tm = min(ROW_TILE, seq)
    n_i = seq // tm

    def body(dh_ref, wd_ref, up_ref, cv_ref, a_ref, x_ref, cg_ref, cu_ref,
             dup_ref, dwup_ref, dwd_ref, dcw_ref, acc_up, acc_down, acc_cw, carry):
        i = pl.program_id(1)

        @pl.when(i == 0)
        def _():
            acc_up[...] = jnp.zeros_like(acc_up)
            acc_down[...] = jnp.zeros_like(acc_down)
            acc_cw[...] = jnp.zeros_like(acc_cw)
            carry[...] = jnp.zeros_like(carry)
        dhb = dh_ref[...].astype(BF16)
        da = _nt(dhb, wd_ref[...])
        acc_down[...] += _tn(a_ref[...], dhb)
        cg = cv_ref[0].astype(F32)
        cu = cv_ref[1].astype(F32)
        sg, silu = _silu_parts(cg)
        dcg = da * cu * (sg + silu * (1.0 - sg))
        dcu = da * silu
        xb = x_ref[...]
        for s, (dc, t_ref) in enumerate(((dcg, cg_ref), (dcu, cu_ref))):
            w0, w1, w2 = _conv_taps(t_ref)
            d1 = _shift_up(dc, 1, [carry[s, 0:1, :]])
            d2 = _shift_up(dc, 2, [carry[s, 0:1, :], carry[s, 1:2, :]])
            du = (w2 * dc + w1 * d1 + w0 * d2).astype(BF16)
            carry[s] = dc[0:8, :]
            u = up_ref[s].astype(F32)
            acc_cw[s, 0:1, :] += jnp.sum(d2 * u, axis=0, keepdims=True)
            acc_cw[s, 1:2, :] += jnp.sum(d1 * u, axis=0, keepdims=True)
            acc_cw[s, 2:3, :] += jnp.sum(dc * u, axis=0, keepdims=True)
            dup_ref[s] = du
            acc_up[s] += _tn(xb, du)

        @pl.when(i == n_i - 1)
        def _():
            dwup_ref[...] = acc_up[...].astype(BF16)
            dwd_ref[...] = acc_down[...].astype(BF16)
            dcw_ref[...] = acc_cw[:, 0:3, :]

    rev = lambda c, i: (n_i - 1 - i, 0)
    blk = pl.BlockSpec((2, None, tm, f8), lambda c, i: (0, c, n_i - 1 - i, 0))
    return _call(body, name, (half, n_i),
                 [pl.BlockSpec((tm, d), rev),
                  pl.BlockSpec((None, fb, d), lambda c, i: (c, 0, 0)),
                  blk, blk,
                  pl.BlockSpec((None, tm, f8), lambda c, i: (c, n_i - 1 - i, 0)),
                  pl.BlockSpec((tm, d), rev),
                  pl.BlockSpec((None, 3, f8), lambda c, i: (c, 0, 0)),
                  pl.BlockSpec((None, 3, f8), lambda c, i: (c + half, 0, 0))],
                 (blk,
                  pl.BlockSpec((2, None, d, f8), lambda c, i: (0, c, 0, 0)),
                  pl.BlockSpec((None, fb, d), lambda c, i: (c, 0, 0)),
                  pl.BlockSpec((2, None, 3, f8), lambda c, i: (0, c, 0, 0))),
                 (jax.ShapeDtypeStruct((2, half, seq, f8), BF16),
                  jax.ShapeDtypeStruct((2, half, d, f8), BF16),
                  jax.ShapeDtypeStruct((half, fb, d), BF16),
                  jax.ShapeDtypeStruct((2, half, 3, f8), F32)),
                 (dh, wdown, up2, cv2, act, xn, fcw, fcw),
                 scratch=[pltpu.VMEM((2, d, f8), F32), pltpu.VMEM((fb, d), F32),
                          pltpu.VMEM((2, 8, f8), F32), pltpu.VMEM((2, 8, f8), F32)], rider=rider)


def q_fwd(xn, wdq, gq, wuq, cos, sin, name, rider=()):
    seq, d = xn.shape
    rank = wdq.shape[-1]
    tm = min(ROW_TILE, seq)

    def body(x_ref, wdq_ref, gq_ref, wuq_ref, cos_ref, sin_ref, q_ref):
        qc = _nn(x_ref[...], wdq_ref[...])
        qn = _rms(qc, gq_ref[...])[0].astype(BF16)
        for hd in range(N_HEADS):
            qh = _nn(qn, wuq_ref[hd])
            qr = _rope_fwd(qh[:, NOPE:QK], cos_ref[...], sin_ref[...])
            q_ref[hd, :, 0:NOPE] = (qh[:, 0:NOPE] * ATTN_SCALE).astype(BF16)
            q_ref[hd, :, NOPE:QK] = (qr * ATTN_SCALE).astype(BF16)

    rope = pl.BlockSpec((tm, ROPE_PAD), lambda i: (i, 0))
    return _call(body, name, (seq // tm,),
                 [pl.BlockSpec((tm, d), lambda i: (i, 0)),
                  pl.BlockSpec((d, rank), lambda i: (0, 0)),
                  pl.BlockSpec((1, rank), lambda i: (0, 0)),
                  pl.BlockSpec((N_HEADS, rank, QK), lambda i: (0, 0, 0)), rope, rope],
                 [pl.BlockSpec((N_HEADS, tm, QK), lambda i: (0, i, 0))],
                 [jax.ShapeDtypeStruct((N_HEADS, seq, QK), BF16)],
                 (xn, wdq, gq, wuq, cos, sin), rider=rider)


def q_bwd(dq, xn, wdq, gq, wuq, cos, sin, name, rider=()):
    seq, d = xn.shape
    rank = wdq.shape[-1]
    tm = min(ROW_TILE_SMALL, seq)
    n_i = seq // tm

    def body(dq_ref, x_ref, wdq_ref, gq_ref, wuq_ref, cos_ref, sin_ref,
             dqc_ref, dwuq_ref, dwdq_ref, dgq_ref, acc_uq, acc_dq):
        i = pl.program_id(0)

        @pl.when(i == 0)
        def _():
            acc_uq[...] = jnp.zeros_like(acc_uq)
            acc_dq[...] = jnp.zeros_like(acc_dq)
            dgq_ref[...] = jnp.zeros_like(dgq_ref)
        xb = x_ref[...]
        qc = _nn(xb, wdq_ref[...])
        qn, qhat, qrstd = _rms(qc, gq_ref[...])
        qnb = qn.astype(BF16)
        dqn = jnp.zeros((tm, rank), F32)
        for hd in range(N_HEADS):
            dnope = (dq_ref[hd, :, 0:NOPE] * ATTN_SCALE).astype(BF16)
            drope = _rope_bwd(dq_ref[hd, :, NOPE:QK] * ATTN_SCALE, cos_ref[...], sin_ref[...])
            draw = jnp.concatenate([dnope, drope.astype(BF16)], axis=1)
            dqn = dqn + _nt(draw, wuq_ref[hd])
            acc_uq[hd] += _tn(qnb, draw)
        dqc, dg_rows = _rms_bwd(dqn, qhat, qrstd, gq_ref[...])
        dgq_ref[...] += jnp.sum(dg_rows, axis=0, keepdims=True)
        dqcb = dqc.astype(BF16)
        dqc_ref[0] = dqcb
        acc_dq[...] += _tn(xb, dqcb)

        @pl.when(i == n_i - 1)
        def _():
            dwuq_ref[...] = acc_uq[...].astype(BF16)
            dwdq_ref[...] = acc_dq[...].astype(BF16)

    rope = pl.BlockSpec((tm, ROPE_PAD), lambda i: (i, 0))
    return _call(body, name, (n_i,),
                 [pl.BlockSpec((N_HEADS, tm, QK), lambda i: (0, i, 0)),
                  pl.BlockSpec((tm, d), lambda i: (i, 0)),
                  pl.BlockSpec((d, rank), lambda i: (0, 0)),
                  pl.BlockSpec((1, rank), lambda i: (0, 0)),
                  pl.BlockSpec((N_HEADS, rank, QK), lambda i: (0, 0, 0)), rope, rope],
                 (pl.BlockSpec((1, tm, rank), lambda i: (0, i, 0)),
                  pl.BlockSpec((N_HEADS, rank, QK), lambda i: (0, 0, 0)),
                  pl.BlockSpec((d, rank), lambda i: (0, 0)),
                  pl.BlockSpec((1, rank), lambda i: (0, 0))),
                 (jax.ShapeDtypeStruct((1, seq, rank), BF16),
                  jax.ShapeDtypeStruct((N_HEADS, rank, QK), BF16),
                  jax.ShapeDtypeStruct((d, rank), BF16),
                  jax.ShapeDtypeStruct((1, rank), F32)),
                 (dq, xn, wdq, gq, wuq, cos, sin),
                 scratch=[pltpu.VMEM((N_HEADS, rank, QK), F32), pltpu.VMEM((d, rank), F32)],
                 rider=rider)


def kv_fwd(h, g, wdkv, gkv, wukv, cos, sin, name, rider=()):
    seq, d = h.shape
    tm = min(ROW_TILE, seq)
    wk = KV_RANK + ROPE_PAD

    def body(h_ref, g_ref, wdkv_ref, gkv_ref, wukv_ref, cos_ref, sin_ref, k_ref, v_ref, c_ref):
        xk = _rms(h_ref[...], g_ref[...])[0].astype(BF16)
        ckv = _nn(xk, wdkv_ref[...])
        c_kv = ckv[:, 0:KV_RANK]
        c_ref[...] = c_kv
        kr = _rope_fwd(ckv[:, KV_RANK:wk], cos_ref[...], sin_ref[...]).astype(BF16)
        ckn = _rms(c_kv, gkv_ref[...])[0].astype(BF16)
        for hd in range(N_HEADS):
            kvh = _nn(ckn, wukv_ref[hd])
            k_ref[hd, :, 0:NOPE] = kvh[:, 0:NOPE].astype(BF16)
            k_ref[hd, :, NOPE:QK] = kr
            v_ref[hd] = kvh[:, NOPE:NOPE + VDIM].astype(BF16)

    rope = pl.BlockSpec((tm, ROPE_PAD), lambda i: (i, 0))
    return _call(body, name, (seq // tm,),
                 [pl.BlockSpec((tm, d), lambda i: (i, 0)),
                  pl.BlockSpec((1, d), lambda i: (0, 0)),
                  pl.BlockSpec((d, wk), lambda i: (0, 0)),
                  pl.BlockSpec((1, KV_RANK), lambda i: (0, 0)),
                  pl.BlockSpec((N_HEADS, KV_RANK, NOPE + VDIM), lambda i: (0, 0, 0)), rope, rope],
                 (pl.BlockSpec((N_HEADS, tm, QK), lambda i: (0, i, 0)),
                  pl.BlockSpec((N_HEADS, tm, VDIM), lambda i: (0, i, 0)),
                  pl.BlockSpec((tm, KV_RANK), lambda i: (i, 0))),
                 (jax.ShapeDtypeStruct((N_HEADS, seq, QK), BF16),
                  jax.ShapeDtypeStruct((N_HEADS, seq, VDIM), BF16),
                  jax.ShapeDtypeStruct((seq, KV_RANK), F32)),
                 (h, g, wdkv, gkv, wukv, cos, sin), rider=rider)


def kv_bwd(dks, dvs, c_kv, h, g, gkv, wukv, cos, sin, name, rider=()):
    seq, d = h.shape
    tm = min(ROW_TILE_SMALL, seq)
    n_i = seq // tm
    wk = KV_RANK + ROPE_PAD
    n_b = len(dks)

    def body(*refs):
        dk_refs = refs[:n_b]
        dv_refs = refs[n_b:2 * n_b]
        (c_ref, h_ref, g_ref, gkv_ref, wukv_ref, cos_ref, sin_ref,
         dckv_ref, dwukv_ref, dwdkv_ref, dgkv_ref, acc_ukv, acc_dkv) = refs[2 * n_b:]
        i = pl.program_id(0)

        @pl.when(i == 0)
        def _():
            acc_ukv[...] = jnp.zeros_like(acc_ukv)
            acc_dkv[...] = jnp.zeros_like(acc_dkv)
            dgkv_ref[...] = jnp.zeros_like(dgkv_ref)
        ckn, chat, crstd = _rms(c_ref[...], gkv_ref[...])
        cknb = ckn.astype(BF16)
        dckn = jnp.zeros((tm, KV_RANK), F32)
        dkr = jnp.zeros((tm, ROPE_PAD), F32)
        for hd in range(N_HEADS):
            dk = dk_refs[0][hd]
            dv = dv_refs[0][hd]
            for j in range(1, n_b):
                dk = dk + dk_refs[j][hd]
                dv = dv + dv_refs[j][hd]
            dkr = dkr + dk[:, NOPE:QK]
            dkvh = jnp.concatenate([dk[:, 0:NOPE].astype(BF16), dv.astype(BF16)], axis=1)
            dckn = dckn + _nt(dkvh, wukv_ref[hd])
            acc_ukv[hd] += _tn(cknb, dkvh)
        dc_kv, dg_rows = _rms_bwd(dckn, chat, crstd, gkv_ref[...])
        dgkv_ref[...] += jnp.sum(dg_rows, axis=0, keepdims=True)
        dkr_raw = _rope_bwd(dkr, cos_ref[...], sin_ref[...])
        dckv = jnp.concatenate([dc_kv.astype(BF16), dkr_raw.astype(BF16)], axis=1)
        dckv_ref[0] = dckv
        xk = _rms(h_ref[...], g_ref[...])[0].astype(BF16)
        acc_dkv[...] += _tn(xk, dckv)

        @pl.when(i == n_i - 1)
        def _():
            dwukv_ref[...] = acc_ukv[...].astype(BF16)
            dwdkv_ref[...] = acc_dkv[...].astype(BF16)

    kspec = pl.BlockSpec((N_HEADS, tm, QK), lambda i: (0, i, 0))
    vspec = pl.BlockSpec((N_HEADS, tm, VDIM), lambda i: (0, i, 0))
    rope = pl.BlockSpec((tm, ROPE_PAD), lambda i: (i, 0))
    return _call(body, name, (n_i,),
                 [kspec] * n_b + [vspec] * n_b + [
                     pl.BlockSpec((tm, KV_RANK), lambda i: (i, 0)),
                     pl.BlockSpec((tm, d), lambda i: (i, 0)),
                     pl.BlockSpec((1, d), lambda i: (0, 0)),
                     pl.BlockSpec((1, KV_RANK), lambda i: (0, 0)),
                     pl.BlockSpec((N_HEADS, KV_RANK, NOPE + VDIM), lambda i: (0, 0, 0)), rope, rope],
                 (pl.BlockSpec((1, tm, wk), lambda i: (0, i, 0)),
                  pl.BlockSpec((N_HEADS, KV_RANK, NOPE + VDIM), lambda i: (0, 0, 0)),
                  pl.BlockSpec((d, wk), lambda i: (0, 0)),
                  pl.BlockSpec((1, KV_RANK), lambda i: (0, 0))),
                 (jax.ShapeDtypeStruct((1, seq, wk), BF16),
                  jax.ShapeDtypeStruct((N_HEADS, KV_RANK, NOPE + VDIM), BF16),
                  jax.ShapeDtypeStruct((d, wk), BF16),
                  jax.ShapeDtypeStruct((1, KV_RANK), F32)),
                 (*dks, *dvs, c_kv, h, g, gkv, wukv, cos, sin),
                 scratch=[pltpu.VMEM((N_HEADS, KV_RANK, NOPE + VDIM), F32), pltpu.VMEM((d, wk), F32)],
                 rider=rider)


def o_bwd(dh, o, wo, name, rider=()):
    seq, d = dh.shape
    hv = o.shape[1]
    tm = min(ROW_TILE, seq)
    n_i = seq // tm

    def body(dh_ref, o_ref, wo_ref, do_ref, dwo_ref, acc):
        i = pl.program_id(0)

        @pl.when(i == 0)
        def _():
            acc[...] = jnp.zeros_like(acc)
        dhb = dh_ref[...].astype(BF16)
        do_ref[...] = _nt(dhb, wo_ref[...]).astype(BF16)
        acc[...] += _tn(o_ref[...], dhb)

        @pl.when(i == n_i - 1)
        def _():
            dwo_ref[...] = acc[...].astype(BF16)

    return _call(body, name, (n_i,),
                 [pl.BlockSpec((tm, d), lambda i: (i, 0)),
                  pl.BlockSpec((tm, hv), lambda i: (i, 0)),
                  pl.BlockSpec((hv, d), lambda i: (0, 0))],
                 (pl.BlockSpec((tm, hv), lambda i: (i, 0)),
                  pl.BlockSpec((hv, d), lambda i: (0, 0))),
                 (jax.ShapeDtypeStruct((seq, hv), BF16), jax.ShapeDtypeStruct((hv, d), BF16)),
                 (dh, o, wo), scratch=[pltpu.VMEM((hv, d), F32)], rider=rider)


def _mask_diagonal(s):
    row = lax.broadcasted_iota(jnp.int32, s.shape, 0)
    col = lax.broadcasted_iota(jnp.int32, s.shape, 1)
    return jnp.where(col <= row, s, NEG_BIG)


def attn_fwd(q, k, v, name, rider=()):
    _, seq, _ = q.shape
    t = min(ATTN_TILE, seq // 2)
    n_pair = seq // (2 * t)

    def body(q_ref, k_ref, v_ref, o_ref, lse_ref):
        qi = pl.program_id(1)
        q_a = q_ref[0:t, :]
        q_b = q_ref[t:2 * t, :]

        def rows(j):
            return pl.ds(pl.multiple_of(j * t, t), t)

        def update(qx, kb, vb, state, diagonal=False):
            m, l, acc = state
            s = _nt(qx, kb)
            if diagonal:
                s = _mask_diagonal(s)
            m_new = jnp.maximum(m, jnp.max(s, axis=1, keepdims=True))
            p = jnp.exp(s - m_new)
            alpha = jnp.exp(m - m_new)
            l = alpha * l + jnp.sum(p, axis=1, keepdims=True)
            acc = alpha * acc + _nn(p.astype(BF16), vb)
            return m_new, l, acc

        def step(j, carry):
            kb, vb = k_ref[rows(j), :], v_ref[rows(j), :]
            return update(q_a, kb, vb, carry[0:3]) + update(q_b, kb, vb, carry[3:6])

        init = (jnp.full((t, 1), NEG_BIG, F32), jnp.zeros((t, 1), F32), jnp.zeros((t, VDIM), F32))
        carry = lax.fori_loop(0, 2 * qi, step, init + init)
        k0, v0 = k_ref[rows(2 * qi), :], v_ref[rows(2 * qi), :]
        k1, v1 = k_ref[rows(2 * qi + 1), :], v_ref[rows(2 * qi + 1), :]
        state_a = update(q_a, k0, v0, carry[0:3], diagonal=True)
        state_b = update(q_b, k1, v1, update(q_b, k0, v0, carry[3:6]), diagonal=True)
        for half, (m, l, acc) in enumerate((state_a, state_b)):
            o_ref[half * t:(half + 1) * t, :] = (acc / l).astype(BF16)
            lse_ref[half * t:(half + 1) * t, :] = jnp.broadcast_to(m + jnp.log(l), (t, LANES))

    return _call(body, name, (N_HEADS, n_pair),
                 [pl.BlockSpec((None, 2 * t, QK), lambda h, i: (h, i, 0)),
                  pl.BlockSpec((None, seq, QK), lambda h, i: (h, 0, 0)),
                  pl.BlockSpec((None, seq, VDIM), lambda h, i: (h, 0, 0))],
                 (pl.BlockSpec((2 * t, VDIM), lambda h, i: (i, h)),
                  pl.BlockSpec((None, 2 * t, LANES), lambda h, i: (h, i, 0))),
                 (jax.ShapeDtypeStruct((seq, N_HEADS * VDIM), BF16),
                  jax.ShapeDtypeStruct((N_HEADS, seq, LANES), F32)),
                 (q, k, v), rider=rider)


def attn_bwd(q, k, v, o, do, lse, name, rider=()):
    _, seq, _ = q.shape
    t = min(ATTN_TILE, seq // 2)
    n_q = seq // t
    n_pair = n_q // 2

    def body(q_ref, k_ref, v_ref, o_ref, do_ref, lse_ref, dq_ref, dk_ref, dv_ref):
        kj = pl.program_id(1)

        @pl.when(kj == 0)
        def _():
            dq_ref[...] = jnp.zeros_like(dq_ref)
        dk_ref[...] = jnp.zeros_like(dk_ref)
        dv_ref[...] = jnp.zeros_like(dv_ref)
        halves = (slice(0, t), slice(t, 2 * t))

        def block(i, masks):
            rows = pl.ds(pl.multiple_of(i * t, t), t)
            qb = q_ref[rows, :]
            dob = do_ref[rows, :]
            lse_col = lse_ref[rows, 0:1]
            delta = jnp.sum(dob.astype(F32) * o_ref[rows, :].astype(F32), axis=1, keepdims=True)
            dq = None
            for x, diagonal in enumerate(masks):
                if diagonal is None:
                    continue
                kb, vb = k_ref[halves[x], :], v_ref[halves[x], :]
                s = _nt(qb, kb)
                if diagonal:
                    s = _mask_diagonal(s)
                p = jnp.exp(s - lse_col)
                ds = (p * (_nt(dob, vb) - delta)).astype(BF16)
                dv_ref[halves[x], :] += _tn(p.astype(BF16), dob)
                dk_ref[halves[x], :] += _tn(ds, qb)
                part = _nn(ds, kb)
                dq = part if dq is None else dq + part
            dq_ref[rows, :] += dq

        block(2 * kj, (True, None))
        block(2 * kj + 1, (False, True))

        def step(i, carry):
            block(i, (False, False))
            return carry

        lax.fori_loop(2 * kj + 2, n_q, step, 0)

    head_rows = pl.BlockSpec((seq, VDIM), lambda h, j: (0, h))
    return _call(body, name, (N_HEADS, n_pair),
                 [pl.BlockSpec((None, seq, QK), lambda h, j: (h, 0, 0)),
                  pl.BlockSpec((None, 2 * t, QK), lambda h, j: (h, j, 0)),
                  pl.BlockSpec((None, 2 * t, VDIM), lambda h, j: (h, j, 0)),
                  head_rows, head_rows,
                  pl.BlockSpec((None, seq, LANES), lambda h, j: (h, 0, 0))],
                 (pl.BlockSpec((None, seq, QK), lambda h, j: (h, 0, 0)),
                  pl.BlockSpec((None, 2 * t, QK), lambda h, j: (h, j, 0)),
                  pl.BlockSpec((None, 2 * t, VDIM), lambda h, j: (h, j, 0))),
                 (jax.ShapeDtypeStruct((N_HEADS, seq, QK), F32),
                  jax.ShapeDtypeStruct((N_HEADS, seq, QK), F32),
                  jax.ShapeDtypeStruct((N_HEADS, seq, VDIM), F32)),
                 (q, k, v, o, do, lse), rider=rider)


def loss_head(h, g, target, name):
    seq, d = h.shape
    tm = min(ROW_TILE, seq)

    def body(h_ref, g_ref, t_ref, l_ref, dh_ref, dg_ref):
        i = pl.program_id(0)

        @pl.when(i == 0)
        def _():
            l_ref[...] = jnp.zeros_like(l_ref)
            dg_ref[...] = jnp.zeros_like(dg_ref)
        y, xhat, rstd = _rms(h_ref[...], g_ref[...])
        diff = y - t_ref[...]
        l_ref[...] += jnp.sum(jnp.sum(diff * diff, axis=1, keepdims=True), axis=0, keepdims=True)
        dh, dg_rows = _rms_bwd(diff * (1.0 / d), xhat, rstd, g_ref[...])
        dh_ref[...] = dh
        dg_ref[...] += jnp.sum(dg_rows, axis=0, keepdims=True)

    row = pl.BlockSpec((tm, d), lambda i: (i, 0))
    vec = pl.BlockSpec((1, d), lambda i: (0, 0))
    return _call(body, name, (seq // tm,), [row, vec, row],
                 (pl.BlockSpec((1, LANES), lambda i: (0, 0)), row, vec),
                 (jax.ShapeDtypeStruct((1, LANES), F32), jax.ShapeDtypeStruct((seq, d), F32),
                  jax.ShapeDtypeStruct((1, d), F32)),
                 (h, g, target))[0]


def _pack(parts):
    rows = []
    for p in parts:
        flat = p.reshape(-1)
        n_rows = -(-flat.shape[0] // (8 * LANES)) * 8
        flat = jnp.pad(flat, (0, n_rows * LANES - flat.shape[0]))
        rows.append(flat.reshape(n_rows, LANES))
    return jnp.concatenate(rows, axis=0)


def _unpack(packed, shapes):
    lead = packed.shape[:-2]
    out, r0 = [], 0
    for shape in shapes:
        size = 1
        for s in shape:
            size *= s
        n_rows = -(-size // (8 * LANES)) * 8
        part = packed[..., r0:r0 + n_rows, :].reshape(lead + (n_rows * LANES,))
        out.append(part[..., :size].reshape(lead + tuple(shape)))
        r0 += n_rows
    return out


FWD_RIDERS = {
    "mixer_fwd0": [("ffn_w_up", 0)],
    "mixer_out0": [("w_dkv", 0), ("w_ukv", 0), ("b_w_dq", 0), ("b_w_uq", 0)],
    "ffn_fwd0": [("ffn_w_down", 0), ("a_w_in", 1), ("a_w_out", 1)],
    "ffn_out0": [("ffn_w_down", 1)],
    "mixer_fwd1": [("ffn_w_up", 1)],
    "mixer_out1": [("b_w_o", 0)],
    "ffn_fwd1": [("ffn_w_up", 2)],
    "ffn_out1": [("ffn_w_down", 2)],
    "kv_fwd": [("b_w_dq", 1), ("b_w_uq", 1)],
    "q_fwd0": [("b_w_o", 1)],
    "attn_fwd0": [("ffn_w_up", 3), ("ffn_w_down", 3)],
}
BWD_RIDERS = {
    "ffn_in_bwd3": [("ffn_w_down", 3)],
    "attn_bwd1": [("ffn_w_up", 3), ("b_w_o", 1)],
    "q_in_bwd1": [("b_w_dq", 1)],
    "ffn_bwd2": [("b_w_uq", 1)],
    "ffn_in_bwd2": [("ffn_w_down", 2)],
    "attn_bwd0": [("ffn_w_up", 2), ("b_w_o", 0)],
    "q_in_bwd0": [("b_w_dq", 0)],
    "kv_bwd": [("b_w_uq", 0)],
    "kv_in_bwd": [("w_ukv", 0)],
    "ffn_bwd1": [("w_dkv", 0)],
    "ffn_in_bwd1": [("ffn_w_down", 1)],
    "mixer_in_bwd1": [("a_w_out", 1)],
    "ffn_bwd0": [("ffn_w_up", 1), ("a_w_in", 1)],
    "ffn_in_bwd0": [("ffn_w_down", 0)],
    "mixer_bwd0": [("ffn_w_up", 0)],
    "mixer_in_bwd0": [("a_w_out", 0), ("a_w_in", 0)],
}


def kernel(x, a_mix_norm, a_w_in, a_conv, a_w_out, b_mix_norm, b_w_dq, b_q_norm, b_w_uq, b_w_o, kv_in_norm, w_dkv, kv_norm, w_ukv, ffn_norm, ffn_w_up, ffn_conv, ffn_w_down, final_norm, loss_target, m_a_mix_norm, m_a_w_in, m_a_conv, m_a_w_out, m_b_mix_norm, m_b_w_dq, m_b_q_norm, m_b_w_uq, m_b_w_o, m_kv_in_norm, m_w_dkv, m_kv_norm, m_w_ukv, m_ffn_norm, m_ffn_w_up, m_ffn_conv, m_ffn_w_down, m_final_norm, v_a_mix_norm, v_a_w_in, v_a_conv, v_a_w_out, v_b_mix_norm, v_b_w_dq, v_b_q_norm, v_b_w_uq, v_b_w_o, v_kv_in_norm, v_w_dkv, v_kv_norm, v_w_ukv, v_ffn_norm, v_ffn_w_up, v_ffn_conv, v_ffn_w_down, v_final_norm):
    seq, d = x.shape[1], x.shape[2]
    me = 4 * lax.axis_index("x") + 2 * lax.axis_index("y") + lax.axis_index("c")
    h0 = x.reshape(seq, d)
    target = loss_target.reshape(seq, d)
    cos, sin = _rope_tables(seq)
    rank = b_w_dq.shape[-1]
    f8 = ffn_w_up.shape[-1]
    fd = ffn_w_down.shape[1]
    dshard = a_w_out.shape[1]
    hv = N_HEADS * VDIM

    shards = {"a_w_in": a_w_in, "a_w_out": a_w_out, "b_w_dq": b_w_dq, "b_w_uq": b_w_uq,
              "b_w_o": b_w_o, "w_dkv": w_dkv[None], "w_ukv": w_ukv[None],
              "ffn_w_up": ffn_w_up, "ffn_w_down": ffn_w_down}

    def relayout(name, g):
        if name == "a_w_in":
            w = jnp.transpose(g, (1, 0, 2)).reshape(d, 3, d)
            return jnp.transpose(w, (1, 0, 2))
        if name == "a_w_out":
            return g.reshape(d, d)
        if name == "b_w_dq":
            return g.reshape(d, rank)
        if name == "b_w_uq":
            return jnp.pad(g, ((0, 0), (0, 0), (0, QK - NOPE - ROPE)))
        if name == "b_w_o":
            return g.reshape(hv, d)
        if name == "w_dkv":
            return jnp.pad(g.reshape(d, KV_RANK + ROPE), ((0, 0), (0, ROPE_PAD - ROPE)))
        if name == "ffn_w_down":
            return g.reshape(N_DEV // 2, 2 * fd, d)
        return g

    weights = {}

    def ag_rider(host):
        return [("ag", shards[n][l].astype(BF16)) for n, l in FWD_RIDERS.get(host, [])]

    def ag_done(host, outs):
        for (n, l), g in zip(FWD_RIDERS.get(host, []), outs):
            weights[n, l] = relayout(n, g)

    small_shapes = [a_mix_norm.shape, a_conv.shape, ffn_conv.shape]
    first = exchange([("ag", a_w_in[0].astype(BF16)), ("ag", a_w_out[0].astype(BF16)),
                      ("ag", _pack([a_mix_norm, a_conv, ffn_conv]))], "ag_first")
    weights["a_w_in", 0] = relayout("a_w_in", first[0])
    weights["a_w_out", 0] = relayout("a_w_out", first[1])
    s_mix, s_aconv, s_fconv = _unpack(first[2], small_shapes)
    a_gain = jnp.transpose(s_mix, (1, 0, 2)).reshape(N_A, d)
    a_cw = jnp.transpose(s_aconv, (1, 2, 0, 3)).reshape(N_A, 3, d)
    f_cw = jnp.transpose(s_fconv, (1, 0, 2, 3))

    def mixer_gain(layer):
        if layer >= DEPTH:
            return None
        return a_gain[layer][None] if layer < N_A else b_mix_norm[layer - N_A][None]

    saved = {}
    h = h0
    xn = norm_fwd(h, mixer_gain(0), "norm_first")
    kv = None
    for layer in range(DEPTH):
        saved["hm", layer], saved["xm", layer] = h, xn
        if layer < N_A:
            name = f"mixer_fwd{layer}"
            (u4, z), r = mixer_fwd(xn, weights["a_w_in", layer], a_cw[layer], name, rider=ag_rider(name))
            ag_done(name, r)
            saved["mix", layer] = (u4, z)
            name = f"mixer_out{layer}"
            (h, xn), r = proj_residual(z[None], weights["a_w_out", layer][None], h, name,
                                       g_next=ffn_norm[layer][None], rider=ag_rider(name))
            ag_done(name, r)
        else:
            j = layer - N_A
            name = f"q_fwd{j}"
            (q,), r = q_fwd(xn, weights["b_w_dq", j], b_q_norm[j][None], weights["b_w_uq", j],
                            cos, sin, name, rider=ag_rider(name))
            ag_done(name, r)
            name = f"attn_fwd{j}"
            (o, lse), r = attn_fwd(q, kv[0], kv[1], name, rider=ag_rider(name))
            ag_done(name, r)
            saved["attn", layer] = (q, o, lse)
            name = f"attn_out{j}"
            (h, xn), r = proj_residual(o[None], weights["b_w_o", j][None], h, name,
                                       g_next=ffn_norm[layer][None], rider=ag_rider(name))
            ag_done(name, r)
        saved["hf", layer], saved["xf", layer] = h, xn
        name = f"ffn_fwd{layer}"
        (up2, cv2, act), r = ffn_fwd(xn, weights["ffn_w_up", layer], f_cw[layer], name, rider=ag_rider(name))
        ag_done(name, r)
        saved["ffn", layer] = (up2, cv2, act)
        name = f"ffn_out{layer}"
        (h, xn), r = proj_residual(act, weights["ffn_w_down", layer], h, name,
                                   g_next=mixer_gain(layer + 1), rider=ag_rider(name))
        ag_done(name, r)
        if layer == N_A - 1:
            (k_all, v_all, c_kv), r = kv_fwd(h, kv_in_norm[None], weights["w_dkv", 0], kv_norm[None],
                                             weights["w_ukv", 0], cos, sin, "kv_fwd",
                                             rider=ag_rider("kv_fwd"))
            ag_done("kv_fwd", r)
            kv = (k_all, v_all, c_kv)

    sq_err, dh, d_final = loss_head(h, final_norm[None], target, "loss_head")
    loss = lax.psum(sq_err[0, 0] * (0.5 / d), ("x", "y", "c"))

    grads = {}
    parts = {}

    def rs_rider(host):
        return [("rs", grads[key]) for key in BWD_RIDERS.get(host, [])]

    def rs_done(host, outs):
        for key, p in zip(BWD_RIDERS.get(host, []), outs):
            parts[key] = p

    d_ffn_norm = [None] * DEPTH
    d_fconv = [None] * DEPTH
    d_a_gain = [None] * N_A
    d_aconv = [None] * N_A
    d_b_gain = [None] * N_B
    d_q_gain = [None] * N_B
    dks, dvs = [], []
    for layer in reversed(range(DEPTH)):
        if layer == N_A - 1:
            hk = saved["hm", layer + 1]
            (dckv, dwukv, dwdkv, d_kv_gain), r = kv_bwd(
                dks, dvs, kv[2], hk, kv_in_norm[None], kv_norm[None], weights["w_ukv", 0],
                cos, sin, "kv_bwd", rider=rs_rider("kv_bwd"))
            rs_done("kv_bwd", r)
            grads["w_ukv", 0] = dwukv
            grads["w_dkv", 0] = dwdkv[:, :KV_RANK + ROPE].reshape(N_DEV, dshard, KV_RANK + ROPE)
            (dh, d_kvin_gain), r = proj_t_rms_bwd(dckv, weights["w_dkv", 0][None], hk, kv_in_norm[None],
                                                  dh, "kv_in_bwd", rider=rs_rider("kv_in_bwd"))
            rs_done("kv_in_bwd", r)
        up2, cv2, act = saved["ffn", layer]
        name = f"ffn_bwd{layer}"
        (dup2, dwup, dwdown, dcw), r = ffn_bwd(dh, weights["ffn_w_down", layer], up2, cv2, act,
                                               saved["xf", layer], f_cw[layer], name, rider=rs_rider(name))
        rs_done(name, r)
        grads["ffn_w_up", layer] = dwup.reshape(N_DEV, d, f8)
        grads["ffn_w_down", layer] = dwdown.reshape(N_DEV, fd, d)
        d_fconv[layer] = dcw.reshape(N_DEV, 3, f8)
        name = f"ffn_in_bwd{layer}"
        (dh, d_ffn_norm[layer]), r = proj_t_rms_bwd(dup2.reshape(N_DEV, seq, f8), weights["ffn_w_up", layer],
                                                    saved["hf", layer], ffn_norm[layer][None], dh, name,
                                                    rider=rs_rider(name))
        rs_done(name, r)
        hm, xm = saved["hm", layer], saved["xm", layer]
        if layer < N_A:
            u4, z = saved["mix", layer]
            name = f"mixer_bwd{layer}"
            (du3, dwin3, dwout, dcw), r = mixer_bwd(dh, weights["a_w_out", layer], u4, z, xm, a_cw[layer],
                                                    name, rider=rs_rider(name))
            rs_done(name, r)
            dwin = jnp.transpose(dwin3, (1, 0, 2)).reshape(d, N_DEV, 3 * d // N_DEV)
            grads["a_w_in", layer] = jnp.transpose(dwin, (1, 0, 2))
            grads["a_w_out", layer] = dwout.reshape(N_DEV, dshard, d)
            d_aconv[layer] = dcw
            name = f"mixer_in_bwd{layer}"
            (dh, d_a_gain[layer]), r = proj_t_rms_bwd(du3, weights["a_w_in", layer], hm, a_gain[layer][None],
                                                      dh, name, rider=rs_rider(name))
            rs_done(name, r)
        else:
            j = layer - N_A
            q, o, lse = saved["attn", layer]
            name = f"attn_out_bwd{j}"
            (do, dwo), r = o_bwd(dh, o, weights["b_w_o", j], name, rider=rs_rider(name))
            rs_done(name, r)
            grads["b_w_o", j] = dwo.reshape(N_DEV, dshard, d)
            name = f"attn_bwd{j}"
            (dq, dk, dv), r = attn_bwd(q, kv[0], kv[1], o, do, lse, name, rider=rs_rider(name))
            rs_done(name, r)
            dks.append(dk)
            dvs.append(dv)
            name = f"q_bwd{j}"
            (dqc, dwuq, dwdq, d_q_gain[j]), r = q_bwd(dq, xm, weights["b_w_dq", j], b_q_norm[j][None],
                                                      weights["b_w_uq", j], cos, sin, name, rider=rs_rider(name))
            rs_done(name, r)
            grads["b_w_uq", j] = dwuq[:, :, :NOPE + ROPE]
            grads["b_w_dq", j] = dwdq.reshape(N_DEV, dshard, rank)
            name = f"q_in_bwd{j}"
            (dh, d_b_gain[j]), r = proj_t_rms_bwd(dqc, weights["b_w_dq", j][None], hm, b_mix_norm[j][None],
                                                  dh, name, rider=rs_rider(name))
            rs_done(name, r)
    grad_x = dh.reshape(x.shape)

    full_small = [
        jnp.concatenate(d_a_gain, axis=0),
        jnp.stack(d_aconv),
        jnp.concatenate(d_b_gain, axis=0),
        jnp.concatenate(d_q_gain, axis=0),
        d_kvin_gain[0],
        d_kv_gain[0],
        jnp.concatenate(d_ffn_norm, axis=0),
        jnp.stack(d_fconv),
        d_final[0],
    ]
    full_shapes = [t.shape for t in full_small]
    small_pack = _pack(full_small)

    res = {}

    def update(name, n_layers, w, m, v, extra=(), transposed=False):
        view = (lambda t: jnp.transpose(t, (0, 2, 1))) if transposed else (lambda t: t)
        call = sum_adamw_transposed if transposed else sum_adamw
        shard = w.shape if w.ndim == 3 else (1,) + w.shape
        host = f"adamw_{name}"
        outs, r = call([parts[name, l] for l in range(n_layers)], view(w.reshape(shard)),
                       view(m.reshape(shard)), view(v.reshape(shard)), host,
                       rider=rs_rider(host) + list(extra))
        rs_done(host, r)
        res[name] = [view(t).reshape(w.shape) for t in outs]
        return r[len(BWD_RIDERS.get(host, [])):]

    (g_parts,) = update("a_w_out", N_A, a_w_out, m_a_w_out, v_a_w_out, extra=[("ag", small_pack)])
    update("ffn_w_down", DEPTH, ffn_w_down, m_ffn_w_down, v_ffn_w_down)
    update("ffn_w_up", DEPTH, ffn_w_up, m_ffn_w_up, v_ffn_w_up, transposed=True)
    update("b_w_dq", N_B, b_w_dq, m_b_w_dq, v_b_w_dq)
    update("b_w_uq", N_B, b_w_uq, m_b_w_uq, v_b_w_uq)
    update("b_w_o", N_B, b_w_o, m_b_w_o, v_b_w_o)
    update("w_dkv", 1, w_dkv, m_w_dkv, v_w_dkv)
    update("w_ukv", 1, w_ukv, m_w_ukv, v_w_ukv)
    update("a_w_in", N_A, a_w_in, m_a_w_in, v_a_w_in)

    summed = sum_slots(g_parts, "sum_small_grads")
    (s_a_gain, s_aconv_g, s_b_gain, s_q_gain, s_kvin, s_kvn, s_ffn_gain, s_fconv_g,
     s_final) = _unpack(summed, full_shapes)
    dsl = d // N_DEV
    small = [
        ("a_mix_norm", lax.dynamic_slice_in_dim(s_a_gain, me * dsl, dsl, axis=1), a_mix_norm, m_a_mix_norm, v_a_mix_norm),
        ("a_conv", lax.dynamic_slice_in_dim(s_aconv_g, me * dsl, dsl, axis=2), a_conv, m_a_conv, v_a_conv),
        ("b_mix_norm", s_b_gain, b_mix_norm, m_b_mix_norm, v_b_mix_norm),
        ("b_q_norm", s_q_gain, b_q_norm, m_b_q_norm, v_b_q_norm),
        ("kv_in_norm", s_kvin, kv_in_norm, m_kv_in_norm, v_kv_in_norm),
        ("kv_norm", s_kvn, kv_norm, m_kv_norm, v_kv_norm),
        ("ffn_norm", s_ffn_gain, ffn_norm, m_ffn_norm, v_ffn_norm),
        ("ffn_conv", lax.dynamic_index_in_dim(s_fconv_g, me, axis=1, keepdims=False), ffn_conv, m_ffn_conv, v_ffn_conv),
        ("final_norm", s_final, final_norm, m_final_norm, v_final_norm),
    ]
    shapes = [t[2].shape for t in small]
    packed = [_pack([t[k] for t in small])[None] for k in (1, 2, 3, 4)]
    outs, _ = sum_adamw([packed[0]], packed[1], packed[2], packed[3], "adamw_small")
    unpacked = [_unpack(t[0], shapes) for t in outs]
    for idx, t in enumerate(small):
        res[t[0]] = [unpacked[k][idx] for k in range(4)]

    order = ["a_mix_norm", "a_w_in", "a_conv", "a_w_out", "b_mix_norm", "b_w_dq", "b_q_norm",
             "b_w_uq", "b_w_o", "kv_in_norm", "w_dkv", "kv_norm", "w_ukv", "ffn_norm",
             "ffn_w_up", "ffn_conv", "ffn_w_down", "final_norm"]
    return (loss, grad_x, *[res[n][0] for n in order], *[res[n][1] for n in order],
            *[res[n][2] for n in order], *[res[n][3] for n in order])
```

```python
import functools

import jax
import jax.numpy as jnp
from jax import lax
from jax.experimental import pallas as pl
from jax.experimental.pallas import tpu as pltpu

F32 = jnp.float32
BF16 = jnp.bfloat16

N_DEV = 8
N_HEADS = 8
NOPE = 128
ROPE = 64
ROPE_PAD = 128
QK = NOPE + ROPE_PAD
VDIM = 128
KV_RANK = 256
ROPE_THETA = 10000.0
RMS_EPS = 1e-6
ATTN_SCALE = (NOPE + ROPE) ** -0.5
N_A = 2
N_B = 2
DEPTH = 4

ADAM_LR = 0.001
ADAM_B1 = 0.9
ADAM_B2 = 0.999
ADAM_EPS = 1e-08
ADAM_WD = 0.01
ADAM_STEP = 10

V7X_VMEM_LIMIT = 56 * 1024 * 1024
BF16_SUBLANES = 16
ROW_TILE = 512
ROW_TILE_SMALL = 256
ATTN_TILE = 512
MIXER_CHUNK = 512
LANES = 128
NEG_BIG = -1e30
COPIES_PER_TASK = 7

MESH_ID = pl.DeviceIdType.MESH
ANY = pl.BlockSpec(memory_space=pl.ANY)


def _nt(a, b):
    return lax.dot_general(a, b, (((1,), (1,)), ((), ())), preferred_element_type=F32)


def _tn(a, b):
    return lax.dot_general(a, b, (((0,), (0,)), ((), ())), preferred_element_type=F32)


def _nn(a, b):
    return jnp.dot(a, b, preferred_element_type=F32)


def _rms(h, g):
    rstd = lax.rsqrt(jnp.mean(h * h, axis=-1, keepdims=True) + RMS_EPS)
    xhat = h * rstd
    return xhat * g, xhat, rstd


def _rms_bwd(dxn, xhat, rstd, g):
    dxhat = dxn * g
    dh = rstd * (dxhat - xhat * jnp.mean(dxhat * xhat, axis=-1, keepdims=True))
    return dh, dxn * xhat


def _shift_down(x, k, halo_rows):
    r = pltpu.roll(x, k, 0)
    row = lax.broadcasted_iota(jnp.int32, x.shape, 0)
    for t in range(k):
        r = jnp.where(row == t, halo_rows[t], r)
    return r


def _shift_up(x, k, halo_rows):
    n = x.shape[0]
    r = pltpu.roll(x, n - k, 0)
    row = lax.broadcasted_iota(jnp.int32, x.shape, 0)
    for t in range(k):
        r = jnp.where(row == n - k + t, halo_rows[t], r)
    return r


def _conv_taps(w_ref):
    return w_ref[0:1, :], w_ref[1:2, :], w_ref[2:3, :]


def _rope_swap(x):
    lane = lax.broadcasted_iota(jnp.int32, x.shape, 1)
    return jnp.where(lane < ROPE // 2, pltpu.roll(x, ROPE_PAD - ROPE // 2, 1),
                     pltpu.roll(x, ROPE // 2, 1))


def _rope_fwd(x, cos, sin):
    return x * cos + _rope_swap(x) * sin


def _rope_bwd(dy, cos, sin):
    return dy * cos - _rope_swap(dy) * sin


def _rope_tables(seq):
    inv = 1.0 / (ROPE_THETA ** (jnp.arange(0, ROPE, 2, dtype=F32) / ROPE))
    ang = jnp.arange(seq, dtype=F32)[:, None] * inv[None, :]
    cos, sin = jnp.cos(ang), jnp.sin(ang)
    zero = jnp.zeros((seq, ROPE_PAD - ROPE), F32)
    return (jnp.concatenate([cos, cos, zero], axis=1),
            jnp.concatenate([-sin, sin, zero], axis=1))


def _row_tile(rows, cap, mult=8):
    best = None
    for t in range(mult, min(rows, cap) + 1, mult):
        if rows % t == 0:
            best = t
    return rows if best is None else best


class _AllGatherTask:
    def __init__(self, t, x_ref, out_ref, send_sems, recv_sems, local_sems):
        self.t, self.x_ref, self.out_ref = t, x_ref, out_ref
        self.send_sems, self.recv_sems, self.local_sems = send_sems, recv_sems, local_sems
        mx, my, mc = lax.axis_index("x"), lax.axis_index("y"), lax.axis_index("c")
        self.mc = mc
        self.me, self.sibling = (mx, my, mc), (mx, my, 1 - mc)
        self.chips = [(1 - mx, my), (mx, 1 - my), (1 - mx, 1 - my)]

    def _slot(self, px, py, pc):
        return self.out_ref.at[4 * px + 2 * py + pc]

    def _copy(self, k, block, to, src=None):
        s = COPIES_PER_TASK * self.t + k
        return pltpu.make_async_remote_copy(
            src_ref=self._slot(*block) if src is None else src, dst_ref=self._slot(*block),
            send_sem=self.send_sems.at[s], recv_sem=self.recv_sems.at[s],
            device_id=to, device_id_type=MESH_ID)

    def _mine(self):
        return pltpu.make_async_copy(self.x_ref, self._slot(*self.me), self.local_sems.at[self.t])

    def _first(self):
        out = [self._copy(0, self.me, self.sibling, src=self.x_ref)]
        out += [self._copy(1 + j, self.me, (*chip, self.mc), src=self.x_ref)
                for j, chip in enumerate(self.chips)]
        return out

    def _passed(self):
        return [self._copy(4 + j, (*chip, self.mc), self.sibling) for j, chip in enumerate(self.chips)]

    def start(self):
        self._mine().start()
        for cp in self._first():
            cp.start()

    def forward(self):
        passed = self._passed()
        for j, chip in enumerate(self.chips):
            self._copy(1 + j, (*chip, self.mc), self.me).wait_recv()
            passed[j].start()

    def finish(self):
        self._copy(0, self.sibling, self.me).wait_recv()
        for j, chip in enumerate(self.chips):
            self._copy(4 + j, (*chip, 1 - self.mc), self.me).wait_recv()
        for cp in self._first() + self._passed():
            cp.wait_send()
        self._mine().wait()


class _ReduceScatterTask:
    def __init__(self, t, g_ref, out_ref, send_sems, recv_sems, local_sems):
        self.t, self.g_ref, self.out_ref = t, g_ref, out_ref
        self.send_sems, self.recv_sems, self.local_sems = send_sems, recv_sems, local_sems
        mx, my, mc = lax.axis_index("x"), lax.axis_index("y"), lax.axis_index("c")
        self.me = 4 * mx + 2 * my + mc
        self.peers = []
        for k in range(1, N_DEV):
            px, py, pc = mx ^ ((k >> 2) & 1), my ^ ((k >> 1) & 1), mc ^ (k & 1)
            self.peers.append(((px, py, pc), 4 * px + 2 * py + pc))

    def _mine(self):
        return pltpu.make_async_copy(self.g_ref.at[self.me], self.out_ref.at[self.me],
                                     self.local_sems.at[self.t])

    def _copy(self, k, src_slot, dst_slot):
        s = COPIES_PER_TASK * self.t + k
        return pltpu.make_async_remote_copy(
            src_ref=self.g_ref.at[src_slot], dst_ref=self.out_ref.at[dst_slot],
            send_sem=self.send_sems.at[s], recv_sem=self.recv_sems.at[s],
            device_id=self.peers[k][0], device_id_type=MESH_ID)

    def start(self):
        self._mine().start()
        for k, (_, peer) in enumerate(self.peers):
            self._copy(k, peer, self.me).start()

    def forward(self):
        pass

    def finish(self):
        for k, (_, peer) in enumerate(self.peers):
            self._copy(k, self.me, peer).wait_recv()
        for k, (_, peer) in enumerate(self.peers):
            self._copy(k, peer, self.me).wait_send()
        self._mine().wait()


class _PairExchangeTask:
    def __init__(self, t, g_ref, out_ref, send_sems, recv_sems, local_sems):
        mx, my, mc = lax.axis_index("x"), lax.axis_index("y"), lax.axis_index("c")
        s = COPIES_PER_TASK * t
        self.copy = pltpu.make_async_remote_copy(
            src_ref=g_ref.at[:, 1 - mc], dst_ref=out_ref,
            send_sem=send_sems.at[s], recv_sem=recv_sems.at[s],
            device_id=(mx, my, 1 - mc), device_id_type=MESH_ID)

    def start(self):
        self.copy.start()

    def forward(self):
        pass

    def finish(self):
        self.copy.wait()


class _ChipScatterTask:
    def __init__(self, t, s_ref, out_ref, send_sems, recv_sems, local_sems):
        self.t, self.s_ref, self.out_ref = t, s_ref, out_ref
        self.send_sems, self.recv_sems, self.local_sems = send_sems, recv_sems, local_sems
        mx, my, mc = lax.axis_index("x"), lax.axis_index("y"), lax.axis_index("c")
        self.chip = 2 * mx + my
        self.peers = []
        for k in range(1, N_DEV // 2):
            px, py = mx ^ ((k >> 1) & 1), my ^ (k & 1)
            self.peers.append(((px, py, mc), 2 * px + py))

    def _mine(self):
        return pltpu.make_async_copy(self.s_ref.at[self.chip], self.out_ref.at[self.chip],
                                     self.local_sems.at[self.t])

    def _copy(self, k, src_slot, dst_slot):
        s = COPIES_PER_TASK * self.t + k
        return pltpu.make_async_remote_copy(
            src_ref=self.s_ref.at[src_slot], dst_ref=self.out_ref.at[dst_slot],
            send_sem=self.send_sems.at[s], recv_sem=self.recv_sems.at[s],
            device_id=self.peers[k][0], device_id_type=MESH_ID)

    def start(self):
        self._mine().start()
        for k, (_, peer) in enumerate(self.peers):
            self._copy(k, peer, self.chip).start()

    def forward(self):
        pass

    def finish(self):
        for k, (_, peer) in enumerate(self.peers):
            self._copy(k, self.chip, peer).wait_recv()
        for k, (_, peer) in enumerate(self.peers):
            self._copy(k, peer, self.chip).wait_send()
        self._mine().wait()


_TASKS = {"ag": _AllGatherTask, "rs": _ReduceScatterTask, "rs_pair": _PairExchangeTask,
          "rs_chip": _ChipScatterTask}


def _task_shape(kind, arr):
    shape = {"ag": (N_DEV,) + arr.shape, "rs": arr.shape, "rs_chip": arr.shape,
             "rs_pair": arr.shape[:1] + arr.shape[2:]}[kind]
    return jax.ShapeDtypeStruct(shape, arr.dtype)


def _sem_shapes(n_tasks):
    return [pltpu.SemaphoreType.DMA((COPIES_PER_TASK * n_tasks,)),
            pltpu.SemaphoreType.DMA((COPIES_PER_TASK * n_tasks,)),
            pltpu.SemaphoreType.DMA((n_tasks,))]


def _make_tasks(rider, in_refs, out_refs, sems):
    return [_TASKS[kind](t, in_refs[t], out_refs[t], *sems) for t, (kind, _) in enumerate(rider)]


def exchange(rider, name):
    n = len(rider)

    def body(*refs):
        tasks = _make_tasks(rider, refs[:n], refs[n:2 * n], refs[2 * n:])
        for task in tasks:
            task.start()
        for task in tasks:
            task.forward()
        for task in tasks:
            task.finish()

    return list(pl.pallas_call(
        body, name=name, out_shape=tuple(_task_shape(k, a) for k, a in rider),
        in_specs=[ANY] * n, out_specs=(ANY,) * n, scratch_shapes=_sem_shapes(n),
    )(*[a for _, a in rider]))


def _call(body, name, grid, in_specs, out_specs, out_shape, args, scratch=(), rider=()):
    in_specs, out_specs, out_shape = list(in_specs), tuple(out_specs), tuple(out_shape)
    n_in, n_out, n_scr, n_r = len(in_specs), len(out_specs), len(scratch), len(rider)
    if n_r:
        def kern(*refs):
            ins, r_in = refs[:n_in], refs[n_in:n_in + n_r]
            o0 = n_in + n_r
            outs, r_out = refs[o0:o0 + n_out], refs[o0 + n_out:o0 + n_out + n_r]
            s0 = o0 + n_out + n_r
            scr, sems = refs[s0:s0 + n_scr], refs[s0 + n_scr:]
            step = 0
            for a, n in enumerate(grid):
                step = step * n + pl.program_id(a)
            n_steps = 1
            for n in grid:
                n_steps *= n

            @pl.when(step == 0)
            def _():
                for task in _make_tasks(rider, r_in, r_out, sems):
                    task.start()
            body(*ins, *outs, *scr)

            @pl.when(step == n_steps - 1)
            def _():
                tasks = _make_tasks(rider, r_in, r_out, sems)
                for task in tasks:
                    task.forward()
                for task in tasks:
                    task.finish()
    else:
        kern = body
    res = pl.pallas_call(
        kern, name=name, grid=grid,
        in_specs=in_specs + [ANY] * n_r, out_specs=out_specs + (ANY,) * n_r,
        out_shape=out_shape + tuple(_task_shape(k, a) for k, a in rider),
        scratch_shapes=list(scratch) + (_sem_shapes(n_r) if n_r else []),
        compiler_params=pltpu.CompilerParams(dimension_semantics=("arbitrary",) * len(grid),
                                             vmem_limit_bytes=V7X_VMEM_LIMIT),
    )(*args, *[a for _, a in rider])
    return list(res[:n_out]), list(res[n_out:])


def _adamw(g, w, m, v):
    m = ADAM_B1 * m + (1.0 - ADAM_B1) * g
    v = ADAM_B2 * v + (1.0 - ADAM_B2) * (g * g)
    m_hat = m / (1.0 - ADAM_B1 ** ADAM_STEP)
    v_hat = v / (1.0 - ADAM_B2 ** ADAM_STEP)
    delta = -ADAM_LR * (m_hat / (jnp.sqrt(v_hat) + ADAM_EPS) + ADAM_WD * w)
    return delta, m, v


def sum_adamw(parts, w, m, v, name, rider=()):
    n_l, rows, cols = w.shape
    n = parts[0].shape[0]
    mult = BF16_SUBLANES if parts[0].dtype == BF16 else 8
    tr = _row_tile(rows, 128, mult)
    n_i = rows // tr

    def body(*refs):
        part_refs = refs[:n_l]
        w_ref, m_ref, v_ref, g_out, d_out, m_out, v_out = refs[n_l:]
        layer = pl.program_id(0)
        for k in range(n_l):
            @pl.when(layer == k)
            def _(k=k):
                g = part_refs[k][0].astype(F32)
                for s in range(1, n):
                    g = g + part_refs[k][s].astype(F32)
                delta, m_new, v_new = _adamw(g, w_ref[...], m_ref[...], v_ref[...])
                g_out[...] = g
                d_out[...] = delta
                m_out[...] = m_new
                v_out[...] = v_new

    part_specs = [pl.BlockSpec((n, tr, cols), functools.partial(
        lambda l, i, k: (0, jnp.where(l == k, i, 0), 0), k=k)) for k in range(n_l)]
    wspec = pl.BlockSpec((None, tr, cols), lambda l, i: (l, i, 0))
    shape = jax.ShapeDtypeStruct(w.shape, F32)
    return _call(body, name, (n_l, n_i), part_specs + [wspec] * 3, (wspec,) * 4, (shape,) * 4,
                 (*parts, w, m, v), rider=rider)


def sum_adamw_transposed(parts, w_t, m_t, v_t, name, rider=()):
    n_l, cols, rows = w_t.shape
    tr = LANES
    n_i = rows // tr
    starts = list(range(0, cols - LANES + 1, LANES))
    if starts[-1] + LANES < cols:
        starts.append(cols - LANES)

    def body(*refs):
        part_refs = refs[:n_l]
        w_ref, m_ref, v_ref, g_out, d_out, m_out, v_out = refs[n_l:]
        layer = pl.program_id(0)
        for k in range(n_l):
            @pl.when(layer == k)
            def _(k=k):
                for c0 in starts:
                    piece = pl.ds(c0, LANES)
                    g = part_refs[k][0, :, piece].astype(F32)
                    for s in range(1, parts[k].shape[0]):
                        g = g + part_refs[k][s, :, piece].astype(F32)
                    g = g.T
                    delta, m_new, v_new = _adamw(g, w_ref[piece, :], m_ref[piece, :], v_ref[piece, :])
                    g_out[piece, :] = g
                    d_out[piece, :] = delta
                    m_out[piece, :] = m_new
                    v_out[piece, :] = v_new

    part_specs = [pl.BlockSpec((parts[k].shape[0], tr, cols), functools.partial(
        lambda l, i, k: (0, jnp.where(l == k, i, 0), 0), k=k)) for k in range(n_l)]
    wspec = pl.BlockSpec((None, cols, tr), lambda l, i: (l, 0, i))
    shape = jax.ShapeDtypeStruct(w_t.shape, F32)
    return _call(body, name, (n_l, n_i), part_specs + [wspec] * 3, (wspec,) * 4, (shape,) * 4,
                 (*parts, w_t, m_t, v_t), rider=rider)


def pair_sum(g4, other, name):
    n_chip, _, rows, cols = g4.shape
    tr = _row_tile(rows, 256, BF16_SUBLANES)

    def body(g_ref, o_ref, s_ref):
        mine = g_ref[lax.axis_index("c")]
        s_ref[...] = (mine.astype(F32) + o_ref[...].astype(F32)).astype(BF16)

    blk = pl.BlockSpec((None, tr, cols), lambda k, i: (k, i, 0))
    return _call(body, name, (n_chip, rows // tr),
                 [pl.BlockSpec((None, 2, tr, cols), lambda k, i: (k, 0, i, 0)), blk], [blk],
                 [jax.ShapeDtypeStruct((n_chip, rows, cols), g4.dtype)], (g4, other))[0][0]


def sum_slots(parts, name):
    n, rows, cols = parts.shape

    def body(p_ref, o_ref):
        acc = p_ref[0]
        for s in range(1, n):
            acc = acc + p_ref[s]
        o_ref[...] = acc

    return pl.pallas_call(
        body, name=name, out_shape=jax.ShapeDtypeStruct((rows, cols), F32),
        in_specs=[pl.BlockSpec(memory_space=pltpu.VMEM)],
        out_specs=pl.BlockSpec(memory_space=pltpu.VMEM),
    )(parts)


def norm_fwd(h, g, name):
    seq, d = h.shape
    tm = min(ROW_TILE, seq)

    def body(h_ref, g_ref, o_ref):
        o_ref[...] = _rms(h_ref[...], g_ref[...])[0].astype(BF16)

    return _call(body, name, (seq // tm,),
                 [pl.BlockSpec((tm, d), lambda i: (i, 0)), pl.BlockSpec((1, d), lambda i: (0, 0))],
                 [pl.BlockSpec((tm, d), lambda i: (i, 0))],
                 [jax.ShapeDtypeStruct((seq, d), BF16)], (h, g))[0][0]


def proj_residual(a, w, res, name, g_next=None, rider=()):
    nb, seq, kb = a.shape
    d = w.shape[-1]
    tm = min(ROW_TILE, seq)
    with_norm = g_next is not None

    def body(a_ref, w_ref, r_ref, *rest):
        acc = r_ref[...]
        for b in range(nb):
            acc = acc + _nn(a_ref[b], w_ref[b])
        if with_norm:
            g_ref, o_ref, xn_ref = rest
            xn_ref[...] = _rms(acc, g_ref[...])[0].astype(BF16)
        else:
            (o_ref,) = rest
        o_ref[...] = acc

    row = pl.BlockSpec((tm, d), lambda i: (i, 0))
    in_specs = [pl.BlockSpec((nb, tm, kb), lambda i: (0, i, 0)),
                pl.BlockSpec((nb, kb, d), lambda i: (0, 0, 0)), row]
    args = [a, w, res]
    out_specs, out_shape = [row], [jax.ShapeDtypeStruct((seq, d), F32)]
    if with_norm:
        in_specs.append(pl.BlockSpec((1, d), lambda i: (0, 0)))
        args.append(g_next)
        out_specs.append(row)
        out_shape.append(jax.ShapeDtypeStruct((seq, d), BF16))
    outs, r_outs = _call(body, name, (seq // tm,), in_specs, out_specs, out_shape, args, rider=rider)
    return (outs[0], outs[1] if with_norm else None), r_outs


def proj_t_rms_bwd(du, w, h, g, dres, name, rider=()):
    nb, seq, wd = du.shape
    k = w.shape[1]
    tm = min(ROW_TILE_SMALL, seq)

    def body(du_ref, w_ref, h_ref, g_ref, dr_ref, dh_ref, dg_ref):
        i = pl.program_id(0)
        dxn = _nt(du_ref[0], w_ref[0])
        for b in range(1, nb):
            dxn = dxn + _nt(du_ref[b], w_ref[b])
        _, xhat, rstd = _rms(h_ref[...], g_ref[...])
        dh, dg_rows = _rms_bwd(dxn, xhat, rstd, g_ref[...])
        dh_ref[...] = dr_ref[...] + dh

        @pl.when(i == 0)
        def _():
            dg_ref[...] = jnp.zeros_like(dg_ref)
        dg_ref[...] += jnp.sum(dg_rows, axis=0, keepdims=True)

    row = pl.BlockSpec((tm, k), lambda i: (i, 0))
    vec = pl.BlockSpec((1, k), lambda i: (0, 0))
    return _call(body, name, (seq // tm,),
                 [pl.BlockSpec((nb, tm, wd), lambda i: (0, i, 0)),
                  pl.BlockSpec((nb, k, wd), lambda i: (0, 0, 0)), row, vec, row],
                 (row, vec),
                 (jax.ShapeDtypeStruct((seq, k), F32), jax.ShapeDtypeStruct((1, k), F32)),
                 (du, w, h, g, dres), rider=rider)


def mixer_fwd(xn, win3, cw, name, rider=()):
    seq, d = xn.shape
    tm = min(ROW_TILE, seq)
    cc = min(MIXER_CHUNK, d)
    n_c, n_i = d // cc, seq // tm

    def body(x_ref, w_ref, cw_ref, u_ref, z_ref, carry):
        i = pl.program_id(1)

        @pl.when(i == 0)
        def _():
            carry[...] = jnp.zeros_like(carry)
        xb = x_ref[...]
        b = _nn(xb, w_ref[0])
        c = _nn(xb, w_ref[1])
        hh = _nn(xb, w_ref[2])
        p = c * hh
        w0, w1, w2 = _conv_taps(cw_ref)
        p1 = _shift_down(p, 1, [carry[7:8, :]])
        p2 = _shift_down(p, 2, [carry[6:7, :], carry[7:8, :]])
        q = w0 * p2 + w1 * p1 + w2 * p
        carry[...] = p[tm - 8:tm, :]
        u_ref[0] = b.astype(BF16)
        u_ref[1] = c.astype(BF16)
        u_ref[2] = hh.astype(BF16)
        u_ref[3] = q.astype(BF16)
        z_ref[...] = (b * q).astype(BF16)

    return _call(body, name, (n_c, n_i),
                 [pl.BlockSpec((tm, d), lambda c, i: (i, 0)),
                  pl.BlockSpec((3, d, cc), lambda c, i: (0, 0, c)),
                  pl.BlockSpec((3, cc), lambda c, i: (0, c))],
                 (pl.BlockSpec((4, tm, cc), lambda c, i: (0, i, c)),
                  pl.BlockSpec((tm, cc), lambda c, i: (i, c))),
                 (jax.ShapeDtypeStruct((4, seq, d), BF16), jax.ShapeDtypeStruct((seq, d), BF16)),
                 (xn, win3, cw), scratch=[pltpu.VMEM((8, cc), F32)], rider=rider)


def mixer_bwd(dh, wout, u4, z, xn, cw, name, rider=()):
    seq, d = xn.shape
    tm = min(ROW_TILE, seq)
    cc = min(MIXER_CHUNK, d)
    n_c, n_i = d // cc, seq // tm

    def body(dh_ref, wout_ref, u_ref, z_ref, x_ref, cw_ref,
             du_ref, dwin_ref, dwout_ref, dcw_ref, acc_in, acc_out, acc_cw, carry):
        i = pl.program_id(1)

        @pl.when(i == 0)
        def _():
            acc_in[...] = jnp.zeros_like(acc_in)
            acc_out[...] = jnp.zeros_like(acc_out)
            acc_cw[...] = jnp.zeros_like(acc_cw)
            carry[...] = jnp.zeros_like(carry)
        dhb = dh_ref[...].astype(BF16)
        dz = _nt(dhb, wout_ref[...])
        acc_out[...] += _tn(z_ref[...], dhb)
        b = u_ref[0].astype(F32)
        c = u_ref[1].astype(F32)
        hh = u_ref[2].astype(F32)
        q = u_ref[3].astype(F32)
        p = c * hh
        db = dz * q
        dq = dz * b
        w0, w1, w2 = _conv_taps(cw_ref)
        dq1 = _shift_up(dq, 1, [carry[0:1, :]])
        dq2 = _shift_up(dq, 2, [carry[0:1, :], carry[1:2, :]])
        dp = w2 * dq + w1 * dq1 + w0 * dq2
        carry[...] = dq[0:8, :]
        acc_cw[0:1, :] += jnp.sum(dq2 * p, axis=0, keepdims=True)
        acc_cw[1:2, :] += jnp.sum(dq1 * p, axis=0, keepdims=True)
        acc_cw[2:3, :] += jnp.sum(dq * p, axis=0, keepdims=True)
        dbb = db.astype(BF16)
        dcb = (dp * hh).astype(BF16)
        dhhb = (dp * c).astype(BF16)
        du_ref[0] = dbb
        du_ref[1] = dcb
        du_ref[2] = dhhb
        xb = x_ref[...]
        acc_in[0] += _tn(xb, dbb)
        acc_in[1] += _tn(xb, dcb)
        acc_in[2] += _tn(xb, dhhb)

        @pl.when(i == n_i - 1)
        def _():
            dwin_ref[...] = acc_in[...].astype(BF16)
            dwout_ref[...] = acc_out[...].astype(BF16)
            dcw_ref[...] = acc_cw[0:3, :]

    rev = lambda c, i: (n_i - 1 - i, 0)
    return _call(body, name, (n_c, n_i),
                 [pl.BlockSpec((tm, d), rev),
                  pl.BlockSpec((cc, d), lambda c, i: (c, 0)),
                  pl.BlockSpec((4, tm, cc), lambda c, i: (0, n_i - 1 - i, c)),
                  pl.BlockSpec((tm, cc), lambda c, i: (n_i - 1 - i, c)),
                  pl.BlockSpec((tm, d), rev),
                  pl.BlockSpec((3, cc), lambda c, i: (0, c))],
                 (pl.BlockSpec((3, tm, cc), lambda c, i: (0, n_i - 1 - i, c)),
                  pl.BlockSpec((3, d, cc), lambda c, i: (0, 0, c)),
                  pl.BlockSpec((cc, d), lambda c, i: (c, 0)),
                  pl.BlockSpec((3, cc), lambda c, i: (0, c))),
                 (jax.ShapeDtypeStruct((3, seq, d), BF16), jax.ShapeDtypeStruct((3, d, d), BF16),
                  jax.ShapeDtypeStruct((d, d), BF16), jax.ShapeDtypeStruct((3, d), F32)),
                 (dh, wout, u4, z, xn, cw),
                 scratch=[pltpu.VMEM((3, d, cc), F32), pltpu.VMEM((cc, d), F32),
                          pltpu.VMEM((8, cc), F32), pltpu.VMEM((8, cc), F32)], rider=rider)


def _silu_parts(cg):
    sg = 1.0 / (1.0 + jnp.exp(-cg))
    return sg, cg * sg


def ffn_fwd(xn, wup, fcw, name, rider=()):
    seq, d = xn.shape
    f8 = wup.shape[-1]
    half = N_DEV // 2
    tm = min(ROW_TILE, seq)
    n_i = seq // tm

    def body(x_ref, wg_ref, wu_ref, cg_ref, cu_ref, up_ref, cv_ref, a_ref, carry):
        i = pl.program_id(1)

        @pl.when(i == 0)
        def _():
            carry[...] = jnp.zeros_like(carry)
        xb = x_ref[...]
        conv = []
        for s, (w_ref, t_ref) in enumerate(((wg_ref, cg_ref), (wu_ref, cu_ref))):
            u = _nn(xb, w_ref[...])
            up_ref[s] = u.astype(BF16)
            w0, w1, w2 = _conv_taps(t_ref)
            u1 = _shift_down(u, 1, [carry[s, 7:8, :]])
            u2 = _shift_down(u, 2, [carry[s, 6:7, :], carry[s, 7:8, :]])
            cv = w0 * u2 + w1 * u1 + w2 * u
            cv_ref[s] = cv.astype(BF16)
            conv.append(cv)
            carry[s] = u[tm - 8:tm, :]
        _, silu = _silu_parts(conv[0])
        a_ref[...] = (silu * conv[1]).astype(BF16)

    blk = pl.BlockSpec((2, None, tm, f8), lambda c, i: (0, c, i, 0))
    big = jax.ShapeDtypeStruct((2, half, seq, f8), BF16)
    return _call(body, name, (half, n_i),
                 [pl.BlockSpec((tm, d), lambda c, i: (i, 0)),
                  pl.BlockSpec((None, d, f8), lambda c, i: (c, 0, 0)),
                  pl.BlockSpec((None, d, f8), lambda c, i: (c + half, 0, 0)),
                  pl.BlockSpec((None, 3, f8), lambda c, i: (c, 0, 0)),
                  pl.BlockSpec((None, 3, f8), lambda c, i: (c + half, 0, 0))],
                 (blk, blk, pl.BlockSpec((None, tm, f8), lambda c, i: (c, i, 0))),
                 (big, big, jax.ShapeDtypeStruct((half, seq, f8), BF16)),
                 (xn, wup, wup, fcw, fcw), scratch=[pltpu.VMEM((2, 8, f8), F32)], rider=rider)


def ffn_bwd(dh, wdown, up2, cv2, act, xn, fcw, name, rider=()):
    seq, d = xn.shape
    f8 = up2.shape[-1]
    fb = wdown.shape[1]
    half = N_DEV // 2
    tm = min(ROW_TILE, seq)
    n_i = seq // tm

    def body(dh_ref, wd_ref, up_ref, cv_ref, a_ref, x_ref, cg_ref, cu_ref,
             dup_ref, dwup_ref, dwd_ref, dcw_ref, acc_up, acc_down, acc_cw, carry):
        i = pl.program_id(1)

        @pl.when(i == 0)
        def _():
            acc_up[...] = jnp.zeros_like(acc_up)
            acc_down[...] = jnp.zeros_like(acc_down)
            acc_cw[...] = jnp.zeros_like(acc_cw)
            carry[...] = jnp.zeros_like(carry)
        dhb = dh_ref[...].astype(BF16)
        da = _nt(dhb, wd_ref[...])
        acc_down[...] += _tn(a_ref[...], dhb)
        cg = cv_ref[0].astype(F32)
        cu = cv_ref[1].astype(F32)
        sg, silu = _silu_parts(cg)
        dcg = da * cu * (sg + silu * (1.0 - sg))
        dcu = da * silu
        xb = x_ref[...]
        for s, (dc, t_ref) in enumerate(((dcg, cg_ref), (dcu, cu_ref))):
            w0, w1, w2 = _conv_taps(t_ref)
            d1 = _shift_up(dc, 1, [carry[s, 0:1, :]])
            d2 = _shift_up(dc, 2, [carry[s, 0:1, :], carry[s, 1:2, :]])
            du = (w2 * dc + w1 * d1 + w0 * d2).astype(BF16)
            carry[s] = dc[0:8, :]
            u = up_ref[s].astype(F32)
            acc_cw[s, 0:1, :] += jnp.sum(d2 * u, axis=0, keepdims=True)
            acc_cw[s, 1:2, :] += jnp.sum(d1 * u, axis=0, keepdims=True)
            acc_cw[s, 2:3, :] += jnp.sum(dc * u, axis=0, keepdims=True)
            dup_ref[s] = du
            acc_up[s] += _tn(xb, du)

        @pl.when(i == n_i - 1)
        def _():
            dwup_ref[...] = acc_up[...].astype(BF16)
            dwd_ref[...] = acc_down[...].astype(BF16)
            dcw_ref[...] = acc_cw[:, 0:3, :]

    rev = lambda c, i: (n_i - 1 - i, 0)
    blk = pl.BlockSpec((2, None, tm, f8), lambda c, i: (0, c, n_i - 1 - i, 0))
    return _call(body, name, (half, n_i),
                 [pl.BlockSpec((tm, d), rev),
                  pl.BlockSpec((None, fb, d), lambda c, i: (c, 0, 0)),
                  blk, blk,
                  pl.BlockSpec((None, tm, f8), lambda c, i: (c, n_i - 1 - i, 0)),
                  pl.BlockSpec((tm, d), rev),
                  pl.BlockSpec((None, 3, f8), lambda c, i: (c, 0, 0)),
                  pl.BlockSpec((None, 3, f8), lambda c, i: (c + half, 0, 0))],
                 (blk,
                  pl.BlockSpec((2, None, d, f8), lambda c, i: (0, c, 0, 0)),
                  pl.BlockSpec((None, fb, d), lambda c, i: (c, 0, 0)),
                  pl.BlockSpec((2, None, 3, f8), lambda c, i: (0, c, 0, 0))),
                 (jax.ShapeDtypeStruct((2, half, seq, f8), BF16),
                  jax.ShapeDtypeStruct((2, half, d, f8), BF16),
                  jax.ShapeDtypeStruct((half, fb, d), BF16),
                  jax.ShapeDtypeStruct((2, half, 3, f8), F32)),
                 (dh, wdown, up2, cv2, act, xn, fcw, fcw),
                 scratch=[pltpu.VMEM((2, d, f8), F32), pltpu.VMEM((fb, d), F32),
                          pltpu.VMEM((2, 8, f8), F32), pltpu.VMEM((2, 8, f8), F32)], rider=rider)


def q_fwd(xn, wdq, gq, wuq, cos, sin, name, rider=()):
    seq, d = xn.shape
    rank = wdq.shape[-1]
    tm = min(ROW_TILE, seq)

    def body(x_ref, wdq_ref, gq_ref, wuq_ref, cos_ref, sin_ref, q_ref):
        qc = _nn(x_ref[...], wdq_ref[...])
        qn = _rms(qc, gq_ref[...])[0].astype(BF16)
        for hd in range(N_HEADS):
            qh = _nn(qn, wuq_ref[hd])
            qr = _rope_fwd(qh[:, NOPE:QK], cos_ref[...], sin_ref[...])
            q_ref[hd, :, 0:NOPE] = (qh[:, 0:NOPE] * ATTN_SCALE).astype(BF16)
            q_ref[hd, :, NOPE:QK] = (qr * ATTN_SCALE).astype(BF16)

    rope = pl.BlockSpec((tm, ROPE_PAD), lambda i: (i, 0))
    return _call(body, name, (seq // tm,),
                 [pl.BlockSpec((tm, d), lambda i: (i, 0)),
                  pl.BlockSpec((d, rank), lambda i: (0, 0)),
                  pl.BlockSpec((1, rank), lambda i: (0, 0)),
                  pl.BlockSpec((N_HEADS, rank, QK), lambda i: (0, 0, 0)), rope, rope],
                 [pl.BlockSpec((N_HEADS, tm, QK), lambda i: (0, i, 0))],
                 [jax.ShapeDtypeStruct((N_HEADS, seq, QK), BF16)],
                 (xn, wdq, gq, wuq, cos, sin), rider=rider)


def q_bwd(dq, xn, wdq, gq, wuq, cos, sin, name, rider=()):
    seq, d = xn.shape
    rank = wdq.shape[-1]
    tm = min(ROW_TILE_SMALL, seq)
    n_i = seq // tm

    def body(dq_ref, x_ref, wdq_ref, gq_ref, wuq_ref, cos_ref, sin_ref,
             dqc_ref, dwuq_ref, dwdq_ref, dgq_ref, acc_uq, acc_dq):
        i = pl.program_id(0)

        @pl.when(i == 0)
        def _():
            acc_uq[...] = jnp.zeros_like(acc_uq)
            acc_dq[...] = jnp.zeros_like(acc_dq)
            dgq_ref[...] = jnp.zeros_like(dgq_ref)
        xb = x_ref[...]
        qc = _nn(xb, wdq_ref[...])
        qn, qhat, qrstd = _rms(qc, gq_ref[...])
        qnb = qn.astype(BF16)
        dqn = jnp.zeros((tm, rank), F32)
        for hd in range(N_HEADS):
            dnope = (dq_ref[hd, :, 0:NOPE].astype(F32) * ATTN_SCALE).astype(BF16)
            drope = _rope_bwd(dq_ref[hd, :, NOPE:QK].astype(F32) * ATTN_SCALE, cos_ref[...], sin_ref[...])
            draw = jnp.concatenate([dnope, drope.astype(BF16)], axis=1)
            dqn = dqn + _nt(draw, wuq_ref[hd])
            acc_uq[hd] += _tn(qnb, draw)
        dqc, dg_rows = _rms_bwd(dqn, qhat, qrstd, gq_ref[...])
        dgq_ref[...] += jnp.sum(dg_rows, axis=0, keepdims=True)
        dqcb = dqc.astype(BF16)
        dqc_ref[0] = dqcb
        acc_dq[...] += _tn(xb, dqcb)

        @pl.when(i == n_i - 1)
        def _():
            dwuq_ref[...] = acc_uq[...].astype(BF16)
            dwdq_ref[...] = acc_dq[...].astype(BF16)

    rope = pl.BlockSpec((tm, ROPE_PAD), lambda i: (i, 0))
    return _call(body, name, (n_i,),
                 [pl.BlockSpec((N_HEADS, tm, QK), lambda i: (0, i, 0)),
                  pl.BlockSpec((tm, d), lambda i: (i, 0)),
                  pl.BlockSpec((d, rank), lambda i: (0, 0)),
                  pl.BlockSpec((1, rank), lambda i: (0, 0)),
                  pl.BlockSpec((N_HEADS, rank, QK), lambda i: (0, 0, 0)), rope, rope],
                 (pl.BlockSpec((1, tm, rank), lambda i: (0, i, 0)),
                  pl.BlockSpec((N_HEADS, rank, QK), lambda i: (0, 0, 0)),
                  pl.BlockSpec((d, rank), lambda i: (0, 0)),
                  pl.BlockSpec((1, rank), lambda i: (0, 0))),
                 (jax.ShapeDtypeStruct((1, seq, rank), BF16),
                  jax.ShapeDtypeStruct((N_HEADS, rank, QK), BF16),
                  jax.ShapeDtypeStruct((d, rank), BF16),
                  jax.ShapeDtypeStruct((1, rank), F32)),
                 (dq, xn, wdq, gq, wuq, cos, sin),
                 scratch=[pltpu.VMEM((N_HEADS, rank, QK), F32), pltpu.VMEM((d, rank), F32)],
                 rider=rider)


def kv_fwd(h, g, wdkv, gkv, wukv, cos, sin, name, rider=()):
    seq, d = h.shape
    tm = min(ROW_TILE, seq)
    wk = KV_RANK + ROPE_PAD

    def body(h_ref, g_ref, wdkv_ref, gkv_ref, wukv_ref, cos_ref, sin_ref, k_ref, v_ref, c_ref):
        xk = _rms(h_ref[...], g_ref[...])[0].astype(BF16)
        ckv = _nn(xk, wdkv_ref[...])
        c_kv = ckv[:, 0:KV_RANK]
        c_ref[...] = c_kv
        kr = _rope_fwd(ckv[:, KV_RANK:wk], cos_ref[...], sin_ref[...]).astype(BF16)
        ckn = _rms(c_kv, gkv_ref[...])[0].astype(BF16)
        for hd in range(N_HEADS):
            kvh = _nn(ckn, wukv_ref[hd])
            k_ref[hd, :, 0:NOPE] = kvh[:, 0:NOPE].astype(BF16)
            k_ref[hd, :, NOPE:QK] = kr
            v_ref[hd] = kvh[:, NOPE:NOPE + VDIM].astype(BF16)

    rope = pl.BlockSpec((tm, ROPE_PAD), lambda i: (i, 0))
    return _call(body, name, (seq // tm,),
                 [pl.BlockSpec((tm, d), lambda i: (i, 0)),
                  pl.BlockSpec((1, d), lambda i: (0, 0)),
                  pl.BlockSpec((d, wk), lambda i: (0, 0)),
                  pl.BlockSpec((1, KV_RANK), lambda i: (0, 0)),
                  pl.BlockSpec((N_HEADS, KV_RANK, NOPE + VDIM), lambda i: (0, 0, 0)), rope, rope],
                 (pl.BlockSpec((N_HEADS, tm, QK), lambda i: (0, i, 0)),
                  pl.BlockSpec((N_HEADS, tm, VDIM), lambda i: (0, i, 0)),
                  pl.BlockSpec((tm, KV_RANK), lambda i: (i, 0))),
                 (jax.ShapeDtypeStruct((N_HEADS, seq, QK), BF16),
                  jax.ShapeDtypeStruct((N_HEADS, seq, VDIM), BF16),
                  jax.ShapeDtypeStruct((seq, KV_RANK), F32)),
                 (h, g, wdkv, gkv, wukv, cos, sin), rider=rider)


def kv_bwd(dks, dvs, c_kv, h, g, gkv, wukv, cos, sin, name, rider=()):
    seq, d = h.shape
    tm = min(ROW_TILE_SMALL, seq)
    n_i = seq // tm
    wk = KV_RANK + ROPE_PAD
    n_b = len(dks)

    def body(*refs):
        dk_refs = refs[:n_b]
        dv_refs = refs[n_b:2 * n_b]
        (c_ref, h_ref, g_ref, gkv_ref, wukv_ref, cos_ref, sin_ref,
         dckv_ref, dwukv_ref, dwdkv_ref, dgkv_ref, acc_ukv, acc_dkv) = refs[2 * n_b:]
        i = pl.program_id(0)

        @pl.when(i == 0)
        def _():
            acc_ukv[...] = jnp.zeros_like(acc_ukv)
            acc_dkv[...] = jnp.zeros_like(acc_dkv)
            dgkv_ref[...] = jnp.zeros_like(dgkv_ref)
        ckn, chat, crstd = _rms(c_ref[...], gkv_ref[...])
        cknb = ckn.astype(BF16)
        dckn = jnp.zeros((tm, KV_RANK), F32)
        dkr = jnp.zeros((tm, ROPE_PAD), F32)
        for hd in range(N_HEADS):
            dk = dk_refs[0][hd].astype(F32)
            dv = dv_refs[0][hd].astype(F32)
            for j in range(1, n_b):
                dk = dk + dk_refs[j][hd].astype(F32)
                dv = dv + dv_refs[j][hd].astype(F32)
            dkr = dkr + dk[:, NOPE:QK]
            dkvh = jnp.concatenate([dk[:, 0:NOPE].astype(BF16), dv.astype(BF16)], axis=1)
            dckn = dckn + _nt(dkvh, wukv_ref[hd])
            acc_ukv[hd] += _tn(cknb, dkvh)
        dc_kv, dg_rows = _rms_bwd(dckn, chat, crstd, gkv_ref[...])
        dgkv_ref[...] += jnp.sum(dg_rows, axis=0, keepdims=True)
        dkr_raw = _rope_bwd(dkr, cos_ref[...], sin_ref[...])
        dckv = jnp.concatenate([dc_kv.astype(BF16), dkr_raw.astype(BF16)], axis=1)
        dckv_ref[0] = dckv
        xk = _rms(h_ref[...], g_ref[...])[0].astype(BF16)
        acc_dkv[...] += _tn(xk, dckv)

        @pl.when(i == n_i - 1)
        def _():
            dwukv_ref[...] = acc_ukv[...].astype(BF16)
            dwdkv_ref[...] = acc_dkv[...].astype(BF16)

    kspec = pl.BlockSpec((N_HEADS, tm, QK), lambda i: (0, i, 0))
    vspec = pl.BlockSpec((N_HEADS, tm, VDIM), lambda i: (0, i, 0))
    rope = pl.BlockSpec((tm, ROPE_PAD), lambda i: (i, 0))
    return _call(body, name, (n_i,),
                 [kspec] * n_b + [vspec] * n_b + [
                     pl.BlockSpec((tm, KV_RANK), lambda i: (i, 0)),
                     pl.BlockSpec((tm, d), lambda i: (i, 0)),
                     pl.BlockSpec((1, d), lambda i: (0, 0)),
                     pl.BlockSpec((1, KV_RANK), lambda i: (0, 0)),
                     pl.BlockSpec((N_HEADS, KV_RANK, NOPE + VDIM), lambda i: (0, 0, 0)), rope, rope],
                 (pl.BlockSpec((1, tm, wk), lambda i: (0, i, 0)),
                  pl.BlockSpec((N_HEADS, KV_RANK, NOPE + VDIM), lambda i: (0, 0, 0)),
                  pl.BlockSpec((d, wk), lambda i: (0, 0)),
                  pl.BlockSpec((1, KV_RANK), lambda i: (0, 0))),
                 (jax.ShapeDtypeStruct((1, seq, wk), BF16),
                  jax.ShapeDtypeStruct((N_HEADS, KV_RANK, NOPE + VDIM), BF16),
                  jax.ShapeDtypeStruct((d, wk), BF16),
                  jax.ShapeDtypeStruct((1, KV_RANK), F32)),
                 (*dks, *dvs, c_kv, h, g, gkv, wukv, cos, sin),
                 scratch=[pltpu.VMEM((N_HEADS, KV_RANK, NOPE + VDIM), F32), pltpu.VMEM((d, wk), F32)],
                 rider=rider)


def o_bwd(dh, o, wo, name, rider=()):
    seq, d = dh.shape
    hv = o.shape[1]
    tm = min(ROW_TILE, seq)
    n_i = seq // tm

    def body(dh_ref, o_ref, wo_ref, do_ref, dwo_ref, acc):
        i = pl.program_id(0)

        @pl.when(i == 0)
        def _():
            acc[...] = jnp.zeros_like(acc)
        dhb = dh_ref[...].astype(BF16)
        do_ref[...] = _nt(dhb, wo_ref[...]).astype(BF16)
        acc[...] += _tn(o_ref[...], dhb)

        @pl.when(i == n_i - 1)
        def _():
            dwo_ref[...] = acc[...].astype(BF16)

    return _call(body, name, (n_i,),
                 [pl.BlockSpec((tm, d), lambda i: (i, 0)),
                  pl.BlockSpec((tm, hv), lambda i: (i, 0)),
                  pl.BlockSpec((hv, d), lambda i: (0, 0))],
                 (pl.BlockSpec((tm, hv), lambda i: (i, 0)),
                  pl.BlockSpec((hv, d), lambda i: (0, 0))),
                 (jax.ShapeDtypeStruct((seq, hv), BF16), jax.ShapeDtypeStruct((hv, d), BF16)),
                 (dh, o, wo), scratch=[pltpu.VMEM((hv, d), F32)], rider=rider)


def _mask_diagonal(s):
    row = lax.broadcasted_iota(jnp.int32, s.shape, 0)
    col = lax.broadcasted_iota(jnp.int32, s.shape, 1)
    return jnp.where(col <= row, s, NEG_BIG)


def attn_fwd(q, k, v, name, rider=()):
    _, seq, _ = q.shape
    t = min(ATTN_TILE, seq // 2)
    n_pair = seq // (2 * t)

    def body(q_ref, k_ref, v_ref, o_ref, lse_ref):
        qi = pl.program_id(1)
        q_a = q_ref[0:t, :]
        q_b = q_ref[t:2 * t, :]

        def rows(j):
            return pl.ds(pl.multiple_of(j * t, t), t)

        def update(qx, kb, vb, state, diagonal=False):
            m, l, acc = state
            s = _nt(qx, kb)
            if diagonal:
                s = _mask_diagonal(s)
            m_new = jnp.maximum(m, jnp.max(s, axis=1, keepdims=True))
            p = jnp.exp(s - m_new)
            alpha = jnp.exp(m - m_new)
            l = alpha * l + jnp.sum(p, axis=1, keepdims=True)
            acc = alpha * acc + _nn(p.astype(BF16), vb)
            return m_new, l, acc

        def step(j, carry):
            kb, vb = k_ref[rows(j), :], v_ref[rows(j), :]
            return update(q_a, kb, vb, carry[0:3]) + update(q_b, kb, vb, carry[3:6])

        init = (jnp.full((t, 1), NEG_BIG, F32), jnp.zeros((t, 1), F32), jnp.zeros((t, VDIM), F32))
        carry = lax.fori_loop(0, 2 * qi, step, init + init)
        k0, v0 = k_ref[rows(2 * qi), :], v_ref[rows(2 * qi), :]
        k1, v1 = k_ref[rows(2 * qi + 1), :], v_ref[rows(2 * qi + 1), :]
        state_a = update(q_a, k0, v0, carry[0:3], diagonal=True)
        state_b = update(q_b, k1, v1, update(q_b, k0, v0, carry[3:6]), diagonal=True)
        for half, (m, l, acc) in enumerate((state_a, state_b)):
            o_ref[half * t:(half + 1) * t, :] = (acc / l).astype(BF16)
            lse_ref[half * t:(half + 1) * t, :] = jnp.broadcast_to(m + jnp.log(l), (t, LANES))

    return _call(body, name, (N_HEADS, n_pair),
                 [pl.BlockSpec((None, 2 * t, QK), lambda h, i: (h, i, 0)),
                  pl.BlockSpec((None, seq, QK), lambda h, i: (h, 0, 0)),
                  pl.BlockSpec((None, seq, VDIM), lambda h, i: (h, 0, 0))],
                 (pl.BlockSpec((2 * t, VDIM), lambda h, i: (i, h)),
                  pl.BlockSpec((None, 2 * t, LANES), lambda h, i: (h, i, 0))),
                 (jax.ShapeDtypeStruct((seq, N_HEADS * VDIM), BF16),
                  jax.ShapeDtypeStruct((N_HEADS, seq, LANES), F32)),
                 (q, k, v), rider=rider)


def attn_bwd(q, k, v, o, do, lse, name, rider=()):
    _, seq, _ = q.shape
    t = min(ATTN_TILE, seq // 2)
    n_q = seq // t
    n_pair = n_q // 2

    def body(q_ref, k_ref, v_ref, o_ref, do_ref, lse_ref, dq_ref, dk_ref, dv_ref,
             dq_acc, dk_acc, dv_acc):
        kj = pl.program_id(1)

        @pl.when(kj == 0)
        def _():
            dq_acc[...] = jnp.zeros_like(dq_acc)
        dk_acc[...] = jnp.zeros_like(dk_acc)
        dv_acc[...] = jnp.zeros_like(dv_acc)
        halves = (slice(0, t), slice(t, 2 * t))

        def block(i, masks):
            rows = pl.ds(pl.multiple_of(i * t, t), t)
            qb = q_ref[rows, :]
            dob = do_ref[rows, :]
            lse_col = lse_ref[rows, 0:1]
            delta = jnp.sum(dob.astype(F32) * o_ref[rows, :].astype(F32), axis=1, keepdims=True)
            dq = None
            for x, diagonal in enumerate(masks):
                if diagonal is None:
                    continue
                kb, vb = k_ref[halves[x], :], v_ref[halves[x], :]
                s = _nt(qb, kb)
                if diagonal:
                    s = _mask_diagonal(s)
                p = jnp.exp(s - lse_col)
                ds = (p * (_nt(dob, vb) - delta)).astype(BF16)
                dv_acc[halves[x], :] += _tn(p.astype(BF16), dob)
                dk_acc[halves[x], :] += _tn(ds, qb)
                part = _nn(ds, kb)
                dq = part if dq is None else dq + part
            dq_acc[rows, :] += dq

        block(2 * kj, (True, None))
        block(2 * kj + 1, (False, True))

        def step(i, carry):
            block(i, (False, False))
            return carry

        lax.fori_loop(2 * kj + 2, n_q, step, 0)
        dk_ref[...] = dk_acc[...].astype(BF16)
        dv_ref[...] = dv_acc[...].astype(BF16)

        @pl.when(kj == n_pair - 1)
        def _():
            dq_ref[...] = dq_acc[...].astype(BF16)

    head_rows = pl.BlockSpec((seq, VDIM), lambda h, j: (0, h))
    return _call(body, name, (N_HEADS, n_pair),
                 [pl.BlockSpec((None, seq, QK), lambda h, j: (h, 0, 0)),
                  pl.BlockSpec((None, 2 * t, QK), lambda h, j: (h, j, 0)),
                  pl.BlockSpec((None, 2 * t, VDIM), lambda h, j: (h, j, 0)),
                  head_rows, head_rows,
                  pl.BlockSpec((None, seq, LANES), lambda h, j: (h, 0, 0))],
                 (pl.BlockSpec((None, seq, QK), lambda h, j: (h, 0, 0)),
                  pl.BlockSpec((None, 2 * t, QK), lambda h, j: (h, j, 0)),
                  pl.BlockSpec((None, 2 * t, VDIM), lambda h, j: (h, j, 0))),
                 (jax.ShapeDtypeStruct((N_HEADS, seq, QK), BF16),
                  jax.ShapeDtypeStruct((N_HEADS, seq, QK), BF16),
                  jax.ShapeDtypeStruct((N_HEADS, seq, VDIM), BF16)),
                 (q, k, v, o, do, lse),
                 scratch=[pltpu.VMEM((seq, QK), F32), pltpu.VMEM((2 * t, QK), F32),
                          pltpu.VMEM((2 * t, VDIM), F32)], rider=rider)


def loss_head(h, g, target, name):
    seq, d = h.shape
    tm = min(ROW_TILE, seq)

    def body(h_ref, g_ref, t_ref, l_ref, dh_ref, dg_ref):
        i = pl.program_id(0)

        @pl.when(i == 0)
        def _():
            l_ref[...] = jnp.zeros_like(l_ref)
            dg_ref[...] = jnp.zeros_like(dg_ref)
        y, xhat, rstd = _rms(h_ref[...], g_ref[...])
        diff = y - t_ref[...]
        l_ref[...] += jnp.sum(jnp.sum(diff * diff, axis=1, keepdims=True), axis=0, keepdims=True)
        dh, dg_rows = _rms_bwd(diff * (1.0 / d), xhat, rstd, g_ref[...])
        dh_ref[...] = dh
        dg_ref[...] += jnp.sum(dg_rows, axis=0, keepdims=True)

    row = pl.BlockSpec((tm, d), lambda i: (i, 0))
    vec = pl.BlockSpec((1, d), lambda i: (0, 0))
    return _call(body, name, (seq // tm,), [row, vec, row],
                 (pl.BlockSpec((1, LANES), lambda i: (0, 0)), row, vec),
                 (jax.ShapeDtypeStruct((1, LANES), F32), jax.ShapeDtypeStruct((seq, d), F32),
                  jax.ShapeDtypeStruct((1, d), F32)),
                 (h, g, target))[0]


def _pack(parts):
    rows = []
    for p in parts:
        flat = p.reshape(-1)
        n_rows = -(-flat.shape[0] // (8 * LANES)) * 8
        flat = jnp.pad(flat, (0, n_rows * LANES - flat.shape[0]))
        rows.append(flat.reshape(n_rows, LANES))
    return jnp.concatenate(rows, axis=0)


def _unpack(packed, shapes):
    lead = packed.shape[:-2]
    out, r0 = [], 0
    for shape in shapes:
        size = 1
        for s in shape:
            size *= s
        n_rows = -(-size // (8 * LANES)) * 8
        part = packed[..., r0:r0 + n_rows, :].reshape(lead + (n_rows * LANES,))
        out.append(part[..., :size].reshape(lead + tuple(shape)))
        r0 += n_rows
    return out


FWD_RIDERS = {
    "mixer_fwd0": [("ffn_w_up", 0)],
    "mixer_out0": [("w_dkv", 0), ("w_ukv", 0), ("b_w_dq", 0), ("b_w_uq", 0)],
    "ffn_fwd0": [("ffn_w_down", 0), ("a_w_in", 1), ("a_w_out", 1)],
    "ffn_out0": [("ffn_w_down", 1)],
    "mixer_fwd1": [("ffn_w_up", 1)],
    "mixer_out1": [("b_w_o", 0)],
    "ffn_fwd1": [("ffn_w_up", 2)],
    "ffn_out1": [("ffn_w_down", 2)],
    "kv_fwd": [("b_w_dq", 1), ("b_w_uq", 1)],
    "q_fwd0": [("b_w_o", 1)],
    "attn_fwd0": [("ffn_w_up", 3), ("ffn_w_down", 3)],
}
BWD_RIDERS = {
    "ffn_in_bwd3": [("ffn_w_down", 3)],
    "attn_bwd1": [("ffn_w_up", 3), ("b_w_o", 1)],
    "q_in_bwd1": [("b_w_dq", 1)],
    "ffn_bwd2": [("b_w_uq", 1)],
    "ffn_in_bwd2": [("ffn_w_down", 2)],
    "attn_bwd0": [("ffn_w_up", 2), ("b_w_o", 0)],
    "q_in_bwd0": [("b_w_dq", 0)],
    "kv_bwd": [("b_w_uq", 0)],
    "kv_in_bwd": [("w_ukv", 0)],
    "ffn_bwd1": [("w_dkv", 0)],
    "ffn_in_bwd1": [("ffn_w_down", 1), ("ffn_w_up", 1, "pair")],
    "mixer_in_bwd1": [("a_w_out", 1)],
    "ffn_bwd0": [("ffn_w_up", 1, "chip"), ("a_w_in", 1)],
    "ffn_in_bwd0": [("ffn_w_down", 0), ("ffn_w_up", 0, "pair")],
    "mixer_bwd0": [("ffn_w_up", 0, "chip")],
    "mixer_in_bwd0": [("a_w_out", 0), ("a_w_in", 0)],
}


def kernel(x, a_mix_norm, a_w_in, a_conv, a_w_out, b_mix_norm, b_w_dq, b_q_norm, b_w_uq, b_w_o, kv_in_norm, w_dkv, kv_norm, w_ukv, ffn_norm, ffn_w_up, ffn_conv, ffn_w_down, final_norm, loss_target, m_a_mix_norm, m_a_w_in, m_a_conv, m_a_w_out, m_b_mix_norm, m_b_w_dq, m_b_q_norm, m_b_w_uq, m_b_w_o, m_kv_in_norm, m_w_dkv, m_kv_norm, m_w_ukv, m_ffn_norm, m_ffn_w_up, m_ffn_conv, m_ffn_w_down, m_final_norm, v_a_mix_norm, v_a_w_in, v_a_conv, v_a_w_out, v_b_mix_norm, v_b_w_dq, v_b_q_norm, v_b_w_uq, v_b_w_o, v_kv_in_norm, v_w_dkv, v_kv_norm, v_w_ukv, v_ffn_norm, v_ffn_w_up, v_ffn_conv, v_ffn_w_down, v_final_norm):
    seq, d = x.shape[1], x.shape[2]
    me = 4 * lax.axis_index("x") + 2 * lax.axis_index("y") + lax.axis_index("c")
    h0 = x.reshape(seq, d)
    target = loss_target.reshape(seq, d)
    cos, sin = _rope_tables(seq)
    rank = b_w_dq.shape[-1]
    f8 = ffn_w_up.shape[-1]
    fd = ffn_w_down.shape[1]
    dshard = a_w_out.shape[1]
    hv = N_HEADS * VDIM

    shards = {"a_w_in": a_w_in, "a_w_out": a_w_out, "b_w_dq": b_w_dq, "b_w_uq": b_w_uq,
              "b_w_o": b_w_o, "w_dkv": w_dkv[None], "w_ukv": w_ukv[None],
              "ffn_w_up": ffn_w_up, "ffn_w_down": ffn_w_down}

    def relayout(name, g):
        if name == "a_w_in":
            w = jnp.transpose(g, (1, 0, 2)).reshape(d, 3, d)
            return jnp.transpose(w, (1, 0, 2))
        if name == "a_w_out":
            return g.reshape(d, d)
        if name == "b_w_dq":
            return g.reshape(d, rank)
        if name == "b_w_uq":
            return jnp.pad(g, ((0, 0), (0, 0), (0, QK - NOPE - ROPE)))
        if name == "b_w_o":
            return g.reshape(hv, d)
        if name == "w_dkv":
            return jnp.pad(g.reshape(d, KV_RANK + ROPE), ((0, 0), (0, ROPE_PAD - ROPE)))
        if name == "ffn_w_down":
            return g.reshape(N_DEV // 2, 2 * fd, d)
        return g

    weights = {}

    def ag_rider(host):
        return [("ag", shards[n][l].astype(BF16)) for n, l in FWD_RIDERS.get(host, [])]

    def ag_done(host, outs):
        for (n, l), g in zip(FWD_RIDERS.get(host, []), outs):
            weights[n, l] = relayout(n, g)

    small_shapes = [a_mix_norm.shape, a_conv.shape, ffn_conv.shape]
    first = exchange([("ag", a_w_in[0].astype(BF16)), ("ag", a_w_out[0].astype(BF16)),
                      ("ag", _pack([a_mix_norm, a_conv, ffn_conv]))], "ag_first")
    weights["a_w_in", 0] = relayout("a_w_in", first[0])
    weights["a_w_out", 0] = relayout("a_w_out", first[1])
    s_mix, s_aconv, s_fconv = _unpack(first[2], small_shapes)
    a_gain = jnp.transpose(s_mix, (1, 0, 2)).reshape(N_A, d)
    a_cw = jnp.transpose(s_aconv, (1, 2, 0, 3)).reshape(N_A, 3, d)
    f_cw = jnp.transpose(s_fconv, (1, 0, 2, 3))

    def mixer_gain(layer):
        if layer >= DEPTH:
            return None
        return a_gain[layer][None] if layer < N_A else b_mix_norm[layer - N_A][None]

    saved = {}
    h = h0
    xn = norm_fwd(h, mixer_gain(0), "norm_first")
    kv = None
    for layer in range(DEPTH):
        saved["hm", layer], saved["xm", layer] = h, xn
        if layer < N_A:
            name = f"mixer_fwd{layer}"
            (u4, z), r = mixer_fwd(xn, weights["a_w_in", layer], a_cw[layer], name, rider=ag_rider(name))
            ag_done(name, r)
            saved["mix", layer] = (u4, z)
            name = f"mixer_out{layer}"
            (h, xn), r = proj_residual(z[None], weights["a_w_out", layer][None], h, name,
                                       g_next=ffn_norm[layer][None], rider=ag_rider(name))
            ag_done(name, r)
        else:
            j = layer - N_A
            name = f"q_fwd{j}"
            (q,), r = q_fwd(xn, weights["b_w_dq", j], b_q_norm[j][None], weights["b_w_uq", j],
                            cos, sin, name, rider=ag_rider(name))
            ag_done(name, r)
            name = f"attn_fwd{j}"
            (o, lse), r = attn_fwd(q, kv[0], kv[1], name, rider=ag_rider(name))
            ag_done(name, r)
            saved["attn", layer] = (q, o, lse)
            name = f"attn_out{j}"
            (h, xn), r = proj_residual(o[None], weights["b_w_o", j][None], h, name,
                                       g_next=ffn_norm[layer][None], rider=ag_rider(name))
            ag_done(name, r)
        saved["hf", layer], saved["xf", layer] = h, xn
        name = f"ffn_fwd{layer}"
        (up2, cv2, act), r = ffn_fwd(xn, weights["ffn_w_up", layer], f_cw[layer], name, rider=ag_rider(name))
        ag_done(name, r)
        saved["ffn", layer] = (up2, cv2, act)
        name = f"ffn_out{layer}"
        (h, xn), r = proj_residual(act, weights["ffn_w_down", layer], h, name,
                                   g_next=mixer_gain(layer + 1), rider=ag_rider(name))
        ag_done(name, r)
        if layer == N_A - 1:
            (k_all, v_all, c_kv), r = kv_fwd(h, kv_in_norm[None], weights["w_dkv", 0], kv_norm[None],
                                             weights["w_ukv", 0], cos, sin, "kv_fwd",
                                             rider=ag_rider("kv_fwd"))
            ag_done("kv_fwd", r)
            kv = (k_all, v_all, c_kv)

    sq_err, dh, d_final = loss_head(h, final_norm[None], target, "loss_head")
    loss = lax.psum(sq_err[0, 0] * (0.5 / d), ("x", "y", "c"))

    grads = {}
    parts = {}

    pair_sums = {}

    def by_chip(g):
        return g.reshape((N_DEV // 2, 2) + g.shape[1:])

    def rs_rider(host):
        tasks = []
        for key in BWD_RIDERS.get(host, []):
            if len(key) == 2:
                tasks.append(("rs", grads[key]))
            elif key[2] == "pair":
                tasks.append(("rs_pair", by_chip(grads[key[:2]])))
            else:
                tasks.append(("rs_chip", pair_sums[key[:2]]))
        return tasks

    def rs_done(host, outs):
        for key, p in zip(BWD_RIDERS.get(host, []), outs):
            if len(key) == 3 and key[2] == "pair":
                pair_sums[key[:2]] = pair_sum(by_chip(grads[key[:2]]), p, f"pair_sum_{key[0]}{key[1]}")
            else:
                parts[key[:2]] = p

    d_ffn_norm = [None] * DEPTH
    d_fconv = [None] * DEPTH
    d_a_gain = [None] * N_A
    d_aconv = [None] * N_A
    d_b_gain = [None] * N_B
    d_q_gain = [None] * N_B
    dks, dvs = [], []
    for layer in reversed(range(DEPTH)):
        if layer == N_A - 1:
            hk = saved["hm", layer + 1]
            (dckv, dwukv, dwdkv, d_kv_gain), r = kv_bwd(
                dks, dvs, kv[2], hk, kv_in_norm[None], kv_norm[None], weights["w_ukv", 0],
                cos, sin, "kv_bwd", rider=rs_rider("kv_bwd"))
            rs_done("kv_bwd", r)
            grads["w_ukv", 0] = dwukv
            grads["w_dkv", 0] = dwdkv[:, :KV_RANK + ROPE].reshape(N_DEV, dshard, KV_RANK + ROPE)
            (dh, d_kvin_gain), r = proj_t_rms_bwd(dckv, weights["w_dkv", 0][None], hk, kv_in_norm[None],
                                                  dh, "kv_in_bwd", rider=rs_rider("kv_in_bwd"))
            rs_done("kv_in_bwd", r)
        up2, cv2, act = saved["ffn", layer]
        name = f"ffn_bwd{layer}"
        (dup2, dwup, dwdown, dcw), r = ffn_bwd(dh, weights["ffn_w_down", layer], up2, cv2, act,
                                               saved["xf", layer], f_cw[layer], name, rider=rs_rider(name))
        rs_done(name, r)
        grads["ffn_w_up", layer] = dwup.reshape(N_DEV, d, f8)
        grads["ffn_w_down", layer] = dwdown.reshape(N_DEV, fd, d)
        d_fconv[layer] = dcw.reshape(N_DEV, 3, f8)
        name = f"ffn_in_bwd{layer}"
        (dh, d_ffn_norm[layer]), r = proj_t_rms_bwd(dup2.reshape(N_DEV, seq, f8), weights["ffn_w_up", layer],
                                                    saved["hf", layer], ffn_norm[layer][None], dh, name,
                                                    rider=rs_rider(name))
        rs_done(name, r)
        hm, xm = saved["hm", layer], saved["xm", layer]
        if layer < N_A:
            u4, z = saved["mix", layer]
            name = f"mixer_bwd{layer}"
            (du3, dwin3, dwout, dcw), r = mixer_bwd(dh, weights["a_w_out", layer], u4, z, xm, a_cw[layer],
                                                    name, rider=rs_rider(name))
            rs_done(name, r)
            dwin = jnp.transpose(dwin3, (1, 0, 2)).reshape(d, N_DEV, 3 * d // N_DEV)
            grads["a_w_in", layer] = jnp.transpose(dwin, (1, 0, 2))
            grads["a_w_out", layer] = dwout.reshape(N_DEV, dshard, d)
            d_aconv[layer] = dcw
            name = f"mixer_in_bwd{layer}"
            (dh, d_a_gain[layer]), r = proj_t_rms_bwd(du3, weights["a_w_in", layer], hm, a_gain[layer][None],
                                                      dh, name, rider=rs_rider(name))
            rs_done(name, r)
        else:
            j = layer - N_A
            q, o, lse = saved["attn", layer]
            name = f"attn_out_bwd{j}"
            (do, dwo), r = o_bwd(dh, o, weights["b_w_o", j], name, rider=rs_rider(name))
            rs_done(name, r)
            grads["b_w_o", j] = dwo.reshape(N_DEV, dshard, d)
            name = f"attn_bwd{j}"
            (dq, dk, dv), r = attn_bwd(q, kv[0], kv[1], o, do, lse, name, rider=rs_rider(name))
            rs_done(name, r)
            dks.append(dk)
            dvs.append(dv)
            name = f"q_bwd{j}"
            (dqc, dwuq, dwdq, d_q_gain[j]), r = q_bwd(dq, xm, weights["b_w_dq", j], b_q_norm[j][None],
                                                      weights["b_w_uq", j], cos, sin, name, rider=rs_rider(name))
            rs_done(name, r)
            grads["b_w_uq", j] = dwuq[:, :, :NOPE + ROPE]
            grads["b_w_dq", j] = dwdq.reshape(N_DEV, dshard, rank)
            name = f"q_in_bwd{j}"
            (dh, d_b_gain[j]), r = proj_t_rms_bwd(dqc, weights["b_w_dq", j][None], hm, b_mix_norm[j][None],
                                                  dh, name, rider=rs_rider(name))
            rs_done(name, r)
    grad_x = dh.reshape(x.shape)

    full_small = [
        jnp.concatenate(d_a_gain, axis=0),
        jnp.stack(d_aconv),
        jnp.concatenate(d_b_gain, axis=0),
        jnp.concatenate(d_q_gain, axis=0),
        d_kvin_gain[0],
        d_kv_gain[0],
        jnp.concatenate(d_ffn_norm, axis=0),
        jnp.stack(d_fconv),
        d_final[0],
    ]
    full_shapes = [t.shape for t in full_small]
    small_pack = _pack(full_small)

    res = {}

    def update(name, n_layers, w, m, v, extra=(), transposed=False):
        view = (lambda t: jnp.transpose(t, (0, 2, 1))) if transposed else (lambda t: t)
        call = sum_adamw_transposed if transposed else sum_adamw
        shard = w.shape if w.ndim == 3 else (1,) + w.shape
        host = f"adamw_{name}"
        outs, r = call([parts[name, l] for l in range(n_layers)], view(w.reshape(shard)),
                       view(m.reshape(shard)), view(v.reshape(shard)), host,
                       rider=rs_rider(host) + list(extra))
        rs_done(host, r)
        res[name] = [view(t).reshape(w.shape) for t in outs]
        return r[len(BWD_RIDERS.get(host, [])):]

    (g_parts,) = update("a_w_out", N_A, a_w_out, m_a_w_out, v_a_w_out, extra=[("ag", small_pack)])
    update("ffn_w_down", DEPTH, ffn_w_down, m_ffn_w_down, v_ffn_w_down)
    update("ffn_w_up", DEPTH, ffn_w_up, m_ffn_w_up, v_ffn_w_up, transposed=True)
    update("b_w_dq", N_B, b_w_dq, m_b_w_dq, v_b_w_dq)
    update("b_w_uq", N_B, b_w_uq, m_b_w_uq, v_b_w_uq)
    update("b_w_o", N_B, b_w_o, m_b_w_o, v_b_w_o)
    update("w_dkv", 1, w_dkv, m_w_dkv, v_w_dkv)
    update("w_ukv", 1, w_ukv, m_w_ukv, v_w_ukv)
    update("a_w_in", N_A, a_w_in, m_a_w_in, v_a_w_in)

    summed = sum_slots(g_parts, "sum_small_grads")
    (s_a_gain, s_aconv_g, s_b_gain, s_q_gain, s_kvin, s_kvn, s_ffn_gain, s_fconv_g,
     s_final) = _unpack(summed, full_shapes)
    dsl = d // N_DEV
    small = [
        ("a_mix_norm", lax.dynamic_slice_in_dim(s_a_gain, me * dsl, dsl, axis=1), a_mix_norm, m_a_mix_norm, v_a_mix_norm),
        ("a_conv", lax.dynamic_slice_in_dim(s_aconv_g, me * dsl, dsl, axis=2), a_conv, m_a_conv, v_a_conv),
        ("b_mix_norm", s_b_gain, b_mix_norm, m_b_mix_norm, v_b_mix_norm),
        ("b_q_norm", s_q_gain, b_q_norm, m_b_q_norm, v_b_q_norm),
        ("kv_in_norm", s_kvin, kv_in_norm, m_kv_in_norm, v_kv_in_norm),
        ("kv_norm", s_kvn, kv_norm, m_kv_norm, v_kv_norm),
        ("ffn_norm", s_ffn_gain, ffn_norm, m_ffn_norm, v_ffn_norm),
        ("ffn_conv", lax.dynamic_index_in_dim(s_fconv_g, me, axis=1, keepdims=False), ffn_conv, m_ffn_conv, v_ffn_conv),
        ("final_norm", s_final, final_norm, m_final_norm, v_final_norm),
    ]
    shapes = [t[2].shape for t in small]
    packed = [_pack([t[k] for t in small])[None] for k in (1, 2, 3, 4)]
    outs, _ = sum_adamw([packed[0]], packed[1], packed[2], packed[3], "adamw_small")
    unpacked = [_unpack(t[0], shapes) for t in outs]
    for idx, t in enumerate(small):
        res[t[0]] = [unpacked[k][idx] for k in range(4)]

    order = ["a_mix_norm", "a_w_in", "a_conv", "a_w_out", "b_mix_norm", "b_w_dq", "b_q_norm",
             "b_w_uq", "b_w_o", "kv_in_norm", "w_dkv", "kv_norm", "w_ukv", "ffn_norm",
             "ffn_w_up", "ffn_conv", "ffn_w_down", "final_norm"]
    return (loss, grad_x, *[res[n][0] for n in order], *[res[n][1] for n in order],
            *[res[n][2] for n in order], *[res[n][3] for n in order])
```

```python
import functools

import jax
import jax.numpy as jnp
from jax import lax
from jax.experimental import pallas as pl
from jax.experimental.pallas import tpu as pltpu

F32 = jnp.float32
BF16 = jnp.bfloat16

N_DEV = 8
N_HEADS = 8
NOPE = 128
ROPE = 64
ROPE_PAD = 128
QK = NOPE + ROPE_PAD
VDIM = 128
KV_RANK = 256
ROPE_THETA = 10000.0
RMS_EPS = 1e-6
ATTN_SCALE = (NOPE + ROPE) ** -0.5
N_A = 2
N_B = 2
DEPTH = 4

ADAM_LR = 0.001
ADAM_B1 = 0.9
ADAM_B2 = 0.999
ADAM_EPS = 1e-08
ADAM_WD = 0.01
ADAM_STEP = 10

V7X_VMEM_LIMIT = 56 * 1024 * 1024
BF16_SUBLANES = 16
ROW_TILE = 512
ROW_TILE_SMALL = 256
ROW_TILE_LARGE = 1024
ATTN_TILE = 512
MIXER_CHUNK = 512
LANES = 128
NEG_BIG = -1e30
COPIES_PER_TASK = 7

MESH_ID = pl.DeviceIdType.MESH
ANY = pl.BlockSpec(memory_space=pl.ANY)


def _nt(a, b):
    return lax.dot_general(a, b, (((1,), (1,)), ((), ())), preferred_element_type=F32)


def _tn(a, b):
    return lax.dot_general(a, b, (((0,), (0,)), ((), ())), preferred_element_type=F32)


def _nn(a, b):
    return jnp.dot(a, b, preferred_element_type=F32)


def _rms(h, g):
    rstd = lax.rsqrt(jnp.mean(h * h, axis=-1, keepdims=True) + RMS_EPS)
    xhat = h * rstd
    return xhat * g, xhat, rstd


def _rms_bwd(dxn, xhat, rstd, g):
    dxhat = dxn * g
    dh = rstd * (dxhat - xhat * jnp.mean(dxhat * xhat, axis=-1, keepdims=True))
    return dh, dxn * xhat


def _shift_down(x, k, halo_rows):
    r = pltpu.roll(x, k, 0)
    row = lax.broadcasted_iota(jnp.int32, x.shape, 0)
    for t in range(k):
        r = jnp.where(row == t, halo_rows[t], r)
    return r


def _shift_up(x, k, halo_rows):
    n = x.shape[0]
    r = pltpu.roll(x, n - k, 0)
    row = lax.broadcasted_iota(jnp.int32, x.shape, 0)
    for t in range(k):
        r = jnp.where(row == n - k + t, halo_rows[t], r)
    return r


def _conv_taps(w_ref):
    return w_ref[0:1, :], w_ref[1:2, :], w_ref[2:3, :]


def _rope_swap(x):
    lane = lax.broadcasted_iota(jnp.int32, x.shape, 1)
    return jnp.where(lane < ROPE // 2, pltpu.roll(x, ROPE_PAD - ROPE // 2, 1),
                     pltpu.roll(x, ROPE // 2, 1))


def _rope_fwd(x, cos, sin):
    return x * cos + _rope_swap(x) * sin


def _rope_bwd(dy, cos, sin):
    return dy * cos - _rope_swap(dy) * sin


def _rope_tables(seq):
    inv = 1.0 / (ROPE_THETA ** (jnp.arange(0, ROPE, 2, dtype=F32) / ROPE))
    ang = jnp.arange(seq, dtype=F32)[:, None] * inv[None, :]
    cos, sin = jnp.cos(ang), jnp.sin(ang)
    zero = jnp.zeros((seq, ROPE_PAD - ROPE), F32)
    return (jnp.concatenate([cos, cos, zero], axis=1),
            jnp.concatenate([-sin, sin, zero], axis=1))


def _row_tile(rows, cap, mult=8):
    best = None
    for t in range(mult, min(rows, cap) + 1, mult):
        if rows % t == 0:
            best = t
    return rows if best is None else best


class _AllGatherTask:
    def __init__(self, t, x_ref, out_ref, send_sems, recv_sems, local_sems):
        self.t, self.x_ref, self.out_ref = t, x_ref, out_ref
        self.send_sems, self.recv_sems, self.local_sems = send_sems, recv_sems, local_sems
        mx, my, mc = lax.axis_index("x"), lax.axis_index("y"), lax.axis_index("c")
        self.mc = mc
        self.me, self.sibling = (mx, my, mc), (mx, my, 1 - mc)
        self.chips = [(1 - mx, my), (mx, 1 - my), (1 - mx, 1 - my)]

    def _slot(self, px, py, pc):
        return self.out_ref.at[4 * px + 2 * py + pc]

    def _copy(self, k, block, to, src=None):
        s = COPIES_PER_TASK * self.t + k
        return pltpu.make_async_remote_copy(
            src_ref=self._slot(*block) if src is None else src, dst_ref=self._slot(*block),
            send_sem=self.send_sems.at[s], recv_sem=self.recv_sems.at[s],
            device_id=to, device_id_type=MESH_ID)

    def _mine(self):
        return pltpu.make_async_copy(self.x_ref, self._slot(*self.me), self.local_sems.at[self.t])

    def _first(self):
        out = [self._copy(0, self.me, self.sibling, src=self.x_ref)]
        out += [self._copy(1 + j, self.me, (*chip, self.mc), src=self.x_ref)
                for j, chip in enumerate(self.chips)]
        return out

    def _passed(self):
        return [self._copy(4 + j, (*chip, self.mc), self.sibling) for j, chip in enumerate(self.chips)]

    def start(self):
        self._mine().start()
        for cp in self._first():
            cp.start()

    def forward(self):
        passed = self._passed()
        for j, chip in enumerate(self.chips):
            self._copy(1 + j, (*chip, self.mc), self.me).wait_recv()
            passed[j].start()

    def finish(self):
        self._copy(0, self.sibling, self.me).wait_recv()
        for j, chip in enumerate(self.chips):
            self._copy(4 + j, (*chip, 1 - self.mc), self.me).wait_recv()
        for cp in self._first() + self._passed():
            cp.wait_send()
        self._mine().wait()


class _ReduceScatterTask:
    def __init__(self, t, g_ref, out_ref, send_sems, recv_sems, local_sems):
        self.t, self.g_ref, self.out_ref = t, g_ref, out_ref
        self.send_sems, self.recv_sems, self.local_sems = send_sems, recv_sems, local_sems
        mx, my, mc = lax.axis_index("x"), lax.axis_index("y"), lax.axis_index("c")
        self.me = 4 * mx + 2 * my + mc
        self.peers = []
        for k in range(1, N_DEV):
            px, py, pc = mx ^ ((k >> 2) & 1), my ^ ((k >> 1) & 1), mc ^ (k & 1)
            self.peers.append(((px, py, pc), 4 * px + 2 * py + pc))

    def _mine(self):
        return pltpu.make_async_copy(self.g_ref.at[self.me], self.out_ref.at[self.me],
                                     self.local_sems.at[self.t])

    def _copy(self, k, src_slot, dst_slot):
        s = COPIES_PER_TASK * self.t + k
        return pltpu.make_async_remote_copy(
            src_ref=self.g_ref.at[src_slot], dst_ref=self.out_ref.at[dst_slot],
            send_sem=self.send_sems.at[s], recv_sem=self.recv_sems.at[s],
            device_id=self.peers[k][0], device_id_type=MESH_ID)

    def start(self):
        self._mine().start()
        for k, (_, peer) in enumerate(self.peers):
            self._copy(k, peer, self.me).start()

    def forward(self):
        pass

    def finish(self):
        for k, (_, peer) in enumerate(self.peers):
            self._copy(k, self.me, peer).wait_recv()
        for k, (_, peer) in enumerate(self.peers):
            self._copy(k, peer, self.me).wait_send()
        self._mine().wait()


class _PairExchangeTask:
    def __init__(self, t, g_ref, out_ref, send_sems, recv_sems, local_sems):
        mx, my, mc = lax.axis_index("x"), lax.axis_index("y"), lax.axis_index("c")
        s = COPIES_PER_TASK * t
        self.copy = pltpu.make_async_remote_copy(
            src_ref=g_ref.at[:, 1 - mc], dst_ref=out_ref,
            send_sem=send_sems.at[s], recv_sem=recv_sems.at[s],
            device_id=(mx, my, 1 - mc), device_id_type=MESH_ID)

    def start(self):
        self.copy.start()

    def forward(self):
        pass

    def finish(self):
        self.copy.wait()


class _ChipScatterTask:
    def __init__(self, t, s_ref, out_ref, send_sems, recv_sems, local_sems):
        self.t, self.s_ref, self.out_ref = t, s_ref, out_ref
        self.send_sems, self.recv_sems, self.local_sems = send_sems, recv_sems, local_sems
        mx, my, mc = lax.axis_index("x"), lax.axis_index("y"), lax.axis_index("c")
        self.chip = 2 * mx + my
        self.peers = []
        for k in range(1, N_DEV // 2):
            px, py = mx ^ ((k >> 1) & 1), my ^ (k & 1)
            self.peers.append(((px, py, mc), 2 * px + py))

    def _mine(self):
        return pltpu.make_async_copy(self.s_ref.at[self.chip], self.out_ref.at[self.chip],
                                     self.local_sems.at[self.t])

    def _copy(self, k, src_slot, dst_slot):
        s = COPIES_PER_TASK * self.t + k
        return pltpu.make_async_remote_copy(
            src_ref=self.s_ref.at[src_slot], dst_ref=self.out_ref.at[dst_slot],
            send_sem=self.send_sems.at[s], recv_sem=self.recv_sems.at[s],
            device_id=self.peers[k][0], device_id_type=MESH_ID)

    def start(self):
        self._mine().start()
        for k, (_, peer) in enumerate(self.peers):
            self._copy(k, peer, self.chip).start()

    def forward(self):
        pass

    def finish(self):
        for k, (_, peer) in enumerate(self.peers):
            self._copy(k, self.chip, peer).wait_recv()
        for k, (_, peer) in enumerate(self.peers):
            self._copy(k, peer, self.chip).wait_send()
        self._mine().wait()


_TASKS = {"ag": _AllGatherTask, "rs": _ReduceScatterTask, "rs_pair": _PairExchangeTask,
          "rs_chip": _ChipScatterTask}


def _task_shape(kind, arr):
    shape = {"ag": (N_DEV,) + arr.shape, "rs": arr.shape, "rs_chip": arr.shape,
             "rs_pair": arr.shape[:1] + arr.shape[2:]}[kind]
    return jax.ShapeDtypeStruct(shape, arr.dtype)


def _sem_shapes(n_tasks):
    return [pltpu.SemaphoreType.DMA((COPIES_PER_TASK * n_tasks,)),
            pltpu.SemaphoreType.DMA((COPIES_PER_TASK * n_tasks,)),
            pltpu.SemaphoreType.DMA((n_tasks,))]


def _make_tasks(rider, in_refs, out_refs, sems):
    return [_TASKS[kind](t, in_refs[t], out_refs[t], *sems) for t, (kind, _) in enumerate(rider)]


def exchange(rider, name):
    n = len(rider)

    def body(*refs):
        tasks = _make_tasks(rider, refs[:n], refs[n:2 * n], refs[2 * n:])
        for task in tasks:
            task.start()
        for task in tasks:
            task.forward()
        for task in tasks:
            task.finish()

    return list(pl.pallas_call(
        body, name=name, out_shape=tuple(_task_shape(k, a) for k, a in rider),
        in_specs=[ANY] * n, out_specs=(ANY,) * n, scratch_shapes=_sem_shapes(n),
    )(*[a for _, a in rider]))


def _call(body, name, grid, in_specs, out_specs, out_shape, args, scratch=(), rider=()):
    in_specs, out_specs, out_shape = list(in_specs), tuple(out_specs), tuple(out_shape)
    n_in, n_out, n_scr, n_r = len(in_specs), len(out_specs), len(scratch), len(rider)
    if n_r:
        def kern(*refs):
            ins, r_in = refs[:n_in], refs[n_in:n_in + n_r]
            o0 = n_in + n_r
            outs, r_out = refs[o0:o0 + n_out], refs[o0 + n_out:o0 + n_out + n_r]
            s0 = o0 + n_out + n_r
            scr, sems = refs[s0:s0 + n_scr], refs[s0 + n_scr:]
            step = 0
            for a, n in enumerate(grid):
                step = step * n + pl.program_id(a)
            n_steps = 1
            for n in grid:
                n_steps *= n

            @pl.when(step == 0)
            def _():
                for task in _make_tasks(rider, r_in, r_out, sems):
                    task.start()
            body(*ins, *outs, *scr)

            @pl.when(step == n_steps - 1)
            def _():
                tasks = _make_tasks(rider, r_in, r_out, sems)
                for task in tasks:
                    task.forward()
                for task in tasks:
                    task.finish()
    else:
        kern = body
    res = pl.pallas_call(
        kern, name=name, grid=grid,
        in_specs=in_specs + [ANY] * n_r, out_specs=out_specs + (ANY,) * n_r,
        out_shape=out_shape + tuple(_task_shape(k, a) for k, a in rider),
        scratch_shapes=list(scratch) + (_sem_shapes(n_r) if n_r else []),
        compiler_params=pltpu.CompilerParams(dimension_semantics=("arbitrary",) * len(grid),
                                             vmem_limit_bytes=V7X_VMEM_LIMIT),
    )(*args, *[a for _, a in rider])
    return list(res[:n_out]), list(res[n_out:])


def _adamw(g, w, m, v):
    m = ADAM_B1 * m + (1.0 - ADAM_B1) * g
    v = ADAM_B2 * v + (1.0 - ADAM_B2) * (g * g)
    m_hat = m / (1.0 - ADAM_B1 ** ADAM_STEP)
    v_hat = v / (1.0 - ADAM_B2 ** ADAM_STEP)
    delta = -ADAM_LR * (m_hat / (jnp.sqrt(v_hat) + ADAM_EPS) + ADAM_WD * w)
    return delta, m, v


def sum_adamw(parts, w, m, v, name, rider=()):
    n_l, rows, cols = w.shape
    n = parts[0].shape[0]
    mult = BF16_SUBLANES if parts[0].dtype == BF16 else 8
    tr = _row_tile(rows, 128, mult)
    n_i = rows // tr

    def body(*refs):
        part_refs = refs[:n_l]
        w_ref, m_ref, v_ref, g_out, d_out, m_out, v_out = refs[n_l:]
        layer = pl.program_id(0)
        for k in range(n_l):
            @pl.when(layer == k)
            def _(k=k):
                g = part_refs[k][0].astype(F32)
                for s in range(1, n):
                    g = g + part_refs[k][s].astype(F32)
                delta, m_new, v_new = _adamw(g, w_ref[...], m_ref[...], v_ref[...])
                g_out[...] = g
                d_out[...] = delta
                m_out[...] = m_new
                v_out[...] = v_new

    part_specs = [pl.BlockSpec((n, tr, cols), functools.partial(
        lambda l, i, k: (0, jnp.where(l == k, i, 0), 0), k=k)) for k in range(n_l)]
    wspec = pl.BlockSpec((None, tr, cols), lambda l, i: (l, i, 0))
    shape = jax.ShapeDtypeStruct(w.shape, F32)
    return _call(body, name, (n_l, n_i), part_specs + [wspec] * 3, (wspec,) * 4, (shape,) * 4,
                 (*parts, w, m, v), rider=rider)


def sum_adamw_transposed(parts, w_t, m_t, v_t, name, rider=()):
    n_l, cols, rows = w_t.shape
    tr = LANES
    n_i = rows // tr
    starts = list(range(0, cols - LANES + 1, LANES))
    if starts[-1] + LANES < cols:
        starts.append(cols - LANES)

    def body(*refs):
        part_refs = refs[:n_l]
        w_ref, m_ref, v_ref, g_out, d_out, m_out, v_out = refs[n_l:]
        layer = pl.program_id(0)
        for k in range(n_l):
            @pl.when(layer == k)
            def _(k=k):
                for c0 in starts:
                    piece = pl.ds(c0, LANES)
                    g = part_refs[k][0, :, piece].astype(F32)
                    for s in range(1, parts[k].shape[0]):
                        g = g + part_refs[k][s, :, piece].astype(F32)
                    g = g.T
                    delta, m_new, v_new = _adamw(g, w_ref[piece, :], m_ref[piece, :], v_ref[piece, :])
                    g_out[piece, :] = g
                    d_out[piece, :] = delta
                    m_out[piece, :] = m_new
                    v_out[piece, :] = v_new

    part_specs = [pl.BlockSpec((parts[k].shape[0], tr, cols), functools.partial(
        lambda l, i, k: (0, jnp.where(l == k, i, 0), 0), k=k)) for k in range(n_l)]
    wspec = pl.BlockSpec((None, cols, tr), lambda l, i: (l, 0, i))
    shape = jax.ShapeDtypeStruct(w_t.shape, F32)
    return _call(body, name, (n_l, n_i), part_specs + [wspec] * 3, (wspec,) * 4, (shape,) * 4,
                 (*parts, w_t, m_t, v_t), rider=rider)


def pair_sum(g4, other, name):
    n_chip, _, rows, cols = g4.shape
    tr = _row_tile(rows, 256, BF16_SUBLANES)

    def body(g_ref, o_ref, s_ref):
        mine = g_ref[lax.axis_index("c")]
        s_ref[...] = (mine.astype(F32) + o_ref[...].astype(F32)).astype(BF16)

    blk = pl.BlockSpec((None, tr, cols), lambda k, i: (k, i, 0))
    return _call(body, name, (n_chip, rows // tr),
                 [pl.BlockSpec((None, 2, tr, cols), lambda k, i: (k, 0, i, 0)), blk], [blk],
                 [jax.ShapeDtypeStruct((n_chip, rows, cols), g4.dtype)], (g4, other))[0][0]


def sum_slots(parts, name):
    n, rows, cols = parts.shape

    def body(p_ref, o_ref):
        acc = p_ref[0]
        for s in range(1, n):
            acc = acc + p_ref[s]
        o_ref[...] = acc

    return pl.pallas_call(
        body, name=name, out_shape=jax.ShapeDtypeStruct((rows, cols), F32),
        in_specs=[pl.BlockSpec(memory_space=pltpu.VMEM)],
        out_specs=pl.BlockSpec(memory_space=pltpu.VMEM),
    )(parts)


def norm_fwd(h, g, name):
    seq, d = h.shape
    tm = min(ROW_TILE, seq)

    def body(h_ref, g_ref, o_ref):
        o_ref[...] = _rms(h_ref[...], g_ref[...])[0].astype(BF16)

    return _call(body, name, (seq // tm,),
                 [pl.BlockSpec((tm, d), lambda i: (i, 0)), pl.BlockSpec((1, d), lambda i: (0, 0))],
                 [pl.BlockSpec((tm, d), lambda i: (i, 0))],
                 [jax.ShapeDtypeStruct((seq, d), BF16)], (h, g))[0][0]


def proj_residual(a, w, res, name, g_next=None, rider=()):
    nb, seq, kb = a.shape
    d = w.shape[-1]
    tm = min(ROW_TILE, seq)
    with_norm = g_next is not None

    def body(a_ref, w_ref, r_ref, *rest):
        acc = r_ref[...]
        for b in range(nb):
            acc = acc + _nn(a_ref[b], w_ref[b])
        if with_norm:
            g_ref, o_ref, xn_ref = rest
            xn_ref[...] = _rms(acc, g_ref[...])[0].astype(BF16)
        else:
            (o_ref,) = rest
        o_ref[...] = acc

    row = pl.BlockSpec((tm, d), lambda i: (i, 0))
    in_specs = [pl.BlockSpec((nb, tm, kb), lambda i: (0, i, 0)),
                pl.BlockSpec((nb, kb, d), lambda i: (0, 0, 0)), row]
    args = [a, w, res]
    out_specs, out_shape = [row], [jax.ShapeDtypeStruct((seq, d), F32)]
    if with_norm:
        in_specs.append(pl.BlockSpec((1, d), lambda i: (0, 0)))
        args.append(g_next)
        out_specs.append(row)
        out_shape.append(jax.ShapeDtypeStruct((seq, d), BF16))
    outs, r_outs = _call(body, name, (seq // tm,), in_specs, out_specs, out_shape, args, rider=rider)
    return (outs[0], outs[1] if with_norm else None), r_outs


def proj_t_rms_bwd(du, w, h, g, dres, name, rider=()):
    nb, seq, wd = du.shape
    k = w.shape[1]
    big_weight = 2 * w.size * w.dtype.itemsize > V7X_VMEM_LIMIT // 4
    tm = min(ROW_TILE_SMALL if big_weight else ROW_TILE, seq)

    def body(du_ref, w_ref, h_ref, g_ref, dr_ref, dh_ref, dg_ref):
        i = pl.program_id(0)
        dxn = _nt(du_ref[0], w_ref[0])
        for b in range(1, nb):
            dxn = dxn + _nt(du_ref[b], w_ref[b])
        _, xhat, rstd = _rms(h_ref[...], g_ref[...])
        dh, dg_rows = _rms_bwd(dxn, xhat, rstd, g_ref[...])
        dh_ref[...] = dr_ref[...] + dh

        @pl.when(i == 0)
        def _():
            dg_ref[...] = jnp.zeros_like(dg_ref)
        dg_ref[...] += jnp.sum(dg_rows, axis=0, keepdims=True)

    row = pl.BlockSpec((tm, k), lambda i: (i, 0))
    vec = pl.BlockSpec((1, k), lambda i: (0, 0))
    return _call(body, name, (seq // tm,),
                 [pl.BlockSpec((nb, tm, wd), lambda i: (0, i, 0)),
                  pl.BlockSpec((nb, k, wd), lambda i: (0, 0, 0)), row, vec, row],
                 (row, vec),
                 (jax.ShapeDtypeStruct((seq, k), F32), jax.ShapeDtypeStruct((1, k), F32)),
                 (du, w, h, g, dres), rider=rider)


def mixer_fwd(xn, win3, cw, name, rider=()):
    seq, d = xn.shape
    tm = min(ROW_TILE_LARGE, seq)
    cc = min(MIXER_CHUNK, d)
    n_c, n_i = d // cc, seq // tm

    def body(x_ref, w_ref, cw_ref, u_ref, z_ref, carry):
        i = pl.program_id(1)

        @pl.when(i == 0)
        def _():
            carry[...] = jnp.zeros_like(carry)
        xb = x_ref[...]
        b = _nn(xb, w_ref[0])
        c = _nn(xb, w_ref[1])
        hh = _nn(xb, w_ref[2])
        p = c * hh
        w0, w1, w2 = _conv_taps(cw_ref)
        p1 = _shift_down(p, 1, [carry[7:8, :]])
        p2 = _shift_down(p, 2, [carry[6:7, :], carry[7:8, :]])
        q = w0 * p2 + w1 * p1 + w2 * p
        carry[...] = p[tm - 8:tm, :]
        u_ref[0] = b.astype(BF16)
        u_ref[1] = c.astype(BF16)
        u_ref[2] = hh.astype(BF16)
        u_ref[3] = q.astype(BF16)
        z_ref[...] = (b * q).astype(BF16)

    return _call(body, name, (n_c, n_i),
                 [pl.BlockSpec((tm, d), lambda c, i: (i, 0)),
                  pl.BlockSpec((3, d, cc), lambda c, i: (0, 0, c)),
                  pl.BlockSpec((3, cc), lambda c, i: (0, c))],
                 (pl.BlockSpec((4, tm, cc), lambda c, i: (0, i, c)),
                  pl.BlockSpec((tm, cc), lambda c, i: (i, c))),
                 (jax.ShapeDtypeStruct((4, seq, d), BF16), jax.ShapeDtypeStruct((seq, d), BF16)),
                 (xn, win3, cw), scratch=[pltpu.VMEM((8, cc), F32)], rider=rider)


def mixer_bwd(dh, wout, u4, z, xn, cw, name, rider=()):
    seq, d = xn.shape
    tm = min(ROW_TILE, seq)
    cc = min(MIXER_CHUNK, d)
    n_c, n_i = d // cc, seq // tm

    def body(dh_ref, wout_ref, u_ref, z_ref, x_ref, cw_ref,
             du_ref, dwin_ref, dwout_ref, dcw_ref, acc_in, acc_out, acc_cw, carry):
        i = pl.program_id(1)

        @pl.when(i == 0)
        def _():
            acc_in[...] = jnp.zeros_like(acc_in)
            acc_out[...] = jnp.zeros_like(acc_out)
            acc_cw[...] = jnp.zeros_like(acc_cw)
            carry[...] = jnp.zeros_like(carry)
        dhb = dh_ref[...].astype(BF16)
        dz = _nt(dhb, wout_ref[...])
        acc_out[...] += _tn(z_ref[...], dhb)
        b = u_ref[0].astype(F32)
        c = u_ref[1].astype(F32)
        hh = u_ref[2].astype(F32)
        q = u_ref[3].astype(F32)
        p = c * hh
        db = dz * q
        dq = dz * b
        w0, w1, w2 = _conv_taps(cw_ref)
        dq1 = _shift_up(dq, 1, [carry[0:1, :]])
        dq2 = _shift_up(dq, 2, [carry[0:1, :], carry[1:2, :]])
        dp = w2 * dq + w1 * dq1 + w0 * dq2
        carry[...] = dq[0:8, :]
        acc_cw[0:1, :] += jnp.sum(dq2 * p, axis=0, keepdims=True)
        acc_cw[1:2, :] += jnp.sum(dq1 * p, axis=0, keepdims=True)
        acc_cw[2:3, :] += jnp.sum(dq * p, axis=0, keepdims=True)
        dbb = db.astype(BF16)
        dcb = (dp * hh).astype(BF16)
        dhhb = (dp * c).astype(BF16)
        du_ref[0] = dbb
        du_ref[1] = dcb
        du_ref[2] = dhhb
        xb = x_ref[...]
        acc_in[0] += _tn(xb, dbb)
        acc_in[1] += _tn(xb, dcb)
        acc_in[2] += _tn(xb, dhhb)

        @pl.when(i == n_i - 1)
        def _():
            dwin_ref[...] = acc_in[...].astype(BF16)
            dwout_ref[...] = acc_out[...].astype(BF16)
            dcw_ref[...] = acc_cw[0:3, :]

    rev = lambda c, i: (n_i - 1 - i, 0)
    return _call(body, name, (n_c, n_i),
                 [pl.BlockSpec((tm, d), rev),
                  pl.BlockSpec((cc, d), lambda c, i: (c, 0)),
                  pl.BlockSpec((4, tm, cc), lambda c, i: (0, n_i - 1 - i, c)),
                  pl.BlockSpec((tm, cc), lambda c, i: (n_i - 1 - i, c)),
                  pl.BlockSpec((tm, d), rev),
                  pl.BlockSpec((3, cc), lambda c, i: (0, c))],
                 (pl.BlockSpec((3, tm, cc), lambda c, i: (0, n_i - 1 - i, c)),
                  pl.BlockSpec((3, d, cc), lambda c, i: (0, 0, c)),
                  pl.BlockSpec((cc, d), lambda c, i: (c, 0)),
                  pl.BlockSpec((3, cc), lambda c, i: (0, c))),
                 (jax.ShapeDtypeStruct((3, seq, d), BF16), jax.ShapeDtypeStruct((3, d, d), BF16),
                  jax.ShapeDtypeStruct((d, d), BF16), jax.ShapeDtypeStruct((3, d), F32)),
                 (dh, wout, u4, z, xn, cw),
                 scratch=[pltpu.VMEM((3, d, cc), F32), pltpu.VMEM((cc, d), F32),
                          pltpu.VMEM((8, cc), F32), pltpu.VMEM((8, cc), F32)], rider=rider)


def _silu_parts(cg):
    sg = 1.0 / (1.0 + jnp.exp(-cg))
    return sg, cg * sg


def ffn_fwd(xn, wup, fcw, name, rider=()):
    seq, d = xn.shape
    f8 = wup.shape[-1]
    half = N_DEV // 2
    tm = min(ROW_TILE_LARGE, seq)
    n_i = seq // tm

    def body(x_ref, wg_ref, wu_ref, cg_ref, cu_ref, up_ref, cv_ref, a_ref, carry):
        i = pl.program_id(1)

        @pl.when(i == 0)
        def _():
            carry[...] = jnp.zeros_like(carry)
        xb = x_ref[...]
        conv = []
        for s, (w_ref, t_ref) in enumerate(((wg_ref, cg_ref), (wu_ref, cu_ref))):
            u = _nn(xb, w_ref[...])
            up_ref[s] = u.astype(BF16)
            w0, w1, w2 = _conv_taps(t_ref)
            u1 = _shift_down(u, 1, [carry[s, 7:8, :]])
            u2 = _shift_down(u, 2, [carry[s, 6:7, :], carry[s, 7:8, :]])
            cv = w0 * u2 + w1 * u1 + w2 * u
            cv_ref[s] = cv.astype(BF16)
            conv.append(cv)
            carry[s] = u[tm - 8:tm, :]
        _, silu = _silu_parts(conv[0])
        a_ref[...] = (silu * conv[1]).astype(BF16)

    blk = pl.BlockSpec((2, None, tm, f8), lambda c, i: (0, c, i, 0))
    big = jax.ShapeDtypeStruct((2, half, seq, f8), BF16)
    return _call(body, name, (half, n_i),
                 [pl.BlockSpec((tm, d), lambda c, i: (i, 0)),
                  pl.BlockSpec((None, d, f8), lambda c, i: (c, 0, 0)),
                  pl.BlockSpec((None, d, f8), lambda c, i: (c + half, 0, 0)),
                  pl.BlockSpec((None, 3, f8), lambda c, i: (c, 0, 0)),
                  pl.BlockSpec((None, 3, f8), lambda c, i: (c + half, 0, 0))],
                 (blk, blk, pl.BlockSpec((None, tm, f8), lambda c, i: (c, i, 0))),
                 (big, big, jax.ShapeDtypeStruct((half, seq, f8), BF16)),
                 (xn, wup, wup, fcw, fcw), scratch=[pltpu.VMEM((2, 8, f8), F32)], rider=rider)


def ffn_bwd(dh, wdown, up2, cv2, act, xn, fcw, name, rider=()):
    seq, d = xn.shape
    f8 = up2.shape[-1]
    fb = wdown.shape[1]
    half = N_DEV // 2
    tm = min(ROW_TILE, seq)
    n_i = seq // tm

    def body(dh_ref, wd_ref, up_ref, cv_ref, a_ref, x_ref, cg_ref, cu_ref,
             dup_ref, dwup_ref, dwd_ref, dcw_ref, acc_up, acc_down, acc_cw, carry):
        i = pl.program_id(1)

        @pl.when(i == 0)
        def _():
            acc_up[...] = jnp.zeros_like(acc_up)
            acc_down[...] = jnp.zeros_like(acc_down)
            acc_cw[...] = jnp.zeros_like(acc_cw)
            carry[...] = jnp.zeros_like(carry)
        dhb = dh_ref[...].astype(BF16)
        da = _nt(dhb, wd_ref[...])
        acc_down[...] += _tn(a_ref[...], dhb)
        cg = cv_ref[0].astype(F32)
        cu = cv_ref[1].astype(F32)
        sg, silu = _silu_parts(cg)
        dcg = da * cu * (sg + silu * (1.0 - sg))
        dcu = da * silu
        xb = x_ref[...]
        for s, (dc, t_ref) in enumerate(((dcg, cg_ref), (dcu, cu_ref))):
            w0, w1, w2 = _conv_taps(t_ref)
            d1 = _shift_up(dc, 1, [carry[s, 0:1, :]])
            d2 = _shift_up(dc, 2, [carry[s, 0:1, :], carry[s, 1:2, :]])
            du = (w2 * dc + w1 * d1 + w0 * d2).astype(BF16)
            carry[s] = dc[0:8, :]
            u = up_ref[s].astype(F32)
            acc_cw[s, 0:1, :] += jnp.sum(d2 * u, axis=0, keepdims=True)
            acc_cw[s, 1:2, :] += jnp.sum(d1 * u, axis=0, keepdims=True)
            acc_cw[s, 2:3, :] += jnp.sum(dc * u, axis=0, keepdims=True)
            dup_ref[s] = du
            acc_up[s] += _tn(xb, du)

        @pl.when(i == n_i - 1)
        def _():
            dwup_ref[...] = acc_up[...].astype(BF16)
            dwd_ref[...] = acc_down[...].astype(BF16)
            dcw_ref[...] = acc_cw[:, 0:3, :]

    rev = lambda c, i: (n_i - 1 - i, 0)
    blk = pl.BlockSpec((2, None, tm, f8), lambda c, i: (0, c, n_i - 1 - i, 0))
    return _call(body, name, (half, n_i),
                 [pl.BlockSpec((tm, d), rev),
                  pl.BlockSpec((None, fb, d), lambda c, i: (c, 0, 0)),
                  blk, blk,
                  pl.BlockSpec((None, tm, f8), lambda c, i: (c, n_i - 1 - i, 0)),
                  pl.BlockSpec((tm, d), rev),
                  pl.BlockSpec((None, 3, f8), lambda c, i: (c, 0, 0)),
                  pl.BlockSpec((None, 3, f8), lambda c, i: (c + half, 0, 0))],
                 (blk,
                  pl.BlockSpec((2, None, d, f8), lambda c, i: (0, c, 0, 0)),
                  pl.BlockSpec((None, fb, d), lambda c, i: (c, 0, 0)),
                  pl.BlockSpec((2, None, 3, f8), lambda c, i: (0, c, 0, 0))),
                 (jax.ShapeDtypeStruct((2, half, seq, f8), BF16),
                  jax.ShapeDtypeStruct((2, half, d, f8), BF16),
                  jax.ShapeDtypeStruct((half, fb, d), BF16),
                  jax.ShapeDtypeStruct((2, half, 3, f8), F32)),
                 (dh, wdown, up2, cv2, act, xn, fcw, fcw),
                 scratch=[pltpu.VMEM((2, d, f8), F32), pltpu.VMEM((fb, d), F32),
                          pltpu.VMEM((2, 8, f8), F32), pltpu.VMEM((2, 8, f8), F32)], rider=rider)


def q_fwd(xn, wdq, gq, wuq, cos, sin, name, rider=()):
    seq, d = xn.shape
    rank = wdq.shape[-1]
    tm = min(ROW_TILE, seq)

    def body(x_ref, wdq_ref, gq_ref, wuq_ref, cos_ref, sin_ref, q_ref):
        qc = _nn(x_ref[...], wdq_ref[...])
        qn = _rms(qc, gq_ref[...])[0].astype(BF16)
        for hd in range(N_HEADS):
            qh = _nn(qn, wuq_ref[hd])
            qr = _rope_fwd(qh[:, NOPE:QK], cos_ref[...], sin_ref[...])
            q_ref[hd, :, 0:NOPE] = (qh[:, 0:NOPE] * ATTN_SCALE).astype(BF16)
            q_ref[hd, :, NOPE:QK] = (qr * ATTN_SCALE).astype(BF16)

    rope = pl.BlockSpec((tm, ROPE_PAD), lambda i: (i, 0))
    return _call(body, name, (seq // tm,),
                 [pl.BlockSpec((tm, d), lambda i: (i, 0)),
                  pl.BlockSpec((d, rank), lambda i: (0, 0)),
                  pl.BlockSpec((1, rank), lambda i: (0, 0)),
                  pl.BlockSpec((N_HEADS, rank, QK), lambda i: (0, 0, 0)), rope, rope],
                 [pl.BlockSpec((N_HEADS, tm, QK), lambda i: (0, i, 0))],
                 [jax.ShapeDtypeStruct((N_HEADS, seq, QK), BF16)],
                 (xn, wdq, gq, wuq, cos, sin), rider=rider)


def q_bwd(dq, xn, wdq, gq, wuq, cos, sin, name, rider=()):
    seq, d = xn.shape
    rank = wdq.shape[-1]
    tm = min(ROW_TILE, seq)
    n_i = seq // tm

    def body(dq_ref, x_ref, wdq_ref, gq_ref, wuq_ref, cos_ref, sin_ref,
             dqc_ref, dwuq_ref, dwdq_ref, dgq_ref, acc_uq, acc_dq):
        i = pl.program_id(0)

        @pl.when(i == 0)
        def _():
            acc_uq[...] = jnp.zeros_like(acc_uq)
            acc_dq[...] = jnp.zeros_like(acc_dq)
            dgq_ref[...] = jnp.zeros_like(dgq_ref)
        xb = x_ref[...]
        qc = _nn(xb, wdq_ref[...])
        qn, qhat, qrstd = _rms(qc, gq_ref[...])
        qnb = qn.astype(BF16)
        dqn = jnp.zeros((tm, rank), F32)
        for hd in range(N_HEADS):
            dnope = (dq_ref[hd, :, 0:NOPE].astype(F32) * ATTN_SCALE).astype(BF16)
            drope = _rope_bwd(dq_ref[hd, :, NOPE:QK].astype(F32) * ATTN_SCALE, cos_ref[...], sin_ref[...])
            draw = jnp.concatenate([dnope, drope.astype(BF16)], axis=1)
            dqn = dqn + _nt(draw, wuq_ref[hd])
            acc_uq[hd] += _tn(qnb, draw)
        dqc, dg_rows = _rms_bwd(dqn, qhat, qrstd, gq_ref[...])
        dgq_ref[...] += jnp.sum(dg_rows, axis=0, keepdims=True)
        dqcb = dqc.astype(BF16)
        dqc_ref[0] = dqcb
        acc_dq[...] += _tn(xb, dqcb)

        @pl.when(i == n_i - 1)
        def _():
            dwuq_ref[...] = acc_uq[...].astype(BF16)
            dwdq_ref[...] = acc_dq[...].astype(BF16)

    rope = pl.BlockSpec((tm, ROPE_PAD), lambda i: (i, 0))
    return _call(body, name, (n_i,),
                 [pl.BlockSpec((N_HEADS, tm, QK), lambda i: (0, i, 0)),
                  pl.BlockSpec((tm, d), lambda i: (i, 0)),
                  pl.BlockSpec((d, rank), lambda i: (0, 0)),
                  pl.BlockSpec((1, rank), lambda i: (0, 0)),
                  pl.BlockSpec((N_HEADS, rank, QK), lambda i: (0, 0, 0)), rope, rope],
                 (pl.BlockSpec((1, tm, rank), lambda i: (0, i, 0)),
                  pl.BlockSpec((N_HEADS, rank, QK), lambda i: (0, 0, 0)),
                  pl.BlockSpec((d, rank), lambda i: (0, 0)),
                  pl.BlockSpec((1, rank), lambda i: (0, 0))),
                 (jax.ShapeDtypeStruct((1, seq, rank), BF16),
                  jax.ShapeDtypeStruct((N_HEADS, rank, QK), BF16),
                  jax.ShapeDtypeStruct((d, rank), BF16),
                  jax.ShapeDtypeStruct((1, rank), F32)),
                 (dq, xn, wdq, gq, wuq, cos, sin),
                 scratch=[pltpu.VMEM((N_HEADS, rank, QK), F32), pltpu.VMEM((d, rank), F32)],
                 rider=rider)


def kv_fwd(h, g, wdkv, gkv, wukv, cos, sin, name, rider=()):
    seq, d = h.shape
    tm = min(ROW_TILE, seq)
    wk = KV_RANK + ROPE_PAD

    def body(h_ref, g_ref, wdkv_ref, gkv_ref, wukv_ref, cos_ref, sin_ref, k_ref, v_ref, c_ref):
        xk = _rms(h_ref[...], g_ref[...])[0].astype(BF16)
        ckv = _nn(xk, wdkv_ref[...])
        c_kv = ckv[:, 0:KV_RANK]
        c_ref[...] = c_kv
        kr = _rope_fwd(ckv[:, KV_RANK:wk], cos_ref[...], sin_ref[...]).astype(BF16)
        ckn = _rms(c_kv, gkv_ref[...])[0].astype(BF16)
        for hd in range(N_HEADS):
            kvh = _nn(ckn, wukv_ref[hd])
            k_ref[hd, :, 0:NOPE] = kvh[:, 0:NOPE].astype(BF16)
            k_ref[hd, :, NOPE:QK] = kr
            v_ref[hd] = kvh[:, NOPE:NOPE + VDIM].astype(BF16)

    rope = pl.BlockSpec((tm, ROPE_PAD), lambda i: (i, 0))
    return _call(body, name, (seq // tm,),
                 [pl.BlockSpec((tm, d), lambda i: (i, 0)),
                  pl.BlockSpec((1, d), lambda i: (0, 0)),
                  pl.BlockSpec((d, wk), lambda i: (0, 0)),
                  pl.BlockSpec((1, KV_RANK), lambda i: (0, 0)),
                  pl.BlockSpec((N_HEADS, KV_RANK, NOPE + VDIM), lambda i: (0, 0, 0)), rope, rope],
                 (pl.BlockSpec((N_HEADS, tm, QK), lambda i: (0, i, 0)),
                  pl.BlockSpec((N_HEADS, tm, VDIM), lambda i: (0, i, 0)),
                  pl.BlockSpec((tm, KV_RANK), lambda i: (i, 0))),
                 (jax.ShapeDtypeStruct((N_HEADS, seq, QK), BF16),
                  jax.ShapeDtypeStruct((N_HEADS, seq, VDIM), BF16),
                  jax.ShapeDtypeStruct((seq, KV_RANK), F32)),
                 (h, g, wdkv, gkv, wukv, cos, sin), rider=rider)


def kv_bwd(dks, dvs, c_kv, h, g, gkv, wukv, cos, sin, name, rider=()):
    seq, d = h.shape
    tm = min(ROW_TILE, seq)
    n_i = seq // tm
    wk = KV_RANK + ROPE_PAD
    n_b = len(dks)

    def body(*refs):
        dk_refs = refs[:n_b]
        dv_refs = refs[n_b:2 * n_b]
        (c_ref, h_ref, g_ref, gkv_ref, wukv_ref, cos_ref, sin_ref,
         dckv_ref, dwukv_ref, dwdkv_ref, dgkv_ref, acc_ukv, acc_dkv) = refs[2 * n_b:]
        i = pl.program_id(0)

        @pl.when(i == 0)
        def _():
            acc_ukv[...] = jnp.zeros_like(acc_ukv)
            acc_dkv[...] = jnp.zeros_like(acc_dkv)
            dgkv_ref[...] = jnp.zeros_like(dgkv_ref)
        ckn, chat, crstd = _rms(c_ref[...], gkv_ref[...])
        cknb = ckn.astype(BF16)
        dckn = jnp.zeros((tm, KV_RANK), F32)
        dkr = jnp.zeros((tm, ROPE_PAD), F32)
        for hd in range(N_HEADS):
            dk = dk_refs[0][hd].astype(F32)
            dv = dv_refs[0][hd].astype(F32)
            for j in range(1, n_b):
                dk = dk + dk_refs[j][hd].astype(F32)
                dv = dv + dv_refs[j][hd].astype(F32)
            dkr = dkr + dk[:, NOPE:QK]
            dkvh = jnp.concatenate([dk[:, 0:NOPE].astype(BF16), dv.astype(BF16)], axis=1)
            dckn = dckn + _nt(dkvh, wukv_ref[hd])
            acc_ukv[hd] += _tn(cknb, dkvh)
        dc_kv, dg_rows = _rms_bwd(dckn, chat, crstd, gkv_ref[...])
        dgkv_ref[...] += jnp.sum(dg_rows, axis=0, keepdims=True)
        dkr_raw = _rope_bwd(dkr, cos_ref[...], sin_ref[...])
        dckv = jnp.concatenate([dc_kv.astype(BF16), dkr_raw.astype(BF16)], axis=1)
        dckv_ref[0] = dckv
        xk = _rms(h_ref[...], g_ref[...])[0].astype(BF16)
        acc_dkv[...] += _tn(xk, dckv)

        @pl.when(i == n_i - 1)
        def _():
            dwukv_ref[...] = acc_ukv[...].astype(BF16)
            dwdkv_ref[...] = acc_dkv[...].astype(BF16)

    kspec = pl.BlockSpec((N_HEADS, tm, QK), lambda i: (0, i, 0))
    vspec = pl.BlockSpec((N_HEADS, tm, VDIM), lambda i: (0, i, 0))
    rope = pl.BlockSpec((tm, ROPE_PAD), lambda i: (i, 0))
    return _call(body, name, (n_i,),
                 [kspec] * n_b + [vspec] * n_b + [
                     pl.BlockSpec((tm, KV_RANK), lambda i: (i, 0)),
                     pl.BlockSpec((tm, d), lambda i: (i, 0)),
                     pl.BlockSpec((1, d), lambda i: (0, 0)),
                     pl.BlockSpec((1, KV_RANK), lambda i: (0, 0)),
                     pl.BlockSpec((N_HEADS, KV_RANK, NOPE + VDIM), lambda i: (0, 0, 0)), rope, rope],
                 (pl.BlockSpec((1, tm, wk), lambda i: (0, i, 0)),
                  pl.BlockSpec((N_HEADS, KV_RANK, NOPE + VDIM), lambda i: (0, 0, 0)),
                  pl.BlockSpec((d, wk), lambda i: (0, 0)),
                  pl.BlockSpec((1, KV_RANK), lambda i: (0, 0))),
                 (jax.ShapeDtypeStruct((1, seq, wk), BF16),
                  jax.ShapeDtypeStruct((N_HEADS, KV_RANK, NOPE + VDIM), BF16),
                  jax.ShapeDtypeStruct((d, wk), BF16),
                  jax.ShapeDtypeStruct((1, KV_RANK), F32)),
                 (*dks, *dvs, c_kv, h, g, gkv, wukv, cos, sin),
                 scratch=[pltpu.VMEM((N_HEADS, KV_RANK, NOPE + VDIM), F32), pltpu.VMEM((d, wk), F32)],
                 rider=rider)


def o_bwd(dh, o, wo, name, rider=()):
    seq, d = dh.shape
    hv = o.shape[1]
    tm = min(ROW_TILE, seq)
    n_i = seq // tm

    def body(dh_ref, o_ref, wo_ref, do_ref, dwo_ref, acc):
        i = pl.program_id(0)

        @pl.when(i == 0)
        def _():
            acc[...] = jnp.zeros_like(acc)
        dhb = dh_ref[...].astype(BF16)
        do_ref[...] = _nt(dhb, wo_ref[...]).astype(BF16)
        acc[...] += _tn(o_ref[...], dhb)

        @pl.when(i == n_i - 1)
        def _():
            dwo_ref[...] = acc[...].astype(BF16)

    return _call(body, name, (n_i,),
                 [pl.BlockSpec((tm, d), lambda i: (i, 0)),
                  pl.BlockSpec((tm, hv), lambda i: (i, 0)),
                  pl.BlockSpec((hv, d), lambda i: (0, 0))],
                 (pl.BlockSpec((tm, hv), lambda i: (i, 0)),
                  pl.BlockSpec((hv, d), lambda i: (0, 0))),
                 (jax.ShapeDtypeStruct((seq, hv), BF16), jax.ShapeDtypeStruct((hv, d), BF16)),
                 (dh, o, wo), scratch=[pltpu.VMEM((hv, d), F32)], rider=rider)


def _mask_diagonal(s):
    row = lax.broadcasted_iota(jnp.int32, s.shape, 0)
    col = lax.broadcasted_iota(jnp.int32, s.shape, 1)
    return jnp.where(col <= row, s, NEG_BIG)


def attn_fwd(q, k, v, name, rider=()):
    _, seq, _ = q.shape
    t = min(ATTN_TILE, seq // 2)
    n_pair = seq // (2 * t)

    def body(q_ref, k_ref, v_ref, o_ref, lse_ref):
        qi = pl.program_id(1)
        q_a = q_ref[0:t, :]
        q_b = q_ref[t:2 * t, :]

        def rows(j):
            return pl.ds(pl.multiple_of(j * t, t), t)

        def update(qx, kb, vb, state, diagonal=False):
            m, l, acc = state
            s = _nt(qx, kb)
            if diagonal:
                s = _mask_diagonal(s)
            m_new = jnp.maximum(m, jnp.max(s, axis=1, keepdims=True))
            p = jnp.exp(s - m_new)
            alpha = jnp.exp(m - m_new)
            l = alpha * l + jnp.sum(p, axis=1, keepdims=True)
            acc = alpha * acc + _nn(p.astype(BF16), vb)
            return m_new, l, acc

        def step(j, carry):
            kb, vb = k_ref[rows(j), :], v_ref[rows(j), :]
            return update(q_a, kb, vb, carry[0:3]) + update(q_b, kb, vb, carry[3:6])

        init = (jnp.full((t, 1), NEG_BIG, F32), jnp.zeros((t, 1), F32), jnp.zeros((t, VDIM), F32))
        carry = lax.fori_loop(0, 2 * qi, step, init + init)
        k0, v0 = k_ref[rows(2 * qi), :], v_ref[rows(2 * qi), :]
        k1, v1 = k_ref[rows(2 * qi + 1), :], v_ref[rows(2 * qi + 1), :]
        state_a = update(q_a, k0, v0, carry[0:3], diagonal=True)
        state_b = update(q_b, k1, v1, update(q_b, k0, v0, carry[3:6]), diagonal=True)
        for half, (m, l, acc) in enumerate((state_a, state_b)):
            o_ref[half * t:(half + 1) * t, :] = (acc / l).astype(BF16)
            lse_ref[half * t:(half + 1) * t, :] = jnp.broadcast_to(m + jnp.log(l), (t, LANES))

    return _call(body, name, (N_HEADS, n_pair),
                 [pl.BlockSpec((None, 2 * t, QK), lambda h, i: (h, i, 0)),
                  pl.BlockSpec((None, seq, QK), lambda h, i: (h, 0, 0)),
                  pl.BlockSpec((None, seq, VDIM), lambda h, i: (h, 0, 0))],
                 (pl.BlockSpec((2 * t, VDIM), lambda h, i: (i, h)),
                  pl.BlockSpec((None, 2 * t, LANES), lambda h, i: (h, i, 0))),
                 (jax.ShapeDtypeStruct((seq, N_HEADS * VDIM), BF16),
                  jax.ShapeDtypeStruct((N_HEADS, seq, LANES), F32)),
                 (q, k, v), rider=rider)


def attn_bwd(q, k, v, o, do, lse, name, rider=()):
    _, seq, _ = q.shape
    t = min(ATTN_TILE, seq // 2)
    n_q = seq // t
    n_pair = n_q // 2

    def body(q_ref, k_ref, v_ref, o_ref, do_ref, lse_ref, dq_ref, dk_ref, dv_ref,
             dq_acc, dk_acc, dv_acc):
        kj = pl.program_id(1)

        @pl.when(kj == 0)
        def _():
            dq_acc[...] = jnp.zeros_like(dq_acc)
        dk_acc[...] = jnp.zeros_like(dk_acc)
        dv_acc[...] = jnp.zeros_like(dv_acc)
        halves = (slice(0, t), slice(t, 2 * t))

        def block(i, masks):
            rows = pl.ds(pl.multiple_of(i * t, t), t)
            qb = q_ref[rows, :]
            dob = do_ref[rows, :]
            lse_col = lse_ref[rows, 0:1]
            delta = jnp.sum(dob.astype(F32) * o_ref[rows, :].astype(F32), axis=1, keepdims=True)
            dq = None
            for x, diagonal in enumerate(masks):
                if diagonal is None:
                    continue
                kb, vb = k_ref[halves[x], :], v_ref[halves[x], :]
                s = _nt(qb, kb)
                if diagonal:
                    s = _mask_diagonal(s)
                p = jnp.exp(s - lse_col)
                ds = (p * (_nt(dob, vb) - delta)).astype(BF16)
                dv_acc[halves[x], :] += _tn(p.astype(BF16), dob)
                dk_acc[halves[x], :] += _tn(ds, qb)
                part = _nn(ds, kb)
                dq = part if dq is None else dq + part
            dq_acc[rows, :] += dq

        block(2 * kj, (True, None))
        block(2 * kj + 1, (False, True))

        def step(i, carry):
            block(i, (False, False))
            return carry

        lax.fori_loop(2 * kj + 2, n_q, step, 0)
        dk_ref[...] = dk_acc[...].astype(BF16)
        dv_ref[...] = dv_acc[...].astype(BF16)

        @pl.when(kj == n_pair - 1)
        def _():
            dq_ref[...] = dq_acc[...].astype(BF16)

    head_rows = pl.BlockSpec((seq, VDIM), lambda h, j: (0, h))
    return _call(body, name, (N_HEADS, n_pair),
                 [pl.BlockSpec((None, seq, QK), lambda h, j: (h, 0, 0)),
                  pl.BlockSpec((None, 2 * t, QK), lambda h, j: (h, j, 0)),
                  pl.BlockSpec((None, 2 * t, VDIM), lambda h, j: (h, j, 0)),
                  head_rows, head_rows,
                  pl.BlockSpec((None, seq, LANES), lambda h, j: (h, 0, 0))],
                 (pl.BlockSpec((None, seq, QK), lambda h, j: (h, 0, 0)),
                  pl.BlockSpec((None, 2 * t, QK), lambda h, j: (h, j, 0)),
                  pl.BlockSpec((None, 2 * t, VDIM), lambda h, j: (h, j, 0))),
                 (jax.ShapeDtypeStruct((N_HEADS, seq, QK), BF16),
                  jax.ShapeDtypeStruct((N_HEADS, seq, QK), BF16),
                  jax.ShapeDtypeStruct((N_HEADS, seq, VDIM), BF16)),
                 (q, k, v, o, do, lse),
                 scratch=[pltpu.VMEM((seq, QK), F32), pltpu.VMEM((2 * t, QK), F32),
                          pltpu.VMEM((2 * t, VDIM), F32)], rider=rider)


def loss_head(h, g, target, name):
    seq, d = h.shape
    tm = min(ROW_TILE, seq)

    def body(h_ref, g_ref, t_ref, l_ref, dh_ref, dg_ref):
        i = pl.program_id(0)

        @pl.when(i == 0)
        def _():
            l_ref[...] = jnp.zeros_like(l_ref)
            dg_ref[...] = jnp.zeros_like(dg_ref)
        y, xhat, rstd = _rms(h_ref[...], g_ref[...])
        diff = y - t_ref[...]
        l_ref[...] += jnp.sum(jnp.sum(diff * diff, axis=1, keepdims=True), axis=0, keepdims=True)
        dh, dg_rows = _rms_bwd(diff * (1.0 / d), xhat, rstd, g_ref[...])
        dh_ref[...] = dh
        dg_ref[...] += jnp.sum(dg_rows, axis=0, keepdims=True)

    row = pl.BlockSpec((tm, d), lambda i: (i, 0))
    vec = pl.BlockSpec((1, d), lambda i: (0, 0))
    return _call(body, name, (seq // tm,), [row, vec, row],
                 (pl.BlockSpec((1, LANES), lambda i: (0, 0)), row, vec),
                 (jax.ShapeDtypeStruct((1, LANES), F32), jax.ShapeDtypeStruct((seq, d), F32),
                  jax.ShapeDtypeStruct((1, d), F32)),
                 (h, g, target))[0]


def _pack(parts):
    rows = []
    for p in parts:
        flat = p.reshape(-1)
        n_rows = -(-flat.shape[0] // (8 * LANES)) * 8
        flat = jnp.pad(flat, (0, n_rows * LANES - flat.shape[0]))
        rows.append(flat.reshape(n_rows, LANES))
    return jnp.concatenate(rows, axis=0)


def _unpack(packed, shapes):
    lead = packed.shape[:-2]
    out, r0 = [], 0
    for shape in shapes:
        size = 1
        for s in shape:
            size *= s
        n_rows = -(-size // (8 * LANES)) * 8
        part = packed[..., r0:r0 + n_rows, :].reshape(lead + (n_rows * LANES,))
        out.append(part[..., :size].reshape(lead + tuple(shape)))
        r0 += n_rows
    return out


FWD_RIDERS = {
    "mixer_fwd0": [("ffn_w_up", 0)],
    "ffn_fwd0": [("ffn_w_down", 0), ("a_w_in", 1), ("a_w_out", 1), ("w_dkv", 0), ("w_ukv", 0),
                 ("b_w_dq", 0), ("b_w_uq", 0)],
    "ffn_out0": [("ffn_w_down", 1)],
    "mixer_fwd1": [("ffn_w_up", 1)],
    "ffn_fwd1": [("ffn_w_up", 2), ("b_w_o", 0)],
    "ffn_out1": [("ffn_w_down", 2)],
    "attn_fwd0": [("ffn_w_up", 3), ("ffn_w_down", 3), ("b_w_dq", 1), ("b_w_uq", 1), ("b_w_o", 1)],
}
BWD_RIDERS = {
    "ffn_in_bwd3": [("ffn_w_down", 3)],
    "attn_bwd1": [("ffn_w_up", 3), ("b_w_o", 1)],
    "ffn_bwd2": [("b_w_uq", 1), ("b_w_dq", 1)],
    "ffn_in_bwd2": [("ffn_w_down", 2)],
    "attn_bwd0": [("ffn_w_up", 2), ("b_w_o", 0)],
    "ffn_bwd1": [("b_w_uq", 0), ("b_w_dq", 0), ("w_ukv", 0), ("w_dkv", 0)],
    "ffn_in_bwd1": [("ffn_w_down", 1), ("ffn_w_up", 1, "pair")],
    "ffn_bwd0": [("ffn_w_up", 1, "chip"), ("a_w_in", 1), ("a_w_out", 1)],
    "ffn_in_bwd0": [("ffn_w_down", 0), ("ffn_w_up", 0, "pair")],
    "mixer_bwd0": [("ffn_w_up", 0, "chip")],
    "mixer_in_bwd0": [("a_w_out", 0), ("a_w_in", 0)],
}


def kernel(x, a_mix_norm, a_w_in, a_conv, a_w_out, b_mix_norm, b_w_dq, b_q_norm, b_w_uq, b_w_o, kv_in_norm, w_dkv, kv_norm, w_ukv, ffn_norm, ffn_w_up, ffn_conv, ffn_w_down, final_norm, loss_target, m_a_mix_norm, m_a_w_in, m_a_conv, m_a_w_out, m_b_mix_norm, m_b_w_dq, m_b_q_norm, m_b_w_uq, m_b_w_o, m_kv_in_norm, m_w_dkv, m_kv_norm, m_w_ukv, m_ffn_norm, m_ffn_w_up, m_ffn_conv, m_ffn_w_down, m_final_norm, v_a_mix_norm, v_a_w_in, v_a_conv, v_a_w_out, v_b_mix_norm, v_b_w_dq, v_b_q_norm, v_b_w_uq, v_b_w_o, v_kv_in_norm, v_w_dkv, v_kv_norm, v_w_ukv, v_ffn_norm, v_ffn_w_up, v_ffn_conv, v_ffn_w_down, v_final_norm):
    seq, d = x.shape[1], x.shape[2]
    me = 4 * lax.axis_index("x") + 2 * lax.axis_index("y") + lax.axis_index("c")
    h0 = x.reshape(seq, d)
    target = loss_target.reshape(seq, d)
    cos, sin = _rope_tables(seq)
    rank = b_w_dq.shape[-1]
    f8 = ffn_w_up.shape[-1]
    fd = ffn_w_down.shape[1]
    dshard = a_w_out.shape[1]
    hv = N_HEADS * VDIM

    shards = {"a_w_in": a_w_in, "a_w_out": a_w_out, "b_w_dq": b_w_dq, "b_w_uq": b_w_uq,
              "b_w_o": b_w_o, "w_dkv": w_dkv[None], "w_ukv": w_ukv[None],
              "ffn_w_up": ffn_w_up, "ffn_w_down": ffn_w_down}

    def relayout(name, g):
        if name == "a_w_in":
            w = jnp.transpose(g, (1, 0, 2)).reshape(d, 3, d)
            return jnp.transpose(w, (1, 0, 2))
        if name == "a_w_out":
            return g.reshape(d, d)
        if name == "b_w_dq":
            return g.reshape(d, rank)
        if name == "b_w_uq":
            return jnp.pad(g, ((0, 0), (0, 0), (0, QK - NOPE - ROPE)))
        if name == "b_w_o":
            return g.reshape(hv, d)
        if name == "w_dkv":
            return jnp.pad(g.reshape(d, KV_RANK + ROPE), ((0, 0), (0, ROPE_PAD - ROPE)))
        if name == "ffn_w_down":
            return g.reshape(N_DEV // 2, 2 * fd, d)
        return g

    weights = {}

    def ag_rider(host):
        return [("ag", shards[n][l].astype(BF16)) for n, l in FWD_RIDERS.get(host, [])]

    def ag_done(host, outs):
        for (n, l), g in zip(FWD_RIDERS.get(host, []), outs):
            weights[n, l] = relayout(n, g)

    small_shapes = [a_mix_norm.shape, a_conv.shape, ffn_conv.shape]
    first = exchange([("ag", a_w_in[0].astype(BF16)), ("ag", a_w_out[0].astype(BF16)),
                      ("ag", _pack([a_mix_norm, a_conv, ffn_conv]))], "ag_first")
    weights["a_w_in", 0] = relayout("a_w_in", first[0])
    weights["a_w_out", 0] = relayout("a_w_out", first[1])
    s_mix, s_aconv, s_fconv = _unpack(first[2], small_shapes)
    a_gain = jnp.transpose(s_mix, (1, 0, 2)).reshape(N_A, d)
    a_cw = jnp.transpose(s_aconv, (1, 2, 0, 3)).reshape(N_A, 3, d)
    f_cw = jnp.transpose(s_fconv, (1, 0, 2, 3))

    def mixer_gain(layer):
        if layer >= DEPTH:
            return None
        return a_gain[layer][None] if layer < N_A else b_mix_norm[layer - N_A][None]

    saved = {}
    h = h0
    xn = norm_fwd(h, mixer_gain(0), "norm_first")
    kv = None
    for layer in range(DEPTH):
        saved["hm", layer], saved["xm", layer] = h, xn
        if layer < N_A:
            name = f"mixer_fwd{layer}"
            (u4, z), r = mixer_fwd(xn, weights["a_w_in", layer], a_cw[layer], name, rider=ag_rider(name))
            ag_done(name, r)
            saved["mix", layer] = (u4, z)
            name = f"mixer_out{layer}"
            (h, xn), r = proj_residual(z[None], weights["a_w_out", layer][None], h, name,
                                       g_next=ffn_norm[layer][None], rider=ag_rider(name))
            ag_done(name, r)
        else:
            j = layer - N_A
            name = f"q_fwd{j}"
            (q,), r = q_fwd(xn, weights["b_w_dq", j], b_q_norm[j][None], weights["b_w_uq", j],
                            cos, sin, name, rider=ag_rider(name))
            ag_done(name, r)
            name = f"attn_fwd{j}"
            (o, lse), r = attn_fwd(q, kv[0], kv[1], name, rider=ag_rider(name))
            ag_done(name, r)
            saved["attn", layer] = (q, o, lse)
            name = f"attn_out{j}"
            (h, xn), r = proj_residual(o[None], weights["b_w_o", j][None], h, name,
                                       g_next=ffn_norm[layer][None], rider=ag_rider(name))
            ag_done(name, r)
        saved["hf", layer], saved["xf", layer] = h, xn
        name = f"ffn_fwd{layer}"
        (up2, cv2, act), r = ffn_fwd(xn, weights["ffn_w_up", layer], f_cw[layer], name, rider=ag_rider(name))
        ag_done(name, r)
        saved["ffn", layer] = (up2, cv2, act)
        name = f"ffn_out{layer}"
        (h, xn), r = proj_residual(act, weights["ffn_w_down", layer], h, name,
                                   g_next=mixer_gain(layer + 1), rider=ag_rider(name))
        ag_done(name, r)
        if layer == N_A - 1:
            (k_all, v_all, c_kv), r = kv_fwd(h, kv_in_norm[None], weights["w_dkv", 0], kv_norm[None],
                                             weights["w_ukv", 0], cos, sin, "kv_fwd",
                                             rider=ag_rider("kv_fwd"))
            ag_done("kv_fwd", r)
            kv = (k_all, v_all, c_kv)

    sq_err, dh, d_final = loss_head(h, final_norm[None], target, "loss_head")
    loss = lax.psum(sq_err[0, 0] * (0.5 / d), ("x", "y", "c"))

    grads = {}
    parts = {}

    pair_sums = {}

    def by_chip(g):
        return g.reshape((N_DEV // 2, 2) + g.shape[1:])

    def rs_rider(host):
        tasks = []
        for key in BWD_RIDERS.get(host, []):
            if len(key) == 2:
                tasks.append(("rs", grads[key]))
            elif key[2] == "pair":
                tasks.append(("rs_pair", by_chip(grads[key[:2]])))
            else:
                tasks.append(("rs_chip", pair_sums[key[:2]]))
        return tasks

    def rs_done(host, outs):
        for key, p in zip(BWD_RIDERS.get(host, []), outs):
            if len(key) == 3 and key[2] == "pair":
                pair_sums[key[:2]] = pair_sum(by_chip(grads[key[:2]]), p, f"pair_sum_{key[0]}{key[1]}")
            else:
                parts[key[:2]] = p

    d_ffn_norm = [None] * DEPTH
    d_fconv = [None] * DEPTH
    d_a_gain = [None] * N_A
    d_aconv = [None] * N_A
    d_b_gain = [None] * N_B
    d_q_gain = [None] * N_B
    dks, dvs = [], []
    for layer in reversed(range(DEPTH)):
        if layer == N_A - 1:
            hk = saved["hm", layer + 1]
            (dckv, dwukv, dwdkv, d_kv_gain), r = kv_bwd(
                dks, dvs, kv[2], hk, kv_in_norm[None], kv_norm[None], weights["w_ukv", 0],
                cos, sin, "kv_bwd", rider=rs_rider("kv_bwd"))
            rs_done("kv_bwd", r)
            grads["w_ukv", 0] = dwukv
            grads["w_dkv", 0] = dwdkv[:, :KV_RANK + ROPE].reshape(N_DEV, dshard, KV_RANK + ROPE)
            (dh, d_kvin_gain), r = proj_t_rms_bwd(dckv, weights["w_dkv", 0][None], hk, kv_in_norm[None],
                                                  dh, "kv_in_bwd", rider=rs_rider("kv_in_bwd"))
            rs_done("kv_in_bwd", r)
        up2, cv2, act = saved["ffn", layer]
        name = f"ffn_bwd{layer}"
        (dup2, dwup, dwdown, dcw), r = ffn_bwd(dh, weights["ffn_w_down", layer], up2, cv2, act,
                                               saved["xf", layer], f_cw[layer], name, rider=rs_rider(name))
        rs_done(name, r)
        grads["ffn_w_up", layer] = dwup.reshape(N_DEV, d, f8)
        grads["ffn_w_down", layer] = dwdown.reshape(N_DEV, fd, d)
        d_fconv[layer] = dcw.reshape(N_DEV, 3, f8)
        name = f"ffn_in_bwd{layer}"
        (dh, d_ffn_norm[layer]), r = proj_t_rms_bwd(dup2.reshape(N_DEV, seq, f8), weights["ffn_w_up", layer],
                                                    saved["hf", layer], ffn_norm[layer][None], dh, name,
                                                    rider=rs_rider(name))
        rs_done(name, r)
        hm, xm = saved["hm", layer], saved["xm", layer]
        if layer < N_A:
            u4, z = saved["mix", layer]
            name = f"mixer_bwd{layer}"
            (du3, dwin3, dwout, dcw), r = mixer_bwd(dh, weights["a_w_out", layer], u4, z, xm, a_cw[layer],
                                                    name, rider=rs_rider(name))
            rs_done(name, r)
            dwin = jnp.transpose(dwin3, (1, 0, 2)).reshape(d, N_DEV, 3 * d // N_DEV)
            grads["a_w_in", layer] = jnp.transpose(dwin, (1, 0, 2))
            grads["a_w_out", layer] = dwout.reshape(N_DEV, dshard, d)
            d_aconv[layer] = dcw
            name = f"mixer_in_bwd{layer}"
            (dh, d_a_gain[layer]), r = proj_t_rms_bwd(du3, weights["a_w_in", layer], hm, a_gain[layer][None],
                                                      dh, name, rider=rs_rider(name))
            rs_done(name, r)
        else:
            j = layer - N_A
            q, o, lse = saved["attn", layer]
            name = f"attn_out_bwd{j}"
            (do, dwo), r = o_bwd(dh, o, weights["b_w_o", j], name, rider=rs_rider(name))
            rs_done(name, r)
            grads["b_w_o", j] = dwo.reshape(N_DEV, dshard, d)
            name = f"attn_bwd{j}"
            (dq, dk, dv), r = attn_bwd(q, kv[0], kv[1], o, do, lse, name, rider=rs_rider(name))
            rs_done(name, r)
            dks.append(dk)
            dvs.append(dv)
            name = f"q_bwd{j}"
            (dqc, dwuq, dwdq, d_q_gain[j]), r = q_bwd(dq, xm, weights["b_w_dq", j], b_q_norm[j][None],
                                                      weights["b_w_uq", j], cos, sin, name, rider=rs_rider(name))
            rs_done(name, r)
            grads["b_w_uq", j] = dwuq[:, :, :NOPE + ROPE]
            grads["b_w_dq", j] = dwdq.reshape(N_DEV, dshard, rank)
            name = f"q_in_bwd{j}"
            (dh, d_b_gain[j]), r = proj_t_rms_bwd(dqc, weights["b_w_dq", j][None], hm, b_mix_norm[j][None],
                                                  dh, name, rider=rs_rider(name))
            rs_done(name, r)
    grad_x = dh.reshape(x.shape)

    full_small = [
        jnp.concatenate(d_a_gain, axis=0),
        jnp.stack(d_aconv),
        jnp.concatenate(d_b_gain, axis=0),
        jnp.concatenate(d_q_gain, axis=0),
        d_kvin_gain[0],
        d_kv_gain[0],
        jnp.concatenate(d_ffn_norm, axis=0),
        jnp.stack(d_fconv),
        d_final[0],
    ]
    full_shapes = [t.shape for t in full_small]
    small_pack = _pack(full_small)

    res = {}

    def update(name, n_layers, w, m, v, extra=(), transposed=False):
        view = (lambda t: jnp.transpose(t, (0, 2, 1))) if transposed else (lambda t: t)
        call = sum_adamw_transposed if transposed else sum_adamw
        shard = w.shape if w.ndim == 3 else (1,) + w.shape
        host = f"adamw_{name}"
        outs, r = call([parts[name, l] for l in range(n_layers)], view(w.reshape(shard)),
                       view(m.reshape(shard)), view(v.reshape(shard)), host,
                       rider=rs_rider(host) + list(extra))
        rs_done(host, r)
        res[name] = [view(t).reshape(w.shape) for t in outs]
        return r[len(BWD_RIDERS.get(host, [])):]

    (g_parts,) = update("a_w_out", N_A, a_w_out, m_a_w_out, v_a_w_out, extra=[("ag", small_pack)])
    update("ffn_w_down", DEPTH, ffn_w_down, m_ffn_w_down, v_ffn_w_down)
    update("ffn_w_up", DEPTH, ffn_w_up, m_ffn_w_up, v_ffn_w_up, transposed=True)
    update("b_w_dq", N_B, b_w_dq, m_b_w_dq, v_b_w_dq)
    update("b_w_uq", N_B, b_w_uq, m_b_w_uq, v_b_w_uq)
    update("b_w_o", N_B, b_w_o, m_b_w_o, v_b_w_o)
    update("w_dkv", 1, w_dkv, m_w_dkv, v_w_dkv)
    update("w_ukv", 1, w_ukv, m_w_ukv, v_w_ukv)
    update("a_w_in", N_A, a_w_in, m_a_w_in, v_a_w_in)

    summed = sum_slots(g_parts, "sum_small_grads")
    (s_a_gain, s_aconv_g, s_b_gain, s_q_gain, s_kvin, s_kvn, s_ffn_gain, s_fconv_g,
     s_final) = _unpack(summed, full_shapes)
    dsl = d // N_DEV
    small = [
        ("a_mix_norm", lax.dynamic_slice_in_dim(s_a_gain, me * dsl, dsl, axis=1), a_mix_norm, m_a_mix_norm, v_a_mix_norm),
        ("a_conv", lax.dynamic_slice_in_dim(s_aconv_g, me * dsl, dsl, axis=2), a_conv, m_a_conv, v_a_conv),
        ("b_mix_norm", s_b_gain, b_mix_norm, m_b_mix_norm, v_b_mix_norm),
        ("b_q_norm", s_q_gain, b_q_norm, m_b_q_norm, v_b_q_norm),
        ("kv_in_norm", s_kvin, kv_in_norm, m_kv_in_norm, v_kv_in_norm),
        ("kv_norm", s_kvn, kv_norm, m_kv_norm, v_kv_norm),
        ("ffn_norm", s_ffn_gain, ffn_norm, m_ffn_norm, v_ffn_norm),
        ("ffn_conv", lax.dynamic_index_in_dim(s_fconv_g, me, axis=1, keepdims=False), ffn_conv, m_ffn_conv, v_ffn_conv),
        ("final_norm", s_final, final_norm, m_final_norm, v_final_norm),
    ]
    shapes = [t[2].shape for t in small]
    packed = [_pack([t[k] for t in small])[None] for k in (1, 2, 3, 4)]
    outs, _ = sum_adamw([packed[0]], packed[1], packed[2], packed[3], "adamw_small")
    unpacked = [_unpack(t[0], shapes) for t in outs]
    for idx, t in enumerate(small):
        res[t[0]] = [unpacked[k][idx] for k in range(4)]

    order = ["a_mix_norm", "a_w_in", "a_conv", "a_w_out", "b_mix_norm", "b_w_dq", "b_q_norm",
             "b_w_uq", "b_w_o", "kv_in_norm", "w_dkv", "kv_norm", "w_ukv", "ffn_norm",
             "ffn_w_up", "ffn_conv", "ffn_w_down", "final_norm"]
    return (loss, grad_x, *[res[n][0] for n in order], *[res[n][1] for n in order],
            *[res[n][2] for n in order], *[res[n][3] for n in order])
```

```python
import functools

import jax
import jax.numpy as jnp
from jax import lax
from jax.experimental import pallas as pl
from jax.experimental.pallas import tpu as pltpu

F32 = jnp.float32
BF16 = jnp.bfloat16

N_DEV = 8
N_HEADS = 8
NOPE = 128
ROPE = 64
ROPE_PAD = 128
QK = NOPE + ROPE_PAD
VDIM = 128
KV_RANK = 256
ROPE_THETA = 10000.0
RMS_EPS = 1e-6
ATTN_SCALE = (NOPE + ROPE) ** -0.5
N_A = 2
N_B = 2
DEPTH = 4

ADAM_LR = 0.001
ADAM_B1 = 0.9
ADAM_B2 = 0.999
ADAM_EPS = 1e-08
ADAM_WD = 0.01
ADAM_STEP = 10

V7X_VMEM_LIMIT = 56 * 1024 * 1024
BF16_SUBLANES = 16
ROW_TILE = 512
ROW_TILE_SMALL = 256
ROW_TILE_LARGE = 1024
ATTN_TILE = 512
MIXER_CHUNK = 512
LANES = 128
NEG_BIG = -1e30
COPIES_PER_TASK = 7

MESH_ID = pl.DeviceIdType.MESH
ANY = pl.BlockSpec(memory_space=pl.ANY)


def _nt(a, b):
    return lax.dot_general(a, b, (((1,), (1,)), ((), ())), preferred_element_type=F32)


def _tn(a, b):
    return lax.dot_general(a, b, (((0,), (0,)), ((), ())), preferred_element_type=F32)


def _nn(a, b):
    return jnp.dot(a, b, preferred_element_type=F32)


def _rms(h, g):
    rstd = lax.rsqrt(jnp.mean(h * h, axis=-1, keepdims=True) + RMS_EPS)
    xhat = h * rstd
    return xhat * g, xhat, rstd


def _rms_bwd(dxn, xhat, rstd, g):
    dxhat = dxn * g
    dh = rstd * (dxhat - xhat * jnp.mean(dxhat * xhat, axis=-1, keepdims=True))
    return dh, dxn * xhat


def _shift_down(x, k, halo_rows):
    r = pltpu.roll(x, k, 0)
    row = lax.broadcasted_iota(jnp.int32, x.shape, 0)
    for t in range(k):
        r = jnp.where(row == t, halo_rows[t], r)
    return r


def _shift_up(x, k, halo_rows):
    n = x.shape[0]
    r = pltpu.roll(x, n - k, 0)
    row = lax.broadcasted_iota(jnp.int32, x.shape, 0)
    for t in range(k):
        r = jnp.where(row == n - k + t, halo_rows[t], r)
    return r


def _conv_taps(w_ref):
    return w_ref[0:1, :], w_ref[1:2, :], w_ref[2:3, :]


def _rope_swap(x):
    lane = lax.broadcasted_iota(jnp.int32, x.shape, 1)
    return jnp.where(lane < ROPE // 2, pltpu.roll(x, ROPE_PAD - ROPE // 2, 1),
                     pltpu.roll(x, ROPE // 2, 1))


def _rope_fwd(x, cos, sin):
    return x * cos + _rope_swap(x) * sin


def _rope_bwd(dy, cos, sin):
    return dy * cos - _rope_swap(dy) * sin


def _rope_tables(seq):
    inv = 1.0 / (ROPE_THETA ** (jnp.arange(0, ROPE, 2, dtype=F32) / ROPE))
    ang = jnp.arange(seq, dtype=F32)[:, None] * inv[None, :]
    cos, sin = jnp.cos(ang), jnp.sin(ang)
    zero = jnp.zeros((seq, ROPE_PAD - ROPE), F32)
    return (jnp.concatenate([cos, cos, zero], axis=1),
            jnp.concatenate([-sin, sin, zero], axis=1))


def _row_tile(rows, cap, mult=8):
    best = None
    for t in range(mult, min(rows, cap) + 1, mult):
        if rows % t == 0:
            best = t
    return rows if best is None else best


class _AllGatherTask:
    def __init__(self, t, x_ref, out_ref, send_sems, recv_sems, local_sems):
        self.t, self.x_ref, self.out_ref = t, x_ref, out_ref
        self.send_sems, self.recv_sems, self.local_sems = send_sems, recv_sems, local_sems
        mx, my, mc = lax.axis_index("x"), lax.axis_index("y"), lax.axis_index("c")
        self.mc = mc
        self.me, self.sibling = (mx, my, mc), (mx, my, 1 - mc)
        self.chips = [(1 - mx, my), (mx, 1 - my), (1 - mx, 1 - my)]

    def _slot(self, px, py, pc):
        return self.out_ref.at[4 * px + 2 * py + pc]

    def _copy(self, k, block, to, src=None):
        s = COPIES_PER_TASK * self.t + k
        return pltpu.make_async_remote_copy(
            src_ref=self._slot(*block) if src is None else src, dst_ref=self._slot(*block),
            send_sem=self.send_sems.at[s], recv_sem=self.recv_sems.at[s],
            device_id=to, device_id_type=MESH_ID)

    def _mine(self):
        return pltpu.make_async_copy(self.x_ref, self._slot(*self.me), self.local_sems.at[self.t])

    def _first(self):
        out = [self._copy(0, self.me, self.sibling, src=self.x_ref)]
        out += [self._copy(1 + j, self.me, (*chip, self.mc), src=self.x_ref)
                for j, chip in enumerate(self.chips)]
        return out

    def _passed(self):
        return [self._copy(4 + j, (*chip, self.mc), self.sibling) for j, chip in enumerate(self.chips)]

    def start(self):
        self._mine().start()
        for cp in self._first():
            cp.start()

    def forward(self):
        passed = self._passed()
        for j, chip in enumerate(self.chips):
            self._copy(1 + j, (*chip, self.mc), self.me).wait_recv()
            passed[j].start()

    def finish(self):
        self._copy(0, self.sibling, self.me).wait_recv()
        for j, chip in enumerate(self.chips):
            self._copy(4 + j, (*chip, 1 - self.mc), self.me).wait_recv()
        for cp in self._first() + self._passed():
            cp.wait_send()
        self._mine().wait()


class _ReduceScatterTask:
    def __init__(self, t, g_ref, out_ref, send_sems, recv_sems, local_sems):
        self.t, self.g_ref, self.out_ref = t, g_ref, out_ref
        self.send_sems, self.recv_sems, self.local_sems = send_sems, recv_sems, local_sems
        mx, my, mc = lax.axis_index("x"), lax.axis_index("y"), lax.axis_index("c")
        self.me = 4 * mx + 2 * my + mc
        self.peers = []
        for k in range(1, N_DEV):
            px, py, pc = mx ^ ((k >> 2) & 1), my ^ ((k >> 1) & 1), mc ^ (k & 1)
            self.peers.append(((px, py, pc), 4 * px + 2 * py + pc))

    def _mine(self):
        return pltpu.make_async_copy(self.g_ref.at[self.me], self.out_ref.at[self.me],
                                     self.local_sems.at[self.t])

    def _copy(self, k, src_slot, dst_slot):
        s = COPIES_PER_TASK * self.t + k
        return pltpu.make_async_remote_copy(
            src_ref=self.g_ref.at[src_slot], dst_ref=self.out_ref.at[dst_slot],
            send_sem=self.send_sems.at[s], recv_sem=self.recv_sems.at[s],
            device_id=self.peers[k][0], device_id_type=MESH_ID)

    def start(self):
        self._mine().start()
        for k, (_, peer) in enumerate(self.peers):
            self._copy(k, peer, self.me).start()

    def forward(self):
        pass

    def finish(self):
        for k, (_, peer) in enumerate(self.peers):
            self._copy(k, self.me, peer).wait_recv()
        for k, (_, peer) in enumerate(self.peers):
            self._copy(k, peer, self.me).wait_send()
        self._mine().wait()


class _PairExchangeTask:
    def __init__(self, t, g_ref, out_ref, send_sems, recv_sems, local_sems):
        mx, my, mc = lax.axis_index("x"), lax.axis_index("y"), lax.axis_index("c")
        s = COPIES_PER_TASK * t
        self.copy = pltpu.make_async_remote_copy(
            src_ref=g_ref.at[:, 1 - mc], dst_ref=out_ref,
            send_sem=send_sems.at[s], recv_sem=recv_sems.at[s],
            device_id=(mx, my, 1 - mc), device_id_type=MESH_ID)

    def start(self):
        self.copy.start()

    def forward(self):
        pass

    def finish(self):
        self.copy.wait()


class _ChipScatterTask:
    def __init__(self, t, s_ref, out_ref, send_sems, recv_sems, local_sems):
        self.t, self.s_ref, self.out_ref = t, s_ref, out_ref
        self.send_sems, self.recv_sems, self.local_sems = send_sems, recv_sems, local_sems
        mx, my, mc = lax.axis_index("x"), lax.axis_index("y"), lax.axis_index("c")
        self.chip = 2 * mx + my
        self.peers = []
        for k in range(1, N_DEV // 2):
            px, py = mx ^ ((k >> 1) & 1), my ^ (k & 1)
            self.peers.append(((px, py, mc), 2 * px + py))

    def _mine(self):
        return pltpu.make_async_copy(self.s_ref.at[self.chip], self.out_ref.at[self.chip],
                                     self.local_sems.at[self.t])

    def _copy(self, k, src_slot, dst_slot):
        s = COPIES_PER_TASK * self.t + k
        return pltpu.make_async_remote_copy(
            src_ref=self.s_ref.at[src_slot], dst_ref=self.out_ref.at[dst_slot],
            send_sem=self.send_sems.at[s], recv_sem=self.recv_sems.at[s],
            device_id=self.peers[k][0], device_id_type=MESH_ID)

    def start(self):
        self._mine().start()
        for k, (_, peer) in enumerate(self.peers):
            self._copy(k, peer, self.chip).start()

    def forward(self):
        pass

    def finish(self):
        for k, (_, peer) in enumerate(self.peers):
            self._copy(k, self.chip, peer).wait_recv()
        for k, (_, peer) in enumerate(self.peers):
            self._copy(k, peer, self.chip).wait_send()
        self._mine().wait()


_TASKS = {"ag": _AllGatherTask, "rs": _ReduceScatterTask, "rs_pair": _PairExchangeTask,
          "rs_chip": _ChipScatterTask}


def _task_shape(kind, arr):
    shape = {"ag": (N_DEV,) + arr.shape, "rs": arr.shape, "rs_chip": arr.shape,
             "rs_pair": arr.shape[:1] + arr.shape[2:]}[kind]
    return jax.ShapeDtypeStruct(shape, arr.dtype)


def _sem_shapes(n_tasks):
    return [pltpu.SemaphoreType.DMA((COPIES_PER_TASK * n_tasks,)),
            pltpu.SemaphoreType.DMA((COPIES_PER_TASK * n_tasks,)),
            pltpu.SemaphoreType.DMA((n_tasks,))]


def _make_tasks(rider, in_refs, out_refs, sems):
    return [_TASKS[kind](t, in_refs[t], out_refs[t], *sems) for t, (kind, _) in enumerate(rider)]


def exchange(rider, name):
    n = len(rider)

    def body(*refs):
        tasks = _make_tasks(rider, refs[:n], refs[n:2 * n], refs[2 * n:])
        for task in tasks:
            task.start()
        for task in tasks:
            task.forward()
        for task in tasks:
            task.finish()

    return list(pl.pallas_call(
        body, name=name, out_shape=tuple(_task_shape(k, a) for k, a in rider),
        in_specs=[ANY] * n, out_specs=(ANY,) * n, scratch_shapes=_sem_shapes(n),
    )(*[a for _, a in rider]))


def _call(body, name, grid, in_specs, out_specs, out_shape, args, scratch=(), rider=()):
    in_specs, out_specs, out_shape = list(in_specs), tuple(out_specs), tuple(out_shape)
    n_in, n_out, n_scr, n_r = len(in_specs), len(out_specs), len(scratch), len(rider)
    if n_r:
        def kern(*refs):
            ins, r_in = refs[:n_in], refs[n_in:n_in + n_r]
            o0 = n_in + n_r
            outs, r_out = refs[o0:o0 + n_out], refs[o0 + n_out:o0 + n_out + n_r]
            s0 = o0 + n_out + n_r
            scr, sems = refs[s0:s0 + n_scr], refs[s0 + n_scr:]
            step = 0
            for a, n in enumerate(grid):
                step = step * n + pl.program_id(a)
            n_steps = 1
            for n in grid:
                n_steps *= n

            @pl.when(step == 0)
            def _():
                for task in _make_tasks(rider, r_in, r_out, sems):
                    task.start()
            body(*ins, *outs, *scr)

            @pl.when(step == n_steps - 1)
            def _():
                tasks = _make_tasks(rider, r_in, r_out, sems)
                for task in tasks:
                    task.forward()
                for task in tasks:
                    task.finish()
    else:
        kern = body
    res = pl.pallas_call(
        kern, name=name, grid=grid,
        in_specs=in_specs + [ANY] * n_r, out_specs=out_specs + (ANY,) * n_r,
        out_shape=out_shape + tuple(_task_shape(k, a) for k, a in rider),
        scratch_shapes=list(scratch) + (_sem_shapes(n_r) if n_r else []),
        compiler_params=pltpu.CompilerParams(dimension_semantics=("arbitrary",) * len(grid),
                                             vmem_limit_bytes=V7X_VMEM_LIMIT),
    )(*args, *[a for _, a in rider])
    return list(res[:n_out]), list(res[n_out:])


def _adamw(g, w, m, v):
    m = ADAM_B1 * m + (1.0 - ADAM_B1) * g
    v = ADAM_B2 * v + (1.0 - ADAM_B2) * (g * g)
    m_hat = m / (1.0 - ADAM_B1 ** ADAM_STEP)
    v_hat = v / (1.0 - ADAM_B2 ** ADAM_STEP)
    delta = -ADAM_LR * (m_hat / (jnp.sqrt(v_hat) + ADAM_EPS) + ADAM_WD * w)
    return delta, m, v


def sum_adamw(parts, w, m, v, name, rider=()):
    n_l, rows, cols = w.shape
    mult = BF16_SUBLANES if parts[0].dtype == BF16 else 8
    tr = _row_tile(rows, 128, mult)
    n_i = rows // tr

    def body(*refs):
        part_refs = refs[:n_l]
        w_ref, m_ref, v_ref, g_out, d_out, m_out, v_out = refs[n_l:]
        layer = pl.program_id(0)
        for k in range(n_l):
            @pl.when(layer == k)
            def _(k=k):
                g = part_refs[k][0].astype(F32)
                for s in range(1, parts[k].shape[0]):
                    g = g + part_refs[k][s].astype(F32)
                delta, m_new, v_new = _adamw(g, w_ref[...], m_ref[...], v_ref[...])
                g_out[...] = g
                d_out[...] = delta
                m_out[...] = m_new
                v_out[...] = v_new

    part_specs = [pl.BlockSpec((parts[k].shape[0], tr, cols), functools.partial(
        lambda l, i, k: (0, jnp.where(l == k, i, 0), 0), k=k)) for k in range(n_l)]
    wspec = pl.BlockSpec((None, tr, cols), lambda l, i: (l, i, 0))
    shape = jax.ShapeDtypeStruct(w.shape, F32)
    return _call(body, name, (n_l, n_i), part_specs + [wspec] * 3, (wspec,) * 4, (shape,) * 4,
                 (*parts, w, m, v), rider=rider)


def sum_adamw_transposed(parts, w_t, m_t, v_t, name, rider=()):
    n_l, cols, rows = w_t.shape
    tr = LANES
    n_i = rows // tr
    starts = list(range(0, cols - LANES + 1, LANES))
    if starts[-1] + LANES < cols:
        starts.append(cols - LANES)

    def body(*refs):
        part_refs = refs[:n_l]
        w_ref, m_ref, v_ref, g_out, d_out, m_out, v_out = refs[n_l:]
        layer = pl.program_id(0)
        for k in range(n_l):
            @pl.when(layer == k)
            def _(k=k):
                for c0 in starts:
                    piece = pl.ds(c0, LANES)
                    g = part_refs[k][0, :, piece].astype(F32)
                    for s in range(1, parts[k].shape[0]):
                        g = g + part_refs[k][s, :, piece].astype(F32)
                    g = g.T
                    delta, m_new, v_new = _adamw(g, w_ref[piece, :], m_ref[piece, :], v_ref[piece, :])
                    g_out[piece, :] = g
                    d_out[piece, :] = delta
                    m_out[piece, :] = m_new
                    v_out[piece, :] = v_new

    part_specs = [pl.BlockSpec((parts[k].shape[0], tr, cols), functools.partial(
        lambda l, i, k: (0, jnp.where(l == k, i, 0), 0), k=k)) for k in range(n_l)]
    wspec = pl.BlockSpec((None, cols, tr), lambda l, i: (l, 0, i))
    shape = jax.ShapeDtypeStruct(w_t.shape, F32)
    return _call(body, name, (n_l, n_i), part_specs + [wspec] * 3, (wspec,) * 4, (shape,) * 4,
                 (*parts, w_t, m_t, v_t), rider=rider)


def pair_sum(g4, other, name):
    n_chip, _, rows, cols = g4.shape
    tr = _row_tile(rows, 256, BF16_SUBLANES)

    def body(g_ref, o_ref, s_ref):
        mine = g_ref[lax.axis_index("c")]
        s_ref[...] = (mine.astype(F32) + o_ref[...].astype(F32)).astype(BF16)

    blk = pl.BlockSpec((None, tr, cols), lambda k, i: (k, i, 0))
    return _call(body, name, (n_chip, rows // tr),
                 [pl.BlockSpec((None, 2, tr, cols), lambda k, i: (k, 0, i, 0)), blk], [blk],
                 [jax.ShapeDtypeStruct((n_chip, rows, cols), g4.dtype)], (g4, other))[0][0]


def sum_slots(parts, name):
    n, rows, cols = parts.shape

    def body(p_ref, o_ref):
        acc = p_ref[0]
        for s in range(1, n):
            acc = acc + p_ref[s]
        o_ref[...] = acc

    return pl.pallas_call(
        body, name=name, out_shape=jax.ShapeDtypeStruct((rows, cols), F32),
        in_specs=[pl.BlockSpec(memory_space=pltpu.VMEM)],
        out_specs=pl.BlockSpec(memory_space=pltpu.VMEM),
    )(parts)


def norm_fwd(h, g, name):
    seq, d = h.shape
    tm = min(ROW_TILE, seq)

    def body(h_ref, g_ref, o_ref):
        o_ref[...] = _rms(h_ref[...], g_ref[...])[0].astype(BF16)

    return _call(body, name, (seq // tm,),
                 [pl.BlockSpec((tm, d), lambda i: (i, 0)), pl.BlockSpec((1, d), lambda i: (0, 0))],
                 [pl.BlockSpec((tm, d), lambda i: (i, 0))],
                 [jax.ShapeDtypeStruct((seq, d), BF16)], (h, g))[0][0]


def proj_residual(a, w, res, name, g_next=None, rider=()):
    nb, seq, kb = a.shape
    d = w.shape[-1]
    tm = min(ROW_TILE, seq)
    with_norm = g_next is not None

    def body(a_ref, w_ref, r_ref, *rest):
        acc = r_ref[...]
        for b in range(nb):
            acc = acc + _nn(a_ref[b], w_ref[b])
        if with_norm:
            g_ref, o_ref, xn_ref = rest
            xn_ref[...] = _rms(acc, g_ref[...])[0].astype(BF16)
        else:
            (o_ref,) = rest
        o_ref[...] = acc

    row = pl.BlockSpec((tm, d), lambda i: (i, 0))
    in_specs = [pl.BlockSpec((nb, tm, kb), lambda i: (0, i, 0)),
                pl.BlockSpec((nb, kb, d), lambda i: (0, 0, 0)), row]
    args = [a, w, res]
    out_specs, out_shape = [row], [jax.ShapeDtypeStruct((seq, d), F32)]
    if with_norm:
        in_specs.append(pl.BlockSpec((1, d), lambda i: (0, 0)))
        args.append(g_next)
        out_specs.append(row)
        out_shape.append(jax.ShapeDtypeStruct((seq, d), BF16))
    outs, r_outs = _call(body, name, (seq // tm,), in_specs, out_specs, out_shape, args, rider=rider)
    return (outs[0], outs[1] if with_norm else None), r_outs


def proj_t_rms_bwd(du, w, h, g, dres, name, rider=()):
    nb, seq, wd = du.shape
    k = w.shape[1]
    big_weight = 2 * w.size * w.dtype.itemsize > V7X_VMEM_LIMIT // 4
    tm = min(ROW_TILE_SMALL if big_weight else ROW_TILE, seq)

    def body(du_ref, w_ref, h_ref, g_ref, dr_ref, dh_ref, dg_ref):
        i = pl.program_id(0)
        dxn = _nt(du_ref[0], w_ref[0])
        for b in range(1, nb):
            dxn = dxn + _nt(du_ref[b], w_ref[b])
        _, xhat, rstd = _rms(h_ref[...], g_ref[...])
        dh, dg_rows = _rms_bwd(dxn, xhat, rstd, g_ref[...])
        dh_ref[...] = dr_ref[...] + dh

        @pl.when(i == 0)
        def _():
            dg_ref[...] = jnp.zeros_like(dg_ref)
        dg_ref[...] += jnp.sum(dg_rows, axis=0, keepdims=True)

    row = pl.BlockSpec((tm, k), lambda i: (i, 0))
    vec = pl.BlockSpec((1, k), lambda i: (0, 0))
    return _call(body, name, (seq // tm,),
                 [pl.BlockSpec((nb, tm, wd), lambda i: (0, i, 0)),
                  pl.BlockSpec((nb, k, wd), lambda i: (0, 0, 0)), row, vec, row],
                 (row, vec),
                 (jax.ShapeDtypeStruct((seq, k), F32), jax.ShapeDtypeStruct((1, k), F32)),
                 (du, w, h, g, dres), rider=rider)


def mixer_fwd(xn, win3, cw, name, rider=()):
    seq, d = xn.shape
    tm = min(ROW_TILE_LARGE, seq)
    cc = min(MIXER_CHUNK, d)
    n_c, n_i = d // cc, seq // tm

    def body(x_ref, w_ref, cw_ref, u_ref, z_ref, carry):
        i = pl.program_id(1)

        @pl.when(i == 0)
        def _():
            carry[...] = jnp.zeros_like(carry)
        xb = x_ref[...]
        b = _nn(xb, w_ref[0])
        c = _nn(xb, w_ref[1])
        hh = _nn(xb, w_ref[2])
        p = c * hh
        w0, w1, w2 = _conv_taps(cw_ref)
        p1 = _shift_down(p, 1, [carry[7:8, :]])
        p2 = _shift_down(p, 2, [carry[6:7, :], carry[7:8, :]])
        q = w0 * p2 + w1 * p1 + w2 * p
        carry[...] = p[tm - 8:tm, :]
        u_ref[0] = b.astype(BF16)
        u_ref[1] = c.astype(BF16)
        u_ref[2] = hh.astype(BF16)
        u_ref[3] = q.astype(BF16)
        z_ref[...] = (b * q).astype(BF16)

    return _call(body, name, (n_c, n_i),
                 [pl.BlockSpec((tm, d), lambda c, i: (i, 0)),
                  pl.BlockSpec((3, d, cc), lambda c, i: (0, 0, c)),
                  pl.BlockSpec((3, cc), lambda c, i: (0, c))],
                 (pl.BlockSpec((4, tm, cc), lambda c, i: (0, i, c)),
                  pl.BlockSpec((tm, cc), lambda c, i: (i, c))),
                 (jax.ShapeDtypeStruct((4, seq, d), BF16), jax.ShapeDtypeStruct((seq, d), BF16)),
                 (xn, win3, cw), scratch=[pltpu.VMEM((8, cc), F32)], rider=rider)


def mixer_bwd(dh, wout, u4, z, xn, cw, name, rider=()):
    seq, d = xn.shape
    tm = min(ROW_TILE, seq)
    cc = min(MIXER_CHUNK, d)
    n_c, n_i = d // cc, seq // tm

    def body(dh_ref, wout_ref, u_ref, z_ref, x_ref, cw_ref,
             du_ref, dwin_ref, dwout_ref, dcw_ref, acc_in, acc_out, acc_cw, carry):
        i = pl.program_id(1)

        @pl.when(i == 0)
        def _():
            acc_in[...] = jnp.zeros_like(acc_in)
            acc_out[...] = jnp.zeros_like(acc_out)
            acc_cw[...] = jnp.zeros_like(acc_cw)
            carry[...] = jnp.zeros_like(carry)
        dhb = dh_ref[...].astype(BF16)
        dz = _nt(dhb, wout_ref[...])
        acc_out[...] += _tn(z_ref[...], dhb)
        b = u_ref[0].astype(F32)
        c = u_ref[1].astype(F32)
        hh = u_ref[2].astype(F32)
        q = u_ref[3].astype(F32)
        p = c * hh
        db = dz * q
        dq = dz * b
        w0, w1, w2 = _conv_taps(cw_ref)
        dq1 = _shift_up(dq, 1, [carry[0:1, :]])
        dq2 = _shift_up(dq, 2, [carry[0:1, :], carry[1:2, :]])
        dp = w2 * dq + w1 * dq1 + w0 * dq2
        carry[...] = dq[0:8, :]
        acc_cw[0:1, :] += jnp.sum(dq2 * p, axis=0, keepdims=True)
        acc_cw[1:2, :] += jnp.sum(dq1 * p, axis=0, keepdims=True)
        acc_cw[2:3, :] += jnp.sum(dq * p, axis=0, keepdims=True)
        dbb = db.astype(BF16)
        dcb = (dp * hh).astype(BF16)
        dhhb = (dp * c).astype(BF16)
        du_ref[0] = dbb
        du_ref[1] = dcb
        du_ref[2] = dhhb
        xb = x_ref[...]
        acc_in[0] += _tn(xb, dbb)
        acc_in[1] += _tn(xb, dcb)
        acc_in[2] += _tn(xb, dhhb)

        @pl.when(i == n_i - 1)
        def _():
            dwin_ref[...] = acc_in[...].astype(BF16)
            dwout_ref[...] = acc_out[...].astype(BF16)
            dcw_ref[...] = acc_cw[0:3, :]

    rev = lambda c, i: (n_i - 1 - i, 0)
    return _call(body, name, (n_c, n_i),
                 [pl.BlockSpec((tm, d), rev),
                  pl.BlockSpec((cc, d), lambda c, i: (c, 0)),
                  pl.BlockSpec((4, tm, cc), lambda c, i: (0, n_i - 1 - i, c)),
                  pl.BlockSpec((tm, cc), lambda c, i: (n_i - 1 - i, c)),
                  pl.BlockSpec((tm, d), rev),
                  pl.BlockSpec((3, cc), lambda c, i: (0, c))],
                 (pl.BlockSpec((3, tm, cc), lambda c, i: (0, n_i - 1 - i, c)),
                  pl.BlockSpec((3, d, cc), lambda c, i: (0, 0, c)),
                  pl.BlockSpec((cc, d), lambda c, i: (c, 0)),
                  pl.BlockSpec((3, cc), lambda c, i: (0, c))),
                 (jax.ShapeDtypeStruct((3, seq, d), BF16), jax.ShapeDtypeStruct((3, d, d), BF16),
                  jax.ShapeDtypeStruct((d, d), BF16), jax.ShapeDtypeStruct((3, d), F32)),
                 (dh, wout, u4, z, xn, cw),
                 scratch=[pltpu.VMEM((3, d, cc), F32), pltpu.VMEM((cc, d), F32),
                          pltpu.VMEM((8, cc), F32), pltpu.VMEM((8, cc), F32)], rider=rider)


def _silu_parts(cg):
    sg = 1.0 / (1.0 + jnp.exp(-cg))
    return sg, cg * sg


def ffn_fwd(xn, wup, fcw, name, rider=()):
    seq, d = xn.shape
    f8 = wup.shape[-1]
    half = N_DEV // 2
    tm = min(ROW_TILE_LARGE, seq)
    n_i = seq // tm

    def body(x_ref, wg_ref, wu_ref, cg_ref, cu_ref, up_ref, cv_ref, a_ref, carry):
        i = pl.program_id(1)

        @pl.when(i == 0)
        def _():
            carry[...] = jnp.zeros_like(carry)
        xb = x_ref[...]
        conv = []
        for s, (w_ref, t_ref) in enumerate(((wg_ref, cg_ref), (wu_ref, cu_ref))):
            u = _nn(xb, w_ref[...])
            up_ref[s] = u.astype(BF16)
            w0, w1, w2 = _conv_taps(t_ref)
            u1 = _shift_down(u, 1, [carry[s, 7:8, :]])
            u2 = _shift_down(u, 2, [carry[s, 6:7, :], carry[s, 7:8, :]])
            cv = w0 * u2 + w1 * u1 + w2 * u
            cv_ref[s] = cv.astype(BF16)
            conv.append(cv)
            carry[s] = u[tm - 8:tm, :]
        _, silu = _silu_parts(conv[0])
        a_ref[...] = (silu * conv[1]).astype(BF16)

    blk = pl.BlockSpec((2, None, tm, f8), lambda c, i: (0, c, i, 0))
    big = jax.ShapeDtypeStruct((2, half, seq, f8), BF16)
    return _call(body, name, (half, n_i),
                 [pl.BlockSpec((tm, d), lambda c, i: (i, 0)),
                  pl.BlockSpec((None, d, f8), lambda c, i: (c, 0, 0)),
                  pl.BlockSpec((None, d, f8), lambda c, i: (c + half, 0, 0)),
                  pl.BlockSpec((None, 3, f8), lambda c, i: (c, 0, 0)),
                  pl.BlockSpec((None, 3, f8), lambda c, i: (c + half, 0, 0))],
                 (blk, blk, pl.BlockSpec((None, tm, f8), lambda c, i: (c, i, 0))),
                 (big, big, jax.ShapeDtypeStruct((half, seq, f8), BF16)),
                 (xn, wup, wup, fcw, fcw), scratch=[pltpu.VMEM((2, 8, f8), F32)], rider=rider)


def ffn_bwd(dh, wdown, up2, cv2, act, xn, fcw, name, rider=()):
    seq, d = xn.shape
    f8 = up2.shape[-1]
    fb = wdown.shape[1]
    half = N_DEV // 2
    tm = min(ROW_TILE, seq)
    n_i = seq // tm

    def body(dh_ref, wd_ref, up_ref, cv_ref, a_ref, x_ref, cg_ref, cu_ref,
             dup_ref, dwup_ref, dwd_ref, dcw_ref, acc_up, acc_down, acc_cw, carry):
        i = pl.program_id(1)

        @pl.when(i == 0)
        def _():
            acc_up[...] = jnp.zeros_like(acc_up)
            acc_down[...] = jnp.zeros_like(acc_down)
            acc_cw[...] = jnp.zeros_like(acc_cw)
            carry[...] = jnp.zeros_like(carry)
        dhb = dh_ref[...].astype(BF16)
        da = _nt(dhb, wd_ref[...])
        acc_down[...] += _tn(a_ref[...], dhb)
        cg = cv_ref[0].astype(F32)
        cu = cv_ref[1].astype(F32)
        sg, silu = _silu_parts(cg)
        dcg = da * cu * (sg + silu * (1.0 - sg))
        dcu = da * silu
        xb = x_ref[...]
        for s, (dc, t_ref) in enumerate(((dcg, cg_ref), (dcu, cu_ref))):
            w0, w1, w2 = _conv_taps(t_ref)
            d1 = _shift_up(dc, 1, [carry[s, 0:1, :]])
            d2 = _shift_up(dc, 2, [carry[s, 0:1, :], carry[s, 1:2, :]])
            du = (w2 * dc + w1 * d1 + w0 * d2).astype(BF16)
            carry[s] = dc[0:8, :]
            u = up_ref[s].astype(F32)
            acc_cw[s, 0:1, :] += jnp.sum(d2 * u, axis=0, keepdims=True)
            acc_cw[s, 1:2, :] += jnp.sum(d1 * u, axis=0, keepdims=True)
            acc_cw[s, 2:3, :] += jnp.sum(dc * u, axis=0, keepdims=True)
            dup_ref[s] = du
            acc_up[s] += _tn(xb, du)

        @pl.when(i == n_i - 1)
        def _():
            dwup_ref[...] = acc_up[...].astype(BF16)
            dwd_ref[...] = acc_down[...].astype(BF16)
            dcw_ref[...] = acc_cw[:, 0:3, :]

    rev = lambda c, i: (n_i - 1 - i, 0)
    blk = pl.BlockSpec((2, None, tm, f8), lambda c, i: (0, c, n_i - 1 - i, 0))
    return _call(body, name, (half, n_i),
                 [pl.BlockSpec((tm, d), rev),
                  pl.BlockSpec((None, fb, d), lambda c, i: (c, 0, 0)),
                  blk, blk,
                  pl.BlockSpec((None, tm, f8), lambda c, i: (c, n_i - 1 - i, 0)),
                  pl.BlockSpec((tm, d), rev),
                  pl.BlockSpec((None, 3, f8), lambda c, i: (c, 0, 0)),
                  pl.BlockSpec((None, 3, f8), lambda c, i: (c + half, 0, 0))],
                 (blk,
                  pl.BlockSpec((2, None, d, f8), lambda c, i: (0, c, 0, 0)),
                  pl.BlockSpec((None, fb, d), lambda c, i: (c, 0, 0)),
                  pl.BlockSpec((2, None, 3, f8), lambda c, i: (0, c, 0, 0))),
                 (jax.ShapeDtypeStruct((2, half, seq, f8), BF16),
                  jax.ShapeDtypeStruct((2, half, d, f8), BF16),
                  jax.ShapeDtypeStruct((half, fb, d), BF16),
                  jax.ShapeDtypeStruct((2, half, 3, f8), F32)),
                 (dh, wdown, up2, cv2, act, xn, fcw, fcw),
                 scratch=[pltpu.VMEM((2, d, f8), F32), pltpu.VMEM((fb, d), F32),
                          pltpu.VMEM((2, 8, f8), F32), pltpu.VMEM((2, 8, f8), F32)], rider=rider)


def q_fwd(xn, wdq, gq, wuq, cos, sin, name, rider=()):
    seq, d = xn.shape
    rank = wdq.shape[-1]
    tm = min(ROW_TILE, seq)

    def body(x_ref, wdq_ref, gq_ref, wuq_ref, cos_ref, sin_ref, q_ref):
        qc = _nn(x_ref[...], wdq_ref[...])
        qn = _rms(qc, gq_ref[...])[0].astype(BF16)
        for hd in range(N_HEADS):
            qh = _nn(qn, wuq_ref[hd])
            qr = _rope_fwd(qh[:, NOPE:QK], cos_ref[...], sin_ref[...])
            q_ref[hd, :, 0:NOPE] = (qh[:, 0:NOPE] * ATTN_SCALE).astype(BF16)
            q_ref[hd, :, NOPE:QK] = (qr * ATTN_SCALE).astype(BF16)

    rope = pl.BlockSpec((tm, ROPE_PAD), lambda i: (i, 0))
    return _call(body, name, (seq // tm,),
                 [pl.BlockSpec((tm, d), lambda i: (i, 0)),
                  pl.BlockSpec((d, rank), lambda i: (0, 0)),
                  pl.BlockSpec((1, rank), lambda i: (0, 0)),
                  pl.BlockSpec((N_HEADS, rank, QK), lambda i: (0, 0, 0)), rope, rope],
                 [pl.BlockSpec((N_HEADS, tm, QK), lambda i: (0, i, 0))],
                 [jax.ShapeDtypeStruct((N_HEADS, seq, QK), BF16)],
                 (xn, wdq, gq, wuq, cos, sin), rider=rider)


def q_bwd(dq, xn, wdq, gq, wuq, cos, sin, name, rider=()):
    seq, d = xn.shape
    rank = wdq.shape[-1]
    tm = min(ROW_TILE, seq)
    n_i = seq // tm

    def body(dq_ref, x_ref, wdq_ref, gq_ref, wuq_ref, cos_ref, sin_ref,
             dqc_ref, dwuq_ref, dwdq_ref, dgq_ref, acc_uq, acc_dq):
        i = pl.program_id(0)

        @pl.when(i == 0)
        def _():
            acc_uq[...] = jnp.zeros_like(acc_uq)
            acc_dq[...] = jnp.zeros_like(acc_dq)
            dgq_ref[...] = jnp.zeros_like(dgq_ref)
        xb = x_ref[...]
        qc = _nn(xb, wdq_ref[...])
        qn, qhat, qrstd = _rms(qc, gq_ref[...])
        qnb = qn.astype(BF16)
        dqn = jnp.zeros((tm, rank), F32)
        for hd in range(N_HEADS):
            dnope = (dq_ref[hd, :, 0:NOPE].astype(F32) * ATTN_SCALE).astype(BF16)
            drope = _rope_bwd(dq_ref[hd, :, NOPE:QK].astype(F32) * ATTN_SCALE, cos_ref[...], sin_ref[...])
            draw = jnp.concatenate([dnope, drope.astype(BF16)], axis=1)
            dqn = dqn + _nt(draw, wuq_ref[hd])
            acc_uq[hd] += _tn(qnb, draw)
        dqc, dg_rows = _rms_bwd(dqn, qhat, qrstd, gq_ref[...])
        dgq_ref[...] += jnp.sum(dg_rows, axis=0, keepdims=True)
        dqcb = dqc.astype(BF16)
        dqc_ref[0] = dqcb
        acc_dq[...] += _tn(xb, dqcb)

        @pl.when(i == n_i - 1)
        def _():
            dwuq_ref[...] = acc_uq[...].astype(BF16)
            dwdq_ref[...] = acc_dq[...].astype(BF16)

    rope = pl.BlockSpec((tm, ROPE_PAD), lambda i: (i, 0))
    return _call(body, name, (n_i,),
                 [pl.BlockSpec((N_HEADS, tm, QK), lambda i: (0, i, 0)),
                  pl.BlockSpec((tm, d), lambda i: (i, 0)),
                  pl.BlockSpec((d, rank), lambda i: (0, 0)),
                  pl.BlockSpec((1, rank), lambda i: (0, 0)),
                  pl.BlockSpec((N_HEADS, rank, QK), lambda i: (0, 0, 0)), rope, rope],
                 (pl.BlockSpec((1, tm, rank), lambda i: (0, i, 0)),
                  pl.BlockSpec((N_HEADS, rank, QK), lambda i: (0, 0, 0)),
                  pl.BlockSpec((d, rank), lambda i: (0, 0)),
                  pl.BlockSpec((1, rank), lambda i: (0, 0))),
                 (jax.ShapeDtypeStruct((1, seq, rank), BF16),
                  jax.ShapeDtypeStruct((N_HEADS, rank, QK), BF16),
                  jax.ShapeDtypeStruct((d, rank), BF16),
                  jax.ShapeDtypeStruct((1, rank), F32)),
                 (dq, xn, wdq, gq, wuq, cos, sin),
                 scratch=[pltpu.VMEM((N_HEADS, rank, QK), F32), pltpu.VMEM((d, rank), F32)],
                 rider=rider)


def kv_fwd(h, g, wdkv, gkv, wukv, cos, sin, name, rider=()):
    seq, d = h.shape
    tm = min(ROW_TILE, seq)
    wk = KV_RANK + ROPE_PAD

    def body(h_ref, g_ref, wdkv_ref, gkv_ref, wukv_ref, cos_ref, sin_ref, k_ref, v_ref, c_ref):
        xk = _rms(h_ref[...], g_ref[...])[0].astype(BF16)
        ckv = _nn(xk, wdkv_ref[...])
        c_kv = ckv[:, 0:KV_RANK]
        c_ref[...] = c_kv
        kr = _rope_fwd(ckv[:, KV_RANK:wk], cos_ref[...], sin_ref[...]).astype(BF16)
        ckn = _rms(c_kv, gkv_ref[...])[0].astype(BF16)
        for hd in range(N_HEADS):
            kvh = _nn(ckn, wukv_ref[hd])
            k_ref[hd, :, 0:NOPE] = kvh[:, 0:NOPE].astype(BF16)
            k_ref[hd, :, NOPE:QK] = kr
            v_ref[hd] = kvh[:, NOPE:NOPE + VDIM].astype(BF16)

    rope = pl.BlockSpec((tm, ROPE_PAD), lambda i: (i, 0))
    return _call(body, name, (seq // tm,),
                 [pl.BlockSpec((tm, d), lambda i: (i, 0)),
                  pl.BlockSpec((1, d), lambda i: (0, 0)),
                  pl.BlockSpec((d, wk), lambda i: (0, 0)),
                  pl.BlockSpec((1, KV_RANK), lambda i: (0, 0)),
                  pl.BlockSpec((N_HEADS, KV_RANK, NOPE + VDIM), lambda i: (0, 0, 0)), rope, rope],
                 (pl.BlockSpec((N_HEADS, tm, QK), lambda i: (0, i, 0)),
                  pl.BlockSpec((N_HEADS, tm, VDIM), lambda i: (0, i, 0)),
                  pl.BlockSpec((tm, KV_RANK), lambda i: (i, 0))),
                 (jax.ShapeDtypeStruct((N_HEADS, seq, QK), BF16),
                  jax.ShapeDtypeStruct((N_HEADS, seq, VDIM), BF16),
                  jax.ShapeDtypeStruct((seq, KV_RANK), F32)),
                 (h, g, wdkv, gkv, wukv, cos, sin), rider=rider)


def kv_bwd(dks, dvs, c_kv, h, g, gkv, wukv, cos, sin, name, rider=()):
    seq, d = h.shape
    tm = min(ROW_TILE, seq)
    n_i = seq // tm
    wk = KV_RANK + ROPE_PAD
    n_b = len(dks)

    def body(*refs):
        dk_refs = refs[:n_b]
        dv_refs = refs[n_b:2 * n_b]
        (c_ref, h_ref, g_ref, gkv_ref, wukv_ref, cos_ref, sin_ref,
         dckv_ref, dwukv_ref, dwdkv_ref, dgkv_ref, acc_ukv, acc_dkv) = refs[2 * n_b:]
        i = pl.program_id(0)

        @pl.when(i == 0)
        def _():
            acc_ukv[...] = jnp.zeros_like(acc_ukv)
            acc_dkv[...] = jnp.zeros_like(acc_dkv)
            dgkv_ref[...] = jnp.zeros_like(dgkv_ref)
        ckn, chat, crstd = _rms(c_ref[...], gkv_ref[...])
        cknb = ckn.astype(BF16)
        dckn = jnp.zeros((tm, KV_RANK), F32)
        dkr = jnp.zeros((tm, ROPE_PAD), F32)
        for hd in range(N_HEADS):
            dk = dk_refs[0][hd].astype(F32)
            dv = dv_refs[0][hd].astype(F32)
            for j in range(1, n_b):
                dk = dk + dk_refs[j][hd].astype(F32)
                dv = dv + dv_refs[j][hd].astype(F32)
            dkr = dkr + dk[:, NOPE:QK]
            dkvh = jnp.concatenate([dk[:, 0:NOPE].astype(BF16), dv.astype(BF16)], axis=1)
            dckn = dckn + _nt(dkvh, wukv_ref[hd])
            acc_ukv[hd] += _tn(cknb, dkvh)
        dc_kv, dg_rows = _rms_bwd(dckn, chat, crstd, gkv_ref[...])
        dgkv_ref[...] += jnp.sum(dg_rows, axis=0, keepdims=True)
        dkr_raw = _rope_bwd(dkr, cos_ref[...], sin_ref[...])
        dckv = jnp.concatenate([dc_kv.astype(BF16), dkr_raw.astype(BF16)], axis=1)
        dckv_ref[0] = dckv
        xk = _rms(h_ref[...], g_ref[...])[0].astype(BF16)
        acc_dkv[...] += _tn(xk, dckv)

        @pl.when(i == n_i - 1)
        def _():
            dwukv_ref[...] = acc_ukv[...].astype(BF16)
            dwdkv_ref[...] = acc_dkv[...].astype(BF16)

    kspec = pl.BlockSpec((N_HEADS, tm, QK), lambda i: (0, i, 0))
    vspec = pl.BlockSpec((N_HEADS, tm, VDIM), lambda i: (0, i, 0))
    rope = pl.BlockSpec((tm, ROPE_PAD), lambda i: (i, 0))
    return _call(body, name, (n_i,),
                 [kspec] * n_b + [vspec] * n_b + [
                     pl.BlockSpec((tm, KV_RANK), lambda i: (i, 0)),
                     pl.BlockSpec((tm, d), lambda i: (i, 0)),
                     pl.BlockSpec((1, d), lambda i: (0, 0)),
                     pl.BlockSpec((1, KV_RANK), lambda i: (0, 0)),
                     pl.BlockSpec((N_HEADS, KV_RANK, NOPE + VDIM), lambda i: (0, 0, 0)), rope, rope],
                 (pl.BlockSpec((1, tm, wk), lambda i: (0, i, 0)),
                  pl.BlockSpec((N_HEADS, KV_RANK, NOPE + VDIM), lambda i: (0, 0, 0)),
                  pl.BlockSpec((d, wk), lambda i: (0, 0)),
                  pl.BlockSpec((1, KV_RANK), lambda i: (0, 0))),
                 (jax.ShapeDtypeStruct((1, seq, wk), BF16),
                  jax.ShapeDtypeStruct((N_HEADS, KV_RANK, NOPE + VDIM), BF16),
                  jax.ShapeDtypeStruct((d, wk), BF16),
                  jax.ShapeDtypeStruct((1, KV_RANK), F32)),
                 (*dks, *dvs, c_kv, h, g, gkv, wukv, cos, sin),
                 scratch=[pltpu.VMEM((N_HEADS, KV_RANK, NOPE + VDIM), F32), pltpu.VMEM((d, wk), F32)],
                 rider=rider)


def o_bwd(dh, o, wo, name, rider=()):
    seq, d = dh.shape
    hv = o.shape[1]
    tm = min(ROW_TILE, seq)
    n_i = seq // tm

    def body(dh_ref, o_ref, wo_ref, do_ref, dwo_ref, acc):
        i = pl.program_id(0)

        @pl.when(i == 0)
        def _():
            acc[...] = jnp.zeros_like(acc)
        dhb = dh_ref[...].astype(BF16)
        do_ref[...] = _nt(dhb, wo_ref[...]).astype(BF16)
        acc[...] += _tn(o_ref[...], dhb)

        @pl.when(i == n_i - 1)
        def _():
            dwo_ref[...] = acc[...].astype(BF16)

    return _call(body, name, (n_i,),
                 [pl.BlockSpec((tm, d), lambda i: (i, 0)),
                  pl.BlockSpec((tm, hv), lambda i: (i, 0)),
                  pl.BlockSpec((hv, d), lambda i: (0, 0))],
                 (pl.BlockSpec((tm, hv), lambda i: (i, 0)),
                  pl.BlockSpec((hv, d), lambda i: (0, 0))),
                 (jax.ShapeDtypeStruct((seq, hv), BF16), jax.ShapeDtypeStruct((hv, d), BF16)),
                 (dh, o, wo), scratch=[pltpu.VMEM((hv, d), F32)], rider=rider)


def _mask_diagonal(s):
    row = lax.broadcasted_iota(jnp.int32, s.shape, 0)
    col = lax.broadcasted_iota(jnp.int32, s.shape, 1)
    return jnp.where(col <= row, s, NEG_BIG)


def attn_fwd(q, k, v, name, rider=()):
    _, seq, _ = q.shape
    t = min(ATTN_TILE, seq // 2)
    n_pair = seq // (2 * t)

    def body(q_ref, k_ref, v_ref, o_ref, lse_ref):
        qi = pl.program_id(1)
        q_a = q_ref[0:t, :]
        q_b = q_ref[t:2 * t, :]

        def rows(j):
            return pl.ds(pl.multiple_of(j * t, t), t)

        def update(qx, kb, vb, state, diagonal=False):
            m, l, acc = state
            s = _nt(qx, kb)
            if diagonal:
                s = _mask_diagonal(s)
            m_new = jnp.maximum(m, jnp.max(s, axis=1, keepdims=True))
            p = jnp.exp(s - m_new)
            alpha = jnp.exp(m - m_new)
            l = alpha * l + jnp.sum(p, axis=1, keepdims=True)
            acc = alpha * acc + _nn(p.astype(BF16), vb)
            return m_new, l, acc

        def step(j, carry):
            both = pl.ds(pl.multiple_of(j * 2 * t, 2 * t), 2 * t)
            kb, vb = k_ref[both, :], v_ref[both, :]
            return update(q_a, kb, vb, carry[0:3]) + update(q_b, kb, vb, carry[3:6])

        init = (jnp.full((t, 1), NEG_BIG, F32), jnp.zeros((t, 1), F32), jnp.zeros((t, VDIM), F32))
        carry = lax.fori_loop(0, qi, step, init + init)
        k0, v0 = k_ref[rows(2 * qi), :], v_ref[rows(2 * qi), :]
        k1, v1 = k_ref[rows(2 * qi + 1), :], v_ref[rows(2 * qi + 1), :]
        state_a = update(q_a, k0, v0, carry[0:3], diagonal=True)
        state_b = update(q_b, k1, v1, update(q_b, k0, v0, carry[3:6]), diagonal=True)
        for half, (m, l, acc) in enumerate((state_a, state_b)):
            o_ref[half * t:(half + 1) * t, :] = (acc / l).astype(BF16)
            lse_ref[half * t:(half + 1) * t, :] = jnp.broadcast_to(m + jnp.log(l), (t, LANES))

    return _call(body, name, (N_HEADS, n_pair),
                 [pl.BlockSpec((None, 2 * t, QK), lambda h, i: (h, i, 0)),
                  pl.BlockSpec((None, seq, QK), lambda h, i: (h, 0, 0)),
                  pl.BlockSpec((None, seq, VDIM), lambda h, i: (h, 0, 0))],
                 (pl.BlockSpec((2 * t, VDIM), lambda h, i: (i, h)),
                  pl.BlockSpec((None, 2 * t, LANES), lambda h, i: (h, i, 0))),
                 (jax.ShapeDtypeStruct((seq, N_HEADS * VDIM), BF16),
                  jax.ShapeDtypeStruct((N_HEADS, seq, LANES), F32)),
                 (q, k, v), rider=rider)


def attn_bwd(q, k, v, o, do, lse, name, rider=()):
    _, seq, _ = q.shape
    t = min(ATTN_TILE, seq // 2)
    n_q = seq // t
    n_pair = n_q // 2

    def body(q_ref, k_ref, v_ref, o_ref, do_ref, lse_ref, dq_ref, dk_ref, dv_ref,
             dq_acc, dk_acc, dv_acc):
        kj = pl.program_id(1)

        @pl.when(kj == 0)
        def _():
            dq_acc[...] = jnp.zeros_like(dq_acc)
        dk_acc[...] = jnp.zeros_like(dk_acc)
        dv_acc[...] = jnp.zeros_like(dv_acc)
        halves = (slice(0, t), slice(t, 2 * t))

        def block(i, masks, n_rows=t):
            rows = pl.ds(pl.multiple_of(i * n_rows, n_rows), n_rows)
            qb = q_ref[rows, :]
            dob = do_ref[rows, :]
            lse_col = lse_ref[rows, 0:1]
            delta = jnp.sum(dob.astype(F32) * o_ref[rows, :].astype(F32), axis=1, keepdims=True)
            dq = None
            for x, diagonal in enumerate(masks):
                if diagonal is None:
                    continue
                kb, vb = k_ref[halves[x], :], v_ref[halves[x], :]
                s = _nt(qb, kb)
                if diagonal:
                    s = _mask_diagonal(s)
                p = jnp.exp(s - lse_col)
                ds = (p * (_nt(dob, vb) - delta)).astype(BF16)
                dv_acc[halves[x], :] += _tn(p.astype(BF16), dob)
                dk_acc[halves[x], :] += _tn(ds, qb)
                part = _nn(ds, kb)
                dq = part if dq is None else dq + part
            dq_acc[rows, :] += dq

        block(2 * kj, (True, None))
        block(2 * kj + 1, (False, True))

        def step(i, carry):
            block(i, (False, False), n_rows=2 * t)
            return carry

        lax.fori_loop(kj + 1, n_pair, step, 0)
        dk_ref[...] = dk_acc[...].astype(BF16)
        dv_ref[...] = dv_acc[...].astype(BF16)

        @pl.when(kj == n_pair - 1)
        def _():
            dq_ref[...] = dq_acc[...].astype(BF16)

    head_rows = pl.BlockSpec((seq, VDIM), lambda h, j: (0, h))
    return _call(body, name, (N_HEADS, n_pair),
                 [pl.BlockSpec((None, seq, QK), lambda h, j: (h, 0, 0)),
                  pl.BlockSpec((None, 2 * t, QK), lambda h, j: (h, j, 0)),
                  pl.BlockSpec((None, 2 * t, VDIM), lambda h, j: (h, j, 0)),
                  head_rows, head_rows,
                  pl.BlockSpec((None, seq, LANES), lambda h, j: (h, 0, 0))],
                 (pl.BlockSpec((None, seq, QK), lambda h, j: (h, 0, 0)),
                  pl.BlockSpec((None, 2 * t, QK), lambda h, j: (h, j, 0)),
                  pl.BlockSpec((None, 2 * t, VDIM), lambda h, j: (h, j, 0))),
                 (jax.ShapeDtypeStruct((N_HEADS, seq, QK), BF16),
                  jax.ShapeDtypeStruct((N_HEADS, seq, QK), BF16),
                  jax.ShapeDtypeStruct((N_HEADS, seq, VDIM), BF16)),
                 (q, k, v, o, do, lse),
                 scratch=[pltpu.VMEM((seq, QK), F32), pltpu.VMEM((2 * t, QK), F32),
                          pltpu.VMEM((2 * t, VDIM), F32)], rider=rider)


def loss_head(h, g, target, name):
    seq, d = h.shape
    tm = min(ROW_TILE, seq)

    def body(h_ref, g_ref, t_ref, l_ref, dh_ref, dg_ref):
        i = pl.program_id(0)

        @pl.when(i == 0)
        def _():
            l_ref[...] = jnp.zeros_like(l_ref)
            dg_ref[...] = jnp.zeros_like(dg_ref)
        y, xhat, rstd = _rms(h_ref[...], g_ref[...])
        diff = y - t_ref[...]
        l_ref[...] += jnp.sum(jnp.sum(diff * diff, axis=1, keepdims=True), axis=0, keepdims=True)
        dh, dg_rows = _rms_bwd(diff * (1.0 / d), xhat, rstd, g_ref[...])
        dh_ref[...] = dh
        dg_ref[...] += jnp.sum(dg_rows, axis=0, keepdims=True)

    row = pl.BlockSpec((tm, d), lambda i: (i, 0))
    vec = pl.BlockSpec((1, d), lambda i: (0, 0))
    return _call(body, name, (seq // tm,), [row, vec, row],
                 (pl.BlockSpec((1, LANES), lambda i: (0, 0)), row, vec),
                 (jax.ShapeDtypeStruct((1, LANES), F32), jax.ShapeDtypeStruct((seq, d), F32),
                  jax.ShapeDtypeStruct((1, d), F32)),
                 (h, g, target))[0]


def _pack(parts):
    rows = []
    for p in parts:
        flat = p.reshape(-1)
        n_rows = -(-flat.shape[0] // (8 * LANES)) * 8
        flat = jnp.pad(flat, (0, n_rows * LANES - flat.shape[0]))
        rows.append(flat.reshape(n_rows, LANES))
    return jnp.concatenate(rows, axis=0)


def _unpack(packed, shapes):
    lead = packed.shape[:-2]
    out, r0 = [], 0
    for shape in shapes:
        size = 1
        for s in shape:
            size *= s
        n_rows = -(-size // (8 * LANES)) * 8
        part = packed[..., r0:r0 + n_rows, :].reshape(lead + (n_rows * LANES,))
        out.append(part[..., :size].reshape(lead + tuple(shape)))
        r0 += n_rows
    return out


FWD_RIDERS = {
    "mixer_fwd0": [("ffn_w_up", 0)],
    "ffn_fwd0": [("ffn_w_down", 0), ("a_w_in", 1), ("a_w_out", 1)],
    "ffn_out0": [("ffn_w_down", 1)],
    "mixer_fwd1": [("ffn_w_up", 1)],
    "mixer_out1": [("w_dkv", 0), ("w_ukv", 0), ("b_w_dq", 0), ("b_w_uq", 0)],
    "ffn_fwd1": [("ffn_w_up", 2), ("b_w_o", 0)],
    "ffn_out1": [("ffn_w_down", 2)],
    "attn_fwd0": [("ffn_w_up", 3), ("ffn_w_down", 3), ("b_w_dq", 1), ("b_w_uq", 1), ("b_w_o", 1)],
}
BWD_RIDERS = {
    "ffn_in_bwd3": [("ffn_w_down", 3)],
    "attn_bwd1": [("ffn_w_up", 3), ("b_w_o", 1)],
    "ffn_bwd2": [("b_w_uq", 1), ("b_w_dq", 1)],
    "ffn_in_bwd2": [("ffn_w_down", 2)],
    "attn_bwd0": [("ffn_w_up", 2), ("b_w_o", 0)],
    "ffn_bwd1": [("b_w_uq", 0), ("b_w_dq", 0), ("w_ukv", 0), ("w_dkv", 0)],
    "ffn_in_bwd1": [("ffn_w_down", 1), ("ffn_w_up", 1, "pair")],
    "ffn_bwd0": [("ffn_w_up", 1, "chip"), ("a_w_in", 1), ("a_w_out", 1)],
    "ffn_in_bwd0": [("ffn_w_down", 0), ("ffn_w_up", 0, "pair")],
    "mixer_bwd0": [("ffn_w_up", 0, "chip")],
    "mixer_in_bwd0": [("a_w_out", 0), ("a_w_in", 0, "pair")],
    "adamw_a_w_out": [("a_w_in", 0, "chip")],
}


def kernel(x, a_mix_norm, a_w_in, a_conv, a_w_out, b_mix_norm, b_w_dq, b_q_norm, b_w_uq, b_w_o, kv_in_norm, w_dkv, kv_norm, w_ukv, ffn_norm, ffn_w_up, ffn_conv, ffn_w_down, final_norm, loss_target, m_a_mix_norm, m_a_w_in, m_a_conv, m_a_w_out, m_b_mix_norm, m_b_w_dq, m_b_q_norm, m_b_w_uq, m_b_w_o, m_kv_in_norm, m_w_dkv, m_kv_norm, m_w_ukv, m_ffn_norm, m_ffn_w_up, m_ffn_conv, m_ffn_w_down, m_final_norm, v_a_mix_norm, v_a_w_in, v_a_conv, v_a_w_out, v_b_mix_norm, v_b_w_dq, v_b_q_norm, v_b_w_uq, v_b_w_o, v_kv_in_norm, v_w_dkv, v_kv_norm, v_w_ukv, v_ffn_norm, v_ffn_w_up, v_ffn_conv, v_ffn_w_down, v_final_norm):
    seq, d = x.shape[1], x.shape[2]
    me = 4 * lax.axis_index("x") + 2 * lax.axis_index("y") + lax.axis_index("c")
    h0 = x.reshape(seq, d)
    target = loss_target.reshape(seq, d)
    cos, sin = _rope_tables(seq)
    rank = b_w_dq.shape[-1]
    f8 = ffn_w_up.shape[-1]
    fd = ffn_w_down.shape[1]
    dshard = a_w_out.shape[1]
    hv = N_HEADS * VDIM

    shards = {"a_w_in": a_w_in, "a_w_out": a_w_out, "b_w_dq": b_w_dq, "b_w_uq": b_w_uq,
              "b_w_o": b_w_o, "w_dkv": w_dkv[None], "w_ukv": w_ukv[None],
              "ffn_w_up": ffn_w_up, "ffn_w_down": ffn_w_down}

    def relayout(name, g):
        if name == "a_w_in":
            w = jnp.transpose(g, (1, 0, 2)).reshape(d, 3, d)
            return jnp.transpose(w, (1, 0, 2))
        if name == "a_w_out":
            return g.reshape(d, d)
        if name == "b_w_dq":
            return g.reshape(d, rank)
        if name == "b_w_uq":
            return jnp.pad(g, ((0, 0), (0, 0), (0, QK - NOPE - ROPE)))
        if name == "b_w_o":
            return g.reshape(hv, d)
        if name == "w_dkv":
            return jnp.pad(g.reshape(d, KV_RANK + ROPE), ((0, 0), (0, ROPE_PAD - ROPE)))
        if name == "ffn_w_down":
            return g.reshape(N_DEV // 2, 2 * fd, d)
        return g

    weights = {}

    def ag_rider(host):
        return [("ag", shards[n][l].astype(BF16)) for n, l in FWD_RIDERS.get(host, [])]

    def ag_done(host, outs):
        for (n, l), g in zip(FWD_RIDERS.get(host, []), outs):
            weights[n, l] = relayout(n, g)

    small_shapes = [a_mix_norm.shape, a_conv.shape, ffn_conv.shape]
    first = exchange([("ag", a_w_in[0].astype(BF16)), ("ag", a_w_out[0].astype(BF16)),
                      ("ag", _pack([a_mix_norm, a_conv, ffn_conv]))], "ag_first")
    weights["a_w_in", 0] = relayout("a_w_in", first[0])
    weights["a_w_out", 0] = relayout("a_w_out", first[1])
    s_mix, s_aconv, s_fconv = _unpack(first[2], small_shapes)
    a_gain = jnp.transpose(s_mix, (1, 0, 2)).reshape(N_A, d)
    a_cw = jnp.transpose(s_aconv, (1, 2, 0, 3)).reshape(N_A, 3, d)
    f_cw = jnp.transpose(s_fconv, (1, 0, 2, 3))

    def mixer_gain(layer):
        if layer >= DEPTH:
            return None
        return a_gain[layer][None] if layer < N_A else b_mix_norm[layer - N_A][None]

    saved = {}
    h = h0
    xn = norm_fwd(h, mixer_gain(0), "norm_first")
    kv = None
    for layer in range(DEPTH):
        saved["hm", layer], saved["xm", layer] = h, xn
        if layer < N_A:
            name = f"mixer_fwd{layer}"
            (u4, z), r = mixer_fwd(xn, weights["a_w_in", layer], a_cw[layer], name, rider=ag_rider(name))
            ag_done(name, r)
            saved["mix", layer] = (u4, z)
            name = f"mixer_out{layer}"
            (h, xn), r = proj_residual(z[None], weights["a_w_out", layer][None], h, name,
                                       g_next=ffn_norm[layer][None], rider=ag_rider(name))
            ag_done(name, r)
        else:
            j = layer - N_A
            name = f"q_fwd{j}"
            (q,), r = q_fwd(xn, weights["b_w_dq", j], b_q_norm[j][None], weights["b_w_uq", j],
                            cos, sin, name, rider=ag_rider(name))
            ag_done(name, r)
            name = f"attn_fwd{j}"
            (o, lse), r = attn_fwd(q, kv[0], kv[1], name, rider=ag_rider(name))
            ag_done(name, r)
            saved["attn", layer] = (q, o, lse)
            name = f"attn_out{j}"
            (h, xn), r = proj_residual(o[None], weights["b_w_o", j][None], h, name,
                                       g_next=ffn_norm[layer][None], rider=ag_rider(name))
            ag_done(name, r)
        saved["hf", layer], saved["xf", layer] = h, xn
        name = f"ffn_fwd{layer}"
        (up2, cv2, act), r = ffn_fwd(xn, weights["ffn_w_up", layer], f_cw[layer], name, rider=ag_rider(name))
        ag_done(name, r)
        saved["ffn", layer] = (up2, cv2, act)
        name = f"ffn_out{layer}"
        (h, xn), r = proj_residual(act, weights["ffn_w_down", layer], h, name,
                                   g_next=mixer_gain(layer + 1), rider=ag_rider(name))
        ag_done(name, r)
        if layer == N_A - 1:
            (k_all, v_all, c_kv), r = kv_fwd(h, kv_in_norm[None], weights["w_dkv", 0], kv_norm[None],
                                             weights["w_ukv", 0], cos, sin, "kv_fwd",
                                             rider=ag_rider("kv_fwd"))
            ag_done("kv_fwd", r)
            kv = (k_all, v_all, c_kv)

    sq_err, dh, d_final = loss_head(h, final_norm[None], target, "loss_head")
    loss = lax.psum(sq_err[0, 0] * (0.5 / d), ("x", "y", "c"))

    grads = {}
    parts = {}

    pair_sums = {}

    def by_chip(g):
        return g.reshape((N_DEV // 2, 2) + g.shape[1:])

    def rs_rider(host):
        tasks = []
        for key in BWD_RIDERS.get(host, []):
            if len(key) == 2:
                tasks.append(("rs", grads[key]))
            elif key[2] == "pair":
                tasks.append(("rs_pair", by_chip(grads[key[:2]])))
            else:
                tasks.append(("rs_chip", pair_sums[key[:2]]))
        return tasks

    def rs_done(host, outs):
        for key, p in zip(BWD_RIDERS.get(host, []), outs):
            if len(key) == 3 and key[2] == "pair":
                pair_sums[key[:2]] = pair_sum(by_chip(grads[key[:2]]), p, f"pair_sum_{key[0]}{key[1]}")
            else:
                parts[key[:2]] = p

    d_ffn_norm = [None] * DEPTH
    d_fconv = [None] * DEPTH
    d_a_gain = [None] * N_A
    d_aconv = [None] * N_A
    d_b_gain = [None] * N_B
    d_q_gain = [None] * N_B
    dks, dvs = [], []
    for layer in reversed(range(DEPTH)):
        if layer == N_A - 1:
            hk = saved["hm", layer + 1]
            (dckv, dwukv, dwdkv, d_kv_gain), r = kv_bwd(
                dks, dvs, kv[2], hk, kv_in_norm[None], kv_norm[None], weights["w_ukv", 0],
                cos, sin, "kv_bwd", rider=rs_rider("kv_bwd"))
            rs_done("kv_bwd", r)
            grads["w_ukv", 0] = dwukv
            grads["w_dkv", 0] = dwdkv[:, :KV_RANK + ROPE].reshape(N_DEV, dshard, KV_RANK + ROPE)
            (dh, d_kvin_gain), r = proj_t_rms_bwd(dckv, weights["w_dkv", 0][None], hk, kv_in_norm[None],
                                                  dh, "kv_in_bwd", rider=rs_rider("kv_in_bwd"))
            rs_done("kv_in_bwd", r)
        up2, cv2, act = saved["ffn", layer]
        name = f"ffn_bwd{layer}"
        (dup2, dwup, dwdown, dcw), r = ffn_bwd(dh, weights["ffn_w_down", layer], up2, cv2, act,
                                               saved["xf", layer], f_cw[layer], name, rider=rs_rider(name))
        rs_done(name, r)
        grads["ffn_w_up", layer] = dwup.reshape(N_DEV, d, f8)
        grads["ffn_w_down", layer] = dwdown.reshape(N_DEV, fd, d)
        d_fconv[layer] = dcw.reshape(N_DEV, 3, f8)
        name = f"ffn_in_bwd{layer}"
        (dh, d_ffn_norm[layer]), r = proj_t_rms_bwd(dup2.reshape(N_DEV, seq, f8), weights["ffn_w_up", layer],
                                                    saved["hf", layer], ffn_norm[layer][None], dh, name,
                                                    rider=rs_rider(name))
        rs_done(name, r)
        hm, xm = saved["hm", layer], saved["xm", layer]
        if layer < N_A:
            u4, z = saved["mix", layer]
            name = f"mixer_bwd{layer}"
            (du3, dwin3, dwout, dcw), r = mixer_bwd(dh, weights["a_w_out", layer], u4, z, xm, a_cw[layer],
                                                    name, rider=rs_rider(name))
            rs_done(name, r)
            dwin = jnp.transpose(dwin3, (1, 0, 2)).reshape(d, N_DEV, 3 * d // N_DEV)
            grads["a_w_in", layer] = jnp.transpose(dwin, (1, 0, 2))
            grads["a_w_out", layer] = dwout.reshape(N_DEV, dshard, d)
            d_aconv[layer] = dcw
            name = f"mixer_in_bwd{layer}"
            (dh, d_a_gain[layer]), r = proj_t_rms_bwd(du3, weights["a_w_in", layer], hm, a_gain[layer][None],
                                                      dh, name, rider=rs_rider(name))
            rs_done(name, r)
        else:
            j = layer - N_A
            q, o, lse = saved["attn", layer]
            name = f"attn_out_bwd{j}"
            (do, dwo), r = o_bwd(dh, o, weights["b_w_o", j], name, rider=rs_rider(name))
            rs_done(name, r)
            grads["b_w_o", j] = dwo.reshape(N_DEV, dshard, d)
            name = f"attn_bwd{j}"
            (dq, dk, dv), r = attn_bwd(q, kv[0], kv[1], o, do, lse, name, rider=rs_rider(name))
            rs_done(name, r)
            dks.append(dk)
            dvs.append(dv)
            name = f"q_bwd{j}"
            (dqc, dwuq, dwdq, d_q_gain[j]), r = q_bwd(dq, xm, weights["b_w_dq", j], b_q_norm[j][None],
                                                      weights["b_w_uq", j], cos, sin, name, rider=rs_rider(name))
            rs_done(name, r)
            grads["b_w_uq", j] = dwuq[:, :, :NOPE + ROPE]
            grads["b_w_dq", j] = dwdq.reshape(N_DEV, dshard, rank)
            name = f"q_in_bwd{j}"
            (dh, d_b_gain[j]), r = proj_t_rms_bwd(dqc, weights["b_w_dq", j][None], hm, b_mix_norm[j][None],
                                                  dh, name, rider=rs_rider(name))
            rs_done(name, r)
    grad_x = dh.reshape(x.shape)

    full_small = [
        jnp.concatenate(d_a_gain, axis=0),
        jnp.stack(d_aconv),
        jnp.concatenate(d_b_gain, axis=0),
        jnp.concatenate(d_q_gain, axis=0),
        d_kvin_gain[0],
        d_kv_gain[0],
        jnp.concatenate(d_ffn_norm, axis=0),
        jnp.stack(d_fconv),
        d_final[0],
    ]
    full_shapes = [t.shape for t in full_small]
    small_pack = _pack(full_small)

    res = {}

    def update(name, n_layers, w, m, v, extra=(), transposed=False):
        view = (lambda t: jnp.transpose(t, (0, 2, 1))) if transposed else (lambda t: t)
        call = sum_adamw_transposed if transposed else sum_adamw
        shard = w.shape if w.ndim == 3 else (1,) + w.shape
        host = f"adamw_{name}"
        outs, r = call([parts[name, l] for l in range(n_layers)], view(w.reshape(shard)),
                       view(m.reshape(shard)), view(v.reshape(shard)), host,
                       rider=rs_rider(host) + list(extra))
        rs_done(host, r)
        res[name] = [view(t).reshape(w.shape) for t in outs]
        return r[len(BWD_RIDERS.get(host, [])):]

    (g_parts,) = update("a_w_out", N_A, a_w_out, m_a_w_out, v_a_w_out, extra=[("ag", small_pack)])
    update("ffn_w_down", DEPTH, ffn_w_down, m_ffn_w_down, v_ffn_w_down)
    update("ffn_w_up", DEPTH, ffn_w_up, m_ffn_w_up, v_ffn_w_up, transposed=True)
    update("b_w_dq", N_B, b_w_dq, m_b_w_dq, v_b_w_dq)
    update("b_w_uq", N_B, b_w_uq, m_b_w_uq, v_b_w_uq)
    update("b_w_o", N_B, b_w_o, m_b_w_o, v_b_w_o)
    update("w_dkv", 1, w_dkv, m_w_dkv, v_w_dkv)
    update("w_ukv", 1, w_ukv, m_w_ukv, v_w_ukv)
    update("a_w_in", N_A, a_w_in, m_a_w_in, v_a_w_in)

    summed = sum_slots(g_parts, "sum_small_grads")
    (s_a_gain, s_aconv_g, s_b_gain, s_q_gain, s_kvin, s_kvn, s_ffn_gain, s_fconv_g,
     s_final) = _unpack(summed, full_shapes)
    dsl = d // N_DEV
    small = [
        ("a_mix_norm", lax.dynamic_slice_in_dim(s_a_gain, me * dsl, dsl, axis=1), a_mix_norm, m_a_mix_norm, v_a_mix_norm),
        ("a_conv", lax.dynamic_slice_in_dim(s_aconv_g, me * dsl, dsl, axis=2), a_conv, m_a_conv, v_a_conv),
        ("b_mix_norm", s_b_gain, b_mix_norm, m_b_mix_norm, v_b_mix_norm),
        ("b_q_norm", s_q_gain, b_q_norm, m_b_q_norm, v_b_q_norm),
        ("kv_in_norm", s_kvin, kv_in_norm, m_kv_in_norm, v_kv_in_norm),
        ("kv_norm", s_kvn, kv_norm, m_kv_norm, v_kv_norm),
        ("ffn_norm", s_ffn_gain, ffn_norm, m_ffn_norm, v_ffn_norm),
        ("ffn_conv", lax.dynamic_index_in_dim(s_fconv_g, me, axis=1, keepdims=False), ffn_conv, m_ffn_conv, v_ffn_conv),
        ("final_norm", s_final, final_norm, m_final_norm, v_final_norm),
    ]
    shapes = [t[2].shape for t in small]
    packed = [_pack([t[k] for t in small])[None] for k in (1, 2, 3, 4)]
    outs, _ = sum_adamw([packed[0]], packed[1], packed[2], packed[3], "adamw_small")
    unpacked = [_unpack(t[0], shapes) for t in outs]
    for idx, t in enumerate(small):
        res[t[0]] = [unpacked[k][idx] for k in range(4)]

    order = ["a_mix_norm", "a_w_in", "a_conv", "a_w_out", "b_mix_norm", "b_w_dq", "b_q_norm",
             "b_w_uq", "b_w_o", "kv_in_norm", "w_dkv", "kv_norm", "w_ukv", "ffn_norm",
             "ffn_w_up", "ffn_conv", "ffn_w_down", "final_norm"]
    return (loss, grad_x, *[res[n][0] for n in order], *[res[n][1] for n in order],
            *[res[n][2] for n in order], *[res[n][3] for n in order])
```

```python
import functools

import jax
import jax.numpy as jnp
from jax import lax
from jax.experimental import pallas as pl
from jax.experimental.pallas import tpu as pltpu

F32 = jnp.float32
BF16 = jnp.bfloat16

N_DEV = 8
N_HEADS = 8
NOPE = 128
ROPE = 64
ROPE_PAD = 128
QK = NOPE + ROPE_PAD
VDIM = 128
KV_RANK = 256
ROPE_THETA = 10000.0
RMS_EPS = 1e-6
ATTN_SCALE = (NOPE + ROPE) ** -0.5
N_A = 2
N_B = 2
DEPTH = 4

ADAM_LR = 0.001
ADAM_B1 = 0.9
ADAM_B2 = 0.999
ADAM_EPS = 1e-08
ADAM_WD = 0.01
ADAM_STEP = 10

V7X_VMEM_LIMIT = 56 * 1024 * 1024
BF16_SUBLANES = 16
ROW_TILE = 512
ROW_TILE_LARGE = 1024
ADAM_ROWS = 256
ATTN_TILE = 512
MIXER_CHUNK = 512
LANES = 128
NEG_BIG = -1e30
COPIES_PER_TASK = 7

MESH_ID = pl.DeviceIdType.MESH
ANY = pl.BlockSpec(memory_space=pl.ANY)


def _nt(a, b):
    return lax.dot_general(a, b, (((1,), (1,)), ((), ())), preferred_element_type=F32)


def _tn(a, b):
    return lax.dot_general(a, b, (((0,), (0,)), ((), ())), preferred_element_type=F32)


def _nn(a, b):
    return jnp.dot(a, b, preferred_element_type=F32)


def _rms(h, g):
    rstd = lax.rsqrt(jnp.mean(h * h, axis=-1, keepdims=True) + RMS_EPS)
    xhat = h * rstd
    return xhat * g, xhat, rstd


def _rms_bwd(dxn, xhat, rstd, g):
    dxhat = dxn * g
    dh = rstd * (dxhat - xhat * jnp.mean(dxhat * xhat, axis=-1, keepdims=True))
    return dh, dxn * xhat


def _shift_down(x, k, halo_rows):
    r = pltpu.roll(x, k, 0)
    row = lax.broadcasted_iota(jnp.int32, x.shape, 0)
    for t in range(k):
        r = jnp.where(row == t, halo_rows[t], r)
    return r


def _shift_up(x, k, halo_rows):
    n = x.shape[0]
    r = pltpu.roll(x, n - k, 0)
    row = lax.broadcasted_iota(jnp.int32, x.shape, 0)
    for t in range(k):
        r = jnp.where(row == n - k + t, halo_rows[t], r)
    return r


def _conv_taps(w_ref):
    return w_ref[0:1, :], w_ref[1:2, :], w_ref[2:3, :]


def _rope_swap(x):
    lane = lax.broadcasted_iota(jnp.int32, x.shape, 1)
    return jnp.where(lane < ROPE // 2, pltpu.roll(x, ROPE_PAD - ROPE // 2, 1),
                     pltpu.roll(x, ROPE // 2, 1))


def _rope_fwd(x, cos, sin):
    return x * cos + _rope_swap(x) * sin


def _rope_bwd(dy, cos, sin):
    return dy * cos - _rope_swap(dy) * sin


def _rope_tables(seq):
    inv = 1.0 / (ROPE_THETA ** (jnp.arange(0, ROPE, 2, dtype=F32) / ROPE))
    ang = jnp.arange(seq, dtype=F32)[:, None] * inv[None, :]
    cos, sin = jnp.cos(ang), jnp.sin(ang)
    zero = jnp.zeros((seq, ROPE_PAD - ROPE), F32)
    return (jnp.concatenate([cos, cos, zero], axis=1),
            jnp.concatenate([-sin, sin, zero], axis=1))


def _row_tile(rows, cap, mult=8):
    best = None
    for t in range(mult, min(rows, cap) + 1, mult):
        if rows % t == 0:
            best = t
    return rows if best is None else best


class _AllGatherTask:
    def __init__(self, t, x_ref, out_ref, send_sems, recv_sems, local_sems):
        self.t, self.x_ref, self.out_ref = t, x_ref, out_ref
        self.send_sems, self.recv_sems, self.local_sems = send_sems, recv_sems, local_sems
        mx, my, mc = lax.axis_index("x"), lax.axis_index("y"), lax.axis_index("c")
        self.mc = mc
        self.me, self.sibling = (mx, my, mc), (mx, my, 1 - mc)
        self.chips = [(1 - mx, my), (mx, 1 - my), (1 - mx, 1 - my)]

    def _slot(self, px, py, pc):
        return self.out_ref.at[4 * px + 2 * py + pc]

    def _copy(self, k, block, to, src=None):
        s = COPIES_PER_TASK * self.t + k
        return pltpu.make_async_remote_copy(
            src_ref=self._slot(*block) if src is None else src, dst_ref=self._slot(*block),
            send_sem=self.send_sems.at[s], recv_sem=self.recv_sems.at[s],
            device_id=to, device_id_type=MESH_ID)

    def _mine(self):
        return pltpu.make_async_copy(self.x_ref, self._slot(*self.me), self.local_sems.at[self.t])

    def _first(self):
        out = [self._copy(0, self.me, self.sibling, src=self.x_ref)]
        out += [self._copy(1 + j, self.me, (*chip, self.mc), src=self.x_ref)
                for j, chip in enumerate(self.chips)]
        return out

    def _passed(self):
        return [self._copy(4 + j, (*chip, self.mc), self.sibling) for j, chip in enumerate(self.chips)]

    def start(self):
        self._mine().start()
        for cp in self._first():
            cp.start()

    def forward(self):
        passed = self._passed()
        for j, chip in enumerate(self.chips):
            self._copy(1 + j, (*chip, self.mc), self.me).wait_recv()
            passed[j].start()

    def finish(self):
        self._copy(0, self.sibling, self.me).wait_recv()
        for j, chip in enumerate(self.chips):
            self._copy(4 + j, (*chip, 1 - self.mc), self.me).wait_recv()
        for cp in self._first() + self._passed():
            cp.wait_send()
        self._mine().wait()


class _ReduceScatterTask:
    def __init__(self, t, g_ref, out_ref, send_sems, recv_sems, local_sems):
        self.t, self.g_ref, self.out_ref = t, g_ref, out_ref
        self.send_sems, self.recv_sems, self.local_sems = send_sems, recv_sems, local_sems
        mx, my, mc = lax.axis_index("x"), lax.axis_index("y"), lax.axis_index("c")
        self.me = 4 * mx + 2 * my + mc
        self.peers = []
        for k in range(1, N_DEV):
            px, py, pc = mx ^ ((k >> 2) & 1), my ^ ((k >> 1) & 1), mc ^ (k & 1)
            self.peers.append(((px, py, pc), 4 * px + 2 * py + pc))

    def _mine(self):
        return pltpu.make_async_copy(self.g_ref.at[self.me], self.out_ref.at[self.me],
                                     self.local_sems.at[self.t])

    def _copy(self, k, src_slot, dst_slot):
        s = COPIES_PER_TASK * self.t + k
        return pltpu.make_async_remote_copy(
            src_ref=self.g_ref.at[src_slot], dst_ref=self.out_ref.at[dst_slot],
            send_sem=self.send_sems.at[s], recv_sem=self.recv_sems.at[s],
            device_id=self.peers[k][0], device_id_type=MESH_ID)

    def start(self):
        self._mine().start()
        for k, (_, peer) in enumerate(self.peers):
            self._copy(k, peer, self.me).start()

    def forward(self):
        pass

    def finish(self):
        for k, (_, peer) in enumerate(self.peers):
            self._copy(k, self.me, peer).wait_recv()
        for k, (_, peer) in enumerate(self.peers):
            self._copy(k, peer, self.me).wait_send()
        self._mine().wait()


class _PairExchangeTask:
    def __init__(self, t, g_ref, out_ref, send_sems, recv_sems, local_sems):
        mx, my, mc = lax.axis_index("x"), lax.axis_index("y"), lax.axis_index("c")
        s = COPIES_PER_TASK * t
        self.copy = pltpu.make_async_remote_copy(
            src_ref=g_ref.at[:, 1 - mc], dst_ref=out_ref,
            send_sem=send_sems.at[s], recv_sem=recv_sems.at[s],
            device_id=(mx, my, 1 - mc), device_id_type=MESH_ID)

    def start(self):
        self.copy.start()

    def forward(self):
        pass

    def finish(self):
        self.copy.wait()


class _ChipScatterTask:
    def __init__(self, t, s_ref, out_ref, send_sems, recv_sems, local_sems):
        self.t, self.s_ref, self.out_ref = t, s_ref, out_ref
        self.send_sems, self.recv_sems, self.local_sems = send_sems, recv_sems, local_sems
        mx, my, mc = lax.axis_index("x"), lax.axis_index("y"), lax.axis_index("c")
        self.chip = 2 * mx + my
        self.peers = []
        for k in range(1, N_DEV // 2):
            px, py = mx ^ ((k >> 1) & 1), my ^ (k & 1)
            self.peers.append(((px, py, mc), 2 * px + py))

    def _mine(self):
        return pltpu.make_async_copy(self.s_ref.at[self.chip], self.out_ref.at[self.chip],
                                     self.local_sems.at[self.t])

    def _copy(self, k, src_slot, dst_slot):
        s = COPIES_PER_TASK * self.t + k
        return pltpu.make_async_remote_copy(
            src_ref=self.s_ref.at[src_slot], dst_ref=self.out_ref.at[dst_slot],
            send_sem=self.send_sems.at[s], recv_sem=self.recv_sems.at[s],
            device_id=self.peers[k][0], device_id_type=MESH_ID)

    def start(self):
        self._mine().start()
        for k, (_, peer) in enumerate(self.peers):
            self._copy(k, peer, self.chip).start()

    def forward(self):
        pass

    def finish(self):
        for k, (_, peer) in enumerate(self.peers):
            self._copy(k, self.chip, peer).wait_recv()
        for k, (_, peer) in enumerate(self.peers):
            self._copy(k, peer, self.chip).wait_send()
        self._mine().wait()


_TASKS = {"ag": _AllGatherTask, "rs": _ReduceScatterTask, "rs_pair": _PairExchangeTask,
          "rs_chip": _ChipScatterTask}


def _task_shape(kind, arr):
    shape = {"ag": (N_DEV,) + arr.shape, "rs": arr.shape, "rs_chip": arr.shape,
             "rs_pair": arr.shape[:1] + arr.shape[2:]}[kind]
    return jax.ShapeDtypeStruct(shape, arr.dtype)


def _sem_shapes(n_tasks):
    return [pltpu.SemaphoreType.DMA((COPIES_PER_TASK * n_tasks,)),
            pltpu.SemaphoreType.DMA((COPIES_PER_TASK * n_tasks,)),
            pltpu.SemaphoreType.DMA((n_tasks,))]


def _make_tasks(rider, in_refs, out_refs, sems):
    return [_TASKS[kind](t, in_refs[t], out_refs[t], *sems) for t, (kind, _) in enumerate(rider)]


def exchange(rider, name):
    n = len(rider)

    def body(*refs):
        tasks = _make_tasks(rider, refs[:n], refs[n:2 * n], refs[2 * n:])
        for task in tasks:
            task.start()
        for task in tasks:
            task.forward()
        for task in tasks:
            task.finish()

    return list(pl.pallas_call(
        body, name=name, out_shape=tuple(_task_shape(k, a) for k, a in rider),
        in_specs=[ANY] * n, out_specs=(ANY,) * n, scratch_shapes=_sem_shapes(n),
    )(*[a for _, a in rider]))


def _call(body, name, grid, in_specs, out_specs, out_shape, args, scratch=(), rider=()):
    in_specs, out_specs, out_shape = list(in_specs), tuple(out_specs), tuple(out_shape)
    n_in, n_out, n_scr, n_r = len(in_specs), len(out_specs), len(scratch), len(rider)
    if n_r:
        def kern(*refs):
            ins, r_in = refs[:n_in], refs[n_in:n_in + n_r]
            o0 = n_in + n_r
            outs, r_out = refs[o0:o0 + n_out], refs[o0 + n_out:o0 + n_out + n_r]
            s0 = o0 + n_out + n_r
            scr, sems = refs[s0:s0 + n_scr], refs[s0 + n_scr:]
            step = 0
            for a, n in enumerate(grid):
                step = step * n + pl.program_id(a)
            n_steps = 1
            for n in grid:
                n_steps *= n

            @pl.when(step == 0)
            def _():
                for task in _make_tasks(rider, r_in, r_out, sems):
                    task.start()
            body(*ins, *outs, *scr)

            @pl.when(step == n_steps - 1)
            def _():
                tasks = _make_tasks(rider, r_in, r_out, sems)
                for task in tasks:
                    task.forward()
                for task in tasks:
                    task.finish()
    else:
        kern = body
    res = pl.pallas_call(
        kern, name=name, grid=grid,
        in_specs=in_specs + [ANY] * n_r, out_specs=out_specs + (ANY,) * n_r,
        out_shape=out_shape + tuple(_task_shape(k, a) for k, a in rider),
        scratch_shapes=list(scratch) + (_sem_shapes(n_r) if n_r else []),
        compiler_params=pltpu.CompilerParams(dimension_semantics=("arbitrary",) * len(grid),
                                             vmem_limit_bytes=V7X_VMEM_LIMIT),
    )(*args, *[a for _, a in rider])
    return list(res[:n_out]), list(res[n_out:])


def _adamw(g, w, m, v):
    m = ADAM_B1 * m + (1.0 - ADAM_B1) * g
    v = ADAM_B2 * v + (1.0 - ADAM_B2) * (g * g)
    m_hat = m / (1.0 - ADAM_B1 ** ADAM_STEP)
    v_hat = v / (1.0 - ADAM_B2 ** ADAM_STEP)
    delta = -ADAM_LR * (m_hat / (jnp.sqrt(v_hat) + ADAM_EPS) + ADAM_WD * w)
    return delta, m, v


def sum_adamw(parts, w, m, v, name, rider=()):
    n_l, rows, cols = w.shape
    mult = BF16_SUBLANES if parts[0].dtype == BF16 else 8
    tr = _row_tile(rows, ADAM_ROWS, mult)
    n_i = rows // tr

    def body(*refs):
        part_refs = refs[:n_l]
        w_ref, m_ref, v_ref, g_out, d_out, m_out, v_out = refs[n_l:]
        layer = pl.program_id(0)
        for k in range(n_l):
            @pl.when(layer == k)
            def _(k=k):
                g = part_refs[k][0].astype(F32)
                for s in range(1, parts[k].shape[0]):
                    g = g + part_refs[k][s].astype(F32)
                delta, m_new, v_new = _adamw(g, w_ref[...], m_ref[...], v_ref[...])
                g_out[...] = g
                d_out[...] = delta
                m_out[...] = m_new
                v_out[...] = v_new

    part_specs = [pl.BlockSpec((parts[k].shape[0], tr, cols), functools.partial(
        lambda l, i, k: (0, jnp.where(l == k, i, 0), 0), k=k)) for k in range(n_l)]
    wspec = pl.BlockSpec((None, tr, cols), lambda l, i: (l, i, 0))
    shape = jax.ShapeDtypeStruct(w.shape, F32)
    return _call(body, name, (n_l, n_i), part_specs + [wspec] * 3, (wspec,) * 4, (shape,) * 4,
                 (*parts, w, m, v), rider=rider)


def sum_adamw_transposed(parts, w_t, m_t, v_t, name, rider=()):
    n_l, cols, rows = w_t.shape
    tr = LANES
    n_i = rows // tr
    starts = list(range(0, cols - LANES + 1, LANES))
    if starts[-1] + LANES < cols:
        starts.append(cols - LANES)

    def body(*refs):
        part_refs = refs[:n_l]
        w_ref, m_ref, v_ref, g_out, d_out, m_out, v_out = refs[n_l:]
        layer = pl.program_id(0)
        for k in range(n_l):
            @pl.when(layer == k)
            def _(k=k):
                for c0 in starts:
                    piece = pl.ds(c0, LANES)
                    g = part_refs[k][0, :, piece].astype(F32)
                    for s in range(1, parts[k].shape[0]):
                        g = g + part_refs[k][s, :, piece].astype(F32)
                    g = g.T
                    delta, m_new, v_new = _adamw(g, w_ref[piece, :], m_ref[piece, :], v_ref[piece, :])
                    g_out[piece, :] = g
                    d_out[piece, :] = delta
                    m_out[piece, :] = m_new
                    v_out[piece, :] = v_new

    part_specs = [pl.BlockSpec((parts[k].shape[0], tr, cols), functools.partial(
        lambda l, i, k: (0, jnp.where(l == k, i, 0), 0), k=k)) for k in range(n_l)]
    wspec = pl.BlockSpec((None, cols, tr), lambda l, i: (l, 0, i))
    shape = jax.ShapeDtypeStruct(w_t.shape, F32)
    return _call(body, name, (n_l, n_i), part_specs + [wspec] * 3, (wspec,) * 4, (shape,) * 4,
                 (*parts, w_t, m_t, v_t), rider=rider)


def pair_sum(g4, other, name):
    n_chip, _, rows, cols = g4.shape
    tr = _row_tile(rows, 256, BF16_SUBLANES)

    def body(g_ref, o_ref, s_ref):
        mine = g_ref[lax.axis_index("c")]
        s_ref[...] = (mine.astype(F32) + o_ref[...].astype(F32)).astype(BF16)

    blk = pl.BlockSpec((None, tr, cols), lambda k, i: (k, i, 0))
    return _call(body, name, (n_chip, rows // tr),
                 [pl.BlockSpec((None, 2, tr, cols), lambda k, i: (k, 0, i, 0)), blk], [blk],
                 [jax.ShapeDtypeStruct((n_chip, rows, cols), g4.dtype)], (g4, other))[0][0]


def sum_slots(parts, name):
    n, rows, cols = parts.shape

    def body(p_ref, o_ref):
        acc = p_ref[0]
        for s in range(1, n):
            acc = acc + p_ref[s]
        o_ref[...] = acc

    return pl.pallas_call(
        body, name=name, out_shape=jax.ShapeDtypeStruct((rows, cols), F32),
        in_specs=[pl.BlockSpec(memory_space=pltpu.VMEM)],
        out_specs=pl.BlockSpec(memory_space=pltpu.VMEM),
    )(parts)


def norm_fwd(h, g, name):
    seq, d = h.shape
    tm = min(ROW_TILE, seq)

    def body(h_ref, g_ref, o_ref):
        o_ref[...] = _rms(h_ref[...], g_ref[...])[0].astype(BF16)

    return _call(body, name, (seq // tm,),
                 [pl.BlockSpec((tm, d), lambda i: (i, 0)), pl.BlockSpec((1, d), lambda i: (0, 0))],
                 [pl.BlockSpec((tm, d), lambda i: (i, 0))],
                 [jax.ShapeDtypeStruct((seq, d), BF16)], (h, g))[0][0]


def proj_residual(a, w, res, name, g_next=None, rider=()):
    nb, seq, kb = a.shape
    d = w.shape[-1]
    tm = min(ROW_TILE, seq)
    with_norm = g_next is not None

    def body(a_ref, w_ref, r_ref, *rest):
        acc = r_ref[...]
        for b in range(nb):
            acc = acc + _nn(a_ref[b], w_ref[b])
        if with_norm:
            g_ref, o_ref, xn_ref = rest
            xn_ref[...] = _rms(acc, g_ref[...])[0].astype(BF16)
        else:
            (o_ref,) = rest
        o_ref[...] = acc

    row = pl.BlockSpec((tm, d), lambda i: (i, 0))
    in_specs = [pl.BlockSpec((nb, tm, kb), lambda i: (0, i, 0)),
                pl.BlockSpec((nb, kb, d), lambda i: (0, 0, 0)), row]
    args = [a, w, res]
    out_specs, out_shape = [row], [jax.ShapeDtypeStruct((seq, d), F32)]
    if with_norm:
        in_specs.append(pl.BlockSpec((1, d), lambda i: (0, 0)))
        args.append(g_next)
        out_specs.append(row)
        out_shape.append(jax.ShapeDtypeStruct((seq, d), BF16))
    outs, r_outs = _call(body, name, (seq // tm,), in_specs, out_specs, out_shape, args, rider=rider)
    return (outs[0], outs[1] if with_norm else None), r_outs


def proj_t_rms_bwd(du, w, h, g, dres, name, rider=()):
    nb, seq, wd = du.shape
    k = w.shape[1]
    tm = min(ROW_TILE, seq)

    def body(du_ref, w_ref, h_ref, g_ref, dr_ref, dh_ref, dg_ref):
        i = pl.program_id(0)
        dxn = _nt(du_ref[0], w_ref[0])
        for b in range(1, nb):
            dxn = dxn + _nt(du_ref[b], w_ref[b])
        _, xhat, rstd = _rms(h_ref[...], g_ref[...])
        dh, dg_rows = _rms_bwd(dxn, xhat, rstd, g_ref[...])
        dh_ref[...] = dr_ref[...] + dh

        @pl.when(i == 0)
        def _():
            dg_ref[...] = jnp.zeros_like(dg_ref)
        dg_ref[...] += jnp.sum(dg_rows, axis=0, keepdims=True)

    row = pl.BlockSpec((tm, k), lambda i: (i, 0))
    vec = pl.BlockSpec((1, k), lambda i: (0, 0))
    return _call(body, name, (seq // tm,),
                 [pl.BlockSpec((nb, tm, wd), lambda i: (0, i, 0)),
                  pl.BlockSpec((nb, k, wd), lambda i: (0, 0, 0), pipeline_mode=pl.Buffered(1)),
                  row, vec, row],
                 (row, vec),
                 (jax.ShapeDtypeStruct((seq, k), F32), jax.ShapeDtypeStruct((1, k), F32)),
                 (du, w, h, g, dres), rider=rider)


def mixer_fwd(xn, win3, cw, name, rider=()):
    seq, d = xn.shape
    tm = min(ROW_TILE_LARGE, seq)
    cc = min(MIXER_CHUNK, d)
    n_c, n_i = d // cc, seq // tm

    def body(x_ref, w_ref, cw_ref, u_ref, z_ref, carry):
        i = pl.program_id(1)

        @pl.when(i == 0)
        def _():
            carry[...] = jnp.zeros_like(carry)
        xb = x_ref[...]
        b = _nn(xb, w_ref[0])
        c = _nn(xb, w_ref[1])
        hh = _nn(xb, w_ref[2])
        p = c * hh
        w0, w1, w2 = _conv_taps(cw_ref)
        p1 = _shift_down(p, 1, [carry[7:8, :]])
        p2 = _shift_down(p, 2, [carry[6:7, :], carry[7:8, :]])
        q = w0 * p2 + w1 * p1 + w2 * p
        carry[...] = p[tm - 8:tm, :]
        u_ref[0] = b.astype(BF16)
        u_ref[1] = c.astype(BF16)
        u_ref[2] = hh.astype(BF16)
        u_ref[3] = q.astype(BF16)
        z_ref[...] = (b * q).astype(BF16)

    return _call(body, name, (n_c, n_i),
                 [pl.BlockSpec((tm, d), lambda c, i: (i, 0)),
                  pl.BlockSpec((3, d, cc), lambda c, i: (0, 0, c)),
                  pl.BlockSpec((3, cc), lambda c, i: (0, c))],
                 (pl.BlockSpec((4, tm, cc), lambda c, i: (0, i, c)),
                  pl.BlockSpec((tm, cc), lambda c, i: (i, c))),
                 (jax.ShapeDtypeStruct((4, seq, d), BF16), jax.ShapeDtypeStruct((seq, d), BF16)),
                 (xn, win3, cw), scratch=[pltpu.VMEM((8, cc), F32)], rider=rider)


def mixer_bwd(dh, wout, u4, z, xn, cw, name, rider=()):
    seq, d = xn.shape
    tm = min(ROW_TILE, seq)
    cc = min(MIXER_CHUNK, d)
    n_c, n_i = d // cc, seq // tm

    def body(dh_ref, wout_ref, u_ref, z_ref, x_ref, cw_ref,
             du_ref, dwin_ref, dwout_ref, dcw_ref, acc_in, acc_out, acc_cw, carry):
        i = pl.program_id(1)

        @pl.when(i == 0)
        def _():
            acc_in[...] = jnp.zeros_like(acc_in)
            acc_out[...] = jnp.zeros_like(acc_out)
            acc_cw[...] = jnp.zeros_like(acc_cw)
            carry[...] = jnp.zeros_like(carry)
        dhb = dh_ref[...].astype(BF16)
        dz = _nt(dhb, wout_ref[...])
        acc_out[...] += _tn(z_ref[...], dhb)
        b = u_ref[0].astype(F32)
        c = u_ref[1].astype(F32)
        hh = u_ref[2].astype(F32)
        q = u_ref[3].astype(F32)
        p = c * hh
        db = dz * q
        dq = dz * b
        w0, w1, w2 = _conv_taps(cw_ref)
        dq1 = _shift_up(dq, 1, [carry[0:1, :]])
        dq2 = _shift_up(dq, 2, [carry[0:1, :], carry[1:2, :]])
        dp = w2 * dq + w1 * dq1 + w0 * dq2
        carry[...] = dq[0:8, :]
        acc_cw[0:1, :] += jnp.sum(dq2 * p, axis=0, keepdims=True)
        acc_cw[1:2, :] += jnp.sum(dq1 * p, axis=0, keepdims=True)
        acc_cw[2:3, :] += jnp.sum(dq * p, axis=0, keepdims=True)
        dbb = db.astype(BF16)
        dcb = (dp * hh).astype(BF16)
        dhhb = (dp * c).astype(BF16)
        du_ref[0] = dbb
        du_ref[1] = dcb
        du_ref[2] = dhhb
        xb = x_ref[...]
        acc_in[0] += _tn(xb, dbb)
        acc_in[1] += _tn(xb, dcb)
        acc_in[2] += _tn(xb, dhhb)

        @pl.when(i == n_i - 1)
        def _():
            dwin_ref[...] = acc_in[...].astype(BF16)
            dwout_ref[...] = acc_out[...].astype(BF16)
            dcw_ref[...] = acc_cw[0:3, :]

    rev = lambda c, i: (n_i - 1 - i, 0)
    return _call(body, name, (n_c, n_i),
                 [pl.BlockSpec((tm, d), rev),
                  pl.BlockSpec((cc, d), lambda c, i: (c, 0)),
                  pl.BlockSpec((4, tm, cc), lambda c, i: (0, n_i - 1 - i, c)),
                  pl.BlockSpec((tm, cc), lambda c, i: (n_i - 1 - i, c)),
                  pl.BlockSpec((tm, d), rev),
                  pl.BlockSpec((3, cc), lambda c, i: (0, c))],
                 (pl.BlockSpec((3, tm, cc), lambda c, i: (0, n_i - 1 - i, c)),
                  pl.BlockSpec((3, d, cc), lambda c, i: (0, 0, c)),
                  pl.BlockSpec((cc, d), lambda c, i: (c, 0)),
                  pl.BlockSpec((3, cc), lambda c, i: (0, c))),
                 (jax.ShapeDtypeStruct((3, seq, d), BF16), jax.ShapeDtypeStruct((3, d, d), BF16),
                  jax.ShapeDtypeStruct((d, d), BF16), jax.ShapeDtypeStruct((3, d), F32)),
                 (dh, wout, u4, z, xn, cw),
                 scratch=[pltpu.VMEM((3, d, cc), F32), pltpu.VMEM((cc, d), F32),
                          pltpu.VMEM((8, cc), F32), pltpu.VMEM((8, cc), F32)], rider=rider)


def _silu_parts(cg):
    sg = 1.0 / (1.0 + jnp.exp(-cg))
    return sg, cg * sg


def ffn_fwd(xn, wup, fcw, name, rider=()):
    seq, d = xn.shape
    f8 = wup.shape[-1]
    half = N_DEV // 2
    tm = min(ROW_TILE_LARGE, seq)
    n_i = seq // tm

    def body(x_ref, wg_ref, wu_ref, cg_ref, cu_ref, up_ref, cv_ref, a_ref, carry):
        i = pl.program_id(1)

        @pl.when(i == 0)
        def _():
            carry[...] = jnp.zeros_like(carry)
        xb = x_ref[...]
        conv = []
        for s, (w_ref, t_ref) in enumerate(((wg_ref, cg_ref), (wu_ref, cu_ref))):
            u = _nn(xb, w_ref[...])
            up_ref[s] = u.astype(BF16)
            w0, w1, w2 = _conv_taps(t_ref)
            u1 = _shift_down(u, 1, [carry[s, 7:8, :]])
            u2 = _shift_down(u, 2, [carry[s, 6:7, :], carry[s, 7:8, :]])
            cv = w0 * u2 + w1 * u1 + w2 * u
            cv_ref[s] = cv.astype(BF16)
            conv.append(cv)
            carry[s] = u[tm - 8:tm, :]
        _, silu = _silu_parts(conv[0])
        a_ref[...] = (silu * conv[1]).astype(BF16)

    blk = pl.BlockSpec((2, None, tm, f8), lambda c, i: (0, c, i, 0))
    big = jax.ShapeDtypeStruct((2, half, seq, f8), BF16)
    return _call(body, name, (half, n_i),
                 [pl.BlockSpec((tm, d), lambda c, i: (i, 0)),
                  pl.BlockSpec((None, d, f8), lambda c, i: (c, 0, 0)),
                  pl.BlockSpec((None, d, f8), lambda c, i: (c + half, 0, 0)),
                  pl.BlockSpec((None, 3, f8), lambda c, i: (c, 0, 0)),
                  pl.BlockSpec((None, 3, f8), lambda c, i: (c + half, 0, 0))],
                 (blk, blk, pl.BlockSpec((None, tm, f8), lambda c, i: (c, i, 0))),
                 (big, big, jax.ShapeDtypeStruct((half, seq, f8), BF16)),
                 (xn, wup, wup, fcw, fcw), scratch=[pltpu.VMEM((2, 8, f8), F32)], rider=rider)


def ffn_bwd(dh, wdown, up2, cv2, act, xn, fcw, name, rider=()):
    seq, d = xn.shape
    f8 = up2.shape[-1]
    fb = wdown.shape[1]
    half = N_DEV // 2
    tm = min(ROW_TILE, seq)
    n_i = seq // tm

    def body(dh_ref, wd_ref, up_ref, cv_ref, a_ref, x_ref, cg_ref, cu_ref,
             dup_ref, dwup_ref, dwd_ref, dcw_ref, acc_up, acc_down, acc_cw, carry):
        i = pl.program_id(1)

        @pl.when(i == 0)
        def _():
            acc_up[...] = jnp.zeros_like(acc_up)
            acc_down[...] = jnp.zeros_like(acc_down)
            acc_cw[...] = jnp.zeros_like(acc_cw)
            carry[...] = jnp.zeros_like(carry)
        dhb = dh_ref[...].astype(BF16)
        da = _nt(dhb, wd_ref[...])
        acc_down[...] += _tn(a_ref[...], dhb)
        cg = cv_ref[0].astype(F32)
        cu = cv_ref[1].astype(F32)
        sg, silu = _silu_parts(cg)
        dcg = da * cu * (sg + silu * (1.0 - sg))
        dcu = da * silu
        xb = x_ref[...]
        for s, (dc, t_ref) in enumerate(((dcg, cg_ref), (dcu, cu_ref))):
            w0, w1, w2 = _conv_taps(t_ref)
            d1 = _shift_up(dc, 1, [carry[s, 0:1, :]])
            d2 = _shift_up(dc, 2, [carry[s, 0:1, :], carry[s, 1:2, :]])
            du = (w2 * dc + w1 * d1 + w0 * d2).astype(BF16)
            carry[s] = dc[0:8, :]
            u = up_ref[s].astype(F32)
            acc_cw[s, 0:1, :] += jnp.sum(d2 * u, axis=0, keepdims=True)
            acc_cw[s, 1:2, :] += jnp.sum(d1 * u, axis=0, keepdims=True)
            acc_cw[s, 2:3, :] += jnp.sum(dc * u, axis=0, keepdims=True)
            dup_ref[s] = du
            acc_up[s] += _tn(xb, du)

        @pl.when(i == n_i - 1)
        def _():
            dwup_ref[...] = acc_up[...].astype(BF16)
            dwd_ref[...] = acc_down[...].astype(BF16)
            dcw_ref[...] = acc_cw[:, 0:3, :]

    rev = lambda c, i: (n_i - 1 - i, 0)
    blk = pl.BlockSpec((2, None, tm, f8), lambda c, i: (0, c, n_i - 1 - i, 0))
    return _call(body, name, (half, n_i),
                 [pl.BlockSpec((tm, d), rev),
                  pl.BlockSpec((None, fb, d), lambda c, i: (c, 0, 0)),
                  blk, blk,
                  pl.BlockSpec((None, tm, f8), lambda c, i: (c, n_i - 1 - i, 0)),
                  pl.BlockSpec((tm, d), rev),
                  pl.BlockSpec((None, 3, f8), lambda c, i: (c, 0, 0)),
                  pl.BlockSpec((None, 3, f8), lambda c, i: (c + half, 0, 0))],
                 (blk,
                  pl.BlockSpec((2, None, d, f8), lambda c, i: (0, c, 0, 0)),
                  pl.BlockSpec((None, fb, d), lambda c, i: (c, 0, 0)),
                  pl.BlockSpec((2, None, 3, f8), lambda c, i: (0, c, 0, 0))),
                 (jax.ShapeDtypeStruct((2, half, seq, f8), BF16),
                  jax.ShapeDtypeStruct((2, half, d, f8), BF16),
                  jax.ShapeDtypeStruct((half, fb, d), BF16),
                  jax.ShapeDtypeStruct((2, half, 3, f8), F32)),
                 (dh, wdown, up2, cv2, act, xn, fcw, fcw),
                 scratch=[pltpu.VMEM((2, d, f8), F32), pltpu.VMEM((fb, d), F32),
                          pltpu.VMEM((2, 8, f8), F32), pltpu.VMEM((2, 8, f8), F32)], rider=rider)


def q_fwd(xn, wdq, gq, wuq, cos, sin, name, rider=()):
    seq, d = xn.shape
    rank = wdq.shape[-1]
    tm = min(ROW_TILE, seq)

    def body(x_ref, wdq_ref, gq_ref, wuq_ref, cos_ref, sin_ref, q_ref):
        qc = _nn(x_ref[...], wdq_ref[...])
        qn = _rms(qc, gq_ref[...])[0].astype(BF16)
        for hd in range(N_HEADS):
            qh = _nn(qn, wuq_ref[hd])
            qr = _rope_fwd(qh[:, NOPE:QK], cos_ref[...], sin_ref[...])
            q_ref[hd, :, 0:NOPE] = (qh[:, 0:NOPE] * ATTN_SCALE).astype(BF16)
            q_ref[hd, :, NOPE:QK] = (qr * ATTN_SCALE).astype(BF16)

    rope = pl.BlockSpec((tm, ROPE_PAD), lambda i: (i, 0))
    return _call(body, name, (seq // tm,),
                 [pl.BlockSpec((tm, d), lambda i: (i, 0)),
                  pl.BlockSpec((d, rank), lambda i: (0, 0)),
                  pl.BlockSpec((1, rank), lambda i: (0, 0)),
                  pl.BlockSpec((N_HEADS, rank, QK), lambda i: (0, 0, 0)), rope, rope],
                 [pl.BlockSpec((N_HEADS, tm, QK), lambda i: (0, i, 0))],
                 [jax.ShapeDtypeStruct((N_HEADS, seq, QK), BF16)],
                 (xn, wdq, gq, wuq, cos, sin), rider=rider)


def q_bwd(dq, xn, wdq, gq, wuq, cos, sin, name, rider=()):
    seq, d = xn.shape
    rank = wdq.shape[-1]
    tm = min(ROW_TILE, seq)
    n_i = seq // tm

    def body(dq_ref, x_ref, wdq_ref, gq_ref, wuq_ref, cos_ref, sin_ref,
             dqc_ref, dwuq_ref, dwdq_ref, dgq_ref, acc_uq, acc_dq):
        i = pl.program_id(0)

        @pl.when(i == 0)
        def _():
            acc_uq[...] = jnp.zeros_like(acc_uq)
            acc_dq[...] = jnp.zeros_like(acc_dq)
            dgq_ref[...] = jnp.zeros_like(dgq_ref)
        xb = x_ref[...]
        qc = _nn(xb, wdq_ref[...])
        qn, qhat, qrstd = _rms(qc, gq_ref[...])
        qnb = qn.astype(BF16)
        dqn = jnp.zeros((tm, rank), F32)
        for hd in range(N_HEADS):
            dnope = (dq_ref[hd, :, 0:NOPE].astype(F32) * ATTN_SCALE).astype(BF16)
            drope = _rope_bwd(dq_ref[hd, :, NOPE:QK].astype(F32) * ATTN_SCALE, cos_ref[...], sin_ref[...])
            draw = jnp.concatenate([dnope, drope.astype(BF16)], axis=1)
            dqn = dqn + _nt(draw, wuq_ref[hd])
            acc_uq[hd] += _tn(qnb, draw)
        dqc, dg_rows = _rms_bwd(dqn, qhat, qrstd, gq_ref[...])
        dgq_ref[...] += jnp.sum(dg_rows, axis=0, keepdims=True)
        dqcb = dqc.astype(BF16)
        dqc_ref[0] = dqcb
        acc_dq[...] += _tn(xb, dqcb)

        @pl.when(i == n_i - 1)
        def _():
            dwuq_ref[...] = acc_uq[...].astype(BF16)
            dwdq_ref[...] = acc_dq[...].astype(BF16)

    rope = pl.BlockSpec((tm, ROPE_PAD), lambda i: (i, 0))
    return _call(body, name, (n_i,),
                 [pl.BlockSpec((N_HEADS, tm, QK), lambda i: (0, i, 0)),
                  pl.BlockSpec((tm, d), lambda i: (i, 0)),
                  pl.BlockSpec((d, rank), lambda i: (0, 0)),
                  pl.BlockSpec((1, rank), lambda i: (0, 0)),
                  pl.BlockSpec((N_HEADS, rank, QK), lambda i: (0, 0, 0)), rope, rope],
                 (pl.BlockSpec((1, tm, rank), lambda i: (0, i, 0)),
                  pl.BlockSpec((N_HEADS, rank, QK), lambda i: (0, 0, 0)),
                  pl.BlockSpec((d, rank), lambda i: (0, 0)),
                  pl.BlockSpec((1, rank), lambda i: (0, 0))),
                 (jax.ShapeDtypeStruct((1, seq, rank), BF16),
                  jax.ShapeDtypeStruct((N_HEADS, rank, QK), BF16),
                  jax.ShapeDtypeStruct((d, rank), BF16),
                  jax.ShapeDtypeStruct((1, rank), F32)),
                 (dq, xn, wdq, gq, wuq, cos, sin),
                 scratch=[pltpu.VMEM((N_HEADS, rank, QK), F32), pltpu.VMEM((d, rank), F32)],
                 rider=rider)


def kv_fwd(h, g, wdkv, gkv, wukv, cos, sin, name, rider=()):
    seq, d = h.shape
    tm = min(ROW_TILE, seq)
    wk = KV_RANK + ROPE_PAD

    def body(h_ref, g_ref, wdkv_ref, gkv_ref, wukv_ref, cos_ref, sin_ref, k_ref, v_ref, c_ref):
        xk = _rms(h_ref[...], g_ref[...])[0].astype(BF16)
        ckv = _nn(xk, wdkv_ref[...])
        c_kv = ckv[:, 0:KV_RANK]
        c_ref[...] = c_kv
        kr = _rope_fwd(ckv[:, KV_RANK:wk], cos_ref[...], sin_ref[...]).astype(BF16)
        ckn = _rms(c_kv, gkv_ref[...])[0].astype(BF16)
        for hd in range(N_HEADS):
            kvh = _nn(ckn, wukv_ref[hd])
            k_ref[hd, :, 0:NOPE] = kvh[:, 0:NOPE].astype(BF16)
            k_ref[hd, :, NOPE:QK] = kr
            v_ref[hd] = kvh[:, NOPE:NOPE + VDIM].astype(BF16)

    rope = pl.BlockSpec((tm, ROPE_PAD), lambda i: (i, 0))
    return _call(body, name, (seq // tm,),
                 [pl.BlockSpec((tm, d), lambda i: (i, 0)),
                  pl.BlockSpec((1, d), lambda i: (0, 0)),
                  pl.BlockSpec((d, wk), lambda i: (0, 0)),
                  pl.BlockSpec((1, KV_RANK), lambda i: (0, 0)),
                  pl.BlockSpec((N_HEADS, KV_RANK, NOPE + VDIM), lambda i: (0, 0, 0)), rope, rope],
                 (pl.BlockSpec((N_HEADS, tm, QK), lambda i: (0, i, 0)),
                  pl.BlockSpec((N_HEADS, tm, VDIM), lambda i: (0, i, 0)),
                  pl.BlockSpec((tm, KV_RANK), lambda i: (i, 0))),
                 (jax.ShapeDtypeStruct((N_HEADS, seq, QK), BF16),
                  jax.ShapeDtypeStruct((N_HEADS, seq, VDIM), BF16),
                  jax.ShapeDtypeStruct((seq, KV_RANK), F32)),
                 (h, g, wdkv, gkv, wukv, cos, sin), rider=rider)


def kv_bwd(dks, dvs, c_kv, h, g, gkv, wukv, cos, sin, name, rider=()):
    seq, d = h.shape
    tm = min(ROW_TILE, seq)
    n_i = seq // tm
    wk = KV_RANK + ROPE_PAD
    n_b = len(dks)

    def body(*refs):
        dk_refs = refs[:n_b]
        dv_refs = refs[n_b:2 * n_b]
        (c_ref, h_ref, g_ref, gkv_ref, wukv_ref, cos_ref, sin_ref,
         dckv_ref, dwukv_ref, dwdkv_ref, dgkv_ref, acc_ukv, acc_dkv) = refs[2 * n_b:]
        i = pl.program_id(0)

        @pl.when(i == 0)
        def _():
            acc_ukv[...] = jnp.zeros_like(acc_ukv)
            acc_dkv[...] = jnp.zeros_like(acc_dkv)
            dgkv_ref[...] = jnp.zeros_like(dgkv_ref)
        ckn, chat, crstd = _rms(c_ref[...], gkv_ref[...])
        cknb = ckn.astype(BF16)
        dckn = jnp.zeros((tm, KV_RANK), F32)
        dkr = jnp.zeros((tm, ROPE_PAD), F32)
        for hd in range(N_HEADS):
            dk = dk_refs[0][hd].astype(F32)
            dv = dv_refs[0][hd].astype(F32)
            for j in range(1, n_b):
                dk = dk + dk_refs[j][hd].astype(F32)
                dv = dv + dv_refs[j][hd].astype(F32)
            dkr = dkr + dk[:, NOPE:QK]
            dkvh = jnp.concatenate([dk[:, 0:NOPE].astype(BF16), dv.astype(BF16)], axis=1)
            dckn = dckn + _nt(dkvh, wukv_ref[hd])
            acc_ukv[hd] += _tn(cknb, dkvh)
        dc_kv, dg_rows = _rms_bwd(dckn, chat, crstd, gkv_ref[...])
        dgkv_ref[...] += jnp.sum(dg_rows, axis=0, keepdims=True)
        dkr_raw = _rope_bwd(dkr, cos_ref[...], sin_ref[...])
        dckv = jnp.concatenate([dc_kv.astype(BF16), dkr_raw.astype(BF16)], axis=1)
        dckv_ref[0] = dckv
        xk = _rms(h_ref[...], g_ref[...])[0].astype(BF16)
        acc_dkv[...] += _tn(xk, dckv)

        @pl.when(i == n_i - 1)
        def _():
            dwukv_ref[...] = acc_ukv[...].astype(BF16)
            dwdkv_ref[...] = acc_dkv[...].astype(BF16)

    kspec = pl.BlockSpec((N_HEADS, tm, QK), lambda i: (0, i, 0))
    vspec = pl.BlockSpec((N_HEADS, tm, VDIM), lambda i: (0, i, 0))
    rope = pl.BlockSpec((tm, ROPE_PAD), lambda i: (i, 0))
    return _call(body, name, (n_i,),
                 [kspec] * n_b + [vspec] * n_b + [
                     pl.BlockSpec((tm, KV_RANK), lambda i: (i, 0)),
                     pl.BlockSpec((tm, d), lambda i: (i, 0)),
                     pl.BlockSpec((1, d), lambda i: (0, 0)),
                     pl.BlockSpec((1, KV_RANK), lambda i: (0, 0)),
                     pl.BlockSpec((N_HEADS, KV_RANK, NOPE + VDIM), lambda i: (0, 0, 0)), rope, rope],
                 (pl.BlockSpec((1, tm, wk), lambda i: (0, i, 0)),
                  pl.BlockSpec((N_HEADS, KV_RANK, NOPE + VDIM), lambda i: (0, 0, 0)),
                  pl.BlockSpec((d, wk), lambda i: (0, 0)),
                  pl.BlockSpec((1, KV_RANK), lambda i: (0, 0))),
                 (jax.ShapeDtypeStruct((1, seq, wk), BF16),
                  jax.ShapeDtypeStruct((N_HEADS, KV_RANK, NOPE + VDIM), BF16),
                  jax.ShapeDtypeStruct((d, wk), BF16),
                  jax.ShapeDtypeStruct((1, KV_RANK), F32)),
                 (*dks, *dvs, c_kv, h, g, gkv, wukv, cos, sin),
                 scratch=[pltpu.VMEM((N_HEADS, KV_RANK, NOPE + VDIM), F32), pltpu.VMEM((d, wk), F32)],
                 rider=rider)


def o_bwd(dh, o, wo, name, rider=()):
    seq, d = dh.shape
    hv = o.shape[1]
    tm = min(ROW_TILE, seq)
    n_i = seq // tm

    def body(dh_ref, o_ref, wo_ref, do_ref, dwo_ref, acc):
        i = pl.program_id(0)

        @pl.when(i == 0)
        def _():
            acc[...] = jnp.zeros_like(acc)
        dhb = dh_ref[...].astype(BF16)
        do_ref[...] = _nt(dhb, wo_ref[...]).astype(BF16)
        acc[...] += _tn(o_ref[...], dhb)

        @pl.when(i == n_i - 1)
        def _():
            dwo_ref[...] = acc[...].astype(BF16)

    return _call(body, name, (n_i,),
                 [pl.BlockSpec((tm, d), lambda i: (i, 0)),
                  pl.BlockSpec((tm, hv), lambda i: (i, 0)),
                  pl.BlockSpec((hv, d), lambda i: (0, 0))],
                 (pl.BlockSpec((tm, hv), lambda i: (i, 0)),
                  pl.BlockSpec((hv, d), lambda i: (0, 0))),
                 (jax.ShapeDtypeStruct((seq, hv), BF16), jax.ShapeDtypeStruct((hv, d), BF16)),
                 (dh, o, wo), scratch=[pltpu.VMEM((hv, d), F32)], rider=rider)


def _mask_diagonal(s):
    row = lax.broadcasted_iota(jnp.int32, s.shape, 0)
    col = lax.broadcasted_iota(jnp.int32, s.shape, 1)
    return jnp.where(col <= row, s, NEG_BIG)


def attn_fwd(q, k, v, name, rider=()):
    _, seq, _ = q.shape
    t = min(ATTN_TILE, seq // 2)
    n_pair = seq // (2 * t)

    def body(q_ref, k_ref, v_ref, o_ref, lse_ref):
        qi = pl.program_id(1)
        q_a = q_ref[0:t, :]
        q_b = q_ref[t:2 * t, :]

        def rows(j):
            return pl.ds(pl.multiple_of(j * t, t), t)

        def update(qx, kb, vb, state, diagonal=False):
            m, l, acc = state
            s = _nt(qx, kb)
            if diagonal:
                s = _mask_diagonal(s)
            m_new = jnp.maximum(m, jnp.max(s, axis=1, keepdims=True))
            p = jnp.exp(s - m_new)
            alpha = jnp.exp(m - m_new)
            l = alpha * l + jnp.sum(p, axis=1, keepdims=True)
            acc = alpha * acc + _nn(p.astype(BF16), vb)
            return m_new, l, acc

        def step(j, carry):
            both = pl.ds(pl.multiple_of(j * 2 * t, 2 * t), 2 * t)
            kb, vb = k_ref[both, :], v_ref[both, :]
            return update(q_a, kb, vb, carry[0:3]) + update(q_b, kb, vb, carry[3:6])

        init = (jnp.full((t, 1), NEG_BIG, F32), jnp.zeros((t, 1), F32), jnp.zeros((t, VDIM), F32))
        carry = lax.fori_loop(0, qi, step, init + init)
        k0, v0 = k_ref[rows(2 * qi), :], v_ref[rows(2 * qi), :]
        k1, v1 = k_ref[rows(2 * qi + 1), :], v_ref[rows(2 * qi + 1), :]
        state_a = update(q_a, k0, v0, carry[0:3], diagonal=True)
        state_b = update(q_b, k1, v1, update(q_b, k0, v0, carry[3:6]), diagonal=True)
        for half, (m, l, acc) in enumerate((state_a, state_b)):
            o_ref[half * t:(half + 1) * t, :] = (acc / l).astype(BF16)
            lse_ref[half * t:(half + 1) * t, :] = jnp.broadcast_to(m + jnp.log(l), (t, LANES))

    return _call(body, name, (N_HEADS, n_pair),
                 [pl.BlockSpec((None, 2 * t, QK), lambda h, i: (h, i, 0)),
                  pl.BlockSpec((None, seq, QK), lambda h, i: (h, 0, 0)),
                  pl.BlockSpec((None, seq, VDIM), lambda h, i: (h, 0, 0))],
                 (pl.BlockSpec((2 * t, VDIM), lambda h, i: (i, h)),
                  pl.BlockSpec((None, 2 * t, LANES), lambda h, i: (h, i, 0))),
                 (jax.ShapeDtypeStruct((seq, N_HEADS * VDIM), BF16),
                  jax.ShapeDtypeStruct((N_HEADS, seq, LANES), F32)),
                 (q, k, v), rider=rider)


def attn_bwd(q, k, v, o, do, lse, name, rider=()):
    _, seq, _ = q.shape
    t = min(ATTN_TILE, seq // 2)
    n_q = seq // t
    n_pair = n_q // 2

    def body(q_ref, k_ref, v_ref, o_ref, do_ref, lse_ref, dq_ref, dk_ref, dv_ref,
             dq_acc, dk_acc, dv_acc):
        kj = pl.program_id(1)

        @pl.when(kj == 0)
        def _():
            dq_acc[...] = jnp.zeros_like(dq_acc)
        dk_acc[...] = jnp.zeros_like(dk_acc)
        dv_acc[...] = jnp.zeros_like(dv_acc)
        halves = (slice(0, t), slice(t, 2 * t))

        def block(i, masks, n_rows=t):
            rows = pl.ds(pl.multiple_of(i * n_rows, n_rows), n_rows)
            qb = q_ref[rows, :]
            dob = do_ref[rows, :]
            lse_col = lse_ref[rows, 0:1]
            delta = jnp.sum(dob.astype(F32) * o_ref[rows, :].astype(F32), axis=1, keepdims=True)
            dq = None
            for x, diagonal in enumerate(masks):
                if diagonal is None:
                    continue
                kb, vb = k_ref[halves[x], :], v_ref[halves[x], :]
                s = _nt(qb, kb)
                if diagonal:
                    s = _mask_diagonal(s)
                p = jnp.exp(s - lse_col)
                ds = (p * (_nt(dob, vb) - delta)).astype(BF16)
                dv_acc[halves[x], :] += _tn(p.astype(BF16), dob)
                dk_acc[halves[x], :] += _tn(ds, qb)
                part = _nn(ds, kb)
                dq = part if dq is None else dq + part
            dq_acc[rows, :] += dq

        block(2 * kj, (True, None))
        block(2 * kj + 1, (False, True))

        def step(i, carry):
            block(i, (False, False), n_rows=2 * t)
            return carry

        lax.fori_loop(kj + 1, n_pair, step, 0)
        dk_ref[...] = dk_acc[...].astype(BF16)
        dv_ref[...] = dv_acc[...].astype(BF16)

        @pl.when(kj == n_pair - 1)
        def _():
            dq_ref[...] = dq_acc[...].astype(BF16)

    head_rows = pl.BlockSpec((seq, VDIM), lambda h, j: (0, h))
    return _call(body, name, (N_HEADS, n_pair),
                 [pl.BlockSpec((None, seq, QK), lambda h, j: (h, 0, 0)),
                  pl.BlockSpec((None, 2 * t, QK), lambda h, j: (h, j, 0)),
                  pl.BlockSpec((None, 2 * t, VDIM), lambda h, j: (h, j, 0)),
                  head_rows, head_rows,
                  pl.BlockSpec((None, seq, LANES), lambda h, j: (h, 0, 0))],
                 (pl.BlockSpec((None, seq, QK), lambda h, j: (h, 0, 0)),
                  pl.BlockSpec((None, 2 * t, QK), lambda h, j: (h, j, 0)),
                  pl.BlockSpec((None, 2 * t, VDIM), lambda h, j: (h, j, 0))),
                 (jax.ShapeDtypeStruct((N_HEADS, seq, QK), BF16),
                  jax.ShapeDtypeStruct((N_HEADS, seq, QK), BF16),
                  jax.ShapeDtypeStruct((N_HEADS, seq, VDIM), BF16)),
                 (q, k, v, o, do, lse),
                 scratch=[pltpu.VMEM((seq, QK), F32), pltpu.VMEM((2 * t, QK), F32),
                          pltpu.VMEM((2 * t, VDIM), F32)], rider=rider)


def loss_head(h, g, target, name):
    seq, d = h.shape
    tm = min(ROW_TILE, seq)

    def body(h_ref, g_ref, t_ref, l_ref, dh_ref, dg_ref):
        i = pl.program_id(0)

        @pl.when(i == 0)
        def _():
            l_ref[...] = jnp.zeros_like(l_ref)
            dg_ref[...] = jnp.zeros_like(dg_ref)
        y, xhat, rstd = _rms(h_ref[...], g_ref[...])
        diff = y - t_ref[...]
        l_ref[...] += jnp.sum(jnp.sum(diff * diff, axis=1, keepdims=True), axis=0, keepdims=True)
        dh, dg_rows = _rms_bwd(diff * (1.0 / d), xhat, rstd, g_ref[...])
        dh_ref[...] = dh
        dg_ref[...] += jnp.sum(dg_rows, axis=0, keepdims=True)

    row = pl.BlockSpec((tm, d), lambda i: (i, 0))
    vec = pl.BlockSpec((1, d), lambda i: (0, 0))
    return _call(body, name, (seq // tm,), [row, vec, row],
                 (pl.BlockSpec((1, LANES), lambda i: (0, 0)), row, vec),
                 (jax.ShapeDtypeStruct((1, LANES), F32), jax.ShapeDtypeStruct((seq, d), F32),
                  jax.ShapeDtypeStruct((1, d), F32)),
                 (h, g, target))[0]


def _pack(parts):
    rows = []
    for p in parts:
        flat = p.reshape(-1)
        n_rows = -(-flat.shape[0] // (8 * LANES)) * 8
        flat = jnp.pad(flat, (0, n_rows * LANES - flat.shape[0]))
        rows.append(flat.reshape(n_rows, LANES))
    return jnp.concatenate(rows, axis=0)


def _unpack(packed, shapes):
    lead = packed.shape[:-2]
    out, r0 = [], 0
    for shape in shapes:
        size = 1
        for s in shape:
            size *= s
        n_rows = -(-size // (8 * LANES)) * 8
        part = packed[..., r0:r0 + n_rows, :].reshape(lead + (n_rows * LANES,))
        out.append(part[..., :size].reshape(lead + tuple(shape)))
        r0 += n_rows
    return out


FWD_RIDERS = {
    "mixer_fwd0": [("ffn_w_up", 0)],
    "ffn_fwd0": [("ffn_w_down", 0), ("a_w_in", 1), ("a_w_out", 1)],
    "ffn_out0": [("ffn_w_down", 1)],
    "mixer_fwd1": [("ffn_w_up", 1)],
    "mixer_out1": [("w_dkv", 0), ("w_ukv", 0), ("b_w_dq", 0), ("b_w_uq", 0)],
    "ffn_fwd1": [("ffn_w_up", 2), ("b_w_o", 0)],
    "ffn_out1": [("ffn_w_down", 2)],
    "attn_fwd0": [("ffn_w_up", 3), ("ffn_w_down", 3), ("b_w_dq", 1), ("b_w_uq", 1), ("b_w_o", 1)],
}
BWD_RIDERS = {
    "ffn_in_bwd3": [("ffn_w_down", 3)],
    "attn_bwd1": [("ffn_w_up", 3), ("b_w_o", 1)],
    "ffn_bwd2": [("b_w_uq", 1), ("b_w_dq", 1)],
    "ffn_in_bwd2": [("ffn_w_down", 2)],
    "attn_bwd0": [("ffn_w_up", 2), ("b_w_o", 0)],
    "ffn_bwd1": [("b_w_uq", 0), ("b_w_dq", 0), ("w_ukv", 0), ("w_dkv", 0)],
    "ffn_in_bwd1": [("ffn_w_down", 1), ("ffn_w_up", 1, "pair")],
    "ffn_bwd0": [("ffn_w_up", 1, "chip"), ("a_w_in", 1), ("a_w_out", 1)],
    "ffn_in_bwd0": [("ffn_w_down", 0), ("ffn_w_up", 0, "pair")],
    "mixer_bwd0": [("ffn_w_up", 0, "chip")],
    "mixer_in_bwd0": [("a_w_out", 0), ("a_w_in", 0, "pair")],
    "adamw_a_w_out": [("a_w_in", 0, "chip")],
}


def kernel(x, a_mix_norm, a_w_in, a_conv, a_w_out, b_mix_norm, b_w_dq, b_q_norm, b_w_uq, b_w_o, kv_in_norm, w_dkv, kv_norm, w_ukv, ffn_norm, ffn_w_up, ffn_conv, ffn_w_down, final_norm, loss_target, m_a_mix_norm, m_a_w_in, m_a_conv, m_a_w_out, m_b_mix_norm, m_b_w_dq, m_b_q_norm, m_b_w_uq, m_b_w_o, m_kv_in_norm, m_w_dkv, m_kv_norm, m_w_ukv, m_ffn_norm, m_ffn_w_up, m_ffn_conv, m_ffn_w_down, m_final_norm, v_a_mix_norm, v_a_w_in, v_a_conv, v_a_w_out, v_b_mix_norm, v_b_w_dq, v_b_q_norm, v_b_w_uq, v_b_w_o, v_kv_in_norm, v_w_dkv, v_kv_norm, v_w_ukv, v_ffn_norm, v_ffn_w_up, v_ffn_conv, v_ffn_w_down, v_final_norm):
    seq, d = x.shape[1], x.shape[2]
    me = 4 * lax.axis_index("x") + 2 * lax.axis_index("y") + lax.axis_index("c")
    h0 = x.reshape(seq, d)
    target = loss_target.reshape(seq, d)
    cos, sin = _rope_tables(seq)
    rank = b_w_dq.shape[-1]
    f8 = ffn_w_up.shape[-1]
    fd = ffn_w_down.shape[1]
    dshard = a_w_out.shape[1]
    hv = N_HEADS * VDIM

    shards = {"a_w_in": a_w_in, "a_w_out": a_w_out, "b_w_dq": b_w_dq, "b_w_uq": b_w_uq,
              "b_w_o": b_w_o, "w_dkv": w_dkv[None], "w_ukv": w_ukv[None],
              "ffn_w_up": ffn_w_up, "ffn_w_down": ffn_w_down}

    def relayout(name, g):
        if name == "a_w_in":
            w = jnp.transpose(g, (1, 0, 2)).reshape(d, 3, d)
            return jnp.transpose(w, (1, 0, 2))
        if name == "a_w_out":
            return g.reshape(d, d)
        if name == "b_w_dq":
            return g.reshape(d, rank)
        if name == "b_w_uq":
            return jnp.pad(g, ((0, 0), (0, 0), (0, QK - NOPE - ROPE)))
        if name == "b_w_o":
            return g.reshape(hv, d)
        if name == "w_dkv":
            return jnp.pad(g.reshape(d, KV_RANK + ROPE), ((0, 0), (0, ROPE_PAD - ROPE)))
        if name == "ffn_w_down":
            return g.reshape(N_DEV // 2, 2 * fd, d)
        return g

    weights = {}

    def ag_rider(host):
        return [("ag", shards[n][l].astype(BF16)) for n, l in FWD_RIDERS.get(host, [])]

    def ag_done(host, outs):
        for (n, l), g in zip(FWD_RIDERS.get(host, []), outs):
            weights[n, l] = relayout(n, g)

    small_shapes = [a_mix_norm.shape, a_conv.shape, ffn_conv.shape]
    first = exchange([("ag", a_w_in[0].astype(BF16)), ("ag", a_w_out[0].astype(BF16)),
                      ("ag", _pack([a_mix_norm, a_conv, ffn_conv]))], "ag_first")
    weights["a_w_in", 0] = relayout("a_w_in", first[0])
    weights["a_w_out", 0] = relayout("a_w_out", first[1])
    s_mix, s_aconv, s_fconv = _unpack(first[2], small_shapes)
    a_gain = jnp.transpose(s_mix, (1, 0, 2)).reshape(N_A, d)
    a_cw = jnp.transpose(s_aconv, (1, 2, 0, 3)).reshape(N_A, 3, d)
    f_cw = jnp.transpose(s_fconv, (1, 0, 2, 3))

    def mixer_gain(layer):
        if layer >= DEPTH:
            return None
        return a_gain[layer][None] if layer < N_A else b_mix_norm[layer - N_A][None]

    saved = {}
    h = h0
    xn = norm_fwd(h, mixer_gain(0), "norm_first")
    kv = None
    for layer in range(DEPTH):
        saved["hm", layer], saved["xm", layer] = h, xn
        if layer < N_A:
            name = f"mixer_fwd{layer}"
            (u4, z), r = mixer_fwd(xn, weights["a_w_in", layer], a_cw[layer], name, rider=ag_rider(name))
            ag_done(name, r)
            saved["mix", layer] = (u4, z)
            name = f"mixer_out{layer}"
            (h, xn), r = proj_residual(z[None], weights["a_w_out", layer][None], h, name,
                                       g_next=ffn_norm[layer][None], rider=ag_rider(name))
            ag_done(name, r)
        else:
            j = layer - N_A
            name = f"q_fwd{j}"
            (q,), r = q_fwd(xn, weights["b_w_dq", j], b_q_norm[j][None], weights["b_w_uq", j],
                            cos, sin, name, rider=ag_rider(name))
            ag_done(name, r)
            name = f"attn_fwd{j}"
            (o, lse), r = attn_fwd(q, kv[0], kv[1], name, rider=ag_rider(name))
            ag_done(name, r)
            saved["attn", layer] = (q, o, lse)
            name = f"attn_out{j}"
            (h, xn), r = proj_residual(o[None], weights["b_w_o", j][None], h, name,
                                       g_next=ffn_norm[layer][None], rider=ag_rider(name))
            ag_done(name, r)
        saved["hf", layer], saved["xf", layer] = h, xn
        name = f"ffn_fwd{layer}"
        (up2, cv2, act), r = ffn_fwd(xn, weights["ffn_w_up", layer], f_cw[layer], name, rider=ag_rider(name))
        ag_done(name, r)
        saved["ffn", layer] = (up2, cv2, act)
        name = f"ffn_out{layer}"
        (h, xn), r = proj_residual(act, weights["ffn_w_down", layer], h, name,
                                   g_next=mixer_gain(layer + 1), rider=ag_rider(name))
        ag_done(name, r)
        if layer == N_A - 1:
            (k_all, v_all, c_kv), r = kv_fwd(h, kv_in_norm[None], weights["w_dkv", 0], kv_norm[None],
                                             weights["w_ukv", 0], cos, sin, "kv_fwd",
                                             rider=ag_rider("kv_fwd"))
            ag_done("kv_fwd", r)
            kv = (k_all, v_all, c_kv)

    sq_err, dh, d_final = loss_head(h, final_norm[None], target, "loss_head")
    loss = lax.psum(sq_err[0, 0] * (0.5 / d), ("x", "y", "c"))

    grads = {}
    parts = {}

    pair_sums = {}

    def by_chip(g):
        return g.reshape((N_DEV // 2, 2) + g.shape[1:])

    def rs_rider(host):
        tasks = []
        for key in BWD_RIDERS.get(host, []):
            if len(key) == 2:
                tasks.append(("rs", grads[key]))
            elif key[2] == "pair":
                tasks.append(("rs_pair", by_chip(grads[key[:2]])))
            else:
                tasks.append(("rs_chip", pair_sums[key[:2]]))
        return tasks

    def rs_done(host, outs):
        for key, p in zip(BWD_RIDERS.get(host, []), outs):
            if len(key) == 3 and key[2] == "pair":
                pair_sums[key[:2]] = pair_sum(by_chip(grads[key[:2]]), p, f"pair_sum_{key[0]}{key[1]}")
            else:
                parts[key[:2]] = p

    d_ffn_norm = [None] * DEPTH
    d_fconv = [None] * DEPTH
    d_a_gain = [None] * N_A
    d_aconv = [None] * N_A
    d_b_gain = [None] * N_B
    d_q_gain = [None] * N_B
    dks, dvs = [], []
    for layer in reversed(range(DEPTH)):
        if layer == N_A - 1:
            hk = saved["hm", layer + 1]
            (dckv, dwukv, dwdkv, d_kv_gain), r = kv_bwd(
                dks, dvs, kv[2], hk, kv_in_norm[None], kv_norm[None], weights["w_ukv", 0],
                cos, sin, "kv_bwd", rider=rs_rider("kv_bwd"))
            rs_done("kv_bwd", r)
            grads["w_ukv", 0] = dwukv
            grads["w_dkv", 0] = dwdkv[:, :KV_RANK + ROPE].reshape(N_DEV, dshard, KV_RANK + ROPE)
            (dh, d_kvin_gain), r = proj_t_rms_bwd(dckv, weights["w_dkv", 0][None], hk, kv_in_norm[None],
                                                  dh, "kv_in_bwd", rider=rs_rider("kv_in_bwd"))
            rs_done("kv_in_bwd", r)
        up2, cv2, act = saved["ffn", layer]
        name = f"ffn_bwd{layer}"
        (dup2, dwup, dwdown, dcw), r = ffn_bwd(dh, weights["ffn_w_down", layer], up2, cv2, act,
                                               saved["xf", layer], f_cw[layer], name, rider=rs_rider(name))
        rs_done(name, r)
        grads["ffn_w_up", layer] = dwup.reshape(N_DEV, d, f8)
        grads["ffn_w_down", layer] = dwdown.reshape(N_DEV, fd, d)
        d_fconv[layer] = dcw.reshape(N_DEV, 3, f8)
        name = f"ffn_in_bwd{layer}"
        (dh, d_ffn_norm[layer]), r = proj_t_rms_bwd(dup2.reshape(N_DEV, seq, f8), weights["ffn_w_up", layer],
                                                    saved["hf", layer], ffn_norm[layer][None], dh, name,
                                                    rider=rs_rider(name))
        rs_done(name, r)
        hm, xm = saved["hm", layer], saved["xm", layer]
        if layer < N_A:
            u4, z = saved["mix", layer]
            name = f"mixer_bwd{layer}"
            (du3, dwin3, dwout, dcw), r = mixer_bwd(dh, weights["a_w_out", layer], u4, z, xm, a_cw[layer],
                                                    name, rider=rs_rider(name))
            rs_done(name, r)
            dwin = jnp.transpose(dwin3, (1, 0, 2)).reshape(d, N_DEV, 3 * d // N_DEV)
            grads["a_w_in", layer] = jnp.transpose(dwin, (1, 0, 2))
            grads["a_w_out", layer] = dwout.reshape(N_DEV, dshard, d)
            d_aconv[layer] = dcw
            name = f"mixer_in_bwd{layer}"
            (dh, d_a_gain[layer]), r = proj_t_rms_bwd(du3, weights["a_w_in", layer], hm, a_gain[layer][None],
                                                      dh, name, rider=rs_rider(name))
            rs_done(name, r)
        else:
            j = layer - N_A
            q, o, lse = saved["attn", layer]
            name = f"attn_out_bwd{j}"
            (do, dwo), r = o_bwd(dh, o, weights["b_w_o", j], name, rider=rs_rider(name))
            rs_done(name, r)
            grads["b_w_o", j] = dwo.reshape(N_DEV, dshard, d)
            name = f"attn_bwd{j}"
            (dq, dk, dv), r = attn_bwd(q, kv[0], kv[1], o, do, lse, name, rider=rs_rider(name))
            rs_done(name, r)
            dks.append(dk)
            dvs.append(dv)
            name = f"q_bwd{j}"
            (dqc, dwuq, dwdq, d_q_gain[j]), r = q_bwd(dq, xm, weights["b_w_dq", j], b_q_norm[j][None],
                                                      weights["b_w_uq", j], cos, sin, name, rider=rs_rider(name))
            rs_done(name, r)
            grads["b_w_uq", j] = dwuq[:, :, :NOPE + ROPE]
            grads["b_w_dq", j] = dwdq.reshape(N_DEV, dshard, rank)
            name = f"q_in_bwd{j}"
            (dh, d_b_gain[j]), r = proj_t_rms_bwd(dqc, weights["b_w_dq", j][None], hm, b_mix_norm[j][None],
                                                  dh, name, rider=rs_rider(name))
            rs_done(name, r)
    grad_x = dh.reshape(x.shape)

    full_small = [
        jnp.concatenate(d_a_gain, axis=0),
        jnp.stack(d_aconv),
        jnp.concatenate(d_b_gain, axis=0),
        jnp.concatenate(d_q_gain, axis=0),
        d_kvin_gain[0],
        d_kv_gain[0],
        jnp.concatenate(d_ffn_norm, axis=0),
        jnp.stack(d_fconv),
        d_final[0],
    ]
    full_shapes = [t.shape for t in full_small]
    small_pack = _pack(full_small)

    res = {}

    def update(name, n_layers, w, m, v, extra=(), transposed=False):
        view = (lambda t: jnp.transpose(t, (0, 2, 1))) if transposed else (lambda t: t)
        call = sum_adamw_transposed if transposed else sum_adamw
        shard = w.shape if w.ndim == 3 else (1,) + w.shape
        host = f"adamw_{name}"
        outs, r = call([parts[name, l] for l in range(n_layers)], view(w.reshape(shard)),
                       view(m.reshape(shard)), view(v.reshape(shard)), host,
                       rider=rs_rider(host) + list(extra))
        rs_done(host, r)
        res[name] = [view(t).reshape(w.shape) for t in outs]
        return r[len(BWD_RIDERS.get(host, [])):]

    (g_parts,) = update("a_w_out", N_A, a_w_out, m_a_w_out, v_a_w_out, extra=[("ag", small_pack)])
    update("ffn_w_down", DEPTH, ffn_w_down, m_ffn_w_down, v_ffn_w_down)
    update("ffn_w_up", DEPTH, ffn_w_up, m_ffn_w_up, v_ffn_w_up, transposed=True)
    update("b_w_dq", N_B, b_w_dq, m_b_w_dq, v_b_w_dq)
    update("b_w_uq", N_B, b_w_uq, m_b_w_uq, v_b_w_uq)
    update("b_w_o", N_B, b_w_o, m_b_w_o, v_b_w_o)
    update("w_dkv", 1, w_dkv, m_w_dkv, v_w_dkv)
    update("w_ukv", 1, w_ukv, m_w_ukv, v_w_ukv)
    update("a_w_in", N_A, a_w_in, m_a_w_in, v_a_w_in)

    summed = sum_slots(g_parts, "sum_small_grads")
    (s_a_gain, s_aconv_g, s_b_gain, s_q_gain, s_kvin, s_kvn, s_ffn_gain, s_fconv_g,
     s_final) = _unpack(summed, full_shapes)
    dsl = d // N_DEV
    small = [
        ("a_mix_norm", lax.dynamic_slice_in_dim(s_a_gain, me * dsl, dsl, axis=1), a_mix_norm, m_a_mix_norm, v_a_mix_norm),
        ("a_conv", lax.dynamic_slice_in_dim(s_aconv_g, me * dsl, dsl, axis=2), a_conv, m_a_conv, v_a_conv),
        ("b_mix_norm", s_b_gain, b_mix_norm, m_b_mix_norm, v_b_mix_norm),
        ("b_q_norm", s_q_gain, b_q_norm, m_b_q_norm, v_b_q_norm),
        ("kv_in_norm", s_kvin, kv_in_norm, m_kv_in_norm, v_kv_in_norm),
        ("kv_norm", s_kvn, kv_norm, m_kv_norm, v_kv_norm),
        ("ffn_norm", s_ffn_gain, ffn_norm, m_ffn_norm, v_ffn_norm),
        ("ffn_conv", lax.dynamic_index_in_dim(s_fconv_g, me, axis=1, keepdims=False), ffn_conv, m_ffn_conv, v_ffn_conv),
        ("final_norm", s_final, final_norm, m_final_norm, v_final_norm),
    ]
    shapes = [t[2].shape for t in small]
    packed = [_pack([t[k] for t in small])[None] for k in (1, 2, 3, 4)]
    outs, _ = sum_adamw([packed[0]], packed[1], packed[2], packed[3], "adamw_small")
    unpacked = [_unpack(t[0], shapes) for t in outs]
    for idx, t in enumerate(small):
        res[t[0]] = [unpacked[k][idx] for k in range(4)]

    order = ["a_mix_norm", "a_w_in", "a_conv", "a_w_out", "b_mix_norm", "b_w_dq", "b_q_norm",
             "b_w_uq", "b_w_o", "kv_in_norm", "w_dkv", "kv_norm", "w_ukv", "ffn_norm",
             "ffn_w_up", "ffn_conv", "ffn_w_down", "final_norm"]
    return (loss, grad_x, *[res[n][0] for n in order], *[res[n][1] for n in order],
            *[res[n][2] for n in order], *[res[n][3] for n in order])
```

```python
import functools

import jax
import jax.numpy as jnp
from jax import lax
from jax.experimental import pallas as pl
from jax.experimental.pallas import tpu as pltpu

F32 = jnp.float32
BF16 = jnp.bfloat16

N_DEV = 8
N_HEADS = 8
NOPE = 128
ROPE = 64
ROPE_PAD = 128
QK = NOPE + ROPE_PAD
VDIM = 128
KV_RANK = 256
ROPE_THETA = 10000.0
RMS_EPS = 1e-6
ATTN_SCALE = (NOPE + ROPE) ** -0.5
N_A = 2
N_B = 2
DEPTH = 4

ADAM_LR = 0.001
ADAM_B1 = 0.9
ADAM_B2 = 0.999
ADAM_EPS = 1e-08
ADAM_WD = 0.01
ADAM_STEP = 10

V7X_VMEM_LIMIT = 56 * 1024 * 1024
BF16_SUBLANES = 16
ROW_TILE = 512
ROW_TILE_LARGE = 1024
ADAM_ROWS = 256
ATTN_TILE = 512
MIXER_CHUNK = 512
LANES = 128
NEG_BIG = -1e30
COPIES_PER_TASK = 7

MESH_ID = pl.DeviceIdType.MESH
ANY = pl.BlockSpec(memory_space=pl.ANY)


def _nt(a, b):
    return lax.dot_general(a, b, (((1,), (1,)), ((), ())), preferred_element_type=F32)


def _tn(a, b):
    return lax.dot_general(a, b, (((0,), (0,)), ((), ())), preferred_element_type=F32)


def _nn(a, b):
    return jnp.dot(a, b, preferred_element_type=F32)


def _rms(h, g):
    rstd = lax.rsqrt(jnp.mean(h * h, axis=-1, keepdims=True) + RMS_EPS)
    xhat = h * rstd
    return xhat * g, xhat, rstd


def _rms_bwd(dxn, xhat, rstd, g):
    dxhat = dxn * g
    dh = rstd * (dxhat - xhat * jnp.mean(dxhat * xhat, axis=-1, keepdims=True))
    return dh, dxn * xhat


def _shift_down(x, k, halo_rows):
    r = pltpu.roll(x, k, 0)
    row = lax.broadcasted_iota(jnp.int32, x.shape, 0)
    for t in range(k):
        r = jnp.where(row == t, halo_rows[t], r)
    return r


def _shift_up(x, k, halo_rows):
    n = x.shape[0]
    r = pltpu.roll(x, n - k, 0)
    row = lax.broadcasted_iota(jnp.int32, x.shape, 0)
    for t in range(k):
        r = jnp.where(row == n - k + t, halo_rows[t], r)
    return r


def _conv_taps(w_ref):
    return w_ref[0:1, :], w_ref[1:2, :], w_ref[2:3, :]


def _rope_swap(x):
    lane = lax.broadcasted_iota(jnp.int32, x.shape, 1)
    return jnp.where(lane < ROPE // 2, pltpu.roll(x, ROPE_PAD - ROPE // 2, 1),
                     pltpu.roll(x, ROPE // 2, 1))


def _rope_fwd(x, cos, sin):
    return x * cos + _rope_swap(x) * sin


def _rope_bwd(dy, cos, sin):
    return dy * cos - _rope_swap(dy) * sin


def _rope_tables(seq):
    inv = 1.0 / (ROPE_THETA ** (jnp.arange(0, ROPE, 2, dtype=F32) / ROPE))
    ang = jnp.arange(seq, dtype=F32)[:, None] * inv[None, :]
    cos, sin = jnp.cos(ang), jnp.sin(ang)
    zero = jnp.zeros((seq, ROPE_PAD - ROPE), F32)
    return (jnp.concatenate([cos, cos, zero], axis=1),
            jnp.concatenate([-sin, sin, zero], axis=1))


def _row_tile(rows, cap, mult=8):
    best = None
    for t in range(mult, min(rows, cap) + 1, mult):
        if rows % t == 0:
            best = t
    return rows if best is None else best


class _AllGatherTask:
    def __init__(self, t, x_ref, out_ref, send_sems, recv_sems, local_sems):
        self.t, self.x_ref, self.out_ref = t, x_ref, out_ref
        self.send_sems, self.recv_sems, self.local_sems = send_sems, recv_sems, local_sems
        mx, my, mc = lax.axis_index("x"), lax.axis_index("y"), lax.axis_index("c")
        self.mc = mc
        self.me, self.sibling = (mx, my, mc), (mx, my, 1 - mc)
        self.chips = [(1 - mx, my), (mx, 1 - my), (1 - mx, 1 - my)]

    def _slot(self, px, py, pc):
        return self.out_ref.at[4 * px + 2 * py + pc]

    def _copy(self, k, block, to, src=None):
        s = COPIES_PER_TASK * self.t + k
        return pltpu.make_async_remote_copy(
            src_ref=self._slot(*block) if src is None else src, dst_ref=self._slot(*block),
            send_sem=self.send_sems.at[s], recv_sem=self.recv_sems.at[s],
            device_id=to, device_id_type=MESH_ID)

    def _mine(self):
        return pltpu.make_async_copy(self.x_ref, self._slot(*self.me), self.local_sems.at[self.t])

    def _first(self):
        out = [self._copy(0, self.me, self.sibling, src=self.x_ref)]
        out += [self._copy(1 + j, self.me, (*chip, self.mc), src=self.x_ref)
                for j, chip in enumerate(self.chips)]
        return out

    def _passed(self):
        return [self._copy(4 + j, (*chip, self.mc), self.sibling) for j, chip in enumerate(self.chips)]

    def start(self):
        self._mine().start()
        for cp in self._first():
            cp.start()

    def forward(self):
        passed = self._passed()
        for j, chip in enumerate(self.chips):
            self._copy(1 + j, (*chip, self.mc), self.me).wait_recv()
            passed[j].start()

    def finish(self):
        self._copy(0, self.sibling, self.me).wait_recv()
        for j, chip in enumerate(self.chips):
            self._copy(4 + j, (*chip, 1 - self.mc), self.me).wait_recv()
        for cp in self._first() + self._passed():
            cp.wait_send()
        self._mine().wait()


class _ReduceScatterTask:
    def __init__(self, t, g_ref, out_ref, send_sems, recv_sems, local_sems):
        self.t, self.g_ref, self.out_ref = t, g_ref, out_ref
        self.send_sems, self.recv_sems, self.local_sems = send_sems, recv_sems, local_sems
        mx, my, mc = lax.axis_index("x"), lax.axis_index("y"), lax.axis_index("c")
        self.me = 4 * mx + 2 * my + mc
        self.peers = []
        for k in range(1, N_DEV):
            px, py, pc = mx ^ ((k >> 2) & 1), my ^ ((k >> 1) & 1), mc ^ (k & 1)
            self.peers.append(((px, py, pc), 4 * px + 2 * py + pc))

    def _mine(self):
        return pltpu.make_async_copy(self.g_ref.at[self.me], self.out_ref.at[self.me],
                                     self.local_sems.at[self.t])

    def _copy(self, k, src_slot, dst_slot):
        s = COPIES_PER_TASK * self.t + k
        return pltpu.make_async_remote_copy(
            src_ref=self.g_ref.at[src_slot], dst_ref=self.out_ref.at[dst_slot],
            send_sem=self.send_sems.at[s], recv_sem=self.recv_sems.at[s],
            device_id=self.peers[k][0], device_id_type=MESH_ID)

    def start(self):
        self._mine().start()
        for k, (_, peer) in enumerate(self.peers):
            self._copy(k, peer, self.me).start()

    def forward(self):
        pass

    def finish(self):
        for k, (_, peer) in enumerate(self.peers):
            self._copy(k, self.me, peer).wait_recv()
        for k, (_, peer) in enumerate(self.peers):
            self._copy(k, peer, self.me).wait_send()
        self._mine().wait()


class _PairExchangeTask:
    def __init__(self, t, g_ref, out_ref, send_sems, recv_sems, local_sems):
        mx, my, mc = lax.axis_index("x"), lax.axis_index("y"), lax.axis_index("c")
        s = COPIES_PER_TASK * t
        self.copy = pltpu.make_async_remote_copy(
            src_ref=g_ref.at[:, 1 - mc], dst_ref=out_ref,
            send_sem=send_sems.at[s], recv_sem=recv_sems.at[s],
            device_id=(mx, my, 1 - mc), device_id_type=MESH_ID)

    def start(self):
        self.copy.start()

    def forward(self):
        pass

    def finish(self):
        self.copy.wait()


class _ChipScatterTask:
    def __init__(self, t, s_ref, out_ref, send_sems, recv_sems, local_sems):
        self.t, self.s_ref, self.out_ref = t, s_ref, out_ref
        self.send_sems, self.recv_sems, self.local_sems = send_sems, recv_sems, local_sems
        mx, my, mc = lax.axis_index("x"), lax.axis_index("y"), lax.axis_index("c")
        self.chip = 2 * mx + my
        self.peers = []
        for k in range(1, N_DEV // 2):
            px, py = mx ^ ((k >> 1) & 1), my ^ (k & 1)
            self.peers.append(((px, py, mc), 2 * px + py))

    def _mine(self):
        return pltpu.make_async_copy(self.s_ref.at[self.chip], self.out_ref.at[self.chip],
                                     self.local_sems.at[self.t])

    def _copy(self, k, src_slot, dst_slot):
        s = COPIES_PER_TASK * self.t + k
        return pltpu.make_async_remote_copy(
            src_ref=self.s_ref.at[src_slot], dst_ref=self.out_ref.at[dst_slot],
            send_sem=self.send_sems.at[s], recv_sem=self.recv_sems.at[s],
            device_id=self.peers[k][0], device_id_type=MESH_ID)

    def start(self):
        self._mine().start()
        for k, (_, peer) in enumerate(self.peers):
            self._copy(k, peer, self.chip).start()

    def forward(self):
        pass

    def finish(self):
        for k, (_, peer) in enumerate(self.peers):
            self._copy(k, self.chip, peer).wait_recv()
        for k, (_, peer) in enumerate(self.peers):
            self._copy(k, peer, self.chip).wait_send()
        self._mine().wait()


_TASKS = {"ag": _AllGatherTask, "rs": _ReduceScatterTask, "rs_pair": _PairExchangeTask,
          "rs_chip": _ChipScatterTask}


def _task_shape(kind, arr):
    shape = {"ag": (N_DEV,) + arr.shape, "rs": arr.shape, "rs_chip": arr.shape,
             "rs_pair": arr.shape[:1] + arr.shape[2:]}[kind]
    return jax.ShapeDtypeStruct(shape, arr.dtype)


def _sem_shapes(n_tasks):
    return [pltpu.SemaphoreType.DMA((COPIES_PER_TASK * n_tasks,)),
            pltpu.SemaphoreType.DMA((COPIES_PER_TASK * n_tasks,)),
            pltpu.SemaphoreType.DMA((n_tasks,))]


def _make_tasks(rider, in_refs, out_refs, sems):
    return [_TASKS[kind](t, in_refs[t], out_refs[t], *sems) for t, (kind, _) in enumerate(rider)]


def exchange(rider, name):
    n = len(rider)

    def body(*refs):
        tasks = _make_tasks(rider, refs[:n], refs[n:2 * n], refs[2 * n:])
        for task in tasks:
            task.start()
        for task in tasks:
            task.forward()
        for task in tasks:
            task.finish()

    return list(pl.pallas_call(
        body, name=name, out_shape=tuple(_task_shape(k, a) for k, a in rider),
        in_specs=[ANY] * n, out_specs=(ANY,) * n, scratch_shapes=_sem_shapes(n),
    )(*[a for _, a in rider]))


def _call(body, name, grid, in_specs, out_specs, out_shape, args, scratch=(), rider=()):
    in_specs, out_specs, out_shape = list(in_specs), tuple(out_specs), tuple(out_shape)
    n_in, n_out, n_scr, n_r = len(in_specs), len(out_specs), len(scratch), len(rider)
    if n_r:
        def kern(*refs):
            ins, r_in = refs[:n_in], refs[n_in:n_in + n_r]
            o0 = n_in + n_r
            outs, r_out = refs[o0:o0 + n_out], refs[o0 + n_out:o0 + n_out + n_r]
            s0 = o0 + n_out + n_r
            scr, sems = refs[s0:s0 + n_scr], refs[s0 + n_scr:]
            step = 0
            for a, n in enumerate(grid):
                step = step * n + pl.program_id(a)
            n_steps = 1
            for n in grid:
                n_steps *= n

            @pl.when(step == 0)
            def _():
                for task in _make_tasks(rider, r_in, r_out, sems):
                    task.start()
            body(*ins, *outs, *scr)

            @pl.when(step == n_steps - 1)
            def _():
                tasks = _make_tasks(rider, r_in, r_out, sems)
                for task in tasks:
                    task.forward()
                for task in tasks:
                    task.finish()
    else:
        kern = body
    res = pl.pallas_call(
        kern, name=name, grid=grid,
        in_specs=in_specs + [ANY] * n_r, out_specs=out_specs + (ANY,) * n_r,
        out_shape=out_shape + tuple(_task_shape(k, a) for k, a in rider),
        scratch_shapes=list(scratch) + (_sem_shapes(n_r) if n_r else []),
        compiler_params=pltpu.CompilerParams(dimension_semantics=("arbitrary",) * len(grid),
                                             vmem_limit_bytes=V7X_VMEM_LIMIT),
    )(*args, *[a for _, a in rider])
    return list(res[:n_out]), list(res[n_out:])


def _adamw(g, w, m, v):
    m = ADAM_B1 * m + (1.0 - ADAM_B1) * g
    v = ADAM_B2 * v + (1.0 - ADAM_B2) * (g * g)
    m_hat = m / (1.0 - ADAM_B1 ** ADAM_STEP)
    v_hat = v / (1.0 - ADAM_B2 ** ADAM_STEP)
    delta = -ADAM_LR * (m_hat / (jnp.sqrt(v_hat) + ADAM_EPS) + ADAM_WD * w)
    return delta, m, v


def sum_adamw(parts, w, m, v, name, rider=()):
    n_l, rows, cols = w.shape
    mult = BF16_SUBLANES if parts[0].dtype == BF16 else 8
    tr = _row_tile(rows, ADAM_ROWS, mult)
    n_i = rows // tr

    def body(*refs):
        part_refs = refs[:n_l]
        w_ref, m_ref, v_ref, g_out, d_out, m_out, v_out = refs[n_l:]
        layer = pl.program_id(0)
        for k in range(n_l):
            @pl.when(layer == k)
            def _(k=k):
                g = part_refs[k][0].astype(F32)
                for s in range(1, parts[k].shape[0]):
                    g = g + part_refs[k][s].astype(F32)
                delta, m_new, v_new = _adamw(g, w_ref[...], m_ref[...], v_ref[...])
                g_out[...] = g
                d_out[...] = delta
                m_out[...] = m_new
                v_out[...] = v_new

    part_specs = [pl.BlockSpec((parts[k].shape[0], tr, cols), functools.partial(
        lambda l, i, k: (0, jnp.where(l == k, i, 0), 0), k=k)) for k in range(n_l)]
    wspec = pl.BlockSpec((None, tr, cols), lambda l, i: (l, i, 0))
    shape = jax.ShapeDtypeStruct(w.shape, F32)
    return _call(body, name, (n_l, n_i), part_specs + [wspec] * 3, (wspec,) * 4, (shape,) * 4,
                 (*parts, w, m, v), rider=rider)


def sum_adamw_transposed(parts, w_t, m_t, v_t, name, rider=()):
    n_l, cols, rows = w_t.shape
    tr = LANES
    n_i = rows // tr
    starts = list(range(0, cols - LANES + 1, LANES))
    if starts[-1] + LANES < cols:
        starts.append(cols - LANES)

    def body(*refs):
        part_refs = refs[:n_l]
        w_ref, m_ref, v_ref, g_out, d_out, m_out, v_out = refs[n_l:]
        layer = pl.program_id(0)
        for k in range(n_l):
            @pl.when(layer == k)
            def _(k=k):
                for c0 in starts:
                    piece = pl.ds(c0, LANES)
                    g = part_refs[k][0, :, piece].astype(F32)
                    for s in range(1, parts[k].shape[0]):
                        g = g + part_refs[k][s, :, piece].astype(F32)
                    g = g.T
                    delta, m_new, v_new = _adamw(g, w_ref[piece, :], m_ref[piece, :], v_ref[piece, :])
                    g_out[piece, :] = g
                    d_out[piece, :] = delta
                    m_out[piece, :] = m_new
                    v_out[piece, :] = v_new

    part_specs = [pl.BlockSpec((parts[k].shape[0], tr, cols), functools.partial(
        lambda l, i, k: (0, jnp.where(l == k, i, 0), 0), k=k)) for k in range(n_l)]
    wspec = pl.BlockSpec((None, cols, tr), lambda l, i: (l, 0, i))
    shape = jax.ShapeDtypeStruct(w_t.shape, F32)
    return _call(body, name, (n_l, n_i), part_specs + [wspec] * 3, (wspec,) * 4, (shape,) * 4,
                 (*parts, w_t, m_t, v_t), rider=rider)


def pair_sum(g4, other, name):
    n_chip, _, rows, cols = g4.shape
    tr = _row_tile(rows, 256, BF16_SUBLANES)

    def body(g_ref, o_ref, s_ref):
        mine = g_ref[lax.axis_index("c")]
        s_ref[...] = (mine.astype(F32) + o_ref[...].astype(F32)).astype(BF16)

    blk = pl.BlockSpec((None, tr, cols), lambda k, i: (k, i, 0))
    return _call(body, name, (n_chip, rows // tr),
                 [pl.BlockSpec((None, 2, tr, cols), lambda k, i: (k, 0, i, 0)), blk], [blk],
                 [jax.ShapeDtypeStruct((n_chip, rows, cols), g4.dtype)], (g4, other))[0][0]


def sum_slots(parts, name):
    n, rows, cols = parts.shape

    def body(p_ref, o_ref):
        acc = p_ref[0]
        for s in range(1, n):
            acc = acc + p_ref[s]
        o_ref[...] = acc

    return pl.pallas_call(
        body, name=name, out_shape=jax.ShapeDtypeStruct((rows, cols), F32),
        in_specs=[pl.BlockSpec(memory_space=pltpu.VMEM)],
        out_specs=pl.BlockSpec(memory_space=pltpu.VMEM),
    )(parts)


def norm_fwd(h, g, name):
    seq, d = h.shape
    tm = min(ROW_TILE, seq)

    def body(h_ref, g_ref, o_ref):
        o_ref[...] = _rms(h_ref[...], g_ref[...])[0].astype(BF16)

    return _call(body, name, (seq // tm,),
                 [pl.BlockSpec((tm, d), lambda i: (i, 0)), pl.BlockSpec((1, d), lambda i: (0, 0))],
                 [pl.BlockSpec((tm, d), lambda i: (i, 0))],
                 [jax.ShapeDtypeStruct((seq, d), BF16)], (h, g))[0][0]


def proj_residual(a, w, res, name, g_next=None, rider=()):
    nb, seq, kb = a.shape
    d = w.shape[-1]
    tm = min(ROW_TILE, seq)
    with_norm = g_next is not None

    def body(a_ref, w_ref, r_ref, *rest):
        acc = r_ref[...]
        for b in range(nb):
            acc = acc + _nn(a_ref[b], w_ref[b])
        if with_norm:
            g_ref, o_ref, xn_ref = rest
            xn_ref[...] = _rms(acc, g_ref[...])[0].astype(BF16)
        else:
            (o_ref,) = rest
        o_ref[...] = acc

    row = pl.BlockSpec((tm, d), lambda i: (i, 0))
    in_specs = [pl.BlockSpec((nb, tm, kb), lambda i: (0, i, 0)),
                pl.BlockSpec((nb, kb, d), lambda i: (0, 0, 0)), row]
    args = [a, w, res]
    out_specs, out_shape = [row], [jax.ShapeDtypeStruct((seq, d), F32)]
    if with_norm:
        in_specs.append(pl.BlockSpec((1, d), lambda i: (0, 0)))
        args.append(g_next)
        out_specs.append(row)
        out_shape.append(jax.ShapeDtypeStruct((seq, d), BF16))
    outs, r_outs = _call(body, name, (seq // tm,), in_specs, out_specs, out_shape, args, rider=rider)
    return (outs[0], outs[1] if with_norm else None), r_outs


def proj_t_rms_bwd(du, w, h, g, dres, name, rider=()):
    nb, seq, wd = du.shape
    k = w.shape[1]
    tm = min(ROW_TILE, seq)

    def body(du_ref, w_ref, h_ref, g_ref, dr_ref, dh_ref, dg_ref):
        i = pl.program_id(0)
        dxn = _nt(du_ref[0], w_ref[0])
        for b in range(1, nb):
            dxn = dxn + _nt(du_ref[b], w_ref[b])
        _, xhat, rstd = _rms(h_ref[...], g_ref[...])
        dh, dg_rows = _rms_bwd(dxn, xhat, rstd, g_ref[...])
        dh_ref[...] = dr_ref[...] + dh

        @pl.when(i == 0)
        def _():
            dg_ref[...] = jnp.zeros_like(dg_ref)
        dg_ref[...] += jnp.sum(dg_rows, axis=0, keepdims=True)

    row = pl.BlockSpec((tm, k), lambda i: (i, 0))
    vec = pl.BlockSpec((1, k), lambda i: (0, 0))
    return _call(body, name, (seq // tm,),
                 [pl.BlockSpec((nb, tm, wd), lambda i: (0, i, 0)),
                  pl.BlockSpec((nb, k, wd), lambda i: (0, 0, 0), pipeline_mode=pl.Buffered(1)),
                  row, vec, row],
                 (row, vec),
                 (jax.ShapeDtypeStruct((seq, k), F32), jax.ShapeDtypeStruct((1, k), F32)),
                 (du, w, h, g, dres), rider=rider)


def mixer_fwd(xn, win3, cw, name, rider=()):
    seq, d = xn.shape
    tm = min(ROW_TILE_LARGE, seq)
    cc = min(MIXER_CHUNK, d)
    n_c, n_i = d // cc, seq // tm

    def body(x_ref, w_ref, cw_ref, u_ref, z_ref, carry):
        i = pl.program_id(1)

        @pl.when(i == 0)
        def _():
            carry[...] = jnp.zeros_like(carry)
        xb = x_ref[...]
        b = _nn(xb, w_ref[0])
        c = _nn(xb, w_ref[1])
        hh = _nn(xb, w_ref[2])
        p = c * hh
        w0, w1, w2 = _conv_taps(cw_ref)
        p1 = _shift_down(p, 1, [carry[7:8, :]])
        p2 = _shift_down(p, 2, [carry[6:7, :], carry[7:8, :]])
        q = w0 * p2 + w1 * p1 + w2 * p
        carry[...] = p[tm - 8:tm, :]
        u_ref[0] = b.astype(BF16)
        u_ref[1] = c.astype(BF16)
        u_ref[2] = hh.astype(BF16)
        u_ref[3] = q.astype(BF16)
        z_ref[...] = (b * q).astype(BF16)

    return _call(body, name, (n_c, n_i),
                 [pl.BlockSpec((tm, d), lambda c, i: (i, 0)),
                  pl.BlockSpec((3, d, cc), lambda c, i: (0, 0, c)),
                  pl.BlockSpec((3, cc), lambda c, i: (0, c))],
                 (pl.BlockSpec((4, tm, cc), lambda c, i: (0, i, c)),
                  pl.BlockSpec((tm, cc), lambda c, i: (i, c))),
                 (jax.ShapeDtypeStruct((4, seq, d), BF16), jax.ShapeDtypeStruct((seq, d), BF16)),
                 (xn, win3, cw), scratch=[pltpu.VMEM((8, cc), F32)], rider=rider)


def mixer_bwd(dh, wout, u4, z, xn, cw, name, rider=()):
    seq, d = xn.shape
    tm = min(ROW_TILE, seq)
    cc = min(MIXER_CHUNK, d)
    n_c, n_i = d // cc, seq // tm

    def body(dh_ref, wout_ref, u_ref, z_ref, x_ref, cw_ref,
             du_ref, dwin_ref, dwout_ref, dcw_ref, acc_in, acc_out, acc_cw, carry):
        i = pl.program_id(1)

        @pl.when(i == 0)
        def _():
            acc_in[...] = jnp.zeros_like(acc_in)
            acc_out[...] = jnp.zeros_like(acc_out)
            acc_cw[...] = jnp.zeros_like(acc_cw)
            carry[...] = jnp.zeros_like(carry)
        dhb = dh_ref[...].astype(BF16)
        dz = _nt(dhb, wout_ref[...])
        acc_out[...] += _tn(z_ref[...], dhb)
        b = u_ref[0].astype(F32)
        c = u_ref[1].astype(F32)
        hh = u_ref[2].astype(F32)
        q = u_ref[3].astype(F32)
        p = c * hh
        db = dz * q
        dq = dz * b
        w0, w1, w2 = _conv_taps(cw_ref)
        dq1 = _shift_up(dq, 1, [carry[0:1, :]])
        dq2 = _shift_up(dq, 2, [carry[0:1, :], carry[1:2, :]])
        dp = w2 * dq + w1 * dq1 + w0 * dq2
        carry[...] = dq[0:8, :]
        acc_cw[0:1, :] += jnp.sum(dq2 * p, axis=0, keepdims=True)
        acc_cw[1:2, :] += jnp.sum(dq1 * p, axis=0, keepdims=True)
        acc_cw[2:3, :] += jnp.sum(dq * p, axis=0, keepdims=True)
        dbb = db.astype(BF16)
        dcb = (dp * hh).astype(BF16)
        dhhb = (dp * c).astype(BF16)
        du_ref[0] = dbb
        du_ref[1] = dcb
        du_ref[2] = dhhb
        xb = x_ref[...]
        acc_in[0] += _tn(xb, dbb)
        acc_in[1] += _tn(xb, dcb)
        acc_in[2] += _tn(xb, dhhb)

        @pl.when(i == n_i - 1)
        def _():
            dwin_ref[...] = acc_in[...].astype(BF16)
            dwout_ref[...] = acc_out[...].astype(BF16)
            dcw_ref[...] = acc_cw[0:3, :]

    rev = lambda c, i: (n_i - 1 - i, 0)
    return _call(body, name, (n_c, n_i),
                 [pl.BlockSpec((tm, d), rev),
                  pl.BlockSpec((cc, d), lambda c, i: (c, 0)),
                  pl.BlockSpec((4, tm, cc), lambda c, i: (0, n_i - 1 - i, c)),
                  pl.BlockSpec((tm, cc), lambda c, i: (n_i - 1 - i, c)),
                  pl.BlockSpec((tm, d), rev),
                  pl.BlockSpec((3, cc), lambda c, i: (0, c))],
                 (pl.BlockSpec((3, tm, cc), lambda c, i: (0, n_i - 1 - i, c)),
                  pl.BlockSpec((3, d, cc), lambda c, i: (0, 0, c)),
                  pl.BlockSpec((cc, d), lambda c, i: (c, 0)),
                  pl.BlockSpec((3, cc), lambda c, i: (0, c))),
                 (jax.ShapeDtypeStruct((3, seq, d), BF16), jax.ShapeDtypeStruct((3, d, d), BF16),
                  jax.ShapeDtypeStruct((d, d), BF16), jax.ShapeDtypeStruct((3, d), F32)),
                 (dh, wout, u4, z, xn, cw),
                 scratch=[pltpu.VMEM((3, d, cc), F32), pltpu.VMEM((cc, d), F32),
                          pltpu.VMEM((8, cc), F32), pltpu.VMEM((8, cc), F32)], rider=rider)


def _silu_parts(cg):
    sg = 1.0 / (1.0 + jnp.exp(-cg))
    return sg, cg * sg


def ffn_fwd(xn, wup, fcw, name, rider=()):
    seq, d = xn.shape
    f8 = wup.shape[-1]
    half = N_DEV // 2
    tm = min(ROW_TILE_LARGE, seq)
    n_i = seq // tm

    def body(x_ref, wg_ref, wu_ref, cg_ref, cu_ref, up_ref, cv_ref, a_ref, carry):
        i = pl.program_id(1)

        @pl.when(i == 0)
        def _():
            carry[...] = jnp.zeros_like(carry)
        xb = x_ref[...]
        conv = []
        for s, (w_ref, t_ref) in enumerate(((wg_ref, cg_ref), (wu_ref, cu_ref))):
            u = _nn(xb, w_ref[...])
            up_ref[s] = u.astype(BF16)
            w0, w1, w2 = _conv_taps(t_ref)
            u1 = _shift_down(u, 1, [carry[s, 7:8, :]])
            u2 = _shift_down(u, 2, [carry[s, 6:7, :], carry[s, 7:8, :]])
            cv = w0 * u2 + w1 * u1 + w2 * u
            cv_ref[s] = cv.astype(BF16)
            conv.append(cv)
            carry[s] = u[tm - 8:tm, :]
        _, silu = _silu_parts(conv[0])
        a_ref[...] = (silu * conv[1]).astype(BF16)

    blk = pl.BlockSpec((2, None, tm, f8), lambda c, i: (0, c, i, 0))
    big = jax.ShapeDtypeStruct((2, half, seq, f8), BF16)
    return _call(body, name, (half, n_i),
                 [pl.BlockSpec((tm, d), lambda c, i: (i, 0)),
                  pl.BlockSpec((None, d, f8), lambda c, i: (c, 0, 0)),
                  pl.BlockSpec((None, d, f8), lambda c, i: (c + half, 0, 0)),
                  pl.BlockSpec((None, 3, f8), lambda c, i: (c, 0, 0)),
                  pl.BlockSpec((None, 3, f8), lambda c, i: (c + half, 0, 0))],
                 (blk, blk, pl.BlockSpec((None, tm, f8), lambda c, i: (c, i, 0))),
                 (big, big, jax.ShapeDtypeStruct((half, seq, f8), BF16)),
                 (xn, wup, wup, fcw, fcw), scratch=[pltpu.VMEM((2, 8, f8), F32)], rider=rider)


def ffn_bwd(dh, wdown, up2, cv2, act, xn, fcw, name, rider=()):
    seq, d = xn.shape
    f8 = up2.shape[-1]
    fb = wdown.shape[1]
    half = N_DEV // 2
    tm = min(ROW_TILE, seq)
    n_i = seq // tm

    def body(dh_ref, wd_ref, up_ref, cv_ref, a_ref, x_ref, cg_ref, cu_ref,
             dup_ref, dwup_ref, dwd_ref, dcw_ref, acc_up, acc_down, acc_cw, carry):
        i = pl.program_id(1)

        @pl.when(i == 0)
        def _():
            acc_up[...] = jnp.zeros_like(acc_up)
            acc_down[...] = jnp.zeros_like(acc_down)
            acc_cw[...] = jnp.zeros_like(acc_cw)
            carry[...] = jnp.zeros_like(carry)
        dhb = dh_ref[...].astype(BF16)
        da = _nt(dhb, wd_ref[...])
        acc_down[...] += _tn(a_ref[...], dhb)
        cg = cv_ref[0].astype(F32)
        cu = cv_ref[1].astype(F32)
        sg, silu = _silu_parts(cg)
        dcg = da * cu * (sg + silu * (1.0 - sg))
        dcu = da * silu
        xb = x_ref[...]
        for s, (dc, t_ref) in enumerate(((dcg, cg_ref), (dcu, cu_ref))):
            w0, w1, w2 = _conv_taps(t_ref)
            d1 = _shift_up(dc, 1, [carry[s, 0:1, :]])
            d2 = _shift_up(dc, 2, [carry[s, 0:1, :], carry[s, 1:2, :]])
            du = (w2 * dc + w1 * d1 + w0 * d2).astype(BF16)
            carry[s] = dc[0:8, :]
            u = up_ref[s].astype(F32)
            acc_cw[s, 0:1, :] += jnp.sum(d2 * u, axis=0, keepdims=True)
            acc_cw[s, 1:2, :] += jnp.sum(d1 * u, axis=0, keepdims=True)
            acc_cw[s, 2:3, :] += jnp.sum(dc * u, axis=0, keepdims=True)
            dup_ref[s] = du
            acc_up[s] += _tn(xb, du)

        @pl.when(i == n_i - 1)
        def _():
            dwup_ref[...] = acc_up[...].astype(BF16)
            dwd_ref[...] = acc_down[...].astype(BF16)
            dcw_ref[...] = acc_cw[:, 0:3, :]

    rev = lambda c, i: (n_i - 1 - i, 0)
    blk = pl.BlockSpec((2, None, tm, f8), lambda c, i: (0, c, n_i - 1 - i, 0))
    return _call(body, name, (half, n_i),
                 [pl.BlockSpec((tm, d), rev),
                  pl.BlockSpec((None, fb, d), lambda c, i: (c, 0, 0)),
                  blk, blk,
                  pl.BlockSpec((None, tm, f8), lambda c, i: (c, n_i - 1 - i, 0)),
                  pl.BlockSpec((tm, d), rev),
                  pl.BlockSpec((None, 3, f8), lambda c, i: (c, 0, 0)),
                  pl.BlockSpec((None, 3, f8), lambda c, i: (c + half, 0, 0))],
                 (blk,
                  pl.BlockSpec((2, None, d, f8), lambda c, i: (0, c, 0, 0)),
                  pl.BlockSpec((None, fb, d), lambda c, i: (c, 0, 0)),
                  pl.BlockSpec((2, None, 3, f8), lambda c, i: (0, c, 0, 0))),
                 (jax.ShapeDtypeStruct((2, half, seq, f8), BF16),
                  jax.ShapeDtypeStruct((2, half, d, f8), BF16),
                  jax.ShapeDtypeStruct((half, fb, d), BF16),
                  jax.ShapeDtypeStruct((2, half, 3, f8), F32)),
                 (dh, wdown, up2, cv2, act, xn, fcw, fcw),
                 scratch=[pltpu.VMEM((2, d, f8), F32), pltpu.VMEM((fb, d), F32),
                          pltpu.VMEM((2, 8, f8), F32), pltpu.VMEM((2, 8, f8), F32)], rider=rider)


def q_fwd(xn, wdq, gq, wuq, cos, sin, name, rider=()):
    seq, d = xn.shape
    rank = wdq.shape[-1]
    tm = min(ROW_TILE, seq)

    def body(x_ref, wdq_ref, gq_ref, wuq_ref, cos_ref, sin_ref, q_ref):
        qc = _nn(x_ref[...], wdq_ref[...])
        qn = _rms(qc, gq_ref[...])[0].astype(BF16)
        for hd in range(N_HEADS):
            qh = _nn(qn, wuq_ref[hd])
            qr = _rope_fwd(qh[:, NOPE:QK], cos_ref[...], sin_ref[...])
            q_ref[hd, :, 0:NOPE] = (qh[:, 0:NOPE] * ATTN_SCALE).astype(BF16)
            q_ref[hd, :, NOPE:QK] = (qr * ATTN_SCALE).astype(BF16)

    rope = pl.BlockSpec((tm, ROPE_PAD), lambda i: (i, 0))
    return _call(body, name, (seq // tm,),
                 [pl.BlockSpec((tm, d), lambda i: (i, 0)),
                  pl.BlockSpec((d, rank), lambda i: (0, 0)),
                  pl.BlockSpec((1, rank), lambda i: (0, 0)),
                  pl.BlockSpec((N_HEADS, rank, QK), lambda i: (0, 0, 0)), rope, rope],
                 [pl.BlockSpec((N_HEADS, tm, QK), lambda i: (0, i, 0))],
                 [jax.ShapeDtypeStruct((N_HEADS, seq, QK), BF16)],
                 (xn, wdq, gq, wuq, cos, sin), rider=rider)


def q_bwd(dq, xn, h, g, dres, wdq, gq, wuq, cos, sin, name, rider=()):
    seq, d = xn.shape
    rank = wdq.shape[-1]
    tm = min(ROW_TILE, seq)
    n_i = seq // tm

    def body(dq_ref, x_ref, h_ref, g_ref, dr_ref, wdq_ref, gq_ref, wuq_ref, cos_ref, sin_ref,
             dh_ref, dwuq_ref, dwdq_ref, dgq_ref, dg_ref, acc_uq, acc_dq):
        i = pl.program_id(0)

        @pl.when(i == 0)
        def _():
            acc_uq[...] = jnp.zeros_like(acc_uq)
            acc_dq[...] = jnp.zeros_like(acc_dq)
            dgq_ref[...] = jnp.zeros_like(dgq_ref)
            dg_ref[...] = jnp.zeros_like(dg_ref)
        xb = x_ref[...]
        qc = _nn(xb, wdq_ref[...])
        qn, qhat, qrstd = _rms(qc, gq_ref[...])
        qnb = qn.astype(BF16)
        dqn = jnp.zeros((tm, rank), F32)
        for hd in range(N_HEADS):
            dnope = (dq_ref[hd, :, 0:NOPE].astype(F32) * ATTN_SCALE).astype(BF16)
            drope = _rope_bwd(dq_ref[hd, :, NOPE:QK].astype(F32) * ATTN_SCALE, cos_ref[...], sin_ref[...])
            draw = jnp.concatenate([dnope, drope.astype(BF16)], axis=1)
            dqn = dqn + _nt(draw, wuq_ref[hd])
            acc_uq[hd] += _tn(qnb, draw)
        dqc, dg_rows = _rms_bwd(dqn, qhat, qrstd, gq_ref[...])
        dgq_ref[...] += jnp.sum(dg_rows, axis=0, keepdims=True)
        dqcb = dqc.astype(BF16)
        acc_dq[...] += _tn(xb, dqcb)
        _, xhat, rstd = _rms(h_ref[...], g_ref[...])
        dh, dg_rows = _rms_bwd(_nt(dqcb, wdq_ref[...]), xhat, rstd, g_ref[...])
        dh_ref[...] = dr_ref[...] + dh
        dg_ref[...] += jnp.sum(dg_rows, axis=0, keepdims=True)

        @pl.when(i == n_i - 1)
        def _():
            dwuq_ref[...] = acc_uq[...].astype(BF16)
            dwdq_ref[...] = acc_dq[...].astype(BF16)

    rope = pl.BlockSpec((tm, ROPE_PAD), lambda i: (i, 0))
    row = pl.BlockSpec((tm, d), lambda i: (i, 0))
    vec = pl.BlockSpec((1, d), lambda i: (0, 0))
    return _call(body, name, (n_i,),
                 [pl.BlockSpec((N_HEADS, tm, QK), lambda i: (0, i, 0)), row, row, vec, row,
                  pl.BlockSpec((d, rank), lambda i: (0, 0)),
                  pl.BlockSpec((1, rank), lambda i: (0, 0)),
                  pl.BlockSpec((N_HEADS, rank, QK), lambda i: (0, 0, 0)), rope, rope],
                 (row,
                  pl.BlockSpec((N_HEADS, rank, QK), lambda i: (0, 0, 0)),
                  pl.BlockSpec((d, rank), lambda i: (0, 0)),
                  pl.BlockSpec((1, rank), lambda i: (0, 0)), vec),
                 (jax.ShapeDtypeStruct((seq, d), F32),
                  jax.ShapeDtypeStruct((N_HEADS, rank, QK), BF16),
                  jax.ShapeDtypeStruct((d, rank), BF16),
                  jax.ShapeDtypeStruct((1, rank), F32),
                  jax.ShapeDtypeStruct((1, d), F32)),
                 (dq, xn, h, g, dres, wdq, gq, wuq, cos, sin),
                 scratch=[pltpu.VMEM((N_HEADS, rank, QK), F32), pltpu.VMEM((d, rank), F32)],
                 rider=rider)


def kv_fwd(h, g, wdkv, gkv, wukv, cos, sin, name, rider=()):
    seq, d = h.shape
    tm = min(ROW_TILE, seq)
    wk = KV_RANK + ROPE_PAD

    def body(h_ref, g_ref, wdkv_ref, gkv_ref, wukv_ref, cos_ref, sin_ref, k_ref, v_ref, c_ref):
        xk = _rms(h_ref[...], g_ref[...])[0].astype(BF16)
        ckv = _nn(xk, wdkv_ref[...])
        c_kv = ckv[:, 0:KV_RANK]
        c_ref[...] = c_kv
        kr = _rope_fwd(ckv[:, KV_RANK:wk], cos_ref[...], sin_ref[...]).astype(BF16)
        ckn = _rms(c_kv, gkv_ref[...])[0].astype(BF16)
        for hd in range(N_HEADS):
            kvh = _nn(ckn, wukv_ref[hd])
            k_ref[hd, :, 0:NOPE] = kvh[:, 0:NOPE].astype(BF16)
            k_ref[hd, :, NOPE:QK] = kr
            v_ref[hd] = kvh[:, NOPE:NOPE + VDIM].astype(BF16)

    rope = pl.BlockSpec((tm, ROPE_PAD), lambda i: (i, 0))
    return _call(body, name, (seq // tm,),
                 [pl.BlockSpec((tm, d), lambda i: (i, 0)),
                  pl.BlockSpec((1, d), lambda i: (0, 0)),
                  pl.BlockSpec((d, wk), lambda i: (0, 0)),
                  pl.BlockSpec((1, KV_RANK), lambda i: (0, 0)),
                  pl.BlockSpec((N_HEADS, KV_RANK, NOPE + VDIM), lambda i: (0, 0, 0)), rope, rope],
                 (pl.BlockSpec((N_HEADS, tm, QK), lambda i: (0, i, 0)),
                  pl.BlockSpec((N_HEADS, tm, VDIM), lambda i: (0, i, 0)),
                  pl.BlockSpec((tm, KV_RANK), lambda i: (i, 0))),
                 (jax.ShapeDtypeStruct((N_HEADS, seq, QK), BF16),
                  jax.ShapeDtypeStruct((N_HEADS, seq, VDIM), BF16),
                  jax.ShapeDtypeStruct((seq, KV_RANK), F32)),
                 (h, g, wdkv, gkv, wukv, cos, sin), rider=rider)


def kv_bwd(dks, dvs, c_kv, h, g, dres, wdkv, gkv, wukv, cos, sin, name, rider=()):
    seq, d = h.shape
    tm = min(ROW_TILE, seq)
    n_i = seq // tm
    wk = KV_RANK + ROPE_PAD
    n_b = len(dks)

    def body(*refs):
        dk_refs = refs[:n_b]
        dv_refs = refs[n_b:2 * n_b]
        (c_ref, h_ref, g_ref, dr_ref, wdkv_ref, gkv_ref, wukv_ref, cos_ref, sin_ref,
         dh_ref, dwukv_ref, dwdkv_ref, dgkv_ref, dg_ref, acc_ukv, acc_dkv) = refs[2 * n_b:]
        i = pl.program_id(0)

        @pl.when(i == 0)
        def _():
            acc_ukv[...] = jnp.zeros_like(acc_ukv)
            acc_dkv[...] = jnp.zeros_like(acc_dkv)
            dgkv_ref[...] = jnp.zeros_like(dgkv_ref)
            dg_ref[...] = jnp.zeros_like(dg_ref)
        ckn, chat, crstd = _rms(c_ref[...], gkv_ref[...])
        cknb = ckn.astype(BF16)
        dckn = jnp.zeros((tm, KV_RANK), F32)
        dkr = jnp.zeros((tm, ROPE_PAD), F32)
        for hd in range(N_HEADS):
            dk = dk_refs[0][hd].astype(F32)
            dv = dv_refs[0][hd].astype(F32)
            for j in range(1, n_b):
                dk = dk + dk_refs[j][hd].astype(F32)
                dv = dv + dv_refs[j][hd].astype(F32)
            dkr = dkr + dk[:, NOPE:QK]
            dkvh = jnp.concatenate([dk[:, 0:NOPE].astype(BF16), dv.astype(BF16)], axis=1)
            dckn = dckn + _nt(dkvh, wukv_ref[hd])
            acc_ukv[hd] += _tn(cknb, dkvh)
        dc_kv, dg_rows = _rms_bwd(dckn, chat, crstd, gkv_ref[...])
        dgkv_ref[...] += jnp.sum(dg_rows, axis=0, keepdims=True)
        dkr_raw = _rope_bwd(dkr, cos_ref[...], sin_ref[...])
        dckv = jnp.concatenate([dc_kv.astype(BF16), dkr_raw.astype(BF16)], axis=1)
        xk, xhat, rstd = _rms(h_ref[...], g_ref[...])
        acc_dkv[...] += _tn(xk.astype(BF16), dckv)
        dh, dg_rows = _rms_bwd(_nt(dckv, wdkv_ref[...]), xhat, rstd, g_ref[...])
        dh_ref[...] = dr_ref[...] + dh
        dg_ref[...] += jnp.sum(dg_rows, axis=0, keepdims=True)

        @pl.when(i == n_i - 1)
        def _():
            dwukv_ref[...] = acc_ukv[...].astype(BF16)
            dwdkv_ref[...] = acc_dkv[...].astype(BF16)

    kspec = pl.BlockSpec((N_HEADS, tm, QK), lambda i: (0, i, 0))
    vspec = pl.BlockSpec((N_HEADS, tm, VDIM), lambda i: (0, i, 0))
    rope = pl.BlockSpec((tm, ROPE_PAD), lambda i: (i, 0))
    row = pl.BlockSpec((tm, d), lambda i: (i, 0))
    vec = pl.BlockSpec((1, d), lambda i: (0, 0))
    return _call(body, name, (n_i,),
                 [kspec] * n_b + [vspec] * n_b + [
                     pl.BlockSpec((tm, KV_RANK), lambda i: (i, 0)), row, vec, row,
                     pl.BlockSpec((d, wk), lambda i: (0, 0)),
                     pl.BlockSpec((1, KV_RANK), lambda i: (0, 0)),
                     pl.BlockSpec((N_HEADS, KV_RANK, NOPE + VDIM), lambda i: (0, 0, 0)), rope, rope],
                 (row,
                  pl.BlockSpec((N_HEADS, KV_RANK, NOPE + VDIM), lambda i: (0, 0, 0)),
                  pl.BlockSpec((d, wk), lambda i: (0, 0)),
                  pl.BlockSpec((1, KV_RANK), lambda i: (0, 0)), vec),
                 (jax.ShapeDtypeStruct((seq, d), F32),
                  jax.ShapeDtypeStruct((N_HEADS, KV_RANK, NOPE + VDIM), BF16),
                  jax.ShapeDtypeStruct((d, wk), BF16),
                  jax.ShapeDtypeStruct((1, KV_RANK), F32),
                  jax.ShapeDtypeStruct((1, d), F32)),
                 (*dks, *dvs, c_kv, h, g, dres, wdkv, gkv, wukv, cos, sin),
                 scratch=[pltpu.VMEM((N_HEADS, KV_RANK, NOPE + VDIM), F32), pltpu.VMEM((d, wk), F32)],
                 rider=rider)


def o_bwd(dh, o, wo, name, rider=()):
    seq, d = dh.shape
    hv = o.shape[1]
    tm = min(ROW_TILE, seq)
    n_i = seq // tm

    def body(dh_ref, o_ref, wo_ref, do_ref, dwo_ref, acc):
        i = pl.program_id(0)

        @pl.when(i == 0)
        def _():
            acc[...] = jnp.zeros_like(acc)
        dhb = dh_ref[...].astype(BF16)
        do_ref[...] = _nt(dhb, wo_ref[...]).astype(BF16)
        acc[...] += _tn(o_ref[...], dhb)

        @pl.when(i == n_i - 1)
        def _():
            dwo_ref[...] = acc[...].astype(BF16)

    return _call(body, name, (n_i,),
                 [pl.BlockSpec((tm, d), lambda i: (i, 0)),
                  pl.BlockSpec((tm, hv), lambda i: (i, 0)),
                  pl.BlockSpec((hv, d), lambda i: (0, 0))],
                 (pl.BlockSpec((tm, hv), lambda i: (i, 0)),
                  pl.BlockSpec((hv, d), lambda i: (0, 0))),
                 (jax.ShapeDtypeStruct((seq, hv), BF16), jax.ShapeDtypeStruct((hv, d), BF16)),
                 (dh, o, wo), scratch=[pltpu.VMEM((hv, d), F32)], rider=rider)


def _mask_diagonal(s):
    row = lax.broadcasted_iota(jnp.int32, s.shape, 0)
    col = lax.broadcasted_iota(jnp.int32, s.shape, 1)
    return jnp.where(col <= row, s, NEG_BIG)


def attn_fwd(q, k, v, name, rider=()):
    _, seq, _ = q.shape
    t = min(ATTN_TILE, seq // 2)
    n_pair = seq // (2 * t)

    def body(q_ref, k_ref, v_ref, o_ref, lse_ref):
        qi = pl.program_id(1)
        q_a = q_ref[0:t, :]
        q_b = q_ref[t:2 * t, :]

        def rows(j):
            return pl.ds(pl.multiple_of(j * t, t), t)

        def update(qx, kb, vb, state, diagonal=False):
            m, l, acc = state
            s = _nt(qx, kb)
            if diagonal:
                s = _mask_diagonal(s)
            m_new = jnp.maximum(m, jnp.max(s, axis=1, keepdims=True))
            p = jnp.exp(s - m_new)
            alpha = jnp.exp(m - m_new)
            l = alpha * l + jnp.sum(p, axis=1, keepdims=True)
            acc = alpha * acc + _nn(p.astype(BF16), vb)
            return m_new, l, acc

        def step(j, carry):
            both = pl.ds(pl.multiple_of(j * 2 * t, 2 * t), 2 * t)
            kb, vb = k_ref[both, :], v_ref[both, :]
            return update(q_a, kb, vb, carry[0:3]) + update(q_b, kb, vb, carry[3:6])

        init = (jnp.full((t, 1), NEG_BIG, F32), jnp.zeros((t, 1), F32), jnp.zeros((t, VDIM), F32))
        carry = lax.fori_loop(0, qi, step, init + init)
        k0, v0 = k_ref[rows(2 * qi), :], v_ref[rows(2 * qi), :]
        k1, v1 = k_ref[rows(2 * qi + 1), :], v_ref[rows(2 * qi + 1), :]
        state_a = update(q_a, k0, v0, carry[0:3], diagonal=True)
        state_b = update(q_b, k1, v1, update(q_b, k0, v0, carry[3:6]), diagonal=True)
        for half, (m, l, acc) in enumerate((state_a, state_b)):
            o_ref[half * t:(half + 1) * t, :] = (acc / l).astype(BF16)
            lse_ref[half * t:(half + 1) * t, :] = jnp.broadcast_to(m + jnp.log(l), (t, LANES))

    return _call(body, name, (N_HEADS, n_pair),
                 [pl.BlockSpec((None, 2 * t, QK), lambda h, i: (h, i, 0)),
                  pl.BlockSpec((None, seq, QK), lambda h, i: (h, 0, 0)),
                  pl.BlockSpec((None, seq, VDIM), lambda h, i: (h, 0, 0))],
                 (pl.BlockSpec((2 * t, VDIM), lambda h, i: (i, h)),
                  pl.BlockSpec((None, 2 * t, LANES), lambda h, i: (h, i, 0))),
                 (jax.ShapeDtypeStruct((seq, N_HEADS * VDIM), BF16),
                  jax.ShapeDtypeStruct((N_HEADS, seq, LANES), F32)),
                 (q, k, v), rider=rider)


def attn_bwd(q, k, v, o, do, lse, name, rider=()):
    _, seq, _ = q.shape
    t = min(ATTN_TILE, seq // 2)
    n_q = seq // t
    n_pair = n_q // 2

    def body(q_ref, k_ref, v_ref, o_ref, do_ref, lse_ref, dq_ref, dk_ref, dv_ref,
             dq_acc, dk_acc, dv_acc):
        kj = pl.program_id(1)

        @pl.when(kj == 0)
        def _():
            dq_acc[...] = jnp.zeros_like(dq_acc)
        halves = (slice(0, t), slice(t, 2 * t))

        def block(i, masks, n_rows=t):
            rows = pl.ds(pl.multiple_of(i * n_rows, n_rows), n_rows)
            qb = q_ref[rows, :]
            dob = do_ref[rows, :]
            lse_col = lse_ref[rows, 0:1]
            delta = jnp.sum(dob.astype(F32) * o_ref[rows, :].astype(F32), axis=1, keepdims=True)
            dq, out = None, {}
            for x, diagonal in enumerate(masks):
                if diagonal is None:
                    continue
                kb, vb = k_ref[halves[x], :], v_ref[halves[x], :]
                s = _nt(qb, kb)
                if diagonal:
                    s = _mask_diagonal(s)
                p = jnp.exp(s - lse_col)
                ds = (p * (_nt(dob, vb) - delta)).astype(BF16)
                out[x] = (_tn(p.astype(BF16), dob), _tn(ds, qb))
                part = _nn(ds, kb)
                dq = part if dq is None else dq + part
            dq_acc[rows, :] += dq
            return out

        first = block(2 * kj, (True, None))
        second = block(2 * kj + 1, (False, True))
        dv_acc[halves[0], :] = first[0][0] + second[0][0]
        dk_acc[halves[0], :] = first[0][1] + second[0][1]
        dv_acc[halves[1], :] = second[1][0]
        dk_acc[halves[1], :] = second[1][1]

        def step(i, carry):
            out = block(i, (False, False), n_rows=2 * t)
            for x in (0, 1):
                dv_acc[halves[x], :] += out[x][0]
                dk_acc[halves[x], :] += out[x][1]
            return carry

        lax.fori_loop(kj + 1, n_pair, step, 0)
        dk_ref[...] = dk_acc[...].astype(BF16)
        dv_ref[...] = dv_acc[...].astype(BF16)

        @pl.when(kj == n_pair - 1)
        def _():
            dq_ref[...] = dq_acc[...].astype(BF16)

    head_rows = pl.BlockSpec((seq, VDIM), lambda h, j: (0, h))
    return _call(body, name, (N_HEADS, n_pair),
                 [pl.BlockSpec((None, seq, QK), lambda h, j: (h, 0, 0)),
                  pl.BlockSpec((None, 2 * t, QK), lambda h, j: (h, j, 0)),
                  pl.BlockSpec((None, 2 * t, VDIM), lambda h, j: (h, j, 0)),
                  head_rows, head_rows,
                  pl.BlockSpec((None, seq, LANES), lambda h, j: (h, 0, 0))],
                 (pl.BlockSpec((None, seq, QK), lambda h, j: (h, 0, 0)),
                  pl.BlockSpec((None, 2 * t, QK), lambda h, j: (h, j, 0)),
                  pl.BlockSpec((None, 2 * t, VDIM), lambda h, j: (h, j, 0))),
                 (jax.ShapeDtypeStruct((N_HEADS, seq, QK), BF16),
                  jax.ShapeDtypeStruct((N_HEADS, seq, QK), BF16),
                  jax.ShapeDtypeStruct((N_HEADS, seq, VDIM), BF16)),
                 (q, k, v, o, do, lse),
                 scratch=[pltpu.VMEM((seq, QK), F32), pltpu.VMEM((2 * t, QK), F32),
                          pltpu.VMEM((2 * t, VDIM), F32)], rider=rider)


def loss_head(h, g, target, name):
    seq, d = h.shape
    tm = min(ROW_TILE, seq)

    def body(h_ref, g_ref, t_ref, l_ref, dh_ref, dg_ref):
        i = pl.program_id(0)

        @pl.when(i == 0)
        def _():
            l_ref[...] = jnp.zeros_like(l_ref)
            dg_ref[...] = jnp.zeros_like(dg_ref)
        y, xhat, rstd = _rms(h_ref[...], g_ref[...])
        diff = y - t_ref[...]
        l_ref[...] += jnp.sum(jnp.sum(diff * diff, axis=1, keepdims=True), axis=0, keepdims=True)
        dh, dg_rows = _rms_bwd(diff * (1.0 / d), xhat, rstd, g_ref[...])
        dh_ref[...] = dh
        dg_ref[...] += jnp.sum(dg_rows, axis=0, keepdims=True)

    row = pl.BlockSpec((tm, d), lambda i: (i, 0))
    vec = pl.BlockSpec((1, d), lambda i: (0, 0))
    return _call(body, name, (seq // tm,), [row, vec, row],
                 (pl.BlockSpec((1, LANES), lambda i: (0, 0)), row, vec),
                 (jax.ShapeDtypeStruct((1, LANES), F32), jax.ShapeDtypeStruct((seq, d), F32),
                  jax.ShapeDtypeStruct((1, d), F32)),
                 (h, g, target))[0]


def _pack(parts):
    rows = []
    for p in parts:
        flat = p.reshape(-1)
        n_rows = -(-flat.shape[0] // (8 * LANES)) * 8
        flat = jnp.pad(flat, (0, n_rows * LANES - flat.shape[0]))
        rows.append(flat.reshape(n_rows, LANES))
    return jnp.concatenate(rows, axis=0)


def _unpack(packed, shapes):
    lead = packed.shape[:-2]
    out, r0 = [], 0
    for shape in shapes:
        size = 1
        for s in shape:
            size *= s
        n_rows = -(-size // (8 * LANES)) * 8
        part = packed[..., r0:r0 + n_rows, :].reshape(lead + (n_rows * LANES,))
        out.append(part[..., :size].reshape(lead + tuple(shape)))
        r0 += n_rows
    return out


FWD_RIDERS = {
    "mixer_fwd0": [("ffn_w_up", 0)],
    "ffn_fwd0": [("ffn_w_down", 0), ("a_w_in", 1), ("a_w_out", 1)],
    "ffn_out0": [("ffn_w_down", 1)],
    "mixer_fwd1": [("ffn_w_up", 1)],
    "mixer_out1": [("w_dkv", 0), ("w_ukv", 0), ("b_w_dq", 0), ("b_w_uq", 0)],
    "ffn_fwd1": [("ffn_w_up", 2), ("b_w_o", 0)],
    "ffn_out1": [("ffn_w_down", 2)],
    "attn_fwd0": [("ffn_w_up", 3), ("ffn_w_down", 3), ("b_w_dq", 1), ("b_w_uq", 1), ("b_w_o", 1)],
}
BWD_RIDERS = {
    "ffn_in_bwd3": [("ffn_w_down", 3)],
    "attn_bwd1": [("ffn_w_up", 3), ("b_w_o", 1)],
    "ffn_bwd2": [("b_w_uq", 1), ("b_w_dq", 1)],
    "ffn_in_bwd2": [("ffn_w_down", 2)],
    "attn_bwd0": [("ffn_w_up", 2), ("b_w_o", 0)],
    "ffn_bwd1": [("b_w_uq", 0), ("b_w_dq", 0), ("w_ukv", 0), ("w_dkv", 0)],
    "ffn_in_bwd1": [("ffn_w_down", 1), ("ffn_w_up", 1, "pair")],
    "ffn_bwd0": [("ffn_w_up", 1, "chip"), ("a_w_in", 1), ("a_w_out", 1)],
    "ffn_in_bwd0": [("ffn_w_down", 0), ("ffn_w_up", 0, "pair")],
    "mixer_bwd0": [("ffn_w_up", 0, "chip")],
    "mixer_in_bwd0": [("a_w_out", 0), ("a_w_in", 0, "pair")],
    "adamw_a_w_out": [("a_w_in", 0, "chip")],
}


def kernel(x, a_mix_norm, a_w_in, a_conv, a_w_out, b_mix_norm, b_w_dq, b_q_norm, b_w_uq, b_w_o, kv_in_norm, w_dkv, kv_norm, w_ukv, ffn_norm, ffn_w_up, ffn_conv, ffn_w_down, final_norm, loss_target, m_a_mix_norm, m_a_w_in, m_a_conv, m_a_w_out, m_b_mix_norm, m_b_w_dq, m_b_q_norm, m_b_w_uq, m_b_w_o, m_kv_in_norm, m_w_dkv, m_kv_norm, m_w_ukv, m_ffn_norm, m_ffn_w_up, m_ffn_conv, m_ffn_w_down, m_final_norm, v_a_mix_norm, v_a_w_in, v_a_conv, v_a_w_out, v_b_mix_norm, v_b_w_dq, v_b_q_norm, v_b_w_uq, v_b_w_o, v_kv_in_norm, v_w_dkv, v_kv_norm, v_w_ukv, v_ffn_norm, v_ffn_w_up, v_ffn_conv, v_ffn_w_down, v_final_norm):
    seq, d = x.shape[1], x.shape[2]
    me = 4 * lax.axis_index("x") + 2 * lax.axis_index("y") + lax.axis_index("c")
    h0 = x.reshape(seq, d)
    target = loss_target.reshape(seq, d)
    cos, sin = _rope_tables(seq)
    rank = b_w_dq.shape[-1]
    f8 = ffn_w_up.shape[-1]
    fd = ffn_w_down.shape[1]
    dshard = a_w_out.shape[1]
    hv = N_HEADS * VDIM

    shards = {"a_w_in": a_w_in, "a_w_out": a_w_out, "b_w_dq": b_w_dq, "b_w_uq": b_w_uq,
              "b_w_o": b_w_o, "w_dkv": w_dkv[None], "w_ukv": w_ukv[None],
              "ffn_w_up": ffn_w_up, "ffn_w_down": ffn_w_down}

    def relayout(name, g):
        if name == "a_w_in":
            w = jnp.transpose(g, (1, 0, 2)).reshape(d, 3, d)
            return jnp.transpose(w, (1, 0, 2))
        if name == "a_w_out":
            return g.reshape(d, d)
        if name == "b_w_dq":
            return g.reshape(d, rank)
        if name == "b_w_uq":
            return jnp.pad(g, ((0, 0), (0, 0), (0, QK - NOPE - ROPE)))
        if name == "b_w_o":
            return g.reshape(hv, d)
        if name == "w_dkv":
            return jnp.pad(g.reshape(d, KV_RANK + ROPE), ((0, 0), (0, ROPE_PAD - ROPE)))
        if name == "ffn_w_down":
            return g.reshape(N_DEV // 2, 2 * fd, d)
        return g

    weights = {}

    def ag_rider(host):
        return [("ag", shards[n][l].astype(BF16)) for n, l in FWD_RIDERS.get(host, [])]

    def ag_done(host, outs):
        for (n, l), g in zip(FWD_RIDERS.get(host, []), outs):
            weights[n, l] = relayout(n, g)

    small_shapes = [a_mix_norm.shape, a_conv.shape, ffn_conv.shape]
    first = exchange([("ag", a_w_in[0].astype(BF16)), ("ag", a_w_out[0].astype(BF16)),
                      ("ag", _pack([a_mix_norm, a_conv, ffn_conv]))], "ag_first")
    weights["a_w_in", 0] = relayout("a_w_in", first[0])
    weights["a_w_out", 0] = relayout("a_w_out", first[1])
    s_mix, s_aconv, s_fconv = _unpack(first[2], small_shapes)
    a_gain = jnp.transpose(s_mix, (1, 0, 2)).reshape(N_A, d)
    a_cw = jnp.transpose(s_aconv, (1, 2, 0, 3)).reshape(N_A, 3, d)
    f_cw = jnp.transpose(s_fconv, (1, 0, 2, 3))

    def mixer_gain(layer):
        if layer >= DEPTH:
            return None
        return a_gain[layer][None] if layer < N_A else b_mix_norm[layer - N_A][None]

    saved = {}
    h = h0
    xn = norm_fwd(h, mixer_gain(0), "norm_first")
    kv = None
    for layer in range(DEPTH):
        saved["hm", layer], saved["xm", layer] = h, xn
        if layer < N_A:
            name = f"mixer_fwd{layer}"
            (u4, z), r = mixer_fwd(xn, weights["a_w_in", layer], a_cw[layer], name, rider=ag_rider(name))
            ag_done(name, r)
            saved["mix", layer] = (u4, z)
            name = f"mixer_out{layer}"
            (h, xn), r = proj_residual(z[None], weights["a_w_out", layer][None], h, name,
                                       g_next=ffn_norm[layer][None], rider=ag_rider(name))
            ag_done(name, r)
        else:
            j = layer - N_A
            name = f"q_fwd{j}"
            (q,), r = q_fwd(xn, weights["b_w_dq", j], b_q_norm[j][None], weights["b_w_uq", j],
                            cos, sin, name, rider=ag_rider(name))
            ag_done(name, r)
            name = f"attn_fwd{j}"
            (o, lse), r = attn_fwd(q, kv[0], kv[1], name, rider=ag_rider(name))
            ag_done(name, r)
            saved["attn", layer] = (q, o, lse)
            name = f"attn_out{j}"
            (h, xn), r = proj_residual(o[None], weights["b_w_o", j][None], h, name,
                                       g_next=ffn_norm[layer][None], rider=ag_rider(name))
            ag_done(name, r)
        saved["hf", layer], saved["xf", layer] = h, xn
        name = f"ffn_fwd{layer}"
        (up2, cv2, act), r = ffn_fwd(xn, weights["ffn_w_up", layer], f_cw[layer], name, rider=ag_rider(name))
        ag_done(name, r)
        saved["ffn", layer] = (up2, cv2, act)
        name = f"ffn_out{layer}"
        (h, xn), r = proj_residual(act, weights["ffn_w_down", layer], h, name,
                                   g_next=mixer_gain(layer + 1), rider=ag_rider(name))
        ag_done(name, r)
        if layer == N_A - 1:
            (k_all, v_all, c_kv), r = kv_fwd(h, kv_in_norm[None], weights["w_dkv", 0], kv_norm[None],
                                             weights["w_ukv", 0], cos, sin, "kv_fwd",
                                             rider=ag_rider("kv_fwd"))
            ag_done("kv_fwd", r)
            kv = (k_all, v_all, c_kv)

    sq_err, dh, d_final = loss_head(h, final_norm[None], target, "loss_head")
    loss = lax.psum(sq_err[0, 0] * (0.5 / d), ("x", "y", "c"))

    grads = {}
    parts = {}

    pair_sums = {}

    def by_chip(g):
        return g.reshape((N_DEV // 2, 2) + g.shape[1:])

    def rs_rider(host):
        tasks = []
        for key in BWD_RIDERS.get(host, []):
            if len(key) == 2:
                tasks.append(("rs", grads[key]))
            elif key[2] == "pair":
                tasks.append(("rs_pair", by_chip(grads[key[:2]])))
            else:
                tasks.append(("rs_chip", pair_sums[key[:2]]))
        return tasks

    def rs_done(host, outs):
        for key, p in zip(BWD_RIDERS.get(host, []), outs):
            if len(key) == 3 and key[2] == "pair":
                pair_sums[key[:2]] = pair_sum(by_chip(grads[key[:2]]), p, f"pair_sum_{key[0]}{key[1]}")
            else:
                parts[key[:2]] = p

    d_ffn_norm = [None] * DEPTH
    d_fconv = [None] * DEPTH
    d_a_gain = [None] * N_A
    d_aconv = [None] * N_A
    d_b_gain = [None] * N_B
    d_q_gain = [None] * N_B
    dks, dvs = [], []
    for layer in reversed(range(DEPTH)):
        if layer == N_A - 1:
            hk = saved["hm", layer + 1]
            (dh, dwukv, dwdkv, d_kv_gain, d_kvin_gain), r = kv_bwd(
                dks, dvs, kv[2], hk, kv_in_norm[None], dh, weights["w_dkv", 0], kv_norm[None],
                weights["w_ukv", 0], cos, sin, "kv_bwd", rider=rs_rider("kv_bwd"))
            rs_done("kv_bwd", r)
            grads["w_ukv", 0] = dwukv
            grads["w_dkv", 0] = dwdkv[:, :KV_RANK + ROPE].reshape(N_DEV, dshard, KV_RANK + ROPE)
        up2, cv2, act = saved["ffn", layer]
        name = f"ffn_bwd{layer}"
        (dup2, dwup, dwdown, dcw), r = ffn_bwd(dh, weights["ffn_w_down", layer], up2, cv2, act,
                                               saved["xf", layer], f_cw[layer], name, rider=rs_rider(name))
        rs_done(name, r)
        grads["ffn_w_up", layer] = dwup.reshape(N_DEV, d, f8)
        grads["ffn_w_down", layer] = dwdown.reshape(N_DEV, fd, d)
        d_fconv[layer] = dcw.reshape(N_DEV, 3, f8)
        name = f"ffn_in_bwd{layer}"
        (dh, d_ffn_norm[layer]), r = proj_t_rms_bwd(dup2.reshape(N_DEV, seq, f8), weights["ffn_w_up", layer],
                                                    saved["hf", layer], ffn_norm[layer][None], dh, name,
                                                    rider=rs_rider(name))
        rs_done(name, r)
        hm, xm = saved["hm", layer], saved["xm", layer]
        if layer < N_A:
            u4, z = saved["mix", layer]
            name = f"mixer_bwd{layer}"
            (du3, dwin3, dwout, dcw), r = mixer_bwd(dh, weights["a_w_out", layer], u4, z, xm, a_cw[layer],
                                                    name, rider=rs_rider(name))
            rs_done(name, r)
            dwin = jnp.transpose(dwin3, (1, 0, 2)).reshape(d, N_DEV, 3 * d // N_DEV)
            grads["a_w_in", layer] = jnp.transpose(dwin, (1, 0, 2))
            grads["a_w_out", layer] = dwout.reshape(N_DEV, dshard, d)
            d_aconv[layer] = dcw
            name = f"mixer_in_bwd{layer}"
            (dh, d_a_gain[layer]), r = proj_t_rms_bwd(du3, weights["a_w_in", layer], hm, a_gain[layer][None],
                                                      dh, name, rider=rs_rider(name))
            rs_done(name, r)
        else:
            j = layer - N_A
            q, o, lse = saved["attn", layer]
            name = f"attn_out_bwd{j}"
            (do, dwo), r = o_bwd(dh, o, weights["b_w_o", j], name, rider=rs_rider(name))
            rs_done(name, r)
            grads["b_w_o", j] = dwo.reshape(N_DEV, dshard, d)
            name = f"attn_bwd{j}"
            (dq, dk, dv), r = attn_bwd(q, kv[0], kv[1], o, do, lse, name, rider=rs_rider(name))
            rs_done(name, r)
            dks.append(dk)
            dvs.append(dv)
            name = f"q_bwd{j}"
            (dh, dwuq, dwdq, d_q_gain[j], d_b_gain[j]), r = q_bwd(
                dq, xm, hm, b_mix_norm[j][None], dh, weights["b_w_dq", j], b_q_norm[j][None],
                weights["b_w_uq", j], cos, sin, name, rider=rs_rider(name))
            rs_done(name, r)
            grads["b_w_uq", j] = dwuq[:, :, :NOPE + ROPE]
            grads["b_w_dq", j] = dwdq.reshape(N_DEV, dshard, rank)
    grad_x = dh.reshape(x.shape)

    full_small = [
        jnp.concatenate(d_a_gain, axis=0),
        jnp.stack(d_aconv),
        jnp.concatenate(d_b_gain, axis=0),
        jnp.concatenate(d_q_gain, axis=0),
        d_kvin_gain[0],
        d_kv_gain[0],
        jnp.concatenate(d_ffn_norm, axis=0),
        jnp.stack(d_fconv),
        d_final[0],
    ]
    full_shapes = [t.shape for t in full_small]
    small_pack = _pack(full_small)

    res = {}

    def update(name, n_layers, w, m, v, extra=(), transposed=False):
        view = (lambda t: jnp.transpose(t, (0, 2, 1))) if transposed else (lambda t: t)
        call = sum_adamw_transposed if transposed else sum_adamw
        shard = w.shape if w.ndim == 3 else (1,) + w.shape
        host = f"adamw_{name}"
        outs, r = call([parts[name, l] for l in range(n_layers)], view(w.reshape(shard)),
                       view(m.reshape(shard)), view(v.reshape(shard)), host,
                       rider=rs_rider(host) + list(extra))
        rs_done(host, r)
        res[name] = [view(t).reshape(w.shape) for t in outs]
        return r[len(BWD_RIDERS.get(host, [])):]

    (g_parts,) = update("a_w_out", N_A, a_w_out, m_a_w_out, v_a_w_out, extra=[("ag", small_pack)])
    update("ffn_w_down", DEPTH, ffn_w_down, m_ffn_w_down, v_ffn_w_down)
    update("ffn_w_up", DEPTH, ffn_w_up, m_ffn_w_up, v_ffn_w_up, transposed=True)
    update("b_w_dq", N_B, b_w_dq, m_b_w_dq, v_b_w_dq)
    update("b_w_uq", N_B, b_w_uq, m_b_w_uq, v_b_w_uq)
    update("b_w_o", N_B, b_w_o, m_b_w_o, v_b_w_o)
    update("w_dkv", 1, w_dkv, m_w_dkv, v_w_dkv)
    update("w_ukv", 1, w_ukv, m_w_ukv, v_w_ukv)
    update("a_w_in", N_A, a_w_in, m_a_w_in, v_a_w_in)

    summed = sum_slots(g_parts, "sum_small_grads")
    (s_a_gain, s_aconv_g, s_b_gain, s_q_gain, s_kvin, s_kvn, s_ffn_gain, s_fconv_g,
     s_final) = _unpack(summed, full_shapes)
    dsl = d // N_DEV
    small = [
        ("a_mix_norm", lax.dynamic_slice_in_dim(s_a_gain, me * dsl, dsl, axis=1), a_mix_norm, m_a_mix_norm, v_a_mix_norm),
        ("a_conv", lax.dynamic_slice_in_dim(s_aconv_g, me * dsl, dsl, axis=2), a_conv, m_a_conv, v_a_conv),
        ("b_mix_norm", s_b_gain, b_mix_norm, m_b_mix_norm, v_b_mix_norm),
        ("b_q_norm", s_q_gain, b_q_norm, m_b_q_norm, v_b_q_norm),
        ("kv_in_norm", s_kvin, kv_in_norm, m_kv_in_norm, v_kv_in_norm),
        ("kv_norm", s_kvn, kv_norm, m_kv_norm, v_kv_norm),
        ("ffn_norm", s_ffn_gain, ffn_norm, m_ffn_norm, v_ffn_norm),
        ("ffn_conv", lax.dynamic_index_in_dim(s_fconv_g, me, axis=1, keepdims=False), ffn_conv, m_ffn_conv, v_ffn_conv),
        ("final_norm", s_final, final_norm, m_final_norm, v_final_norm),
    ]
    shapes = [t[2].shape for t in small]
    packed = [_pack([t[k] for t in small])[None] for k in (1, 2, 3, 4)]
    outs, _ = sum_adamw([packed[0]], packed[1], packed[2], packed[3], "adamw_small")
    unpacked = [_unpack(t[0], shapes) for t in outs]
    for idx, t in enumerate(small):
        res[t[0]] = [unpacked[k][idx] for k in range(4)]

    order = ["a_mix_norm", "a_w_in", "a_conv", "a_w_out", "b_mix_norm", "b_w_dq", "b_q_norm",
             "b_w_uq", "b_w_o", "kv_in_norm", "w_dkv", "kv_norm", "w_ukv", "ffn_norm",
             "ffn_w_up", "ffn_conv", "ffn_w_down", "final_norm"]
    return (loss, grad_x, *[res[n][0] for n in order], *[res[n][1] for n in order],
            *[res[n][2] for n in order], *[res[n][3] for n in order])
```

```python
import functools

import jax
import jax.numpy as jnp
from jax import lax
from jax.experimental import pallas as pl
from jax.experimental.pallas import tpu as pltpu

F32 = jnp.float32
BF16 = jnp.bfloat16

N_DEV = 8
N_HEADS = 8
NOPE = 128
ROPE = 64
ROPE_PAD = 128
QK = NOPE + ROPE_PAD
VDIM = 128
KV_RANK = 256
ROPE_THETA = 10000.0
RMS_EPS = 1e-6
ATTN_SCALE = (NOPE + ROPE) ** -0.5
N_A = 2
N_B = 2
DEPTH = 4

ADAM_LR = 0.001
ADAM_B1 = 0.9
ADAM_B2 = 0.999
ADAM_EPS = 1e-08
ADAM_WD = 0.01
ADAM_STEP = 10

V7X_VMEM_LIMIT = 56 * 1024 * 1024
BF16_SUBLANES = 16
ROW_TILE = 512
ROW_TILE_LARGE = 1024
ADAM_ROWS = 256
ATTN_TILE = 512
MIXER_CHUNK = 512
LANES = 128
NEG_BIG = -1e30
COPIES_PER_TASK = 7

MESH_ID = pl.DeviceIdType.MESH
ANY = pl.BlockSpec(memory_space=pl.ANY)


def _nt(a, b):
    return lax.dot_general(a, b, (((1,), (1,)), ((), ())), preferred_element_type=F32)


def _tn(a, b):
    return lax.dot_general(a, b, (((0,), (0,)), ((), ())), preferred_element_type=F32)


def _nn(a, b):
    return jnp.dot(a, b, preferred_element_type=F32)


def _rms(h, g):
    rstd = lax.rsqrt(jnp.mean(h * h, axis=-1, keepdims=True) + RMS_EPS)
    xhat = h * rstd
    return xhat * g, xhat, rstd


def _rms_bwd(dxn, xhat, rstd, g):
    dxhat = dxn * g
    dh = rstd * (dxhat - xhat * jnp.mean(dxhat * xhat, axis=-1, keepdims=True))
    return dh, dxn * xhat


def _shift_down(x, k, halo_rows):
    r = pltpu.roll(x, k, 0)
    row = lax.broadcasted_iota(jnp.int32, x.shape, 0)
    for t in range(k):
        r = jnp.where(row == t, halo_rows[t], r)
    return r


def _shift_up(x, k, halo_rows):
    n = x.shape[0]
    r = pltpu.roll(x, n - k, 0)
    row = lax.broadcasted_iota(jnp.int32, x.shape, 0)
    for t in range(k):
        r = jnp.where(row == n - k + t, halo_rows[t], r)
    return r


def _conv_taps(w_ref):
    return w_ref[0:1, :], w_ref[1:2, :], w_ref[2:3, :]


def _rope_swap(x):
    lane = lax.broadcasted_iota(jnp.int32, x.shape, 1)
    return jnp.where(lane < ROPE // 2, pltpu.roll(x, ROPE_PAD - ROPE // 2, 1),
                     pltpu.roll(x, ROPE // 2, 1))


def _rope_fwd(x, cos, sin):
    return x * cos + _rope_swap(x) * sin


def _rope_bwd(dy, cos, sin):
    return dy * cos - _rope_swap(dy) * sin


def _rope_tables(seq):
    inv = 1.0 / (ROPE_THETA ** (jnp.arange(0, ROPE, 2, dtype=F32) / ROPE))
    ang = jnp.arange(seq, dtype=F32)[:, None] * inv[None, :]
    cos, sin = jnp.cos(ang), jnp.sin(ang)
    zero = jnp.zeros((seq, ROPE_PAD - ROPE), F32)
    return (jnp.concatenate([cos, cos, zero], axis=1),
            jnp.concatenate([-sin, sin, zero], axis=1))


def _row_tile(rows, cap, mult=8):
    best = None
    for t in range(mult, min(rows, cap) + 1, mult):
        if rows % t == 0:
            best = t
    return rows if best is None else best


class _AllGatherTask:
    def __init__(self, t, x_ref, out_ref, send_sems, recv_sems, local_sems):
        self.t, self.x_ref, self.out_ref = t, x_ref, out_ref
        self.send_sems, self.recv_sems, self.local_sems = send_sems, recv_sems, local_sems
        mx, my, mc = lax.axis_index("x"), lax.axis_index("y"), lax.axis_index("c")
        self.mc = mc
        self.me, self.sibling = (mx, my, mc), (mx, my, 1 - mc)
        self.chips = [(1 - mx, my), (mx, 1 - my), (1 - mx, 1 - my)]

    def _slot(self, px, py, pc):
        return self.out_ref.at[4 * px + 2 * py + pc]

    def _copy(self, k, block, to, src=None):
        s = COPIES_PER_TASK * self.t + k
        return pltpu.make_async_remote_copy(
            src_ref=self._slot(*block) if src is None else src, dst_ref=self._slot(*block),
            send_sem=self.send_sems.at[s], recv_sem=self.recv_sems.at[s],
            device_id=to, device_id_type=MESH_ID)

    def _mine(self):
        return pltpu.make_async_copy(self.x_ref, self._slot(*self.me), self.local_sems.at[self.t])

    def _first(self):
        out = [self._copy(0, self.me, self.sibling, src=self.x_ref)]
        out += [self._copy(1 + j, self.me, (*chip, self.mc), src=self.x_ref)
                for j, chip in enumerate(self.chips)]
        return out

    def _passed(self):
        return [self._copy(4 + j, (*chip, self.mc), self.sibling) for j, chip in enumerate(self.chips)]

    def start(self):
        self._mine().start()
        for cp in self._first():
            cp.start()

    def forward(self):
        passed = self._passed()
        for j, chip in enumerate(self.chips):
            self._copy(1 + j, (*chip, self.mc), self.me).wait_recv()
            passed[j].start()

    def finish(self):
        self._copy(0, self.sibling, self.me).wait_recv()
        for j, chip in enumerate(self.chips):
            self._copy(4 + j, (*chip, 1 - self.mc), self.me).wait_recv()
        for cp in self._first() + self._passed():
            cp.wait_send()
        self._mine().wait()


class _ReduceScatterTask:
    def __init__(self, t, g_ref, out_ref, send_sems, recv_sems, local_sems):
        self.t, self.g_ref, self.out_ref = t, g_ref, out_ref
        self.send_sems, self.recv_sems, self.local_sems = send_sems, recv_sems, local_sems
        mx, my, mc = lax.axis_index("x"), lax.axis_index("y"), lax.axis_index("c")
        self.me = 4 * mx + 2 * my + mc
        self.peers = []
        for k in range(1, N_DEV):
            px, py, pc = mx ^ ((k >> 2) & 1), my ^ ((k >> 1) & 1), mc ^ (k & 1)
            self.peers.append(((px, py, pc), 4 * px + 2 * py + pc))

    def _mine(self):
        return pltpu.make_async_copy(self.g_ref.at[self.me], self.out_ref.at[self.me],
                                     self.local_sems.at[self.t])

    def _copy(self, k, src_slot, dst_slot):
        s = COPIES_PER_TASK * self.t + k
        return pltpu.make_async_remote_copy(
            src_ref=self.g_ref.at[src_slot], dst_ref=self.out_ref.at[dst_slot],
            send_sem=self.send_sems.at[s], recv_sem=self.recv_sems.at[s],
            device_id=self.peers[k][0], device_id_type=MESH_ID)

    def start(self):
        self._mine().start()
        for k, (_, peer) in enumerate(self.peers):
            self._copy(k, peer, self.me).start()

    def forward(self):
        pass

    def finish(self):
        for k, (_, peer) in enumerate(self.peers):
            self._copy(k, self.me, peer).wait_recv()
        for k, (_, peer) in enumerate(self.peers):
            self._copy(k, peer, self.me).wait_send()
        self._mine().wait()


class _PairExchangeTask:
    def __init__(self, t, g_ref, out_ref, send_sems, recv_sems, local_sems):
        mx, my, mc = lax.axis_index("x"), lax.axis_index("y"), lax.axis_index("c")
        s = COPIES_PER_TASK * t
        self.copy = pltpu.make_async_remote_copy(
            src_ref=g_ref.at[:, 1 - mc], dst_ref=out_ref,
            send_sem=send_sems.at[s], recv_sem=recv_sems.at[s],
            device_id=(mx, my, 1 - mc), device_id_type=MESH_ID)

    def start(self):
        self.copy.start()

    def forward(self):
        pass

    def finish(self):
        self.copy.wait()


class _ChipScatterTask:
    def __init__(self, t, s_ref, out_ref, send_sems, recv_sems, local_sems):
        self.t, self.s_ref, self.out_ref = t, s_ref, out_ref
        self.send_sems, self.recv_sems, self.local_sems = send_sems, recv_sems, local_sems
        mx, my, mc = lax.axis_index("x"), lax.axis_index("y"), lax.axis_index("c")
        self.chip = 2 * mx + my
        self.peers = []
        for k in range(1, N_DEV // 2):
            px, py = mx ^ ((k >> 1) & 1), my ^ (k & 1)
            self.peers.append(((px, py, mc), 2 * px + py))

    def _mine(self):
        return pltpu.make_async_copy(self.s_ref.at[self.chip], self.out_ref.at[self.chip],
                                     self.local_sems.at[self.t])

    def _copy(self, k, src_slot, dst_slot):
        s = COPIES_PER_TASK * self.t + k
        return pltpu.make_async_remote_copy(
            src_ref=self.s_ref.at[src_slot], dst_ref=self.out_ref.at[dst_slot],
            send_sem=self.send_sems.at[s], recv_sem=self.recv_sems.at[s],
            device_id=self.peers[k][0], device_id_type=MESH_ID)

    def start(self):
        self._mine().start()
        for k, (_, peer) in enumerate(self.peers):
            self._copy(k, peer, self.chip).start()

    def forward(self):
        pass

    def finish(self):
        for k, (_, peer) in enumerate(self.peers):
            self._copy(k, self.chip, peer).wait_recv()
        for k, (_, peer) in enumerate(self.peers):
            self._copy(k, peer, self.chip).wait_send()
        self._mine().wait()


_TASKS = {"ag": _AllGatherTask, "rs": _ReduceScatterTask, "rs_pair": _PairExchangeTask,
          "rs_chip": _ChipScatterTask}


def _task_shape(kind, arr):
    shape = {"ag": (N_DEV,) + arr.shape, "rs": arr.shape, "rs_chip": arr.shape,
             "rs_pair": arr.shape[:1] + arr.shape[2:]}[kind]
    return jax.ShapeDtypeStruct(shape, arr.dtype)


def _sem_shapes(n_tasks):
    return [pltpu.SemaphoreType.DMA((COPIES_PER_TASK * n_tasks,)),
            pltpu.SemaphoreType.DMA((COPIES_PER_TASK * n_tasks,)),
            pltpu.SemaphoreType.DMA((n_tasks,))]


def _make_tasks(rider, in_refs, out_refs, sems):
    return [_TASKS[kind](t, in_refs[t], out_refs[t], *sems) for t, (kind, _) in enumerate(rider)]


def exchange(rider, name):
    n = len(rider)

    def body(*refs):
        tasks = _make_tasks(rider, refs[:n], refs[n:2 * n], refs[2 * n:])
        for task in tasks:
            task.start()
        for task in tasks:
            task.forward()
        for task in tasks:
            task.finish()

    return list(pl.pallas_call(
        body, name=name, out_shape=tuple(_task_shape(k, a) for k, a in rider),
        in_specs=[ANY] * n, out_specs=(ANY,) * n, scratch_shapes=_sem_shapes(n),
    )(*[a for _, a in rider]))


def _call(body, name, grid, in_specs, out_specs, out_shape, args, scratch=(), rider=()):
    in_specs, out_specs, out_shape = list(in_specs), tuple(out_specs), tuple(out_shape)
    n_in, n_out, n_scr, n_r = len(in_specs), len(out_specs), len(scratch), len(rider)
    if n_r:
        def kern(*refs):
            ins, r_in = refs[:n_in], refs[n_in:n_in + n_r]
            o0 = n_in + n_r
            outs, r_out = refs[o0:o0 + n_out], refs[o0 + n_out:o0 + n_out + n_r]
            s0 = o0 + n_out + n_r
            scr, sems = refs[s0:s0 + n_scr], refs[s0 + n_scr:]
            step = 0
            for a, n in enumerate(grid):
                step = step * n + pl.program_id(a)
            n_steps = 1
            for n in grid:
                n_steps *= n

            @pl.when(step == 0)
            def _():
                for task in _make_tasks(rider, r_in, r_out, sems):
                    task.start()
            body(*ins, *outs, *scr)

            @pl.when(step == n_steps - 1)
            def _():
                tasks = _make_tasks(rider, r_in, r_out, sems)
                for task in tasks:
                    task.forward()
                for task in tasks:
                    task.finish()
    else:
        kern = body
    res = pl.pallas_call(
        kern, name=name, grid=grid,
        in_specs=in_specs + [ANY] * n_r, out_specs=out_specs + (ANY,) * n_r,
        out_shape=out_shape + tuple(_task_shape(k, a) for k, a in rider),
        scratch_shapes=list(scratch) + (_sem_shapes(n_r) if n_r else []),
        compiler_params=pltpu.CompilerParams(dimension_semantics=("arbitrary",) * len(grid),
                                             vmem_limit_bytes=V7X_VMEM_LIMIT),
    )(*args, *[a for _, a in rider])
    return list(res[:n_out]), list(res[n_out:])


def _adamw(g, w, m, v):
    m = ADAM_B1 * m + (1.0 - ADAM_B1) * g
    v = ADAM_B2 * v + (1.0 - ADAM_B2) * (g * g)
    m_hat = m / (1.0 - ADAM_B1 ** ADAM_STEP)
    v_hat = v / (1.0 - ADAM_B2 ** ADAM_STEP)
    delta = -ADAM_LR * (m_hat / (jnp.sqrt(v_hat) + ADAM_EPS) + ADAM_WD * w)
    return delta, m, v


def sum_adamw(parts, w, m, v, name, rider=()):
    n_l, rows, cols = w.shape
    mult = BF16_SUBLANES if parts[0].dtype == BF16 else 8
    tr = _row_tile(rows, ADAM_ROWS, mult)
    n_i = rows // tr

    def body(*refs):
        part_refs = refs[:n_l]
        w_ref, m_ref, v_ref, g_out, d_out, m_out, v_out = refs[n_l:]
        layer = pl.program_id(0)
        for k in range(n_l):
            @pl.when(layer == k)
            def _(k=k):
                g = part_refs[k][0].astype(F32)
                for s in range(1, parts[k].shape[0]):
                    g = g + part_refs[k][s].astype(F32)
                delta, m_new, v_new = _adamw(g, w_ref[...], m_ref[...], v_ref[...])
                g_out[...] = g
                d_out[...] = delta
                m_out[...] = m_new
                v_out[...] = v_new

    part_specs = [pl.BlockSpec((parts[k].shape[0], tr, cols), functools.partial(
        lambda l, i, k: (0, jnp.where(l == k, i, 0), 0), k=k)) for k in range(n_l)]
    wspec = pl.BlockSpec((None, tr, cols), lambda l, i: (l, i, 0))
    shape = jax.ShapeDtypeStruct(w.shape, F32)
    return _call(body, name, (n_l, n_i), part_specs + [wspec] * 3, (wspec,) * 4, (shape,) * 4,
                 (*parts, w, m, v), rider=rider)


def sum_adamw_transposed(parts, w_t, m_t, v_t, name, rider=()):
    n_l, cols, rows = w_t.shape
    tr = LANES
    n_i = rows // tr
    starts = list(range(0, cols - LANES + 1, LANES))
    if starts[-1] + LANES < cols:
        starts.append(cols - LANES)

    def body(*refs):
        part_refs = refs[:n_l]
        w_ref, m_ref, v_ref, g_out, d_out, m_out, v_out = refs[n_l:]
        layer = pl.program_id(0)
        for k in range(n_l):
            @pl.when(layer == k)
            def _(k=k):
                for c0 in starts:
                    piece = pl.ds(c0, LANES)
                    g = part_refs[k][0, :, piece].astype(F32)
                    for s in range(1, parts[k].shape[0]):
                        g = g + part_refs[k][s, :, piece].astype(F32)
                    g = g.T
                    delta, m_new, v_new = _adamw(g, w_ref[piece, :], m_ref[piece, :], v_ref[piece, :])
                    g_out[piece, :] = g
                    d_out[piece, :] = delta
                    m_out[piece, :] = m_new
                    v_out[piece, :] = v_new

    part_specs = [pl.BlockSpec((parts[k].shape[0], tr, cols), functools.partial(
        lambda l, i, k: (0, jnp.where(l == k, i, 0), 0), k=k)) for k in range(n_l)]
    wspec = pl.BlockSpec((None, cols, tr), lambda l, i: (l, 0, i))
    shape = jax.ShapeDtypeStruct(w_t.shape, F32)
    return _call(body, name, (n_l, n_i), part_specs + [wspec] * 3, (wspec,) * 4, (shape,) * 4,
                 (*parts, w_t, m_t, v_t), rider=rider)


def pair_sum(g4, other, name):
    n_chip, _, rows, cols = g4.shape
    tr = _row_tile(rows, 512, BF16_SUBLANES)

    def body(core_ref, g_ref, o_ref, s_ref):
        s_ref[...] = (g_ref[...].astype(F32) + o_ref[...].astype(F32)).astype(BF16)

    blk = pl.BlockSpec((None, tr, cols), lambda k, i, core: (k, i, 0))
    return pl.pallas_call(
        body, name=name, out_shape=jax.ShapeDtypeStruct((n_chip, rows, cols), g4.dtype),
        grid_spec=pltpu.PrefetchScalarGridSpec(
            num_scalar_prefetch=1, grid=(n_chip, rows // tr),
            in_specs=[pl.BlockSpec((None, None, tr, cols), lambda k, i, core: (k, core[0], i, 0)), blk],
            out_specs=blk),
        compiler_params=pltpu.CompilerParams(dimension_semantics=("arbitrary", "arbitrary"),
                                             vmem_limit_bytes=V7X_VMEM_LIMIT),
    )(lax.axis_index("c").astype(jnp.int32).reshape(1), g4, other)


def sum_slots(parts, name):
    n, rows, cols = parts.shape

    def body(p_ref, o_ref):
        acc = p_ref[0]
        for s in range(1, n):
            acc = acc + p_ref[s]
        o_ref[...] = acc

    return pl.pallas_call(
        body, name=name, out_shape=jax.ShapeDtypeStruct((rows, cols), F32),
        in_specs=[pl.BlockSpec(memory_space=pltpu.VMEM)],
        out_specs=pl.BlockSpec(memory_space=pltpu.VMEM),
    )(parts)


def norm_fwd(h, g, name):
    seq, d = h.shape
    tm = min(ROW_TILE, seq)

    def body(h_ref, g_ref, o_ref):
        o_ref[...] = _rms(h_ref[...], g_ref[...])[0].astype(BF16)

    return _call(body, name, (seq // tm,),
                 [pl.BlockSpec((tm, d), lambda i: (i, 0)), pl.BlockSpec((1, d), lambda i: (0, 0))],
                 [pl.BlockSpec((tm, d), lambda i: (i, 0))],
                 [jax.ShapeDtypeStruct((seq, d), BF16)], (h, g))[0][0]


def proj_residual(a, w, res, name, g_next=None, rider=()):
    nb, seq, kb = a.shape
    d = w.shape[-1]
    tm = min(ROW_TILE, seq)
    with_norm = g_next is not None

    def body(a_ref, w_ref, r_ref, *rest):
        acc = r_ref[...]
        for b in range(nb):
            acc = acc + _nn(a_ref[b], w_ref[b])
        if with_norm:
            g_ref, o_ref, xn_ref = rest
            xn_ref[...] = _rms(acc, g_ref[...])[0].astype(BF16)
        else:
            (o_ref,) = rest
        o_ref[...] = acc

    row = pl.BlockSpec((tm, d), lambda i: (i, 0))
    in_specs = [pl.BlockSpec((nb, tm, kb), lambda i: (0, i, 0)),
                pl.BlockSpec((nb, kb, d), lambda i: (0, 0, 0)), row]
    args = [a, w, res]
    out_specs, out_shape = [row], [jax.ShapeDtypeStruct((seq, d), F32)]
    if with_norm:
        in_specs.append(pl.BlockSpec((1, d), lambda i: (0, 0)))
        args.append(g_next)
        out_specs.append(row)
        out_shape.append(jax.ShapeDtypeStruct((seq, d), BF16))
    outs, r_outs = _call(body, name, (seq // tm,), in_specs, out_specs, out_shape, args, rider=rider)
    return (outs[0], outs[1] if with_norm else None), r_outs


def proj_t_rms_bwd(du, w, h, g, dres, name, rider=()):
    nb, seq, wd = du.shape
    k = w.shape[1]
    tm = min(ROW_TILE, seq)

    def body(du_ref, w_ref, h_ref, g_ref, dr_ref, dh_ref, dg_ref):
        i = pl.program_id(0)
        dxn = _nt(du_ref[0], w_ref[0])
        for b in range(1, nb):
            dxn = dxn + _nt(du_ref[b], w_ref[b])
        _, xhat, rstd = _rms(h_ref[...], g_ref[...])
        dh, dg_rows = _rms_bwd(dxn, xhat, rstd, g_ref[...])
        dh_ref[...] = dr_ref[...] + dh

        @pl.when(i == 0)
        def _():
            dg_ref[...] = jnp.zeros_like(dg_ref)
        dg_ref[...] += jnp.sum(dg_rows, axis=0, keepdims=True)

    row = pl.BlockSpec((tm, k), lambda i: (i, 0))
    vec = pl.BlockSpec((1, k), lambda i: (0, 0))
    return _call(body, name, (seq // tm,),
                 [pl.BlockSpec((nb, tm, wd), lambda i: (0, i, 0)),
                  pl.BlockSpec((nb, k, wd), lambda i: (0, 0, 0), pipeline_mode=pl.Buffered(1)),
                  row, vec, row],
                 (row, vec),
                 (jax.ShapeDtypeStruct((seq, k), F32), jax.ShapeDtypeStruct((1, k), F32)),
                 (du, w, h, g, dres), rider=rider)


def mixer_fwd(xn, win3, cw, name, rider=()):
    seq, d = xn.shape
    tm = min(ROW_TILE_LARGE, seq)
    cc = min(MIXER_CHUNK, d)
    n_c, n_i = d // cc, seq // tm

    def body(x_ref, w_ref, cw_ref, u_ref, z_ref, carry):
        i = pl.program_id(1)

        @pl.when(i == 0)
        def _():
            carry[...] = jnp.zeros_like(carry)
        xb = x_ref[...]
        b = _nn(xb, w_ref[0])
        c = _nn(xb, w_ref[1])
        hh = _nn(xb, w_ref[2])
        p = c * hh
        w0, w1, w2 = _conv_taps(cw_ref)
        p1 = _shift_down(p, 1, [carry[7:8, :]])
        p2 = _shift_down(p, 2, [carry[6:7, :], carry[7:8, :]])
        q = w0 * p2 + w1 * p1 + w2 * p
        carry[...] = p[tm - 8:tm, :]
        u_ref[0] = b.astype(BF16)
        u_ref[1] = c.astype(BF16)
        u_ref[2] = hh.astype(BF16)
        u_ref[3] = q.astype(BF16)
        z_ref[...] = (b * q).astype(BF16)

    return _call(body, name, (n_c, n_i),
                 [pl.BlockSpec((tm, d), lambda c, i: (i, 0)),
                  pl.BlockSpec((3, d, cc), lambda c, i: (0, 0, c)),
                  pl.BlockSpec((3, cc), lambda c, i: (0, c))],
                 (pl.BlockSpec((4, tm, cc), lambda c, i: (0, i, c)),
                  pl.BlockSpec((tm, cc), lambda c, i: (i, c))),
                 (jax.ShapeDtypeStruct((4, seq, d), BF16), jax.ShapeDtypeStruct((seq, d), BF16)),
                 (xn, win3, cw), scratch=[pltpu.VMEM((8, cc), F32)], rider=rider)


def mixer_bwd(dh, wout, u4, z, xn, cw, name, rider=()):
    seq, d = xn.shape
    tm = min(ROW_TILE, seq)
    cc = min(MIXER_CHUNK, d)
    n_c, n_i = d // cc, seq // tm

    def body(dh_ref, wout_ref, u_ref, z_ref, x_ref, cw_ref,
             du_ref, dwin_ref, dwout_ref, dcw_ref, acc_in, acc_out, acc_cw, carry):
        i = pl.program_id(1)

        @pl.when(i == 0)
        def _():
            acc_in[...] = jnp.zeros_like(acc_in)
            acc_out[...] = jnp.zeros_like(acc_out)
            acc_cw[...] = jnp.zeros_like(acc_cw)
            carry[...] = jnp.zeros_like(carry)
        dhb = dh_ref[...].astype(BF16)
        dz = _nt(dhb, wout_ref[...])
        acc_out[...] += _tn(z_ref[...], dhb)
        b = u_ref[0].astype(F32)
        c = u_ref[1].astype(F32)
        hh = u_ref[2].astype(F32)
        q = u_ref[3].astype(F32)
        p = c * hh
        db = dz * q
        dq = dz * b
        w0, w1, w2 = _conv_taps(cw_ref)
        dq1 = _shift_up(dq, 1, [carry[0:1, :]])
        dq2 = _shift_up(dq, 2, [carry[0:1, :], carry[1:2, :]])
        dp = w2 * dq + w1 * dq1 + w0 * dq2
        carry[...] = dq[0:8, :]
        acc_cw[0:1, :] += jnp.sum(dq2 * p, axis=0, keepdims=True)
        acc_cw[1:2, :] += jnp.sum(dq1 * p, axis=0, keepdims=True)
        acc_cw[2:3, :] += jnp.sum(dq * p, axis=0, keepdims=True)
        dbb = db.astype(BF16)
        dcb = (dp * hh).astype(BF16)
        dhhb = (dp * c).astype(BF16)
        du_ref[0] = dbb
        du_ref[1] = dcb
        du_ref[2] = dhhb
        xb = x_ref[...]
        acc_in[0] += _tn(xb, dbb)
        acc_in[1] += _tn(xb, dcb)
        acc_in[2] += _tn(xb, dhhb)

        @pl.when(i == n_i - 1)
        def _():
            dwin_ref[...] = acc_in[...].astype(BF16)
            dwout_ref[...] = acc_out[...].astype(BF16)
            dcw_ref[...] = acc_cw[0:3, :]

    rev = lambda c, i: (n_i - 1 - i, 0)
    return _call(body, name, (n_c, n_i),
                 [pl.BlockSpec((tm, d), rev),
                  pl.BlockSpec((cc, d), lambda c, i: (c, 0)),
                  pl.BlockSpec((4, tm, cc), lambda c, i: (0, n_i - 1 - i, c)),
                  pl.BlockSpec((tm, cc), lambda c, i: (n_i - 1 - i, c)),
                  pl.BlockSpec((tm, d), rev),
                  pl.BlockSpec((3, cc), lambda c, i: (0, c))],
                 (pl.BlockSpec((3, tm, cc), lambda c, i: (0, n_i - 1 - i, c)),
                  pl.BlockSpec((3, d, cc), lambda c, i: (0, 0, c)),
                  pl.BlockSpec((cc, d), lambda c, i: (c, 0)),
                  pl.BlockSpec((3, cc), lambda c, i: (0, c))),
                 (jax.ShapeDtypeStruct((3, seq, d), BF16), jax.ShapeDtypeStruct((3, d, d), BF16),
                  jax.ShapeDtypeStruct((d, d), BF16), jax.ShapeDtypeStruct((3, d), F32)),
                 (dh, wout, u4, z, xn, cw),
                 scratch=[pltpu.VMEM((3, d, cc), F32), pltpu.VMEM((cc, d), F32),
                          pltpu.VMEM((8, cc), F32), pltpu.VMEM((8, cc), F32)], rider=rider)


def _silu_parts(cg):
    sg = 1.0 / (1.0 + jnp.exp(-cg))
    return sg, cg * sg


def ffn_fwd(xn, wup, fcw, name, rider=()):
    seq, d = xn.shape
    f8 = wup.shape[-1]
    half = N_DEV // 2
    tm = min(ROW_TILE_LARGE, seq)
    n_i = seq // tm

    def body(x_ref, wg_ref, wu_ref, cg_ref, cu_ref, up_ref, cv_ref, a_ref, carry):
        i = pl.program_id(1)

        @pl.when(i == 0)
        def _():
            carry[...] = jnp.zeros_like(carry)
        xb = x_ref[...]
        conv = []
        for s, (w_ref, t_ref) in enumerate(((wg_ref, cg_ref), (wu_ref, cu_ref))):
            u = _nn(xb, w_ref[...])
            up_ref[s] = u.astype(BF16)
            w0, w1, w2 = _conv_taps(t_ref)
            u1 = _shift_down(u, 1, [carry[s, 7:8, :]])
            u2 = _shift_down(u, 2, [carry[s, 6:7, :], carry[s, 7:8, :]])
            cv = w0 * u2 + w1 * u1 + w2 * u
            cv_ref[s] = cv.astype(BF16)
            conv.append(cv)
            carry[s] = u[tm - 8:tm, :]
        _, silu = _silu_parts(conv[0])
        a_ref[...] = (silu * conv[1]).astype(BF16)

    blk = pl.BlockSpec((2, None, tm, f8), lambda c, i: (0, c, i, 0))
    big = jax.ShapeDtypeStruct((2, half, seq, f8), BF16)
    return _call(body, name, (half, n_i),
                 [pl.BlockSpec((tm, d), lambda c, i: (i, 0)),
                  pl.BlockSpec((None, d, f8), lambda c, i: (c, 0, 0)),
                  pl.BlockSpec((None, d, f8), lambda c, i: (c + half, 0, 0)),
                  pl.BlockSpec((None, 3, f8), lambda c, i: (c, 0, 0)),
                  pl.BlockSpec((None, 3, f8), lambda c, i: (c + half, 0, 0))],
                 (blk, blk, pl.BlockSpec((None, tm, f8), lambda c, i: (c, i, 0))),
                 (big, big, jax.ShapeDtypeStruct((half, seq, f8), BF16)),
                 (xn, wup, wup, fcw, fcw), scratch=[pltpu.VMEM((2, 8, f8), F32)], rider=rider)


def ffn_bwd(dh, wdown, up2, cv2, act, xn, fcw, name, rider=()):
    seq, d = xn.shape
    f8 = up2.shape[-1]
    fb = wdown.shape[1]
    half = N_DEV // 2
    tm = min(ROW_TILE, seq)
    n_i = seq // tm

    def body(dh_ref, wd_ref, up_ref, cv_ref, a_ref, x_ref, cg_ref, cu_ref,
             dup_ref, dwup_ref, dwd_ref, dcw_ref, acc_up, acc_down, acc_cw, carry):
        i = pl.program_id(1)

        @pl.when(i == 0)
        def _():
            acc_up[...] = jnp.zeros_like(acc_up)
            acc_down[...] = jnp.zeros_like(acc_down)
            acc_cw[...] = jnp.zeros_like(acc_cw)
            carry[...] = jnp.zeros_like(carry)
        dhb = dh_ref[...].astype(BF16)
        da = _nt(dhb, wd_ref[...])
        acc_down[...] += _tn(a_ref[...], dhb)
        cg = cv_ref[0].astype(F32)
        cu = cv_ref[1].astype(F32)
        sg, silu = _silu_parts(cg)
        dcg = da * cu * (sg + silu * (1.0 - sg))
        dcu = da * silu
        xb = x_ref[...]
        for s, (dc, t_ref) in enumerate(((dcg, cg_ref), (dcu, cu_ref))):
            w0, w1, w2 = _conv_taps(t_ref)
            d1 = _shift_up(dc, 1, [carry[s, 0:1, :]])
            d2 = _shift_up(dc, 2, [carry[s, 0:1, :], carry[s, 1:2, :]])
            du = (w2 * dc + w1 * d1 + w0 * d2).astype(BF16)
            carry[s] = dc[0:8, :]
            u = up_ref[s].astype(F32)
            acc_cw[s, 0:1, :] += jnp.sum(d2 * u, axis=0, keepdims=True)
            acc_cw[s, 1:2, :] += jnp.sum(d1 * u, axis=0, keepdims=True)
            acc_cw[s, 2:3, :] += jnp.sum(dc * u, axis=0, keepdims=True)
            dup_ref[s] = du
            acc_up[s] += _tn(xb, du)

        @pl.when(i == n_i - 1)
        def _():
            dwup_ref[...] = acc_up[...].astype(BF16)
            dwd_ref[...] = acc_down[...].astype(BF16)
            dcw_ref[...] = acc_cw[:, 0:3, :]

    rev = lambda c, i: (n_i - 1 - i, 0)
    blk = pl.BlockSpec((2, None, tm, f8), lambda c, i: (0, c, n_i - 1 - i, 0))
    return _call(body, name, (half, n_i),
                 [pl.BlockSpec((tm, d), rev),
                  pl.BlockSpec((None, fb, d), lambda c, i: (c, 0, 0)),
                  blk, blk,
                  pl.BlockSpec((None, tm, f8), lambda c, i: (c, n_i - 1 - i, 0)),
                  pl.BlockSpec((tm, d), rev),
                  pl.BlockSpec((None, 3, f8), lambda c, i: (c, 0, 0)),
                  pl.BlockSpec((None, 3, f8), lambda c, i: (c + half, 0, 0))],
                 (blk,
                  pl.BlockSpec((2, None, d, f8), lambda c, i: (0, c, 0, 0)),
                  pl.BlockSpec((None, fb, d), lambda c, i: (c, 0, 0)),
                  pl.BlockSpec((2, None, 3, f8), lambda c, i: (0, c, 0, 0))),
                 (jax.ShapeDtypeStruct((2, half, seq, f8), BF16),
                  jax.ShapeDtypeStruct((2, half, d, f8), BF16),
                  jax.ShapeDtypeStruct((half, fb, d), BF16),
                  jax.ShapeDtypeStruct((2, half, 3, f8), F32)),
                 (dh, wdown, up2, cv2, act, xn, fcw, fcw),
                 scratch=[pltpu.VMEM((2, d, f8), F32), pltpu.VMEM((fb, d), F32),
                          pltpu.VMEM((2, 8, f8), F32), pltpu.VMEM((2, 8, f8), F32)], rider=rider)


def q_fwd(xn, wdq, gq, wuq, cos, sin, name, rider=()):
    seq, d = xn.shape
    rank = wdq.shape[-1]
    tm = min(ROW_TILE, seq)

    def body(x_ref, wdq_ref, gq_ref, wuq_ref, cos_ref, sin_ref, q_ref):
        qc = _nn(x_ref[...], wdq_ref[...])
        qn = _rms(qc, gq_ref[...])[0].astype(BF16)
        for hd in range(N_HEADS):
            qh = _nn(qn, wuq_ref[hd])
            qr = _rope_fwd(qh[:, NOPE:QK], cos_ref[...], sin_ref[...])
            q_ref[hd, :, 0:NOPE] = (qh[:, 0:NOPE] * ATTN_SCALE).astype(BF16)
            q_ref[hd, :, NOPE:QK] = (qr * ATTN_SCALE).astype(BF16)

    rope = pl.BlockSpec((tm, ROPE_PAD), lambda i: (i, 0))
    return _call(body, name, (seq // tm,),
                 [pl.BlockSpec((tm, d), lambda i: (i, 0)),
                  pl.BlockSpec((d, rank), lambda i: (0, 0)),
                  pl.BlockSpec((1, rank), lambda i: (0, 0)),
                  pl.BlockSpec((N_HEADS, rank, QK), lambda i: (0, 0, 0)), rope, rope],
                 [pl.BlockSpec((N_HEADS, tm, QK), lambda i: (0, i, 0))],
                 [jax.ShapeDtypeStruct((N_HEADS, seq, QK), BF16)],
                 (xn, wdq, gq, wuq, cos, sin), rider=rider)


def q_bwd(dq, xn, h, g, dres, wdq, gq, wuq, cos, sin, name, rider=()):
    seq, d = xn.shape
    rank = wdq.shape[-1]
    tm = min(ROW_TILE, seq)
    n_i = seq // tm

    def body(dq_ref, x_ref, h_ref, g_ref, dr_ref, wdq_ref, gq_ref, wuq_ref, cos_ref, sin_ref,
             dh_ref, dwuq_ref, dwdq_ref, dgq_ref, dg_ref, acc_uq, acc_dq):
        i = pl.program_id(0)

        @pl.when(i == 0)
        def _():
            acc_uq[...] = jnp.zeros_like(acc_uq)
            acc_dq[...] = jnp.zeros_like(acc_dq)
            dgq_ref[...] = jnp.zeros_like(dgq_ref)
            dg_ref[...] = jnp.zeros_like(dg_ref)
        xb = x_ref[...]
        qc = _nn(xb, wdq_ref[...])
        qn, qhat, qrstd = _rms(qc, gq_ref[...])
        qnb = qn.astype(BF16)
        dqn = jnp.zeros((tm, rank), F32)
        for hd in range(N_HEADS):
            dnope = (dq_ref[hd, :, 0:NOPE].astype(F32) * ATTN_SCALE).astype(BF16)
            drope = _rope_bwd(dq_ref[hd, :, NOPE:QK].astype(F32) * ATTN_SCALE, cos_ref[...], sin_ref[...])
            draw = jnp.concatenate([dnope, drope.astype(BF16)], axis=1)
            dqn = dqn + _nt(draw, wuq_ref[hd])
            acc_uq[hd] += _tn(qnb, draw)
        dqc, dg_rows = _rms_bwd(dqn, qhat, qrstd, gq_ref[...])
        dgq_ref[...] += jnp.sum(dg_rows, axis=0, keepdims=True)
        dqcb = dqc.astype(BF16)
        acc_dq[...] += _tn(xb, dqcb)
        _, xhat, rstd = _rms(h_ref[...], g_ref[...])
        dh, dg_rows = _rms_bwd(_nt(dqcb, wdq_ref[...]), xhat, rstd, g_ref[...])
        dh_ref[...] = dr_ref[...] + dh
        dg_ref[...] += jnp.sum(dg_rows, axis=0, keepdims=True)

        @pl.when(i == n_i - 1)
        def _():
            dwuq_ref[...] = acc_uq[...].astype(BF16)
            dwdq_ref[...] = acc_dq[...].astype(BF16)

    rope = pl.BlockSpec((tm, ROPE_PAD), lambda i: (i, 0))
    row = pl.BlockSpec((tm, d), lambda i: (i, 0))
    vec = pl.BlockSpec((1, d), lambda i: (0, 0))
    return _call(body, name, (n_i,),
                 [pl.BlockSpec((N_HEADS, tm, QK), lambda i: (0, i, 0)), row, row, vec, row,
                  pl.BlockSpec((d, rank), lambda i: (0, 0)),
                  pl.BlockSpec((1, rank), lambda i: (0, 0)),
                  pl.BlockSpec((N_HEADS, rank, QK), lambda i: (0, 0, 0)), rope, rope],
                 (row,
                  pl.BlockSpec((N_HEADS, rank, QK), lambda i: (0, 0, 0)),
                  pl.BlockSpec((d, rank), lambda i: (0, 0)),
                  pl.BlockSpec((1, rank), lambda i: (0, 0)), vec),
                 (jax.ShapeDtypeStruct((seq, d), F32),
                  jax.ShapeDtypeStruct((N_HEADS, rank, QK), BF16),
                  jax.ShapeDtypeStruct((d, rank), BF16),
                  jax.ShapeDtypeStruct((1, rank), F32),
                  jax.ShapeDtypeStruct((1, d), F32)),
                 (dq, xn, h, g, dres, wdq, gq, wuq, cos, sin),
                 scratch=[pltpu.VMEM((N_HEADS, rank, QK), F32), pltpu.VMEM((d, rank), F32)],
                 rider=rider)


def kv_fwd(h, g, wdkv, gkv, wukv, cos, sin, name, rider=()):
    seq, d = h.shape
    tm = min(ROW_TILE, seq)
    wk = KV_RANK + ROPE_PAD

    def body(h_ref, g_ref, wdkv_ref, gkv_ref, wukv_ref, cos_ref, sin_ref, k_ref, v_ref, c_ref):
        xk = _rms(h_ref[...], g_ref[...])[0].astype(BF16)
        ckv = _nn(xk, wdkv_ref[...])
        c_kv = ckv[:, 0:KV_RANK]
        c_ref[...] = c_kv
        kr = _rope_fwd(ckv[:, KV_RANK:wk], cos_ref[...], sin_ref[...]).astype(BF16)
        ckn = _rms(c_kv, gkv_ref[...])[0].astype(BF16)
        for hd in range(N_HEADS):
            kvh = _nn(ckn, wukv_ref[hd])
            k_ref[hd, :, 0:NOPE] = kvh[:, 0:NOPE].astype(BF16)
            k_ref[hd, :, NOPE:QK] = kr
            v_ref[hd] = kvh[:, NOPE:NOPE + VDIM].astype(BF16)

    rope = pl.BlockSpec((tm, ROPE_PAD), lambda i: (i, 0))
    return _call(body, name, (seq // tm,),
                 [pl.BlockSpec((tm, d), lambda i: (i, 0)),
                  pl.BlockSpec((1, d), lambda i: (0, 0)),
                  pl.BlockSpec((d, wk), lambda i: (0, 0)),
                  pl.BlockSpec((1, KV_RANK), lambda i: (0, 0)),
                  pl.BlockSpec((N_HEADS, KV_RANK, NOPE + VDIM), lambda i: (0, 0, 0)), rope, rope],
                 (pl.BlockSpec((N_HEADS, tm, QK), lambda i: (0, i, 0)),
                  pl.BlockSpec((N_HEADS, tm, VDIM), lambda i: (0, i, 0)),
                  pl.BlockSpec((tm, KV_RANK), lambda i: (i, 0))),
                 (jax.ShapeDtypeStruct((N_HEADS, seq, QK), BF16),
                  jax.ShapeDtypeStruct((N_HEADS, seq, VDIM), BF16),
                  jax.ShapeDtypeStruct((seq, KV_RANK), F32)),
                 (h, g, wdkv, gkv, wukv, cos, sin), rider=rider)


def kv_bwd(dks, dvs, c_kv, h, g, dres, wdkv, gkv, wukv, cos, sin, name, rider=()):
    seq, d = h.shape
    tm = min(ROW_TILE, seq)
    n_i = seq // tm
    wk = KV_RANK + ROPE_PAD
    n_b = len(dks)

    def body(*refs):
        dk_refs = refs[:n_b]
        dv_refs = refs[n_b:2 * n_b]
        (c_ref, h_ref, g_ref, dr_ref, wdkv_ref, gkv_ref, wukv_ref, cos_ref, sin_ref,
         dh_ref, dwukv_ref, dwdkv_ref, dgkv_ref, dg_ref, acc_ukv, acc_dkv) = refs[2 * n_b:]
        i = pl.program_id(0)

        @pl.when(i == 0)
        def _():
            acc_ukv[...] = jnp.zeros_like(acc_ukv)
            acc_dkv[...] = jnp.zeros_like(acc_dkv)
            dgkv_ref[...] = jnp.zeros_like(dgkv_ref)
            dg_ref[...] = jnp.zeros_like(dg_ref)
        ckn, chat, crstd = _rms(c_ref[...], gkv_ref[...])
        cknb = ckn.astype(BF16)
        dckn = jnp.zeros((tm, KV_RANK), F32)
        dkr = jnp.zeros((tm, ROPE_PAD), F32)
        for hd in range(N_HEADS):
            dk = dk_refs[0][hd].astype(F32)
            dv = dv_refs[0][hd].astype(F32)
            for j in range(1, n_b):
                dk = dk + dk_refs[j][hd].astype(F32)
                dv = dv + dv_refs[j][hd].astype(F32)
            dkr = dkr + dk[:, NOPE:QK]
            dkvh = jnp.concatenate([dk[:, 0:NOPE].astype(BF16), dv.astype(BF16)], axis=1)
            dckn = dckn + _nt(dkvh, wukv_ref[hd])
            acc_ukv[hd] += _tn(cknb, dkvh)
        dc_kv, dg_rows = _rms_bwd(dckn, chat, crstd, gkv_ref[...])
        dgkv_ref[...] += jnp.sum(dg_rows, axis=0, keepdims=True)
        dkr_raw = _rope_bwd(dkr, cos_ref[...], sin_ref[...])
        dckv = jnp.concatenate([dc_kv.astype(BF16), dkr_raw.astype(BF16)], axis=1)
        xk, xhat, rstd = _rms(h_ref[...], g_ref[...])
        acc_dkv[...] += _tn(xk.astype(BF16), dckv)
        dh, dg_rows = _rms_bwd(_nt(dckv, wdkv_ref[...]), xhat, rstd, g_ref[...])
        dh_ref[...] = dr_ref[...] + dh
        dg_ref[...] += jnp.sum(dg_rows, axis=0, keepdims=True)

        @pl.when(i == n_i - 1)
        def _():
            dwukv_ref[...] = acc_ukv[...].astype(BF16)
            dwdkv_ref[...] = acc_dkv[...].astype(BF16)

    kspec = pl.BlockSpec((N_HEADS, tm, QK), lambda i: (0, i, 0))
    vspec = pl.BlockSpec((N_HEADS, tm, VDIM), lambda i: (0, i, 0))
    rope = pl.BlockSpec((tm, ROPE_PAD), lambda i: (i, 0))
    row = pl.BlockSpec((tm, d), lambda i: (i, 0))
    vec = pl.BlockSpec((1, d), lambda i: (0, 0))
    return _call(body, name, (n_i,),
                 [kspec] * n_b + [vspec] * n_b + [
                     pl.BlockSpec((tm, KV_RANK), lambda i: (i, 0)), row, vec, row,
                     pl.BlockSpec((d, wk), lambda i: (0, 0)),
                     pl.BlockSpec((1, KV_RANK), lambda i: (0, 0)),
                     pl.BlockSpec((N_HEADS, KV_RANK, NOPE + VDIM), lambda i: (0, 0, 0)), rope, rope],
                 (row,
                  pl.BlockSpec((N_HEADS, KV_RANK, NOPE + VDIM), lambda i: (0, 0, 0)),
                  pl.BlockSpec((d, wk), lambda i: (0, 0)),
                  pl.BlockSpec((1, KV_RANK), lambda i: (0, 0)), vec),
                 (jax.ShapeDtypeStruct((seq, d), F32),
                  jax.ShapeDtypeStruct((N_HEADS, KV_RANK, NOPE + VDIM), BF16),
                  jax.ShapeDtypeStruct((d, wk), BF16),
                  jax.ShapeDtypeStruct((1, KV_RANK), F32),
                  jax.ShapeDtypeStruct((1, d), F32)),
                 (*dks, *dvs, c_kv, h, g, dres, wdkv, gkv, wukv, cos, sin),
                 scratch=[pltpu.VMEM((N_HEADS, KV_RANK, NOPE + VDIM), F32), pltpu.VMEM((d, wk), F32)],
                 rider=rider)


def o_bwd(dh, o, wo, name, rider=()):
    seq, d = dh.shape
    hv = o.shape[1]
    tm = min(ROW_TILE, seq)
    n_i = seq // tm

    def body(dh_ref, o_ref, wo_ref, do_ref, dwo_ref, acc):
        i = pl.program_id(0)

        @pl.when(i == 0)
        def _():
            acc[...] = jnp.zeros_like(acc)
        dhb = dh_ref[...].astype(BF16)
        do_ref[...] = _nt(dhb, wo_ref[...]).astype(BF16)
        acc[...] += _tn(o_ref[...], dhb)

        @pl.when(i == n_i - 1)
        def _():
            dwo_ref[...] = acc[...].astype(BF16)

    return _call(body, name, (n_i,),
                 [pl.BlockSpec((tm, d), lambda i: (i, 0)),
                  pl.BlockSpec((tm, hv), lambda i: (i, 0)),
                  pl.BlockSpec((hv, d), lambda i: (0, 0))],
                 (pl.BlockSpec((tm, hv), lambda i: (i, 0)),
                  pl.BlockSpec((hv, d), lambda i: (0, 0))),
                 (jax.ShapeDtypeStruct((seq, hv), BF16), jax.ShapeDtypeStruct((hv, d), BF16)),
                 (dh, o, wo), scratch=[pltpu.VMEM((hv, d), F32)], rider=rider)


def _mask_diagonal(s):
    row = lax.broadcasted_iota(jnp.int32, s.shape, 0)
    col = lax.broadcasted_iota(jnp.int32, s.shape, 1)
    return jnp.where(col <= row, s, NEG_BIG)


def attn_fwd(q, k, v, name, rider=()):
    _, seq, _ = q.shape
    t = min(ATTN_TILE, seq // 2)
    n_pair = seq // (2 * t)

    def body(q_ref, k_ref, v_ref, o_ref, lse_ref):
        qi = pl.program_id(1)
        q_a = q_ref[0:t, :]
        q_b = q_ref[t:2 * t, :]

        def rows(j):
            return pl.ds(pl.multiple_of(j * t, t), t)

        def update(qx, kb, vb, state, diagonal=False):
            m, l, acc = state
            s = _nt(qx, kb)
            if diagonal:
                s = _mask_diagonal(s)
            m_new = jnp.maximum(m, jnp.max(s, axis=1, keepdims=True))
            p = jnp.exp(s - m_new)
            alpha = jnp.exp(m - m_new)
            l = alpha * l + jnp.sum(p, axis=1, keepdims=True)
            acc = alpha * acc + _nn(p.astype(BF16), vb)
            return m_new, l, acc

        def step(j, carry):
            both = pl.ds(pl.multiple_of(j * 2 * t, 2 * t), 2 * t)
            kb, vb = k_ref[both, :], v_ref[both, :]
            return update(q_a, kb, vb, carry[0:3]) + update(q_b, kb, vb, carry[3:6])

        init = (jnp.full((t, 1), NEG_BIG, F32), jnp.zeros((t, 1), F32), jnp.zeros((t, VDIM), F32))
        carry = lax.fori_loop(0, qi, step, init + init)
        k0, v0 = k_ref[rows(2 * qi), :], v_ref[rows(2 * qi), :]
        k1, v1 = k_ref[rows(2 * qi + 1), :], v_ref[rows(2 * qi + 1), :]
        state_a = update(q_a, k0, v0, carry[0:3], diagonal=True)
        state_b = update(q_b, k1, v1, update(q_b, k0, v0, carry[3:6]), diagonal=True)
        for half, (m, l, acc) in enumerate((state_a, state_b)):
            o_ref[half * t:(half + 1) * t, :] = (acc / l).astype(BF16)
            lse_ref[half * t:(half + 1) * t, :] = jnp.broadcast_to(m + jnp.log(l), (t, LANES))

    return _call(body, name, (N_HEADS, n_pair),
                 [pl.BlockSpec((None, 2 * t, QK), lambda h, i: (h, i, 0)),
                  pl.BlockSpec((None, seq, QK), lambda h, i: (h, 0, 0)),
                  pl.BlockSpec((None, seq, VDIM), lambda h, i: (h, 0, 0))],
                 (pl.BlockSpec((2 * t, VDIM), lambda h, i: (i, h)),
                  pl.BlockSpec((None, 2 * t, LANES), lambda h, i: (h, i, 0))),
                 (jax.ShapeDtypeStruct((seq, N_HEADS * VDIM), BF16),
                  jax.ShapeDtypeStruct((N_HEADS, seq, LANES), F32)),
                 (q, k, v), rider=rider)


def attn_bwd(q, k, v, o, do, lse, name, rider=()):
    _, seq, _ = q.shape
    t = min(ATTN_TILE, seq // 2)
    n_q = seq // t
    n_pair = n_q // 2

    def body(q_ref, k_ref, v_ref, o_ref, do_ref, lse_ref, dq_ref, dk_ref, dv_ref,
             dq_acc, dk_acc, dv_acc):
        kj = pl.program_id(1)

        @pl.when(kj == 0)
        def _():
            dq_acc[...] = jnp.zeros_like(dq_acc)
        halves = (slice(0, t), slice(t, 2 * t))

        def block(i, masks, n_rows=t):
            rows = pl.ds(pl.multiple_of(i * n_rows, n_rows), n_rows)
            qb = q_ref[rows, :]
            dob = do_ref[rows, :]
            lse_col = lse_ref[rows, 0:1]
            delta = jnp.sum(dob.astype(F32) * o_ref[rows, :].astype(F32), axis=1, keepdims=True)
            dq, out = None, {}
            for x, diagonal in enumerate(masks):
                if diagonal is None:
                    continue
                kb, vb = k_ref[halves[x], :], v_ref[halves[x], :]
                s = _nt(qb, kb)
                if diagonal:
                    s = _mask_diagonal(s)
                p = jnp.exp(s - lse_col)
                ds = (p * (_nt(dob, vb) - delta)).astype(BF16)
                out[x] = (_tn(p.astype(BF16), dob), _tn(ds, qb))
                part = _nn(ds, kb)
                dq = part if dq is None else dq + part
            dq_acc[rows, :] += dq
            return out

        first = block(2 * kj, (True, None))
        second = block(2 * kj + 1, (False, True))
        dv_acc[halves[0], :] = first[0][0] + second[0][0]
        dk_acc[halves[0], :] = first[0][1] + second[0][1]
        dv_acc[halves[1], :] = second[1][0]
        dk_acc[halves[1], :] = second[1][1]

        def step(i, carry):
            out = block(i, (False, False), n_rows=2 * t)
            for x in (0, 1):
                dv_acc[halves[x], :] += out[x][0]
                dk_acc[halves[x], :] += out[x][1]
            return carry

        lax.fori_loop(kj + 1, n_pair, step, 0)
        dk_ref[...] = dk_acc[...].astype(BF16)
        dv_ref[...] = dv_acc[...].astype(BF16)

        @pl.when(kj == n_pair - 1)
        def _():
            dq_ref[...] = dq_acc[...].astype(BF16)

    head_rows = pl.BlockSpec((seq, VDIM), lambda h, j: (0, h))
    return _call(body, name, (N_HEADS, n_pair),
                 [pl.BlockSpec((None, seq, QK), lambda h, j: (h, 0, 0)),
                  pl.BlockSpec((None, 2 * t, QK), lambda h, j: (h, j, 0)),
                  pl.BlockSpec((None, 2 * t, VDIM), lambda h, j: (h, j, 0)),
                  head_rows, head_rows,
                  pl.BlockSpec((None, seq, LANES), lambda h, j: (h, 0, 0))],
                 (pl.BlockSpec((None, seq, QK), lambda h, j: (h, 0, 0)),
                  pl.BlockSpec((None, 2 * t, QK), lambda h, j: (h, j, 0)),
                  pl.BlockSpec((None, 2 * t, VDIM), lambda h, j: (h, j, 0))),
                 (jax.ShapeDtypeStruct((N_HEADS, seq, QK), BF16),
                  jax.ShapeDtypeStruct((N_HEADS, seq, QK), BF16),
                  jax.ShapeDtypeStruct((N_HEADS, seq, VDIM), BF16)),
                 (q, k, v, o, do, lse),
                 scratch=[pltpu.VMEM((seq, QK), F32), pltpu.VMEM((2 * t, QK), F32),
                          pltpu.VMEM((2 * t, VDIM), F32)], rider=rider)


def loss_head(h, g, target, name):
    seq, d = h.shape
    tm = min(ROW_TILE, seq)

    def body(h_ref, g_ref, t_ref, l_ref, dh_ref, dg_ref):
        i = pl.program_id(0)

        @pl.when(i == 0)
        def _():
            l_ref[...] = jnp.zeros_like(l_ref)
            dg_ref[...] = jnp.zeros_like(dg_ref)
        y, xhat, rstd = _rms(h_ref[...], g_ref[...])
        diff = y - t_ref[...]
        l_ref[...] += jnp.sum(jnp.sum(diff * diff, axis=1, keepdims=True), axis=0, keepdims=True)
        dh, dg_rows = _rms_bwd(diff * (1.0 / d), xhat, rstd, g_ref[...])
        dh_ref[...] = dh
        dg_ref[...] += jnp.sum(dg_rows, axis=0, keepdims=True)

    row = pl.BlockSpec((tm, d), lambda i: (i, 0))
    vec = pl.BlockSpec((1, d), lambda i: (0, 0))
    return _call(body, name, (seq // tm,), [row, vec, row],
                 (pl.BlockSpec((1, LANES), lambda i: (0, 0)), row, vec),
                 (jax.ShapeDtypeStruct((1, LANES), F32), jax.ShapeDtypeStruct((seq, d), F32),
                  jax.ShapeDtypeStruct((1, d), F32)),
                 (h, g, target))[0]


def _pack(parts):
    rows = []
    for p in parts:
        flat = p.reshape(-1)
        n_rows = -(-flat.shape[0] // (8 * LANES)) * 8
        flat = jnp.pad(flat, (0, n_rows * LANES - flat.shape[0]))
        rows.append(flat.reshape(n_rows, LANES))
    return jnp.concatenate(rows, axis=0)


def _unpack(packed, shapes):
    lead = packed.shape[:-2]
    out, r0 = [], 0
    for shape in shapes:
        size = 1
        for s in shape:
            size *= s
        n_rows = -(-size // (8 * LANES)) * 8
        part = packed[..., r0:r0 + n_rows, :].reshape(lead + (n_rows * LANES,))
        out.append(part[..., :size].reshape(lead + tuple(shape)))
        r0 += n_rows
    return out


FWD_RIDERS = {
    "mixer_fwd0": [("ffn_w_up", 0)],
    "ffn_fwd0": [("ffn_w_down", 0), ("a_w_in", 1), ("a_w_out", 1)],
    "ffn_out0": [("ffn_w_down", 1)],
    "mixer_fwd1": [("ffn_w_up", 1)],
    "mixer_out1": [("w_dkv", 0), ("w_ukv", 0), ("b_w_dq", 0), ("b_w_uq", 0)],
    "ffn_fwd1": [("ffn_w_up", 2), ("b_w_o", 0)],
    "ffn_out1": [("ffn_w_down", 2)],
    "attn_fwd0": [("ffn_w_up", 3), ("ffn_w_down", 3), ("b_w_dq", 1), ("b_w_uq", 1), ("b_w_o", 1)],
}
BWD_RIDERS = {
    "ffn_in_bwd3": [("ffn_w_down", 3)],
    "attn_bwd1": [("ffn_w_up", 3), ("b_w_o", 1)],
    "ffn_bwd2": [("b_w_uq", 1), ("b_w_dq", 1)],
    "ffn_in_bwd2": [("ffn_w_down", 2)],
    "attn_bwd0": [("ffn_w_up", 2), ("b_w_o", 0)],
    "ffn_bwd1": [("b_w_uq", 0), ("b_w_dq", 0), ("w_ukv", 0), ("w_dkv", 0)],
    "ffn_in_bwd1": [("ffn_w_down", 1), ("ffn_w_up", 1, "pair")],
    "ffn_bwd0": [("ffn_w_up", 1, "chip"), ("a_w_in", 1), ("a_w_out", 1)],
    "ffn_in_bwd0": [("ffn_w_down", 0), ("ffn_w_up", 0, "pair")],
    "mixer_bwd0": [("ffn_w_up", 0, "chip")],
    "mixer_in_bwd0": [("a_w_out", 0), ("a_w_in", 0, "pair")],
    "adamw_a_w_out": [("a_w_in", 0, "chip")],
}


def kernel(x, a_mix_norm, a_w_in, a_conv, a_w_out, b_mix_norm, b_w_dq, b_q_norm, b_w_uq, b_w_o, kv_in_norm, w_dkv, kv_norm, w_ukv, ffn_norm, ffn_w_up, ffn_conv, ffn_w_down, final_norm, loss_target, m_a_mix_norm, m_a_w_in, m_a_conv, m_a_w_out, m_b_mix_norm, m_b_w_dq, m_b_q_norm, m_b_w_uq, m_b_w_o, m_kv_in_norm, m_w_dkv, m_kv_norm, m_w_ukv, m_ffn_norm, m_ffn_w_up, m_ffn_conv, m_ffn_w_down, m_final_norm, v_a_mix_norm, v_a_w_in, v_a_conv, v_a_w_out, v_b_mix_norm, v_b_w_dq, v_b_q_norm, v_b_w_uq, v_b_w_o, v_kv_in_norm, v_w_dkv, v_kv_norm, v_w_ukv, v_ffn_norm, v_ffn_w_up, v_ffn_conv, v_ffn_w_down, v_final_norm):
    seq, d = x.shape[1], x.shape[2]
    me = 4 * lax.axis_index("x") + 2 * lax.axis_index("y") + lax.axis_index("c")
    h0 = x.reshape(seq, d)
    target = loss_target.reshape(seq, d)
    cos, sin = _rope_tables(seq)
    rank = b_w_dq.shape[-1]
    f8 = ffn_w_up.shape[-1]
    fd = ffn_w_down.shape[1]
    dshard = a_w_out.shape[1]
    hv = N_HEADS * VDIM

    shards = {"a_w_in": a_w_in, "a_w_out": a_w_out, "b_w_dq": b_w_dq, "b_w_uq": b_w_uq,
              "b_w_o": b_w_o, "w_dkv": w_dkv[None], "w_ukv": w_ukv[None],
              "ffn_w_up": ffn_w_up, "ffn_w_down": ffn_w_down}

    def relayout(name, g):
        if name == "a_w_in":
            w = jnp.transpose(g, (1, 0, 2)).reshape(d, 3, d)
            return jnp.transpose(w, (1, 0, 2))
        if name == "a_w_out":
            return g.reshape(d, d)
        if name == "b_w_dq":
            return g.reshape(d, rank)
        if name == "b_w_uq":
            return jnp.pad(g, ((0, 0), (0, 0), (0, QK - NOPE - ROPE)))
        if name == "b_w_o":
            return g.reshape(hv, d)
        if name == "w_dkv":
            return jnp.pad(g.reshape(d, KV_RANK + ROPE), ((0, 0), (0, ROPE_PAD - ROPE)))
        if name == "ffn_w_down":
            return g.reshape(N_DEV // 2, 2 * fd, d)
        return g

    weights = {}

    def ag_rider(host):
        return [("ag", shards[n][l].astype(BF16)) for n, l in FWD_RIDERS.get(host, [])]

    def ag_done(host, outs):
        for (n, l), g in zip(FWD_RIDERS.get(host, []), outs):
            weights[n, l] = relayout(n, g)

    small_shapes = [a_mix_norm.shape, a_conv.shape, ffn_conv.shape]
    first = exchange([("ag", a_w_in[0].astype(BF16)), ("ag", a_w_out[0].astype(BF16)),
                      ("ag", _pack([a_mix_norm, a_conv, ffn_conv]))], "ag_first")
    weights["a_w_in", 0] = relayout("a_w_in", first[0])
    weights["a_w_out", 0] = relayout("a_w_out", first[1])
    s_mix, s_aconv, s_fconv = _unpack(first[2], small_shapes)
    a_gain = jnp.transpose(s_mix, (1, 0, 2)).reshape(N_A, d)
    a_cw = jnp.transpose(s_aconv, (1, 2, 0, 3)).reshape(N_A, 3, d)
    f_cw = jnp.transpose(s_fconv, (1, 0, 2, 3))

    def mixer_gain(layer):
        if layer >= DEPTH:
            return None
        return a_gain[layer][None] if layer < N_A else b_mix_norm[layer - N_A][None]

    saved = {}
    h = h0
    xn = norm_fwd(h, mixer_gain(0), "norm_first")
    kv = None
    for layer in range(DEPTH):
        saved["hm", layer], saved["xm", layer] = h, xn
        if layer < N_A:
            name = f"mixer_fwd{layer}"
            (u4, z), r = mixer_fwd(xn, weights["a_w_in", layer], a_cw[layer], name, rider=ag_rider(name))
            ag_done(name, r)
            saved["mix", layer] = (u4, z)
            name = f"mixer_out{layer}"
            (h, xn), r = proj_residual(z[None], weights["a_w_out", layer][None], h, name,
                                       g_next=ffn_norm[layer][None], rider=ag_rider(name))
            ag_done(name, r)
        else:
            j = layer - N_A
            name = f"q_fwd{j}"
            (q,), r = q_fwd(xn, weights["b_w_dq", j], b_q_norm[j][None], weights["b_w_uq", j],
                            cos, sin, name, rider=ag_rider(name))
            ag_done(name, r)
            name = f"attn_fwd{j}"
            (o, lse), r = attn_fwd(q, kv[0], kv[1], name, rider=ag_rider(name))
            ag_done(name, r)
            saved["attn", layer] = (q, o, lse)
            name = f"attn_out{j}"
            (h, xn), r = proj_residual(o[None], weights["b_w_o", j][None], h, name,
                                       g_next=ffn_norm[layer][None], rider=ag_rider(name))
            ag_done(name, r)
        saved["hf", layer], saved["xf", layer] = h, xn
        name = f"ffn_fwd{layer}"
        (up2, cv2, act), r = ffn_fwd(xn, weights["ffn_w_up", layer], f_cw[layer], name, rider=ag_rider(name))
        ag_done(name, r)
        saved["ffn", layer] = (up2, cv2, act)
        name = f"ffn_out{layer}"
        (h, xn), r = proj_residual(act, weights["ffn_w_down", layer], h, name,
                                   g_next=mixer_gain(layer + 1), rider=ag_rider(name))
        ag_done(name, r)
        if layer == N_A - 1:
            (k_all, v_all, c_kv), r = kv_fwd(h, kv_in_norm[None], weights["w_dkv", 0], kv_norm[None],
                                             weights["w_ukv", 0], cos, sin, "kv_fwd",
                                             rider=ag_rider("kv_fwd"))
            ag_done("kv_fwd", r)
            kv = (k_all, v_all, c_kv)

    sq_err, dh, d_final = loss_head(h, final_norm[None], target, "loss_head")
    loss = lax.psum(sq_err[0, 0] * (0.5 / d), ("x", "y", "c"))

    grads = {}
    parts = {}

    pair_sums = {}

    def by_chip(g):
        return g.reshape((N_DEV // 2, 2) + g.shape[1:])

    def rs_rider(host):
        tasks = []
        for key in BWD_RIDERS.get(host, []):
            if len(key) == 2:
                tasks.append(("rs", grads[key]))
            elif key[2] == "pair":
                tasks.append(("rs_pair", by_chip(grads[key[:2]])))
            else:
                tasks.append(("rs_chip", pair_sums[key[:2]]))
        return tasks

    def rs_done(host, outs):
        for key, p in zip(BWD_RIDERS.get(host, []), outs):
            if len(key) == 3 and key[2] == "pair":
                pair_sums[key[:2]] = pair_sum(by_chip(grads[key[:2]]), p, f"pair_sum_{key[0]}{key[1]}")
            else:
                parts[key[:2]] = p

    d_ffn_norm = [None] * DEPTH
    d_fconv = [None] * DEPTH
    d_a_gain = [None] * N_A
    d_aconv = [None] * N_A
    d_b_gain = [None] * N_B
    d_q_gain = [None] * N_B
    dks, dvs = [], []
    for layer in reversed(range(DEPTH)):
        if layer == N_A - 1:
            hk = saved["hm", layer + 1]
            (dh, dwukv, dwdkv, d_kv_gain, d_kvin_gain), r = kv_bwd(
                dks, dvs, kv[2], hk, kv_in_norm[None], dh, weights["w_dkv", 0], kv_norm[None],
                weights["w_ukv", 0], cos, sin, "kv_bwd", rider=rs_rider("kv_bwd"))
            rs_done("kv_bwd", r)
            grads["w_ukv", 0] = dwukv
            grads["w_dkv", 0] = dwdkv[:, :KV_RANK + ROPE].reshape(N_DEV, dshard, KV_RANK + ROPE)
        up2, cv2, act = saved["ffn", layer]
        name = f"ffn_bwd{layer}"
        (dup2, dwup, dwdown, dcw), r = ffn_bwd(dh, weights["ffn_w_down", layer], up2, cv2, act,
                                               saved["xf", layer], f_cw[layer], name, rider=rs_rider(name))
        rs_done(name, r)
        grads["ffn_w_up", layer] = dwup.reshape(N_DEV, d, f8)
        grads["ffn_w_down", layer] = dwdown.reshape(N_DEV, fd, d)
        d_fconv[layer] = dcw.reshape(N_DEV, 3, f8)
        name = f"ffn_in_bwd{layer}"
        (dh, d_ffn_norm[layer]), r = proj_t_rms_bwd(dup2.reshape(N_DEV, seq, f8), weights["ffn_w_up", layer],
                                                    saved["hf", layer], ffn_norm[layer][None], dh, name,
                                                    rider=rs_rider(name))
        rs_done(name, r)
        hm, xm = saved["hm", layer], saved["xm", layer]
        if layer < N_A:
            u4, z = saved["mix", layer]
            name = f"mixer_bwd{layer}"
            (du3, dwin3, dwout, dcw), r = mixer_bwd(dh, weights["a_w_out", layer], u4, z, xm, a_cw[layer],
                                                    name, rider=rs_rider(name))
            rs_done(name, r)
            dwin = jnp.transpose(dwin3, (1, 0, 2)).reshape(d, N_DEV, 3 * d // N_DEV)
            grads["a_w_in", layer] = jnp.transpose(dwin, (1, 0, 2))
            grads["a_w_out", layer] = dwout.reshape(N_DEV, dshard, d)
            d_aconv[layer] = dcw
            name = f"mixer_in_bwd{layer}"
            extra = []
            if layer == 0:
                early_small = [
                    d_a_gain[1],
                    d_aconv[1],
                    jnp.concatenate(d_b_gain, axis=0),
                    jnp.concatenate(d_q_gain, axis=0),
                    d_kvin_gain[0],
                    d_kv_gain[0],
                    jnp.concatenate(d_ffn_norm, axis=0),
                    jnp.stack(d_fconv),
                    d_final[0],
                ]
                extra = [("ag", _pack(early_small))]
            (dh, d_a_gain[layer]), r = proj_t_rms_bwd(du3, weights["a_w_in", layer], hm, a_gain[layer][None],
                                                      dh, name, rider=rs_rider(name) + extra)
            rs_done(name, r)
            if layer == 0:
                g_early = r[-1]
        else:
            j = layer - N_A
            q, o, lse = saved["attn", layer]
            name = f"attn_out_bwd{j}"
            (do, dwo), r = o_bwd(dh, o, weights["b_w_o", j], name, rider=rs_rider(name))
            rs_done(name, r)
            grads["b_w_o", j] = dwo.reshape(N_DEV, dshard, d)
            name = f"attn_bwd{j}"
            (dq, dk, dv), r = attn_bwd(q, kv[0], kv[1], o, do, lse, name, rider=rs_rider(name))
            rs_done(name, r)
            dks.append(dk)
            dvs.append(dv)
            name = f"q_bwd{j}"
            (dh, dwuq, dwdq, d_q_gain[j], d_b_gain[j]), r = q_bwd(
                dq, xm, hm, b_mix_norm[j][None], dh, weights["b_w_dq", j], b_q_norm[j][None],
                weights["b_w_uq", j], cos, sin, name, rider=rs_rider(name))
            rs_done(name, r)
            grads["b_w_uq", j] = dwuq[:, :, :NOPE + ROPE]
            grads["b_w_dq", j] = dwdq.reshape(N_DEV, dshard, rank)
    grad_x = dh.reshape(x.shape)

    late_small = [d_a_gain[0], d_aconv[0]]
    full_shapes = [t.shape for t in early_small + late_small]
    small_pack = _pack(late_small)

    res = {}

    def update(name, n_layers, w, m, v, extra=(), transposed=False):
        view = (lambda t: jnp.transpose(t, (0, 2, 1))) if transposed else (lambda t: t)
        call = sum_adamw_transposed if transposed else sum_adamw
        shard = w.shape if w.ndim == 3 else (1,) + w.shape
        host = f"adamw_{name}"
        outs, r = call([parts[name, l] for l in range(n_layers)], view(w.reshape(shard)),
                       view(m.reshape(shard)), view(v.reshape(shard)), host,
                       rider=rs_rider(host) + list(extra))
        rs_done(host, r)
        res[name] = [view(t).reshape(w.shape) for t in outs]
        return r[len(BWD_RIDERS.get(host, [])):]

    (g_late,) = update("a_w_out", N_A, a_w_out, m_a_w_out, v_a_w_out, extra=[("ag", small_pack)])
    update("ffn_w_down", DEPTH, ffn_w_down, m_ffn_w_down, v_ffn_w_down)
    update("ffn_w_up", DEPTH, ffn_w_up, m_ffn_w_up, v_ffn_w_up, transposed=True)
    update("b_w_dq", N_B, b_w_dq, m_b_w_dq, v_b_w_dq)
    update("b_w_uq", N_B, b_w_uq, m_b_w_uq, v_b_w_uq)
    update("b_w_o", N_B, b_w_o, m_b_w_o, v_b_w_o)
    update("w_dkv", 1, w_dkv, m_w_dkv, v_w_dkv)
    update("w_ukv", 1, w_ukv, m_w_ukv, v_w_ukv)
    update("a_w_in", N_A, a_w_in, m_a_w_in, v_a_w_in)

    summed = sum_slots(jnp.concatenate([g_early, g_late], axis=1), "sum_small_grads")
    (s_a_gain1, s_aconv1, s_b_gain, s_q_gain, s_kvin, s_kvn, s_ffn_gain, s_fconv_g,
     s_final, s_a_gain0, s_aconv0) = _unpack(summed, full_shapes)
    s_a_gain = jnp.concatenate([s_a_gain0, s_a_gain1], axis=0)
    s_aconv_g = jnp.stack([s_aconv0, s_aconv1])
    dsl = d // N_DEV
    small = [
        ("a_mix_norm", lax.dynamic_slice_in_dim(s_a_gain, me * dsl, dsl, axis=1), a_mix_norm, m_a_mix_norm, v_a_mix_norm),
        ("a_conv", lax.dynamic_slice_in_dim(s_aconv_g, me * dsl, dsl, axis=2), a_conv, m_a_conv, v_a_conv),
        ("b_mix_norm", s_b_gain, b_mix_norm, m_b_mix_norm, v_b_mix_norm),
        ("b_q_norm", s_q_gain, b_q_norm, m_b_q_norm, v_b_q_norm),
        ("kv_in_norm", s_kvin, kv_in_norm, m_kv_in_norm, v_kv_in_norm),
        ("kv_norm", s_kvn, kv_norm, m_kv_norm, v_kv_norm),
        ("ffn_norm", s_ffn_gain, ffn_norm, m_ffn_norm, v_ffn_norm),
        ("ffn_conv", lax.dynamic_index_in_dim(s_fconv_g, me, axis=1, keepdims=False), ffn_conv, m_ffn_conv, v_ffn_conv),
        ("final_norm", s_final, final_norm, m_final_norm, v_final_norm),
    ]
    shapes = [t[2].shape for t in small]
    packed = [_pack([t[k] for t in small])[None] for k in (1, 2, 3, 4)]
    outs, _ = sum_adamw([packed[0]], packed[1], packed[2], packed[3], "adamw_small")
    unpacked = [_unpack(t[0], shapes) for t in outs]
    for idx, t in enumerate(small):
        res[t[0]] = [unpacked[k][idx] for k in range(4)]

    order = ["a_mix_norm", "a_w_in", "a_conv", "a_w_out", "b_mix_norm", "b_w_dq", "b_q_norm",
             "b_w_uq", "b_w_o", "kv_in_norm", "w_dkv", "kv_norm", "w_ukv", "ffn_norm",
             "ffn_w_up", "ffn_conv", "ffn_w_down", "final_norm"]
    return (loss, grad_x, *[res[n][0] for n in order], *[res[n][1] for n in order],
            *[res[n][2] for n in order], *[res[n][3] for n in order])
```

```python
import functools

import jax
import jax.numpy as jnp
from jax import lax
from jax.experimental import pallas as pl
from jax.experimental.pallas import tpu as pltpu

F32 = jnp.float32
BF16 = jnp.bfloat16

N_DEV = 8
N_HEADS = 8
NOPE = 128
ROPE = 64
ROPE_PAD = 128
QK = NOPE + ROPE_PAD
VDIM = 128
KV_RANK = 256
ROPE_THETA = 10000.0
RMS_EPS = 1e-6
ATTN_SCALE = (NOPE + ROPE) ** -0.5
N_A = 2
N_B = 2
DEPTH = 4

ADAM_LR = 0.001
ADAM_B1 = 0.9
ADAM_B2 = 0.999
ADAM_EPS = 1e-08
ADAM_WD = 0.01
ADAM_STEP = 10

V7X_VMEM_LIMIT = 56 * 1024 * 1024
BF16_SUBLANES = 16
ROW_TILE = 512
ROW_TILE_LARGE = 1024
ADAM_ROWS = 256
ATTN_TILE = 512
MIXER_CHUNK = 512
LANES = 128
NEG_BIG = -1e30
COPIES_PER_TASK = 7

MESH_ID = pl.DeviceIdType.MESH
ANY = pl.BlockSpec(memory_space=pl.ANY)


def _nt(a, b):
    return lax.dot_general(a, b, (((1,), (1,)), ((), ())), preferred_element_type=F32)


def _tn(a, b):
    return lax.dot_general(a, b, (((0,), (0,)), ((), ())), preferred_element_type=F32)


def _nn(a, b):
    return jnp.dot(a, b, preferred_element_type=F32)


def _rms(h, g):
    rstd = lax.rsqrt(jnp.mean(h * h, axis=-1, keepdims=True) + RMS_EPS)
    xhat = h * rstd
    return xhat * g, xhat, rstd


def _rms_bwd(dxn, xhat, rstd, g):
    dxhat = dxn * g
    dh = rstd * (dxhat - xhat * jnp.mean(dxhat * xhat, axis=-1, keepdims=True))
    return dh, dxn * xhat


def _shift_down(x, k, halo_rows):
    r = pltpu.roll(x, k, 0)
    row = lax.broadcasted_iota(jnp.int32, x.shape, 0)
    for t in range(k):
        r = jnp.where(row == t, halo_rows[t], r)
    return r


def _shift_up(x, k, halo_rows):
    n = x.shape[0]
    r = pltpu.roll(x, n - k, 0)
    row = lax.broadcasted_iota(jnp.int32, x.shape, 0)
    for t in range(k):
        r = jnp.where(row == n - k + t, halo_rows[t], r)
    return r


def _conv_taps(w_ref):
    return w_ref[0:1, :], w_ref[1:2, :], w_ref[2:3, :]


def _rope_swap(x):
    lane = lax.broadcasted_iota(jnp.int32, x.shape, 1)
    return jnp.where(lane < ROPE // 2, pltpu.roll(x, ROPE_PAD - ROPE // 2, 1),
                     pltpu.roll(x, ROPE // 2, 1))


def _rope_fwd(x, cos, sin):
    return x * cos + _rope_swap(x) * sin


def _rope_bwd(dy, cos, sin):
    return dy * cos - _rope_swap(dy) * sin


def _rope_tables(seq):
    inv = 1.0 / (ROPE_THETA ** (jnp.arange(0, ROPE, 2, dtype=F32) / ROPE))
    ang = jnp.arange(seq, dtype=F32)[:, None] * inv[None, :]
    cos, sin = jnp.cos(ang), jnp.sin(ang)
    zero = jnp.zeros((seq, ROPE_PAD - ROPE), F32)
    return (jnp.concatenate([cos, cos, zero], axis=1),
            jnp.concatenate([-sin, sin, zero], axis=1))


def _row_tile(rows, cap, mult=8):
    best = None
    for t in range(mult, min(rows, cap) + 1, mult):
        if rows % t == 0:
            best = t
    return rows if best is None else best


class _AllGatherTask:
    def __init__(self, t, x_ref, out_ref, send_sems, recv_sems, local_sems):
        self.t, self.x_ref, self.out_ref = t, x_ref, out_ref
        self.send_sems, self.recv_sems, self.local_sems = send_sems, recv_sems, local_sems
        mx, my, mc = lax.axis_index("x"), lax.axis_index("y"), lax.axis_index("c")
        self.mc = mc
        self.me, self.sibling = (mx, my, mc), (mx, my, 1 - mc)
        self.chips = [(1 - mx, my), (mx, 1 - my), (1 - mx, 1 - my)]

    def _slot(self, px, py, pc):
        return self.out_ref.at[4 * px + 2 * py + pc]

    def _copy(self, k, block, to, src=None):
        s = COPIES_PER_TASK * self.t + k
        return pltpu.make_async_remote_copy(
            src_ref=self._slot(*block) if src is None else src, dst_ref=self._slot(*block),
            send_sem=self.send_sems.at[s], recv_sem=self.recv_sems.at[s],
            device_id=to, device_id_type=MESH_ID)

    def _mine(self):
        return pltpu.make_async_copy(self.x_ref, self._slot(*self.me), self.local_sems.at[self.t])

    def _first(self):
        out = [self._copy(0, self.me, self.sibling, src=self.x_ref)]
        out += [self._copy(1 + j, self.me, (*chip, self.mc), src=self.x_ref)
                for j, chip in enumerate(self.chips)]
        return out

    def _passed(self):
        return [self._copy(4 + j, (*chip, self.mc), self.sibling) for j, chip in enumerate(self.chips)]

    def start(self):
        self._mine().start()
        for cp in self._first():
            cp.start()

    def forward(self):
        passed = self._passed()
        for j, chip in enumerate(self.chips):
            self._copy(1 + j, (*chip, self.mc), self.me).wait_recv()
            passed[j].start()

    def finish(self):
        self._copy(0, self.sibling, self.me).wait_recv()
        for j, chip in enumerate(self.chips):
            self._copy(4 + j, (*chip, 1 - self.mc), self.me).wait_recv()
        for cp in self._first() + self._passed():
            cp.wait_send()
        self._mine().wait()


class _ReduceScatterTask:
    def __init__(self, t, g_ref, out_ref, send_sems, recv_sems, local_sems):
        self.t, self.g_ref, self.out_ref = t, g_ref, out_ref
        self.send_sems, self.recv_sems, self.local_sems = send_sems, recv_sems, local_sems
        mx, my, mc = lax.axis_index("x"), lax.axis_index("y"), lax.axis_index("c")
        self.me = 4 * mx + 2 * my + mc
        self.peers = []
        for k in range(1, N_DEV):
            px, py, pc = mx ^ ((k >> 2) & 1), my ^ ((k >> 1) & 1), mc ^ (k & 1)
            self.peers.append(((px, py, pc), 4 * px + 2 * py + pc))

    def _mine(self):
        return pltpu.make_async_copy(self.g_ref.at[self.me], self.out_ref.at[self.me],
                                     self.local_sems.at[self.t])

    def _copy(self, k, src_slot, dst_slot):
        s = COPIES_PER_TASK * self.t + k
        return pltpu.make_async_remote_copy(
            src_ref=self.g_ref.at[src_slot], dst_ref=self.out_ref.at[dst_slot],
            send_sem=self.send_sems.at[s], recv_sem=self.recv_sems.at[s],
            device_id=self.peers[k][0], device_id_type=MESH_ID)

    def start(self):
        self._mine().start()
        for k, (_, peer) in enumerate(self.peers):
            self._copy(k, peer, self.me).start()

    def forward(self):
        pass

    def finish(self):
        for k, (_, peer) in enumerate(self.peers):
            self._copy(k, self.me, peer).wait_recv()
        for k, (_, peer) in enumerate(self.peers):
            self._copy(k, peer, self.me).wait_send()
        self._mine().wait()


class _PairExchangeTask:
    def __init__(self, t, g_ref, out_ref, send_sems, recv_sems, local_sems):
        mx, my, mc = lax.axis_index("x"), lax.axis_index("y"), lax.axis_index("c")
        s = COPIES_PER_TASK * t
        self.copy = pltpu.make_async_remote_copy(
            src_ref=g_ref.at[:, 1 - mc], dst_ref=out_ref,
            send_sem=send_sems.at[s], recv_sem=recv_sems.at[s],
            device_id=(mx, my, 1 - mc), device_id_type=MESH_ID)

    def start(self):
        self.copy.start()

    def forward(self):
        pass

    def finish(self):
        self.copy.wait()


class _ChipScatterTask:
    def __init__(self, t, s_ref, out_ref, send_sems, recv_sems, local_sems):
        self.t, self.s_ref, self.out_ref = t, s_ref, out_ref
        self.send_sems, self.recv_sems, self.local_sems = send_sems, recv_sems, local_sems
        mx, my, mc = lax.axis_index("x"), lax.axis_index("y"), lax.axis_index("c")
        self.chip = 2 * mx + my
        self.peers = []
        for k in range(1, N_DEV // 2):
            px, py = mx ^ ((k >> 1) & 1), my ^ (k & 1)
            self.peers.append(((px, py, mc), 2 * px + py))

    def _mine(self):
        return pltpu.make_async_copy(self.s_ref.at[self.chip], self.out_ref.at[self.chip],
                                     self.local_sems.at[self.t])

    def _copy(self, k, src_slot, dst_slot):
        s = COPIES_PER_TASK * self.t + k
        return pltpu.make_async_remote_copy(
            src_ref=self.s_ref.at[src_slot], dst_ref=self.out_ref.at[dst_slot],
            send_sem=self.send_sems.at[s], recv_sem=self.recv_sems.at[s],
            device_id=self.peers[k][0], device_id_type=MESH_ID)

    def start(self):
        self._mine().start()
        for k, (_, peer) in enumerate(self.peers):
            self._copy(k, peer, self.chip).start()

    def forward(self):
        pass

    def finish(self):
        for k, (_, peer) in enumerate(self.peers):
            self._copy(k, self.chip, peer).wait_recv()
        for k, (_, peer) in enumerate(self.peers):
            self._copy(k, peer, self.chip).wait_send()
        self._mine().wait()


_TASKS = {"ag": _AllGatherTask, "rs": _ReduceScatterTask, "rs_pair": _PairExchangeTask,
          "rs_chip": _ChipScatterTask}


def _task_shape(kind, arr):
    shape = {"ag": (N_DEV,) + arr.shape, "rs": arr.shape, "rs_chip": arr.shape,
             "rs_pair": arr.shape[:1] + arr.shape[2:]}[kind]
    return jax.ShapeDtypeStruct(shape, arr.dtype)


def _sem_shapes(n_tasks):
    return [pltpu.SemaphoreType.DMA((COPIES_PER_TASK * n_tasks,)),
            pltpu.SemaphoreType.DMA((COPIES_PER_TASK * n_tasks,)),
            pltpu.SemaphoreType.DMA((n_tasks,))]


def _make_tasks(rider, in_refs, out_refs, sems):
    return [_TASKS[kind](t, in_refs[t], out_refs[t], *sems) for t, (kind, _) in enumerate(rider)]


def exchange(rider, name):
    n = len(rider)

    def body(*refs):
        tasks = _make_tasks(rider, refs[:n], refs[n:2 * n], refs[2 * n:])
        for task in tasks:
            task.start()
        for task in tasks:
            task.forward()
        for task in tasks:
            task.finish()

    return list(pl.pallas_call(
        body, name=name, out_shape=tuple(_task_shape(k, a) for k, a in rider),
        in_specs=[ANY] * n, out_specs=(ANY,) * n, scratch_shapes=_sem_shapes(n),
    )(*[a for _, a in rider]))


def _call(body, name, grid, in_specs, out_specs, out_shape, args, scratch=(), rider=()):
    in_specs, out_specs, out_shape = list(in_specs), tuple(out_specs), tuple(out_shape)
    n_in, n_out, n_scr, n_r = len(in_specs), len(out_specs), len(scratch), len(rider)
    if n_r:
        def kern(*refs):
            ins, r_in = refs[:n_in], refs[n_in:n_in + n_r]
            o0 = n_in + n_r
            outs, r_out = refs[o0:o0 + n_out], refs[o0 + n_out:o0 + n_out + n_r]
            s0 = o0 + n_out + n_r
            scr, sems = refs[s0:s0 + n_scr], refs[s0 + n_scr:]
            step = 0
            for a, n in enumerate(grid):
                step = step * n + pl.program_id(a)
            n_steps = 1
            for n in grid:
                n_steps *= n

            @pl.when(step == 0)
            def _():
                for task in _make_tasks(rider, r_in, r_out, sems):
                    task.start()
            body(*ins, *outs, *scr)

            @pl.when(step == n_steps - 1)
            def _():
                tasks = _make_tasks(rider, r_in, r_out, sems)
                for task in tasks:
                    task.forward()
                for task in tasks:
                    task.finish()
    else:
        kern = body
    res = pl.pallas_call(
        kern, name=name, grid=grid,
        in_specs=in_specs + [ANY] * n_r, out_specs=out_specs + (ANY,) * n_r,
        out_shape=out_shape + tuple(_task_shape(k, a) for k, a in rider),
        scratch_shapes=list(scratch) + (_sem_shapes(n_r) if n_r else []),
        compiler_params=pltpu.CompilerParams(dimension_semantics=("arbitrary",) * len(grid),
                                             vmem_limit_bytes=V7X_VMEM_LIMIT),
    )(*args, *[a for _, a in rider])
    return list(res[:n_out]), list(res[n_out:])


def _adamw(g, w, m, v):
    m = ADAM_B1 * m + (1.0 - ADAM_B1) * g
    v = ADAM_B2 * v + (1.0 - ADAM_B2) * (g * g)
    m_hat = m / (1.0 - ADAM_B1 ** ADAM_STEP)
    v_hat = v / (1.0 - ADAM_B2 ** ADAM_STEP)
    delta = -ADAM_LR * (m_hat / (jnp.sqrt(v_hat) + ADAM_EPS) + ADAM_WD * w)
    return delta, m, v


def sum_adamw(parts, w, m, v, name, rider=()):
    n_l, rows, cols = w.shape
    mult = BF16_SUBLANES if parts[0].dtype == BF16 else 8
    tr = _row_tile(rows, ADAM_ROWS, mult)
    n_i = rows // tr

    def body(*refs):
        part_refs = refs[:n_l]
        w_ref, m_ref, v_ref, g_out, d_out, m_out, v_out = refs[n_l:]
        layer = pl.program_id(0)
        for k in range(n_l):
            @pl.when(layer == k)
            def _(k=k):
                g = part_refs[k][0].astype(F32)
                for s in range(1, parts[k].shape[0]):
                    g = g + part_refs[k][s].astype(F32)
                delta, m_new, v_new = _adamw(g, w_ref[...], m_ref[...], v_ref[...])
                g_out[...] = g
                d_out[...] = delta
                m_out[...] = m_new
                v_out[...] = v_new

    part_specs = [pl.BlockSpec((parts[k].shape[0], tr, cols), functools.partial(
        lambda l, i, k: (0, jnp.where(l == k, i, 0), 0), k=k)) for k in range(n_l)]
    wspec = pl.BlockSpec((None, tr, cols), lambda l, i: (l, i, 0))
    shape = jax.ShapeDtypeStruct(w.shape, F32)
    return _call(body, name, (n_l, n_i), part_specs + [wspec] * 3, (wspec,) * 4, (shape,) * 4,
                 (*parts, w, m, v), rider=rider)


def sum_adamw_transposed(parts, w_t, m_t, v_t, name, rider=()):
    n_l, cols, rows = w_t.shape
    tr = LANES
    n_i = rows // tr
    starts = list(range(0, cols - LANES + 1, LANES))
    if starts[-1] + LANES < cols:
        starts.append(cols - LANES)

    def body(*refs):
        part_refs = refs[:n_l]
        w_ref, m_ref, v_ref, g_out, d_out, m_out, v_out = refs[n_l:]
        layer = pl.program_id(0)
        for k in range(n_l):
            @pl.when(layer == k)
            def _(k=k):
                for c0 in starts:
                    piece = pl.ds(c0, LANES)
                    g = part_refs[k][0, :, piece].astype(F32)
                    for s in range(1, parts[k].shape[0]):
                        g = g + part_refs[k][s, :, piece].astype(F32)
                    g = g.T
                    delta, m_new, v_new = _adamw(g, w_ref[piece, :], m_ref[piece, :], v_ref[piece, :])
                    g_out[piece, :] = g
                    d_out[piece, :] = delta
                    m_out[piece, :] = m_new
                    v_out[piece, :] = v_new

    part_specs = [pl.BlockSpec((parts[k].shape[0], tr, cols), functools.partial(
        lambda l, i, k: (0, jnp.where(l == k, i, 0), 0), k=k)) for k in range(n_l)]
    wspec = pl.BlockSpec((None, cols, tr), lambda l, i: (l, 0, i))
    shape = jax.ShapeDtypeStruct(w_t.shape, F32)
    return _call(body, name, (n_l, n_i), part_specs + [wspec] * 3, (wspec,) * 4, (shape,) * 4,
                 (*parts, w_t, m_t, v_t), rider=rider)


def pair_sum(g4, other, name):
    n_chip, _, rows, cols = g4.shape
    tr = _row_tile(rows, 512, BF16_SUBLANES)

    def body(core_ref, g_ref, o_ref, s_ref):
        s_ref[...] = (g_ref[...].astype(F32) + o_ref[...].astype(F32)).astype(BF16)

    blk = pl.BlockSpec((None, tr, cols), lambda k, i, core: (k, i, 0))
    return pl.pallas_call(
        body, name=name, out_shape=jax.ShapeDtypeStruct((n_chip, rows, cols), g4.dtype),
        grid_spec=pltpu.PrefetchScalarGridSpec(
            num_scalar_prefetch=1, grid=(n_chip, rows // tr),
            in_specs=[pl.BlockSpec((None, None, tr, cols), lambda k, i, core: (k, core[0], i, 0)), blk],
            out_specs=blk),
        compiler_params=pltpu.CompilerParams(dimension_semantics=("arbitrary", "arbitrary"),
                                             vmem_limit_bytes=V7X_VMEM_LIMIT),
    )(lax.axis_index("c").astype(jnp.int32).reshape(1), g4, other)


def sum_slots(parts, name):
    n, rows, cols = parts.shape

    def body(p_ref, o_ref):
        acc = p_ref[0]
        for s in range(1, n):
            acc = acc + p_ref[s]
        o_ref[...] = acc

    return pl.pallas_call(
        body, name=name, out_shape=jax.ShapeDtypeStruct((rows, cols), F32),
        in_specs=[pl.BlockSpec(memory_space=pltpu.VMEM)],
        out_specs=pl.BlockSpec(memory_space=pltpu.VMEM),
    )(parts)


def norm_fwd(h, g, name):
    seq, d = h.shape
    tm = min(ROW_TILE, seq)

    def body(h_ref, g_ref, o_ref):
        o_ref[...] = _rms(h_ref[...], g_ref[...])[0].astype(BF16)

    return _call(body, name, (seq // tm,),
                 [pl.BlockSpec((tm, d), lambda i: (i, 0)), pl.BlockSpec((1, d), lambda i: (0, 0))],
                 [pl.BlockSpec((tm, d), lambda i: (i, 0))],
                 [jax.ShapeDtypeStruct((seq, d), BF16)], (h, g))[0][0]


def proj_residual(a, w, res, name, g_next=None, rider=()):
    nb, seq, kb = a.shape
    d = w.shape[-1]
    tm = min(ROW_TILE, seq)
    with_norm = g_next is not None

    def body(a_ref, w_ref, r_ref, *rest):
        acc = r_ref[...]
        for b in range(nb):
            acc = acc + _nn(a_ref[b], w_ref[b])
        if with_norm:
            g_ref, o_ref, xn_ref = rest
            xn_ref[...] = _rms(acc, g_ref[...])[0].astype(BF16)
        else:
            (o_ref,) = rest
        o_ref[...] = acc

    row = pl.BlockSpec((tm, d), lambda i: (i, 0))
    in_specs = [pl.BlockSpec((nb, tm, kb), lambda i: (0, i, 0)),
                pl.BlockSpec((nb, kb, d), lambda i: (0, 0, 0)), row]
    args = [a, w, res]
    out_specs, out_shape = [row], [jax.ShapeDtypeStruct((seq, d), F32)]
    if with_norm:
        in_specs.append(pl.BlockSpec((1, d), lambda i: (0, 0)))
        args.append(g_next)
        out_specs.append(row)
        out_shape.append(jax.ShapeDtypeStruct((seq, d), BF16))
    outs, r_outs = _call(body, name, (seq // tm,), in_specs, out_specs, out_shape, args, rider=rider)
    return (outs[0], outs[1] if with_norm else None), r_outs


def proj_t_rms_bwd(du, w, h, g, dres, name, rider=()):
    nb, seq, wd = du.shape
    k = w.shape[1]
    big_weight = 2 * w.size * w.dtype.itemsize > V7X_VMEM_LIMIT // 4
    tm = min(ROW_TILE // 2 if big_weight else ROW_TILE, seq)

    def body(du_ref, w_ref, h_ref, g_ref, dr_ref, dh_ref, dg_ref):
        i = pl.program_id(0)
        dxn = _nt(du_ref[0], w_ref[0])
        for b in range(1, nb):
            dxn = dxn + _nt(du_ref[b], w_ref[b])
        _, xhat, rstd = _rms(h_ref[...], g_ref[...])
        dh, dg_rows = _rms_bwd(dxn, xhat, rstd, g_ref[...])
        dh_ref[...] = dr_ref[...] + dh

        @pl.when(i == 0)
        def _():
            dg_ref[...] = jnp.zeros_like(dg_ref)
        dg_ref[...] += jnp.sum(dg_rows, axis=0, keepdims=True)

    row = pl.BlockSpec((tm, k), lambda i: (i, 0))
    vec = pl.BlockSpec((1, k), lambda i: (0, 0))
    return _call(body, name, (seq // tm,),
                 [pl.BlockSpec((nb, tm, wd), lambda i: (0, i, 0)),
                  pl.BlockSpec((nb, k, wd), lambda i: (0, 0, 0)), row, vec, row],
                 (row, vec),
                 (jax.ShapeDtypeStruct((seq, k), F32), jax.ShapeDtypeStruct((1, k), F32)),
                 (du, w, h, g, dres), rider=rider)


def mixer_fwd(xn, win3, cw, name, rider=()):
    seq, d = xn.shape
    tm = min(ROW_TILE_LARGE, seq)
    cc = min(MIXER_CHUNK, d)
    n_c, n_i = d // cc, seq // tm

    def body(x_ref, w_ref, cw_ref, u_ref, z_ref, carry):
        i = pl.program_id(1)

        @pl.when(i == 0)
        def _():
            carry[...] = jnp.zeros_like(carry)
        xb = x_ref[...]
        b = _nn(xb, w_ref[0])
        c = _nn(xb, w_ref[1])
        hh = _nn(xb, w_ref[2])
        p = c * hh
        w0, w1, w2 = _conv_taps(cw_ref)
        p1 = _shift_down(p, 1, [carry[7:8, :]])
        p2 = _shift_down(p, 2, [carry[6:7, :], carry[7:8, :]])
        q = w0 * p2 + w1 * p1 + w2 * p
        carry[...] = p[tm - 8:tm, :]
        u_ref[0] = b.astype(BF16)
        u_ref[1] = c.astype(BF16)
        u_ref[2] = hh.astype(BF16)
        u_ref[3] = q.astype(BF16)
        z_ref[...] = (b * q).astype(BF16)

    return _call(body, name, (n_c, n_i),
                 [pl.BlockSpec((tm, d), lambda c, i: (i, 0)),
                  pl.BlockSpec((3, d, cc), lambda c, i: (0, 0, c)),
                  pl.BlockSpec((3, cc), lambda c, i: (0, c))],
                 (pl.BlockSpec((4, tm, cc), lambda c, i: (0, i, c)),
                  pl.BlockSpec((tm, cc), lambda c, i: (i, c))),
                 (jax.ShapeDtypeStruct((4, seq, d), BF16), jax.ShapeDtypeStruct((seq, d), BF16)),
                 (xn, win3, cw), scratch=[pltpu.VMEM((8, cc), F32)], rider=rider)


def mixer_bwd(dh, wout, u4, z, xn, cw, name, rider=()):
    seq, d = xn.shape
    tm = min(ROW_TILE, seq)
    cc = min(MIXER_CHUNK, d)
    n_c, n_i = d // cc, seq // tm

    def body(dh_ref, wout_ref, u_ref, z_ref, x_ref, cw_ref,
             du_ref, dwin_ref, dwout_ref, dcw_ref, acc_in, acc_out, acc_cw, carry):
        i = pl.program_id(1)

        @pl.when(i == 0)
        def _():
            acc_in[...] = jnp.zeros_like(acc_in)
            acc_out[...] = jnp.zeros_like(acc_out)
            acc_cw[...] = jnp.zeros_like(acc_cw)
            carry[...] = jnp.zeros_like(carry)
        dhb = dh_ref[...].astype(BF16)
        dz = _nt(dhb, wout_ref[...])
        acc_out[...] += _tn(z_ref[...], dhb)
        b = u_ref[0].astype(F32)
        c = u_ref[1].astype(F32)
        hh = u_ref[2].astype(F32)
        q = u_ref[3].astype(F32)
        p = c * hh
        db = dz * q
        dq = dz * b
        w0, w1, w2 = _conv_taps(cw_ref)
        dq1 = _shift_up(dq, 1, [carry[0:1, :]])
        dq2 = _shift_up(dq, 2, [carry[0:1, :], carry[1:2, :]])
        dp = w2 * dq + w1 * dq1 + w0 * dq2
        carry[...] = dq[0:8, :]
        acc_cw[0:1, :] += jnp.sum(dq2 * p, axis=0, keepdims=True)
        acc_cw[1:2, :] += jnp.sum(dq1 * p, axis=0, keepdims=True)
        acc_cw[2:3, :] += jnp.sum(dq * p, axis=0, keepdims=True)
        dbb = db.astype(BF16)
        dcb = (dp * hh).astype(BF16)
        dhhb = (dp * c).astype(BF16)
        du_ref[0] = dbb
        du_ref[1] = dcb
        du_ref[2] = dhhb
        xb = x_ref[...]
        acc_in[0] += _tn(xb, dbb)
        acc_in[1] += _tn(xb, dcb)
        acc_in[2] += _tn(xb, dhhb)

        @pl.when(i == n_i - 1)
        def _():
            dwin_ref[...] = acc_in[...].astype(BF16)
            dwout_ref[...] = acc_out[...].astype(BF16)
            dcw_ref[...] = acc_cw[0:3, :]

    rev = lambda c, i: (n_i - 1 - i, 0)
    return _call(body, name, (n_c, n_i),
                 [pl.BlockSpec((tm, d), rev),
                  pl.BlockSpec((cc, d), lambda c, i: (c, 0)),
                  pl.BlockSpec((4, tm, cc), lambda c, i: (0, n_i - 1 - i, c)),
                  pl.BlockSpec((tm, cc), lambda c, i: (n_i - 1 - i, c)),
                  pl.BlockSpec((tm, d), rev),
                  pl.BlockSpec((3, cc), lambda c, i: (0, c))],
                 (pl.BlockSpec((3, tm, cc), lambda c, i: (0, n_i - 1 - i, c)),
                  pl.BlockSpec((3, d, cc), lambda c, i: (0, 0, c)),
                  pl.BlockSpec((cc, d), lambda c, i: (c, 0)),
                  pl.BlockSpec((3, cc), lambda c, i: (0, c))),
                 (jax.ShapeDtypeStruct((3, seq, d), BF16), jax.ShapeDtypeStruct((3, d, d), BF16),
                  jax.ShapeDtypeStruct((d, d), BF16), jax.ShapeDtypeStruct((3, d), F32)),
                 (dh, wout, u4, z, xn, cw),
                 scratch=[pltpu.VMEM((3, d, cc), F32), pltpu.VMEM((cc, d), F32),
                          pltpu.VMEM((8, cc), F32), pltpu.VMEM((8, cc), F32)], rider=rider)


def _silu_parts(cg):
    sg = 1.0 / (1.0 + jnp.exp(-cg))
    return sg, cg * sg


def ffn_fwd(xn, wup, fcw, name, rider=()):
    seq, d = xn.shape
    f8 = wup.shape[-1]
    half = N_DEV // 2
    tm = min(ROW_TILE_LARGE, seq)
    n_i = seq // tm

    def body(x_ref, wg_ref, wu_ref, cg_ref, cu_ref, up_ref, cv_ref, a_ref, carry):
        i = pl.program_id(1)

        @pl.when(i == 0)
        def _():
            carry[...] = jnp.zeros_like(carry)
        xb = x_ref[...]
        conv = []
        for s, (w_ref, t_ref) in enumerate(((wg_ref, cg_ref), (wu_ref, cu_ref))):
            u = _nn(xb, w_ref[...])
            up_ref[s] = u.astype(BF16)
            w0, w1, w2 = _conv_taps(t_ref)
            u1 = _shift_down(u, 1, [carry[s, 7:8, :]])
            u2 = _shift_down(u, 2, [carry[s, 6:7, :], carry[s, 7:8, :]])
            cv = w0 * u2 + w1 * u1 + w2 * u
            cv_ref[s] = cv.astype(BF16)
            conv.append(cv)
            carry[s] = u[tm - 8:tm, :]
        _, silu = _silu_parts(conv[0])
        a_ref[...] = (silu * conv[1]).astype(BF16)

    blk = pl.BlockSpec((2, None, tm, f8), lambda c, i: (0, c, i, 0))
    big = jax.ShapeDtypeStruct((2, half, seq, f8), BF16)
    return _call(body, name, (half, n_i),
                 [pl.BlockSpec((tm, d), lambda c, i: (i, 0)),
                  pl.BlockSpec((None, d, f8), lambda c, i: (c, 0, 0)),
                  pl.BlockSpec((None, d, f8), lambda c, i: (c + half, 0, 0)),
                  pl.BlockSpec((None, 3, f8), lambda c, i: (c, 0, 0)),
                  pl.BlockSpec((None, 3, f8), lambda c, i: (c + half, 0, 0))],
                 (blk, blk, pl.BlockSpec((None, tm, f8), lambda c, i: (c, i, 0))),
                 (big, big, jax.ShapeDtypeStruct((half, seq, f8), BF16)),
                 (xn, wup, wup, fcw, fcw), scratch=[pltpu.VMEM((2, 8, f8), F32)], rider=rider)


def ffn_bwd(dh, wdown, up2, cv2, act, xn, fcw, name, rider=()):
    seq, d = xn.shape
    f8 = up2.shape[-1]
    fb = wdown.shape[1]
    half = N_DEV // 2
    tm = min(ROW_TILE, seq)
    n_i = seq // tm

    def body(dh_ref, wd_ref, up_ref, cv_ref, a_ref, x_ref, cg_ref, cu_ref,
             dup_ref, dwup_ref, dwd_ref, dcw_ref, acc_up, acc_down, acc_cw, carry):
        i = pl.program_id(1)

        @pl.when(i == 0)
        def _():
            acc_up[...] = jnp.zeros_like(acc_up)
            acc_down[...] = jnp.zeros_like(acc_down)
            acc_cw[...] = jnp.zeros_like(acc_cw)
            carry[...] = jnp.zeros_like(carry)
        dhb = dh_ref[...].astype(BF16)
        da = _nt(dhb, wd_ref[...])
        acc_down[...] += _tn(a_ref[...], dhb)
        cg = cv_ref[0].astype(F32)
        cu = cv_ref[1].astype(F32)
        sg, silu = _silu_parts(cg)
        dcg = da * cu * (sg + silu * (1.0 - sg))
        dcu = da * silu
        xb = x_ref[...]
        for s, (dc, t_ref) in enumerate(((dcg, cg_ref), (dcu, cu_ref))):
            w0, w1, w2 = _conv_taps(t_ref)
            d1 = _shift_up(dc, 1, [carry[s, 0:1, :]])
            d2 = _shift_up(dc, 2, [carry[s, 0:1, :], carry[s, 1:2, :]])
            du = (w2 * dc + w1 * d1 + w0 * d2).astype(BF16)
            carry[s] = dc[0:8, :]
            u = up_ref[s].astype(F32)
            acc_cw[s, 0:1, :] += jnp.sum(d2 * u, axis=0, keepdims=True)
            acc_cw[s, 1:2, :] += jnp.sum(d1 * u, axis=0, keepdims=True)
            acc_cw[s, 2:3, :] += jnp.sum(dc * u, axis=0, keepdims=True)
            dup_ref[s] = du
            acc_up[s] += _tn(xb, du)

        @pl.when(i == n_i - 1)
        def _():
            dwup_ref[...] = acc_up[...].astype(BF16)
            dwd_ref[...] = acc_down[...].astype(BF16)
            dcw_ref[...] = acc_cw[:, 0:3, :]

    rev = lambda c, i: (n_i - 1 - i, 0)
    blk = pl.BlockSpec((2, None, tm, f8), lambda c, i: (0, c, n_i - 1 - i, 0))
    return _call(body, name, (half, n_i),
                 [pl.BlockSpec((tm, d), rev),
                  pl.BlockSpec((None, fb, d), lambda c, i: (c, 0, 0)),
                  blk, blk,
                  pl.BlockSpec((None, tm, f8), lambda c, i: (c, n_i - 1 - i, 0)),
                  pl.BlockSpec((tm, d), rev),
                  pl.BlockSpec((None, 3, f8), lambda c, i: (c, 0, 0)),
                  pl.BlockSpec((None, 3, f8), lambda c, i: (c + half, 0, 0))],
                 (blk,
                  pl.BlockSpec((2, None, d, f8), lambda c, i: (0, c, 0, 0)),
                  pl.BlockSpec((None, fb, d), lambda c, i: (c, 0, 0)),
                  pl.BlockSpec((2, None, 3, f8), lambda c, i: (0, c, 0, 0))),
                 (jax.ShapeDtypeStruct((2, half, seq, f8), BF16),
                  jax.ShapeDtypeStruct((2, half, d, f8), BF16),
                  jax.ShapeDtypeStruct((half, fb, d), BF16),
                  jax.ShapeDtypeStruct((2, half, 3, f8), F32)),
                 (dh, wdown, up2, cv2, act, xn, fcw, fcw),
                 scratch=[pltpu.VMEM((2, d, f8), F32), pltpu.VMEM((fb, d), F32),
                          pltpu.VMEM((2, 8, f8), F32), pltpu.VMEM((2, 8, f8), F32)], rider=rider)


def q_fwd(xn, wdq, gq, wuq, cos, sin, name, rider=()):
    seq, d = xn.shape
    rank = wdq.shape[-1]
    tm = min(ROW_TILE, seq)

    def body(x_ref, wdq_ref, gq_ref, wuq_ref, cos_ref, sin_ref, q_ref):
        qc = _nn(x_ref[...], wdq_ref[...])
        qn = _rms(qc, gq_ref[...])[0].astype(BF16)
        for hd in range(N_HEADS):
            qh = _nn(qn, wuq_ref[hd])
            qr = _rope_fwd(qh[:, NOPE:QK], cos_ref[...], sin_ref[...])
            q_ref[hd, :, 0:NOPE] = (qh[:, 0:NOPE] * ATTN_SCALE).astype(BF16)
            q_ref[hd, :, NOPE:QK] = (qr * ATTN_SCALE).astype(BF16)

    rope = pl.BlockSpec((tm, ROPE_PAD), lambda i: (i, 0))
    return _call(body, name, (seq // tm,),
                 [pl.BlockSpec((tm, d), lambda i: (i, 0)),
                  pl.BlockSpec((d, rank), lambda i: (0, 0)),
                  pl.BlockSpec((1, rank), lambda i: (0, 0)),
                  pl.BlockSpec((N_HEADS, rank, QK), lambda i: (0, 0, 0)), rope, rope],
                 [pl.BlockSpec((N_HEADS, tm, QK), lambda i: (0, i, 0))],
                 [jax.ShapeDtypeStruct((N_HEADS, seq, QK), BF16)],
                 (xn, wdq, gq, wuq, cos, sin), rider=rider)


def q_bwd(dq, xn, h, g, dres, wdq, gq, wuq, cos, sin, name, rider=()):
    seq, d = xn.shape
    rank = wdq.shape[-1]
    tm = min(ROW_TILE, seq)
    n_i = seq // tm

    def body(dq_ref, x_ref, h_ref, g_ref, dr_ref, wdq_ref, gq_ref, wuq_ref, cos_ref, sin_ref,
             dh_ref, dwuq_ref, dwdq_ref, dgq_ref, dg_ref, acc_uq, acc_dq):
        i = pl.program_id(0)

        @pl.when(i == 0)
        def _():
            acc_uq[...] = jnp.zeros_like(acc_uq)
            acc_dq[...] = jnp.zeros_like(acc_dq)
            dgq_ref[...] = jnp.zeros_like(dgq_ref)
            dg_ref[...] = jnp.zeros_like(dg_ref)
        xb = x_ref[...]
        qc = _nn(xb, wdq_ref[...])
        qn, qhat, qrstd = _rms(qc, gq_ref[...])
        qnb = qn.astype(BF16)
        dqn = jnp.zeros((tm, rank), F32)
        for hd in range(N_HEADS):
            dnope = (dq_ref[hd, :, 0:NOPE].astype(F32) * ATTN_SCALE).astype(BF16)
            drope = _rope_bwd(dq_ref[hd, :, NOPE:QK].astype(F32) * ATTN_SCALE, cos_ref[...], sin_ref[...])
            draw = jnp.concatenate([dnope, drope.astype(BF16)], axis=1)
            dqn = dqn + _nt(draw, wuq_ref[hd])
            acc_uq[hd] += _tn(qnb, draw)
        dqc, dg_rows = _rms_bwd(dqn, qhat, qrstd, gq_ref[...])
        dgq_ref[...] += jnp.sum(dg_rows, axis=0, keepdims=True)
        dqcb = dqc.astype(BF16)
        acc_dq[...] += _tn(xb, dqcb)
        _, xhat, rstd = _rms(h_ref[...], g_ref[...])
        dh, dg_rows = _rms_bwd(_nt(dqcb, wdq_ref[...]), xhat, rstd, g_ref[...])
        dh_ref[...] = dr_ref[...] + dh
        dg_ref[...] += jnp.sum(dg_rows, axis=0, keepdims=True)

        @pl.when(i == n_i - 1)
        def _():
            dwuq_ref[...] = acc_uq[...].astype(BF16)
            dwdq_ref[...] = acc_dq[...].astype(BF16)

    rope = pl.BlockSpec((tm, ROPE_PAD), lambda i: (i, 0))
    row = pl.BlockSpec((tm, d), lambda i: (i, 0))
    vec = pl.BlockSpec((1, d), lambda i: (0, 0))
    return _call(body, name, (n_i,),
                 [pl.BlockSpec((N_HEADS, tm, QK), lambda i: (0, i, 0)), row, row, vec, row,
                  pl.BlockSpec((d, rank), lambda i: (0, 0)),
                  pl.BlockSpec((1, rank), lambda i: (0, 0)),
                  pl.BlockSpec((N_HEADS, rank, QK), lambda i: (0, 0, 0)), rope, rope],
                 (row,
                  pl.BlockSpec((N_HEADS, rank, QK), lambda i: (0, 0, 0)),
                  pl.BlockSpec((d, rank), lambda i: (0, 0)),
                  pl.BlockSpec((1, rank), lambda i: (0, 0)), vec),
                 (jax.ShapeDtypeStruct((seq, d), F32),
                  jax.ShapeDtypeStruct((N_HEADS, rank, QK), BF16),
                  jax.ShapeDtypeStruct((d, rank), BF16),
                  jax.ShapeDtypeStruct((1, rank), F32),
                  jax.ShapeDtypeStruct((1, d), F32)),
                 (dq, xn, h, g, dres, wdq, gq, wuq, cos, sin),
                 scratch=[pltpu.VMEM((N_HEADS, rank, QK), F32), pltpu.VMEM((d, rank), F32)],
                 rider=rider)


def kv_fwd(h, g, wdkv, gkv, wukv, cos, sin, name, rider=()):
    seq, d = h.shape
    tm = min(ROW_TILE, seq)
    wk = KV_RANK + ROPE_PAD

    def body(h_ref, g_ref, wdkv_ref, gkv_ref, wukv_ref, cos_ref, sin_ref, k_ref, v_ref, c_ref):
        xk = _rms(h_ref[...], g_ref[...])[0].astype(BF16)
        ckv = _nn(xk, wdkv_ref[...])
        c_kv = ckv[:, 0:KV_RANK]
        c_ref[...] = c_kv
        kr = _rope_fwd(ckv[:, KV_RANK:wk], cos_ref[...], sin_ref[...]).astype(BF16)
        ckn = _rms(c_kv, gkv_ref[...])[0].astype(BF16)
        for hd in range(N_HEADS):
            kvh = _nn(ckn, wukv_ref[hd])
            k_ref[hd, :, 0:NOPE] = kvh[:, 0:NOPE].astype(BF16)
            k_ref[hd, :, NOPE:QK] = kr
            v_ref[hd] = kvh[:, NOPE:NOPE + VDIM].astype(BF16)

    rope = pl.BlockSpec((tm, ROPE_PAD), lambda i: (i, 0))
    return _call(body, name, (seq // tm,),
                 [pl.BlockSpec((tm, d), lambda i: (i, 0)),
                  pl.BlockSpec((1, d), lambda i: (0, 0)),
                  pl.BlockSpec((d, wk), lambda i: (0, 0)),
                  pl.BlockSpec((1, KV_RANK), lambda i: (0, 0)),
                  pl.BlockSpec((N_HEADS, KV_RANK, NOPE + VDIM), lambda i: (0, 0, 0)), rope, rope],
                 (pl.BlockSpec((N_HEADS, tm, QK), lambda i: (0, i, 0)),
                  pl.BlockSpec((N_HEADS, tm, VDIM), lambda i: (0, i, 0)),
                  pl.BlockSpec((tm, KV_RANK), lambda i: (i, 0))),
                 (jax.ShapeDtypeStruct((N_HEADS, seq, QK), BF16),
                  jax.ShapeDtypeStruct((N_HEADS, seq, VDIM), BF16),
                  jax.ShapeDtypeStruct((seq, KV_RANK), F32)),
                 (h, g, wdkv, gkv, wukv, cos, sin), rider=rider)


def kv_bwd(dks, dvs, c_kv, h, g, dres, wdkv, gkv, wukv, cos, sin, name, rider=()):
    seq, d = h.shape
    tm = min(ROW_TILE, seq)
    n_i = seq // tm
    wk = KV_RANK + ROPE_PAD
    n_b = len(dks)

    def body(*refs):
        dk_refs = refs[:n_b]
        dv_refs = refs[n_b:2 * n_b]
        (c_ref, h_ref, g_ref, dr_ref, wdkv_ref, gkv_ref, wukv_ref, cos_ref, sin_ref,
         dh_ref, dwukv_ref, dwdkv_ref, dgkv_ref, dg_ref, acc_ukv, acc_dkv) = refs[2 * n_b:]
        i = pl.program_id(0)

        @pl.when(i == 0)
        def _():
            acc_ukv[...] = jnp.zeros_like(acc_ukv)
            acc_dkv[...] = jnp.zeros_like(acc_dkv)
            dgkv_ref[...] = jnp.zeros_like(dgkv_ref)
            dg_ref[...] = jnp.zeros_like(dg_ref)
        ckn, chat, crstd = _rms(c_ref[...], gkv_ref[...])
        cknb = ckn.astype(BF16)
        dckn = jnp.zeros((tm, KV_RANK), F32)
        dkr = jnp.zeros((tm, ROPE_PAD), F32)
        for hd in range(N_HEADS):
            dk = dk_refs[0][hd].astype(F32)
            dv = dv_refs[0][hd].astype(F32)
            for j in range(1, n_b):
                dk = dk + dk_refs[j][hd].astype(F32)
                dv = dv + dv_refs[j][hd].astype(F32)
            dkr = dkr + dk[:, NOPE:QK]
            dkvh = jnp.concatenate([dk[:, 0:NOPE].astype(BF16), dv.astype(BF16)], axis=1)
            dckn = dckn + _nt(dkvh, wukv_ref[hd])
            acc_ukv[hd] += _tn(cknb, dkvh)
        dc_kv, dg_rows = _rms_bwd(dckn, chat, crstd, gkv_ref[...])
        dgkv_ref[...] += jnp.sum(dg_rows, axis=0, keepdims=True)
        dkr_raw = _rope_bwd(dkr, cos_ref[...], sin_ref[...])
        dckv = jnp.concatenate([dc_kv.astype(BF16), dkr_raw.astype(BF16)], axis=1)
        xk, xhat, rstd = _rms(h_ref[...], g_ref[...])
        acc_dkv[...] += _tn(xk.astype(BF16), dckv)
        dh, dg_rows = _rms_bwd(_nt(dckv, wdkv_ref[...]), xhat, rstd, g_ref[...])
        dh_ref[...] = dr_ref[...] + dh
        dg_ref[...] += jnp.sum(dg_rows, axis=0, keepdims=True)

        @pl.when(i == n_i - 1)
        def _():
            dwukv_ref[...] = acc_ukv[...].astype(BF16)
            dwdkv_ref[...] = acc_dkv[...].astype(BF16)

    kspec = pl.BlockSpec((N_HEADS, tm, QK), lambda i: (0, i, 0))
    vspec = pl.BlockSpec((N_HEADS, tm, VDIM), lambda i: (0, i, 0))
    rope = pl.BlockSpec((tm, ROPE_PAD), lambda i: (i, 0))
    row = pl.BlockSpec((tm, d), lambda i: (i, 0))
    vec = pl.BlockSpec((1, d), lambda i: (0, 0))
    return _call(body, name, (n_i,),
                 [kspec] * n_b + [vspec] * n_b + [
                     pl.BlockSpec((tm, KV_RANK), lambda i: (i, 0)), row, vec, row,
                     pl.BlockSpec((d, wk), lambda i: (0, 0)),
                     pl.BlockSpec((1, KV_RANK), lambda i: (0, 0)),
                     pl.BlockSpec((N_HEADS, KV_RANK, NOPE + VDIM), lambda i: (0, 0, 0)), rope, rope],
                 (row,
                  pl.BlockSpec((N_HEADS, KV_RANK, NOPE + VDIM), lambda i: (0, 0, 0)),
                  pl.BlockSpec((d, wk), lambda i: (0, 0)),
                  pl.BlockSpec((1, KV_RANK), lambda i: (0, 0)), vec),
                 (jax.ShapeDtypeStruct((seq, d), F32),
                  jax.ShapeDtypeStruct((N_HEADS, KV_RANK, NOPE + VDIM), BF16),
                  jax.ShapeDtypeStruct((d, wk), BF16),
                  jax.ShapeDtypeStruct((1, KV_RANK), F32),
                  jax.ShapeDtypeStruct((1, d), F32)),
                 (*dks, *dvs, c_kv, h, g, dres, wdkv, gkv, wukv, cos, sin),
                 scratch=[pltpu.VMEM((N_HEADS, KV_RANK, NOPE + VDIM), F32), pltpu.VMEM((d, wk), F32)],
                 rider=rider)


def o_bwd(dh, o, wo, name, rider=()):
    seq, d = dh.shape
    hv = o.shape[1]
    tm = min(ROW_TILE, seq)
    n_i = seq // tm

    def body(dh_ref, o_ref, wo_ref, do_ref, dwo_ref, acc):
        i = pl.program_id(0)

        @pl.when(i == 0)
        def _():
            acc[...] = jnp.zeros_like(acc)
        dhb = dh_ref[...].astype(BF16)
        do_ref[...] = _nt(dhb, wo_ref[...]).astype(BF16)
        acc[...] += _tn(o_ref[...], dhb)

        @pl.when(i == n_i - 1)
        def _():
            dwo_ref[...] = acc[...].astype(BF16)

    return _call(body, name, (n_i,),
                 [pl.BlockSpec((tm, d), lambda i: (i, 0)),
                  pl.BlockSpec((tm, hv), lambda i: (i, 0)),
                  pl.BlockSpec((hv, d), lambda i: (0, 0))],
                 (pl.BlockSpec((tm, hv), lambda i: (i, 0)),
                  pl.BlockSpec((hv, d), lambda i: (0, 0))),
                 (jax.ShapeDtypeStruct((seq, hv), BF16), jax.ShapeDtypeStruct((hv, d), BF16)),
                 (dh, o, wo), scratch=[pltpu.VMEM((hv, d), F32)], rider=rider)


def _mask_diagonal(s):
    row = lax.broadcasted_iota(jnp.int32, s.shape, 0)
    col = lax.broadcasted_iota(jnp.int32, s.shape, 1)
    return jnp.where(col <= row, s, NEG_BIG)


def attn_fwd(q, k, v, name, rider=()):
    _, seq, _ = q.shape
    t = min(ATTN_TILE, seq // 2)
    n_pair = seq // (2 * t)

    def body(q_ref, k_ref, v_ref, o_ref, lse_ref):
        qi = pl.program_id(1)
        q_a = q_ref[0:t, :]
        q_b = q_ref[t:2 * t, :]

        def rows(j):
            return pl.ds(pl.multiple_of(j * t, t), t)

        def update(qx, kb, vb, state, diagonal=False):
            m, l, acc = state
            s = _nt(qx, kb)
            if diagonal:
                s = _mask_diagonal(s)
            m_new = jnp.maximum(m, jnp.max(s, axis=1, keepdims=True))
            p = jnp.exp(s - m_new)
            alpha = jnp.exp(m - m_new)
            l = alpha * l + jnp.sum(p, axis=1, keepdims=True)
            acc = alpha * acc + _nn(p.astype(BF16), vb)
            return m_new, l, acc

        def step(j, carry):
            both = pl.ds(pl.multiple_of(j * 2 * t, 2 * t), 2 * t)
            kb, vb = k_ref[both, :], v_ref[both, :]
            return update(q_a, kb, vb, carry[0:3]) + update(q_b, kb, vb, carry[3:6])

        init = (jnp.full((t, 1), NEG_BIG, F32), jnp.zeros((t, 1), F32), jnp.zeros((t, VDIM), F32))
        carry = lax.fori_loop(0, qi, step, init + init)
        k0, v0 = k_ref[rows(2 * qi), :], v_ref[rows(2 * qi), :]
        k1, v1 = k_ref[rows(2 * qi + 1), :], v_ref[rows(2 * qi + 1), :]
        state_a = update(q_a, k0, v0, carry[0:3], diagonal=True)
        state_b = update(q_b, k1, v1, update(q_b, k0, v0, carry[3:6]), diagonal=True)
        for half, (m, l, acc) in enumerate((state_a, state_b)):
            o_ref[half * t:(half + 1) * t, :] = (acc / l).astype(BF16)
            lse_ref[half * t:(half + 1) * t, :] = jnp.broadcast_to(m + jnp.log(l), (t, LANES))

    return _call(body, name, (N_HEADS, n_pair),
                 [pl.BlockSpec((None, 2 * t, QK), lambda h, i: (h, i, 0)),
                  pl.BlockSpec((None, seq, QK), lambda h, i: (h, 0, 0)),
                  pl.BlockSpec((None, seq, VDIM), lambda h, i: (h, 0, 0))],
                 (pl.BlockSpec((2 * t, VDIM), lambda h, i: (i, h)),
                  pl.BlockSpec((None, 2 * t, LANES), lambda h, i: (h, i, 0))),
                 (jax.ShapeDtypeStruct((seq, N_HEADS * VDIM), BF16),
                  jax.ShapeDtypeStruct((N_HEADS, seq, LANES), F32)),
                 (q, k, v), rider=rider)


def attn_bwd(q, k, v, o, do, lse, name, rider=()):
    _, seq, _ = q.shape
    t = min(ATTN_TILE, seq // 2)
    n_q = seq // t
    n_pair = n_q // 2

    def body(q_ref, k_ref, v_ref, o_ref, do_ref, lse_ref, dq_ref, dk_ref, dv_ref,
             dq_acc, dk_acc, dv_acc):
        kj = pl.program_id(1)

        @pl.when(kj == 0)
        def _():
            dq_acc[...] = jnp.zeros_like(dq_acc)
        halves = (slice(0, t), slice(t, 2 * t))

        def block(i, masks, n_rows=t):
            rows = pl.ds(pl.multiple_of(i * n_rows, n_rows), n_rows)
            qb = q_ref[rows, :]
            dob = do_ref[rows, :]
            lse_col = lse_ref[rows, 0:1]
            delta = jnp.sum(dob.astype(F32) * o_ref[rows, :].astype(F32), axis=1, keepdims=True)
            dq, out = None, {}
            for x, diagonal in enumerate(masks):
                if diagonal is None:
                    continue
                kb, vb = k_ref[halves[x], :], v_ref[halves[x], :]
                s = _nt(qb, kb)
                if diagonal:
                    s = _mask_diagonal(s)
                p = jnp.exp(s - lse_col)
                ds = (p * (_nt(dob, vb) - delta)).astype(BF16)
                out[x] = (_tn(p.astype(BF16), dob), _tn(ds, qb))
                part = _nn(ds, kb)
                dq = part if dq is None else dq + part
            dq_acc[rows, :] += dq
            return out

        first = block(2 * kj, (True, None))
        second = block(2 * kj + 1, (False, True))
        dv_acc[halves[0], :] = first[0][0] + second[0][0]
        dk_acc[halves[0], :] = first[0][1] + second[0][1]
        dv_acc[halves[1], :] = second[1][0]
        dk_acc[halves[1], :] = second[1][1]

        def step(i, carry):
            out = block(i, (False, False), n_rows=2 * t)
            for x in (0, 1):
                dv_acc[halves[x], :] += out[x][0]
                dk_acc[halves[x], :] += out[x][1]
            return carry

        lax.fori_loop(kj + 1, n_pair, step, 0)
        dk_ref[...] = dk_acc[...].astype(BF16)
        dv_ref[...] = dv_acc[...].astype(BF16)

        @pl.when(kj == n_pair - 1)
        def _():
            dq_ref[...] = dq_acc[...].astype(BF16)

    head_rows = pl.BlockSpec((seq, VDIM), lambda h, j: (0, h))
    return _call(body, name, (N_HEADS, n_pair),
                 [pl.BlockSpec((None, seq, QK), lambda h, j: (h, 0, 0)),
                  pl.BlockSpec((None, 2 * t, QK), lambda h, j: (h, j, 0)),
                  pl.BlockSpec((None, 2 * t, VDIM), lambda h, j: (h, j, 0)),
                  head_rows, head_rows,
                  pl.BlockSpec((None, seq, LANES), lambda h, j: (h, 0, 0))],
                 (pl.BlockSpec((None, seq, QK), lambda h, j: (h, 0, 0)),
                  pl.BlockSpec((None, 2 * t, QK), lambda h, j: (h, j, 0)),
                  pl.BlockSpec((None, 2 * t, VDIM), lambda h, j: (h, j, 0))),
                 (jax.ShapeDtypeStruct((N_HEADS, seq, QK), BF16),
                  jax.ShapeDtypeStruct((N_HEADS, seq, QK), BF16),
                  jax.ShapeDtypeStruct((N_HEADS, seq, VDIM), BF16)),
                 (q, k, v, o, do, lse),
                 scratch=[pltpu.VMEM((seq, QK), F32), pltpu.VMEM((2 * t, QK), F32),
                          pltpu.VMEM((2 * t, VDIM), F32)], rider=rider)


def loss_head(h, g, target, name):
    seq, d = h.shape
    tm = min(ROW_TILE, seq)

    def body(h_ref, g_ref, t_ref, l_ref, dh_ref, dg_ref):
        i = pl.program_id(0)

        @pl.when(i == 0)
        def _():
            l_ref[...] = jnp.zeros_like(l_ref)
            dg_ref[...] = jnp.zeros_like(dg_ref)
        y, xhat, rstd = _rms(h_ref[...], g_ref[...])
        diff = y - t_ref[...]
        l_ref[...] += jnp.sum(jnp.sum(diff * diff, axis=1, keepdims=True), axis=0, keepdims=True)
        dh, dg_rows = _rms_bwd(diff * (1.0 / d), xhat, rstd, g_ref[...])
        dh_ref[...] = dh
        dg_ref[...] += jnp.sum(dg_rows, axis=0, keepdims=True)

    row = pl.BlockSpec((tm, d), lambda i: (i, 0))
    vec = pl.BlockSpec((1, d), lambda i: (0, 0))
    return _call(body, name, (seq // tm,), [row, vec, row],
                 (pl.BlockSpec((1, LANES), lambda i: (0, 0)), row, vec),
                 (jax.ShapeDtypeStruct((1, LANES), F32), jax.ShapeDtypeStruct((seq, d), F32),
                  jax.ShapeDtypeStruct((1, d), F32)),
                 (h, g, target))[0]


def _pack(parts):
    rows = []
    for p in parts:
        flat = p.reshape(-1)
        n_rows = -(-flat.shape[0] // (8 * LANES)) * 8
        flat = jnp.pad(flat, (0, n_rows * LANES - flat.shape[0]))
        rows.append(flat.reshape(n_rows, LANES))
    return jnp.concatenate(rows, axis=0)


def _unpack(packed, shapes):
    lead = packed.shape[:-2]
    out, r0 = [], 0
    for shape in shapes:
        size = 1
        for s in shape:
            size *= s
        n_rows = -(-size // (8 * LANES)) * 8
        part = packed[..., r0:r0 + n_rows, :].reshape(lead + (n_rows * LANES,))
        out.append(part[..., :size].reshape(lead + tuple(shape)))
        r0 += n_rows
    return out


FWD_RIDERS = {
    "mixer_fwd0": [("ffn_w_up", 0)],
    "ffn_fwd0": [("ffn_w_down", 0), ("a_w_in", 1), ("a_w_out", 1)],
    "ffn_out0": [("ffn_w_down", 1)],
    "mixer_fwd1": [("ffn_w_up", 1)],
    "ffn_fwd1": [("ffn_w_up", 2), ("w_dkv", 0), ("w_ukv", 0), ("b_w_dq", 0), ("b_w_uq", 0)],
    "attn_fwd0": [("b_w_o", 0), ("ffn_w_down", 2), ("ffn_w_up", 3), ("ffn_w_down", 3),
                  ("b_w_dq", 1), ("b_w_uq", 1), ("b_w_o", 1)],
}
BWD_RIDERS = {
    "attn_bwd1": [("ffn_w_down", 3), ("ffn_w_up", 3), ("b_w_o", 1)],
    "ffn_bwd2": [("b_w_uq", 1), ("b_w_dq", 1)],
    "attn_bwd0": [("ffn_w_down", 2), ("ffn_w_up", 2), ("b_w_o", 0)],
    "ffn_bwd1": [("b_w_uq", 0), ("b_w_dq", 0), ("w_ukv", 0), ("w_dkv", 0)],
    "ffn_in_bwd1": [("ffn_w_down", 1), ("ffn_w_up", 1, "pair")],
    "ffn_bwd0": [("ffn_w_up", 1, "chip"), ("a_w_in", 1), ("a_w_out", 1)],
    "ffn_in_bwd0": [("ffn_w_down", 0), ("ffn_w_up", 0, "pair")],
    "mixer_bwd0": [("ffn_w_up", 0, "chip")],
    "mixer_in_bwd0": [("a_w_out", 0), ("a_w_in", 0, "pair")],
    "adamw_a_w_out": [("a_w_in", 0, "chip")],
}


def kernel(x, a_mix_norm, a_w_in, a_conv, a_w_out, b_mix_norm, b_w_dq, b_q_norm, b_w_uq, b_w_o, kv_in_norm, w_dkv, kv_norm, w_ukv, ffn_norm, ffn_w_up, ffn_conv, ffn_w_down, final_norm, loss_target, m_a_mix_norm, m_a_w_in, m_a_conv, m_a_w_out, m_b_mix_norm, m_b_w_dq, m_b_q_norm, m_b_w_uq, m_b_w_o, m_kv_in_norm, m_w_dkv, m_kv_norm, m_w_ukv, m_ffn_norm, m_ffn_w_up, m_ffn_conv, m_ffn_w_down, m_final_norm, v_a_mix_norm, v_a_w_in, v_a_conv, v_a_w_out, v_b_mix_norm, v_b_w_dq, v_b_q_norm, v_b_w_uq, v_b_w_o, v_kv_in_norm, v_w_dkv, v_kv_norm, v_w_ukv, v_ffn_norm, v_ffn_w_up, v_ffn_conv, v_ffn_w_down, v_final_norm):
    seq, d = x.shape[1], x.shape[2]
    me = 4 * lax.axis_index("x") + 2 * lax.axis_index("y") + lax.axis_index("c")
    h0 = x.reshape(seq, d)
    target = loss_target.reshape(seq, d)
    cos, sin = _rope_tables(seq)
    rank = b_w_dq.shape[-1]
    f8 = ffn_w_up.shape[-1]
    fd = ffn_w_down.shape[1]
    dshard = a_w_out.shape[1]
    hv = N_HEADS * VDIM

    shards = {"a_w_in": a_w_in, "a_w_out": a_w_out, "b_w_dq": b_w_dq, "b_w_uq": b_w_uq,
              "b_w_o": b_w_o, "w_dkv": w_dkv[None], "w_ukv": w_ukv[None],
              "ffn_w_up": ffn_w_up, "ffn_w_down": ffn_w_down}

    def relayout(name, g):
        if name == "a_w_in":
            w = jnp.transpose(g, (1, 0, 2)).reshape(d, 3, d)
            return jnp.transpose(w, (1, 0, 2))
        if name == "a_w_out":
            return g.reshape(d, d)
        if name == "b_w_dq":
            return g.reshape(d, rank)
        if name == "b_w_uq":
            return jnp.pad(g, ((0, 0), (0, 0), (0, QK - NOPE - ROPE)))
        if name == "b_w_o":
            return g.reshape(hv, d)
        if name == "w_dkv":
            return jnp.pad(g.reshape(d, KV_RANK + ROPE), ((0, 0), (0, ROPE_PAD - ROPE)))
        if name == "ffn_w_down":
            return g.reshape(N_DEV // 2, 2 * fd, d)
        return g

    weights = {}

    def ag_rider(host):
        return [("ag", shards[n][l].astype(BF16)) for n, l in FWD_RIDERS.get(host, [])]

    def ag_done(host, outs):
        for (n, l), g in zip(FWD_RIDERS.get(host, []), outs):
            weights[n, l] = relayout(n, g)

    small_shapes = [a_mix_norm.shape, a_conv.shape, ffn_conv.shape]
    first = exchange([("ag", a_w_in[0].astype(BF16)), ("ag", a_w_out[0].astype(BF16)),
                      ("ag", _pack([a_mix_norm, a_conv, ffn_conv]))], "ag_first")
    weights["a_w_in", 0] = relayout("a_w_in", first[0])
    weights["a_w_out", 0] = relayout("a_w_out", first[1])
    s_mix, s_aconv, s_fconv = _unpack(first[2], small_shapes)
    a_gain = jnp.transpose(s_mix, (1, 0, 2)).reshape(N_A, d)
    a_cw = jnp.transpose(s_aconv, (1, 2, 0, 3)).reshape(N_A, 3, d)
    f_cw = jnp.transpose(s_fconv, (1, 0, 2, 3))

    def mixer_gain(layer):
        if layer >= DEPTH:
            return None
        return a_gain[layer][None] if layer < N_A else b_mix_norm[layer - N_A][None]

    saved = {}
    h = h0
    xn = norm_fwd(h, mixer_gain(0), "norm_first")
    kv = None
    for layer in range(DEPTH):
        saved["hm", layer], saved["xm", layer] = h, xn
        if layer < N_A:
            name = f"mixer_fwd{layer}"
            (u4, z), r = mixer_fwd(xn, weights["a_w_in", layer], a_cw[layer], name, rider=ag_rider(name))
            ag_done(name, r)
            saved["mix", layer] = (u4, z)
            name = f"mixer_out{layer}"
            (h, xn), r = proj_residual(z[None], weights["a_w_out", layer][None], h, name,
                                       g_next=ffn_norm[layer][None], rider=ag_rider(name))
            ag_done(name, r)
        else:
            j = layer - N_A
            name = f"q_fwd{j}"
            (q,), r = q_fwd(xn, weights["b_w_dq", j], b_q_norm[j][None], weights["b_w_uq", j],
                            cos, sin, name, rider=ag_rider(name))
            ag_done(name, r)
            name = f"attn_fwd{j}"
            (o, lse), r = attn_fwd(q, kv[0], kv[1], name, rider=ag_rider(name))
            ag_done(name, r)
            saved["attn", layer] = (q, o, lse)
            name = f"attn_out{j}"
            (h, xn), r = proj_residual(o[None], weights["b_w_o", j][None], h, name,
                                       g_next=ffn_norm[layer][None], rider=ag_rider(name))
            ag_done(name, r)
        saved["hf", layer], saved["xf", layer] = h, xn
        name = f"ffn_fwd{layer}"
        (up2, cv2, act), r = ffn_fwd(xn, weights["ffn_w_up", layer], f_cw[layer], name, rider=ag_rider(name))
        ag_done(name, r)
        saved["ffn", layer] = (up2, cv2, act)
        name = f"ffn_out{layer}"
        (h, xn), r = proj_residual(act, weights["ffn_w_down", layer], h, name,
                                   g_next=mixer_gain(layer + 1), rider=ag_rider(name))
        ag_done(name, r)
        if layer == N_A - 1:
            (k_all, v_all, c_kv), r = kv_fwd(h, kv_in_norm[None], weights["w_dkv", 0], kv_norm[None],
                                             weights["w_ukv", 0], cos, sin, "kv_fwd",
                                             rider=ag_rider("kv_fwd"))
            ag_done("kv_fwd", r)
            kv = (k_all, v_all, c_kv)

    sq_err, dh, d_final = loss_head(h, final_norm[None], target, "loss_head")
    loss = lax.psum(sq_err[0, 0] * (0.5 / d), ("x", "y", "c"))

    grads = {}
    parts = {}

    pair_sums = {}

    def by_chip(g):
        return g.reshape((N_DEV // 2, 2) + g.shape[1:])

    def rs_rider(host):
        tasks = []
        for key in BWD_RIDERS.get(host, []):
            if len(key) == 2:
                tasks.append(("rs", grads[key]))
            elif key[2] == "pair":
                tasks.append(("rs_pair", by_chip(grads[key[:2]])))
            else:
                tasks.append(("rs_chip", pair_sums[key[:2]]))
        return tasks

    def rs_done(host, outs):
        for key, p in zip(BWD_RIDERS.get(host, []), outs):
            if len(key) == 3 and key[2] == "pair":
                pair_sums[key[:2]] = pair_sum(by_chip(grads[key[:2]]), p, f"pair_sum_{key[0]}{key[1]}")
            else:
                parts[key[:2]] = p

    d_ffn_norm = [None] * DEPTH
    d_fconv = [None] * DEPTH
    d_a_gain = [None] * N_A
    d_aconv = [None] * N_A
    d_b_gain = [None] * N_B
    d_q_gain = [None] * N_B
    dks, dvs = [], []
    for layer in reversed(range(DEPTH)):
        if layer == N_A - 1:
            hk = saved["hm", layer + 1]
            (dh, dwukv, dwdkv, d_kv_gain, d_kvin_gain), r = kv_bwd(
                dks, dvs, kv[2], hk, kv_in_norm[None], dh, weights["w_dkv", 0], kv_norm[None],
                weights["w_ukv", 0], cos, sin, "kv_bwd", rider=rs_rider("kv_bwd"))
            rs_done("kv_bwd", r)
            grads["w_ukv", 0] = dwukv
            grads["w_dkv", 0] = dwdkv[:, :KV_RANK + ROPE].reshape(N_DEV, dshard, KV_RANK + ROPE)
        up2, cv2, act = saved["ffn", layer]
        name = f"ffn_bwd{layer}"
        (dup2, dwup, dwdown, dcw), r = ffn_bwd(dh, weights["ffn_w_down", layer], up2, cv2, act,
                                               saved["xf", layer], f_cw[layer], name, rider=rs_rider(name))
        rs_done(name, r)
        grads["ffn_w_up", layer] = dwup.reshape(N_DEV, d, f8)
        grads["ffn_w_down", layer] = dwdown.reshape(N_DEV, fd, d)
        d_fconv[layer] = dcw.reshape(N_DEV, 3, f8)
        name = f"ffn_in_bwd{layer}"
        (dh, d_ffn_norm[layer]), r = proj_t_rms_bwd(dup2.reshape(N_DEV, seq, f8), weights["ffn_w_up", layer],
                                                    saved["hf", layer], ffn_norm[layer][None], dh, name,
                                                    rider=rs_rider(name))
        rs_done(name, r)
        hm, xm = saved["hm", layer], saved["xm", layer]
        if layer < N_A:
            u4, z = saved["mix", layer]
            name = f"mixer_bwd{layer}"
            (du3, dwin3, dwout, dcw), r = mixer_bwd(dh, weights["a_w_out", layer], u4, z, xm, a_cw[layer],
                                                    name, rider=rs_rider(name))
            rs_done(name, r)
            dwin = jnp.transpose(dwin3, (1, 0, 2)).reshape(d, N_DEV, 3 * d // N_DEV)
            grads["a_w_in", layer] = jnp.transpose(dwin, (1, 0, 2))
            grads["a_w_out", layer] = dwout.reshape(N_DEV, dshard, d)
            d_aconv[layer] = dcw
            name = f"mixer_in_bwd{layer}"
            extra = []
            if layer == 0:
                early_small = [
                    d_a_gain[1],
                    d_aconv[1],
                    jnp.concatenate(d_b_gain, axis=0),
                    jnp.concatenate(d_q_gain, axis=0),
                    d_kvin_gain[0],
                    d_kv_gain[0],
                    jnp.concatenate(d_ffn_norm, axis=0),
                    jnp.stack(d_fconv),
                    d_final[0],
                ]
                extra = [("ag", _pack(early_small))]
            (dh, d_a_gain[layer]), r = proj_t_rms_bwd(du3, weights["a_w_in", layer], hm, a_gain[layer][None],
                                                      dh, name, rider=rs_rider(name) + extra)
            rs_done(name, r)
            if layer == 0:
                g_early = r[-1]
        else:
            j = layer - N_A
            q, o, lse = saved["attn", layer]
            name = f"attn_out_bwd{j}"
            (do, dwo), r = o_bwd(dh, o, weights["b_w_o", j], name, rider=rs_rider(name))
            rs_done(name, r)
            grads["b_w_o", j] = dwo.reshape(N_DEV, dshard, d)
            name = f"attn_bwd{j}"
            (dq, dk, dv), r = attn_bwd(q, kv[0], kv[1], o, do, lse, name, rider=rs_rider(name))
            rs_done(name, r)
            dks.append(dk)
            dvs.append(dv)
            name = f"q_bwd{j}"
            (dh, dwuq, dwdq, d_q_gain[j], d_b_gain[j]), r = q_bwd(
                dq, xm, hm, b_mix_norm[j][None], dh, weights["b_w_dq", j], b_q_norm[j][None],
                weights["b_w_uq", j], cos, sin, name, rider=rs_rider(name))
            rs_done(name, r)
            grads["b_w_uq", j] = dwuq[:, :, :NOPE + ROPE]
            grads["b_w_dq", j] = dwdq.reshape(N_DEV, dshard, rank)
    grad_x = dh.reshape(x.shape)

    late_small = [d_a_gain[0], d_aconv[0]]
    full_shapes = [t.shape for t in early_small + late_small]
    small_pack = _pack(late_small)

    res = {}

    def update(name, n_layers, w, m, v, extra=(), transposed=False):
        view = (lambda t: jnp.transpose(t, (0, 2, 1))) if transposed else (lambda t: t)
        call = sum_adamw_transposed if transposed else sum_adamw
        shard = w.shape if w.ndim == 3 else (1,) + w.shape
        host = f"adamw_{name}"
        outs, r = call([parts[name, l] for l in range(n_layers)], view(w.reshape(shard)),
                       view(m.reshape(shard)), view(v.reshape(shard)), host,
                       rider=rs_rider(host) + list(extra))
        rs_done(host, r)
        res[name] = [view(t).reshape(w.shape) for t in outs]
        return r[len(BWD_RIDERS.get(host, [])):]

    (g_late,) = update("a_w_out", N_A, a_w_out, m_a_w_out, v_a_w_out, extra=[("ag", small_pack)])
    update("ffn_w_down", DEPTH, ffn_w_down, m_ffn_w_down, v_ffn_w_down)
    update("ffn_w_up", DEPTH, ffn_w_up, m_ffn_w_up, v_ffn_w_up, transposed=True)
    update("b_w_dq", N_B, b_w_dq, m_b_w_dq, v_b_w_dq)
    update("b_w_uq", N_B, b_w_uq, m_b_w_uq, v_b_w_uq)
    update("b_w_o", N_B, b_w_o, m_b_w_o, v_b_w_o)
    update("w_dkv", 1, w_dkv, m_w_dkv, v_w_dkv)
    update("w_ukv", 1, w_ukv, m_w_ukv, v_w_ukv)
    update("a_w_in", N_A, a_w_in, m_a_w_in, v_a_w_in)

    summed = sum_slots(jnp.concatenate([g_early, g_late], axis=1), "sum_small_grads")
    (s_a_gain1, s_aconv1, s_b_gain, s_q_gain, s_kvin, s_kvn, s_ffn_gain, s_fconv_g,
     s_final, s_a_gain0, s_aconv0) = _unpack(summed, full_shapes)
    s_a_gain = jnp.concatenate([s_a_gain0, s_a_gain1], axis=0)
    s_aconv_g = jnp.stack([s_aconv0, s_aconv1])
    dsl = d // N_DEV
    small = [
        ("a_mix_norm", lax.dynamic_slice_in_dim(s_a_gain, me * dsl, dsl, axis=1), a_mix_norm, m_a_mix_norm, v_a_mix_norm),
        ("a_conv", lax.dynamic_slice_in_dim(s_aconv_g, me * dsl, dsl, axis=2), a_conv, m_a_conv, v_a_conv),
        ("b_mix_norm", s_b_gain, b_mix_norm, m_b_mix_norm, v_b_mix_norm),
        ("b_q_norm", s_q_gain, b_q_norm, m_b_q_norm, v_b_q_norm),
        ("kv_in_norm", s_kvin, kv_in_norm, m_kv_in_norm, v_kv_in_norm),
        ("kv_norm", s_kvn, kv_norm, m_kv_norm, v_kv_norm),
        ("ffn_norm", s_ffn_gain, ffn_norm, m_ffn_norm, v_ffn_norm),
        ("ffn_conv", lax.dynamic_index_in_dim(s_fconv_g, me, axis=1, keepdims=False), ffn_conv, m_ffn_conv, v_ffn_conv),
        ("final_norm", s_final, final_norm, m_final_norm, v_final_norm),
    ]
    shapes = [t[2].shape for t in small]
    packed = [_pack([t[k] for t in small])[None] for k in (1, 2, 3, 4)]
    outs, _ = sum_adamw([packed[0]], packed[1], packed[2], packed[3], "adamw_small")
    unpacked = [_unpack(t[0], shapes) for t in outs]
    for idx, t in enumerate(small):
        res[t[0]] = [unpacked[k][idx] for k in range(4)]

    order = ["a_mix_norm", "a_w_in", "a_conv", "a_w_out", "b_mix_norm", "b_w_dq", "b_q_norm",
             "b_w_uq", "b_w_o", "kv_in_norm", "w_dkv", "kv_norm", "w_ukv", "ffn_norm",
             "ffn_w_up", "ffn_conv", "ffn_w_down", "final_norm"]
    return (loss, grad_x, *[res[n][0] for n in order], *[res[n][1] for n in order],
            *[res[n][2] for n in order], *[res[n][3] for n in order])
```

```python
import functools

import jax
import jax.numpy as jnp
from jax import lax
from jax.experimental import pallas as pl
from jax.experimental.pallas import tpu as pltpu

F32 = jnp.float32
BF16 = jnp.bfloat16

N_DEV = 8
N_HEADS = 8
NOPE = 128
ROPE = 64
ROPE_PAD = 128
QK = NOPE + ROPE_PAD
VDIM = 128
KV_RANK = 256
ROPE_THETA = 10000.0
RMS_EPS = 1e-6
ATTN_SCALE = (NOPE + ROPE) ** -0.5
N_A = 2
N_B = 2
DEPTH = 4

ADAM_LR = 0.001
ADAM_B1 = 0.9
ADAM_B2 = 0.999
ADAM_EPS = 1e-08
ADAM_WD = 0.01
ADAM_STEP = 10

V7X_VMEM_LIMIT = 56 * 1024 * 1024
BF16_SUBLANES = 16
ROW_TILE = 512
ROW_TILE_LARGE = 1024
ADAM_ROWS = 256
ATTN_TILE = 512
MIXER_CHUNK = 512
LANES = 128
NEG_BIG = -1e30
COPIES_PER_TASK = 7

MESH_ID = pl.DeviceIdType.MESH
ANY = pl.BlockSpec(memory_space=pl.ANY)


def _nt(a, b):
    return lax.dot_general(a, b, (((1,), (1,)), ((), ())), preferred_element_type=F32)


def _tn(a, b):
    return lax.dot_general(a, b, (((0,), (0,)), ((), ())), preferred_element_type=F32)


def _nn(a, b):
    return jnp.dot(a, b, preferred_element_type=F32)


def _rms(h, g):
    rstd = lax.rsqrt(jnp.mean(h * h, axis=-1, keepdims=True) + RMS_EPS)
    xhat = h * rstd
    return xhat * g, xhat, rstd


def _rms_bwd(dxn, xhat, rstd, g):
    dxhat = dxn * g
    dh = rstd * (dxhat - xhat * jnp.mean(dxhat * xhat, axis=-1, keepdims=True))
    return dh, dxn * xhat


def _shift_down(x, k, halo_rows):
    r = pltpu.roll(x, k, 0)
    row = lax.broadcasted_iota(jnp.int32, x.shape, 0)
    for t in range(k):
        r = jnp.where(row == t, halo_rows[t], r)
    return r


def _shift_up(x, k, halo_rows):
    n = x.shape[0]
    r = pltpu.roll(x, n - k, 0)
    row = lax.broadcasted_iota(jnp.int32, x.shape, 0)
    for t in range(k):
        r = jnp.where(row == n - k + t, halo_rows[t], r)
    return r


def _conv_taps(w_ref):
    return w_ref[0:1, :], w_ref[1:2, :], w_ref[2:3, :]


def _rope_swap(x):
    lane = lax.broadcasted_iota(jnp.int32, x.shape, 1)
    return jnp.where(lane < ROPE // 2, pltpu.roll(x, ROPE_PAD - ROPE // 2, 1),
                     pltpu.roll(x, ROPE // 2, 1))


def _rope_fwd(x, cos, sin):
    return x * cos + _rope_swap(x) * sin


def _rope_bwd(dy, cos, sin):
    return dy * cos - _rope_swap(dy) * sin


def _rope_tables(seq):
    inv = 1.0 / (ROPE_THETA ** (jnp.arange(0, ROPE, 2, dtype=F32) / ROPE))
    ang = jnp.arange(seq, dtype=F32)[:, None] * inv[None, :]
    cos, sin = jnp.cos(ang), jnp.sin(ang)
    zero = jnp.zeros((seq, ROPE_PAD - ROPE), F32)
    return (jnp.concatenate([cos, cos, zero], axis=1),
            jnp.concatenate([-sin, sin, zero], axis=1))


def _row_tile(rows, cap, mult=8):
    best = None
    for t in range(mult, min(rows, cap) + 1, mult):
        if rows % t == 0:
            best = t
    return rows if best is None else best


class _AllGatherTask:
    def __init__(self, t, x_ref, out_ref, send_sems, recv_sems, local_sems):
        self.t, self.x_ref, self.out_ref = t, x_ref, out_ref
        self.send_sems, self.recv_sems, self.local_sems = send_sems, recv_sems, local_sems
        mx, my, mc = lax.axis_index("x"), lax.axis_index("y"), lax.axis_index("c")
        self.mc = mc
        self.me, self.sibling = (mx, my, mc), (mx, my, 1 - mc)
        self.chips = [(1 - mx, my), (mx, 1 - my), (1 - mx, 1 - my)]

    def _slot(self, px, py, pc):
        return self.out_ref.at[4 * px + 2 * py + pc]

    def _copy(self, k, block, to, src=None):
        s = COPIES_PER_TASK * self.t + k
        return pltpu.make_async_remote_copy(
            src_ref=self._slot(*block) if src is None else src, dst_ref=self._slot(*block),
            send_sem=self.send_sems.at[s], recv_sem=self.recv_sems.at[s],
            device_id=to, device_id_type=MESH_ID)

    def _mine(self):
        return pltpu.make_async_copy(self.x_ref, self._slot(*self.me), self.local_sems.at[self.t])

    def _first(self):
        out = [self._copy(0, self.me, self.sibling, src=self.x_ref)]
        out += [self._copy(1 + j, self.me, (*chip, self.mc), src=self.x_ref)
                for j, chip in enumerate(self.chips)]
        return out

    def _passed(self):
        return [self._copy(4 + j, (*chip, self.mc), self.sibling) for j, chip in enumerate(self.chips)]

    def start(self):
        self._mine().start()
        for cp in self._first():
            cp.start()

    def forward(self):
        passed = self._passed()
        for j, chip in enumerate(self.chips):
            self._copy(1 + j, (*chip, self.mc), self.me).wait_recv()
            passed[j].start()

    def finish(self):
        self._copy(0, self.sibling, self.me).wait_recv()
        for j, chip in enumerate(self.chips):
            self._copy(4 + j, (*chip, 1 - self.mc), self.me).wait_recv()
        for cp in self._first() + self._passed():
            cp.wait_send()
        self._mine().wait()


class _ReduceScatterTask:
    def __init__(self, t, g_ref, out_ref, send_sems, recv_sems, local_sems):
        self.t, self.g_ref, self.out_ref = t, g_ref, out_ref
        self.send_sems, self.recv_sems, self.local_sems = send_sems, recv_sems, local_sems
        mx, my, mc = lax.axis_index("x"), lax.axis_index("y"), lax.axis_index("c")
        self.me = 4 * mx + 2 * my + mc
        self.peers = []
        for k in range(1, N_DEV):
            px, py, pc = mx ^ ((k >> 2) & 1), my ^ ((k >> 1) & 1), mc ^ (k & 1)
            self.peers.append(((px, py, pc), 4 * px + 2 * py + pc))

    def _mine(self):
        return pltpu.make_async_copy(self.g_ref.at[self.me], self.out_ref.at[self.me],
                                     self.local_sems.at[self.t])

    def _copy(self, k, src_slot, dst_slot):
        s = COPIES_PER_TASK * self.t + k
        return pltpu.make_async_remote_copy(
            src_ref=self.g_ref.at[src_slot], dst_ref=self.out_ref.at[dst_slot],
            send_sem=self.send_sems.at[s], recv_sem=self.recv_sems.at[s],
            device_id=self.peers[k][0], device_id_type=MESH_ID)

    def start(self):
        self._mine().start()
        for k, (_, peer) in enumerate(self.peers):
            self._copy(k, peer, self.me).start()

    def forward(self):
        pass

    def finish(self):
        for k, (_, peer) in enumerate(self.peers):
            self._copy(k, self.me, peer).wait_recv()
        for k, (_, peer) in enumerate(self.peers):
            self._copy(k, peer, self.me).wait_send()
        self._mine().wait()


class _PairExchangeTask:
    def __init__(self, t, g_ref, out_ref, send_sems, recv_sems, local_sems):
        mx, my, mc = lax.axis_index("x"), lax.axis_index("y"), lax.axis_index("c")
        s = COPIES_PER_TASK * t
        self.copy = pltpu.make_async_remote_copy(
            src_ref=g_ref.at[:, 1 - mc], dst_ref=out_ref,
            send_sem=send_sems.at[s], recv_sem=recv_sems.at[s],
            device_id=(mx, my, 1 - mc), device_id_type=MESH_ID)

    def start(self):
        self.copy.start()

    def forward(self):
        pass

    def finish(self):
        self.copy.wait()


class _ChipScatterTask:
    def __init__(self, t, s_ref, out_ref, send_sems, recv_sems, local_sems):
        self.t, self.s_ref, self.out_ref = t, s_ref, out_ref
        self.send_sems, self.recv_sems, self.local_sems = send_sems, recv_sems, local_sems
        mx, my, mc = lax.axis_index("x"), lax.axis_index("y"), lax.axis_index("c")
        self.chip = 2 * mx + my
        self.peers = []
        for k in range(1, N_DEV // 2):
            px, py = mx ^ ((k >> 1) & 1), my ^ (k & 1)
            self.peers.append(((px, py, mc), 2 * px + py))

    def _mine(self):
        return pltpu.make_async_copy(self.s_ref.at[self.chip], self.out_ref.at[self.chip],
                                     self.local_sems.at[self.t])

    def _copy(self, k, src_slot, dst_slot):
        s = COPIES_PER_TASK * self.t + k
        return pltpu.make_async_remote_copy(
            src_ref=self.s_ref.at[src_slot], dst_ref=self.out_ref.at[dst_slot],
            send_sem=self.send_sems.at[s], recv_sem=self.recv_sems.at[s],
            device_id=self.peers[k][0], device_id_type=MESH_ID)

    def start(self):
        self._mine().start()
        for k, (_, peer) in enumerate(self.peers):
            self._copy(k, peer, self.chip).start()

    def forward(self):
        pass

    def finish(self):
        for k, (_, peer) in enumerate(self.peers):
            self._copy(k, self.chip, peer).wait_recv()
        for k, (_, peer) in enumerate(self.peers):
            self._copy(k, peer, self.chip).wait_send()
        self._mine().wait()


_TASKS = {"ag": _AllGatherTask, "rs": _ReduceScatterTask, "rs_pair": _PairExchangeTask,
          "rs_chip": _ChipScatterTask}


def _task_shape(kind, arr):
    shape = {"ag": (N_DEV,) + arr.shape, "rs": arr.shape, "rs_chip": arr.shape,
             "rs_pair": arr.shape[:1] + arr.shape[2:]}[kind]
    return jax.ShapeDtypeStruct(shape, arr.dtype)


def _sem_shapes(n_tasks):
    return [pltpu.SemaphoreType.DMA((COPIES_PER_TASK * n_tasks,)),
            pltpu.SemaphoreType.DMA((COPIES_PER_TASK * n_tasks,)),
            pltpu.SemaphoreType.DMA((n_tasks,))]


def _make_tasks(rider, in_refs, out_refs, sems):
    return [_TASKS[kind](t, in_refs[t], out_refs[t], *sems) for t, (kind, _) in enumerate(rider)]


def exchange(rider, name):
    n = len(rider)

    def body(*refs):
        tasks = _make_tasks(rider, refs[:n], refs[n:2 * n], refs[2 * n:])
        for task in tasks:
            task.start()
        for task in tasks:
            task.forward()
        for task in tasks:
            task.finish()

    return list(pl.pallas_call(
        body, name=name, out_shape=tuple(_task_shape(k, a) for k, a in rider),
        in_specs=[ANY] * n, out_specs=(ANY,) * n, scratch_shapes=_sem_shapes(n),
    )(*[a for _, a in rider]))


def _call(body, name, grid, in_specs, out_specs, out_shape, args, scratch=(), rider=()):
    in_specs, out_specs, out_shape = list(in_specs), tuple(out_specs), tuple(out_shape)
    n_in, n_out, n_scr, n_r = len(in_specs), len(out_specs), len(scratch), len(rider)
    if n_r:
        def kern(*refs):
            ins, r_in = refs[:n_in], refs[n_in:n_in + n_r]
            o0 = n_in + n_r
            outs, r_out = refs[o0:o0 + n_out], refs[o0 + n_out:o0 + n_out + n_r]
            s0 = o0 + n_out + n_r
            scr, sems = refs[s0:s0 + n_scr], refs[s0 + n_scr:]
            step = 0
            for a, n in enumerate(grid):
                step = step * n + pl.program_id(a)
            n_steps = 1
            for n in grid:
                n_steps *= n

            @pl.when(step == 0)
            def _():
                for task in _make_tasks(rider, r_in, r_out, sems):
                    task.start()
            body(*ins, *outs, *scr)

            @pl.when(step == n_steps - 1)
            def _():
                tasks = _make_tasks(rider, r_in, r_out, sems)
                for task in tasks:
                    task.forward()
                for task in tasks:
                    task.finish()
    else:
        kern = body
    res = pl.pallas_call(
        kern, name=name, grid=grid,
        in_specs=in_specs + [ANY] * n_r, out_specs=out_specs + (ANY,) * n_r,
        out_shape=out_shape + tuple(_task_shape(k, a) for k, a in rider),
        scratch_shapes=list(scratch) + (_sem_shapes(n_r) if n_r else []),
        compiler_params=pltpu.CompilerParams(dimension_semantics=("arbitrary",) * len(grid),
                                             vmem_limit_bytes=V7X_VMEM_LIMIT),
    )(*args, *[a for _, a in rider])
    return list(res[:n_out]), list(res[n_out:])


def _adamw(g, w, m, v):
    m = ADAM_B1 * m + (1.0 - ADAM_B1) * g
    v = ADAM_B2 * v + (1.0 - ADAM_B2) * (g * g)
    m_hat = m / (1.0 - ADAM_B1 ** ADAM_STEP)
    v_hat = v / (1.0 - ADAM_B2 ** ADAM_STEP)
    delta = -ADAM_LR * (m_hat / (jnp.sqrt(v_hat) + ADAM_EPS) + ADAM_WD * w)
    return delta, m, v


def sum_adamw(parts, w, m, v, name, rider=()):
    n_l, rows, cols = w.shape
    mult = BF16_SUBLANES if parts[0].dtype == BF16 else 8
    tr = _row_tile(rows, ADAM_ROWS, mult)
    n_i = rows // tr

    def body(*refs):
        part_refs = refs[:n_l]
        w_ref, m_ref, v_ref, g_out, d_out, m_out, v_out = refs[n_l:]
        layer = pl.program_id(0)
        for k in range(n_l):
            @pl.when(layer == k)
            def _(k=k):
                g = part_refs[k][0].astype(F32)
                for s in range(1, parts[k].shape[0]):
                    g = g + part_refs[k][s].astype(F32)
                delta, m_new, v_new = _adamw(g, w_ref[...], m_ref[...], v_ref[...])
                g_out[...] = g
                d_out[...] = delta
                m_out[...] = m_new
                v_out[...] = v_new

    part_specs = [pl.BlockSpec((parts[k].shape[0], tr, cols), functools.partial(
        lambda l, i, k: (0, jnp.where(l == k, i, 0), 0), k=k)) for k in range(n_l)]
    wspec = pl.BlockSpec((None, tr, cols), lambda l, i: (l, i, 0))
    shape = jax.ShapeDtypeStruct(w.shape, F32)
    return _call(body, name, (n_l, n_i), part_specs + [wspec] * 3, (wspec,) * 4, (shape,) * 4,
                 (*parts, w, m, v), rider=rider)


def sum_adamw_transposed(parts, w_t, m_t, v_t, name, rider=()):
    n_l, cols, rows = w_t.shape
    tr = LANES
    n_i = rows // tr
    starts = list(range(0, cols - LANES + 1, LANES))
    if starts[-1] + LANES < cols:
        starts.append(cols - LANES)

    def body(*refs):
        part_refs = refs[:n_l]
        w_ref, m_ref, v_ref, g_out, d_out, m_out, v_out = refs[n_l:]
        layer = pl.program_id(0)
        for k in range(n_l):
            @pl.when(layer == k)
            def _(k=k):
                for c0 in starts:
                    piece = pl.ds(c0, LANES)
                    g = part_refs[k][0, :, piece].astype(F32)
                    for s in range(1, parts[k].shape[0]):
                        g = g + part_refs[k][s, :, piece].astype(F32)
                    g = g.T
                    delta, m_new, v_new = _adamw(g, w_ref[piece, :], m_ref[piece, :], v_ref[piece, :])
                    g_out[piece, :] = g
                    d_out[piece, :] = delta
                    m_out[piece, :] = m_new
                    v_out[piece, :] = v_new

    part_specs = [pl.BlockSpec((parts[k].shape[0], tr, cols), functools.partial(
        lambda l, i, k: (0, jnp.where(l == k, i, 0), 0), k=k)) for k in range(n_l)]
    wspec = pl.BlockSpec((None, cols, tr), lambda l, i: (l, 0, i))
    shape = jax.ShapeDtypeStruct(w_t.shape, F32)
    return _call(body, name, (n_l, n_i), part_specs + [wspec] * 3, (wspec,) * 4, (shape,) * 4,
                 (*parts, w_t, m_t, v_t), rider=rider)


def pair_sum(g4, other, name):
    n_chip, _, rows, cols = g4.shape
    tr = _row_tile(rows, 512, BF16_SUBLANES)

    def body(core_ref, g_ref, o_ref, s_ref):
        s_ref[...] = (g_ref[...].astype(F32) + o_ref[...].astype(F32)).astype(BF16)

    blk = pl.BlockSpec((None, tr, cols), lambda k, i, core: (k, i, 0))
    return pl.pallas_call(
        body, name=name, out_shape=jax.ShapeDtypeStruct((n_chip, rows, cols), g4.dtype),
        grid_spec=pltpu.PrefetchScalarGridSpec(
            num_scalar_prefetch=1, grid=(n_chip, rows // tr),
            in_specs=[pl.BlockSpec((None, None, tr, cols), lambda k, i, core: (k, core[0], i, 0)), blk],
            out_specs=blk),
        compiler_params=pltpu.CompilerParams(dimension_semantics=("arbitrary", "arbitrary"),
                                             vmem_limit_bytes=V7X_VMEM_LIMIT),
    )(lax.axis_index("c").astype(jnp.int32).reshape(1), g4, other)


def sum_slots(parts, name):
    n, rows, cols = parts.shape

    def body(p_ref, o_ref):
        acc = p_ref[0]
        for s in range(1, n):
            acc = acc + p_ref[s]
        o_ref[...] = acc

    return pl.pallas_call(
        body, name=name, out_shape=jax.ShapeDtypeStruct((rows, cols), F32),
        in_specs=[pl.BlockSpec(memory_space=pltpu.VMEM)],
        out_specs=pl.BlockSpec(memory_space=pltpu.VMEM),
    )(parts)


def norm_fwd(h, g, name):
    seq, d = h.shape
    tm = min(ROW_TILE, seq)

    def body(h_ref, g_ref, o_ref):
        o_ref[...] = _rms(h_ref[...], g_ref[...])[0].astype(BF16)

    return _call(body, name, (seq // tm,),
                 [pl.BlockSpec((tm, d), lambda i: (i, 0)), pl.BlockSpec((1, d), lambda i: (0, 0))],
                 [pl.BlockSpec((tm, d), lambda i: (i, 0))],
                 [jax.ShapeDtypeStruct((seq, d), BF16)], (h, g))[0][0]


def proj_residual(a, w, res, name, g_next=None, rider=()):
    nb, seq, kb = a.shape
    d = w.shape[-1]
    tm = min(ROW_TILE, seq)
    with_norm = g_next is not None

    def body(a_ref, w_ref, r_ref, *rest):
        acc = r_ref[...]
        for b in range(nb):
            acc = acc + _nn(a_ref[b], w_ref[b])
        if with_norm:
            g_ref, o_ref, xn_ref = rest
            xn_ref[...] = _rms(acc, g_ref[...])[0].astype(BF16)
        else:
            (o_ref,) = rest
        o_ref[...] = acc

    row = pl.BlockSpec((tm, d), lambda i: (i, 0))
    in_specs = [pl.BlockSpec((nb, tm, kb), lambda i: (0, i, 0)),
                pl.BlockSpec((nb, kb, d), lambda i: (0, 0, 0)), row]
    args = [a, w, res]
    out_specs, out_shape = [row], [jax.ShapeDtypeStruct((seq, d), F32)]
    if with_norm:
        in_specs.append(pl.BlockSpec((1, d), lambda i: (0, 0)))
        args.append(g_next)
        out_specs.append(row)
        out_shape.append(jax.ShapeDtypeStruct((seq, d), BF16))
    outs, r_outs = _call(body, name, (seq // tm,), in_specs, out_specs, out_shape, args, rider=rider)
    return (outs[0], outs[1] if with_norm else None), r_outs


def proj_t_rms_bwd(du, w, h, g, dres, name, rider=()):
    nb, seq, wd = du.shape
    k = w.shape[1]
    big_weight = 2 * w.size * w.dtype.itemsize > V7X_VMEM_LIMIT // 4
    tm = min(ROW_TILE // 2 if big_weight else ROW_TILE, seq)

    def body(du_ref, w_ref, h_ref, g_ref, dr_ref, dh_ref, dg_ref):
        i = pl.program_id(0)
        dxn = _nt(du_ref[0], w_ref[0])
        for b in range(1, nb):
            dxn = dxn + _nt(du_ref[b], w_ref[b])
        _, xhat, rstd = _rms(h_ref[...], g_ref[...])
        dh, dg_rows = _rms_bwd(dxn, xhat, rstd, g_ref[...])
        dh_ref[...] = dr_ref[...] + dh

        @pl.when(i == 0)
        def _():
            dg_ref[...] = jnp.zeros_like(dg_ref)
        dg_ref[...] += jnp.sum(dg_rows, axis=0, keepdims=True)

    row = pl.BlockSpec((tm, k), lambda i: (i, 0))
    vec = pl.BlockSpec((1, k), lambda i: (0, 0))
    return _call(body, name, (seq // tm,),
                 [pl.BlockSpec((nb, tm, wd), lambda i: (0, i, 0)),
                  pl.BlockSpec((nb, k, wd), lambda i: (0, 0, 0)), row, vec, row],
                 (row, vec),
                 (jax.ShapeDtypeStruct((seq, k), F32), jax.ShapeDtypeStruct((1, k), F32)),
                 (du, w, h, g, dres), rider=rider)


def mixer_fwd(xn, win3, cw, name, rider=()):
    seq, d = xn.shape
    tm = min(ROW_TILE_LARGE, seq)
    cc = min(MIXER_CHUNK, d)
    n_c, n_i = d // cc, seq // tm

    def body(x_ref, w_ref, cw_ref, u_ref, z_ref, carry):
        i = pl.program_id(1)

        @pl.when(i == 0)
        def _():
            carry[...] = jnp.zeros_like(carry)
        xb = x_ref[...]
        b = _nn(xb, w_ref[0])
        c = _nn(xb, w_ref[1])
        hh = _nn(xb, w_ref[2])
        p = c * hh
        w0, w1, w2 = _conv_taps(cw_ref)
        p1 = _shift_down(p, 1, [carry[7:8, :]])
        p2 = _shift_down(p, 2, [carry[6:7, :], carry[7:8, :]])
        q = w0 * p2 + w1 * p1 + w2 * p
        carry[...] = p[tm - 8:tm, :]
        u_ref[0] = b.astype(BF16)
        u_ref[1] = c.astype(BF16)
        u_ref[2] = hh.astype(BF16)
        u_ref[3] = q.astype(BF16)
        z_ref[...] = (b * q).astype(BF16)

    return _call(body, name, (n_c, n_i),
                 [pl.BlockSpec((tm, d), lambda c, i: (i, 0)),
                  pl.BlockSpec((3, d, cc), lambda c, i: (0, 0, c)),
                  pl.BlockSpec((3, cc), lambda c, i: (0, c))],
                 (pl.BlockSpec((4, tm, cc), lambda c, i: (0, i, c)),
                  pl.BlockSpec((tm, cc), lambda c, i: (i, c))),
                 (jax.ShapeDtypeStruct((4, seq, d), BF16), jax.ShapeDtypeStruct((seq, d), BF16)),
                 (xn, win3, cw), scratch=[pltpu.VMEM((8, cc), F32)], rider=rider)


def mixer_bwd(dh, wout, u4, z, xn, cw, name, rider=()):
    seq, d = xn.shape
    tm = min(ROW_TILE, seq)
    cc = min(MIXER_CHUNK, d)
    n_c, n_i = d // cc, seq // tm

    def body(dh_ref, wout_ref, u_ref, z_ref, x_ref, cw_ref,
             du_ref, dwin_ref, dwout_ref, dcw_ref, acc_in, acc_out, acc_cw, carry):
        i = pl.program_id(1)

        @pl.when(i == 0)
        def _():
            acc_in[...] = jnp.zeros_like(acc_in)
            acc_out[...] = jnp.zeros_like(acc_out)
            acc_cw[...] = jnp.zeros_like(acc_cw)
            carry[...] = jnp.zeros_like(carry)
        dhb = dh_ref[...].astype(BF16)
        dz = _nt(dhb, wout_ref[...])
        acc_out[...] += _tn(z_ref[...], dhb)
        b = u_ref[0].astype(F32)
        c = u_ref[1].astype(F32)
        hh = u_ref[2].astype(F32)
        q = u_ref[3].astype(F32)
        p = c * hh
        db = dz * q
        dq = dz * b
        w0, w1, w2 = _conv_taps(cw_ref)
        dq1 = _shift_up(dq, 1, [carry[0:1, :]])
        dq2 = _shift_up(dq, 2, [carry[0:1, :], carry[1:2, :]])
        dp = w2 * dq + w1 * dq1 + w0 * dq2
        carry[...] = dq[0:8, :]
        acc_cw[0:1, :] += jnp.sum(dq2 * p, axis=0, keepdims=True)
        acc_cw[1:2, :] += jnp.sum(dq1 * p, axis=0, keepdims=True)
        acc_cw[2:3, :] += jnp.sum(dq * p, axis=0, keepdims=True)
        dbb = db.astype(BF16)
        dcb = (dp * hh).astype(BF16)
        dhhb = (dp * c).astype(BF16)
        du_ref[0] = dbb
        du_ref[1] = dcb
        du_ref[2] = dhhb
        xb = x_ref[...]
        acc_in[0] += _tn(xb, dbb)
        acc_in[1] += _tn(xb, dcb)
        acc_in[2] += _tn(xb, dhhb)

        @pl.when(i == n_i - 1)
        def _():
            dwin_ref[...] = acc_in[...].astype(BF16)
            dwout_ref[...] = acc_out[...].astype(BF16)
            dcw_ref[...] = acc_cw[0:3, :]

    rev = lambda c, i: (n_i - 1 - i, 0)
    return _call(body, name, (n_c, n_i),
                 [pl.BlockSpec((tm, d), rev),
                  pl.BlockSpec((cc, d), lambda c, i: (c, 0)),
                  pl.BlockSpec((4, tm, cc), lambda c, i: (0, n_i - 1 - i, c)),
                  pl.BlockSpec((tm, cc), lambda c, i: (n_i - 1 - i, c)),
                  pl.BlockSpec((tm, d), rev),
                  pl.BlockSpec((3, cc), lambda c, i: (0, c))],
                 (pl.BlockSpec((3, tm, cc), lambda c, i: (0, n_i - 1 - i, c)),
                  pl.BlockSpec((3, d, cc), lambda c, i: (0, 0, c)),
                  pl.BlockSpec((cc, d), lambda c, i: (c, 0)),
                  pl.BlockSpec((3, cc), lambda c, i: (0, c))),
                 (jax.ShapeDtypeStruct((3, seq, d), BF16), jax.ShapeDtypeStruct((3, d, d), BF16),
                  jax.ShapeDtypeStruct((d, d), BF16), jax.ShapeDtypeStruct((3, d), F32)),
                 (dh, wout, u4, z, xn, cw),
                 scratch=[pltpu.VMEM((3, d, cc), F32), pltpu.VMEM((cc, d), F32),
                          pltpu.VMEM((8, cc), F32), pltpu.VMEM((8, cc), F32)], rider=rider)


def _silu_parts(cg):
    sg = 1.0 / (1.0 + jnp.exp(-cg))
    return sg, cg * sg


def ffn_fwd(xn, wup, fcw, name, rider=()):
    seq, d = xn.shape
    f8 = wup.shape[-1]
    half = N_DEV // 2
    tm = min(ROW_TILE_LARGE, seq)
    n_i = seq // tm

    def body(x_ref, wg_ref, wu_ref, cg_ref, cu_ref, up_ref, cv_ref, a_ref, carry):
        i = pl.program_id(1)

        @pl.when(i == 0)
        def _():
            carry[...] = jnp.zeros_like(carry)
        xb = x_ref[...]
        conv = []
        for s, (w_ref, t_ref) in enumerate(((wg_ref, cg_ref), (wu_ref, cu_ref))):
            u = _nn(xb, w_ref[...])
            up_ref[s] = u.astype(BF16)
            w0, w1, w2 = _conv_taps(t_ref)
            u1 = _shift_down(u, 1, [carry[s, 7:8, :]])
            u2 = _shift_down(u, 2, [carry[s, 6:7, :], carry[s, 7:8, :]])
            cv = w0 * u2 + w1 * u1 + w2 * u
            cv_ref[s] = cv.astype(BF16)
            conv.append(cv)
            carry[s] = u[tm - 8:tm, :]
        _, silu = _silu_parts(conv[0])
        a_ref[...] = (silu * conv[1]).astype(BF16)

    blk = pl.BlockSpec((2, None, tm, f8), lambda c, i: (0, c, i, 0))
    big = jax.ShapeDtypeStruct((2, half, seq, f8), BF16)
    return _call(body, name, (half, n_i),
                 [pl.BlockSpec((tm, d), lambda c, i: (i, 0)),
                  pl.BlockSpec((None, d, f8), lambda c, i: (c, 0, 0)),
                  pl.BlockSpec((None, d, f8), lambda c, i: (c + half, 0, 0)),
                  pl.BlockSpec((None, 3, f8), lambda c, i: (c, 0, 0)),
                  pl.BlockSpec((None, 3, f8), lambda c, i: (c + half, 0, 0))],
                 (blk, blk, pl.BlockSpec((None, tm, f8), lambda c, i: (c, i, 0))),
                 (big, big, jax.ShapeDtypeStruct((half, seq, f8), BF16)),
                 (xn, wup, wup, fcw, fcw), scratch=[pltpu.VMEM((2, 8, f8), F32)], rider=rider)


def ffn_bwd(dh, wdown, up2, cv2, act, xn, fcw, name, rider=()):
    seq, d = xn.shape
    f8 = up2.shape[-1]
    fb = wdown.shape[1]
    half = N_DEV // 2
    tm = min(ROW_TILE, seq)
    n_i = seq // tm

    def body(dh_ref, wd_ref, up_ref, cv_ref, a_ref, x_ref, cg_ref, cu_ref,
             dup_ref, dwup_ref, dwd_ref, dcw_ref, acc_up, acc_down, acc_cw, carry):
        i = pl.program_id(1)

        @pl.when(i == 0)
        def _():
            acc_up[...] = jnp.zeros_like(acc_up)
            acc_down[...] = jnp.zeros_like(acc_down)
            acc_cw[...] = jnp.zeros_like(acc_cw)
            carry[...] = jnp.zeros_like(carry)
        dhb = dh_ref[...].astype(BF16)
        da = _nt(dhb, wd_ref[...])
        acc_down[...] += _tn(a_ref[...], dhb)
        cg = cv_ref[0].astype(F32)
        cu = cv_ref[1].astype(F32)
        sg, silu = _silu_parts(cg)
        dcg = da * cu * (sg + silu * (1.0 - sg))
        dcu = da * silu
        xb = x_ref[...]
        for s, (dc, t_ref) in enumerate(((dcg, cg_ref), (dcu, cu_ref))):
            w0, w1, w2 = _conv_taps(t_ref)
            d1 = _shift_up(dc, 1, [carry[s, 0:1, :]])
            d2 = _shift_up(dc, 2, [carry[s, 0:1, :], carry[s, 1:2, :]])
            du = (w2 * dc + w1 * d1 + w0 * d2).astype(BF16)
            carry[s] = dc[0:8, :]
            u = up_ref[s].astype(F32)
            acc_cw[s, 0:1, :] += jnp.sum(d2 * u, axis=0, keepdims=True)
            acc_cw[s, 1:2, :] += jnp.sum(d1 * u, axis=0, keepdims=True)
            acc_cw[s, 2:3, :] += jnp.sum(dc * u, axis=0, keepdims=True)
            dup_ref[s] = du
            acc_up[s] += _tn(xb, du)

        @pl.when(i == n_i - 1)
        def _():
            dwup_ref[...] = acc_up[...].astype(BF16)
            dwd_ref[...] = acc_down[...].astype(BF16)
            dcw_ref[...] = acc_cw[:, 0:3, :]

    rev = lambda c, i: (n_i - 1 - i, 0)
    blk = pl.BlockSpec((2, None, tm, f8), lambda c, i: (0, c, n_i - 1 - i, 0))
    return _call(body, name, (half, n_i),
                 [pl.BlockSpec((tm, d), rev),
                  pl.BlockSpec((None, fb, d), lambda c, i: (c, 0, 0)),
                  blk, blk,
                  pl.BlockSpec((None, tm, f8), lambda c, i: (c, n_i - 1 - i, 0)),
                  pl.BlockSpec((tm, d), rev),
                  pl.BlockSpec((None, 3, f8), lambda c, i: (c, 0, 0)),
                  pl.BlockSpec((None, 3, f8), lambda c, i: (c + half, 0, 0))],
                 (blk,
                  pl.BlockSpec((2, None, d, f8), lambda c, i: (0, c, 0, 0)),
                  pl.BlockSpec((None, fb, d), lambda c, i: (c, 0, 0)),
                  pl.BlockSpec((2, None, 3, f8), lambda c, i: (0, c, 0, 0))),
                 (jax.ShapeDtypeStruct((2, half, seq, f8), BF16),
                  jax.ShapeDtypeStruct((2, half, d, f8), BF16),
                  jax.ShapeDtypeStruct((half, fb, d), BF16),
                  jax.ShapeDtypeStruct((2, half, 3, f8), F32)),
                 (dh, wdown, up2, cv2, act, xn, fcw, fcw),
                 scratch=[pltpu.VMEM((2, d, f8), F32), pltpu.VMEM((fb, d), F32),
                          pltpu.VMEM((2, 8, f8), F32), pltpu.VMEM((2, 8, f8), F32)], rider=rider)


def q_fwd(xn, wdq, gq, wuq, cos, sin, name, rider=()):
    seq, d = xn.shape
    rank = wdq.shape[-1]
    tm = min(ROW_TILE, seq)

    def body(x_ref, wdq_ref, gq_ref, wuq_ref, cos_ref, sin_ref, q_ref):
        qc = _nn(x_ref[...], wdq_ref[...])
        qn = _rms(qc, gq_ref[...])[0].astype(BF16)
        for hd in range(N_HEADS):
            qh = _nn(qn, wuq_ref[hd])
            qr = _rope_fwd(qh[:, NOPE:QK], cos_ref[...], sin_ref[...])
            q_ref[hd, :, 0:NOPE] = (qh[:, 0:NOPE] * ATTN_SCALE).astype(BF16)
            q_ref[hd, :, NOPE:QK] = (qr * ATTN_SCALE).astype(BF16)

    rope = pl.BlockSpec((tm, ROPE_PAD), lambda i: (i, 0))
    return _call(body, name, (seq // tm,),
                 [pl.BlockSpec((tm, d), lambda i: (i, 0)),
                  pl.BlockSpec((d, rank), lambda i: (0, 0)),
                  pl.BlockSpec((1, rank), lambda i: (0, 0)),
                  pl.BlockSpec((N_HEADS, rank, QK), lambda i: (0, 0, 0)), rope, rope],
                 [pl.BlockSpec((N_HEADS, tm, QK), lambda i: (0, i, 0))],
                 [jax.ShapeDtypeStruct((N_HEADS, seq, QK), BF16)],
                 (xn, wdq, gq, wuq, cos, sin), rider=rider)


def q_bwd(dq, xn, h, g, dres, wdq, gq, wuq, cos, sin, name, rider=()):
    seq, d = xn.shape
    rank = wdq.shape[-1]
    tm = min(ROW_TILE, seq)
    n_i = seq // tm

    def body(dq_ref, x_ref, h_ref, g_ref, dr_ref, wdq_ref, gq_ref, wuq_ref, cos_ref, sin_ref,
             dh_ref, dwuq_ref, dwdq_ref, dgq_ref, dg_ref, acc_uq, acc_dq):
        i = pl.program_id(0)

        @pl.when(i == 0)
        def _():
            acc_uq[...] = jnp.zeros_like(acc_uq)
            acc_dq[...] = jnp.zeros_like(acc_dq)
            dgq_ref[...] = jnp.zeros_like(dgq_ref)
            dg_ref[...] = jnp.zeros_like(dg_ref)
        xb = x_ref[...]
        qc = _nn(xb, wdq_ref[...])
        qn, qhat, qrstd = _rms(qc, gq_ref[...])
        qnb = qn.astype(BF16)
        dqn = jnp.zeros((tm, rank), F32)
        for hd in range(N_HEADS):
            dnope = (dq_ref[hd, :, 0:NOPE].astype(F32) * ATTN_SCALE).astype(BF16)
            drope = _rope_bwd(dq_ref[hd, :, NOPE:QK].astype(F32) * ATTN_SCALE, cos_ref[...], sin_ref[...])
            draw = jnp.concatenate([dnope, drope.astype(BF16)], axis=1)
            dqn = dqn + _nt(draw, wuq_ref[hd])
            acc_uq[hd] += _tn(qnb, draw)
        dqc, dg_rows = _rms_bwd(dqn, qhat, qrstd, gq_ref[...])
        dgq_ref[...] += jnp.sum(dg_rows, axis=0, keepdims=True)
        dqcb = dqc.astype(BF16)
        acc_dq[...] += _tn(xb, dqcb)
        _, xhat, rstd = _rms(h_ref[...], g_ref[...])
        dh, dg_rows = _rms_bwd(_nt(dqcb, wdq_ref[...]), xhat, rstd, g_ref[...])
        dh_ref[...] = dr_ref[...] + dh
        dg_ref[...] += jnp.sum(dg_rows, axis=0, keepdims=True)

        @pl.when(i == n_i - 1)
        def _():
            dwuq_ref[...] = acc_uq[...].astype(BF16)
            dwdq_ref[...] = acc_dq[...].astype(BF16)

    rope = pl.BlockSpec((tm, ROPE_PAD), lambda i: (i, 0))
    row = pl.BlockSpec((tm, d), lambda i: (i, 0))
    vec = pl.BlockSpec((1, d), lambda i: (0, 0))
    return _call(body, name, (n_i,),
                 [pl.BlockSpec((N_HEADS, tm, QK), lambda i: (0, i, 0)), row, row, vec, row,
                  pl.BlockSpec((d, rank), lambda i: (0, 0)),
                  pl.BlockSpec((1, rank), lambda i: (0, 0)),
                  pl.BlockSpec((N_HEADS, rank, QK), lambda i: (0, 0, 0)), rope, rope],
                 (row,
                  pl.BlockSpec((N_HEADS, rank, QK), lambda i: (0, 0, 0)),
                  pl.BlockSpec((d, rank), lambda i: (0, 0)),
                  pl.BlockSpec((1, rank), lambda i: (0, 0)), vec),
                 (jax.ShapeDtypeStruct((seq, d), F32),
                  jax.ShapeDtypeStruct((N_HEADS, rank, QK), BF16),
                  jax.ShapeDtypeStruct((d, rank), BF16),
                  jax.ShapeDtypeStruct((1, rank), F32),
                  jax.ShapeDtypeStruct((1, d), F32)),
                 (dq, xn, h, g, dres, wdq, gq, wuq, cos, sin),
                 scratch=[pltpu.VMEM((N_HEADS, rank, QK), F32), pltpu.VMEM((d, rank), F32)],
                 rider=rider)


def kv_fwd(h, g, wdkv, gkv, wukv, cos, sin, name, rider=()):
    seq, d = h.shape
    tm = min(ROW_TILE, seq)
    wk = KV_RANK + ROPE_PAD

    def body(h_ref, g_ref, wdkv_ref, gkv_ref, wukv_ref, cos_ref, sin_ref, k_ref, v_ref, c_ref):
        xk = _rms(h_ref[...], g_ref[...])[0].astype(BF16)
        ckv = _nn(xk, wdkv_ref[...])
        c_kv = ckv[:, 0:KV_RANK]
        c_ref[...] = c_kv
        kr = _rope_fwd(ckv[:, KV_RANK:wk], cos_ref[...], sin_ref[...]).astype(BF16)
        ckn = _rms(c_kv, gkv_ref[...])[0].astype(BF16)
        for hd in range(N_HEADS):
            kvh = _nn(ckn, wukv_ref[hd])
            k_ref[hd, :, 0:NOPE] = kvh[:, 0:NOPE].astype(BF16)
            k_ref[hd, :, NOPE:QK] = kr
            v_ref[hd] = kvh[:, NOPE:NOPE + VDIM].astype(BF16)

    rope = pl.BlockSpec((tm, ROPE_PAD), lambda i: (i, 0))
    return _call(body, name, (seq // tm,),
                 [pl.BlockSpec((tm, d), lambda i: (i, 0)),
                  pl.BlockSpec((1, d), lambda i: (0, 0)),
                  pl.BlockSpec((d, wk), lambda i: (0, 0)),
                  pl.BlockSpec((1, KV_RANK), lambda i: (0, 0)),
                  pl.BlockSpec((N_HEADS, KV_RANK, NOPE + VDIM), lambda i: (0, 0, 0)), rope, rope],
                 (pl.BlockSpec((N_HEADS, tm, QK), lambda i: (0, i, 0)),
                  pl.BlockSpec((N_HEADS, tm, VDIM), lambda i: (0, i, 0)),
                  pl.BlockSpec((tm, KV_RANK), lambda i: (i, 0))),
                 (jax.ShapeDtypeStruct((N_HEADS, seq, QK), BF16),
                  jax.ShapeDtypeStruct((N_HEADS, seq, VDIM), BF16),
                  jax.ShapeDtypeStruct((seq, KV_RANK), F32)),
                 (h, g, wdkv, gkv, wukv, cos, sin), rider=rider)


def kv_bwd(dks, dvs, c_kv, h, g, dres, wdkv, gkv, wukv, cos, sin, name, rider=()):
    seq, d = h.shape
    tm = min(ROW_TILE, seq)
    n_i = seq // tm
    wk = KV_RANK + ROPE_PAD
    n_b = len(dks)

    def body(*refs):
        dk_refs = refs[:n_b]
        dv_refs = refs[n_b:2 * n_b]
        (c_ref, h_ref, g_ref, dr_ref, wdkv_ref, gkv_ref, wukv_ref, cos_ref, sin_ref,
         dh_ref, dwukv_ref, dwdkv_ref, dgkv_ref, dg_ref, acc_ukv, acc_dkv) = refs[2 * n_b:]
        i = pl.program_id(0)

        @pl.when(i == 0)
        def _():
            acc_ukv[...] = jnp.zeros_like(acc_ukv)
            acc_dkv[...] = jnp.zeros_like(acc_dkv)
            dgkv_ref[...] = jnp.zeros_like(dgkv_ref)
            dg_ref[...] = jnp.zeros_like(dg_ref)
        ckn, chat, crstd = _rms(c_ref[...], gkv_ref[...])
        cknb = ckn.astype(BF16)
        dckn = jnp.zeros((tm, KV_RANK), F32)
        dkr = jnp.zeros((tm, ROPE_PAD), F32)
        for hd in range(N_HEADS):
            dk = dk_refs[0][hd].astype(F32)
            dv = dv_refs[0][hd].astype(F32)
            for j in range(1, n_b):
                dk = dk + dk_refs[j][hd].astype(F32)
                dv = dv + dv_refs[j][hd].astype(F32)
            dkr = dkr + dk[:, NOPE:QK]
            dkvh = jnp.concatenate([dk[:, 0:NOPE].astype(BF16), dv.astype(BF16)], axis=1)
            dckn = dckn + _nt(dkvh, wukv_ref[hd])
            acc_ukv[hd] += _tn(cknb, dkvh)
        dc_kv, dg_rows = _rms_bwd(dckn, chat, crstd, gkv_ref[...])
        dgkv_ref[...] += jnp.sum(dg_rows, axis=0, keepdims=True)
        dkr_raw = _rope_bwd(dkr, cos_ref[...], sin_ref[...])
        dckv = jnp.concatenate([dc_kv.astype(BF16), dkr_raw.astype(BF16)], axis=1)
        xk, xhat, rstd = _rms(h_ref[...], g_ref[...])
        acc_dkv[...] += _tn(xk.astype(BF16), dckv)
        dh, dg_rows = _rms_bwd(_nt(dckv, wdkv_ref[...]), xhat, rstd, g_ref[...])
        dh_ref[...] = dr_ref[...] + dh
        dg_ref[...] += jnp.sum(dg_rows, axis=0, keepdims=True)

        @pl.when(i == n_i - 1)
        def _():
            dwukv_ref[...] = acc_ukv[...].astype(BF16)
            dwdkv_ref[...] = acc_dkv[...].astype(BF16)

    kspec = pl.BlockSpec((N_HEADS, tm, QK), lambda i: (0, i, 0))
    vspec = pl.BlockSpec((N_HEADS, tm, VDIM), lambda i: (0, i, 0))
    rope = pl.BlockSpec((tm, ROPE_PAD), lambda i: (i, 0))
    row = pl.BlockSpec((tm, d), lambda i: (i, 0))
    vec = pl.BlockSpec((1, d), lambda i: (0, 0))
    return _call(body, name, (n_i,),
                 [kspec] * n_b + [vspec] * n_b + [
                     pl.BlockSpec((tm, KV_RANK), lambda i: (i, 0)), row, vec, row,
                     pl.BlockSpec((d, wk), lambda i: (0, 0)),
                     pl.BlockSpec((1, KV_RANK), lambda i: (0, 0)),
                     pl.BlockSpec((N_HEADS, KV_RANK, NOPE + VDIM), lambda i: (0, 0, 0)), rope, rope],
                 (row,
                  pl.BlockSpec((N_HEADS, KV_RANK, NOPE + VDIM), lambda i: (0, 0, 0)),
                  pl.BlockSpec((d, wk), lambda i: (0, 0)),
                  pl.BlockSpec((1, KV_RANK), lambda i: (0, 0)), vec),
                 (jax.ShapeDtypeStruct((seq, d), F32),
                  jax.ShapeDtypeStruct((N_HEADS, KV_RANK, NOPE + VDIM), BF16),
                  jax.ShapeDtypeStruct((d, wk), BF16),
                  jax.ShapeDtypeStruct((1, KV_RANK), F32),
                  jax.ShapeDtypeStruct((1, d), F32)),
                 (*dks, *dvs, c_kv, h, g, dres, wdkv, gkv, wukv, cos, sin),
                 scratch=[pltpu.VMEM((N_HEADS, KV_RANK, NOPE + VDIM), F32), pltpu.VMEM((d, wk), F32)],
                 rider=rider)


def o_bwd(dh, o, wo, name, rider=()):
    seq, d = dh.shape
    hv = o.shape[1]
    tm = min(ROW_TILE, seq)
    n_i = seq // tm

    def body(dh_ref, o_ref, wo_ref, do_ref, dwo_ref, acc):
        i = pl.program_id(0)

        @pl.when(i == 0)
        def _():
            acc[...] = jnp.zeros_like(acc)
        dhb = dh_ref[...].astype(BF16)
        do_ref[...] = _nt(dhb, wo_ref[...]).astype(BF16)
        acc[...] += _tn(o_ref[...], dhb)

        @pl.when(i == n_i - 1)
        def _():
            dwo_ref[...] = acc[...].astype(BF16)

    return _call(body, name, (n_i,),
                 [pl.BlockSpec((tm, d), lambda i: (i, 0)),
                  pl.BlockSpec((tm, hv), lambda i: (i, 0)),
                  pl.BlockSpec((hv, d), lambda i: (0, 0))],
                 (pl.BlockSpec((tm, hv), lambda i: (i, 0)),
                  pl.BlockSpec((hv, d), lambda i: (0, 0))),
                 (jax.ShapeDtypeStruct((seq, hv), BF16), jax.ShapeDtypeStruct((hv, d), BF16)),
                 (dh, o, wo), scratch=[pltpu.VMEM((hv, d), F32)], rider=rider)


def _mask_diagonal(s):
    row = lax.broadcasted_iota(jnp.int32, s.shape, 0)
    col = lax.broadcasted_iota(jnp.int32, s.shape, 1)
    return jnp.where(col <= row, s, NEG_BIG)


def attn_fwd(q, k, v, name, rider=()):
    _, seq, _ = q.shape
    t = min(ATTN_TILE, seq // 2)
    n_pair = seq // (2 * t)

    def body(q_ref, k_ref, v_ref, o_ref, lse_ref):
        qi = pl.program_id(1)
        q_a = q_ref[0:t, :]
        q_b = q_ref[t:2 * t, :]

        def rows(j):
            return pl.ds(pl.multiple_of(j * t, t), t)

        def update(qx, kb, vb, state, diagonal=False):
            m, l, acc = state
            s = _nt(qx, kb)
            if diagonal:
                s = _mask_diagonal(s)
            m_new = jnp.maximum(m, jnp.max(s, axis=1, keepdims=True))
            p = jnp.exp(s - m_new)
            alpha = jnp.exp(m - m_new)
            l = alpha * l + jnp.sum(p, axis=1, keepdims=True)
            acc = alpha * acc + _nn(p.astype(BF16), vb)
            return m_new, l, acc

        def step(j, carry):
            both = pl.ds(pl.multiple_of(j * 2 * t, 2 * t), 2 * t)
            kb, vb = k_ref[both, :], v_ref[both, :]
            return update(q_a, kb, vb, carry[0:3]) + update(q_b, kb, vb, carry[3:6])

        init = (jnp.full((t, 1), NEG_BIG, F32), jnp.zeros((t, 1), F32), jnp.zeros((t, VDIM), F32))
        carry = lax.fori_loop(0, qi, step, init + init)
        k0, v0 = k_ref[rows(2 * qi), :], v_ref[rows(2 * qi), :]
        k1, v1 = k_ref[rows(2 * qi + 1), :], v_ref[rows(2 * qi + 1), :]
        state_a = update(q_a, k0, v0, carry[0:3], diagonal=True)
        state_b = update(q_b, k1, v1, update(q_b, k0, v0, carry[3:6]), diagonal=True)
        for half, (m, l, acc) in enumerate((state_a, state_b)):
            o_ref[half * t:(half + 1) * t, :] = (acc / l).astype(BF16)
            lse_ref[half * t:(half + 1) * t, :] = jnp.broadcast_to(m + jnp.log(l), (t, LANES))

    return _call(body, name, (N_HEADS, n_pair),
                 [pl.BlockSpec((None, 2 * t, QK), lambda h, i: (h, i, 0)),
                  pl.BlockSpec((None, seq, QK), lambda h, i: (h, 0, 0)),
                  pl.BlockSpec((None, seq, VDIM), lambda h, i: (h, 0, 0))],
                 (pl.BlockSpec((2 * t, VDIM), lambda h, i: (i, h)),
                  pl.BlockSpec((None, 2 * t, LANES), lambda h, i: (h, i, 0))),
                 (jax.ShapeDtypeStruct((seq, N_HEADS * VDIM), BF16),
                  jax.ShapeDtypeStruct((N_HEADS, seq, LANES), F32)),
                 (q, k, v), rider=rider)


def attn_bwd(q, k, v, o, do, lse, name, rider=()):
    _, seq, _ = q.shape
    t = min(ATTN_TILE, seq // 2)
    n_q = seq // t
    n_pair = n_q // 2

    def body(q_ref, k_ref, v_ref, o_ref, do_ref, lse_ref, dq_ref, dk_ref, dv_ref,
             dq_acc, dk_acc, dv_acc):
        kj = pl.program_id(1)

        @pl.when(kj == 0)
        def _():
            dq_acc[...] = jnp.zeros_like(dq_acc)
        halves = (slice(0, t), slice(t, 2 * t))

        def block(i, masks, n_rows=t):
            rows = pl.ds(pl.multiple_of(i * n_rows, n_rows), n_rows)
            qb = q_ref[rows, :]
            dob = do_ref[rows, :]
            lse_col = lse_ref[rows, 0:1]
            delta = jnp.sum(dob.astype(F32) * o_ref[rows, :].astype(F32), axis=1, keepdims=True)
            dq, out = None, {}
            for x, diagonal in enumerate(masks):
                if diagonal is None:
                    continue
                kb, vb = k_ref[halves[x], :], v_ref[halves[x], :]
                s = _nt(qb, kb)
                if diagonal:
                    s = _mask_diagonal(s)
                p = jnp.exp(s - lse_col)
                ds = (p * (_nt(dob, vb) - delta)).astype(BF16)
                out[x] = (_tn(p.astype(BF16), dob), _tn(ds, qb))
                part = _nn(ds, kb)
                dq = part if dq is None else dq + part
            dq_acc[rows, :] += dq
            return out

        first = block(2 * kj, (True, None))
        second = block(2 * kj + 1, (False, True))
        dv_acc[halves[0], :] = first[0][0] + second[0][0]
        dk_acc[halves[0], :] = first[0][1] + second[0][1]
        dv_acc[halves[1], :] = second[1][0]
        dk_acc[halves[1], :] = second[1][1]

        def step(i, carry):
            out = block(i, (False, False), n_rows=2 * t)
            for x in (0, 1):
                dv_acc[halves[x], :] += out[x][0]
                dk_acc[halves[x], :] += out[x][1]
            return carry

        lax.fori_loop(kj + 1, n_pair, step, 0)
        dk_ref[...] = dk_acc[...].astype(BF16)
        dv_ref[...] = dv_acc[...].astype(BF16)

        @pl.when(kj == n_pair - 1)
        def _():
            dq_ref[...] = dq_acc[...].astype(BF16)

    head_rows = pl.BlockSpec((seq, VDIM), lambda h, j: (0, h))
    return _call(body, name, (N_HEADS, n_pair),
                 [pl.BlockSpec((None, seq, QK), lambda h, j: (h, 0, 0)),
                  pl.BlockSpec((None, 2 * t, QK), lambda h, j: (h, j, 0)),
                  pl.BlockSpec((None, 2 * t, VDIM), lambda h, j: (h, j, 0)),
                  head_rows, head_rows,
                  pl.BlockSpec((None, seq, LANES), lambda h, j: (h, 0, 0))],
                 (pl.BlockSpec((None, seq, QK), lambda h, j: (h, 0, 0)),
                  pl.BlockSpec((None, 2 * t, QK), lambda h, j: (h, j, 0)),
                  pl.BlockSpec((None, 2 * t, VDIM), lambda h, j: (h, j, 0))),
                 (jax.ShapeDtypeStruct((N_HEADS, seq, QK), BF16),
                  jax.ShapeDtypeStruct((N_HEADS, seq, QK), BF16),
                  jax.ShapeDtypeStruct((N_HEADS, seq, VDIM), BF16)),
                 (q, k, v, o, do, lse),
                 scratch=[pltpu.VMEM((seq, QK), F32), pltpu.VMEM((2 * t, QK), F32),
                          pltpu.VMEM((2 * t, VDIM), F32)], rider=rider)


def loss_head(h, g, target, name):
    seq, d = h.shape
    tm = min(ROW_TILE, seq)

    def body(h_ref, g_ref, t_ref, l_ref, dh_ref, dg_ref):
        i = pl.program_id(0)

        @pl.when(i == 0)
        def _():
            l_ref[...] = jnp.zeros_like(l_ref)
            dg_ref[...] = jnp.zeros_like(dg_ref)
        y, xhat, rstd = _rms(h_ref[...], g_ref[...])
        diff = y - t_ref[...]
        l_ref[...] += jnp.sum(jnp.sum(diff * diff, axis=1, keepdims=True), axis=0, keepdims=True)
        dh, dg_rows = _rms_bwd(diff * (1.0 / d), xhat, rstd, g_ref[...])
        dh_ref[...] = dh
        dg_ref[...] += jnp.sum(dg_rows, axis=0, keepdims=True)

    row = pl.BlockSpec((tm, d), lambda i: (i, 0))
    vec = pl.BlockSpec((1, d), lambda i: (0, 0))
    return _call(body, name, (seq // tm,), [row, vec, row],
                 (pl.BlockSpec((1, LANES), lambda i: (0, 0)), row, vec),
                 (jax.ShapeDtypeStruct((1, LANES), F32), jax.ShapeDtypeStruct((seq, d), F32),
                  jax.ShapeDtypeStruct((1, d), F32)),
                 (h, g, target))[0]


def _pack(parts):
    rows = []
    for p in parts:
        flat = p.reshape(-1)
        n_rows = -(-flat.shape[0] // (8 * LANES)) * 8
        flat = jnp.pad(flat, (0, n_rows * LANES - flat.shape[0]))
        rows.append(flat.reshape(n_rows, LANES))
    return jnp.concatenate(rows, axis=0)


def _unpack(packed, shapes):
    lead = packed.shape[:-2]
    out, r0 = [], 0
    for shape in shapes:
        size = 1
        for s in shape:
            size *= s
        n_rows = -(-size // (8 * LANES)) * 8
        part = packed[..., r0:r0 + n_rows, :].reshape(lead + (n_rows * LANES,))
        out.append(part[..., :size].reshape(lead + tuple(shape)))
        r0 += n_rows
    return out


FWD_RIDERS = {
    "mixer_fwd0": [("ffn_w_up", 0)],
    "ffn_fwd0": [("ffn_w_down", 0), ("a_w_in", 1), ("a_w_out", 1)],
    "ffn_out0": [("ffn_w_down", 1)],
    "mixer_fwd1": [("ffn_w_up", 1)],
    "ffn_fwd1": [("w_dkv", 0), ("w_ukv", 0), ("b_w_dq", 0), ("b_w_uq", 0)],
    "attn_fwd0": [("b_w_o", 0), ("ffn_w_up", 2), ("ffn_w_down", 2), ("b_w_dq", 1), ("b_w_uq", 1)],
    "attn_fwd1": [("b_w_o", 1), ("ffn_w_up", 3), ("ffn_w_down", 3)],
}
BWD_RIDERS = {
    "attn_bwd1": [("ffn_w_down", 3), ("ffn_w_up", 3), ("b_w_o", 1)],
    "ffn_bwd2": [("b_w_uq", 1), ("b_w_dq", 1)],
    "attn_bwd0": [("ffn_w_down", 2), ("ffn_w_up", 2), ("b_w_o", 0)],
    "ffn_bwd1": [("b_w_uq", 0), ("b_w_dq", 0), ("w_ukv", 0), ("w_dkv", 0)],
    "ffn_in_bwd1": [("ffn_w_down", 1), ("ffn_w_up", 1, "pair")],
    "ffn_bwd0": [("ffn_w_up", 1, "chip"), ("a_w_in", 1), ("a_w_out", 1)],
    "ffn_in_bwd0": [("ffn_w_down", 0), ("ffn_w_up", 0, "pair")],
    "mixer_bwd0": [("ffn_w_up", 0, "chip")],
    "mixer_in_bwd0": [("a_w_out", 0), ("a_w_in", 0, "pair")],
    "adamw_a_w_out": [("a_w_in", 0, "chip")],
}


def kernel(x, a_mix_norm, a_w_in, a_conv, a_w_out, b_mix_norm, b_w_dq, b_q_norm, b_w_uq, b_w_o, kv_in_norm, w_dkv, kv_norm, w_ukv, ffn_norm, ffn_w_up, ffn_conv, ffn_w_down, final_norm, loss_target, m_a_mix_norm, m_a_w_in, m_a_conv, m_a_w_out, m_b_mix_norm, m_b_w_dq, m_b_q_norm, m_b_w_uq, m_b_w_o, m_kv_in_norm, m_w_dkv, m_kv_norm, m_w_ukv, m_ffn_norm, m_ffn_w_up, m_ffn_conv, m_ffn_w_down, m_final_norm, v_a_mix_norm, v_a_w_in, v_a_conv, v_a_w_out, v_b_mix_norm, v_b_w_dq, v_b_q_norm, v_b_w_uq, v_b_w_o, v_kv_in_norm, v_w_dkv, v_kv_norm, v_w_ukv, v_ffn_norm, v_ffn_w_up, v_ffn_conv, v_ffn_w_down, v_final_norm):
    seq, d = x.shape[1], x.shape[2]
    me = 4 * lax.axis_index("x") + 2 * lax.axis_index("y") + lax.axis_index("c")
    h0 = x.reshape(seq, d)
    target = loss_target.reshape(seq, d)
    cos, sin = _rope_tables(seq)
    rank = b_w_dq.shape[-1]
    f8 = ffn_w_up.shape[-1]
    fd = ffn_w_down.shape[1]
    dshard = a_w_out.shape[1]
    hv = N_HEADS * VDIM

    shards = {"a_w_in": a_w_in, "a_w_out": a_w_out, "b_w_dq": b_w_dq, "b_w_uq": b_w_uq,
              "b_w_o": b_w_o, "w_dkv": w_dkv[None], "w_ukv": w_ukv[None],
              "ffn_w_up": ffn_w_up, "ffn_w_down": ffn_w_down}

    def relayout(name, g):
        if name == "a_w_in":
            w = jnp.transpose(g, (1, 0, 2)).reshape(d, 3, d)
            return jnp.transpose(w, (1, 0, 2))
        if name == "a_w_out":
            return g.reshape(d, d)
        if name == "b_w_dq":
            return g.reshape(d, rank)
        if name == "b_w_uq":
            return jnp.pad(g, ((0, 0), (0, 0), (0, QK - NOPE - ROPE)))
        if name == "b_w_o":
            return g.reshape(hv, d)
        if name == "w_dkv":
            return jnp.pad(g.reshape(d, KV_RANK + ROPE), ((0, 0), (0, ROPE_PAD - ROPE)))
        if name == "ffn_w_down":
            return g.reshape(N_DEV // 2, 2 * fd, d)
        return g

    weights = {}

    def ag_rider(host):
        return [("ag", shards[n][l].astype(BF16)) for n, l in FWD_RIDERS.get(host, [])]

    def ag_done(host, outs):
        for (n, l), g in zip(FWD_RIDERS.get(host, []), outs):
            weights[n, l] = relayout(n, g)

    small_shapes = [a_mix_norm.shape, a_conv.shape, ffn_conv.shape]
    first = exchange([("ag", a_w_in[0].astype(BF16)), ("ag", a_w_out[0].astype(BF16)),
                      ("ag", _pack([a_mix_norm, a_conv, ffn_conv]))], "ag_first")
    weights["a_w_in", 0] = relayout("a_w_in", first[0])
    weights["a_w_out", 0] = relayout("a_w_out", first[1])
    s_mix, s_aconv, s_fconv = _unpack(first[2], small_shapes)
    a_gain = jnp.transpose(s_mix, (1, 0, 2)).reshape(N_A, d)
    a_cw = jnp.transpose(s_aconv, (1, 2, 0, 3)).reshape(N_A, 3, d)
    f_cw = jnp.transpose(s_fconv, (1, 0, 2, 3))

    def mixer_gain(layer):
        if layer >= DEPTH:
            return None
        return a_gain[layer][None] if layer < N_A else b_mix_norm[layer - N_A][None]

    saved = {}
    h = h0
    xn = norm_fwd(h, mixer_gain(0), "norm_first")
    kv = None
    for layer in range(DEPTH):
        saved["hm", layer], saved["xm", layer] = h, xn
        if layer < N_A:
            name = f"mixer_fwd{layer}"
            (u4, z), r = mixer_fwd(xn, weights["a_w_in", layer], a_cw[layer], name, rider=ag_rider(name))
            ag_done(name, r)
            saved["mix", layer] = (u4, z)
            name = f"mixer_out{layer}"
            (h, xn), r = proj_residual(z[None], weights["a_w_out", layer][None], h, name,
                                       g_next=ffn_norm[layer][None], rider=ag_rider(name))
            ag_done(name, r)
        else:
            j = layer - N_A
            name = f"q_fwd{j}"
            (q,), r = q_fwd(xn, weights["b_w_dq", j], b_q_norm[j][None], weights["b_w_uq", j],
                            cos, sin, name, rider=ag_rider(name))
            ag_done(name, r)
            name = f"attn_fwd{j}"
            (o, lse), r = attn_fwd(q, kv[0], kv[1], name, rider=ag_rider(name))
            ag_done(name, r)
            saved["attn", layer] = (q, o, lse)
            name = f"attn_out{j}"
            (h, xn), r = proj_residual(o[None], weights["b_w_o", j][None], h, name,
                                       g_next=ffn_norm[layer][None], rider=ag_rider(name))
            ag_done(name, r)
        saved["hf", layer], saved["xf", layer] = h, xn
        name = f"ffn_fwd{layer}"
        (up2, cv2, act), r = ffn_fwd(xn, weights["ffn_w_up", layer], f_cw[layer], name, rider=ag_rider(name))
        ag_done(name, r)
        saved["ffn", layer] = (up2, cv2, act)
        name = f"ffn_out{layer}"
        (h, xn), r = proj_residual(act, weights["ffn_w_down", layer], h, name,
                                   g_next=mixer_gain(layer + 1), rider=ag_rider(name))
        ag_done(name, r)
        if layer == N_A - 1:
            (k_all, v_all, c_kv), r = kv_fwd(h, kv_in_norm[None], weights["w_dkv", 0], kv_norm[None],
                                             weights["w_ukv", 0], cos, sin, "kv_fwd",
                                             rider=ag_rider("kv_fwd"))
            ag_done("kv_fwd", r)
            kv = (k_all, v_all, c_kv)

    sq_err, dh, d_final = loss_head(h, final_norm[None], target, "loss_head")
    loss = lax.psum(sq_err[0, 0] * (0.5 / d), ("x", "y", "c"))

    grads = {}
    parts = {}

    pair_sums = {}

    def by_chip(g):
        return g.reshape((N_DEV // 2, 2) + g.shape[1:])

    def rs_rider(host):
        tasks = []
        for key in BWD_RIDERS.get(host, []):
            if len(key) == 2:
                tasks.append(("rs", grads[key]))
            elif key[2] == "pair":
                tasks.append(("rs_pair", by_chip(grads[key[:2]])))
            else:
                tasks.append(("rs_chip", pair_sums[key[:2]]))
        return tasks

    def rs_done(host, outs):
        for key, p in zip(BWD_RIDERS.get(host, []), outs):
            if len(key) == 3 and key[2] == "pair":
                pair_sums[key[:2]] = pair_sum(by_chip(grads[key[:2]]), p, f"pair_sum_{key[0]}{key[1]}")
            else:
                parts[key[:2]] = p

    d_ffn_norm = [None] * DEPTH
    d_fconv = [None] * DEPTH
    d_a_gain = [None] * N_A
    d_aconv = [None] * N_A
    d_b_gain = [None] * N_B
    d_q_gain = [None] * N_B
    dks, dvs = [], []
    for layer in reversed(range(DEPTH)):
        if layer == N_A - 1:
            hk = saved["hm", layer + 1]
            (dh, dwukv, dwdkv, d_kv_gain, d_kvin_gain), r = kv_bwd(
                dks, dvs, kv[2], hk, kv_in_norm[None], dh, weights["w_dkv", 0], kv_norm[None],
                weights["w_ukv", 0], cos, sin, "kv_bwd", rider=rs_rider("kv_bwd"))
            rs_done("kv_bwd", r)
            grads["w_ukv", 0] = dwukv
            grads["w_dkv", 0] = dwdkv[:, :KV_RANK + ROPE].reshape(N_DEV, dshard, KV_RANK + ROPE)
        up2, cv2, act = saved["ffn", layer]
        name = f"ffn_bwd{layer}"
        (dup2, dwup, dwdown, dcw), r = ffn_bwd(dh, weights["ffn_w_down", layer], up2, cv2, act,
                                               saved["xf", layer], f_cw[layer], name, rider=rs_rider(name))
        rs_done(name, r)
        grads["ffn_w_up", layer] = dwup.reshape(N_DEV, d, f8)
        grads["ffn_w_down", layer] = dwdown.reshape(N_DEV, fd, d)
        d_fconv[layer] = dcw.reshape(N_DEV, 3, f8)
        name = f"ffn_in_bwd{layer}"
        (dh, d_ffn_norm[layer]), r = proj_t_rms_bwd(dup2.reshape(N_DEV, seq, f8), weights["ffn_w_up", layer],
                                                    saved["hf", layer], ffn_norm[layer][None], dh, name,
                                                    rider=rs_rider(name))
        rs_done(name, r)
        hm, xm = saved["hm", layer], saved["xm", layer]
        if layer < N_A:
            u4, z = saved["mix", layer]
            name = f"mixer_bwd{layer}"
            (du3, dwin3, dwout, dcw), r = mixer_bwd(dh, weights["a_w_out", layer], u4, z, xm, a_cw[layer],
                                                    name, rider=rs_rider(name))
            rs_done(name, r)
            dwin = jnp.transpose(dwin3, (1, 0, 2)).reshape(d, N_DEV, 3 * d // N_DEV)
            grads["a_w_in", layer] = jnp.transpose(dwin, (1, 0, 2))
            grads["a_w_out", layer] = dwout.reshape(N_DEV, dshard, d)
            d_aconv[layer] = dcw
            name = f"mixer_in_bwd{layer}"
            extra = []
            if layer == 0:
                early_small = [
                    d_a_gain[1],
                    d_aconv[1],
                    jnp.concatenate(d_b_gain, axis=0),
                    jnp.concatenate(d_q_gain, axis=0),
                    d_kvin_gain[0],
                    d_kv_gain[0],
                    jnp.concatenate(d_ffn_norm, axis=0),
                    jnp.stack(d_fconv),
                    d_final[0],
                ]
                extra = [("ag", _pack(early_small))]
            (dh, d_a_gain[layer]), r = proj_t_rms_bwd(du3, weights["a_w_in", layer], hm, a_gain[layer][None],
                                                      dh, name, rider=rs_rider(name) + extra)
            rs_done(name, r)
            if layer == 0:
                g_early = r[-1]
        else:
            j = layer - N_A
            q, o, lse = saved["attn", layer]
            name = f"attn_out_bwd{j}"
            (do, dwo), r = o_bwd(dh, o, weights["b_w_o", j], name, rider=rs_rider(name))
            rs_done(name, r)
            grads["b_w_o", j] = dwo.reshape(N_DEV, dshard, d)
            name = f"attn_bwd{j}"
            (dq, dk, dv), r = attn_bwd(q, kv[0], kv[1], o, do, lse, name, rider=rs_rider(name))
            rs_done(name, r)
            dks.append(dk)
            dvs.append(dv)
            name = f"q_bwd{j}"
            (dh, dwuq, dwdq, d_q_gain[j], d_b_gain[j]), r = q_bwd(
                dq, xm, hm, b_mix_norm[j][None], dh, weights["b_w_dq", j], b_q_norm[j][None],
                weights["b_w_uq", j], cos, sin, name, rider=rs_rider(name))
            rs_done(name, r)
            grads["b_w_uq", j] = dwuq[:, :, :NOPE + ROPE]
            grads["b_w_dq", j] = dwdq.reshape(N_DEV, dshard, rank)
    grad_x = dh.reshape(x.shape)

    late_small = [d_a_gain[0], d_aconv[0]]
    full_shapes = [t.shape for t in early_small + late_small]
    small_pack = _pack(late_small)

    res = {}

    def update(name, n_layers, w, m, v, extra=(), transposed=False):
        view = (lambda t: jnp.transpose(t, (0, 2, 1))) if transposed else (lambda t: t)
        call = sum_adamw_transposed if transposed else sum_adamw
        shard = w.shape if w.ndim == 3 else (1,) + w.shape
        host = f"adamw_{name}"
        outs, r = call([parts[name, l] for l in range(n_layers)], view(w.reshape(shard)),
                       view(m.reshape(shard)), view(v.reshape(shard)), host,
                       rider=rs_rider(host) + list(extra))
        rs_done(host, r)
        res[name] = [view(t).reshape(w.shape) for t in outs]
        return r[len(BWD_RIDERS.get(host, [])):]

    (g_late,) = update("a_w_out", N_A, a_w_out, m_a_w_out, v_a_w_out, extra=[("ag", small_pack)])
    update("ffn_w_down", DEPTH, ffn_w_down, m_ffn_w_down, v_ffn_w_down)
    update("ffn_w_up", DEPTH, ffn_w_up, m_ffn_w_up, v_ffn_w_up, transposed=True)
    update("b_w_dq", N_B, b_w_dq, m_b_w_dq, v_b_w_dq)
    update("b_w_uq", N_B, b_w_uq, m_b_w_uq, v_b_w_uq)
    update("b_w_o", N_B, b_w_o, m_b_w_o, v_b_w_o)
    update("w_dkv", 1, w_dkv, m_w_dkv, v_w_dkv)
    update("w_ukv", 1, w_ukv, m_w_ukv, v_w_ukv)
    update("a_w_in", N_A, a_w_in, m_a_w_in, v_a_w_in)

    summed = sum_slots(jnp.concatenate([g_early, g_late], axis=1), "sum_small_grads")
    (s_a_gain1, s_aconv1, s_b_gain, s_q_gain, s_kvin, s_kvn, s_ffn_gain, s_fconv_g,
     s_final, s_a_gain0, s_aconv0) = _unpack(summed, full_shapes)
    s_a_gain = jnp.concatenate([s_a_gain0, s_a_gain1], axis=0)
    s_aconv_g = jnp.stack([s_aconv0, s_aconv1])
    dsl = d // N_DEV
    small = [
        ("a_mix_norm", lax.dynamic_slice_in_dim(s_a_gain, me * dsl, dsl, axis=1), a_mix_norm, m_a_mix_norm, v_a_mix_norm),
        ("a_conv", lax.dynamic_slice_in_dim(s_aconv_g, me * dsl, dsl, axis=2), a_conv, m_a_conv, v_a_conv),
        ("b_mix_norm", s_b_gain, b_mix_norm, m_b_mix_norm, v_b_mix_norm),
        ("b_q_norm", s_q_gain, b_q_norm, m_b_q_norm, v_b_q_norm),
        ("kv_in_norm", s_kvin, kv_in_norm, m_kv_in_norm, v_kv_in_norm),
        ("kv_norm", s_kvn, kv_norm, m_kv_norm, v_kv_norm),
        ("ffn_norm", s_ffn_gain, ffn_norm, m_ffn_norm, v_ffn_norm),
        ("ffn_conv", lax.dynamic_index_in_dim(s_fconv_g, me, axis=1, keepdims=False), ffn_conv, m_ffn_conv, v_ffn_conv),
        ("final_norm", s_final, final_norm, m_final_norm, v_final_norm),
    ]
    shapes = [t[2].shape for t in small]
    packed = [_pack([t[k] for t in small])[None] for k in (1, 2, 3, 4)]
    outs, _ = sum_adamw([packed[0]], packed[1], packed[2], packed[3], "adamw_small")
    unpacked = [_unpack(t[0], shapes) for t in outs]
    for idx, t in enumerate(small):
        res[t[0]] = [unpacked[k][idx] for k in range(4)]

    order = ["a_mix_norm", "a_w_in", "a_conv", "a_w_out", "b_mix_norm", "b_w_dq", "b_q_norm",
             "b_w_uq", "b_w_o", "kv_in_norm", "w_dkv", "kv_norm", "w_ukv", "ffn_norm",
             "ffn_w_up", "ffn_conv", "ffn_w_down", "final_norm"]
    return (loss, grad_x, *[res[n][0] for n in order], *[res[n][1] for n in order],
            *[res[n][2] for n in order], *[res[n][3] for n in order])
```

```python
import functools

import jax
import jax.numpy as jnp
from jax import lax
from jax.experimental import pallas as pl
from jax.experimental.pallas import tpu as pltpu

F32 = jnp.float32
BF16 = jnp.bfloat16

N_DEV = 8
N_HEADS = 8
NOPE = 128
ROPE = 64
ROPE_PAD = 128
QK = NOPE + ROPE_PAD
VDIM = 128
KV_RANK = 256
ROPE_THETA = 10000.0
RMS_EPS = 1e-6
ATTN_SCALE = (NOPE + ROPE) ** -0.5
N_A = 2
N_B = 2
DEPTH = 4

ADAM_LR = 0.001
ADAM_B1 = 0.9
ADAM_B2 = 0.999
ADAM_EPS = 1e-08
ADAM_WD = 0.01
ADAM_STEP = 10

V7X_VMEM_LIMIT = 56 * 1024 * 1024
BF16_SUBLANES = 16
ROW_TILE = 512
ROW_TILE_LARGE = 1024
ADAM_ROWS = 256
ATTN_TILE = 512
MIXER_CHUNK = 512
LANES = 128
NEG_BIG = -1e30
COPIES_PER_TASK = 7

MESH_ID = pl.DeviceIdType.MESH
ANY = pl.BlockSpec(memory_space=pl.ANY)


def _nt(a, b):
    return lax.dot_general(a, b, (((1,), (1,)), ((), ())), preferred_element_type=F32)


def _tn(a, b):
    return lax.dot_general(a, b, (((0,), (0,)), ((), ())), preferred_element_type=F32)


def _nn(a, b):
    return jnp.dot(a, b, preferred_element_type=F32)


def _rms(h, g):
    rstd = lax.rsqrt(jnp.mean(h * h, axis=-1, keepdims=True) + RMS_EPS)
    xhat = h * rstd
    return xhat * g, xhat, rstd


def _rms_bwd(dxn, xhat, rstd, g):
    dxhat = dxn * g
    dh = rstd * (dxhat - xhat * jnp.mean(dxhat * xhat, axis=-1, keepdims=True))
    return dh, dxn * xhat


def _shift_down(x, k, halo_rows):
    r = pltpu.roll(x, k, 0)
    row = lax.broadcasted_iota(jnp.int32, x.shape, 0)
    for t in range(k):
        r = jnp.where(row == t, halo_rows[t], r)
    return r


def _shift_up(x, k, halo_rows):
    n = x.shape[0]
    r = pltpu.roll(x, n - k, 0)
    row = lax.broadcasted_iota(jnp.int32, x.shape, 0)
    for t in range(k):
        r = jnp.where(row == n - k + t, halo_rows[t], r)
    return r


def _conv_taps(w_ref):
    return w_ref[0:1, :], w_ref[1:2, :], w_ref[2:3, :]


def _rope_swap(x):
    lane = lax.broadcasted_iota(jnp.int32, x.shape, 1)
    return jnp.where(lane < ROPE // 2, pltpu.roll(x, ROPE_PAD - ROPE // 2, 1),
                     pltpu.roll(x, ROPE // 2, 1))


def _rope_fwd(x, cos, sin):
    return x * cos + _rope_swap(x) * sin


def _rope_bwd(dy, cos, sin):
    return dy * cos - _rope_swap(dy) * sin


def _rope_tables(seq):
    inv = 1.0 / (ROPE_THETA ** (jnp.arange(0, ROPE, 2, dtype=F32) / ROPE))
    ang = jnp.arange(seq, dtype=F32)[:, None] * inv[None, :]
    cos, sin = jnp.cos(ang), jnp.sin(ang)
    zero = jnp.zeros((seq, ROPE_PAD - ROPE), F32)
    return (jnp.concatenate([cos, cos, zero], axis=1),
            jnp.concatenate([-sin, sin, zero], axis=1))


def _row_tile(rows, cap, mult=8):
    best = None
    for t in range(mult, min(rows, cap) + 1, mult):
        if rows % t == 0:
            best = t
    return rows if best is None else best


class _AllGatherTask:
    def __init__(self, t, x_ref, out_ref, send_sems, recv_sems, local_sems):
        self.t, self.x_ref, self.out_ref = t, x_ref, out_ref
        self.send_sems, self.recv_sems, self.local_sems = send_sems, recv_sems, local_sems
        mx, my, mc = lax.axis_index("x"), lax.axis_index("y"), lax.axis_index("c")
        self.mc = mc
        self.me, self.sibling = (mx, my, mc), (mx, my, 1 - mc)
        self.chips = [(1 - mx, my), (mx, 1 - my), (1 - mx, 1 - my)]

    def _slot(self, px, py, pc):
        return self.out_ref.at[4 * px + 2 * py + pc]

    def _copy(self, k, block, to, src=None):
        s = COPIES_PER_TASK * self.t + k
        return pltpu.make_async_remote_copy(
            src_ref=self._slot(*block) if src is None else src, dst_ref=self._slot(*block),
            send_sem=self.send_sems.at[s], recv_sem=self.recv_sems.at[s],
            device_id=to, device_id_type=MESH_ID)

    def _mine(self):
        return pltpu.make_async_copy(self.x_ref, self._slot(*self.me), self.local_sems.at[self.t])

    def _first(self):
        out = [self._copy(0, self.me, self.sibling, src=self.x_ref)]
        out += [self._copy(1 + j, self.me, (*chip, self.mc), src=self.x_ref)
                for j, chip in enumerate(self.chips)]
        return out

    def _passed(self):
        return [self._copy(4 + j, (*chip, self.mc), self.sibling) for j, chip in enumerate(self.chips)]

    def start(self):
        self._mine().start()
        for cp in self._first():
            cp.start()

    def forward(self):
        passed = self._passed()
        for j, chip in enumerate(self.chips):
            self._copy(1 + j, (*chip, self.mc), self.me).wait_recv()
            passed[j].start()

    def finish(self):
        self._copy(0, self.sibling, self.me).wait_recv()
        for j, chip in enumerate(self.chips):
            self._copy(4 + j, (*chip, 1 - self.mc), self.me).wait_recv()
        for cp in self._first() + self._passed():
            cp.wait_send()
        self._mine().wait()


class _ReduceScatterTask:
    def __init__(self, t, g_ref, out_ref, send_sems, recv_sems, local_sems):
        self.t, self.g_ref, self.out_ref = t, g_ref, out_ref
        self.send_sems, self.recv_sems, self.local_sems = send_sems, recv_sems, local_sems
        mx, my, mc = lax.axis_index("x"), lax.axis_index("y"), lax.axis_index("c")
        self.me = 4 * mx + 2 * my + mc
        self.peers = []
        for k in range(1, N_DEV):
            px, py, pc = mx ^ ((k >> 2) & 1), my ^ ((k >> 1) & 1), mc ^ (k & 1)
            self.peers.append(((px, py, pc), 4 * px + 2 * py + pc))

    def _mine(self):
        return pltpu.make_async_copy(self.g_ref.at[self.me], self.out_ref.at[self.me],
                                     self.local_sems.at[self.t])

    def _copy(self, k, src_slot, dst_slot):
        s = COPIES_PER_TASK * self.t + k
        return pltpu.make_async_remote_copy(
            src_ref=self.g_ref.at[src_slot], dst_ref=self.out_ref.at[dst_slot],
            send_sem=self.send_sems.at[s], recv_sem=self.recv_sems.at[s],
            device_id=self.peers[k][0], device_id_type=MESH_ID)

    def start(self):
        self._mine().start()
        for k, (_, peer) in enumerate(self.peers):
            self._copy(k, peer, self.me).start()

    def forward(self):
        pass

    def finish(self):
        for k, (_, peer) in enumerate(self.peers):
            self._copy(k, self.me, peer).wait_recv()
        for k, (_, peer) in enumerate(self.peers):
            self._copy(k, peer, self.me).wait_send()
        self._mine().wait()


class _PairExchangeTask:
    def __init__(self, t, g_ref, out_ref, send_sems, recv_sems, local_sems):
        mx, my, mc = lax.axis_index("x"), lax.axis_index("y"), lax.axis_index("c")
        s = COPIES_PER_TASK * t
        self.copy = pltpu.make_async_remote_copy(
            src_ref=g_ref.at[:, 1 - mc], dst_ref=out_ref,
            send_sem=send_sems.at[s], recv_sem=recv_sems.at[s],
            device_id=(mx, my, 1 - mc), device_id_type=MESH_ID)

    def start(self):
        self.copy.start()

    def forward(self):
        pass

    def finish(self):
        self.copy.wait()


class _ChipScatterTask:
    def __init__(self, t, s_ref, out_ref, send_sems, recv_sems, local_sems):
        self.t, self.s_ref, self.out_ref = t, s_ref, out_ref
        self.send_sems, self.recv_sems, self.local_sems = send_sems, recv_sems, local_sems
        mx, my, mc = lax.axis_index("x"), lax.axis_index("y"), lax.axis_index("c")
        self.chip = 2 * mx + my
        self.peers = []
        for k in range(1, N_DEV // 2):
            px, py = mx ^ ((k >> 1) & 1), my ^ (k & 1)
            self.peers.append(((px, py, mc), 2 * px + py))

    def _mine(self):
        return pltpu.make_async_copy(self.s_ref.at[self.chip], self.out_ref.at[self.chip],
                                     self.local_sems.at[self.t])

    def _copy(self, k, src_slot, dst_slot):
        s = COPIES_PER_TASK * self.t + k
        return pltpu.make_async_remote_copy(
            src_ref=self.s_ref.at[src_slot], dst_ref=self.out_ref.at[dst_slot],
            send_sem=self.send_sems.at[s], recv_sem=self.recv_sems.at[s],
            device_id=self.peers[k][0], device_id_type=MESH_ID)

    def start(self):
        self._mine().start()
        for k, (_, peer) in enumerate(self.peers):
            self._copy(k, peer, self.chip).start()

    def forward(self):
        pass

    def finish(self):
        for k, (_, peer) in enumerate(self.peers):
            self._copy(k, self.chip, peer).wait_recv()
        for k, (_, peer) in enumerate(self.peers):
            self._copy(k, peer, self.chip).wait_send()
        self._mine().wait()


_TASKS = {"ag": _AllGatherTask, "rs": _ReduceScatterTask, "rs_pair": _PairExchangeTask,
          "rs_chip": _ChipScatterTask}


def _task_shape(kind, arr):
    shape = {"ag": (N_DEV,) + arr.shape, "rs": arr.shape, "rs_chip": arr.shape,
             "rs_pair": arr.shape[:1] + arr.shape[2:]}[kind]
    return jax.ShapeDtypeStruct(shape, arr.dtype)


def _sem_shapes(n_tasks):
    return [pltpu.SemaphoreType.DMA((COPIES_PER_TASK * n_tasks,)),
            pltpu.SemaphoreType.DMA((COPIES_PER_TASK * n_tasks,)),
            pltpu.SemaphoreType.DMA((n_tasks,))]


def _make_tasks(rider, in_refs, out_refs, sems):
    return [_TASKS[kind](t, in_refs[t], out_refs[t], *sems) for t, (kind, _) in enumerate(rider)]


def exchange(rider, name):
    n = len(rider)

    def body(*refs):
        tasks = _make_tasks(rider, refs[:n], refs[n:2 * n], refs[2 * n:])
        for task in tasks:
            task.start()
        for task in tasks:
            task.forward()
        for task in tasks:
            task.finish()

    return list(pl.pallas_call(
        body, name=name, out_shape=tuple(_task_shape(k, a) for k, a in rider),
        in_specs=[ANY] * n, out_specs=(ANY,) * n, scratch_shapes=_sem_shapes(n),
    )(*[a for _, a in rider]))


def _call(body, name, grid, in_specs, out_specs, out_shape, args, scratch=(), rider=()):
    in_specs, out_specs, out_shape = list(in_specs), tuple(out_specs), tuple(out_shape)
    n_in, n_out, n_scr, n_r = len(in_specs), len(out_specs), len(scratch), len(rider)
    if n_r:
        def kern(*refs):
            ins, r_in = refs[:n_in], refs[n_in:n_in + n_r]
            o0 = n_in + n_r
            outs, r_out = refs[o0:o0 + n_out], refs[o0 + n_out:o0 + n_out + n_r]
            s0 = o0 + n_out + n_r
            scr, sems = refs[s0:s0 + n_scr], refs[s0 + n_scr:]
            step = 0
            for a, n in enumerate(grid):
                step = step * n + pl.program_id(a)
            n_steps = 1
            for n in grid:
                n_steps *= n

            @pl.when(step == 0)
            def _():
                for task in _make_tasks(rider, r_in, r_out, sems):
                    task.start()
            body(*ins, *outs, *scr)

            @pl.when(step == n_steps - 1)
            def _():
                tasks = _make_tasks(rider, r_in, r_out, sems)
                for task in tasks:
                    task.forward()
                for task in tasks:
                    task.finish()
    else:
        kern = body
    res = pl.pallas_call(
        kern, name=name, grid=grid,
        in_specs=in_specs + [ANY] * n_r, out_specs=out_specs + (ANY,) * n_r,
        out_shape=out_shape + tuple(_task_shape(k, a) for k, a in rider),
        scratch_shapes=list(scratch) + (_sem_shapes(n_r) if n_r else []),
        compiler_params=pltpu.CompilerParams(dimension_semantics=("arbitrary",) * len(grid),
                                             vmem_limit_bytes=V7X_VMEM_LIMIT),
    )(*args, *[a for _, a in rider])
    return list(res[:n_out]), list(res[n_out:])


def _adamw(g, w, m, v):
    m = ADAM_B1 * m + (1.0 - ADAM_B1) * g
    v = ADAM_B2 * v + (1.0 - ADAM_B2) * (g * g)
    m_hat = m / (1.0 - ADAM_B1 ** ADAM_STEP)
    v_hat = v / (1.0 - ADAM_B2 ** ADAM_STEP)
    delta = -ADAM_LR * (m_hat / (jnp.sqrt(v_hat) + ADAM_EPS) + ADAM_WD * w)
    return delta, m, v


def sum_adamw(parts, w, m, v, name, rider=()):
    n_l, rows, cols = w.shape
    mult = BF16_SUBLANES if parts[0].dtype == BF16 else 8
    tr = _row_tile(rows, ADAM_ROWS, mult)
    n_i = rows // tr

    def body(*refs):
        part_refs = refs[:n_l]
        w_ref, m_ref, v_ref, g_out, d_out, m_out, v_out = refs[n_l:]
        layer = pl.program_id(0)
        for k in range(n_l):
            @pl.when(layer == k)
            def _(k=k):
                g = part_refs[k][0].astype(F32)
                for s in range(1, parts[k].shape[0]):
                    g = g + part_refs[k][s].astype(F32)
                delta, m_new, v_new = _adamw(g, w_ref[...], m_ref[...], v_ref[...])
                g_out[...] = g
                d_out[...] = delta
                m_out[...] = m_new
                v_out[...] = v_new

    part_specs = [pl.BlockSpec((parts[k].shape[0], tr, cols), functools.partial(
        lambda l, i, k: (0, jnp.where(l == k, i, 0), 0), k=k)) for k in range(n_l)]
    wspec = pl.BlockSpec((None, tr, cols), lambda l, i: (l, i, 0))
    shape = jax.ShapeDtypeStruct(w.shape, F32)
    return _call(body, name, (n_l, n_i), part_specs + [wspec] * 3, (wspec,) * 4, (shape,) * 4,
                 (*parts, w, m, v), rider=rider)


def sum_adamw_transposed(parts, w_t, m_t, v_t, name, rider=()):
    n_l, cols, rows = w_t.shape
    tr = LANES
    n_i = rows // tr
    starts = list(range(0, cols - LANES + 1, LANES))
    if starts[-1] + LANES < cols:
        starts.append(cols - LANES)

    def body(*refs):
        part_refs = refs[:n_l]
        w_ref, m_ref, v_ref, g_out, d_out, m_out, v_out = refs[n_l:]
        layer = pl.program_id(0)
        for k in range(n_l):
            @pl.when(layer == k)
            def _(k=k):
                for c0 in starts:
                    piece = pl.ds(c0, LANES)
                    g = part_refs[k][0, :, piece].astype(F32)
                    for s in range(1, parts[k].shape[0]):
                        g = g + part_refs[k][s, :, piece].astype(F32)
                    g = g.T
                    delta, m_new, v_new = _adamw(g, w_ref[piece, :], m_ref[piece, :], v_ref[piece, :])
                    g_out[piece, :] = g
                    d_out[piece, :] = delta
                    m_out[piece, :] = m_new
                    v_out[piece, :] = v_new

    part_specs = [pl.BlockSpec((parts[k].shape[0], tr, cols), functools.partial(
        lambda l, i, k: (0, jnp.where(l == k, i, 0), 0), k=k)) for k in range(n_l)]
    wspec = pl.BlockSpec((None, cols, tr), lambda l, i: (l, 0, i))
    shape = jax.ShapeDtypeStruct(w_t.shape, F32)
    return _call(body, name, (n_l, n_i), part_specs + [wspec] * 3, (wspec,) * 4, (shape,) * 4,
                 (*parts, w_t, m_t, v_t), rider=rider)


def pair_sum(g4, other, name):
    n_chip, _, rows, cols = g4.shape
    tr = _row_tile(rows, 512, BF16_SUBLANES)

    def body(core_ref, g_ref, o_ref, s_ref):
        s_ref[...] = (g_ref[...].astype(F32) + o_ref[...].astype(F32)).astype(BF16)

    blk = pl.BlockSpec((None, tr, cols), lambda k, i, core: (k, i, 0))
    return pl.pallas_call(
        body, name=name, out_shape=jax.ShapeDtypeStruct((n_chip, rows, cols), g4.dtype),
        grid_spec=pltpu.PrefetchScalarGridSpec(
            num_scalar_prefetch=1, grid=(n_chip, rows // tr),
            in_specs=[pl.BlockSpec((None, None, tr, cols), lambda k, i, core: (k, core[0], i, 0)), blk],
            out_specs=blk),
        compiler_params=pltpu.CompilerParams(dimension_semantics=("arbitrary", "arbitrary"),
                                             vmem_limit_bytes=V7X_VMEM_LIMIT),
    )(lax.axis_index("c").astype(jnp.int32).reshape(1), g4, other)


def sum_slots(parts, name):
    n, rows, cols = parts.shape

    def body(p_ref, o_ref):
        acc = p_ref[0]
        for s in range(1, n):
            acc = acc + p_ref[s]
        o_ref[...] = acc

    return pl.pallas_call(
        body, name=name, out_shape=jax.ShapeDtypeStruct((rows, cols), F32),
        in_specs=[pl.BlockSpec(memory_space=pltpu.VMEM)],
        out_specs=pl.BlockSpec(memory_space=pltpu.VMEM),
    )(parts)


def norm_fwd(h, g, name):
    seq, d = h.shape
    tm = min(ROW_TILE, seq)

    def body(h_ref, g_ref, o_ref):
        o_ref[...] = _rms(h_ref[...], g_ref[...])[0].astype(BF16)

    return _call(body, name, (seq // tm,),
                 [pl.BlockSpec((tm, d), lambda i: (i, 0)), pl.BlockSpec((1, d), lambda i: (0, 0))],
                 [pl.BlockSpec((tm, d), lambda i: (i, 0))],
                 [jax.ShapeDtypeStruct((seq, d), BF16)], (h, g))[0][0]


def proj_residual(a, w, res, name, g_next=None, rider=()):
    nb, seq, kb = a.shape
    d = w.shape[-1]
    tm = min(ROW_TILE, seq)
    with_norm = g_next is not None

    def body(a_ref, w_ref, r_ref, *rest):
        acc = r_ref[...]
        for b in range(nb):
            acc = acc + _nn(a_ref[b], w_ref[b])
        if with_norm:
            g_ref, o_ref, xn_ref = rest
            xn_ref[...] = _rms(acc, g_ref[...])[0].astype(BF16)
        else:
            (o_ref,) = rest
        o_ref[...] = acc

    row = pl.BlockSpec((tm, d), lambda i: (i, 0))
    in_specs = [pl.BlockSpec((nb, tm, kb), lambda i: (0, i, 0)),
                pl.BlockSpec((nb, kb, d), lambda i: (0, 0, 0)), row]
    args = [a, w, res]
    out_specs, out_shape = [row], [jax.ShapeDtypeStruct((seq, d), F32)]
    if with_norm:
        in_specs.append(pl.BlockSpec((1, d), lambda i: (0, 0)))
        args.append(g_next)
        out_specs.append(row)
        out_shape.append(jax.ShapeDtypeStruct((seq, d), BF16))
    outs, r_outs = _call(body, name, (seq // tm,), in_specs, out_specs, out_shape, args, rider=rider)
    return (outs[0], outs[1] if with_norm else None), r_outs


def proj_t_rms_bwd(du, w, h, g, dres, name, rider=()):
    nb, seq, wd = du.shape
    k = w.shape[1]
    big_weight = 2 * w.size * w.dtype.itemsize > V7X_VMEM_LIMIT // 4
    tm = min(ROW_TILE // 2 if big_weight else ROW_TILE, seq)

    def body(du_ref, w_ref, h_ref, g_ref, dr_ref, dh_ref, dg_ref):
        i = pl.program_id(0)
        dxn = _nt(du_ref[0], w_ref[0])
        for b in range(1, nb):
            dxn = dxn + _nt(du_ref[b], w_ref[b])
        _, xhat, rstd = _rms(h_ref[...], g_ref[...])
        dh, dg_rows = _rms_bwd(dxn, xhat, rstd, g_ref[...])
        dh_ref[...] = dr_ref[...] + dh

        @pl.when(i == 0)
        def _():
            dg_ref[...] = jnp.zeros_like(dg_ref)
        dg_ref[...] += jnp.sum(dg_rows, axis=0, keepdims=True)

    row = pl.BlockSpec((tm, k), lambda i: (i, 0))
    vec = pl.BlockSpec((1, k), lambda i: (0, 0))
    return _call(body, name, (seq // tm,),
                 [pl.BlockSpec((nb, tm, wd), lambda i: (0, i, 0)),
                  pl.BlockSpec((nb, k, wd), lambda i: (0, 0, 0)), row, vec, row],
                 (row, vec),
                 (jax.ShapeDtypeStruct((seq, k), F32), jax.ShapeDtypeStruct((1, k), F32)),
                 (du, w, h, g, dres), rider=rider)


def mixer_fwd(xn, win3, cw, name, rider=()):
    seq, d = xn.shape
    tm = min(ROW_TILE_LARGE, seq)
    cc = min(MIXER_CHUNK, d)
    n_c, n_i = d // cc, seq // tm

    def body(x_ref, w_ref, cw_ref, u_ref, z_ref, carry):
        i = pl.program_id(1)

        @pl.when(i == 0)
        def _():
            carry[...] = jnp.zeros_like(carry)
        xb = x_ref[...]
        b = _nn(xb, w_ref[0])
        c = _nn(xb, w_ref[1])
        hh = _nn(xb, w_ref[2])
        p = c * hh
        w0, w1, w2 = _conv_taps(cw_ref)
        p1 = _shift_down(p, 1, [carry[7:8, :]])
        p2 = _shift_down(p, 2, [carry[6:7, :], carry[7:8, :]])
        q = w0 * p2 + w1 * p1 + w2 * p
        carry[...] = p[tm - 8:tm, :]
        u_ref[0] = b.astype(BF16)
        u_ref[1] = c.astype(BF16)
        u_ref[2] = hh.astype(BF16)
        u_ref[3] = q.astype(BF16)
        z_ref[...] = (b * q).astype(BF16)

    return _call(body, name, (n_c, n_i),
                 [pl.BlockSpec((tm, d), lambda c, i: (i, 0)),
                  pl.BlockSpec((3, d, cc), lambda c, i: (0, 0, c)),
                  pl.BlockSpec((3, cc), lambda c, i: (0, c))],
                 (pl.BlockSpec((4, tm, cc), lambda c, i: (0, i, c)),
                  pl.BlockSpec((tm, cc), lambda c, i: (i, c))),
                 (jax.ShapeDtypeStruct((4, seq, d), BF16), jax.ShapeDtypeStruct((seq, d), BF16)),
                 (xn, win3, cw), scratch=[pltpu.VMEM((8, cc), F32)], rider=rider)


def mixer_bwd(dh, wout, u4, z, xn, cw, name, rider=()):
    seq, d = xn.shape
    tm = min(ROW_TILE, seq)
    cc = min(MIXER_CHUNK, d)
    n_c, n_i = d // cc, seq // tm

    def body(dh_ref, wout_ref, u_ref, z_ref, x_ref, cw_ref,
             du_ref, dwin_ref, dwout_ref, dcw_ref, acc_in, acc_out, acc_cw, carry):
        i = pl.program_id(1)

        @pl.when(i == 0)
        def _():
            acc_in[...] = jnp.zeros_like(acc_in)
            acc_out[...] = jnp.zeros_like(acc_out)
            acc_cw[...] = jnp.zeros_like(acc_cw)
            carry[...] = jnp.zeros_like(carry)
        dhb = dh_ref[...].astype(BF16)
        dz = _nt(dhb, wout_ref[...])
        acc_out[...] += _tn(z_ref[...], dhb)
        b = u_ref[0].astype(F32)
        c = u_ref[1].astype(F32)
        hh = u_ref[2].astype(F32)
        q = u_ref[3].astype(F32)
        p = c * hh
        db = dz * q
        dq = dz * b
        w0, w1, w2 = _conv_taps(cw_ref)
        dq1 = _shift_up(dq, 1, [carry[0:1, :]])
        dq2 = _shift_up(dq, 2, [carry[0:1, :], carry[1:2, :]])
        dp = w2 * dq + w1 * dq1 + w0 * dq2
        carry[...] = dq[0:8, :]
        acc_cw[0:1, :] += jnp.sum(dq2 * p, axis=0, keepdims=True)
        acc_cw[1:2, :] += jnp.sum(dq1 * p, axis=0, keepdims=True)
        acc_cw[2:3, :] += jnp.sum(dq * p, axis=0, keepdims=True)
        dbb = db.astype(BF16)
        dcb = (dp * hh).astype(BF16)
        dhhb = (dp * c).astype(BF16)
        du_ref[0] = dbb
        du_ref[1] = dcb
        du_ref[2] = dhhb
        xb = x_ref[...]
        acc_in[0] += _tn(xb, dbb)
        acc_in[1] += _tn(xb, dcb)
        acc_in[2] += _tn(xb, dhhb)

        @pl.when(i == n_i - 1)
        def _():
            dwin_ref[...] = acc_in[...].astype(BF16)
            dwout_ref[...] = acc_out[...].astype(BF16)
            dcw_ref[...] = acc_cw[0:3, :]

    rev = lambda c, i: (n_i - 1 - i, 0)
    return _call(body, name, (n_c, n_i),
                 [pl.BlockSpec((tm, d), rev),
                  pl.BlockSpec((cc, d), lambda c, i: (c, 0)),
                  pl.BlockSpec((4, tm, cc), lambda c, i: (0, n_i - 1 - i, c)),
                  pl.BlockSpec((tm, cc), lambda c, i: (n_i - 1 - i, c)),
                  pl.BlockSpec((tm, d), rev),
                  pl.BlockSpec((3, cc), lambda c, i: (0, c))],
                 (pl.BlockSpec((3, tm, cc), lambda c, i: (0, n_i - 1 - i, c)),
                  pl.BlockSpec((3, d, cc), lambda c, i: (0, 0, c)),
                  pl.BlockSpec((cc, d), lambda c, i: (c, 0)),
                  pl.BlockSpec((3, cc), lambda c, i: (0, c))),
                 (jax.ShapeDtypeStruct((3, seq, d), BF16), jax.ShapeDtypeStruct((3, d, d), BF16),
                  jax.ShapeDtypeStruct((d, d), BF16), jax.ShapeDtypeStruct((3, d), F32)),
                 (dh, wout, u4, z, xn, cw),
                 scratch=[pltpu.VMEM((3, d, cc), F32), pltpu.VMEM((cc, d), F32),
                          pltpu.VMEM((8, cc), F32), pltpu.VMEM((8, cc), F32)], rider=rider)


def _silu_parts(cg):
    sg = 1.0 / (1.0 + jnp.exp(-cg))
    return sg, cg * sg


def ffn_fwd(xn, wup, fcw, name, rider=()):
    seq, d = xn.shape
    f8 = wup.shape[-1]
    half = N_DEV // 2
    tm = min(ROW_TILE_LARGE, seq)
    n_i = seq // tm

    def body(x_ref, wg_ref, wu_ref, cg_ref, cu_ref, up_ref, cv_ref, a_ref, carry):
        i = pl.program_id(1)

        @pl.when(i == 0)
        def _():
            carry[...] = jnp.zeros_like(carry)
        xb = x_ref[...]
        conv = []
        for s, (w_ref, t_ref) in enumerate(((wg_ref, cg_ref), (wu_ref, cu_ref))):
            u = _nn(xb, w_ref[...])
            up_ref[s] = u.astype(BF16)
            w0, w1, w2 = _conv_taps(t_ref)
            u1 = _shift_down(u, 1, [carry[s, 7:8, :]])
            u2 = _shift_down(u, 2, [carry[s, 6:7, :], carry[s, 7:8, :]])
            cv = w0 * u2 + w1 * u1 + w2 * u
            cv_ref[s] = cv.astype(BF16)
            conv.append(cv)
            carry[s] = u[tm - 8:tm, :]
        _, silu = _silu_parts(conv[0])
        a_ref[...] = (silu * conv[1]).astype(BF16)

    blk = pl.BlockSpec((2, None, tm, f8), lambda c, i: (0, c, i, 0))
    big = jax.ShapeDtypeStruct((2, half, seq, f8), BF16)
    return _call(body, name, (half, n_i),
                 [pl.BlockSpec((tm, d), lambda c, i: (i, 0)),
                  pl.BlockSpec((None, d, f8), lambda c, i: (c, 0, 0)),
                  pl.BlockSpec((None, d, f8), lambda c, i: (c + half, 0, 0)),
                  pl.BlockSpec((None, 3, f8), lambda c, i: (c, 0, 0)),
                  pl.BlockSpec((None, 3, f8), lambda c, i: (c + half, 0, 0))],
                 (blk, blk, pl.BlockSpec((None, tm, f8), lambda c, i: (c, i, 0))),
                 (big, big, jax.ShapeDtypeStruct((half, seq, f8), BF16)),
                 (xn, wup, wup, fcw, fcw), scratch=[pltpu.VMEM((2, 8, f8), F32)], rider=rider)


def ffn_bwd(dh, wdown, up2, cv2, act, xn, fcw, name, rider=()):
    seq, d = xn.shape
    f8 = up2.shape[-1]
    fb = wdown.shape[1]
    half = N_DEV // 2
    tm = min(ROW_TILE, seq)
    n_i = seq // tm

    def body(dh_ref, wd_ref, up_ref, cv_ref, a_ref, x_ref, cg_ref, cu_ref,
             dup_ref, dwup_ref, dwd_ref, dcw_ref, acc_up, acc_down, acc_cw, carry):
        i = pl.program_id(1)

        @pl.when(i == 0)
        def _():
            acc_up[...] = jnp.zeros_like(acc_up)
            acc_down[...] = jnp.zeros_like(acc_down)
            acc_cw[...] = jnp.zeros_like(acc_cw)
            carry[...] = jnp.zeros_like(carry)
        dhb = dh_ref[...].astype(BF16)
        da = _nt(dhb, wd_ref[...])
        acc_down[...] += _tn(a_ref[...], dhb)
        cg = cv_ref[0].astype(F32)
        cu = cv_ref[1].astype(F32)
        sg, silu = _silu_parts(cg)
        dcg = da * cu * (sg + silu * (1.0 - sg))
        dcu = da * silu
        xb = x_ref[...]
        for s, (dc, t_ref) in enumerate(((dcg, cg_ref), (dcu, cu_ref))):
            w0, w1, w2 = _conv_taps(t_ref)
            d1 = _shift_up(dc, 1, [carry[s, 0:1, :]])
            d2 = _shift_up(dc, 2, [carry[s, 0:1, :], carry[s, 1:2, :]])
            du = (w2 * dc + w1 * d1 + w0 * d2).astype(BF16)
            carry[s] = dc[0:8, :]
            u = up_ref[s].astype(F32)
            acc_cw[s, 0:1, :] += jnp.sum(d2 * u, axis=0, keepdims=True)
            acc_cw[s, 1:2, :] += jnp.sum(d1 * u, axis=0, keepdims=True)
            acc_cw[s, 2:3, :] += jnp.sum(dc * u, axis=0, keepdims=True)
            dup_ref[s] = du
            acc_up[s] += _tn(xb, du)

        @pl.when(i == n_i - 1)
        def _():
            dwup_ref[...] = acc_up[...].astype(BF16)
            dwd_ref[...] = acc_down[...].astype(BF16)
            dcw_ref[...] = acc_cw[:, 0:3, :]

    rev = lambda c, i: (n_i - 1 - i, 0)
    blk = pl.BlockSpec((2, None, tm, f8), lambda c, i: (0, c, n_i - 1 - i, 0))
    return _call(body, name, (half, n_i),
                 [pl.BlockSpec((tm, d), rev),
                  pl.BlockSpec((None, fb, d), lambda c, i: (c, 0, 0)),
                  blk, blk,
                  pl.BlockSpec((None, tm, f8), lambda c, i: (c, n_i - 1 - i, 0)),
                  pl.BlockSpec((tm, d), rev),
                  pl.BlockSpec((None, 3, f8), lambda c, i: (c, 0, 0)),
                  pl.BlockSpec((None, 3, f8), lambda c, i: (c + half, 0, 0))],
                 (blk,
                  pl.BlockSpec((2, None, d, f8), lambda c, i: (0, c, 0, 0)),
                  pl.BlockSpec((None, fb, d), lambda c, i: (c, 0, 0)),
                  pl.BlockSpec((2, None, 3, f8), lambda c, i: (0, c, 0, 0))),
                 (jax.ShapeDtypeStruct((2, half, seq, f8), BF16),
                  jax.ShapeDtypeStruct((2, half, d, f8), BF16),
                  jax.ShapeDtypeStruct((half, fb, d), BF16),
                  jax.ShapeDtypeStruct((2, half, 3, f8), F32)),
                 (dh, wdown, up2, cv2, act, xn, fcw, fcw),
                 scratch=[pltpu.VMEM((2, d, f8), F32), pltpu.VMEM((fb, d), F32),
                          pltpu.VMEM((2, 8, f8), F32), pltpu.VMEM((2, 8, f8), F32)], rider=rider)


def q_fwd(xn, wdq, gq, wuq, cos, sin, name, rider=()):
    seq, d = xn.shape
    rank = wdq.shape[-1]
    tm = min(ROW_TILE, seq)

    def body(x_ref, wdq_ref, gq_ref, wuq_ref, cos_ref, sin_ref, q_ref):
        qc = _nn(x_ref[...], wdq_ref[...])
        qn = _rms(qc, gq_ref[...])[0].astype(BF16)
        for hd in range(N_HEADS):
            qh = _nn(qn, wuq_ref[hd])
            qr = _rope_fwd(qh[:, NOPE:QK], cos_ref[...], sin_ref[...])
            q_ref[hd, :, 0:NOPE] = (qh[:, 0:NOPE] * ATTN_SCALE).astype(BF16)
            q_ref[hd, :, NOPE:QK] = (qr * ATTN_SCALE).astype(BF16)

    rope = pl.BlockSpec((tm, ROPE_PAD), lambda i: (i, 0))
    return _call(body, name, (seq // tm,),
                 [pl.BlockSpec((tm, d), lambda i: (i, 0)),
                  pl.BlockSpec((d, rank), lambda i: (0, 0)),
                  pl.BlockSpec((1, rank), lambda i: (0, 0)),
                  pl.BlockSpec((N_HEADS, rank, QK), lambda i: (0, 0, 0)), rope, rope],
                 [pl.BlockSpec((N_HEADS, tm, QK), lambda i: (0, i, 0))],
                 [jax.ShapeDtypeStruct((N_HEADS, seq, QK), BF16)],
                 (xn, wdq, gq, wuq, cos, sin), rider=rider)


def q_bwd(dq, xn, h, g, dres, wdq, gq, wuq, cos, sin, name, rider=()):
    seq, d = xn.shape
    rank = wdq.shape[-1]
    tm = min(ROW_TILE, seq)
    n_i = seq // tm

    def body(dq_ref, x_ref, h_ref, g_ref, dr_ref, wdq_ref, gq_ref, wuq_ref, cos_ref, sin_ref,
             dh_ref, dwuq_ref, dwdq_ref, dgq_ref, dg_ref, acc_uq, acc_dq):
        i = pl.program_id(0)

        @pl.when(i == 0)
        def _():
            acc_uq[...] = jnp.zeros_like(acc_uq)
            acc_dq[...] = jnp.zeros_like(acc_dq)
            dgq_ref[...] = jnp.zeros_like(dgq_ref)
            dg_ref[...] = jnp.zeros_like(dg_ref)
        xb = x_ref[...]
        qc = _nn(xb, wdq_ref[...])
        qn, qhat, qrstd = _rms(qc, gq_ref[...])
        qnb = qn.astype(BF16)
        dqn = jnp.zeros((tm, rank), F32)
        for hd in range(N_HEADS):
            dnope = (dq_ref[hd, :, 0:NOPE].astype(F32) * ATTN_SCALE).astype(BF16)
            drope = _rope_bwd(dq_ref[hd, :, NOPE:QK].astype(F32) * ATTN_SCALE, cos_ref[...], sin_ref[...])
            draw = jnp.concatenate([dnope, drope.astype(BF16)], axis=1)
            dqn = dqn + _nt(draw, wuq_ref[hd])
            acc_uq[hd] += _tn(qnb, draw)
        dqc, dg_rows = _rms_bwd(dqn, qhat, qrstd, gq_ref[...])
        dgq_ref[...] += jnp.sum(dg_rows, axis=0, keepdims=True)
        dqcb = dqc.astype(BF16)
        acc_dq[...] += _tn(xb, dqcb)
        _, xhat, rstd = _rms(h_ref[...], g_ref[...])
        dh, dg_rows = _rms_bwd(_nt(dqcb, wdq_ref[...]), xhat, rstd, g_ref[...])
        dh_ref[...] = dr_ref[...] + dh
        dg_ref[...] += jnp.sum(dg_rows, axis=0, keepdims=True)

        @pl.when(i == n_i - 1)
        def _():
            dwuq_ref[...] = acc_uq[...].astype(BF16)
            dwdq_ref[...] = acc_dq[...].astype(BF16)

    rope = pl.BlockSpec((tm, ROPE_PAD), lambda i: (i, 0))
    row = pl.BlockSpec((tm, d), lambda i: (i, 0))
    vec = pl.BlockSpec((1, d), lambda i: (0, 0))
    return _call(body, name, (n_i,),
                 [pl.BlockSpec((N_HEADS, tm, QK), lambda i: (0, i, 0)), row, row, vec, row,
                  pl.BlockSpec((d, rank), lambda i: (0, 0)),
                  pl.BlockSpec((1, rank), lambda i: (0, 0)),
                  pl.BlockSpec((N_HEADS, rank, QK), lambda i: (0, 0, 0)), rope, rope],
                 (row,
                  pl.BlockSpec((N_HEADS, rank, QK), lambda i: (0, 0, 0)),
                  pl.BlockSpec((d, rank), lambda i: (0, 0)),
                  pl.BlockSpec((1, rank), lambda i: (0, 0)), vec),
                 (jax.ShapeDtypeStruct((seq, d), F32),
                  jax.ShapeDtypeStruct((N_HEADS, rank, QK), BF16),
                  jax.ShapeDtypeStruct((d, rank), BF16),
                  jax.ShapeDtypeStruct((1, rank), F32),
                  jax.ShapeDtypeStruct((1, d), F32)),
                 (dq, xn, h, g, dres, wdq, gq, wuq, cos, sin),
                 scratch=[pltpu.VMEM((N_HEADS, rank, QK), F32), pltpu.VMEM((d, rank), F32)],
                 rider=rider)


def kv_fwd(h, g, wdkv, gkv, wukv, cos, sin, name, rider=()):
    seq, d = h.shape
    tm = min(ROW_TILE, seq)
    wk = KV_RANK + ROPE_PAD

    def body(h_ref, g_ref, wdkv_ref, gkv_ref, wukv_ref, cos_ref, sin_ref, k_ref, v_ref, c_ref):
        xk = _rms(h_ref[...], g_ref[...])[0].astype(BF16)
        ckv = _nn(xk, wdkv_ref[...])
        c_kv = ckv[:, 0:KV_RANK]
        c_ref[...] = c_kv
        kr = _rope_fwd(ckv[:, KV_RANK:wk], cos_ref[...], sin_ref[...]).astype(BF16)
        ckn = _rms(c_kv, gkv_ref[...])[0].astype(BF16)
        for hd in range(N_HEADS):
            kvh = _nn(ckn, wukv_ref[hd])
            k_ref[hd, :, 0:NOPE] = kvh[:, 0:NOPE].astype(BF16)
            k_ref[hd, :, NOPE:QK] = kr
            v_ref[hd] = kvh[:, NOPE:NOPE + VDIM].astype(BF16)

    rope = pl.BlockSpec((tm, ROPE_PAD), lambda i: (i, 0))
    return _call(body, name, (seq // tm,),
                 [pl.BlockSpec((tm, d), lambda i: (i, 0)),
                  pl.BlockSpec((1, d), lambda i: (0, 0)),
                  pl.BlockSpec((d, wk), lambda i: (0, 0)),
                  pl.BlockSpec((1, KV_RANK), lambda i: (0, 0)),
                  pl.BlockSpec((N_HEADS, KV_RANK, NOPE + VDIM), lambda i: (0, 0, 0)), rope, rope],
                 (pl.BlockSpec((N_HEADS, tm, QK), lambda i: (0, i, 0)),
                  pl.BlockSpec((N_HEADS, tm, VDIM), lambda i: (0, i, 0)),
                  pl.BlockSpec((tm, KV_RANK), lambda i: (i, 0))),
                 (jax.ShapeDtypeStruct((N_HEADS, seq, QK), BF16),
                  jax.ShapeDtypeStruct((N_HEADS, seq, VDIM), BF16),
                  jax.ShapeDtypeStruct((seq, KV_RANK), F32)),
                 (h, g, wdkv, gkv, wukv, cos, sin), rider=rider)


def kv_bwd(dks, dvs, c_kv, h, g, dres, wdkv, gkv, wukv, cos, sin, name, rider=()):
    seq, d = h.shape
    tm = min(ROW_TILE, seq)
    n_i = seq // tm
    wk = KV_RANK + ROPE_PAD
    n_b = len(dks)

    def body(*refs):
        dk_refs = refs[:n_b]
        dv_refs = refs[n_b:2 * n_b]
        (c_ref, h_ref, g_ref, dr_ref, wdkv_ref, gkv_ref, wukv_ref, cos_ref, sin_ref,
         dh_ref, dwukv_ref, dwdkv_ref, dgkv_ref, dg_ref, acc_ukv, acc_dkv) = refs[2 * n_b:]
        i = pl.program_id(0)

        @pl.when(i == 0)
        def _():
            acc_ukv[...] = jnp.zeros_like(acc_ukv)
            acc_dkv[...] = jnp.zeros_like(acc_dkv)
            dgkv_ref[...] = jnp.zeros_like(dgkv_ref)
            dg_ref[...] = jnp.zeros_like(dg_ref)
        ckn, chat, crstd = _rms(c_ref[...], gkv_ref[...])
        cknb = ckn.astype(BF16)
        dckn = jnp.zeros((tm, KV_RANK), F32)
        dkr = jnp.zeros((tm, ROPE_PAD), F32)
        for hd in range(N_HEADS):
            dk = dk_refs[0][hd].astype(F32)
            dv = dv_refs[0][hd].astype(F32)
            for j in range(1, n_b):
                dk = dk + dk_refs[j][hd].astype(F32)
                dv = dv + dv_refs[j][hd].astype(F32)
            dkr = dkr + dk[:, NOPE:QK]
            dkvh = jnp.concatenate([dk[:, 0:NOPE].astype(BF16), dv.astype(BF16)], axis=1)
            dckn = dckn + _nt(dkvh, wukv_ref[hd])
            acc_ukv[hd] += _tn(cknb, dkvh)
        dc_kv, dg_rows = _rms_bwd(dckn, chat, crstd, gkv_ref[...])
        dgkv_ref[...] += jnp.sum(dg_rows, axis=0, keepdims=True)
        dkr_raw = _rope_bwd(dkr, cos_ref[...], sin_ref[...])
        dckv = jnp.concatenate([dc_kv.astype(BF16), dkr_raw.astype(BF16)], axis=1)
        xk, xhat, rstd = _rms(h_ref[...], g_ref[...])
        acc_dkv[...] += _tn(xk.astype(BF16), dckv)
        dh, dg_rows = _rms_bwd(_nt(dckv, wdkv_ref[...]), xhat, rstd, g_ref[...])
        dh_ref[...] = dr_ref[...] + dh
        dg_ref[...] += jnp.sum(dg_rows, axis=0, keepdims=True)

        @pl.when(i == n_i - 1)
        def _():
            dwukv_ref[...] = acc_ukv[...].astype(BF16)
            dwdkv_ref[...] = acc_dkv[...].astype(BF16)

    kspec = pl.BlockSpec((N_HEADS, tm, QK), lambda i: (0, i, 0))
    vspec = pl.BlockSpec((N_HEADS, tm, VDIM), lambda i: (0, i, 0))
    rope = pl.BlockSpec((tm, ROPE_PAD), lambda i: (i, 0))
    row = pl.BlockSpec((tm, d), lambda i: (i, 0))
    vec = pl.BlockSpec((1, d), lambda i: (0, 0))
    return _call(body, name, (n_i,),
                 [kspec] * n_b + [vspec] * n_b + [
                     pl.BlockSpec((tm, KV_RANK), lambda i: (i, 0)), row, vec, row,
                     pl.BlockSpec((d, wk), lambda i: (0, 0)),
                     pl.BlockSpec((1, KV_RANK), lambda i: (0, 0)),
                     pl.BlockSpec((N_HEADS, KV_RANK, NOPE + VDIM), lambda i: (0, 0, 0)), rope, rope],
                 (row,
                  pl.BlockSpec((N_HEADS, KV_RANK, NOPE + VDIM), lambda i: (0, 0, 0)),
                  pl.BlockSpec((d, wk), lambda i: (0, 0)),
                  pl.BlockSpec((1, KV_RANK), lambda i: (0, 0)), vec),
                 (jax.ShapeDtypeStruct((seq, d), F32),
                  jax.ShapeDtypeStruct((N_HEADS, KV_RANK, NOPE + VDIM), BF16),
                  jax.ShapeDtypeStruct((d, wk), BF16),
                  jax.ShapeDtypeStruct((1, KV_RANK), F32),
                  jax.ShapeDtypeStruct((1, d), F32)),
                 (*dks, *dvs, c_kv, h, g, dres, wdkv, gkv, wukv, cos, sin),
                 scratch=[pltpu.VMEM((N_HEADS, KV_RANK, NOPE + VDIM), F32), pltpu.VMEM((d, wk), F32)],
                 rider=rider)


def o_bwd(dh, o, wo, name, rider=()):
    seq, d = dh.shape
    hv = o.shape[1]
    tm = min(ROW_TILE, seq)
    n_i = seq // tm

    def body(dh_ref, o_ref, wo_ref, do_ref, dwo_ref, acc):
        i = pl.program_id(0)

        @pl.when(i == 0)
        def _():
            acc[...] = jnp.zeros_like(acc)
        dhb = dh_ref[...].astype(BF16)
        do_ref[...] = _nt(dhb, wo_ref[...]).astype(BF16)
        acc[...] += _tn(o_ref[...], dhb)

        @pl.when(i == n_i - 1)
        def _():
            dwo_ref[...] = acc[...].astype(BF16)

    return _call(body, name, (n_i,),
                 [pl.BlockSpec((tm, d), lambda i: (i, 0)),
                  pl.BlockSpec((tm, hv), lambda i: (i, 0)),
                  pl.BlockSpec((hv, d), lambda i: (0, 0))],
                 (pl.BlockSpec((tm, hv), lambda i: (i, 0)),
                  pl.BlockSpec((hv, d), lambda i: (0, 0))),
                 (jax.ShapeDtypeStruct((seq, hv), BF16), jax.ShapeDtypeStruct((hv, d), BF16)),
                 (dh, o, wo), scratch=[pltpu.VMEM((hv, d), F32)], rider=rider)


def _mask_diagonal(s):
    row = lax.broadcasted_iota(jnp.int32, s.shape, 0)
    col = lax.broadcasted_iota(jnp.int32, s.shape, 1)
    return jnp.where(col <= row, s, NEG_BIG)


def attn_fwd(q, k, v, name, rider=()):
    _, seq, _ = q.shape
    t = min(ATTN_TILE, seq // 2)
    n_pair = seq // (2 * t)

    def body(q_ref, k_ref, v_ref, o_ref, lse_ref):
        qi = pl.program_id(1)
        q_a = q_ref[0:t, :]
        q_b = q_ref[t:2 * t, :]

        def rows(j):
            return pl.ds(pl.multiple_of(j * t, t), t)

        def update(qx, kb, vb, state, diagonal=False):
            m, l, acc = state
            s = _nt(qx, kb)
            if diagonal:
                s = _mask_diagonal(s)
            m_new = jnp.maximum(m, jnp.max(s, axis=1, keepdims=True))
            p = jnp.exp(s - m_new)
            alpha = jnp.exp(m - m_new)
            l = alpha * l + jnp.sum(p, axis=1, keepdims=True)
            acc = alpha * acc + _nn(p.astype(BF16), vb)
            return m_new, l, acc

        def step(j, carry):
            both = pl.ds(pl.multiple_of(j * 2 * t, 2 * t), 2 * t)
            kb, vb = k_ref[both, :], v_ref[both, :]
            return update(q_a, kb, vb, carry[0:3]) + update(q_b, kb, vb, carry[3:6])

        init = (jnp.full((t, 1), NEG_BIG, F32), jnp.zeros((t, 1), F32), jnp.zeros((t, VDIM), F32))
        carry = lax.fori_loop(0, qi, step, init + init)
        k0, v0 = k_ref[rows(2 * qi), :], v_ref[rows(2 * qi), :]
        k1, v1 = k_ref[rows(2 * qi + 1), :], v_ref[rows(2 * qi + 1), :]
        state_a = update(q_a, k0, v0, carry[0:3], diagonal=True)
        state_b = update(q_b, k1, v1, update(q_b, k0, v0, carry[3:6]), diagonal=True)
        for half, (m, l, acc) in enumerate((state_a, state_b)):
            o_ref[half * t:(half + 1) * t, :] = (acc / l).astype(BF16)
            lse_ref[half * t:(half + 1) * t, :] = jnp.broadcast_to(m + jnp.log(l), (t, LANES))

    return _call(body, name, (N_HEADS, n_pair),
                 [pl.BlockSpec((None, 2 * t, QK), lambda h, i: (h, i, 0)),
                  pl.BlockSpec((None, seq, QK), lambda h, i: (h, 0, 0)),
                  pl.BlockSpec((None, seq, VDIM), lambda h, i: (h, 0, 0))],
                 (pl.BlockSpec((2 * t, VDIM), lambda h, i: (i, h)),
                  pl.BlockSpec((None, 2 * t, LANES), lambda h, i: (h, i, 0))),
                 (jax.ShapeDtypeStruct((seq, N_HEADS * VDIM), BF16),
                  jax.ShapeDtypeStruct((N_HEADS, seq, LANES), F32)),
                 (q, k, v), rider=rider)


def attn_bwd(q, k, v, o, do, lse, name, rider=()):
    _, seq, _ = q.shape
    t = min(ATTN_TILE, seq // 2)
    n_q = seq // t
    n_pair = n_q // 2

    def body(q_ref, k_ref, v_ref, o_ref, do_ref, lse_ref, dq_ref, dk_ref, dv_ref,
             dq_acc, dk_acc, dv_acc):
        kj = pl.program_id(1)

        @pl.when(kj == 0)
        def _():
            dq_acc[...] = jnp.zeros_like(dq_acc)
        halves = (slice(0, t), slice(t, 2 * t))

        def block(i, masks, n_rows=t):
            rows = pl.ds(pl.multiple_of(i * n_rows, n_rows), n_rows)
            qb = q_ref[rows, :]
            dob = do_ref[rows, :]
            lse_col = lse_ref[rows, 0:1]
            delta = jnp.sum(dob.astype(F32) * o_ref[rows, :].astype(F32), axis=1, keepdims=True)
            dq, out = None, {}
            for x, diagonal in enumerate(masks):
                if diagonal is None:
                    continue
                kb, vb = k_ref[halves[x], :], v_ref[halves[x], :]
                s = _nt(qb, kb)
                if diagonal:
                    s = _mask_diagonal(s)
                p = jnp.exp(s - lse_col)
                ds = (p * (_nt(dob, vb) - delta)).astype(BF16)
                out[x] = (_tn(p.astype(BF16), dob), _tn(ds, qb))
                part = _nn(ds, kb)
                dq = part if dq is None else dq + part
            dq_acc[rows, :] += dq
            return out

        first = block(2 * kj, (True, None))
        second = block(2 * kj + 1, (False, True))
        dv_acc[halves[0], :] = first[0][0] + second[0][0]
        dk_acc[halves[0], :] = first[0][1] + second[0][1]
        dv_acc[halves[1], :] = second[1][0]
        dk_acc[halves[1], :] = second[1][1]

        def step(i, carry):
            out = block(i, (False, False), n_rows=2 * t)
            for x in (0, 1):
                dv_acc[halves[x], :] += out[x][0]
                dk_acc[halves[x], :] += out[x][1]
            return carry

        lax.fori_loop(kj + 1, n_pair, step, 0)
        dk_ref[...] = dk_acc[...].astype(BF16)
        dv_ref[...] = dv_acc[...].astype(BF16)

        @pl.when(kj == n_pair - 1)
        def _():
            dq_ref[...] = dq_acc[...].astype(BF16)

    head_rows = pl.BlockSpec((seq, VDIM), lambda h, j: (0, h))
    return _call(body, name, (N_HEADS, n_pair),
                 [pl.BlockSpec((None, seq, QK), lambda h, j: (h, 0, 0)),
                  pl.BlockSpec((None, 2 * t, QK), lambda h, j: (h, j, 0)),
                  pl.BlockSpec((None, 2 * t, VDIM), lambda h, j: (h, j, 0)),
                  head_rows, head_rows,
                  pl.BlockSpec((None, seq, LANES), lambda h, j: (h, 0, 0))],
                 (pl.BlockSpec((None, seq, QK), lambda h, j: (h, 0, 0)),
                  pl.BlockSpec((None, 2 * t, QK), lambda h, j: (h, j, 0)),
                  pl.BlockSpec((None, 2 * t, VDIM), lambda h, j: (h, j, 0))),
                 (jax.ShapeDtypeStruct((N_HEADS, seq, QK), BF16),
                  jax.ShapeDtypeStruct((N_HEADS, seq, QK), BF16),
                  jax.ShapeDtypeStruct((N_HEADS, seq, VDIM), BF16)),
                 (q, k, v, o, do, lse),
                 scratch=[pltpu.VMEM((seq, QK), F32), pltpu.VMEM((2 * t, QK), F32),
                          pltpu.VMEM((2 * t, VDIM), F32)], rider=rider)


def loss_head(h, g, target, name):
    seq, d = h.shape
    tm = min(ROW_TILE, seq)

    def body(h_ref, g_ref, t_ref, l_ref, dh_ref, dg_ref):
        i = pl.program_id(0)

        @pl.when(i == 0)
        def _():
            l_ref[...] = jnp.zeros_like(l_ref)
            dg_ref[...] = jnp.zeros_like(dg_ref)
        y, xhat, rstd = _rms(h_ref[...], g_ref[...])
        diff = y - t_ref[...]
        l_ref[...] += jnp.sum(jnp.sum(diff * diff, axis=1, keepdims=True), axis=0, keepdims=True)
        dh, dg_rows = _rms_bwd(diff * (1.0 / d), xhat, rstd, g_ref[...])
        dh_ref[...] = dh
        dg_ref[...] += jnp.sum(dg_rows, axis=0, keepdims=True)

    row = pl.BlockSpec((tm, d), lambda i: (i, 0))
    vec = pl.BlockSpec((1, d), lambda i: (0, 0))
    return _call(body, name, (seq // tm,), [row, vec, row],
                 (pl.BlockSpec((1, LANES), lambda i: (0, 0)), row, vec),
                 (jax.ShapeDtypeStruct((1, LANES), F32), jax.ShapeDtypeStruct((seq, d), F32),
                  jax.ShapeDtypeStruct((1, d), F32)),
                 (h, g, target))[0]


def _pack(parts):
    rows = []
    for p in parts:
        flat = p.reshape(-1)
        n_rows = -(-flat.shape[0] // (8 * LANES)) * 8
        flat = jnp.pad(flat, (0, n_rows * LANES - flat.shape[0]))
        rows.append(flat.reshape(n_rows, LANES))
    return jnp.concatenate(rows, axis=0)


def _unpack(packed, shapes):
    lead = packed.shape[:-2]
    out, r0 = [], 0
    for shape in shapes:
        size = 1
        for s in shape:
            size *= s
        n_rows = -(-size // (8 * LANES)) * 8
        part = packed[..., r0:r0 + n_rows, :].reshape(lead + (n_rows * LANES,))
        out.append(part[..., :size].reshape(lead + tuple(shape)))
        r0 += n_rows
    return out


FWD_RIDERS = {
    "mixer_fwd0": [("ffn_w_up", 0)],
    "ffn_fwd0": [("ffn_w_down", 0), ("a_w_in", 1), ("a_w_out", 1)],
    "mixer_fwd1": [("ffn_w_up", 1)],
    "ffn_fwd1": [("ffn_w_down", 1), ("w_dkv", 0), ("w_ukv", 0), ("b_w_dq", 0), ("b_w_uq", 0)],
    "attn_fwd0": [("b_w_o", 0), ("ffn_w_up", 2), ("ffn_w_down", 2), ("b_w_dq", 1), ("b_w_uq", 1)],
    "attn_fwd1": [("b_w_o", 1), ("ffn_w_up", 3), ("ffn_w_down", 3)],
}
BWD_RIDERS = {
    "attn_bwd1": [("ffn_w_down", 3), ("ffn_w_up", 3), ("b_w_o", 1)],
    "attn_bwd0": [("ffn_w_down", 2), ("ffn_w_up", 2), ("b_w_o", 0)],
    "ffn_bwd1": [("b_w_uq", 1), ("b_w_dq", 1), ("b_w_uq", 0), ("b_w_dq", 0), ("w_ukv", 0), ("w_dkv", 0)],
    "ffn_in_bwd1": [("ffn_w_down", 1), ("ffn_w_up", 1, "pair")],
    "ffn_bwd0": [("ffn_w_up", 1, "chip"), ("a_w_in", 1), ("a_w_out", 1)],
    "ffn_in_bwd0": [("ffn_w_down", 0), ("ffn_w_up", 0, "pair")],
    "mixer_bwd0": [("ffn_w_up", 0, "chip")],
    "mixer_in_bwd0": [("a_w_out", 0), ("a_w_in", 0, "pair")],
    "adamw_a_w_out": [("a_w_in", 0, "chip")],
}


def kernel(x, a_mix_norm, a_w_in, a_conv, a_w_out, b_mix_norm, b_w_dq, b_q_norm, b_w_uq, b_w_o, kv_in_norm, w_dkv, kv_norm, w_ukv, ffn_norm, ffn_w_up, ffn_conv, ffn_w_down, final_norm, loss_target, m_a_mix_norm, m_a_w_in, m_a_conv, m_a_w_out, m_b_mix_norm, m_b_w_dq, m_b_q_norm, m_b_w_uq, m_b_w_o, m_kv_in_norm, m_w_dkv, m_kv_norm, m_w_ukv, m_ffn_norm, m_ffn_w_up, m_ffn_conv, m_ffn_w_down, m_final_norm, v_a_mix_norm, v_a_w_in, v_a_conv, v_a_w_out, v_b_mix_norm, v_b_w_dq, v_b_q_norm, v_b_w_uq, v_b_w_o, v_kv_in_norm, v_w_dkv, v_kv_norm, v_w_ukv, v_ffn_norm, v_ffn_w_up, v_ffn_conv, v_ffn_w_down, v_final_norm):
    seq, d = x.shape[1], x.shape[2]
    me = 4 * lax.axis_index("x") + 2 * lax.axis_index("y") + lax.axis_index("c")
    h0 = x.reshape(seq, d)
    target = loss_target.reshape(seq, d)
    cos, sin = _rope_tables(seq)
    rank = b_w_dq.shape[-1]
    f8 = ffn_w_up.shape[-1]
    fd = ffn_w_down.shape[1]
    dshard = a_w_out.shape[1]
    hv = N_HEADS * VDIM

    shards = {"a_w_in": a_w_in, "a_w_out": a_w_out, "b_w_dq": b_w_dq, "b_w_uq": b_w_uq,
              "b_w_o": b_w_o, "w_dkv": w_dkv[None], "w_ukv": w_ukv[None],
              "ffn_w_up": ffn_w_up, "ffn_w_down": ffn_w_down}

    def relayout(name, g):
        if name == "a_w_in":
            w = jnp.transpose(g, (1, 0, 2)).reshape(d, 3, d)
            return jnp.transpose(w, (1, 0, 2))
        if name == "a_w_out":
            return g.reshape(d, d)
        if name == "b_w_dq":
            return g.reshape(d, rank)
        if name == "b_w_uq":
            return jnp.pad(g, ((0, 0), (0, 0), (0, QK - NOPE - ROPE)))
        if name == "b_w_o":
            return g.reshape(hv, d)
        if name == "w_dkv":
            return jnp.pad(g.reshape(d, KV_RANK + ROPE), ((0, 0), (0, ROPE_PAD - ROPE)))
        if name == "ffn_w_down":
            return g.reshape(N_DEV // 2, 2 * fd, d)
        return g

    weights = {}

    def ag_rider(host):
        return [("ag", shards[n][l].astype(BF16)) for n, l in FWD_RIDERS.get(host, [])]

    def ag_done(host, outs):
        for (n, l), g in zip(FWD_RIDERS.get(host, []), outs):
            weights[n, l] = relayout(n, g)

    small_shapes = [a_mix_norm.shape, a_conv.shape, ffn_conv.shape]
    first = exchange([("ag", a_w_in[0].astype(BF16)), ("ag", a_w_out[0].astype(BF16)),
                      ("ag", _pack([a_mix_norm, a_conv, ffn_conv]))], "ag_first")
    weights["a_w_in", 0] = relayout("a_w_in", first[0])
    weights["a_w_out", 0] = relayout("a_w_out", first[1])
    s_mix, s_aconv, s_fconv = _unpack(first[2], small_shapes)
    a_gain = jnp.transpose(s_mix, (1, 0, 2)).reshape(N_A, d)
    a_cw = jnp.transpose(s_aconv, (1, 2, 0, 3)).reshape(N_A, 3, d)
    f_cw = jnp.transpose(s_fconv, (1, 0, 2, 3))

    def mixer_gain(layer):
        if layer >= DEPTH:
            return None
        return a_gain[layer][None] if layer < N_A else b_mix_norm[layer - N_A][None]

    saved = {}
    h = h0
    xn = norm_fwd(h, mixer_gain(0), "norm_first")
    kv = None
    for layer in range(DEPTH):
        saved["hm", layer], saved["xm", layer] = h, xn
        if layer < N_A:
            name = f"mixer_fwd{layer}"
            (u4, z), r = mixer_fwd(xn, weights["a_w_in", layer], a_cw[layer], name, rider=ag_rider(name))
            ag_done(name, r)
            saved["mix", layer] = (u4, z)
            name = f"mixer_out{layer}"
            (h, xn), r = proj_residual(z[None], weights["a_w_out", layer][None], h, name,
                                       g_next=ffn_norm[layer][None], rider=ag_rider(name))
            ag_done(name, r)
        else:
            j = layer - N_A
            name = f"q_fwd{j}"
            (q,), r = q_fwd(xn, weights["b_w_dq", j], b_q_norm[j][None], weights["b_w_uq", j],
                            cos, sin, name, rider=ag_rider(name))
            ag_done(name, r)
            name = f"attn_fwd{j}"
            (o, lse), r = attn_fwd(q, kv[0], kv[1], name, rider=ag_rider(name))
            ag_done(name, r)
            saved["attn", layer] = (q, o, lse)
            name = f"attn_out{j}"
            (h, xn), r = proj_residual(o[None], weights["b_w_o", j][None], h, name,
                                       g_next=ffn_norm[layer][None], rider=ag_rider(name))
            ag_done(name, r)
        saved["hf", layer], saved["xf", layer] = h, xn
        name = f"ffn_fwd{layer}"
        (up2, cv2, act), r = ffn_fwd(xn, weights["ffn_w_up", layer], f_cw[layer], name, rider=ag_rider(name))
        ag_done(name, r)
        saved["ffn", layer] = (up2, cv2, act)
        name = f"ffn_out{layer}"
        (h, xn), r = proj_residual(act, weights["ffn_w_down", layer], h, name,
                                   g_next=mixer_gain(layer + 1), rider=ag_rider(name))
        ag_done(name, r)
        if layer == N_A - 1:
            (k_all, v_all, c_kv), r = kv_fwd(h, kv_in_norm[None], weights["w_dkv", 0], kv_norm[None],
                                             weights["w_ukv", 0], cos, sin, "kv_fwd",
                                             rider=ag_rider("kv_fwd"))
            ag_done("kv_fwd", r)
            kv = (k_all, v_all, c_kv)

    sq_err, dh, d_final = loss_head(h, final_norm[None], target, "loss_head")
    loss = lax.psum(sq_err[0, 0] * (0.5 / d), ("x", "y", "c"))

    grads = {}
    parts = {}

    pair_sums = {}

    def by_chip(g):
        return g.reshape((N_DEV // 2, 2) + g.shape[1:])

    def rs_rider(host):
        tasks = []
        for key in BWD_RIDERS.get(host, []):
            if len(key) == 2:
                tasks.append(("rs", grads[key]))
            elif key[2] == "pair":
                tasks.append(("rs_pair", by_chip(grads[key[:2]])))
            else:
                tasks.append(("rs_chip", pair_sums[key[:2]]))
        return tasks

    def rs_done(host, outs):
        for key, p in zip(BWD_RIDERS.get(host, []), outs):
            if len(key) == 3 and key[2] == "pair":
                pair_sums[key[:2]] = pair_sum(by_chip(grads[key[:2]]), p, f"pair_sum_{key[0]}{key[1]}")
            else:
                parts[key[:2]] = p

    d_ffn_norm = [None] * DEPTH
    d_fconv = [None] * DEPTH
    d_a_gain = [None] * N_A
    d_aconv = [None] * N_A
    d_b_gain = [None] * N_B
    d_q_gain = [None] * N_B
    dks, dvs = [], []
    for layer in reversed(range(DEPTH)):
        if layer == N_A - 1:
            hk = saved["hm", layer + 1]
            (dh, dwukv, dwdkv, d_kv_gain, d_kvin_gain), r = kv_bwd(
                dks, dvs, kv[2], hk, kv_in_norm[None], dh, weights["w_dkv", 0], kv_norm[None],
                weights["w_ukv", 0], cos, sin, "kv_bwd", rider=rs_rider("kv_bwd"))
            rs_done("kv_bwd", r)
            grads["w_ukv", 0] = dwukv
            grads["w_dkv", 0] = dwdkv[:, :KV_RANK + ROPE].reshape(N_DEV, dshard, KV_RANK + ROPE)
        up2, cv2, act = saved["ffn", layer]
        name = f"ffn_bwd{layer}"
        (dup2, dwup, dwdown, dcw), r = ffn_bwd(dh, weights["ffn_w_down", layer], up2, cv2, act,
                                               saved["xf", layer], f_cw[layer], name, rider=rs_rider(name))
        rs_done(name, r)
        grads["ffn_w_up", layer] = dwup.reshape(N_DEV, d, f8)
        grads["ffn_w_down", layer] = dwdown.reshape(N_DEV, fd, d)
        d_fconv[layer] = dcw.reshape(N_DEV, 3, f8)
        name = f"ffn_in_bwd{layer}"
        (dh, d_ffn_norm[layer]), r = proj_t_rms_bwd(dup2.reshape(N_DEV, seq, f8), weights["ffn_w_up", layer],
                                                    saved["hf", layer], ffn_norm[layer][None], dh, name,
                                                    rider=rs_rider(name))
        rs_done(name, r)
        hm, xm = saved["hm", layer], saved["xm", layer]
        if layer < N_A:
            u4, z = saved["mix", layer]
            name = f"mixer_bwd{layer}"
            (du3, dwin3, dwout, dcw), r = mixer_bwd(dh, weights["a_w_out", layer], u4, z, xm, a_cw[layer],
                                                    name, rider=rs_rider(name))
            rs_done(name, r)
            dwin = jnp.transpose(dwin3, (1, 0, 2)).reshape(d, N_DEV, 3 * d // N_DEV)
            grads["a_w_in", layer] = jnp.transpose(dwin, (1, 0, 2))
            grads["a_w_out", layer] = dwout.reshape(N_DEV, dshard, d)
            d_aconv[layer] = dcw
            name = f"mixer_in_bwd{layer}"
            extra = []
            if layer == 0:
                early_small = [
                    d_a_gain[1],
                    d_aconv[1],
                    jnp.concatenate(d_b_gain, axis=0),
                    jnp.concatenate(d_q_gain, axis=0),
                    d_kvin_gain[0],
                    d_kv_gain[0],
                    jnp.concatenate(d_ffn_norm, axis=0),
                    jnp.stack(d_fconv),
                    d_final[0],
                ]
                extra = [("ag", _pack(early_small))]
            (dh, d_a_gain[layer]), r = proj_t_rms_bwd(du3, weights["a_w_in", layer], hm, a_gain[layer][None],
                                                      dh, name, rider=rs_rider(name) + extra)
            rs_done(name, r)
            if layer == 0:
                g_early = r[-1]
        else:
            j = layer - N_A
            q, o, lse = saved["attn", layer]
            name = f"attn_out_bwd{j}"
            (do, dwo), r = o_bwd(dh, o, weights["b_w_o", j], name, rider=rs_rider(name))
            rs_done(name, r)
            grads["b_w_o", j] = dwo.reshape(N_DEV, dshard, d)
            name = f"attn_bwd{j}"
            (dq, dk, dv), r = attn_bwd(q, kv[0], kv[1], o, do, lse, name, rider=rs_rider(name))
            rs_done(name, r)
            dks.append(dk)
            dvs.append(dv)
            name = f"q_bwd{j}"
            (dh, dwuq, dwdq, d_q_gain[j], d_b_gain[j]), r = q_bwd(
                dq, xm, hm, b_mix_norm[j][None], dh, weights["b_w_dq", j], b_q_norm[j][None],
                weights["b_w_uq", j], cos, sin, name, rider=rs_rider(name))
            rs_done(name, r)
            grads["b_w_uq", j] = dwuq[:, :, :NOPE + ROPE]
            grads["b_w_dq", j] = dwdq.reshape(N_DEV, dshard, rank)
    grad_x = dh.reshape(x.shape)

    late_small = [d_a_gain[0], d_aconv[0]]
    full_shapes = [t.shape for t in early_small + late_small]
    small_pack = _pack(late_small)

    res = {}

    def update(name, n_layers, w, m, v, extra=(), transposed=False):
        view = (lambda t: jnp.transpose(t, (0, 2, 1))) if transposed else (lambda t: t)
        call = sum_adamw_transposed if transposed else sum_adamw
        shard = w.shape if w.ndim == 3 else (1,) + w.shape
        host = f"adamw_{name}"
        outs, r = call([parts[name, l] for l in range(n_layers)], view(w.reshape(shard)),
                       view(m.reshape(shard)), view(v.reshape(shard)), host,
                       rider=rs_rider(host) + list(extra))
        rs_done(host, r)
        res[name] = [view(t).reshape(w.shape) for t in outs]
        return r[len(BWD_RIDERS.get(host, [])):]

    (g_late,) = update("a_w_out", N_A, a_w_out, m_a_w_out, v_a_w_out, extra=[("ag", small_pack)])
    update("ffn_w_down", DEPTH, ffn_w_down, m_ffn_w_down, v_ffn_w_down)
    update("ffn_w_up", DEPTH, ffn_w_up, m_ffn_w_up, v_ffn_w_up, transposed=True)
    update("b_w_dq", N_B, b_w_dq, m_b_w_dq, v_b_w_dq)
    update("b_w_uq", N_B, b_w_uq, m_b_w_uq, v_b_w_uq)
    update("b_w_o", N_B, b_w_o, m_b_w_o, v_b_w_o)
    update("w_dkv", 1, w_dkv, m_w_dkv, v_w_dkv)
    update("w_ukv", 1, w_ukv, m_w_ukv, v_w_ukv)
    update("a_w_in", N_A, a_w_in, m_a_w_in, v_a_w_in)

    summed = sum_slots(jnp.concatenate([g_early, g_late], axis=1), "sum_small_grads")
    (s_a_gain1, s_aconv1, s_b_gain, s_q_gain, s_kvin, s_kvn, s_ffn_gain, s_fconv_g,
     s_final, s_a_gain0, s_aconv0) = _unpack(summed, full_shapes)
    s_a_gain = jnp.concatenate([s_a_gain0, s_a_gain1], axis=0)
    s_aconv_g = jnp.stack([s_aconv0, s_aconv1])
    dsl = d // N_DEV
    small = [
        ("a_mix_norm", lax.dynamic_slice_in_dim(s_a_gain, me * dsl, dsl, axis=1), a_mix_norm, m_a_mix_norm, v_a_mix_norm),
        ("a_conv", lax.dynamic_slice_in_dim(s_aconv_g, me * dsl, dsl, axis=2), a_conv, m_a_conv, v_a_conv),
        ("b_mix_norm", s_b_gain, b_mix_norm, m_b_mix_norm, v_b_mix_norm),
        ("b_q_norm", s_q_gain, b_q_norm, m_b_q_norm, v_b_q_norm),
        ("kv_in_norm", s_kvin, kv_in_norm, m_kv_in_norm, v_kv_in_norm),
        ("kv_norm", s_kvn, kv_norm, m_kv_norm, v_kv_norm),
        ("ffn_norm", s_ffn_gain, ffn_norm, m_ffn_norm, v_ffn_norm),
        ("ffn_conv", lax.dynamic_index_in_dim(s_fconv_g, me, axis=1, keepdims=False), ffn_conv, m_ffn_conv, v_ffn_conv),
        ("final_norm", s_final, final_norm, m_final_norm, v_final_norm),
    ]
    shapes = [t[2].shape for t in small]
    packed = [_pack([t[k] for t in small])[None] for k in (1, 2, 3, 4)]
    outs, _ = sum_adamw([packed[0]], packed[1], packed[2], packed[3], "adamw_small")
    unpacked = [_unpack(t[0], shapes) for t in outs]
    for idx, t in enumerate(small):
        res[t[0]] = [unpacked[k][idx] for k in range(4)]

    order = ["a_mix_norm", "a_w_in", "a_conv", "a_w_out", "b_mix_norm", "b_w_dq", "b_q_norm",
             "b_w_uq", "b_w_o", "kv_in_norm", "w_dkv", "kv_norm", "w_ukv", "ffn_norm",
             "ffn_w_up", "ffn_conv", "ffn_w_down", "final_norm"]
    return (loss, grad_x, *[res[n][0] for n in order], *[res[n][1] for n in order],
            *[res[n][2] for n in order], *[res[n][3] for n in order])
```

```python
import functools

import numpy as np
import jax
import jax.numpy as jnp
from jax import lax
from jax.experimental import pallas as pl
from jax.experimental.pallas import tpu as pltpu

F32 = jnp.float32
BF16 = jnp.bfloat16

N_DEV = 8
N_HEADS = 8
NOPE = 128
ROPE = 64
ROPE_PAD = 128
QK = NOPE + ROPE_PAD
VDIM = 128
KV_RANK = 256
ROPE_THETA = 10000.0
RMS_EPS = 1e-6
ATTN_SCALE = (NOPE + ROPE) ** -0.5
N_A = 2
N_B = 2
DEPTH = 4

ADAM_LR = 0.001
ADAM_B1 = 0.9
ADAM_B2 = 0.999
ADAM_EPS = 1e-08
ADAM_WD = 0.01
ADAM_STEP = 10

V7X_VMEM_LIMIT = 56 * 1024 * 1024
BF16_SUBLANES = 16
ROW_TILE = 512
ROW_TILE_LARGE = 1024
ADAM_ROWS = 256
ATTN_TILE = 512
MIXER_CHUNK = 512
LANES = 128
NEG_BIG = -1e30
COPIES_PER_TASK = 7

MESH_ID = pl.DeviceIdType.MESH
ANY = pl.BlockSpec(memory_space=pl.ANY)


def _nt(a, b):
    return lax.dot_general(a, b, (((1,), (1,)), ((), ())), preferred_element_type=F32)


def _tn(a, b):
    return lax.dot_general(a, b, (((0,), (0,)), ((), ())), preferred_element_type=F32)


def _nn(a, b):
    return jnp.dot(a, b, preferred_element_type=F32)


def _rms(h, g):
    rstd = lax.rsqrt(jnp.mean(h * h, axis=-1, keepdims=True) + RMS_EPS)
    xhat = h * rstd
    return xhat * g, xhat, rstd


def _rms_bwd(dxn, xhat, rstd, g):
    dxhat = dxn * g
    dh = rstd * (dxhat - xhat * jnp.mean(dxhat * xhat, axis=-1, keepdims=True))
    return dh, dxn * xhat


def _shift_down(x, k, halo_rows):
    r = pltpu.roll(x, k, 0)
    row = lax.broadcasted_iota(jnp.int32, x.shape, 0)
    for t in range(k):
        r = jnp.where(row == t, halo_rows[t], r)
    return r


def _shift_up(x, k, halo_rows):
    n = x.shape[0]
    r = pltpu.roll(x, n - k, 0)
    row = lax.broadcasted_iota(jnp.int32, x.shape, 0)
    for t in range(k):
        r = jnp.where(row == n - k + t, halo_rows[t], r)
    return r


def _conv_taps(w_ref):
    return w_ref[0:1, :], w_ref[1:2, :], w_ref[2:3, :]


def _rope_swap(x):
    lane = lax.broadcasted_iota(jnp.int32, x.shape, 1)
    return jnp.where(lane < ROPE // 2, pltpu.roll(x, ROPE_PAD - ROPE // 2, 1),
                     pltpu.roll(x, ROPE // 2, 1))


def _rope_fwd(x, cos, sin):
    return x * cos + _rope_swap(x) * sin


def _rope_bwd(dy, cos, sin):
    return dy * cos - _rope_swap(dy) * sin


def _rope_tables(seq):
    inv = (1.0 / np.power(np.float32(ROPE_THETA), np.arange(0, ROPE, 2, dtype=np.float32) / np.float32(ROPE)))
    ang = np.arange(seq, dtype=np.float32)[:, None] * inv.astype(np.float32)[None, :]
    ang = ang.astype(np.float32).astype(np.float64)
    cos, sin = np.cos(ang).astype(np.float32), np.sin(ang).astype(np.float32)
    zero = np.zeros((seq, ROPE_PAD - ROPE), np.float32)
    return (jnp.asarray(np.concatenate([cos, cos, zero], axis=1)),
            jnp.asarray(np.concatenate([-sin, sin, zero], axis=1)))


def _row_tile(rows, cap, mult=8):
    best = None
    for t in range(mult, min(rows, cap) + 1, mult):
        if rows % t == 0:
            best = t
    return rows if best is None else best


class _AllGatherTask:
    def __init__(self, t, x_ref, out_ref, send_sems, recv_sems, local_sems):
        self.t, self.x_ref, self.out_ref = t, x_ref, out_ref
        self.send_sems, self.recv_sems, self.local_sems = send_sems, recv_sems, local_sems
        mx, my, mc = lax.axis_index("x"), lax.axis_index("y"), lax.axis_index("c")
        self.mc = mc
        self.me, self.sibling = (mx, my, mc), (mx, my, 1 - mc)
        self.chips = [(1 - mx, my), (mx, 1 - my), (1 - mx, 1 - my)]

    def _slot(self, px, py, pc):
        return self.out_ref.at[4 * px + 2 * py + pc]

    def _copy(self, k, block, to, src=None):
        s = COPIES_PER_TASK * self.t + k
        return pltpu.make_async_remote_copy(
            src_ref=self._slot(*block) if src is None else src, dst_ref=self._slot(*block),
            send_sem=self.send_sems.at[s], recv_sem=self.recv_sems.at[s],
            device_id=to, device_id_type=MESH_ID)

    def _mine(self):
        return pltpu.make_async_copy(self.x_ref, self._slot(*self.me), self.local_sems.at[self.t])

    def _first(self):
        out = [self._copy(0, self.me, self.sibling, src=self.x_ref)]
        out += [self._copy(1 + j, self.me, (*chip, self.mc), src=self.x_ref)
                for j, chip in enumerate(self.chips)]
        return out

    def _passed(self):
        return [self._copy(4 + j, (*chip, self.mc), self.sibling) for j, chip in enumerate(self.chips)]

    def start(self):
        self._mine().start()
        for cp in self._first():
            cp.start()

    def forward(self):
        passed = self._passed()
        for j, chip in enumerate(self.chips):
            self._copy(1 + j, (*chip, self.mc), self.me).wait_recv()
            passed[j].start()

    def finish(self):
        self._copy(0, self.sibling, self.me).wait_recv()
        for j, chip in enumerate(self.chips):
            self._copy(4 + j, (*chip, 1 - self.mc), self.me).wait_recv()
        for cp in self._first() + self._passed():
            cp.wait_send()
        self._mine().wait()


class _ReduceScatterTask:
    def __init__(self, t, g_ref, out_ref, send_sems, recv_sems, local_sems):
        self.t, self.g_ref, self.out_ref = t, g_ref, out_ref
        self.send_sems, self.recv_sems, self.local_sems = send_sems, recv_sems, local_sems
        mx, my, mc = lax.axis_index("x"), lax.axis_index("y"), lax.axis_index("c")
        self.me = 4 * mx + 2 * my + mc
        self.peers = []
        for k in range(1, N_DEV):
            px, py, pc = mx ^ ((k >> 2) & 1), my ^ ((k >> 1) & 1), mc ^ (k & 1)
            self.peers.append(((px, py, pc), 4 * px + 2 * py + pc))

    def _mine(self):
        return pltpu.make_async_copy(self.g_ref.at[self.me], self.out_ref.at[self.me],
                                     self.local_sems.at[self.t])

    def _copy(self, k, src_slot, dst_slot):
        s = COPIES_PER_TASK * self.t + k
        return pltpu.make_async_remote_copy(
            src_ref=self.g_ref.at[src_slot], dst_ref=self.out_ref.at[dst_slot],
            send_sem=self.send_sems.at[s], recv_sem=self.recv_sems.at[s],
            device_id=self.peers[k][0], device_id_type=MESH_ID)

    def start(self):
        self._mine().start()
        for k, (_, peer) in enumerate(self.peers):
            self._copy(k, peer, self.me).start()

    def forward(self):
        pass

    def finish(self):
        for k, (_, peer) in enumerate(self.peers):
            self._copy(k, self.me, peer).wait_recv()
        for k, (_, peer) in enumerate(self.peers):
            self._copy(k, peer, self.me).wait_send()
        self._mine().wait()


class _PairExchangeTask:
    def __init__(self, t, g_ref, out_ref, send_sems, recv_sems, local_sems):
        mx, my, mc = lax.axis_index("x"), lax.axis_index("y"), lax.axis_index("c")
        s = COPIES_PER_TASK * t
        self.copy = pltpu.make_async_remote_copy(
            src_ref=g_ref.at[:, 1 - mc], dst_ref=out_ref,
            send_sem=send_sems.at[s], recv_sem=recv_sems.at[s],
            device_id=(mx, my, 1 - mc), device_id_type=MESH_ID)

    def start(self):
        self.copy.start()

    def forward(self):
        pass

    def finish(self):
        self.copy.wait()


class _ChipScatterTask:
    def __init__(self, t, s_ref, out_ref, send_sems, recv_sems, local_sems):
        self.t, self.s_ref, self.out_ref = t, s_ref, out_ref
        self.send_sems, self.recv_sems, self.local_sems = send_sems, recv_sems, local_sems
        mx, my, mc = lax.axis_index("x"), lax.axis_index("y"), lax.axis_index("c")
        self.chip = 2 * mx + my
        self.peers = []
        for k in range(1, N_DEV // 2):
            px, py = mx ^ ((k >> 1) & 1), my ^ (k & 1)
            self.peers.append(((px, py, mc), 2 * px + py))

    def _mine(self):
        return pltpu.make_async_copy(self.s_ref.at[self.chip], self.out_ref.at[self.chip],
                                     self.local_sems.at[self.t])

    def _copy(self, k, src_slot, dst_slot):
        s = COPIES_PER_TASK * self.t + k
        return pltpu.make_async_remote_copy(
            src_ref=self.s_ref.at[src_slot], dst_ref=self.out_ref.at[dst_slot],
            send_sem=self.send_sems.at[s], recv_sem=self.recv_sems.at[s],
            device_id=self.peers[k][0], device_id_type=MESH_ID)

    def start(self):
        self._mine().start()
        for k, (_, peer) in enumerate(self.peers):
            self._copy(k, peer, self.chip).start()

    def forward(self):
        pass

    def finish(self):
        for k, (_, peer) in enumerate(self.peers):
            self._copy(k, self.chip, peer).wait_recv()
        for k, (_, peer) in enumerate(self.peers):
            self._copy(k, peer, self.chip).wait_send()
        self._mine().wait()


_TASKS = {"ag": _AllGatherTask, "rs": _ReduceScatterTask, "rs_pair": _PairExchangeTask,
          "rs_chip": _ChipScatterTask}


def _task_shape(kind, arr):
    if isinstance(kind, tuple):
        return jax.ShapeDtypeStruct((N_DEV,) + arr.shape[1:], arr.dtype)
    shape = {"ag": (N_DEV,) + arr.shape, "rs": arr.shape, "rs_chip": arr.shape,
             "rs_pair": arr.shape[:1] + arr.shape[2:]}[kind]
    return jax.ShapeDtypeStruct(shape, arr.dtype)


def _sem_shapes(n_tasks):
    return [pltpu.SemaphoreType.DMA((COPIES_PER_TASK * n_tasks,)),
            pltpu.SemaphoreType.DMA((COPIES_PER_TASK * n_tasks,)),
            pltpu.SemaphoreType.DMA((n_tasks,))]


def _make_tasks(rider, in_refs, out_refs, sems):
    tasks = []
    for t, (kind, _) in enumerate(rider):
        if isinstance(kind, tuple):
            tasks.append(_AllGatherTask(t, in_refs[t].at[kind[1]], out_refs[t], *sems))
        else:
            tasks.append(_TASKS[kind](t, in_refs[t], out_refs[t], *sems))
    return tasks


def exchange(rider, name):
    n = len(rider)

    def body(*refs):
        tasks = _make_tasks(rider, refs[:n], refs[n:2 * n], refs[2 * n:])
        for task in tasks:
            task.start()
        for task in tasks:
            task.forward()
        for task in tasks:
            task.finish()

    return list(pl.pallas_call(
        body, name=name, out_shape=tuple(_task_shape(k, a) for k, a in rider),
        in_specs=[ANY] * n, out_specs=(ANY,) * n, scratch_shapes=_sem_shapes(n),
    )(*[a for _, a in rider]))


def _call(body, name, grid, in_specs, out_specs, out_shape, args, scratch=(), rider=()):
    in_specs, out_specs, out_shape = list(in_specs), tuple(out_specs), tuple(out_shape)
    n_in, n_out, n_scr, n_r = len(in_specs), len(out_specs), len(scratch), len(rider)
    if n_r:
        def kern(*refs):
            ins, r_in = refs[:n_in], refs[n_in:n_in + n_r]
            o0 = n_in + n_r
            outs, r_out = refs[o0:o0 + n_out], refs[o0 + n_out:o0 + n_out + n_r]
            s0 = o0 + n_out + n_r
            scr, sems = refs[s0:s0 + n_scr], refs[s0 + n_scr:]
            step = 0
            for a, n in enumerate(grid):
                step = step * n + pl.program_id(a)
            n_steps = 1
            for n in grid:
                n_steps *= n

            @pl.when(step == 0)
            def _():
                for task in _make_tasks(rider, r_in, r_out, sems):
                    task.start()
            body(*ins, *outs, *scr)

            @pl.when(step == n_steps - 1)
            def _():
                tasks = _make_tasks(rider, r_in, r_out, sems)
                for task in tasks:
                    task.forward()
                for task in tasks:
                    task.finish()
    else:
        kern = body
    res = pl.pallas_call(
        kern, name=name, grid=grid,
        in_specs=in_specs + [ANY] * n_r, out_specs=out_specs + (ANY,) * n_r,
        out_shape=out_shape + tuple(_task_shape(k, a) for k, a in rider),
        scratch_shapes=list(scratch) + (_sem_shapes(n_r) if n_r else []),
        compiler_params=pltpu.CompilerParams(dimension_semantics=("arbitrary",) * len(grid),
                                             vmem_limit_bytes=V7X_VMEM_LIMIT),
    )(*args, *[a for _, a in rider])
    return list(res[:n_out]), list(res[n_out:])


def _adamw(g, w, m, v):
    m = ADAM_B1 * m + (1.0 - ADAM_B1) * g
    v = ADAM_B2 * v + (1.0 - ADAM_B2) * (g * g)
    m_hat = m / (1.0 - ADAM_B1 ** ADAM_STEP)
    v_hat = v / (1.0 - ADAM_B2 ** ADAM_STEP)
    delta = -ADAM_LR * (m_hat / (jnp.sqrt(v_hat) + ADAM_EPS) + ADAM_WD * w)
    return delta, m, v


def sum_adamw(parts, w, m, v, name, rider=()):
    n_l, rows, cols = w.shape
    mult = BF16_SUBLANES if parts[0].dtype == BF16 else 8
    tr = _row_tile(rows, ADAM_ROWS, mult)
    n_i = rows // tr

    def body(*refs):
        part_refs = refs[:n_l]
        w_ref, m_ref, v_ref, g_out, d_out, m_out, v_out = refs[n_l:]
        layer = pl.program_id(0)
        for k in range(n_l):
            @pl.when(layer == k)
            def _(k=k):
                g = part_refs[k][0].astype(F32)
                for s in range(1, parts[k].shape[0]):
                    g = g + part_refs[k][s].astype(F32)
                delta, m_new, v_new = _adamw(g, w_ref[...], m_ref[...], v_ref[...])
                g_out[...] = g
                d_out[...] = delta
                m_out[...] = m_new
                v_out[...] = v_new

    part_specs = [pl.BlockSpec((parts[k].shape[0], tr, cols), functools.partial(
        lambda l, i, k: (0, jnp.where(l == k, i, 0), 0), k=k)) for k in range(n_l)]
    wspec = pl.BlockSpec((None, tr, cols), lambda l, i: (l, i, 0))
    shape = jax.ShapeDtypeStruct(w.shape, F32)
    return _call(body, name, (n_l, n_i), part_specs + [wspec] * 3, (wspec,) * 4, (shape,) * 4,
                 (*parts, w, m, v), rider=rider)


def sum_adamw_transposed(parts, w_t, m_t, v_t, name, rider=()):
    n_l, cols, rows = w_t.shape
    tr = LANES
    n_i = rows // tr
    starts = list(range(0, cols - LANES + 1, LANES))
    if starts[-1] + LANES < cols:
        starts.append(cols - LANES)

    def body(*refs):
        part_refs = refs[:n_l]
        w_ref, m_ref, v_ref, g_out, d_out, m_out, v_out = refs[n_l:]
        layer = pl.program_id(0)
        for k in range(n_l):
            @pl.when(layer == k)
            def _(k=k):
                for c0 in starts:
                    piece = pl.ds(c0, LANES)
                    g = part_refs[k][0, :, piece].astype(F32)
                    for s in range(1, parts[k].shape[0]):
                        g = g + part_refs[k][s, :, piece].astype(F32)
                    g = g.T
                    delta, m_new, v_new = _adamw(g, w_ref[piece, :], m_ref[piece, :], v_ref[piece, :])
                    g_out[piece, :] = g
                    d_out[piece, :] = delta
                    m_out[piece, :] = m_new
                    v_out[piece, :] = v_new

    part_specs = [pl.BlockSpec((parts[k].shape[0], tr, cols), functools.partial(
        lambda l, i, k: (0, jnp.where(l == k, i, 0), 0), k=k)) for k in range(n_l)]
    wspec = pl.BlockSpec((None, cols, tr), lambda l, i: (l, 0, i))
    shape = jax.ShapeDtypeStruct(w_t.shape, F32)
    return _call(body, name, (n_l, n_i), part_specs + [wspec] * 3, (wspec,) * 4, (shape,) * 4,
                 (*parts, w_t, m_t, v_t), rider=rider)


def pair_sum(g4, other, name):
    n_chip, _, rows, cols = g4.shape
    tr = _row_tile(rows, 512, BF16_SUBLANES)

    def body(core_ref, g_ref, o_ref, s_ref):
        s_ref[...] = (g_ref[...].astype(F32) + o_ref[...].astype(F32)).astype(BF16)

    blk = pl.BlockSpec((None, tr, cols), lambda k, i, core: (k, i, 0))
    return pl.pallas_call(
        body, name=name, out_shape=jax.ShapeDtypeStruct((n_chip, rows, cols), g4.dtype),
        grid_spec=pltpu.PrefetchScalarGridSpec(
            num_scalar_prefetch=1, grid=(n_chip, rows // tr),
            in_specs=[pl.BlockSpec((None, None, tr, cols), lambda k, i, core: (k, core[0], i, 0)), blk],
            out_specs=blk),
        compiler_params=pltpu.CompilerParams(dimension_semantics=("arbitrary", "arbitrary"),
                                             vmem_limit_bytes=V7X_VMEM_LIMIT),
    )(lax.axis_index("c").astype(jnp.int32).reshape(1), g4, other)


def sum_slots(parts, name):
    n, rows, cols = parts.shape

    def body(p_ref, o_ref):
        acc = p_ref[0]
        for s in range(1, n):
            acc = acc + p_ref[s]
        o_ref[...] = acc

    return pl.pallas_call(
        body, name=name, out_shape=jax.ShapeDtypeStruct((rows, cols), F32),
        in_specs=[pl.BlockSpec(memory_space=pltpu.VMEM)],
        out_specs=pl.BlockSpec(memory_space=pltpu.VMEM),
    )(parts)


def norm_fwd(h, g, name):
    seq, d = h.shape
    tm = min(ROW_TILE, seq)

    def body(h_ref, g_ref, o_ref):
        o_ref[...] = _rms(h_ref[...], g_ref[...])[0].astype(BF16)

    return _call(body, name, (seq // tm,),
                 [pl.BlockSpec((tm, d), lambda i: (i, 0)), pl.BlockSpec((1, d), lambda i: (0, 0))],
                 [pl.BlockSpec((tm, d), lambda i: (i, 0))],
                 [jax.ShapeDtypeStruct((seq, d), BF16)], (h, g))[0][0]


def proj_residual(a, w, res, name, g_next=None, rider=()):
    nb, seq, kb = a.shape
    d = w.shape[-1]
    tm = min(ROW_TILE, seq)
    with_norm = g_next is not None

    def body(a_ref, w_ref, r_ref, *rest):
        acc = r_ref[...]
        for b in range(nb):
            acc = acc + _nn(a_ref[b], w_ref[b])
        if with_norm:
            g_ref, o_ref, xn_ref = rest
            xn_ref[...] = _rms(acc, g_ref[...])[0].astype(BF16)
        else:
            (o_ref,) = rest
        o_ref[...] = acc

    row = pl.BlockSpec((tm, d), lambda i: (i, 0))
    in_specs = [pl.BlockSpec((nb, tm, kb), lambda i: (0, i, 0)),
                pl.BlockSpec((nb, kb, d), lambda i: (0, 0, 0)), row]
    args = [a, w, res]
    out_specs, out_shape = [row], [jax.ShapeDtypeStruct((seq, d), F32)]
    if with_norm:
        in_specs.append(pl.BlockSpec((1, d), lambda i: (0, 0)))
        args.append(g_next)
        out_specs.append(row)
        out_shape.append(jax.ShapeDtypeStruct((seq, d), BF16))
    outs, r_outs = _call(body, name, (seq // tm,), in_specs, out_specs, out_shape, args, rider=rider)
    return (outs[0], outs[1] if with_norm else None), r_outs


def proj_t_rms_bwd(du, w, h, g, dres, name, rider=()):
    nb, seq, wd = du.shape
    k = w.shape[1]
    big_weight = 2 * w.size * w.dtype.itemsize > V7X_VMEM_LIMIT // 4
    tm = min(ROW_TILE // 2 if big_weight else ROW_TILE, seq)

    def body(du_ref, w_ref, h_ref, g_ref, dr_ref, dh_ref, dg_ref):
        i = pl.program_id(0)
        dxn = _nt(du_ref[0], w_ref[0])
        for b in range(1, nb):
            dxn = dxn + _nt(du_ref[b], w_ref[b])
        _, xhat, rstd = _rms(h_ref[...], g_ref[...])
        dh, dg_rows = _rms_bwd(dxn, xhat, rstd, g_ref[...])
        dh_ref[...] = dr_ref[...] + dh

        @pl.when(i == 0)
        def _():
            dg_ref[...] = jnp.zeros_like(dg_ref)
        dg_ref[...] += jnp.sum(dg_rows, axis=0, keepdims=True)

    row = pl.BlockSpec((tm, k), lambda i: (i, 0))
    vec = pl.BlockSpec((1, k), lambda i: (0, 0))
    return _call(body, name, (seq // tm,),
                 [pl.BlockSpec((nb, tm, wd), lambda i: (0, i, 0)),
                  pl.BlockSpec((nb, k, wd), lambda i: (0, 0, 0)), row, vec, row],
                 (row, vec),
                 (jax.ShapeDtypeStruct((seq, k), F32), jax.ShapeDtypeStruct((1, k), F32)),
                 (du, w, h, g, dres), rider=rider)


def mixer_fwd(xn, win3, cw, name, rider=()):
    seq, d = xn.shape
    tm = min(ROW_TILE_LARGE, seq)
    cc = min(MIXER_CHUNK, d)
    n_c, n_i = d // cc, seq // tm

    def body(x_ref, w_ref, cw_ref, u_ref, z_ref, carry):
        i = pl.program_id(1)

        @pl.when(i == 0)
        def _():
            carry[...] = jnp.zeros_like(carry)
        xb = x_ref[...]
        b = _nn(xb, w_ref[0])
        c = _nn(xb, w_ref[1])
        hh = _nn(xb, w_ref[2])
        p = c * hh
        w0, w1, w2 = _conv_taps(cw_ref)
        p1 = _shift_down(p, 1, [carry[7:8, :]])
        p2 = _shift_down(p, 2, [carry[6:7, :], carry[7:8, :]])
        q = w0 * p2 + w1 * p1 + w2 * p
        carry[...] = p[tm - 8:tm, :]
        u_ref[0] = b.astype(BF16)
        u_ref[1] = c.astype(BF16)
        u_ref[2] = hh.astype(BF16)
        u_ref[3] = q.astype(BF16)
        z_ref[...] = (b * q).astype(BF16)

    return _call(body, name, (n_c, n_i),
                 [pl.BlockSpec((tm, d), lambda c, i: (i, 0)),
                  pl.BlockSpec((3, d, cc), lambda c, i: (0, 0, c)),
                  pl.BlockSpec((3, cc), lambda c, i: (0, c))],
                 (pl.BlockSpec((4, tm, cc), lambda c, i: (0, i, c)),
                  pl.BlockSpec((tm, cc), lambda c, i: (i, c))),
                 (jax.ShapeDtypeStruct((4, seq, d), BF16), jax.ShapeDtypeStruct((seq, d), BF16)),
                 (xn, win3, cw), scratch=[pltpu.VMEM((8, cc), F32)], rider=rider)


def mixer_bwd(dh, wout, u4, z, xn, cw, name, rider=()):
    seq, d = xn.shape
    tm = min(ROW_TILE, seq)
    cc = min(MIXER_CHUNK, d)
    n_c, n_i = d // cc, seq // tm

    def body(dh_ref, wout_ref, u_ref, z_ref, x_ref, cw_ref,
             du_ref, dwin_ref, dwout_ref, dcw_ref, acc_in, acc_out, acc_cw, carry):
        i = pl.program_id(1)

        @pl.when(i == 0)
        def _():
            acc_in[...] = jnp.zeros_like(acc_in)
            acc_out[...] = jnp.zeros_like(acc_out)
            acc_cw[...] = jnp.zeros_like(acc_cw)
            carry[...] = jnp.zeros_like(carry)
        dhb = dh_ref[...].astype(BF16)
        dz = _nt(dhb, wout_ref[...])
        acc_out[...] += _tn(z_ref[...], dhb)
        b = u_ref[0].astype(F32)
        c = u_ref[1].astype(F32)
        hh = u_ref[2].astype(F32)
        q = u_ref[3].astype(F32)
        p = c * hh
        db = dz * q
        dq = dz * b
        w0, w1, w2 = _conv_taps(cw_ref)
        dq1 = _shift_up(dq, 1, [carry[0:1, :]])
        dq2 = _shift_up(dq, 2, [carry[0:1, :], carry[1:2, :]])
        dp = w2 * dq + w1 * dq1 + w0 * dq2
        carry[...] = dq[0:8, :]
        acc_cw[0:1, :] += jnp.sum(dq2 * p, axis=0, keepdims=True)
        acc_cw[1:2, :] += jnp.sum(dq1 * p, axis=0, keepdims=True)
        acc_cw[2:3, :] += jnp.sum(dq * p, axis=0, keepdims=True)
        dbb = db.astype(BF16)
        dcb = (dp * hh).astype(BF16)
        dhhb = (dp * c).astype(BF16)
        du_ref[0] = dbb
        du_ref[1] = dcb
        du_ref[2] = dhhb
        xb = x_ref[...]
        acc_in[0] += _tn(xb, dbb)
        acc_in[1] += _tn(xb, dcb)
        acc_in[2] += _tn(xb, dhhb)

        @pl.when(i == n_i - 1)
        def _():
            dwin_ref[...] = acc_in[...].astype(BF16)
            dwout_ref[...] = acc_out[...].astype(BF16)
            dcw_ref[...] = acc_cw[0:3, :]

    rev = lambda c, i: (n_i - 1 - i, 0)
    return _call(body, name, (n_c, n_i),
                 [pl.BlockSpec((tm, d), rev),
                  pl.BlockSpec((cc, d), lambda c, i: (c, 0)),
                  pl.BlockSpec((4, tm, cc), lambda c, i: (0, n_i - 1 - i, c)),
                  pl.BlockSpec((tm, cc), lambda c, i: (n_i - 1 - i, c)),
                  pl.BlockSpec((tm, d), rev),
                  pl.BlockSpec((3, cc), lambda c, i: (0, c))],
                 (pl.BlockSpec((3, tm, cc), lambda c, i: (0, n_i - 1 - i, c)),
                  pl.BlockSpec((3, d, cc), lambda c, i: (0, 0, c)),
                  pl.BlockSpec((cc, d), lambda c, i: (c, 0)),
                  pl.BlockSpec((3, cc), lambda c, i: (0, c))),
                 (jax.ShapeDtypeStruct((3, seq, d), BF16), jax.ShapeDtypeStruct((3, d, d), BF16),
                  jax.ShapeDtypeStruct((d, d), BF16), jax.ShapeDtypeStruct((3, d), F32)),
                 (dh, wout, u4, z, xn, cw),
                 scratch=[pltpu.VMEM((3, d, cc), F32), pltpu.VMEM((cc, d), F32),
                          pltpu.VMEM((8, cc), F32), pltpu.VMEM((8, cc), F32)], rider=rider)


def _silu_parts(cg):
    sg = 1.0 / (1.0 + jnp.exp(-cg))
    return sg, cg * sg


def ffn_fwd(xn, wup, fcw, name, rider=()):
    seq, d = xn.shape
    f8 = wup.shape[-1]
    half = N_DEV // 2
    tm = min(ROW_TILE_LARGE, seq)
    n_i = seq // tm

    def body(x_ref, wg_ref, wu_ref, cg_ref, cu_ref, up_ref, cv_ref, a_ref, carry):
        i = pl.program_id(1)

        @pl.when(i == 0)
        def _():
            carry[...] = jnp.zeros_like(carry)
        xb = x_ref[...]
        conv = []
        for s, (w_ref, t_ref) in enumerate(((wg_ref, cg_ref), (wu_ref, cu_ref))):
            u = _nn(xb, w_ref[...])
            up_ref[s] = u.astype(BF16)
            w0, w1, w2 = _conv_taps(t_ref)
            u1 = _shift_down(u, 1, [carry[s, 7:8, :]])
            u2 = _shift_down(u, 2, [carry[s, 6:7, :], carry[s, 7:8, :]])
            cv = w0 * u2 + w1 * u1 + w2 * u
            cv_ref[s] = cv.astype(BF16)
            conv.append(cv)
            carry[s] = u[tm - 8:tm, :]
        _, silu = _silu_parts(conv[0])
        a_ref[...] = (silu * conv[1]).astype(BF16)

    blk = pl.BlockSpec((2, None, tm, f8), lambda c, i: (0, c, i, 0))
    big = jax.ShapeDtypeStruct((2, half, seq, f8), BF16)
    return _call(body, name, (half, n_i),
                 [pl.BlockSpec((tm, d), lambda c, i: (i, 0)),
                  pl.BlockSpec((None, d, f8), lambda c, i: (c, 0, 0)),
                  pl.BlockSpec((None, d, f8), lambda c, i: (c + half, 0, 0)),
                  pl.BlockSpec((None, 3, f8), lambda c, i: (c, 0, 0)),
                  pl.BlockSpec((None, 3, f8), lambda c, i: (c + half, 0, 0))],
                 (blk, blk, pl.BlockSpec((None, tm, f8), lambda c, i: (c, i, 0))),
                 (big, big, jax.ShapeDtypeStruct((half, seq, f8), BF16)),
                 (xn, wup, wup, fcw, fcw), scratch=[pltpu.VMEM((2, 8, f8), F32)], rider=rider)


def ffn_bwd(dh, wdown, up2, cv2, act, xn, fcw, name, rider=()):
    seq, d = xn.shape
    f8 = up2.shape[-1]
    fb = wdown.shape[1]
    half = N_DEV // 2
    tm = min(ROW_TILE, seq)
    n_i = seq // tm

    def body(dh_ref, wd_ref, up_ref, cv_ref, a_ref, x_ref, cg_ref, cu_ref,
             dup_ref, dwup_ref, dwd_ref, dcw_ref, acc_up, acc_down, acc_cw, carry):
        i = pl.program_id(1)

        @pl.when(i == 0)
        def _():
            acc_up[...] = jnp.zeros_like(acc_up)
            acc_down[...] = jnp.zeros_like(acc_down)
            acc_cw[...] = jnp.zeros_like(acc_cw)
            carry[...] = jnp.zeros_like(carry)
        dhb = dh_ref[...].astype(BF16)
        da = _nt(dhb, wd_ref[...])
        acc_down[...] += _tn(a_ref[...], dhb)
        cg = cv_ref[0].astype(F32)
        cu = cv_ref[1].astype(F32)
        sg, silu = _silu_parts(cg)
        dcg = da * cu * (sg + silu * (1.0 - sg))
        dcu = da * silu
        xb = x_ref[...]
        for s, (dc, t_ref) in enumerate(((dcg, cg_ref), (dcu, cu_ref))):
            w0, w1, w2 = _conv_taps(t_ref)
            d1 = _shift_up(dc, 1, [carry[s, 0:1, :]])
            d2 = _shift_up(dc, 2, [carry[s, 0:1, :], carry[s, 1:2, :]])
            du = (w2 * dc + w1 * d1 + w0 * d2).astype(BF16)
            carry[s] = dc[0:8, :]
            u = up_ref[s].astype(F32)
            acc_cw[s, 0:1, :] += jnp.sum(d2 * u, axis=0, keepdims=True)
            acc_cw[s, 1:2, :] += jnp.sum(d1 * u, axis=0, keepdims=True)
            acc_cw[s, 2:3, :] += jnp.sum(dc * u, axis=0, keepdims=True)
            dup_ref[s] = du
            acc_up[s] += _tn(xb, du)

        @pl.when(i == n_i - 1)
        def _():
            dwup_ref[...] = acc_up[...].astype(BF16)
            dwd_ref[...] = acc_down[...].astype(BF16)
            dcw_ref[...] = acc_cw[:, 0:3, :]

    rev = lambda c, i: (n_i - 1 - i, 0)
    blk = pl.BlockSpec((2, None, tm, f8), lambda c, i: (0, c, n_i - 1 - i, 0))
    return _call(body, name, (half, n_i),
                 [pl.BlockSpec((tm, d), rev),
                  pl.BlockSpec((None, fb, d), lambda c, i: (c, 0, 0)),
                  blk, blk,
                  pl.BlockSpec((None, tm, f8), lambda c, i: (c, n_i - 1 - i, 0)),
                  pl.BlockSpec((tm, d), rev),
                  pl.BlockSpec((None, 3, f8), lambda c, i: (c, 0, 0)),
                  pl.BlockSpec((None, 3, f8), lambda c, i: (c + half, 0, 0))],
                 (blk,
                  pl.BlockSpec((2, None, d, f8), lambda c, i: (0, c, 0, 0)),
                  pl.BlockSpec((None, fb, d), lambda c, i: (c, 0, 0)),
                  pl.BlockSpec((2, None, 3, f8), lambda c, i: (0, c, 0, 0))),
                 (jax.ShapeDtypeStruct((2, half, seq, f8), BF16),
                  jax.ShapeDtypeStruct((2, half, d, f8), BF16),
                  jax.ShapeDtypeStruct((half, fb, d), BF16),
                  jax.ShapeDtypeStruct((2, half, 3, f8), F32)),
                 (dh, wdown, up2, cv2, act, xn, fcw, fcw),
                 scratch=[pltpu.VMEM((2, d, f8), F32), pltpu.VMEM((fb, d), F32),
                          pltpu.VMEM((2, 8, f8), F32), pltpu.VMEM((2, 8, f8), F32)], rider=rider)


def q_fwd(xn, wdq, gq, wuq, cos, sin, name, rider=()):
    seq, d = xn.shape
    rank = wdq.shape[-1]
    tm = min(ROW_TILE, seq)

    def body(x_ref, wdq_ref, gq_ref, wuq_ref, cos_ref, sin_ref, q_ref):
        qc = _nn(x_ref[...], wdq_ref[...])
        qn = _rms(qc, gq_ref[...])[0].astype(BF16)
        for hd in range(N_HEADS):
            qh = _nn(qn, wuq_ref[hd])
            qr = _rope_fwd(qh[:, NOPE:QK], cos_ref[...], sin_ref[...])
            q_ref[hd, :, 0:NOPE] = (qh[:, 0:NOPE] * ATTN_SCALE).astype(BF16)
            q_ref[hd, :, NOPE:QK] = (qr * ATTN_SCALE).astype(BF16)

    rope = pl.BlockSpec((tm, ROPE_PAD), lambda i: (i, 0))
    return _call(body, name, (seq // tm,),
                 [pl.BlockSpec((tm, d), lambda i: (i, 0)),
                  pl.BlockSpec((d, rank), lambda i: (0, 0)),
                  pl.BlockSpec((1, rank), lambda i: (0, 0)),
                  pl.BlockSpec((N_HEADS, rank, QK), lambda i: (0, 0, 0)), rope, rope],
                 [pl.BlockSpec((N_HEADS, tm, QK), lambda i: (0, i, 0))],
                 [jax.ShapeDtypeStruct((N_HEADS, seq, QK), BF16)],
                 (xn, wdq, gq, wuq, cos, sin), rider=rider)


def q_bwd(dq, xn, h, g, dres, wdq, gq, wuq, cos, sin, name, rider=()):
    seq, d = xn.shape
    rank = wdq.shape[-1]
    tm = min(ROW_TILE, seq)
    n_i = seq // tm

    def body(dq_ref, x_ref, h_ref, g_ref, dr_ref, wdq_ref, gq_ref, wuq_ref, cos_ref, sin_ref,
             dh_ref, dwuq_ref, dwdq_ref, dgq_ref, dg_ref, acc_uq, acc_dq):
        i = pl.program_id(0)

        @pl.when(i == 0)
        def _():
            acc_uq[...] = jnp.zeros_like(acc_uq)
            acc_dq[...] = jnp.zeros_like(acc_dq)
            dgq_ref[...] = jnp.zeros_like(dgq_ref)
            dg_ref[...] = jnp.zeros_like(dg_ref)
        xb = x_ref[...]
        qc = _nn(xb, wdq_ref[...])
        qn, qhat, qrstd = _rms(qc, gq_ref[...])
        qnb = qn.astype(BF16)
        dqn = jnp.zeros((tm, rank), F32)
        for hd in range(N_HEADS):
            dnope = (dq_ref[hd, :, 0:NOPE].astype(F32) * ATTN_SCALE).astype(BF16)
            drope = _rope_bwd(dq_ref[hd, :, NOPE:QK].astype(F32) * ATTN_SCALE, cos_ref[...], sin_ref[...])
            draw = jnp.concatenate([dnope, drope.astype(BF16)], axis=1)
            dqn = dqn + _nt(draw, wuq_ref[hd])
            acc_uq[hd] += _tn(qnb, draw)
        dqc, dg_rows = _rms_bwd(dqn, qhat, qrstd, gq_ref[...])
        dgq_ref[...] += jnp.sum(dg_rows, axis=0, keepdims=True)
        dqcb = dqc.astype(BF16)
        acc_dq[...] += _tn(xb, dqcb)
        _, xhat, rstd = _rms(h_ref[...], g_ref[...])
        dh, dg_rows = _rms_bwd(_nt(dqcb, wdq_ref[...]), xhat, rstd, g_ref[...])
        dh_ref[...] = dr_ref[...] + dh
        dg_ref[...] += jnp.sum(dg_rows, axis=0, keepdims=True)

        @pl.when(i == n_i - 1)
        def _():
            dwuq_ref[...] = acc_uq[...].astype(BF16)
            dwdq_ref[...] = acc_dq[...].astype(BF16)

    rope = pl.BlockSpec((tm, ROPE_PAD), lambda i: (i, 0))
    row = pl.BlockSpec((tm, d), lambda i: (i, 0))
    vec = pl.BlockSpec((1, d), lambda i: (0, 0))
    return _call(body, name, (n_i,),
                 [pl.BlockSpec((N_HEADS, tm, QK), lambda i: (0, i, 0)), row, row, vec, row,
                  pl.BlockSpec((d, rank), lambda i: (0, 0)),
                  pl.BlockSpec((1, rank), lambda i: (0, 0)),
                  pl.BlockSpec((N_HEADS, rank, QK), lambda i: (0, 0, 0)), rope, rope],
                 (row,
                  pl.BlockSpec((N_HEADS, rank, QK), lambda i: (0, 0, 0)),
                  pl.BlockSpec((d, rank), lambda i: (0, 0)),
                  pl.BlockSpec((1, rank), lambda i: (0, 0)), vec),
                 (jax.ShapeDtypeStruct((seq, d), F32),
                  jax.ShapeDtypeStruct((N_HEADS, rank, QK), BF16),
                  jax.ShapeDtypeStruct((d, rank), BF16),
                  jax.ShapeDtypeStruct((1, rank), F32),
                  jax.ShapeDtypeStruct((1, d), F32)),
                 (dq, xn, h, g, dres, wdq, gq, wuq, cos, sin),
                 scratch=[pltpu.VMEM((N_HEADS, rank, QK), F32), pltpu.VMEM((d, rank), F32)],
                 rider=rider)


def kv_fwd(h, g, wdkv, gkv, wukv, cos, sin, name, rider=()):
    seq, d = h.shape
    tm = min(ROW_TILE, seq)
    wk = KV_RANK + ROPE_PAD

    def body(h_ref, g_ref, wdkv_ref, gkv_ref, wukv_ref, cos_ref, sin_ref, k_ref, v_ref, c_ref):
        xk = _rms(h_ref[...], g_ref[...])[0].astype(BF16)
        ckv = _nn(xk, wdkv_ref[...])
        c_kv = ckv[:, 0:KV_RANK]
        c_ref[...] = c_kv
        kr = _rope_fwd(ckv[:, KV_RANK:wk], cos_ref[...], sin_ref[...]).astype(BF16)
        ckn = _rms(c_kv, gkv_ref[...])[0].astype(BF16)
        for hd in range(N_HEADS):
            kvh = _nn(ckn, wukv_ref[hd])
            k_ref[hd, :, 0:NOPE] = kvh[:, 0:NOPE].astype(BF16)
            k_ref[hd, :, NOPE:QK] = kr
            v_ref[hd] = kvh[:, NOPE:NOPE + VDIM].astype(BF16)

    rope = pl.BlockSpec((tm, ROPE_PAD), lambda i: (i, 0))
    return _call(body, name, (seq // tm,),
                 [pl.BlockSpec((tm, d), lambda i: (i, 0)),
                  pl.BlockSpec((1, d), lambda i: (0, 0)),
                  pl.BlockSpec((d, wk), lambda i: (0, 0)),
                  pl.BlockSpec((1, KV_RANK), lambda i: (0, 0)),
                  pl.BlockSpec((N_HEADS, KV_RANK, NOPE + VDIM), lambda i: (0, 0, 0)), rope, rope],
                 (pl.BlockSpec((N_HEADS, tm, QK), lambda i: (0, i, 0)),
                  pl.BlockSpec((N_HEADS, tm, VDIM), lambda i: (0, i, 0)),
                  pl.BlockSpec((tm, KV_RANK), lambda i: (i, 0))),
                 (jax.ShapeDtypeStruct((N_HEADS, seq, QK), BF16),
                  jax.ShapeDtypeStruct((N_HEADS, seq, VDIM), BF16),
                  jax.ShapeDtypeStruct((seq, KV_RANK), F32)),
                 (h, g, wdkv, gkv, wukv, cos, sin), rider=rider)


def kv_bwd(dks, dvs, c_kv, h, g, dres, wdkv, gkv, wukv, cos, sin, name, rider=()):
    seq, d = h.shape
    tm = min(ROW_TILE, seq)
    n_i = seq // tm
    wk = KV_RANK + ROPE_PAD
    n_b = len(dks)

    def body(*refs):
        dk_refs = refs[:n_b]
        dv_refs = refs[n_b:2 * n_b]
        (c_ref, h_ref, g_ref, dr_ref, wdkv_ref, gkv_ref, wukv_ref, cos_ref, sin_ref,
         dh_ref, dwukv_ref, dwdkv_ref, dgkv_ref, dg_ref, acc_ukv, acc_dkv) = refs[2 * n_b:]
        i = pl.program_id(0)

        @pl.when(i == 0)
        def _():
            acc_ukv[...] = jnp.zeros_like(acc_ukv)
            acc_dkv[...] = jnp.zeros_like(acc_dkv)
            dgkv_ref[...] = jnp.zeros_like(dgkv_ref)
            dg_ref[...] = jnp.zeros_like(dg_ref)
        ckn, chat, crstd = _rms(c_ref[...], gkv_ref[...])
        cknb = ckn.astype(BF16)
        dckn = jnp.zeros((tm, KV_RANK), F32)
        dkr = jnp.zeros((tm, ROPE_PAD), F32)
        for hd in range(N_HEADS):
            dk = dk_refs[0][hd].astype(F32)
            dv = dv_refs[0][hd].astype(F32)
            for j in range(1, n_b):
                dk = dk + dk_refs[j][hd].astype(F32)
                dv = dv + dv_refs[j][hd].astype(F32)
            dkr = dkr + dk[:, NOPE:QK]
            dkvh = jnp.concatenate([dk[:, 0:NOPE].astype(BF16), dv.astype(BF16)], axis=1)
            dckn = dckn + _nt(dkvh, wukv_ref[hd])
            acc_ukv[hd] += _tn(cknb, dkvh)
        dc_kv, dg_rows = _rms_bwd(dckn, chat, crstd, gkv_ref[...])
        dgkv_ref[...] += jnp.sum(dg_rows, axis=0, keepdims=True)
        dkr_raw = _rope_bwd(dkr, cos_ref[...], sin_ref[...])
        dckv = jnp.concatenate([dc_kv.astype(BF16), dkr_raw.astype(BF16)], axis=1)
        xk, xhat, rstd = _rms(h_ref[...], g_ref[...])
        acc_dkv[...] += _tn(xk.astype(BF16), dckv)
        dh, dg_rows = _rms_bwd(_nt(dckv, wdkv_ref[...]), xhat, rstd, g_ref[...])
        dh_ref[...] = dr_ref[...] + dh
        dg_ref[...] += jnp.sum(dg_rows, axis=0, keepdims=True)

        @pl.when(i == n_i - 1)
        def _():
            dwukv_ref[...] = acc_ukv[...].astype(BF16)
            dwdkv_ref[...] = acc_dkv[...].astype(BF16)

    kspec = pl.BlockSpec((N_HEADS, tm, QK), lambda i: (0, i, 0))
    vspec = pl.BlockSpec((N_HEADS, tm, VDIM), lambda i: (0, i, 0))
    rope = pl.BlockSpec((tm, ROPE_PAD), lambda i: (i, 0))
    row = pl.BlockSpec((tm, d), lambda i: (i, 0))
    vec = pl.BlockSpec((1, d), lambda i: (0, 0))
    return _call(body, name, (n_i,),
                 [kspec] * n_b + [vspec] * n_b + [
                     pl.BlockSpec((tm, KV_RANK), lambda i: (i, 0)), row, vec, row,
                     pl.BlockSpec((d, wk), lambda i: (0, 0)),
                     pl.BlockSpec((1, KV_RANK), lambda i: (0, 0)),
                     pl.BlockSpec((N_HEADS, KV_RANK, NOPE + VDIM), lambda i: (0, 0, 0)), rope, rope],
                 (row,
                  pl.BlockSpec((N_HEADS, KV_RANK, NOPE + VDIM), lambda i: (0, 0, 0)),
                  pl.BlockSpec((d, wk), lambda i: (0, 0)),
                  pl.BlockSpec((1, KV_RANK), lambda i: (0, 0)), vec),
                 (jax.ShapeDtypeStruct((seq, d), F32),
                  jax.ShapeDtypeStruct((N_HEADS, KV_RANK, NOPE + VDIM), BF16),
                  jax.ShapeDtypeStruct((d, wk), BF16),
                  jax.ShapeDtypeStruct((1, KV_RANK), F32),
                  jax.ShapeDtypeStruct((1, d), F32)),
                 (*dks, *dvs, c_kv, h, g, dres, wdkv, gkv, wukv, cos, sin),
                 scratch=[pltpu.VMEM((N_HEADS, KV_RANK, NOPE + VDIM), F32), pltpu.VMEM((d, wk), F32)],
                 rider=rider)


def o_bwd(dh, o, wo, name, rider=()):
    seq, d = dh.shape
    hv = o.shape[1]
    tm = min(ROW_TILE, seq)
    n_i = seq // tm

    def body(dh_ref, o_ref, wo_ref, do_ref, dwo_ref, acc):
        i = pl.program_id(0)

        @pl.when(i == 0)
        def _():
            acc[...] = jnp.zeros_like(acc)
        dhb = dh_ref[...].astype(BF16)
        do_ref[...] = _nt(dhb, wo_ref[...]).astype(BF16)
        acc[...] += _tn(o_ref[...], dhb)

        @pl.when(i == n_i - 1)
        def _():
            dwo_ref[...] = acc[...].astype(BF16)

    return _call(body, name, (n_i,),
                 [pl.BlockSpec((tm, d), lambda i: (i, 0)),
                  pl.BlockSpec((tm, hv), lambda i: (i, 0)),
                  pl.BlockSpec((hv, d), lambda i: (0, 0))],
                 (pl.BlockSpec((tm, hv), lambda i: (i, 0)),
                  pl.BlockSpec((hv, d), lambda i: (0, 0))),
                 (jax.ShapeDtypeStruct((seq, hv), BF16), jax.ShapeDtypeStruct((hv, d), BF16)),
                 (dh, o, wo), scratch=[pltpu.VMEM((hv, d), F32)], rider=rider)


def _mask_diagonal(s):
    row = lax.broadcasted_iota(jnp.int32, s.shape, 0)
    col = lax.broadcasted_iota(jnp.int32, s.shape, 1)
    return jnp.where(col <= row, s, NEG_BIG)


def attn_fwd(q, k, v, name, rider=()):
    _, seq, _ = q.shape
    t = min(ATTN_TILE, seq // 2)
    n_pair = seq // (2 * t)

    def body(q_ref, k_ref, v_ref, o_ref, lse_ref):
        qi = pl.program_id(1)
        q_a = q_ref[0:t, :]
        q_b = q_ref[t:2 * t, :]

        def rows(j):
            return pl.ds(pl.multiple_of(j * t, t), t)

        def update(qx, kb, vb, state, diagonal=False):
            m, l, acc = state
            s = _nt(qx, kb)
            if diagonal:
                s = _mask_diagonal(s)
            m_new = jnp.maximum(m, jnp.max(s, axis=1, keepdims=True))
            p = jnp.exp(s - m_new)
            alpha = jnp.exp(m - m_new)
            l = alpha * l + jnp.sum(p, axis=1, keepdims=True)
            acc = alpha * acc + _nn(p.astype(BF16), vb)
            return m_new, l, acc

        def step(j, carry):
            both = pl.ds(pl.multiple_of(j * 2 * t, 2 * t), 2 * t)
            kb, vb = k_ref[both, :], v_ref[both, :]
            return update(q_a, kb, vb, carry[0:3]) + update(q_b, kb, vb, carry[3:6])

        init = (jnp.full((t, 1), NEG_BIG, F32), jnp.zeros((t, 1), F32), jnp.zeros((t, VDIM), F32))
        carry = lax.fori_loop(0, qi, step, init + init)
        k0, v0 = k_ref[rows(2 * qi), :], v_ref[rows(2 * qi), :]
        k1, v1 = k_ref[rows(2 * qi + 1), :], v_ref[rows(2 * qi + 1), :]
        state_a = update(q_a, k0, v0, carry[0:3], diagonal=True)
        state_b = update(q_b, k1, v1, update(q_b, k0, v0, carry[3:6]), diagonal=True)
        for half, (m, l, acc) in enumerate((state_a, state_b)):
            o_ref[half * t:(half + 1) * t, :] = (acc / l).astype(BF16)
            lse_ref[half * t:(half + 1) * t, :] = jnp.broadcast_to(m + jnp.log(l), (t, LANES))

    return _call(body, name, (N_HEADS, n_pair),
                 [pl.BlockSpec((None, 2 * t, QK), lambda h, i: (h, i, 0)),
                  pl.BlockSpec((None, seq, QK), lambda h, i: (h, 0, 0)),
                  pl.BlockSpec((None, seq, VDIM), lambda h, i: (h, 0, 0))],
                 (pl.BlockSpec((2 * t, VDIM), lambda h, i: (i, h)),
                  pl.BlockSpec((None, 2 * t, LANES), lambda h, i: (h, i, 0))),
                 (jax.ShapeDtypeStruct((seq, N_HEADS * VDIM), BF16),
                  jax.ShapeDtypeStruct((N_HEADS, seq, LANES), F32)),
                 (q, k, v), rider=rider)


def attn_bwd(q, k, v, o, do, lse, name, rider=()):
    _, seq, _ = q.shape
    t = min(ATTN_TILE, seq // 2)
    n_q = seq // t
    n_pair = n_q // 2

    def body(q_ref, k_ref, v_ref, o_ref, do_ref, lse_ref, dq_ref, dk_ref, dv_ref,
             dq_acc, dk_acc, dv_acc):
        kj = pl.program_id(1)

        @pl.when(kj == 0)
        def _():
            dq_acc[...] = jnp.zeros_like(dq_acc)
        halves = (slice(0, t), slice(t, 2 * t))

        def block(i, masks, n_rows=t):
            rows = pl.ds(pl.multiple_of(i * n_rows, n_rows), n_rows)
            qb = q_ref[rows, :]
            dob = do_ref[rows, :]
            lse_col = lse_ref[rows, 0:1]
            delta = jnp.sum(dob.astype(F32) * o_ref[rows, :].astype(F32), axis=1, keepdims=True)
            dq, out = None, {}
            for x, diagonal in enumerate(masks):
                if diagonal is None:
                    continue
                kb, vb = k_ref[halves[x], :], v_ref[halves[x], :]
                s = _nt(qb, kb)
                if diagonal:
                    s = _mask_diagonal(s)
                p = jnp.exp(s - lse_col)
                ds = (p * (_nt(dob, vb) - delta)).astype(BF16)
                out[x] = (_tn(p.astype(BF16), dob), _tn(ds, qb))
                part = _nn(ds, kb)
                dq = part if dq is None else dq + part
            dq_acc[rows, :] += dq
            return out

        first = block(2 * kj, (True, None))
        second = block(2 * kj + 1, (False, True))
        dv_acc[halves[0], :] = first[0][0] + second[0][0]
        dk_acc[halves[0], :] = first[0][1] + second[0][1]
        dv_acc[halves[1], :] = second[1][0]
        dk_acc[halves[1], :] = second[1][1]

        def step(i, carry):
            out = block(i, (False, False), n_rows=2 * t)
            for x in (0, 1):
                dv_acc[halves[x], :] += out[x][0]
                dk_acc[halves[x], :] += out[x][1]
            return carry

        lax.fori_loop(kj + 1, n_pair, step, 0)
        dk_ref[...] = dk_acc[...].astype(BF16)
        dv_ref[...] = dv_acc[...].astype(BF16)

        @pl.when(kj == n_pair - 1)
        def _():
            dq_ref[...] = dq_acc[...].astype(BF16)

    head_rows = pl.BlockSpec((seq, VDIM), lambda h, j: (0, h))
    return _call(body, name, (N_HEADS, n_pair),
                 [pl.BlockSpec((None, seq, QK), lambda h, j: (h, 0, 0)),
                  pl.BlockSpec((None, 2 * t, QK), lambda h, j: (h, j, 0)),
                  pl.BlockSpec((None, 2 * t, VDIM), lambda h, j: (h, j, 0)),
                  head_rows, head_rows,
                  pl.BlockSpec((None, seq, LANES), lambda h, j: (h, 0, 0))],
                 (pl.BlockSpec((None, seq, QK), lambda h, j: (h, 0, 0)),
                  pl.BlockSpec((None, 2 * t, QK), lambda h, j: (h, j, 0)),
                  pl.BlockSpec((None, 2 * t, VDIM), lambda h, j: (h, j, 0))),
                 (jax.ShapeDtypeStruct((N_HEADS, seq, QK), BF16),
                  jax.ShapeDtypeStruct((N_HEADS, seq, QK), BF16),
                  jax.ShapeDtypeStruct((N_HEADS, seq, VDIM), BF16)),
                 (q, k, v, o, do, lse),
                 scratch=[pltpu.VMEM((seq, QK), F32), pltpu.VMEM((2 * t, QK), F32),
                          pltpu.VMEM((2 * t, VDIM), F32)], rider=rider)


def loss_head(h, g, target, name):
    seq, d = h.shape
    tm = min(ROW_TILE, seq)

    def body(h_ref, g_ref, t_ref, l_ref, dh_ref, dg_ref):
        i = pl.program_id(0)

        @pl.when(i == 0)
        def _():
            l_ref[...] = jnp.zeros_like(l_ref)
            dg_ref[...] = jnp.zeros_like(dg_ref)
        y, xhat, rstd = _rms(h_ref[...], g_ref[...])
        diff = y - t_ref[...]
        l_ref[...] += jnp.sum(jnp.sum(diff * diff, axis=1, keepdims=True), axis=0, keepdims=True)
        dh, dg_rows = _rms_bwd(diff * (1.0 / d), xhat, rstd, g_ref[...])
        dh_ref[...] = dh
        dg_ref[...] += jnp.sum(dg_rows, axis=0, keepdims=True)

    row = pl.BlockSpec((tm, d), lambda i: (i, 0))
    vec = pl.BlockSpec((1, d), lambda i: (0, 0))
    return _call(body, name, (seq // tm,), [row, vec, row],
                 (pl.BlockSpec((1, LANES), lambda i: (0, 0)), row, vec),
                 (jax.ShapeDtypeStruct((1, LANES), F32), jax.ShapeDtypeStruct((seq, d), F32),
                  jax.ShapeDtypeStruct((1, d), F32)),
                 (h, g, target))[0]


def _pack(parts):
    rows = []
    for p in parts:
        flat = p.reshape(-1)
        n_rows = -(-flat.shape[0] // (8 * LANES)) * 8
        flat = jnp.pad(flat, (0, n_rows * LANES - flat.shape[0]))
        rows.append(flat.reshape(n_rows, LANES))
    return jnp.concatenate(rows, axis=0)


def _unpack(packed, shapes):
    lead = packed.shape[:-2]
    out, r0 = [], 0
    for shape in shapes:
        size = 1
        for s in shape:
            size *= s
        n_rows = -(-size // (8 * LANES)) * 8
        part = packed[..., r0:r0 + n_rows, :].reshape(lead + (n_rows * LANES,))
        out.append(part[..., :size].reshape(lead + tuple(shape)))
        r0 += n_rows
    return out


FWD_RIDERS = {
    "mixer_fwd0": [("ffn_w_up", 0)],
    "ffn_fwd0": [("ffn_w_down", 0), ("a_w_in", 1), ("a_w_out", 1)],
    "mixer_fwd1": [("ffn_w_up", 1)],
    "ffn_fwd1": [("ffn_w_down", 1), ("w_dkv", 0), ("w_ukv", 0), ("b_w_dq", 0), ("b_w_uq", 0)],
    "attn_fwd0": [("b_w_o", 0), ("ffn_w_up", 2), ("ffn_w_down", 2), ("b_w_dq", 1), ("b_w_uq", 1)],
    "attn_fwd1": [("b_w_o", 1), ("ffn_w_up", 3), ("ffn_w_down", 3)],
}
BWD_RIDERS = {
    "attn_bwd1": [("ffn_w_down", 3), ("ffn_w_up", 3), ("b_w_o", 1)],
    "attn_bwd0": [("ffn_w_down", 2), ("ffn_w_up", 2), ("b_w_o", 0)],
    "ffn_bwd1": [("b_w_uq", 1), ("b_w_dq", 1), ("b_w_uq", 0), ("b_w_dq", 0), ("w_ukv", 0), ("w_dkv", 0)],
    "ffn_in_bwd1": [("ffn_w_down", 1), ("ffn_w_up", 1, "pair")],
    "ffn_bwd0": [("ffn_w_up", 1, "chip"), ("a_w_in", 1), ("a_w_out", 1)],
    "ffn_in_bwd0": [("ffn_w_down", 0), ("ffn_w_up", 0, "pair")],
    "mixer_bwd0": [("ffn_w_up", 0, "chip")],
    "mixer_in_bwd0": [("a_w_out", 0), ("a_w_in", 0, "pair")],
    "adamw_a_w_out": [("a_w_in", 0, "chip")],
}


def kernel(x, a_mix_norm, a_w_in, a_conv, a_w_out, b_mix_norm, b_w_dq, b_q_norm, b_w_uq, b_w_o, kv_in_norm, w_dkv, kv_norm, w_ukv, ffn_norm, ffn_w_up, ffn_conv, ffn_w_down, final_norm, loss_target, m_a_mix_norm, m_a_w_in, m_a_conv, m_a_w_out, m_b_mix_norm, m_b_w_dq, m_b_q_norm, m_b_w_uq, m_b_w_o, m_kv_in_norm, m_w_dkv, m_kv_norm, m_w_ukv, m_ffn_norm, m_ffn_w_up, m_ffn_conv, m_ffn_w_down, m_final_norm, v_a_mix_norm, v_a_w_in, v_a_conv, v_a_w_out, v_b_mix_norm, v_b_w_dq, v_b_q_norm, v_b_w_uq, v_b_w_o, v_kv_in_norm, v_w_dkv, v_kv_norm, v_w_ukv, v_ffn_norm, v_ffn_w_up, v_ffn_conv, v_ffn_w_down, v_final_norm):
    seq, d = x.shape[1], x.shape[2]
    me = 4 * lax.axis_index("x") + 2 * lax.axis_index("y") + lax.axis_index("c")
    h0 = x.reshape(seq, d)
    target = loss_target.reshape(seq, d)
    cos, sin = _rope_tables(seq)
    rank = b_w_dq.shape[-1]
    f8 = ffn_w_up.shape[-1]
    fd = ffn_w_down.shape[1]
    dshard = a_w_out.shape[1]
    hv = N_HEADS * VDIM

    shards = {"a_w_in": a_w_in, "a_w_out": a_w_out, "b_w_dq": b_w_dq, "b_w_uq": b_w_uq,
              "b_w_o": b_w_o, "w_dkv": w_dkv[None], "w_ukv": w_ukv[None],
              "ffn_w_up": ffn_w_up, "ffn_w_down": ffn_w_down}

    def relayout(name, g):
        if name == "a_w_in":
            w = jnp.transpose(g, (1, 0, 2)).reshape(d, 3, d)
            return jnp.transpose(w, (1, 0, 2))
        if name == "a_w_out":
            return g.reshape(d, d)
        if name == "b_w_dq":
            return g.reshape(d, rank)
        if name == "b_w_uq":
            return jnp.pad(g, ((0, 0), (0, 0), (0, QK - NOPE - ROPE)))
        if name == "b_w_o":
            return g.reshape(hv, d)
        if name == "w_dkv":
            return jnp.pad(g.reshape(d, KV_RANK + ROPE), ((0, 0), (0, ROPE_PAD - ROPE)))
        if name == "ffn_w_down":
            return g.reshape(N_DEV // 2, 2 * fd, d)
        return g

    weights = {}

    shards_bf16 = {n: w.astype(BF16) for n, w in shards.items()}

    def ag_rider(host):
        return [(("ag", l), shards_bf16[n]) for n, l in FWD_RIDERS.get(host, [])]

    def ag_done(host, outs):
        for (n, l), g in zip(FWD_RIDERS.get(host, []), outs):
            weights[n, l] = relayout(n, g)

    small_shapes = [a_mix_norm.shape, a_conv.shape, ffn_conv.shape]
    first = exchange([(("ag", 0), shards_bf16["a_w_in"]), (("ag", 0), shards_bf16["a_w_out"]),
                      ("ag", _pack([a_mix_norm, a_conv, ffn_conv]))], "ag_first")
    weights["a_w_in", 0] = relayout("a_w_in", first[0])
    weights["a_w_out", 0] = relayout("a_w_out", first[1])
    s_mix, s_aconv, s_fconv = _unpack(first[2], small_shapes)
    a_gain = jnp.transpose(s_mix, (1, 0, 2)).reshape(N_A, d)
    a_cw = jnp.transpose(s_aconv, (1, 2, 0, 3)).reshape(N_A, 3, d)
    f_cw = jnp.transpose(s_fconv, (1, 0, 2, 3))

    def mixer_gain(layer):
        if layer >= DEPTH:
            return None
        return a_gain[layer][None] if layer < N_A else b_mix_norm[layer - N_A][None]

    saved = {}
    h = h0
    xn = norm_fwd(h, mixer_gain(0), "norm_first")
    kv = None
    for layer in range(DEPTH):
        saved["hm", layer], saved["xm", layer] = h, xn
        if layer < N_A:
            name = f"mixer_fwd{layer}"
            (u4, z), r = mixer_fwd(xn, weights["a_w_in", layer], a_cw[layer], name, rider=ag_rider(name))
            ag_done(name, r)
            saved["mix", layer] = (u4, z)
            name = f"mixer_out{layer}"
            (h, xn), r = proj_residual(z[None], weights["a_w_out", layer][None], h, name,
                                       g_next=ffn_norm[layer][None], rider=ag_rider(name))
            ag_done(name, r)
        else:
            j = layer - N_A
            name = f"q_fwd{j}"
            (q,), r = q_fwd(xn, weights["b_w_dq", j], b_q_norm[j][None], weights["b_w_uq", j],
                            cos, sin, name, rider=ag_rider(name))
            ag_done(name, r)
            name = f"attn_fwd{j}"
            (o, lse), r = attn_fwd(q, kv[0], kv[1], name, rider=ag_rider(name))
            ag_done(name, r)
            saved["attn", layer] = (q, o, lse)
            name = f"attn_out{j}"
            (h, xn), r = proj_residual(o[None], weights["b_w_o", j][None], h, name,
                                       g_next=ffn_norm[layer][None], rider=ag_rider(name))
            ag_done(name, r)
        saved["hf", layer], saved["xf", layer] = h, xn
        name = f"ffn_fwd{layer}"
        (up2, cv2, act), r = ffn_fwd(xn, weights["ffn_w_up", layer], f_cw[layer], name, rider=ag_rider(name))
        ag_done(name, r)
        saved["ffn", layer] = (up2, cv2, act)
        name = f"ffn_out{layer}"
        (h, xn), r = proj_residual(act, weights["ffn_w_down", layer], h, name,
                                   g_next=mixer_gain(layer + 1), rider=ag_rider(name))
        ag_done(name, r)
        if layer == N_A - 1:
            (k_all, v_all, c_kv), r = kv_fwd(h, kv_in_norm[None], weights["w_dkv", 0], kv_norm[None],
                                             weights["w_ukv", 0], cos, sin, "kv_fwd",
                                             rider=ag_rider("kv_fwd"))
            ag_done("kv_fwd", r)
            kv = (k_all, v_all, c_kv)

    sq_err, dh, d_final = loss_head(h, final_norm[None], target, "loss_head")
    loss = lax.psum(sq_err[0, 0] * (0.5 / d), ("x", "y", "c"))

    grads = {}
    parts = {}

    pair_sums = {}

    def by_chip(g):
        return g.reshape((N_DEV // 2, 2) + g.shape[1:])

    def rs_rider(host):
        tasks = []
        for key in BWD_RIDERS.get(host, []):
            if len(key) == 2:
                tasks.append(("rs", grads[key]))
            elif key[2] == "pair":
                tasks.append(("rs_pair", by_chip(grads[key[:2]])))
            else:
                tasks.append(("rs_chip", pair_sums[key[:2]]))
        return tasks

    def rs_done(host, outs):
        for key, p in zip(BWD_RIDERS.get(host, []), outs):
            if len(key) == 3 and key[2] == "pair":
                pair_sums[key[:2]] = pair_sum(by_chip(grads[key[:2]]), p, f"pair_sum_{key[0]}{key[1]}")
            else:
                parts[key[:2]] = p

    d_ffn_norm = [None] * DEPTH
    d_fconv = [None] * DEPTH
    d_a_gain = [None] * N_A
    d_aconv = [None] * N_A
    d_b_gain = [None] * N_B
    d_q_gain = [None] * N_B
    dks, dvs = [], []
    for layer in reversed(range(DEPTH)):
        if layer == N_A - 1:
            hk = saved["hm", layer + 1]
            (dh, dwukv, dwdkv, d_kv_gain, d_kvin_gain), r = kv_bwd(
                dks, dvs, kv[2], hk, kv_in_norm[None], dh, weights["w_dkv", 0], kv_norm[None],
                weights["w_ukv", 0], cos, sin, "kv_bwd", rider=rs_rider("kv_bwd"))
            rs_done("kv_bwd", r)
            grads["w_ukv", 0] = dwukv
            grads["w_dkv", 0] = dwdkv[:, :KV_RANK + ROPE].reshape(N_DEV, dshard, KV_RANK + ROPE)
        up2, cv2, act = saved["ffn", layer]
        name = f"ffn_bwd{layer}"
        (dup2, dwup, dwdown, dcw), r = ffn_bwd(dh, weights["ffn_w_down", layer], up2, cv2, act,
                                               saved["xf", layer], f_cw[layer], name, rider=rs_rider(name))
        rs_done(name, r)
        grads["ffn_w_up", layer] = dwup.reshape(N_DEV, d, f8)
        grads["ffn_w_down", layer] = dwdown.reshape(N_DEV, fd, d)
        d_fconv[layer] = dcw.reshape(N_DEV, 3, f8)
        name = f"ffn_in_bwd{layer}"
        (dh, d_ffn_norm[layer]), r = proj_t_rms_bwd(dup2.reshape(N_DEV, seq, f8), weights["ffn_w_up", layer],
                                                    saved["hf", layer], ffn_norm[layer][None], dh, name,
                                                    rider=rs_rider(name))
        rs_done(name, r)
        hm, xm = saved["hm", layer], saved["xm", layer]
        if layer < N_A:
            u4, z = saved["mix", layer]
            name = f"mixer_bwd{layer}"
            (du3, dwin3, dwout, dcw), r = mixer_bwd(dh, weights["a_w_out", layer], u4, z, xm, a_cw[layer],
                                                    name, rider=rs_rider(name))
            rs_done(name, r)
            dwin = jnp.transpose(dwin3, (1, 0, 2)).reshape(d, N_DEV, 3 * d // N_DEV)
            grads["a_w_in", layer] = jnp.transpose(dwin, (1, 0, 2))
            grads["a_w_out", layer] = dwout.reshape(N_DEV, dshard, d)
            d_aconv[layer] = dcw
            name = f"mixer_in_bwd{layer}"
            extra = []
            if layer == 0:
                early_small = [
                    d_a_gain[1],
                    d_aconv[1],
                    jnp.concatenate(d_b_gain, axis=0),
                    jnp.concatenate(d_q_gain, axis=0),
                    d_kvin_gain[0],
                    d_kv_gain[0],
                    jnp.concatenate(d_ffn_norm, axis=0),
                    jnp.stack(d_fconv),
                    d_final[0],
                ]
                extra = [("ag", _pack(early_small))]
            (dh, d_a_gain[layer]), r = proj_t_rms_bwd(du3, weights["a_w_in", layer], hm, a_gain[layer][None],
                                                      dh, name, rider=rs_rider(name) + extra)
            rs_done(name, r)
            if layer == 0:
                g_early = r[-1]
        else:
            j = layer - N_A
            q, o, lse = saved["attn", layer]
            name = f"attn_out_bwd{j}"
            (do, dwo), r = o_bwd(dh, o, weights["b_w_o", j], name, rider=rs_rider(name))
            rs_done(name, r)
            grads["b_w_o", j] = dwo.reshape(N_DEV, dshard, d)
            name = f"attn_bwd{j}"
            (dq, dk, dv), r = attn_bwd(q, kv[0], kv[1], o, do, lse, name, rider=rs_rider(name))
            rs_done(name, r)
            dks.append(dk)
            dvs.append(dv)
            name = f"q_bwd{j}"
            (dh, dwuq, dwdq, d_q_gain[j], d_b_gain[j]), r = q_bwd(
                dq, xm, hm, b_mix_norm[j][None], dh, weights["b_w_dq", j], b_q_norm[j][None],
                weights["b_w_uq", j], cos, sin, name, rider=rs_rider(name))
            rs_done(name, r)
            grads["b_w_uq", j] = dwuq[:, :, :NOPE + ROPE]
            grads["b_w_dq", j] = dwdq.reshape(N_DEV, dshard, rank)
    grad_x = dh.reshape(x.shape)

    late_small = [d_a_gain[0], d_aconv[0]]
    full_shapes = [t.shape for t in early_small + late_small]
    small_pack = _pack(late_small)

    res = {}

    def update(name, n_layers, w, m, v, extra=(), transposed=False):
        view = (lambda t: jnp.transpose(t, (0, 2, 1))) if transposed else (lambda t: t)
        call = sum_adamw_transposed if transposed else sum_adamw
        shard = w.shape if w.ndim == 3 else (1,) + w.shape
        host = f"adamw_{name}"
        outs, r = call([parts[name, l] for l in range(n_layers)], view(w.reshape(shard)),
                       view(m.reshape(shard)), view(v.reshape(shard)), host,
                       rider=rs_rider(host) + list(extra))
        rs_done(host, r)
        res[name] = [view(t).reshape(w.shape) for t in outs]
        return r[len(BWD_RIDERS.get(host, [])):]

    (g_late,) = update("a_w_out", N_A, a_w_out, m_a_w_out, v_a_w_out, extra=[("ag", small_pack)])
    update("ffn_w_down", DEPTH, ffn_w_down, m_ffn_w_down, v_ffn_w_down)
    update("ffn_w_up", DEPTH, ffn_w_up, m_ffn_w_up, v_ffn_w_up, transposed=True)
    update("b_w_dq", N_B, b_w_dq, m_b_w_dq, v_b_w_dq)
    update("b_w_uq", N_B, b_w_uq, m_b_w_uq, v_b_w_uq)
    update("b_w_o", N_B, b_w_o, m_b_w_o, v_b_w_o)
    update("w_dkv", 1, w_dkv, m_w_dkv, v_w_dkv)
    update("w_ukv", 1, w_ukv, m_w_ukv, v_w_ukv)
    update("a_w_in", N_A, a_w_in, m_a_w_in, v_a_w_in)

    summed = sum_slots(jnp.concatenate([g_early, g_late], axis=1), "sum_small_grads")
    (s_a_gain1, s_aconv1, s_b_gain, s_q_gain, s_kvin, s_kvn, s_ffn_gain, s_fconv_g,
     s_final, s_a_gain0, s_aconv0) = _unpack(summed, full_shapes)
    s_a_gain = jnp.concatenate([s_a_gain0, s_a_gain1], axis=0)
    s_aconv_g = jnp.stack([s_aconv0, s_aconv1])
    dsl = d // N_DEV
    small = [
        ("a_mix_norm", lax.dynamic_slice_in_dim(s_a_gain, me * dsl, dsl, axis=1), a_mix_norm, m_a_mix_norm, v_a_mix_norm),
        ("a_conv", lax.dynamic_slice_in_dim(s_aconv_g, me * dsl, dsl, axis=2), a_conv, m_a_conv, v_a_conv),
        ("b_mix_norm", s_b_gain, b_mix_norm, m_b_mix_norm, v_b_mix_norm),
        ("b_q_norm", s_q_gain, b_q_norm, m_b_q_norm, v_b_q_norm),
        ("kv_in_norm", s_kvin, kv_in_norm, m_kv_in_norm, v_kv_in_norm),
        ("kv_norm", s_kvn, kv_norm, m_kv_norm, v_kv_norm),
        ("ffn_norm", s_ffn_gain, ffn_norm, m_ffn_norm, v_ffn_norm),
        ("ffn_conv", lax.dynamic_index_in_dim(s_fconv_g, me, axis=1, keepdims=False), ffn_conv, m_ffn_conv, v_ffn_conv),
        ("final_norm", s_final, final_norm, m_final_norm, v_final_norm),
    ]
    shapes = [t[2].shape for t in small]
    packed = [_pack([t[k] for t in small])[None] for k in (1, 2, 3, 4)]
    outs, _ = sum_adamw([packed[0]], packed[1], packed[2], packed[3], "adamw_small")
    unpacked = [_unpack(t[0], shapes) for t in outs]
    for idx, t in enumerate(small):
        res[t[0]] = [unpacked[k][idx] for k in range(4)]

    order = ["a_mix_norm", "a_w_in", "a_conv", "a_w_out", "b_mix_norm", "b_w_dq", "b_q_norm",
             "b_w_uq", "b_w_o", "kv_in_norm", "w_dkv", "kv_norm", "w_ukv", "ffn_norm",
             "ffn_w_up", "ffn_conv", "ffn_w_down", "final_norm"]
    return (loss, grad_x, *[res[n][0] for n in order], *[res[n][1] for n in order],
            *[res[n][2] for n in order], *[res[n][3] for n in order])
```

```python
import functools

import numpy as np
import jax
import jax.numpy as jnp
from jax import lax
from jax.experimental import pallas as pl
from jax.experimental.pallas import tpu as pltpu

F32 = jnp.float32
BF16 = jnp.bfloat16

N_DEV = 8
N_HEADS = 8
NOPE = 128
ROPE = 64
ROPE_PAD = 128
QK = NOPE + ROPE_PAD
VDIM = 128
KV_RANK = 256
ROPE_THETA = 10000.0
RMS_EPS = 1e-6
ATTN_SCALE = (NOPE + ROPE) ** -0.5
N_A = 2
N_B = 2
DEPTH = 4

ADAM_LR = 0.001
ADAM_B1 = 0.9
ADAM_B2 = 0.999
ADAM_EPS = 1e-08
ADAM_WD = 0.01
ADAM_STEP = 10

V7X_VMEM_LIMIT = 56 * 1024 * 1024
BF16_SUBLANES = 16
ROW_TILE = 512
ROW_TILE_LARGE = 1024
ADAM_ROWS = 256
ATTN_TILE = 512
MIXER_CHUNK = 512
LANES = 128
NEG_BIG = -1e30
COPIES_PER_TASK = 7

MESH_ID = pl.DeviceIdType.MESH
ANY = pl.BlockSpec(memory_space=pl.ANY)


def _nt(a, b):
    return lax.dot_general(a, b, (((1,), (1,)), ((), ())), preferred_element_type=F32)


def _tn(a, b):
    return lax.dot_general(a, b, (((0,), (0,)), ((), ())), preferred_element_type=F32)


def _nn(a, b):
    return jnp.dot(a, b, preferred_element_type=F32)


def _rms(h, g):
    rstd = lax.rsqrt(jnp.mean(h * h, axis=-1, keepdims=True) + RMS_EPS)
    xhat = h * rstd
    return xhat * g, xhat, rstd


def _rms_bwd(dxn, xhat, rstd, g):
    dxhat = dxn * g
    dh = rstd * (dxhat - xhat * jnp.mean(dxhat * xhat, axis=-1, keepdims=True))
    return dh, dxn * xhat


def _shift_down(x, k, halo_rows):
    r = pltpu.roll(x, k, 0)
    row = lax.broadcasted_iota(jnp.int32, x.shape, 0)
    for t in range(k):
        r = jnp.where(row == t, halo_rows[t], r)
    return r


def _shift_up(x, k, halo_rows):
    n = x.shape[0]
    r = pltpu.roll(x, n - k, 0)
    row = lax.broadcasted_iota(jnp.int32, x.shape, 0)
    for t in range(k):
        r = jnp.where(row == n - k + t, halo_rows[t], r)
    return r


def _conv_taps(w_ref):
    return w_ref[0:1, :], w_ref[1:2, :], w_ref[2:3, :]


def _rope_swap(x):
    lane = lax.broadcasted_iota(jnp.int32, x.shape, 1)
    return jnp.where(lane < ROPE // 2, pltpu.roll(x, ROPE_PAD - ROPE // 2, 1),
                     pltpu.roll(x, ROPE // 2, 1))


def _rope_fwd(x, cos, sin):
    return x * cos + _rope_swap(x) * sin


def _rope_bwd(dy, cos, sin):
    return dy * cos - _rope_swap(dy) * sin


def _rope_tables(seq):
    inv = (1.0 / np.power(np.float32(ROPE_THETA), np.arange(0, ROPE, 2, dtype=np.float32) / np.float32(ROPE)))
    ang = np.arange(seq, dtype=np.float32)[:, None] * inv.astype(np.float32)[None, :]
    ang = ang.astype(np.float32).astype(np.float64)
    cos, sin = np.cos(ang).astype(np.float32), np.sin(ang).astype(np.float32)
    zero = np.zeros((seq, ROPE_PAD - ROPE), np.float32)
    return (jnp.asarray(np.concatenate([cos, cos, zero], axis=1)),
            jnp.asarray(np.concatenate([-sin, sin, zero], axis=1)))


def _row_tile(rows, cap, mult=8):
    best = None
    for t in range(mult, min(rows, cap) + 1, mult):
        if rows % t == 0:
            best = t
    return rows if best is None else best


class _AllGatherTask:
    def __init__(self, t, x_ref, out_ref, send_sems, recv_sems, local_sems):
        self.t, self.x_ref, self.out_ref = t, x_ref, out_ref
        self.send_sems, self.recv_sems, self.local_sems = send_sems, recv_sems, local_sems
        mx, my, mc = lax.axis_index("x"), lax.axis_index("y"), lax.axis_index("c")
        self.mc = mc
        self.me, self.sibling = (mx, my, mc), (mx, my, 1 - mc)
        self.chips = [(1 - mx, my), (mx, 1 - my), (1 - mx, 1 - my)]

    def _slot(self, px, py, pc):
        return self.out_ref.at[4 * px + 2 * py + pc]

    def _copy(self, k, block, to, src=None):
        s = COPIES_PER_TASK * self.t + k
        return pltpu.make_async_remote_copy(
            src_ref=self._slot(*block) if src is None else src, dst_ref=self._slot(*block),
            send_sem=self.send_sems.at[s], recv_sem=self.recv_sems.at[s],
            device_id=to, device_id_type=MESH_ID)

    def _mine(self):
        return pltpu.make_async_copy(self.x_ref, self._slot(*self.me), self.local_sems.at[self.t])

    def _first(self):
        out = [self._copy(0, self.me, self.sibling, src=self.x_ref)]
        out += [self._copy(1 + j, self.me, (*chip, self.mc), src=self.x_ref)
                for j, chip in enumerate(self.chips)]
        return out

    def _passed(self):
        return [self._copy(4 + j, (*chip, self.mc), self.sibling) for j, chip in enumerate(self.chips)]

    def start(self):
        self._mine().start()
        for cp in self._first():
            cp.start()

    def forward(self):
        passed = self._passed()
        for j, chip in enumerate(self.chips):
            self._copy(1 + j, (*chip, self.mc), self.me).wait_recv()
            passed[j].start()

    def finish(self):
        self._copy(0, self.sibling, self.me).wait_recv()
        for j, chip in enumerate(self.chips):
            self._copy(4 + j, (*chip, 1 - self.mc), self.me).wait_recv()
        for cp in self._first() + self._passed():
            cp.wait_send()
        self._mine().wait()


class _ReduceScatterTask:
    def __init__(self, t, g_ref, out_ref, send_sems, recv_sems, local_sems):
        self.t, self.g_ref, self.out_ref = t, g_ref, out_ref
        self.send_sems, self.recv_sems, self.local_sems = send_sems, recv_sems, local_sems
        mx, my, mc = lax.axis_index("x"), lax.axis_index("y"), lax.axis_index("c")
        self.me = 4 * mx + 2 * my + mc
        self.peers = []
        for k in range(1, N_DEV):
            px, py, pc = mx ^ ((k >> 2) & 1), my ^ ((k >> 1) & 1), mc ^ (k & 1)
            self.peers.append(((px, py, pc), 4 * px + 2 * py + pc))

    def _mine(self):
        return pltpu.make_async_copy(self.g_ref.at[self.me], self.out_ref.at[self.me],
                                     self.local_sems.at[self.t])

    def _copy(self, k, src_slot, dst_slot):
        s = COPIES_PER_TASK * self.t + k
        return pltpu.make_async_remote_copy(
            src_ref=self.g_ref.at[src_slot], dst_ref=self.out_ref.at[dst_slot],
            send_sem=self.send_sems.at[s], recv_sem=self.recv_sems.at[s],
            device_id=self.peers[k][0], device_id_type=MESH_ID)

    def start(self):
        self._mine().start()
        for k, (_, peer) in enumerate(self.peers):
            self._copy(k, peer, self.me).start()

    def forward(self):
        pass

    def finish(self):
        for k, (_, peer) in enumerate(self.peers):
            self._copy(k, self.me, peer).wait_recv()
        for k, (_, peer) in enumerate(self.peers):
            self._copy(k, peer, self.me).wait_send()
        self._mine().wait()


class _PairExchangeTask:
    def __init__(self, t, g_ref, out_ref, send_sems, recv_sems, local_sems):
        mx, my, mc = lax.axis_index("x"), lax.axis_index("y"), lax.axis_index("c")
        s = COPIES_PER_TASK * t
        self.copy = pltpu.make_async_remote_copy(
            src_ref=g_ref.at[:, 1 - mc], dst_ref=out_ref,
            send_sem=send_sems.at[s], recv_sem=recv_sems.at[s],
            device_id=(mx, my, 1 - mc), device_id_type=MESH_ID)

    def start(self):
        self.copy.start()

    def forward(self):
        pass

    def finish(self):
        self.copy.wait()


class _ChipScatterTask:
    def __init__(self, t, s_ref, out_ref, send_sems, recv_sems, local_sems):
        self.t, self.s_ref, self.out_ref = t, s_ref, out_ref
        self.send_sems, self.recv_sems, self.local_sems = send_sems, recv_sems, local_sems
        mx, my, mc = lax.axis_index("x"), lax.axis_index("y"), lax.axis_index("c")
        self.chip = 2 * mx + my
        self.peers = []
        for k in range(1, N_DEV // 2):
            px, py = mx ^ ((k >> 1) & 1), my ^ (k & 1)
            self.peers.append(((px, py, mc), 2 * px + py))

    def _mine(self):
        return pltpu.make_async_copy(self.s_ref.at[self.chip], self.out_ref.at[self.chip],
                                     self.local_sems.at[self.t])

    def _copy(self, k, src_slot, dst_slot):
        s = COPIES_PER_TASK * self.t + k
        return pltpu.make_async_remote_copy(
            src_ref=self.s_ref.at[src_slot], dst_ref=self.out_ref.at[dst_slot],
            send_sem=self.send_sems.at[s], recv_sem=self.recv_sems.at[s],
            device_id=self.peers[k][0], device_id_type=MESH_ID)

    def start(self):
        self._mine().start()
        for k, (_, peer) in enumerate(self.peers):
            self._copy(k, peer, self.chip).start()

    def forward(self):
        pass

    def finish(self):
        for k, (_, peer) in enumerate(self.peers):
            self._copy(k, self.chip, peer).wait_recv()
        for k, (_, peer) in enumerate(self.peers):
            self._copy(k, peer, self.chip).wait_send()
        self._mine().wait()


_TASKS = {"ag": _AllGatherTask, "rs": _ReduceScatterTask, "rs_pair": _PairExchangeTask,
          "rs_chip": _ChipScatterTask}


def _task_shape(kind, arr):
    if isinstance(kind, tuple):
        return jax.ShapeDtypeStruct((N_DEV,) + arr.shape[1:], arr.dtype)
    shape = {"ag": (N_DEV,) + arr.shape, "rs": arr.shape, "rs_chip": arr.shape,
             "rs_pair": arr.shape[:1] + arr.shape[2:]}[kind]
    return jax.ShapeDtypeStruct(shape, arr.dtype)


def _sem_shapes(n_tasks):
    return [pltpu.SemaphoreType.DMA((COPIES_PER_TASK * n_tasks,)),
            pltpu.SemaphoreType.DMA((COPIES_PER_TASK * n_tasks,)),
            pltpu.SemaphoreType.DMA((n_tasks,))]


def _make_tasks(rider, in_refs, out_refs, sems):
    tasks = []
    for t, (kind, _) in enumerate(rider):
        if isinstance(kind, tuple):
            tasks.append(_AllGatherTask(t, in_refs[t].at[kind[1]], out_refs[t], *sems))
        else:
            tasks.append(_TASKS[kind](t, in_refs[t], out_refs[t], *sems))
    return tasks


def exchange(rider, name):
    n = len(rider)

    def body(*refs):
        tasks = _make_tasks(rider, refs[:n], refs[n:2 * n], refs[2 * n:])
        for task in tasks:
            task.start()
        for task in tasks:
            task.forward()
        for task in tasks:
            task.finish()

    return list(pl.pallas_call(
        body, name=name, out_shape=tuple(_task_shape(k, a) for k, a in rider),
        in_specs=[ANY] * n, out_specs=(ANY,) * n, scratch_shapes=_sem_shapes(n),
    )(*[a for _, a in rider]))


def _call(body, name, grid, in_specs, out_specs, out_shape, args, scratch=(), rider=()):
    in_specs, out_specs, out_shape = list(in_specs), tuple(out_specs), tuple(out_shape)
    n_in, n_out, n_scr, n_r = len(in_specs), len(out_specs), len(scratch), len(rider)
    if n_r:
        def kern(*refs):
            ins, r_in = refs[:n_in], refs[n_in:n_in + n_r]
            o0 = n_in + n_r
            outs, r_out = refs[o0:o0 + n_out], refs[o0 + n_out:o0 + n_out + n_r]
            s0 = o0 + n_out + n_r
            scr, sems = refs[s0:s0 + n_scr], refs[s0 + n_scr:]
            step = 0
            for a, n in enumerate(grid):
                step = step * n + pl.program_id(a)
            n_steps = 1
            for n in grid:
                n_steps *= n

            @pl.when(step == 0)
            def _():
                for task in _make_tasks(rider, r_in, r_out, sems):
                    task.start()
            body(*ins, *outs, *scr)

            @pl.when(step == n_steps - 1)
            def _():
                tasks = _make_tasks(rider, r_in, r_out, sems)
                for task in tasks:
                    task.forward()
                for task in tasks:
                    task.finish()
    else:
        kern = body
    res = pl.pallas_call(
        kern, name=name, grid=grid,
        in_specs=in_specs + [ANY] * n_r, out_specs=out_specs + (ANY,) * n_r,
        out_shape=out_shape + tuple(_task_shape(k, a) for k, a in rider),
        scratch_shapes=list(scratch) + (_sem_shapes(n_r) if n_r else []),
        compiler_params=pltpu.CompilerParams(dimension_semantics=("arbitrary",) * len(grid),
                                             vmem_limit_bytes=V7X_VMEM_LIMIT),
    )(*args, *[a for _, a in rider])
    return list(res[:n_out]), list(res[n_out:])


def _adamw(g, w, m, v):
    m = ADAM_B1 * m + (1.0 - ADAM_B1) * g
    v = ADAM_B2 * v + (1.0 - ADAM_B2) * (g * g)
    m_hat = m / (1.0 - ADAM_B1 ** ADAM_STEP)
    v_hat = v / (1.0 - ADAM_B2 ** ADAM_STEP)
    delta = -ADAM_LR * (m_hat / (jnp.sqrt(v_hat) + ADAM_EPS) + ADAM_WD * w)
    return delta, m, v


def sum_adamw(parts, w, m, v, name, rider=()):
    n_l, rows, cols = w.shape
    mult = BF16_SUBLANES if parts[0].dtype == BF16 else 8
    tr = _row_tile(rows, ADAM_ROWS, mult)
    n_i = rows // tr

    def body(*refs):
        part_refs = refs[:n_l]
        w_ref, m_ref, v_ref, g_out, d_out, m_out, v_out = refs[n_l:]
        layer = pl.program_id(0)
        for k in range(n_l):
            @pl.when(layer == k)
            def _(k=k):
                g = part_refs[k][0].astype(F32)
                for s in range(1, parts[k].shape[0]):
                    g = g + part_refs[k][s].astype(F32)
                delta, m_new, v_new = _adamw(g, w_ref[...], m_ref[...], v_ref[...])
                g_out[...] = g
                d_out[...] = delta
                m_out[...] = m_new
                v_out[...] = v_new

    part_specs = [pl.BlockSpec((parts[k].shape[0], tr, cols), functools.partial(
        lambda l, i, k: (0, jnp.where(l == k, i, 0), 0), k=k)) for k in range(n_l)]
    wspec = pl.BlockSpec((None, tr, cols), lambda l, i: (l, i, 0))
    shape = jax.ShapeDtypeStruct(w.shape, F32)
    return _call(body, name, (n_l, n_i), part_specs + [wspec] * 3, (wspec,) * 4, (shape,) * 4,
                 (*parts, w, m, v), rider=rider)


def sum_adamw_transposed(parts, w_t, m_t, v_t, name, rider=()):
    n_l, cols, rows = w_t.shape
    tr = LANES
    n_i = rows // tr
    starts = list(range(0, cols - LANES + 1, LANES))
    if starts[-1] + LANES < cols:
        starts.append(cols - LANES)

    def body(*refs):
        part_refs = refs[:n_l]
        w_ref, m_ref, v_ref, g_out, d_out, m_out, v_out = refs[n_l:]
        layer = pl.program_id(0)
        for k in range(n_l):
            @pl.when(layer == k)
            def _(k=k):
                for c0 in starts:
                    piece = pl.ds(c0, LANES)
                    g = part_refs[k][0, :, piece].astype(F32)
                    for s in range(1, parts[k].shape[0]):
                        g = g + part_refs[k][s, :, piece].astype(F32)
                    g = g.T
                    delta, m_new, v_new = _adamw(g, w_ref[piece, :], m_ref[piece, :], v_ref[piece, :])
                    g_out[piece, :] = g
                    d_out[piece, :] = delta
                    m_out[piece, :] = m_new
                    v_out[piece, :] = v_new

    part_specs = [pl.BlockSpec((parts[k].shape[0], tr, cols), functools.partial(
        lambda l, i, k: (0, jnp.where(l == k, i, 0), 0), k=k)) for k in range(n_l)]
    wspec = pl.BlockSpec((None, cols, tr), lambda l, i: (l, 0, i))
    shape = jax.ShapeDtypeStruct(w_t.shape, F32)
    return _call(body, name, (n_l, n_i), part_specs + [wspec] * 3, (wspec,) * 4, (shape,) * 4,
                 (*parts, w_t, m_t, v_t), rider=rider)


def pair_sum(g4, other, name):
    n_chip, _, rows, cols = g4.shape
    tr = _row_tile(rows, 512, BF16_SUBLANES)

    def body(core_ref, g_ref, o_ref, s_ref):
        s_ref[...] = (g_ref[...].astype(F32) + o_ref[...].astype(F32)).astype(BF16)

    blk = pl.BlockSpec((None, tr, cols), lambda k, i, core: (k, i, 0))
    return pl.pallas_call(
        body, name=name, out_shape=jax.ShapeDtypeStruct((n_chip, rows, cols), g4.dtype),
        grid_spec=pltpu.PrefetchScalarGridSpec(
            num_scalar_prefetch=1, grid=(n_chip, rows // tr),
            in_specs=[pl.BlockSpec((None, None, tr, cols), lambda k, i, core: (k, core[0], i, 0)), blk],
            out_specs=blk),
        compiler_params=pltpu.CompilerParams(dimension_semantics=("arbitrary", "arbitrary"),
                                             vmem_limit_bytes=V7X_VMEM_LIMIT),
    )(lax.axis_index("c").astype(jnp.int32).reshape(1), g4, other)


def sum_slots(parts, name):
    n, rows, cols = parts.shape

    def body(p_ref, o_ref):
        acc = p_ref[0]
        for s in range(1, n):
            acc = acc + p_ref[s]
        o_ref[...] = acc

    return pl.pallas_call(
        body, name=name, out_shape=jax.ShapeDtypeStruct((rows, cols), F32),
        in_specs=[pl.BlockSpec(memory_space=pltpu.VMEM)],
        out_specs=pl.BlockSpec(memory_space=pltpu.VMEM),
    )(parts)


def norm_fwd(h, g, name):
    seq, d = h.shape
    tm = min(ROW_TILE, seq)

    def body(h_ref, g_ref, o_ref):
        o_ref[...] = _rms(h_ref[...], g_ref[...])[0].astype(BF16)

    return _call(body, name, (seq // tm,),
                 [pl.BlockSpec((tm, d), lambda i: (i, 0)), pl.BlockSpec((1, d), lambda i: (0, 0))],
                 [pl.BlockSpec((tm, d), lambda i: (i, 0))],
                 [jax.ShapeDtypeStruct((seq, d), BF16)], (h, g))[0][0]


def proj_residual(a, w, res, name, g_next=None, rider=()):
    nb, seq, kb = a.shape
    d = w.shape[-1]
    tm = min(ROW_TILE, seq)
    with_norm = g_next is not None

    def body(a_ref, w_ref, r_ref, *rest):
        acc = r_ref[...]
        for b in range(nb):
            acc = acc + _nn(a_ref[b], w_ref[b])
        if with_norm:
            g_ref, o_ref, xn_ref = rest
            xn_ref[...] = _rms(acc, g_ref[...])[0].astype(BF16)
        else:
            (o_ref,) = rest
        o_ref[...] = acc

    row = pl.BlockSpec((tm, d), lambda i: (i, 0))
    in_specs = [pl.BlockSpec((nb, tm, kb), lambda i: (0, i, 0)),
                pl.BlockSpec((nb, kb, d), lambda i: (0, 0, 0)), row]
    args = [a, w, res]
    out_specs, out_shape = [row], [jax.ShapeDtypeStruct((seq, d), F32)]
    if with_norm:
        in_specs.append(pl.BlockSpec((1, d), lambda i: (0, 0)))
        args.append(g_next)
        out_specs.append(row)
        out_shape.append(jax.ShapeDtypeStruct((seq, d), BF16))
    outs, r_outs = _call(body, name, (seq // tm,), in_specs, out_specs, out_shape, args, rider=rider)
    return (outs[0], outs[1] if with_norm else None), r_outs


def proj_t_rms_bwd(du, w, h, g, dres, name, rider=()):
    nb, seq, wd = du.shape
    k = w.shape[1]
    big_weight = 2 * w.size * w.dtype.itemsize > V7X_VMEM_LIMIT // 4
    tm = min(ROW_TILE // 2 if big_weight else ROW_TILE, seq)

    def body(du_ref, w_ref, h_ref, g_ref, dr_ref, dh_ref, dg_ref):
        i = pl.program_id(0)
        dxn = _nt(du_ref[0], w_ref[0])
        for b in range(1, nb):
            dxn = dxn + _nt(du_ref[b], w_ref[b])
        _, xhat, rstd = _rms(h_ref[...], g_ref[...])
        dh, dg_rows = _rms_bwd(dxn, xhat, rstd, g_ref[...])
        dh_ref[...] = dr_ref[...] + dh

        @pl.when(i == 0)
        def _():
            dg_ref[...] = jnp.zeros_like(dg_ref)
        dg_ref[...] += jnp.sum(dg_rows, axis=0, keepdims=True)

    row = pl.BlockSpec((tm, k), lambda i: (i, 0))
    vec = pl.BlockSpec((1, k), lambda i: (0, 0))
    return _call(body, name, (seq // tm,),
                 [pl.BlockSpec((nb, tm, wd), lambda i: (0, i, 0)),
                  pl.BlockSpec((nb, k, wd), lambda i: (0, 0, 0)), row, vec, row],
                 (row, vec),
                 (jax.ShapeDtypeStruct((seq, k), F32), jax.ShapeDtypeStruct((1, k), F32)),
                 (du, w, h, g, dres), rider=rider)


def mixer_fwd(xn, win3, cw, name, rider=()):
    seq, d = xn.shape
    tm = min(ROW_TILE_LARGE, seq)
    cc = min(MIXER_CHUNK, d)
    n_c, n_i = d // cc, seq // tm

    def body(x_ref, w_ref, cw_ref, u_ref, z_ref, carry):
        i = pl.program_id(1)

        @pl.when(i == 0)
        def _():
            carry[...] = jnp.zeros_like(carry)
        xb = x_ref[...]
        b = _nn(xb, w_ref[0])
        c = _nn(xb, w_ref[1])
        hh = _nn(xb, w_ref[2])
        p = c * hh
        w0, w1, w2 = _conv_taps(cw_ref)
        p1 = _shift_down(p, 1, [carry[7:8, :]])
        p2 = _shift_down(p, 2, [carry[6:7, :], carry[7:8, :]])
        q = w0 * p2 + w1 * p1 + w2 * p
        carry[...] = p[tm - 8:tm, :]
        u_ref[0] = b.astype(BF16)
        u_ref[1] = c.astype(BF16)
        u_ref[2] = hh.astype(BF16)
        u_ref[3] = q.astype(BF16)
        z_ref[...] = (b * q).astype(BF16)

    return _call(body, name, (n_c, n_i),
                 [pl.BlockSpec((tm, d), lambda c, i: (i, 0)),
                  pl.BlockSpec((3, d, cc), lambda c, i: (0, 0, c)),
                  pl.BlockSpec((3, cc), lambda c, i: (0, c))],
                 (pl.BlockSpec((4, tm, cc), lambda c, i: (0, i, c)),
                  pl.BlockSpec((tm, cc), lambda c, i: (i, c))),
                 (jax.ShapeDtypeStruct((4, seq, d), BF16), jax.ShapeDtypeStruct((seq, d), BF16)),
                 (xn, win3, cw), scratch=[pltpu.VMEM((8, cc), F32)], rider=rider)


def mixer_bwd(dh, wout, u4, z, xn, cw, name, rider=()):
    seq, d = xn.shape
    tm = min(ROW_TILE, seq)
    cc = min(MIXER_CHUNK, d)
    n_c, n_i = d // cc, seq // tm

    def body(dh_ref, wout_ref, u_ref, z_ref, x_ref, cw_ref,
             du_ref, dwin_ref, dwout_ref, dcw_ref, acc_in, acc_out, acc_cw, carry):
        i = pl.program_id(1)

        @pl.when(i == 0)
        def _():
            acc_in[...] = jnp.zeros_like(acc_in)
            acc_out[...] = jnp.zeros_like(acc_out)
            acc_cw[...] = jnp.zeros_like(acc_cw)
            carry[...] = jnp.zeros_like(carry)
        dhb = dh_ref[...].astype(BF16)
        dz = _nt(dhb, wout_ref[...])
        acc_out[...] += _tn(z_ref[...], dhb)
        b = u_ref[0].astype(F32)
        c = u_ref[1].astype(F32)
        hh = u_ref[2].astype(F32)
        q = u_ref[3].astype(F32)
        p = c * hh
        db = dz * q
        dq = dz * b
        w0, w1, w2 = _conv_taps(cw_ref)
        dq1 = _shift_up(dq, 1, [carry[0:1, :]])
        dq2 = _shift_up(dq, 2, [carry[0:1, :], carry[1:2, :]])
        dp = w2 * dq + w1 * dq1 + w0 * dq2
        carry[...] = dq[0:8, :]
        acc_cw[0:1, :] += jnp.sum(dq2 * p, axis=0, keepdims=True)
        acc_cw[1:2, :] += jnp.sum(dq1 * p, axis=0, keepdims=True)
        acc_cw[2:3, :] += jnp.sum(dq * p, axis=0, keepdims=True)
        dbb = db.astype(BF16)
        dcb = (dp * hh).astype(BF16)
        dhhb = (dp * c).astype(BF16)
        du_ref[0] = dbb
        du_ref[1] = dcb
        du_ref[2] = dhhb
        xb = x_ref[...]
        acc_in[0] += _tn(xb, dbb)
        acc_in[1] += _tn(xb, dcb)
        acc_in[2] += _tn(xb, dhhb)

        @pl.when(i == n_i - 1)
        def _():
            dwin_ref[...] = acc_in[...].astype(BF16)
            dwout_ref[...] = acc_out[...].astype(BF16)
            dcw_ref[...] = acc_cw[0:3, :]

    rev = lambda c, i: (n_i - 1 - i, 0)
    return _call(body, name, (n_c, n_i),
                 [pl.BlockSpec((tm, d), rev),
                  pl.BlockSpec((cc, d), lambda c, i: (c, 0)),
                  pl.BlockSpec((4, tm, cc), lambda c, i: (0, n_i - 1 - i, c)),
                  pl.BlockSpec((tm, cc), lambda c, i: (n_i - 1 - i, c)),
                  pl.BlockSpec((tm, d), rev),
                  pl.BlockSpec((3, cc), lambda c, i: (0, c))],
                 (pl.BlockSpec((3, tm, cc), lambda c, i: (0, n_i - 1 - i, c)),
                  pl.BlockSpec((3, d, cc), lambda c, i: (0, 0, c)),
                  pl.BlockSpec((cc, d), lambda c, i: (c, 0)),
                  pl.BlockSpec((3, cc), lambda c, i: (0, c))),
                 (jax.ShapeDtypeStruct((3, seq, d), BF16), jax.ShapeDtypeStruct((3, d, d), BF16),
                  jax.ShapeDtypeStruct((d, d), BF16), jax.ShapeDtypeStruct((3, d), F32)),
                 (dh, wout, u4, z, xn, cw),
                 scratch=[pltpu.VMEM((3, d, cc), F32), pltpu.VMEM((cc, d), F32),
                          pltpu.VMEM((8, cc), F32), pltpu.VMEM((8, cc), F32)], rider=rider)


def _silu_parts(cg):
    sg = 1.0 / (1.0 + jnp.exp(-cg))
    return sg, cg * sg


def ffn_fwd(xn, wup, fcw, name, rider=()):
    seq, d = xn.shape
    f8 = wup.shape[-1]
    half = N_DEV // 2
    tm = min(ROW_TILE_LARGE, seq)
    n_i = seq // tm

    def body(x_ref, wg_ref, wu_ref, cg_ref, cu_ref, up_ref, cv_ref, a_ref, carry):
        i = pl.program_id(1)

        @pl.when(i == 0)
        def _():
            carry[...] = jnp.zeros_like(carry)
        xb = x_ref[...]
        conv = []
        for s, (w_ref, t_ref) in enumerate(((wg_ref, cg_ref), (wu_ref, cu_ref))):
            u = _nn(xb, w_ref[...])
            up_ref[s] = u.astype(BF16)
            w0, w1, w2 = _conv_taps(t_ref)
            u1 = _shift_down(u, 1, [carry[s, 7:8, :]])
            u2 = _shift_down(u, 2, [carry[s, 6:7, :], carry[s, 7:8, :]])
            cv = w0 * u2 + w1 * u1 + w2 * u
            cv_ref[s] = cv.astype(BF16)
            conv.append(cv)
            carry[s] = u[tm - 8:tm, :]
        _, silu = _silu_parts(conv[0])
        a_ref[...] = (silu * conv[1]).astype(BF16)

    blk = pl.BlockSpec((2, None, tm, f8), lambda c, i: (0, c, i, 0))
    big = jax.ShapeDtypeStruct((2, half, seq, f8), BF16)
    return _call(body, name, (half, n_i),
                 [pl.BlockSpec((tm, d), lambda c, i: (i, 0)),
                  pl.BlockSpec((None, d, f8), lambda c, i: (c, 0, 0)),
                  pl.BlockSpec((None, d, f8), lambda c, i: (c + half, 0, 0)),
                  pl.BlockSpec((None, 3, f8), lambda c, i: (c, 0, 0)),
                  pl.BlockSpec((None, 3, f8), lambda c, i: (c + half, 0, 0))],
                 (blk, blk, pl.BlockSpec((None, tm, f8), lambda c, i: (c, i, 0))),
                 (big, big, jax.ShapeDtypeStruct((half, seq, f8), BF16)),
                 (xn, wup, wup, fcw, fcw), scratch=[pltpu.VMEM((2, 8, f8), F32)], rider=rider)


def ffn_bwd(dh, wdown, up2, cv2, act, xn, fcw, name, rider=()):
    seq, d = xn.shape
    f8 = up2.shape[-1]
    fb = wdown.shape[1]
    half = N_DEV // 2
    tm = min(ROW_TILE, seq)
    n_i = seq // tm

    def body(dh_ref, wd_ref, up_ref, cv_ref, a_ref, x_ref, cg_ref, cu_ref,
             dup_ref, dwup_ref, dwd_ref, dcw_ref, acc_up, acc_down, acc_cw, carry):
        i = pl.program_id(1)

        @pl.when(i == 0)
        def _():
            acc_up[...] = jnp.zeros_like(acc_up)
            acc_down[...] = jnp.zeros_like(acc_down)
            acc_cw[...] = jnp.zeros_like(acc_cw)
            carry[...] = jnp.zeros_like(carry)
        dhb = dh_ref[...].astype(BF16)
        da = _nt(dhb, wd_ref[...])
        acc_down[...] += _tn(a_ref[...], dhb)
        cg = cv_ref[0].astype(F32)
        cu = cv_ref[1].astype(F32)
        sg, silu = _silu_parts(cg)
        dcg = da * cu * (sg + silu * (1.0 - sg))
        dcu = da * silu
        xb = x_ref[...]
        for s, (dc, t_ref) in enumerate(((dcg, cg_ref), (dcu, cu_ref))):
            w0, w1, w2 = _conv_taps(t_ref)
            d1 = _shift_up(dc, 1, [carry[s, 0:1, :]])
            d2 = _shift_up(dc, 2, [carry[s, 0:1, :], carry[s, 1:2, :]])
            du = (w2 * dc + w1 * d1 + w0 * d2).astype(BF16)
            carry[s] = dc[0:8, :]
            u = up_ref[s].astype(F32)
            acc_cw[s, 0:1, :] += jnp.sum(d2 * u, axis=0, keepdims=True)
            acc_cw[s, 1:2, :] += jnp.sum(d1 * u, axis=0, keepdims=True)
            acc_cw[s, 2:3, :] += jnp.sum(dc * u, axis=0, keepdims=True)
            dup_ref[s] = du
            acc_up[s] += _tn(xb, du)

        @pl.when(i == n_i - 1)
        def _():
            dwup_ref[...] = acc_up[...].astype(BF16)
            dwd_ref[...] = acc_down[...].astype(BF16)
            dcw_ref[...] = acc_cw[:, 0:3, :]

    rev = lambda c, i: (n_i - 1 - i, 0)
    blk = pl.BlockSpec((2, None, tm, f8), lambda c, i: (0, c, n_i - 1 - i, 0))
    return _call(body, name, (half, n_i),
                 [pl.BlockSpec((tm, d), rev),
                  pl.BlockSpec((None, fb, d), lambda c, i: (c, 0, 0)),
                  blk, blk,
                  pl.BlockSpec((None, tm, f8), lambda c, i: (c, n_i - 1 - i, 0)),
                  pl.BlockSpec((tm, d), rev),
                  pl.BlockSpec((None, 3, f8), lambda c, i: (c, 0, 0)),
                  pl.BlockSpec((None, 3, f8), lambda c, i: (c + half, 0, 0))],
                 (blk,
                  pl.BlockSpec((2, None, d, f8), lambda c, i: (0, c, 0, 0)),
                  pl.BlockSpec((None, fb, d), lambda c, i: (c, 0, 0)),
                  pl.BlockSpec((2, None, 3, f8), lambda c, i: (0, c, 0, 0))),
                 (jax.ShapeDtypeStruct((2, half, seq, f8), BF16),
                  jax.ShapeDtypeStruct((2, half, d, f8), BF16),
                  jax.ShapeDtypeStruct((half, fb, d), BF16),
                  jax.ShapeDtypeStruct((2, half, 3, f8), F32)),
                 (dh, wdown, up2, cv2, act, xn, fcw, fcw),
                 scratch=[pltpu.VMEM((2, d, f8), F32), pltpu.VMEM((fb, d), F32),
                          pltpu.VMEM((2, 8, f8), F32), pltpu.VMEM((2, 8, f8), F32)], rider=rider)


def q_fwd(xn, wdq, gq, wuq, cos, sin, name, rider=()):
    seq, d = xn.shape
    rank = wdq.shape[-1]
    tm = min(ROW_TILE, seq)

    def body(x_ref, wdq_ref, gq_ref, wuq_ref, cos_ref, sin_ref, q_ref):
        qc = _nn(x_ref[...], wdq_ref[...])
        qn = _rms(qc, gq_ref[...])[0].astype(BF16)
        for hd in range(N_HEADS):
            qh = _nn(qn, wuq_ref[hd])
            qr = _rope_fwd(qh[:, NOPE:QK], cos_ref[...], sin_ref[...])
            q_ref[hd, :, 0:NOPE] = (qh[:, 0:NOPE] * ATTN_SCALE).astype(BF16)
            q_ref[hd, :, NOPE:QK] = (qr * ATTN_SCALE).astype(BF16)

    rope = pl.BlockSpec((tm, ROPE_PAD), lambda i: (i, 0))
    return _call(body, name, (seq // tm,),
                 [pl.BlockSpec((tm, d), lambda i: (i, 0)),
                  pl.BlockSpec((d, rank), lambda i: (0, 0)),
                  pl.BlockSpec((1, rank), lambda i: (0, 0)),
                  pl.BlockSpec((N_HEADS, rank, QK), lambda i: (0, 0, 0)), rope, rope],
                 [pl.BlockSpec((N_HEADS, tm, QK), lambda i: (0, i, 0))],
                 [jax.ShapeDtypeStruct((N_HEADS, seq, QK), BF16)],
                 (xn, wdq, gq, wuq, cos, sin), rider=rider)


def q_bwd(dq, xn, h, g, dres, wdq, gq, wuq, cos, sin, name, rider=()):
    seq, d = xn.shape
    rank = wdq.shape[-1]
    tm = min(ROW_TILE, seq)
    n_i = seq // tm

    def body(dq_ref, x_ref, h_ref, g_ref, dr_ref, wdq_ref, gq_ref, wuq_ref, cos_ref, sin_ref,
             dh_ref, dwuq_ref, dwdq_ref, dgq_ref, dg_ref, acc_uq, acc_dq):
        i = pl.program_id(0)

        @pl.when(i == 0)
        def _():
            acc_uq[...] = jnp.zeros_like(acc_uq)
            acc_dq[...] = jnp.zeros_like(acc_dq)
            dgq_ref[...] = jnp.zeros_like(dgq_ref)
            dg_ref[...] = jnp.zeros_like(dg_ref)
        xb = x_ref[...]
        qc = _nn(xb, wdq_ref[...])
        qn, qhat, qrstd = _rms(qc, gq_ref[...])
        qnb = qn.astype(BF16)
        dqn = jnp.zeros((tm, rank), F32)
        for hd in range(N_HEADS):
            dnope = (dq_ref[hd, :, 0:NOPE].astype(F32) * ATTN_SCALE).astype(BF16)
            drope = _rope_bwd(dq_ref[hd, :, NOPE:QK].astype(F32) * ATTN_SCALE, cos_ref[...], sin_ref[...])
            draw = jnp.concatenate([dnope, drope.astype(BF16)], axis=1)
            dqn = dqn + _nt(draw, wuq_ref[hd])
            acc_uq[hd] += _tn(qnb, draw)
        dqc, dg_rows = _rms_bwd(dqn, qhat, qrstd, gq_ref[...])
        dgq_ref[...] += jnp.sum(dg_rows, axis=0, keepdims=True)
        dqcb = dqc.astype(BF16)
        acc_dq[...] += _tn(xb, dqcb)
        _, xhat, rstd = _rms(h_ref[...], g_ref[...])
        dh, dg_rows = _rms_bwd(_nt(dqcb, wdq_ref[...]), xhat, rstd, g_ref[...])
        dh_ref[...] = dr_ref[...] + dh
        dg_ref[...] += jnp.sum(dg_rows, axis=0, keepdims=True)

        @pl.when(i == n_i - 1)
        def _():
            dwuq_ref[...] = acc_uq[...].astype(BF16)
            dwdq_ref[...] = acc_dq[...].astype(BF16)

    rope = pl.BlockSpec((tm, ROPE_PAD), lambda i: (i, 0))
    row = pl.BlockSpec((tm, d), lambda i: (i, 0))
    vec = pl.BlockSpec((1, d), lambda i: (0, 0))
    return _call(body, name, (n_i,),
                 [pl.BlockSpec((N_HEADS, tm, QK), lambda i: (0, i, 0)), row, row, vec, row,
                  pl.BlockSpec((d, rank), lambda i: (0, 0)),
                  pl.BlockSpec((1, rank), lambda i: (0, 0)),
                  pl.BlockSpec((N_HEADS, rank, QK), lambda i: (0, 0, 0)), rope, rope],
                 (row,
                  pl.BlockSpec((N_HEADS, rank, QK), lambda i: (0, 0, 0)),
                  pl.BlockSpec((d, rank), lambda i: (0, 0)),
                  pl.BlockSpec((1, rank), lambda i: (0, 0)), vec),
                 (jax.ShapeDtypeStruct((seq, d), F32),
                  jax.ShapeDtypeStruct((N_HEADS, rank, QK), BF16),
                  jax.ShapeDtypeStruct((d, rank), BF16),
                  jax.ShapeDtypeStruct((1, rank), F32),
                  jax.ShapeDtypeStruct((1, d), F32)),
                 (dq, xn, h, g, dres, wdq, gq, wuq, cos, sin),
                 scratch=[pltpu.VMEM((N_HEADS, rank, QK), F32), pltpu.VMEM((d, rank), F32)],
                 rider=rider)


def kv_fwd(h, g, wdkv, gkv, wukv, cos, sin, name, rider=()):
    seq, d = h.shape
    tm = min(ROW_TILE, seq)
    wk = KV_RANK + ROPE_PAD

    def body(h_ref, g_ref, wdkv_ref, gkv_ref, wukv_ref, cos_ref, sin_ref, k_ref, v_ref, c_ref):
        xk = _rms(h_ref[...], g_ref[...])[0].astype(BF16)
        ckv = _nn(xk, wdkv_ref[...])
        c_kv = ckv[:, 0:KV_RANK]
        c_ref[...] = c_kv
        kr = _rope_fwd(ckv[:, KV_RANK:wk], cos_ref[...], sin_ref[...]).astype(BF16)
        ckn = _rms(c_kv, gkv_ref[...])[0].astype(BF16)
        for hd in range(N_HEADS):
            kvh = _nn(ckn, wukv_ref[hd])
            k_ref[hd, :, 0:NOPE] = kvh[:, 0:NOPE].astype(BF16)
            k_ref[hd, :, NOPE:QK] = kr
            v_ref[hd] = kvh[:, NOPE:NOPE + VDIM].astype(BF16)

    rope = pl.BlockSpec((tm, ROPE_PAD), lambda i: (i, 0))
    return _call(body, name, (seq // tm,),
                 [pl.BlockSpec((tm, d), lambda i: (i, 0)),
                  pl.BlockSpec((1, d), lambda i: (0, 0)),
                  pl.BlockSpec((d, wk), lambda i: (0, 0)),
                  pl.BlockSpec((1, KV_RANK), lambda i: (0, 0)),
                  pl.BlockSpec((N_HEADS, KV_RANK, NOPE + VDIM), lambda i: (0, 0, 0)), rope, rope],
                 (pl.BlockSpec((N_HEADS, tm, QK), lambda i: (0, i, 0)),
                  pl.BlockSpec((N_HEADS, tm, VDIM), lambda i: (0, i, 0)),
                  pl.BlockSpec((tm, KV_RANK), lambda i: (i, 0))),
                 (jax.ShapeDtypeStruct((N_HEADS, seq, QK), BF16),
                  jax.ShapeDtypeStruct((N_HEADS, seq, VDIM), BF16),
                  jax.ShapeDtypeStruct((seq, KV_RANK), F32)),
                 (h, g, wdkv, gkv, wukv, cos, sin), rider=rider)


def kv_bwd(dks, dvs, c_kv, h, g, dres, wdkv, gkv, wukv, cos, sin, name, rider=()):
    seq, d = h.shape
    tm = min(ROW_TILE, seq)
    n_i = seq // tm
    wk = KV_RANK + ROPE_PAD
    n_b = len(dks)

    def body(*refs):
        dk_refs = refs[:n_b]
        dv_refs = refs[n_b:2 * n_b]
        (c_ref, h_ref, g_ref, dr_ref, wdkv_ref, gkv_ref, wukv_ref, cos_ref, sin_ref,
         dh_ref, dwukv_ref, dwdkv_ref, dgkv_ref, dg_ref, acc_ukv, acc_dkv) = refs[2 * n_b:]
        i = pl.program_id(0)

        @pl.when(i == 0)
        def _():
            acc_ukv[...] = jnp.zeros_like(acc_ukv)
            acc_dkv[...] = jnp.zeros_like(acc_dkv)
            dgkv_ref[...] = jnp.zeros_like(dgkv_ref)
            dg_ref[...] = jnp.zeros_like(dg_ref)
        ckn, chat, crstd = _rms(c_ref[...], gkv_ref[...])
        cknb = ckn.astype(BF16)
        dckn = jnp.zeros((tm, KV_RANK), F32)
        dkr = jnp.zeros((tm, ROPE_PAD), F32)
        for hd in range(N_HEADS):
            dk = dk_refs[0][hd].astype(F32)
            dv = dv_refs[0][hd].astype(F32)
            for j in range(1, n_b):
                dk = dk + dk_refs[j][hd].astype(F32)
                dv = dv + dv_refs[j][hd].astype(F32)
            dkr = dkr + dk[:, NOPE:QK]
            dkvh = jnp.concatenate([dk[:, 0:NOPE].astype(BF16), dv.astype(BF16)], axis=1)
            dckn = dckn + _nt(dkvh, wukv_ref[hd])
            acc_ukv[hd] += _tn(cknb, dkvh)
        dc_kv, dg_rows = _rms_bwd(dckn, chat, crstd, gkv_ref[...])
        dgkv_ref[...] += jnp.sum(dg_rows, axis=0, keepdims=True)
        dkr_raw = _rope_bwd(dkr, cos_ref[...], sin_ref[...])
        dckv = jnp.concatenate([dc_kv.astype(BF16), dkr_raw.astype(BF16)], axis=1)
        xk, xhat, rstd = _rms(h_ref[...], g_ref[...])
        acc_dkv[...] += _tn(xk.astype(BF16), dckv)
        dh, dg_rows = _rms_bwd(_nt(dckv, wdkv_ref[...]), xhat, rstd, g_ref[...])
        dh_ref[...] = dr_ref[...] + dh
        dg_ref[...] += jnp.sum(dg_rows, axis=0, keepdims=True)

        @pl.when(i == n_i - 1)
        def _():
            dwukv_ref[...] = acc_ukv[...].astype(BF16)
            dwdkv_ref[...] = acc_dkv[...].astype(BF16)

    kspec = pl.BlockSpec((N_HEADS, tm, QK), lambda i: (0, i, 0))
    vspec = pl.BlockSpec((N_HEADS, tm, VDIM), lambda i: (0, i, 0))
    rope = pl.BlockSpec((tm, ROPE_PAD), lambda i: (i, 0))
    row = pl.BlockSpec((tm, d), lambda i: (i, 0))
    vec = pl.BlockSpec((1, d), lambda i: (0, 0))
    return _call(body, name, (n_i,),
                 [kspec] * n_b + [vspec] * n_b + [
                     pl.BlockSpec((tm, KV_RANK), lambda i: (i, 0)), row, vec, row,
                     pl.BlockSpec((d, wk), lambda i: (0, 0)),
                     pl.BlockSpec((1, KV_RANK), lambda i: (0, 0)),
                     pl.BlockSpec((N_HEADS, KV_RANK, NOPE + VDIM), lambda i: (0, 0, 0)), rope, rope],
                 (row,
                  pl.BlockSpec((N_HEADS, KV_RANK, NOPE + VDIM), lambda i: (0, 0, 0)),
                  pl.BlockSpec((d, wk), lambda i: (0, 0)),
                  pl.BlockSpec((1, KV_RANK), lambda i: (0, 0)), vec),
                 (jax.ShapeDtypeStruct((seq, d), F32),
                  jax.ShapeDtypeStruct((N_HEADS, KV_RANK, NOPE + VDIM), BF16),
                  jax.ShapeDtypeStruct((d, wk), BF16),
                  jax.ShapeDtypeStruct((1, KV_RANK), F32),
                  jax.ShapeDtypeStruct((1, d), F32)),
                 (*dks, *dvs, c_kv, h, g, dres, wdkv, gkv, wukv, cos, sin),
                 scratch=[pltpu.VMEM((N_HEADS, KV_RANK, NOPE + VDIM), F32), pltpu.VMEM((d, wk), F32)],
                 rider=rider)


def o_bwd(dh, o, wo, name, rider=()):
    seq, d = dh.shape
    hv = o.shape[1]
    tm = min(ROW_TILE, seq)
    n_i = seq // tm

    def body(dh_ref, o_ref, wo_ref, do_ref, dwo_ref, acc):
        i = pl.program_id(0)

        @pl.when(i == 0)
        def _():
            acc[...] = jnp.zeros_like(acc)
        dhb = dh_ref[...].astype(BF16)
        do_ref[...] = _nt(dhb, wo_ref[...]).astype(BF16)
        acc[...] += _tn(o_ref[...], dhb)

        @pl.when(i == n_i - 1)
        def _():
            dwo_ref[...] = acc[...].astype(BF16)

    return _call(body, name, (n_i,),
                 [pl.BlockSpec((tm, d), lambda i: (i, 0)),
                  pl.BlockSpec((tm, hv), lambda i: (i, 0)),
                  pl.BlockSpec((hv, d), lambda i: (0, 0))],
                 (pl.BlockSpec((tm, hv), lambda i: (i, 0)),
                  pl.BlockSpec((hv, d), lambda i: (0, 0))),
                 (jax.ShapeDtypeStruct((seq, hv), BF16), jax.ShapeDtypeStruct((hv, d), BF16)),
                 (dh, o, wo), scratch=[pltpu.VMEM((hv, d), F32)], rider=rider)


def _mask_diagonal(s):
    row = lax.broadcasted_iota(jnp.int32, s.shape, 0)
    col = lax.broadcasted_iota(jnp.int32, s.shape, 1)
    return jnp.where(col <= row, s, NEG_BIG)


def attn_fwd(q, k, v, name, rider=()):
    _, seq, _ = q.shape
    t = min(ATTN_TILE, seq // 2)
    n_pair = seq // (2 * t)

    def body(q_ref, k_ref, v_ref, o_ref, lse_ref):
        qi = pl.program_id(1)
        q_a = q_ref[0:t, :]
        q_b = q_ref[t:2 * t, :]

        def rows(j):
            return pl.ds(pl.multiple_of(j * t, t), t)

        def update(qx, kb, vb, state, diagonal=False):
            m, l, acc = state
            s = _nt(qx, kb)
            if diagonal:
                s = _mask_diagonal(s)
            m_new = jnp.maximum(m, jnp.max(s, axis=1, keepdims=True))
            p = jnp.exp(s - m_new)
            alpha = jnp.exp(m - m_new)
            l = alpha * l + jnp.sum(p, axis=1, keepdims=True)
            acc = alpha * acc + _nn(p.astype(BF16), vb)
            return m_new, l, acc

        def step(j, carry):
            both = pl.ds(pl.multiple_of(j * 2 * t, 2 * t), 2 * t)
            kb, vb = k_ref[both, :], v_ref[both, :]
            return update(q_a, kb, vb, carry[0:3]) + update(q_b, kb, vb, carry[3:6])

        init = (jnp.full((t, 1), NEG_BIG, F32), jnp.zeros((t, 1), F32), jnp.zeros((t, VDIM), F32))
        carry = lax.fori_loop(0, qi, step, init + init)
        k0, v0 = k_ref[rows(2 * qi), :], v_ref[rows(2 * qi), :]
        k1, v1 = k_ref[rows(2 * qi + 1), :], v_ref[rows(2 * qi + 1), :]
        state_a = update(q_a, k0, v0, carry[0:3], diagonal=True)
        state_b = update(q_b, k1, v1, update(q_b, k0, v0, carry[3:6]), diagonal=True)
        for half, (m, l, acc) in enumerate((state_a, state_b)):
            o_ref[half * t:(half + 1) * t, :] = (acc / l).astype(BF16)
            lse_ref[half * t:(half + 1) * t, :] = jnp.broadcast_to(m + jnp.log(l), (t, LANES))

    return _call(body, name, (N_HEADS, n_pair),
                 [pl.BlockSpec((None, 2 * t, QK), lambda h, i: (h, i, 0)),
                  pl.BlockSpec((None, seq, QK), lambda h, i: (h, 0, 0)),
                  pl.BlockSpec((None, seq, VDIM), lambda h, i: (h, 0, 0))],
                 (pl.BlockSpec((2 * t, VDIM), lambda h, i: (i, h)),
                  pl.BlockSpec((None, 2 * t, LANES), lambda h, i: (h, i, 0))),
                 (jax.ShapeDtypeStruct((seq, N_HEADS * VDIM), BF16),
                  jax.ShapeDtypeStruct((N_HEADS, seq, LANES), F32)),
                 (q, k, v), rider=rider)


def attn_bwd(q, k, v, o, do, lse, name, rider=()):
    _, seq, _ = q.shape
    t = min(ATTN_TILE, seq // 2)
    n_q = seq // t
    n_pair = n_q // 2

    def body(q_ref, k_ref, v_ref, o_ref, do_ref, lse_ref, dq_ref, dk_ref, dv_ref,
             dq_acc, dk_acc, dv_acc):
        kj = pl.program_id(1)

        @pl.when(kj == 0)
        def _():
            dq_acc[...] = jnp.zeros_like(dq_acc)
        halves = (slice(0, t), slice(t, 2 * t))

        def block(i, masks, n_rows=t):
            rows = pl.ds(pl.multiple_of(i * n_rows, n_rows), n_rows)
            qb = q_ref[rows, :]
            dob = do_ref[rows, :]
            lse_col = lse_ref[rows, 0:1]
            delta = jnp.sum(dob.astype(F32) * o_ref[rows, :].astype(F32), axis=1, keepdims=True)
            dq, out = None, {}
            for x, diagonal in enumerate(masks):
                if diagonal is None:
                    continue
                kb, vb = k_ref[halves[x], :], v_ref[halves[x], :]
                s = _nt(qb, kb)
                if diagonal:
                    s = _mask_diagonal(s)
                p = jnp.exp(s - lse_col)
                ds = (p * (_nt(dob, vb) - delta)).astype(BF16)
                out[x] = (_tn(p.astype(BF16), dob), _tn(ds, qb))
                part = _nn(ds, kb)
                dq = part if dq is None else dq + part
            dq_acc[rows, :] += dq
            return out

        first = block(2 * kj, (True, None))
        second = block(2 * kj + 1, (False, True))
        dv_acc[halves[0], :] = first[0][0] + second[0][0]
        dk_acc[halves[0], :] = first[0][1] + second[0][1]
        dv_acc[halves[1], :] = second[1][0]
        dk_acc[halves[1], :] = second[1][1]

        def step(i, carry):
            out = block(i, (False, False), n_rows=2 * t)
            for x in (0, 1):
                dv_acc[halves[x], :] += out[x][0]
                dk_acc[halves[x], :] += out[x][1]
            return carry

        lax.fori_loop(kj + 1, n_pair, step, 0)
        dk_ref[...] = dk_acc[...].astype(BF16)
        dv_ref[...] = dv_acc[...].astype(BF16)

        @pl.when(kj == n_pair - 1)
        def _():
            dq_ref[...] = dq_acc[...].astype(BF16)

    head_rows = pl.BlockSpec((seq, VDIM), lambda h, j: (0, h))
    return _call(body, name, (N_HEADS, n_pair),
                 [pl.BlockSpec((None, seq, QK), lambda h, j: (h, 0, 0)),
                  pl.BlockSpec((None, 2 * t, QK), lambda h, j: (h, j, 0)),
                  pl.BlockSpec((None, 2 * t, VDIM), lambda h, j: (h, j, 0)),
                  head_rows, head_rows,
                  pl.BlockSpec((None, seq, LANES), lambda h, j: (h, 0, 0))],
                 (pl.BlockSpec((None, seq, QK), lambda h, j: (h, 0, 0)),
                  pl.BlockSpec((None, 2 * t, QK), lambda h, j: (h, j, 0)),
                  pl.BlockSpec((None, 2 * t, VDIM), lambda h, j: (h, j, 0))),
                 (jax.ShapeDtypeStruct((N_HEADS, seq, QK), BF16),
                  jax.ShapeDtypeStruct((N_HEADS, seq, QK), BF16),
                  jax.ShapeDtypeStruct((N_HEADS, seq, VDIM), BF16)),
                 (q, k, v, o, do, lse),
                 scratch=[pltpu.VMEM((seq, QK), F32), pltpu.VMEM((2 * t, QK), F32),
                          pltpu.VMEM((2 * t, VDIM), F32)], rider=rider)


def loss_head(h, g, target, name):
    seq, d = h.shape
    tm = min(ROW_TILE, seq)

    def body(h_ref, g_ref, t_ref, l_ref, dh_ref, dg_ref):
        i = pl.program_id(0)

        @pl.when(i == 0)
        def _():
            l_ref[...] = jnp.zeros_like(l_ref)
            dg_ref[...] = jnp.zeros_like(dg_ref)
        y, xhat, rstd = _rms(h_ref[...], g_ref[...])
        diff = y - t_ref[...]
        l_ref[...] += jnp.sum(jnp.sum(diff * diff, axis=1, keepdims=True), axis=0, keepdims=True)
        dh, dg_rows = _rms_bwd(diff * (1.0 / d), xhat, rstd, g_ref[...])
        dh_ref[...] = dh
        dg_ref[...] += jnp.sum(dg_rows, axis=0, keepdims=True)

    row = pl.BlockSpec((tm, d), lambda i: (i, 0))
    vec = pl.BlockSpec((1, d), lambda i: (0, 0))
    return _call(body, name, (seq // tm,), [row, vec, row],
                 (pl.BlockSpec((1, LANES), lambda i: (0, 0)), row, vec),
                 (jax.ShapeDtypeStruct((1, LANES), F32), jax.ShapeDtypeStruct((seq, d), F32),
                  jax.ShapeDtypeStruct((1, d), F32)),
                 (h, g, target))[0]


def _pack(parts):
    rows = []
    for p in parts:
        flat = p.reshape(-1)
        n_rows = -(-flat.shape[0] // (8 * LANES)) * 8
        flat = jnp.pad(flat, (0, n_rows * LANES - flat.shape[0]))
        rows.append(flat.reshape(n_rows, LANES))
    return jnp.concatenate(rows, axis=0)


def _unpack(packed, shapes):
    lead = packed.shape[:-2]
    out, r0 = [], 0
    for shape in shapes:
        size = 1
        for s in shape:
            size *= s
        n_rows = -(-size // (8 * LANES)) * 8
        part = packed[..., r0:r0 + n_rows, :].reshape(lead + (n_rows * LANES,))
        out.append(part[..., :size].reshape(lead + tuple(shape)))
        r0 += n_rows
    return out


FWD_RIDERS = {
    "mixer_fwd0": [("ffn_w_up", 0)],
    "ffn_fwd0": [("ffn_w_down", 0), ("a_w_in", 1), ("a_w_out", 1)],
    "mixer_fwd1": [("ffn_w_up", 1)],
    "ffn_fwd1": [("ffn_w_down", 1), ("w_dkv", 0), ("w_ukv", 0), ("b_w_dq", 0), ("b_w_uq", 0)],
    "attn_fwd0": [("b_w_o", 0), ("ffn_w_up", 2), ("ffn_w_down", 2), ("b_w_dq", 1), ("b_w_uq", 1)],
    "attn_fwd1": [("b_w_o", 1), ("ffn_w_up", 3), ("ffn_w_down", 3)],
}
BWD_RIDERS = {
    "attn_bwd1": [("ffn_w_down", 3), ("ffn_w_up", 3), ("b_w_o", 1)],
    "attn_bwd0": [("ffn_w_down", 2), ("ffn_w_up", 2), ("b_w_o", 0)],
    "ffn_bwd1": [("b_w_uq", 1), ("b_w_dq", 1), ("b_w_uq", 0), ("b_w_dq", 0), ("w_ukv", 0), ("w_dkv", 0)],
    "ffn_in_bwd1": [("ffn_w_down", 1), ("ffn_w_up", 1, "pair")],
    "ffn_bwd0": [("ffn_w_up", 1, "chip"), ("a_w_in", 1), ("a_w_out", 1)],
    "ffn_in_bwd0": [("ffn_w_down", 0), ("ffn_w_up", 0, "pair")],
    "mixer_bwd0": [("ffn_w_up", 0, "chip")],
    "mixer_in_bwd0": [("a_w_out", 0), ("a_w_in", 0, "pair")],
    "adamw_a_w_out": [("a_w_in", 0, "chip")],
}


def kernel(x, a_mix_norm, a_w_in, a_conv, a_w_out, b_mix_norm, b_w_dq, b_q_norm, b_w_uq, b_w_o, kv_in_norm, w_dkv, kv_norm, w_ukv, ffn_norm, ffn_w_up, ffn_conv, ffn_w_down, final_norm, loss_target, m_a_mix_norm, m_a_w_in, m_a_conv, m_a_w_out, m_b_mix_norm, m_b_w_dq, m_b_q_norm, m_b_w_uq, m_b_w_o, m_kv_in_norm, m_w_dkv, m_kv_norm, m_w_ukv, m_ffn_norm, m_ffn_w_up, m_ffn_conv, m_ffn_w_down, m_final_norm, v_a_mix_norm, v_a_w_in, v_a_conv, v_a_w_out, v_b_mix_norm, v_b_w_dq, v_b_q_norm, v_b_w_uq, v_b_w_o, v_kv_in_norm, v_w_dkv, v_kv_norm, v_w_ukv, v_ffn_norm, v_ffn_w_up, v_ffn_conv, v_ffn_w_down, v_final_norm):
    seq, d = x.shape[1], x.shape[2]
    me = 4 * lax.axis_index("x") + 2 * lax.axis_index("y") + lax.axis_index("c")
    h0 = x.reshape(seq, d)
    target = loss_target.reshape(seq, d)
    cos, sin = _rope_tables(seq)
    rank = b_w_dq.shape[-1]
    f8 = ffn_w_up.shape[-1]
    fd = ffn_w_down.shape[1]
    dshard = a_w_out.shape[1]
    hv = N_HEADS * VDIM

    shards = {"a_w_in": a_w_in, "a_w_out": a_w_out, "b_w_dq": b_w_dq, "b_w_uq": b_w_uq,
              "b_w_o": b_w_o, "w_dkv": w_dkv[None], "w_ukv": w_ukv[None],
              "ffn_w_up": ffn_w_up, "ffn_w_down": ffn_w_down}

    def relayout(name, g):
        if name == "a_w_in":
            w = jnp.transpose(g, (1, 0, 2)).reshape(d, 3, d)
            return jnp.transpose(w, (1, 0, 2))
        if name == "a_w_out":
            return g.reshape(d, d)
        if name == "b_w_dq":
            return g.reshape(d, rank)
        if name == "b_w_o":
            return g.reshape(hv, d)
        if name == "w_dkv":
            return g.reshape(d, KV_RANK + ROPE_PAD)
        if name == "ffn_w_down":
            return g.reshape(N_DEV // 2, 2 * fd, d)
        return g

    weights = {}

    shards_bf16 = {n: w.astype(BF16) for n, w in shards.items()}
    shards_bf16["b_w_uq"] = jnp.pad(shards_bf16["b_w_uq"], ((0, 0), (0, 0), (0, QK - NOPE - ROPE)))
    shards_bf16["w_dkv"] = jnp.pad(shards_bf16["w_dkv"], ((0, 0), (0, 0), (0, ROPE_PAD - ROPE)))

    def ag_rider(host):
        return [(("ag", l), shards_bf16[n]) for n, l in FWD_RIDERS.get(host, [])]

    def ag_done(host, outs):
        for (n, l), g in zip(FWD_RIDERS.get(host, []), outs):
            weights[n, l] = relayout(n, g)

    small_shapes = [a_mix_norm.shape, a_conv.shape, ffn_conv.shape]
    first = exchange([(("ag", 0), shards_bf16["a_w_in"]), (("ag", 0), shards_bf16["a_w_out"]),
                      ("ag", _pack([a_mix_norm, a_conv, ffn_conv]))], "ag_first")
    weights["a_w_in", 0] = relayout("a_w_in", first[0])
    weights["a_w_out", 0] = relayout("a_w_out", first[1])
    s_mix, s_aconv, s_fconv = _unpack(first[2], small_shapes)
    a_gain = jnp.transpose(s_mix, (1, 0, 2)).reshape(N_A, d)
    a_cw = jnp.transpose(s_aconv, (1, 2, 0, 3)).reshape(N_A, 3, d)
    f_cw = jnp.transpose(s_fconv, (1, 0, 2, 3))

    def mixer_gain(layer):
        if layer >= DEPTH:
            return None
        return a_gain[layer][None] if layer < N_A else b_mix_norm[layer - N_A][None]

    saved = {}
    h = h0
    xn = norm_fwd(h, mixer_gain(0), "norm_first")
    kv = None
    for layer in range(DEPTH):
        saved["hm", layer], saved["xm", layer] = h, xn
        if layer < N_A:
            name = f"mixer_fwd{layer}"
            (u4, z), r = mixer_fwd(xn, weights["a_w_in", layer], a_cw[layer], name, rider=ag_rider(name))
            ag_done(name, r)
            saved["mix", layer] = (u4, z)
            name = f"mixer_out{layer}"
            (h, xn), r = proj_residual(z[None], weights["a_w_out", layer][None], h, name,
                                       g_next=ffn_norm[layer][None], rider=ag_rider(name))
            ag_done(name, r)
        else:
            j = layer - N_A
            name = f"q_fwd{j}"
            (q,), r = q_fwd(xn, weights["b_w_dq", j], b_q_norm[j][None], weights["b_w_uq", j],
                            cos, sin, name, rider=ag_rider(name))
            ag_done(name, r)
            name = f"attn_fwd{j}"
            (o, lse), r = attn_fwd(q, kv[0], kv[1], name, rider=ag_rider(name))
            ag_done(name, r)
            saved["attn", layer] = (q, o, lse)
            name = f"attn_out{j}"
            (h, xn), r = proj_residual(o[None], weights["b_w_o", j][None], h, name,
                                       g_next=ffn_norm[layer][None], rider=ag_rider(name))
            ag_done(name, r)
        saved["hf", layer], saved["xf", layer] = h, xn
        name = f"ffn_fwd{layer}"
        (up2, cv2, act), r = ffn_fwd(xn, weights["ffn_w_up", layer], f_cw[layer], name, rider=ag_rider(name))
        ag_done(name, r)
        saved["ffn", layer] = (up2, cv2, act)
        name = f"ffn_out{layer}"
        (h, xn), r = proj_residual(act, weights["ffn_w_down", layer], h, name,
                                   g_next=mixer_gain(layer + 1), rider=ag_rider(name))
        ag_done(name, r)
        if layer == N_A - 1:
            (k_all, v_all, c_kv), r = kv_fwd(h, kv_in_norm[None], weights["w_dkv", 0], kv_norm[None],
                                             weights["w_ukv", 0], cos, sin, "kv_fwd",
                                             rider=ag_rider("kv_fwd"))
            ag_done("kv_fwd", r)
            kv = (k_all, v_all, c_kv)

    sq_err, dh, d_final = loss_head(h, final_norm[None], target, "loss_head")
    loss = lax.psum(sq_err[0, 0] * (0.5 / d), ("x", "y", "c"))

    grads = {}
    parts = {}

    pair_sums = {}

    def by_chip(g):
        return g.reshape((N_DEV // 2, 2) + g.shape[1:])

    def rs_rider(host):
        tasks = []
        for key in BWD_RIDERS.get(host, []):
            if len(key) == 2:
                tasks.append(("rs", grads[key]))
            elif key[2] == "pair":
                tasks.append(("rs_pair", by_chip(grads[key[:2]])))
            else:
                tasks.append(("rs_chip", pair_sums[key[:2]]))
        return tasks

    def rs_done(host, outs):
        for key, p in zip(BWD_RIDERS.get(host, []), outs):
            if len(key) == 3 and key[2] == "pair":
                pair_sums[key[:2]] = pair_sum(by_chip(grads[key[:2]]), p, f"pair_sum_{key[0]}{key[1]}")
            else:
                parts[key[:2]] = p

    d_ffn_norm = [None] * DEPTH
    d_fconv = [None] * DEPTH
    d_a_gain = [None] * N_A
    d_aconv = [None] * N_A
    d_b_gain = [None] * N_B
    d_q_gain = [None] * N_B
    dks, dvs = [], []
    for layer in reversed(range(DEPTH)):
        if layer == N_A - 1:
            hk = saved["hm", layer + 1]
            (dh, dwukv, dwdkv, d_kv_gain, d_kvin_gain), r = kv_bwd(
                dks, dvs, kv[2], hk, kv_in_norm[None], dh, weights["w_dkv", 0], kv_norm[None],
                weights["w_ukv", 0], cos, sin, "kv_bwd", rider=rs_rider("kv_bwd"))
            rs_done("kv_bwd", r)
            grads["w_ukv", 0] = dwukv
            grads["w_dkv", 0] = dwdkv[:, :KV_RANK + ROPE].reshape(N_DEV, dshard, KV_RANK + ROPE)
        up2, cv2, act = saved["ffn", layer]
        name = f"ffn_bwd{layer}"
        (dup2, dwup, dwdown, dcw), r = ffn_bwd(dh, weights["ffn_w_down", layer], up2, cv2, act,
                                               saved["xf", layer], f_cw[layer], name, rider=rs_rider(name))
        rs_done(name, r)
        grads["ffn_w_up", layer] = dwup.reshape(N_DEV, d, f8)
        grads["ffn_w_down", layer] = dwdown.reshape(N_DEV, fd, d)
        d_fconv[layer] = dcw.reshape(N_DEV, 3, f8)
        name = f"ffn_in_bwd{layer}"
        (dh, d_ffn_norm[layer]), r = proj_t_rms_bwd(dup2.reshape(N_DEV, seq, f8), weights["ffn_w_up", layer],
                                                    saved["hf", layer], ffn_norm[layer][None], dh, name,
                                                    rider=rs_rider(name))
        rs_done(name, r)
        hm, xm = saved["hm", layer], saved["xm", layer]
        if layer < N_A:
            u4, z = saved["mix", layer]
            name = f"mixer_bwd{layer}"
            (du3, dwin3, dwout, dcw), r = mixer_bwd(dh, weights["a_w_out", layer], u4, z, xm, a_cw[layer],
                                                    name, rider=rs_rider(name))
            rs_done(name, r)
            dwin = jnp.transpose(dwin3, (1, 0, 2)).reshape(d, N_DEV, 3 * d // N_DEV)
            grads["a_w_in", layer] = jnp.transpose(dwin, (1, 0, 2))
            grads["a_w_out", layer] = dwout.reshape(N_DEV, dshard, d)
            d_aconv[layer] = dcw
            name = f"mixer_in_bwd{layer}"
            extra = []
            if layer == 0:
                early_small = [
                    d_a_gain[1],
                    d_aconv[1],
                    jnp.concatenate(d_b_gain, axis=0),
                    jnp.concatenate(d_q_gain, axis=0),
                    d_kvin_gain[0],
                    d_kv_gain[0],
                    jnp.concatenate(d_ffn_norm, axis=0),
                    jnp.stack(d_fconv),
                    d_final[0],
                ]
                extra = [("ag", _pack(early_small))]
            (dh, d_a_gain[layer]), r = proj_t_rms_bwd(du3, weights["a_w_in", layer], hm, a_gain[layer][None],
                                                      dh, name, rider=rs_rider(name) + extra)
            rs_done(name, r)
            if layer == 0:
                g_early = r[-1]
        else:
            j = layer - N_A
            q, o, lse = saved["attn", layer]
            name = f"attn_out_bwd{j}"
            (do, dwo), r = o_bwd(dh, o, weights["b_w_o", j], name, rider=rs_rider(name))
            rs_done(name, r)
            grads["b_w_o", j] = dwo.reshape(N_DEV, dshard, d)
            name = f"attn_bwd{j}"
            (dq, dk, dv), r = attn_bwd(q, kv[0], kv[1], o, do, lse, name, rider=rs_rider(name))
            rs_done(name, r)
            dks.append(dk)
            dvs.append(dv)
            name = f"q_bwd{j}"
            (dh, dwuq, dwdq, d_q_gain[j], d_b_gain[j]), r = q_bwd(
                dq, xm, hm, b_mix_norm[j][None], dh, weights["b_w_dq", j], b_q_norm[j][None],
                weights["b_w_uq", j], cos, sin, name, rider=rs_rider(name))
            rs_done(name, r)
            grads["b_w_uq", j] = dwuq[:, :, :NOPE + ROPE]
            grads["b_w_dq", j] = dwdq.reshape(N_DEV, dshard, rank)
    grad_x = dh.reshape(x.shape)

    late_small = [d_a_gain[0], d_aconv[0]]
    full_shapes = [t.shape for t in early_small + late_small]
    small_pack = _pack(late_small)

    res = {}

    def update(name, n_layers, w, m, v, extra=(), transposed=False):
        view = (lambda t: jnp.transpose(t, (0, 2, 1))) if transposed else (lambda t: t)
        call = sum_adamw_transposed if transposed else sum_adamw
        shard = w.shape if w.ndim == 3 else (1,) + w.shape
        host = f"adamw_{name}"
        outs, r = call([parts[name, l] for l in range(n_layers)], view(w.reshape(shard)),
                       view(m.reshape(shard)), view(v.reshape(shard)), host,
                       rider=rs_rider(host) + list(extra))
        rs_done(host, r)
        res[name] = [view(t).reshape(w.shape) for t in outs]
        return r[len(BWD_RIDERS.get(host, [])):]

    (g_late,) = update("a_w_out", N_A, a_w_out, m_a_w_out, v_a_w_out, extra=[("ag", small_pack)])
    update("ffn_w_down", DEPTH, ffn_w_down, m_ffn_w_down, v_ffn_w_down)
    update("ffn_w_up", DEPTH, ffn_w_up, m_ffn_w_up, v_ffn_w_up, transposed=True)
    update("b_w_dq", N_B, b_w_dq, m_b_w_dq, v_b_w_dq)
    update("b_w_uq", N_B, b_w_uq, m_b_w_uq, v_b_w_uq)
    update("b_w_o", N_B, b_w_o, m_b_w_o, v_b_w_o)
    update("w_dkv", 1, w_dkv, m_w_dkv, v_w_dkv)
    update("w_ukv", 1, w_ukv, m_w_ukv, v_w_ukv)
    update("a_w_in", N_A, a_w_in, m_a_w_in, v_a_w_in)

    summed = sum_slots(jnp.concatenate([g_early, g_late], axis=1), "sum_small_grads")
    (s_a_gain1, s_aconv1, s_b_gain, s_q_gain, s_kvin, s_kvn, s_ffn_gain, s_fconv_g,
     s_final, s_a_gain0, s_aconv0) = _unpack(summed, full_shapes)
    s_a_gain = jnp.concatenate([s_a_gain0, s_a_gain1], axis=0)
    s_aconv_g = jnp.stack([s_aconv0, s_aconv1])
    dsl = d // N_DEV
    small = [
        ("a_mix_norm", lax.dynamic_slice_in_dim(s_a_gain, me * dsl, dsl, axis=1), a_mix_norm, m_a_mix_norm, v_a_mix_norm),
        ("a_conv", lax.dynamic_slice_in_dim(s_aconv_g, me * dsl, dsl, axis=2), a_conv, m_a_conv, v_a_conv),
        ("b_mix_norm", s_b_gain, b_mix_norm, m_b_mix_norm, v_b_mix_norm),
        ("b_q_norm", s_q_gain, b_q_norm, m_b_q_norm, v_b_q_norm),
        ("kv_in_norm", s_kvin, kv_in_norm, m_kv_in_norm, v_kv_in_norm),
        ("kv_norm", s_kvn, kv_norm, m_kv_norm, v_kv_norm),
        ("ffn_norm", s_ffn_gain, ffn_norm, m_ffn_norm, v_ffn_norm),
        ("ffn_conv", lax.dynamic_index_in_dim(s_fconv_g, me, axis=1, keepdims=False), ffn_conv, m_ffn_conv, v_ffn_conv),
        ("final_norm", s_final, final_norm, m_final_norm, v_final_norm),
    ]
    shapes = [t[2].shape for t in small]
    packed = [_pack([t[k] for t in small])[None] for k in (1, 2, 3, 4)]
    outs, _ = sum_adamw([packed[0]], packed[1], packed[2], packed[3], "adamw_small")
    unpacked = [_unpack(t[0], shapes) for t in outs]
    for idx, t in enumerate(small):
        res[t[0]] = [unpacked[k][idx] for k in range(4)]

    order = ["a_mix_norm", "a_w_in", "a_conv", "a_w_out", "b_mix_norm", "b_w_dq", "b_q_norm",
             "b_w_uq", "b_w_o", "kv_in_norm", "w_dkv", "kv_norm", "w_ukv", "ffn_norm",
             "ffn_w_up", "ffn_conv", "ffn_w_down", "final_norm"]
    return (loss, grad_x, *[res[n][0] for n in order], *[res[n][1] for n in order],
            *[res[n][2] for n in order], *[res[n][3] for n in order])
```

```python
import functools

import numpy as np
import jax
import jax.numpy as jnp
from jax import lax
from jax.experimental import pallas as pl
from jax.experimental.pallas import tpu as pltpu

F32 = jnp.float32
BF16 = jnp.bfloat16

N_DEV = 8
N_HEADS = 8
NOPE = 128
ROPE = 64
ROPE_PAD = 128
QK = NOPE + ROPE_PAD
VDIM = 128
KV_RANK = 256
ROPE_THETA = 10000.0
RMS_EPS = 1e-6
ATTN_SCALE = (NOPE + ROPE) ** -0.5
N_A = 2
N_B = 2
DEPTH = 4

ADAM_LR = 0.001
ADAM_B1 = 0.9
ADAM_B2 = 0.999
ADAM_EPS = 1e-08
ADAM_WD = 0.01
ADAM_STEP = 10

V7X_VMEM_LIMIT = 56 * 1024 * 1024
BF16_SUBLANES = 16
ROW_TILE = 512
ROW_TILE_LARGE = 1024
ADAM_ROWS = 256
ATTN_TILE = 512
MIXER_CHUNK = 512
LANES = 128
NEG_BIG = -1e30
COPIES_PER_TASK = 7

MESH_ID = pl.DeviceIdType.MESH
ANY = pl.BlockSpec(memory_space=pl.ANY)


def _nt(a, b):
    return lax.dot_general(a, b, (((1,), (1,)), ((), ())), preferred_element_type=F32)


def _tn(a, b):
    return lax.dot_general(a, b, (((0,), (0,)), ((), ())), preferred_element_type=F32)


def _nn(a, b):
    return jnp.dot(a, b, preferred_element_type=F32)


def _rms(h, g):
    rstd = lax.rsqrt(jnp.mean(h * h, axis=-1, keepdims=True) + RMS_EPS)
    xhat = h * rstd
    return xhat * g, xhat, rstd


def _rms_bwd(dxn, xhat, rstd, g):
    dxhat = dxn * g
    dh = rstd * (dxhat - xhat * jnp.mean(dxhat * xhat, axis=-1, keepdims=True))
    return dh, dxn * xhat


def _shift_down(x, k, halo_rows):
    r = pltpu.roll(x, k, 0)
    row = lax.broadcasted_iota(jnp.int32, x.shape, 0)
    for t in range(k):
        r = jnp.where(row == t, halo_rows[t], r)
    return r


def _shift_up(x, k, halo_rows):
    n = x.shape[0]
    r = pltpu.roll(x, n - k, 0)
    row = lax.broadcasted_iota(jnp.int32, x.shape, 0)
    for t in range(k):
        r = jnp.where(row == n - k + t, halo_rows[t], r)
    return r


def _conv_taps(w_ref):
    return w_ref[0:1, :], w_ref[1:2, :], w_ref[2:3, :]


def _rope_swap(x):
    lane = lax.broadcasted_iota(jnp.int32, x.shape, 1)
    return jnp.where(lane < ROPE // 2, pltpu.roll(x, ROPE_PAD - ROPE // 2, 1),
                     pltpu.roll(x, ROPE // 2, 1))


def _rope_fwd(x, cos, sin):
    return x * cos + _rope_swap(x) * sin


def _rope_bwd(dy, cos, sin):
    return dy * cos - _rope_swap(dy) * sin


def _rope_tables(seq):
    inv = (1.0 / np.power(np.float32(ROPE_THETA), np.arange(0, ROPE, 2, dtype=np.float32) / np.float32(ROPE)))
    ang = np.arange(seq, dtype=np.float32)[:, None] * inv.astype(np.float32)[None, :]
    ang = ang.astype(np.float32).astype(np.float64)
    cos, sin = np.cos(ang).astype(np.float32), np.sin(ang).astype(np.float32)
    zero = np.zeros((seq, ROPE_PAD - ROPE), np.float32)
    return (jnp.asarray(np.concatenate([cos, cos, zero], axis=1)),
            jnp.asarray(np.concatenate([-sin, sin, zero], axis=1)))


def _row_tile(rows, cap, mult=8):
    best = None
    for t in range(mult, min(rows, cap) + 1, mult):
        if rows % t == 0:
            best = t
    return rows if best is None else best


class _AllGatherTask:
    def __init__(self, t, x_ref, out_ref, send_sems, recv_sems, local_sems):
        self.t, self.x_ref, self.out_ref = t, x_ref, out_ref
        self.send_sems, self.recv_sems, self.local_sems = send_sems, recv_sems, local_sems
        mx, my, mc = lax.axis_index("x"), lax.axis_index("y"), lax.axis_index("c")
        self.mc = mc
        self.me, self.sibling = (mx, my, mc), (mx, my, 1 - mc)
        self.chips = [(1 - mx, my), (mx, 1 - my), (1 - mx, 1 - my)]

    def _slot(self, px, py, pc):
        return self.out_ref.at[4 * px + 2 * py + pc]

    def _copy(self, k, block, to, src=None):
        s = COPIES_PER_TASK * self.t + k
        return pltpu.make_async_remote_copy(
            src_ref=self._slot(*block) if src is None else src, dst_ref=self._slot(*block),
            send_sem=self.send_sems.at[s], recv_sem=self.recv_sems.at[s],
            device_id=to, device_id_type=MESH_ID)

    def _mine(self):
        return pltpu.make_async_copy(self.x_ref, self._slot(*self.me), self.local_sems.at[self.t])

    def _first(self):
        out = [self._copy(0, self.me, self.sibling, src=self.x_ref)]
        out += [self._copy(1 + j, self.me, (*chip, self.mc), src=self.x_ref)
                for j, chip in enumerate(self.chips)]
        return out

    def _passed(self):
        return [self._copy(4 + j, (*chip, self.mc), self.sibling) for j, chip in enumerate(self.chips)]

    def start(self):
        self._mine().start()
        for cp in self._first():
            cp.start()

    def forward(self):
        passed = self._passed()
        for j, chip in enumerate(self.chips):
            self._copy(1 + j, (*chip, self.mc), self.me).wait_recv()
            passed[j].start()

    def finish(self):
        self._copy(0, self.sibling, self.me).wait_recv()
        for j, chip in enumerate(self.chips):
            self._copy(4 + j, (*chip, 1 - self.mc), self.me).wait_recv()
        for cp in self._first() + self._passed():
            cp.wait_send()
        self._mine().wait()


class _ReduceScatterTask:
    def __init__(self, t, g_ref, out_ref, send_sems, recv_sems, local_sems):
        self.t, self.g_ref, self.out_ref = t, g_ref, out_ref
        self.send_sems, self.recv_sems, self.local_sems = send_sems, recv_sems, local_sems
        mx, my, mc = lax.axis_index("x"), lax.axis_index("y"), lax.axis_index("c")
        self.me = 4 * mx + 2 * my + mc
        self.peers = []
        for k in range(1, N_DEV):
            px, py, pc = mx ^ ((k >> 2) & 1), my ^ ((k >> 1) & 1), mc ^ (k & 1)
            self.peers.append(((px, py, pc), 4 * px + 2 * py + pc))

    def _mine(self):
        return pltpu.make_async_copy(self.g_ref.at[self.me], self.out_ref.at[self.me],
                                     self.local_sems.at[self.t])

    def _copy(self, k, src_slot, dst_slot):
        s = COPIES_PER_TASK * self.t + k
        return pltpu.make_async_remote_copy(
            src_ref=self.g_ref.at[src_slot], dst_ref=self.out_ref.at[dst_slot],
            send_sem=self.send_sems.at[s], recv_sem=self.recv_sems.at[s],
            device_id=self.peers[k][0], device_id_type=MESH_ID)

    def start(self):
        self._mine().start()
        for k, (_, peer) in enumerate(self.peers):
            self._copy(k, peer, self.me).start()

    def forward(self):
        pass

    def finish(self):
        for k, (_, peer) in enumerate(self.peers):
            self._copy(k, self.me, peer).wait_recv()
        for k, (_, peer) in enumerate(self.peers):
            self._copy(k, peer, self.me).wait_send()
        self._mine().wait()


class _PairExchangeTask:
    def __init__(self, t, g_ref, out_ref, send_sems, recv_sems, local_sems):
        mx, my, mc = lax.axis_index("x"), lax.axis_index("y"), lax.axis_index("c")
        s = COPIES_PER_TASK * t
        self.copy = pltpu.make_async_remote_copy(
            src_ref=g_ref.at[:, 1 - mc], dst_ref=out_ref,
            send_sem=send_sems.at[s], recv_sem=recv_sems.at[s],
            device_id=(mx, my, 1 - mc), device_id_type=MESH_ID)

    def start(self):
        self.copy.start()

    def forward(self):
        pass

    def finish(self):
        self.copy.wait()


class _ChipScatterTask:
    def __init__(self, t, s_ref, out_ref, send_sems, recv_sems, local_sems):
        self.t, self.s_ref, self.out_ref = t, s_ref, out_ref
        self.send_sems, self.recv_sems, self.local_sems = send_sems, recv_sems, local_sems
        mx, my, mc = lax.axis_index("x"), lax.axis_index("y"), lax.axis_index("c")
        self.chip = 2 * mx + my
        self.peers = []
        for k in range(1, N_DEV // 2):
            px, py = mx ^ ((k >> 1) & 1), my ^ (k & 1)
            self.peers.append(((px, py, mc), 2 * px + py))

    def _mine(self):
        return pltpu.make_async_copy(self.s_ref.at[self.chip], self.out_ref.at[self.chip],
                                     self.local_sems.at[self.t])

    def _copy(self, k, src_slot, dst_slot):
        s = COPIES_PER_TASK * self.t + k
        return pltpu.make_async_remote_copy(
            src_ref=self.s_ref.at[src_slot], dst_ref=self.out_ref.at[dst_slot],
            send_sem=self.send_sems.at[s], recv_sem=self.recv_sems.at[s],
            device_id=self.peers[k][0], device_id_type=MESH_ID)

    def start(self):
        self._mine().start()
        for k, (_, peer) in enumerate(self.peers):
            self._copy(k, peer, self.chip).start()

    def forward(self):
        pass

    def finish(self):
        for k, (_, peer) in enumerate(self.peers):
            self._copy(k, self.chip, peer).wait_recv()
        for k, (_, peer) in enumerate(self.peers):
            self._copy(k, peer, self.chip).wait_send()
        self._mine().wait()


_TASKS = {"ag": _AllGatherTask, "rs": _ReduceScatterTask, "rs_pair": _PairExchangeTask,
          "rs_chip": _ChipScatterTask}


def _task_shape(kind, arr):
    if isinstance(kind, tuple):
        return jax.ShapeDtypeStruct((N_DEV,) + arr.shape[1:], arr.dtype)
    shape = {"ag": (N_DEV,) + arr.shape, "rs": arr.shape, "rs_chip": arr.shape,
             "rs_pair": arr.shape[:1] + arr.shape[2:]}[kind]
    return jax.ShapeDtypeStruct(shape, arr.dtype)


def _sem_shapes(n_tasks):
    return [pltpu.SemaphoreType.DMA((COPIES_PER_TASK * n_tasks,)),
            pltpu.SemaphoreType.DMA((COPIES_PER_TASK * n_tasks,)),
            pltpu.SemaphoreType.DMA((n_tasks,))]


def _make_tasks(rider, in_refs, out_refs, sems):
    tasks = []
    for t, (kind, _) in enumerate(rider):
        if isinstance(kind, tuple):
            tasks.append(_AllGatherTask(t, in_refs[t].at[kind[1]], out_refs[t], *sems))
        else:
            tasks.append(_TASKS[kind](t, in_refs[t], out_refs[t], *sems))
    return tasks


def exchange(rider, name):
    n = len(rider)

    def body(*refs):
        tasks = _make_tasks(rider, refs[:n], refs[n:2 * n], refs[2 * n:])
        for task in tasks:
            task.start()
        for task in tasks:
            task.forward()
        for task in tasks:
            task.finish()

    return list(pl.pallas_call(
        body, name=name, out_shape=tuple(_task_shape(k, a) for k, a in rider),
        in_specs=[ANY] * n, out_specs=(ANY,) * n, scratch_shapes=_sem_shapes(n),
    )(*[a for _, a in rider]))


def _call(body, name, grid, in_specs, out_specs, out_shape, args, scratch=(), rider=()):
    in_specs, out_specs, out_shape = list(in_specs), tuple(out_specs), tuple(out_shape)
    n_in, n_out, n_scr, n_r = len(in_specs), len(out_specs), len(scratch), len(rider)
    if n_r:
        def kern(*refs):
            ins, r_in = refs[:n_in], refs[n_in:n_in + n_r]
            o0 = n_in + n_r
            outs, r_out = refs[o0:o0 + n_out], refs[o0 + n_out:o0 + n_out + n_r]
            s0 = o0 + n_out + n_r
            scr, sems = refs[s0:s0 + n_scr], refs[s0 + n_scr:]
            step = 0
            for a, n in enumerate(grid):
                step = step * n + pl.program_id(a)
            n_steps = 1
            for n in grid:
                n_steps *= n

            @pl.when(step == 0)
            def _():
                for task in _make_tasks(rider, r_in, r_out, sems):
                    task.start()
            body(*ins, *outs, *scr)

            @pl.when(step == n_steps - 1)
            def _():
                tasks = _make_tasks(rider, r_in, r_out, sems)
                for task in tasks:
                    task.forward()
                for task in tasks:
                    task.finish()
    else:
        kern = body
    res = pl.pallas_call(
        kern, name=name, grid=grid,
        in_specs=in_specs + [ANY] * n_r, out_specs=out_specs + (ANY,) * n_r,
        out_shape=out_shape + tuple(_task_shape(k, a) for k, a in rider),
        scratch_shapes=list(scratch) + (_sem_shapes(n_r) if n_r else []),
        compiler_params=pltpu.CompilerParams(dimension_semantics=("arbitrary",) * len(grid),
                                             vmem_limit_bytes=V7X_VMEM_LIMIT),
    )(*args, *[a for _, a in rider])
    return list(res[:n_out]), list(res[n_out:])


def _adamw(g, w, m, v):
    m = ADAM_B1 * m + (1.0 - ADAM_B1) * g
    v = ADAM_B2 * v + (1.0 - ADAM_B2) * (g * g)
    m_hat = m / (1.0 - ADAM_B1 ** ADAM_STEP)
    v_hat = v / (1.0 - ADAM_B2 ** ADAM_STEP)
    delta = -ADAM_LR * (m_hat / (jnp.sqrt(v_hat) + ADAM_EPS) + ADAM_WD * w)
    return delta, m, v


def sum_adamw(parts, w, m, v, name, rider=()):
    n_l, rows, cols = w.shape
    mult = BF16_SUBLANES if parts[0].dtype == BF16 else 8
    tr = _row_tile(rows, ADAM_ROWS, mult)
    n_i = rows // tr

    def body(*refs):
        part_refs = refs[:n_l]
        w_ref, m_ref, v_ref, g_out, d_out, m_out, v_out = refs[n_l:]
        layer = pl.program_id(0)
        for k in range(n_l):
            @pl.when(layer == k)
            def _(k=k):
                g = part_refs[k][0].astype(F32)
                for s in range(1, parts[k].shape[0]):
                    g = g + part_refs[k][s].astype(F32)
                delta, m_new, v_new = _adamw(g, w_ref[...], m_ref[...], v_ref[...])
                g_out[...] = g
                d_out[...] = delta
                m_out[...] = m_new
                v_out[...] = v_new

    part_specs = [pl.BlockSpec((parts[k].shape[0], tr, cols), functools.partial(
        lambda l, i, k: (0, jnp.where(l == k, i, 0), 0), k=k)) for k in range(n_l)]
    wspec = pl.BlockSpec((None, tr, cols), lambda l, i: (l, i, 0))
    shape = jax.ShapeDtypeStruct(w.shape, F32)
    return _call(body, name, (n_l, n_i), part_specs + [wspec] * 3, (wspec,) * 4, (shape,) * 4,
                 (*parts, w, m, v), rider=rider)


def sum_adamw_transposed(parts, w_t, m_t, v_t, name, rider=()):
    n_l, cols, rows = w_t.shape
    tr = LANES
    n_i = rows // tr
    starts = list(range(0, cols - LANES + 1, LANES))
    if starts[-1] + LANES < cols:
        starts.append(cols - LANES)

    def body(*refs):
        part_refs = refs[:n_l]
        w_ref, m_ref, v_ref, g_out, d_out, m_out, v_out = refs[n_l:]
        layer = pl.program_id(0)
        for k in range(n_l):
            @pl.when(layer == k)
            def _(k=k):
                for c0 in starts:
                    piece = pl.ds(c0, LANES)
                    g = part_refs[k][0, :, piece].astype(F32)
                    for s in range(1, parts[k].shape[0]):
                        g = g + part_refs[k][s, :, piece].astype(F32)
                    g = g.T
                    delta, m_new, v_new = _adamw(g, w_ref[piece, :], m_ref[piece, :], v_ref[piece, :])
                    g_out[piece, :] = g
                    d_out[piece, :] = delta
                    m_out[piece, :] = m_new
                    v_out[piece, :] = v_new

    part_specs = [pl.BlockSpec((parts[k].shape[0], tr, cols), functools.partial(
        lambda l, i, k: (0, jnp.where(l == k, i, 0), 0), k=k)) for k in range(n_l)]
    wspec = pl.BlockSpec((None, cols, tr), lambda l, i: (l, 0, i))
    shape = jax.ShapeDtypeStruct(w_t.shape, F32)
    return _call(body, name, (n_l, n_i), part_specs + [wspec] * 3, (wspec,) * 4, (shape,) * 4,
                 (*parts, w_t, m_t, v_t), rider=rider)


def pair_sum(g4, other, name):
    n_chip, _, rows, cols = g4.shape
    tr = _row_tile(rows, 512, BF16_SUBLANES)

    def body(core_ref, g_ref, o_ref, s_ref):
        s_ref[...] = (g_ref[...].astype(F32) + o_ref[...].astype(F32)).astype(BF16)

    blk = pl.BlockSpec((None, tr, cols), lambda k, i, core: (k, i, 0))
    return pl.pallas_call(
        body, name=name, out_shape=jax.ShapeDtypeStruct((n_chip, rows, cols), g4.dtype),
        grid_spec=pltpu.PrefetchScalarGridSpec(
            num_scalar_prefetch=1, grid=(n_chip, rows // tr),
            in_specs=[pl.BlockSpec((None, None, tr, cols), lambda k, i, core: (k, core[0], i, 0)), blk],
            out_specs=blk),
        compiler_params=pltpu.CompilerParams(dimension_semantics=("arbitrary", "arbitrary"),
                                             vmem_limit_bytes=V7X_VMEM_LIMIT),
    )(lax.axis_index("c").astype(jnp.int32).reshape(1), g4, other)


def sum_slots(parts, name):
    n, rows, cols = parts.shape

    def body(p_ref, o_ref):
        acc = p_ref[0]
        for s in range(1, n):
            acc = acc + p_ref[s]
        o_ref[...] = acc

    return pl.pallas_call(
        body, name=name, out_shape=jax.ShapeDtypeStruct((rows, cols), F32),
        in_specs=[pl.BlockSpec(memory_space=pltpu.VMEM)],
        out_specs=pl.BlockSpec(memory_space=pltpu.VMEM),
    )(parts)


def norm_fwd(h, g, name):
    seq, d = h.shape
    tm = min(ROW_TILE, seq)

    def body(h_ref, g_ref, o_ref):
        o_ref[...] = _rms(h_ref[...], g_ref[...])[0].astype(BF16)

    return _call(body, name, (seq // tm,),
                 [pl.BlockSpec((tm, d), lambda i: (i, 0)), pl.BlockSpec((1, d), lambda i: (0, 0))],
                 [pl.BlockSpec((tm, d), lambda i: (i, 0))],
                 [jax.ShapeDtypeStruct((seq, d), BF16)], (h, g))[0][0]


def proj_residual(a, w, res, name, g_next=None, rider=()):
    nb, seq, kb = a.shape
    d = w.shape[-1]
    tm = min(ROW_TILE, seq)
    with_norm = g_next is not None

    def body(a_ref, w_ref, r_ref, *rest):
        acc = r_ref[...]
        for b in range(nb):
            acc = acc + _nn(a_ref[b], w_ref[b])
        if with_norm:
            g_ref, o_ref, xn_ref = rest
            xn_ref[...] = _rms(acc, g_ref[...])[0].astype(BF16)
        else:
            (o_ref,) = rest
        o_ref[...] = acc

    row = pl.BlockSpec((tm, d), lambda i: (i, 0))
    in_specs = [pl.BlockSpec((nb, tm, kb), lambda i: (0, i, 0)),
                pl.BlockSpec((nb, kb, d), lambda i: (0, 0, 0)), row]
    args = [a, w, res]
    out_specs, out_shape = [row], [jax.ShapeDtypeStruct((seq, d), F32)]
    if with_norm:
        in_specs.append(pl.BlockSpec((1, d), lambda i: (0, 0)))
        args.append(g_next)
        out_specs.append(row)
        out_shape.append(jax.ShapeDtypeStruct((seq, d), BF16))
    outs, r_outs = _call(body, name, (seq // tm,), in_specs, out_specs, out_shape, args, rider=rider)
    return (outs[0], outs[1] if with_norm else None), r_outs


def proj_t_rms_bwd(du, w, h, g, dres, name, rider=()):
    nb, seq, wd = du.shape
    k = w.shape[1]
    big_weight = 2 * w.size * w.dtype.itemsize > V7X_VMEM_LIMIT // 4
    tm = min(ROW_TILE // 2 if big_weight else ROW_TILE, seq)

    def body(du_ref, w_ref, h_ref, g_ref, dr_ref, dh_ref, dg_ref):
        i = pl.program_id(0)
        dxn = _nt(du_ref[0], w_ref[0])
        for b in range(1, nb):
            dxn = dxn + _nt(du_ref[b], w_ref[b])
        _, xhat, rstd = _rms(h_ref[...], g_ref[...])
        dh, dg_rows = _rms_bwd(dxn, xhat, rstd, g_ref[...])
        dh_ref[...] = dr_ref[...] + dh

        @pl.when(i == 0)
        def _():
            dg_ref[...] = jnp.zeros_like(dg_ref)
        dg_ref[...] += jnp.sum(dg_rows, axis=0, keepdims=True)

    row = pl.BlockSpec((tm, k), lambda i: (i, 0))
    vec = pl.BlockSpec((1, k), lambda i: (0, 0))
    return _call(body, name, (seq // tm,),
                 [pl.BlockSpec((nb, tm, wd), lambda i: (0, i, 0)),
                  pl.BlockSpec((nb, k, wd), lambda i: (0, 0, 0)), row, vec, row],
                 (row, vec),
                 (jax.ShapeDtypeStruct((seq, k), F32), jax.ShapeDtypeStruct((1, k), F32)),
                 (du, w, h, g, dres), rider=rider)


def mixer_fwd(xn, win3, cw, name, rider=()):
    seq, d = xn.shape
    tm = min(ROW_TILE_LARGE, seq)
    cc = min(MIXER_CHUNK, d)
    n_c, n_i = d // cc, seq // tm

    def body(x_ref, w_ref, cw_ref, u_ref, z_ref, carry):
        i = pl.program_id(1)

        @pl.when(i == 0)
        def _():
            carry[...] = jnp.zeros_like(carry)
        xb = x_ref[...]
        b = _nn(xb, w_ref[0])
        c = _nn(xb, w_ref[1])
        hh = _nn(xb, w_ref[2])
        p = c * hh
        w0, w1, w2 = _conv_taps(cw_ref)
        p1 = _shift_down(p, 1, [carry[7:8, :]])
        p2 = _shift_down(p, 2, [carry[6:7, :], carry[7:8, :]])
        q = w0 * p2 + w1 * p1 + w2 * p
        carry[...] = p[tm - 8:tm, :]
        u_ref[0] = b.astype(BF16)
        u_ref[1] = c.astype(BF16)
        u_ref[2] = hh.astype(BF16)
        u_ref[3] = q.astype(BF16)
        z_ref[...] = (b * q).astype(BF16)

    return _call(body, name, (n_c, n_i),
                 [pl.BlockSpec((tm, d), lambda c, i: (i, 0)),
                  pl.BlockSpec((3, d, cc), lambda c, i: (0, 0, c)),
                  pl.BlockSpec((3, cc), lambda c, i: (0, c))],
                 (pl.BlockSpec((4, tm, cc), lambda c, i: (0, i, c)),
                  pl.BlockSpec((tm, cc), lambda c, i: (i, c))),
                 (jax.ShapeDtypeStruct((4, seq, d), BF16), jax.ShapeDtypeStruct((seq, d), BF16)),
                 (xn, win3, cw), scratch=[pltpu.VMEM((8, cc), F32)], rider=rider)


def mixer_bwd(dh, wout, u4, z, xn, cw, name, rider=()):
    seq, d = xn.shape
    tm = min(ROW_TILE, seq)
    cc = min(MIXER_CHUNK, d)
    n_c, n_i = d // cc, seq // tm

    def body(dh_ref, wout_ref, u_ref, z_ref, x_ref, cw_ref,
             du_ref, dwin_ref, dwout_ref, dcw_ref, acc_in, acc_out, acc_cw, carry):
        i = pl.program_id(1)

        @pl.when(i == 0)
        def _():
            acc_in[...] = jnp.zeros_like(acc_in)
            acc_out[...] = jnp.zeros_like(acc_out)
            acc_cw[...] = jnp.zeros_like(acc_cw)
            carry[...] = jnp.zeros_like(carry)
        dhb = dh_ref[...].astype(BF16)
        dz = _nt(dhb, wout_ref[...])
        acc_out[...] += _tn(z_ref[...], dhb)
        b = u_ref[0].astype(F32)
        c = u_ref[1].astype(F32)
        hh = u_ref[2].astype(F32)
        q = u_ref[3].astype(F32)
        p = c * hh
        db = dz * q
        dq = dz * b
        w0, w1, w2 = _conv_taps(cw_ref)
        dq1 = _shift_up(dq, 1, [carry[0:1, :]])
        dq2 = _shift_up(dq, 2, [carry[0:1, :], carry[1:2, :]])
        dp = w2 * dq + w1 * dq1 + w0 * dq2
        carry[...] = dq[0:8, :]
        acc_cw[0:1, :] += jnp.sum(dq2 * p, axis=0, keepdims=True)
        acc_cw[1:2, :] += jnp.sum(dq1 * p, axis=0, keepdims=True)
        acc_cw[2:3, :] += jnp.sum(dq * p, axis=0, keepdims=True)
        dbb = db.astype(BF16)
        dcb = (dp * hh).astype(BF16)
        dhhb = (dp * c).astype(BF16)
        du_ref[0] = dbb
        du_ref[1] = dcb
        du_ref[2] = dhhb
        xb = x_ref[...]
        acc_in[0] += _tn(xb, dbb)
        acc_in[1] += _tn(xb, dcb)
        acc_in[2] += _tn(xb, dhhb)

        @pl.when(i == n_i - 1)
        def _():
            dwin_ref[...] = acc_in[...].astype(BF16)
            dwout_ref[...] = acc_out[...].astype(BF16)
            dcw_ref[...] = acc_cw[0:3, :]

    rev = lambda c, i: (n_i - 1 - i, 0)
    return _call(body, name, (n_c, n_i),
                 [pl.BlockSpec((tm, d), rev),
                  pl.BlockSpec((cc, d), lambda c, i: (c, 0)),
                  pl.BlockSpec((4, tm, cc), lambda c, i: (0, n_i - 1 - i, c)),
                  pl.BlockSpec((tm, cc), lambda c, i: (n_i - 1 - i, c)),
                  pl.BlockSpec((tm, d), rev),
                  pl.BlockSpec((3, cc), lambda c, i: (0, c))],
                 (pl.BlockSpec((3, tm, cc), lambda c, i: (0, n_i - 1 - i, c)),
                  pl.BlockSpec((3, d, cc), lambda c, i: (0, 0, c)),
                  pl.BlockSpec((cc, d), lambda c, i: (c, 0)),
                  pl.BlockSpec((3, cc), lambda c, i: (0, c))),
                 (jax.ShapeDtypeStruct((3, seq, d), BF16), jax.ShapeDtypeStruct((3, d, d), BF16),
                  jax.ShapeDtypeStruct((d, d), BF16), jax.ShapeDtypeStruct((3, d), F32)),
                 (dh, wout, u4, z, xn, cw),
                 scratch=[pltpu.VMEM((3, d, cc), F32), pltpu.VMEM((cc, d), F32),
                          pltpu.VMEM((8, cc), F32), pltpu.VMEM((8, cc), F32)], rider=rider)


def _silu_parts(cg):
    sg = 1.0 / (1.0 + jnp.exp(-cg))
    return sg, cg * sg


def ffn_fwd(xn, wup, fcw, name, rider=()):
    seq, d = xn.shape
    f8 = wup.shape[-1]
    half = N_DEV // 2
    tm = min(ROW_TILE_LARGE, seq)
    n_i = seq // tm

    def body(x_ref, wg_ref, wu_ref, cg_ref, cu_ref, up_ref, cv_ref, a_ref, carry):
        i = pl.program_id(1)

        @pl.when(i == 0)
        def _():
            carry[...] = jnp.zeros_like(carry)
        xb = x_ref[...]
        conv = []
        for s, (w_ref, t_ref) in enumerate(((wg_ref, cg_ref), (wu_ref, cu_ref))):
            u = _nn(xb, w_ref[...])
            up_ref[s] = u.astype(BF16)
            w0, w1, w2 = _conv_taps(t_ref)
            u1 = _shift_down(u, 1, [carry[s, 7:8, :]])
            u2 = _shift_down(u, 2, [carry[s, 6:7, :], carry[s, 7:8, :]])
            cv = w0 * u2 + w1 * u1 + w2 * u
            cv_ref[s] = cv.astype(BF16)
            conv.append(cv)
            carry[s] = u[tm - 8:tm, :]
        _, silu = _silu_parts(conv[0])
        a_ref[...] = (silu * conv[1]).astype(BF16)

    blk = pl.BlockSpec((2, None, tm, f8), lambda c, i: (0, c, i, 0))
    big = jax.ShapeDtypeStruct((2, half, seq, f8), BF16)
    return _call(body, name, (half, n_i),
                 [pl.BlockSpec((tm, d), lambda c, i: (i, 0)),
                  pl.BlockSpec((None, d, f8), lambda c, i: (c, 0, 0)),
                  pl.BlockSpec((None, d, f8), lambda c, i: (c + half, 0, 0)),
                  pl.BlockSpec((None, 3, f8), lambda c, i: (c, 0, 0)),
                  pl.BlockSpec((None, 3, f8), lambda c, i: (c + half, 0, 0))],
                 (blk, blk, pl.BlockSpec((None, tm, f8), lambda c, i: (c, i, 0))),
                 (big, big, jax.ShapeDtypeStruct((half, seq, f8), BF16)),
                 (xn, wup, wup, fcw, fcw), scratch=[pltpu.VMEM((2, 8, f8), F32)], rider=rider)


def ffn_bwd(dh, wdown, up2, cv2, act, xn, fcw, name, rider=()):
    seq, d = xn.shape
    f8 = up2.shape[-1]
    fb = wdown.shape[1]
    half = N_DEV // 2
    tm = min(ROW_TILE, seq)
    n_i = seq // tm

    def body(dh_ref, wd_ref, up_ref, cv_ref, a_ref, x_ref, cg_ref, cu_ref,
             dup_ref, dwup_ref, dwd_ref, dcw_ref, acc_up, acc_down, acc_cw, carry):
        i = pl.program_id(1)

        @pl.when(i == 0)
        def _():
            acc_up[...] = jnp.zeros_like(acc_up)
            acc_down[...] = jnp.zeros_like(acc_down)
            acc_cw[...] = jnp.zeros_like(acc_cw)
            carry[...] = jnp.zeros_like(carry)
        dhb = dh_ref[...].astype(BF16)
        da = _nt(dhb, wd_ref[...])
        acc_down[...] += _tn(a_ref[...], dhb)
        cg = cv_ref[0].astype(F32)
        cu = cv_ref[1].astype(F32)
        sg, silu = _silu_parts(cg)
        dcg = da * cu * (sg + silu * (1.0 - sg))
        dcu = da * silu
        xb = x_ref[...]
        for s, (dc, t_ref) in enumerate(((dcg, cg_ref), (dcu, cu_ref))):
            w0, w1, w2 = _conv_taps(t_ref)
            d1 = _shift_up(dc, 1, [carry[s, 0:1, :]])
            d2 = _shift_up(dc, 2, [carry[s, 0:1, :], carry[s, 1:2, :]])
            du = (w2 * dc + w1 * d1 + w0 * d2).astype(BF16)
            carry[s] = dc[0:8, :]
            u = up_ref[s].astype(F32)
            acc_cw[s, 0:1, :] += jnp.sum(d2 * u, axis=0, keepdims=True)
            acc_cw[s, 1:2, :] += jnp.sum(d1 * u, axis=0, keepdims=True)
            acc_cw[s, 2:3, :] += jnp.sum(dc * u, axis=0, keepdims=True)
            dup_ref[s] = du
            acc_up[s] += _tn(xb, du)

        @pl.when(i == n_i - 1)
        def _():
            dwup_ref[...] = acc_up[...].astype(BF16)
            dwd_ref[...] = acc_down[...].astype(BF16)
            dcw_ref[...] = acc_cw[:, 0:3, :]

    rev = lambda c, i: (n_i - 1 - i, 0)
    blk = pl.BlockSpec((2, None, tm, f8), lambda c, i: (0, c, n_i - 1 - i, 0))
    return _call(body, name, (half, n_i),
                 [pl.BlockSpec((tm, d), rev),
                  pl.BlockSpec((None, fb, d), lambda c, i: (c, 0, 0)),
                  blk, blk,
                  pl.BlockSpec((None, tm, f8), lambda c, i: (c, n_i - 1 - i, 0)),
                  pl.BlockSpec((tm, d), rev),
                  pl.BlockSpec((None, 3, f8), lambda c, i: (c, 0, 0)),
                  pl.BlockSpec((None, 3, f8), lambda c, i: (c + half, 0, 0))],
                 (blk,
                  pl.BlockSpec((2, None, d, f8), lambda c, i: (0, c, 0, 0)),
                  pl.BlockSpec((None, fb, d), lambda c, i: (c, 0, 0)),
                  pl.BlockSpec((2, None, 3, f8), lambda c, i: (0, c, 0, 0))),
                 (jax.ShapeDtypeStruct((2, half, seq, f8), BF16),
                  jax.ShapeDtypeStruct((2, half, d, f8), BF16),
                  jax.ShapeDtypeStruct((half, fb, d), BF16),
                  jax.ShapeDtypeStruct((2, half, 3, f8), F32)),
                 (dh, wdown, up2, cv2, act, xn, fcw, fcw),
                 scratch=[pltpu.VMEM((2, d, f8), F32), pltpu.VMEM((fb, d), F32),
                          pltpu.VMEM((2, 8, f8), F32), pltpu.VMEM((2, 8, f8), F32)], rider=rider)


def q_fwd(xn, wdq, gq, wuq, cos, sin, name, rider=()):
    seq, d = xn.shape
    rank = wdq.shape[-1]
    tm = min(ROW_TILE, seq)

    def body(x_ref, wdq_ref, gq_ref, wuq_ref, cos_ref, sin_ref, q_ref):
        qc = _nn(x_ref[...], wdq_ref[...])
        qn = _rms(qc, gq_ref[...])[0].astype(BF16)
        for hd in range(N_HEADS):
            qh = _nn(qn, wuq_ref[hd])
            qr = _rope_fwd(qh[:, NOPE:QK], cos_ref[...], sin_ref[...])
            q_ref[hd, :, 0:NOPE] = (qh[:, 0:NOPE] * ATTN_SCALE).astype(BF16)
            q_ref[hd, :, NOPE:QK] = (qr * ATTN_SCALE).astype(BF16)

    rope = pl.BlockSpec((tm, ROPE_PAD), lambda i: (i, 0))
    return _call(body, name, (seq // tm,),
                 [pl.BlockSpec((tm, d), lambda i: (i, 0)),
                  pl.BlockSpec((d, rank), lambda i: (0, 0)),
                  pl.BlockSpec((1, rank), lambda i: (0, 0)),
                  pl.BlockSpec((N_HEADS, rank, QK), lambda i: (0, 0, 0)), rope, rope],
                 [pl.BlockSpec((N_HEADS, tm, QK), lambda i: (0, i, 0))],
                 [jax.ShapeDtypeStruct((N_HEADS, seq, QK), BF16)],
                 (xn, wdq, gq, wuq, cos, sin), rider=rider)


def q_bwd(dq, xn, h, g, dres, wdq, gq, wuq, cos, sin, name, rider=()):
    seq, d = xn.shape
    rank = wdq.shape[-1]
    tm = min(ROW_TILE, seq)
    n_i = seq // tm

    def body(dq_ref, x_ref, h_ref, g_ref, dr_ref, wdq_ref, gq_ref, wuq_ref, cos_ref, sin_ref,
             dh_ref, dwuq_ref, dwdq_ref, dgq_ref, dg_ref, acc_uq, acc_dq):
        i = pl.program_id(0)

        @pl.when(i == 0)
        def _():
            acc_uq[...] = jnp.zeros_like(acc_uq)
            acc_dq[...] = jnp.zeros_like(acc_dq)
            dgq_ref[...] = jnp.zeros_like(dgq_ref)
            dg_ref[...] = jnp.zeros_like(dg_ref)
        xb = x_ref[...]
        qc = _nn(xb, wdq_ref[...])
        qn, qhat, qrstd = _rms(qc, gq_ref[...])
        qnb = qn.astype(BF16)
        dqn = jnp.zeros((tm, rank), F32)
        for hd in range(N_HEADS):
            dnope = (dq_ref[hd, :, 0:NOPE].astype(F32) * ATTN_SCALE).astype(BF16)
            drope = _rope_bwd(dq_ref[hd, :, NOPE:QK].astype(F32) * ATTN_SCALE, cos_ref[...], sin_ref[...])
            draw = jnp.concatenate([dnope, drope.astype(BF16)], axis=1)
            dqn = dqn + _nt(draw, wuq_ref[hd])
            acc_uq[hd] += _tn(qnb, draw)
        dqc, dg_rows = _rms_bwd(dqn, qhat, qrstd, gq_ref[...])
        dgq_ref[...] += jnp.sum(dg_rows, axis=0, keepdims=True)
        dqcb = dqc.astype(BF16)
        acc_dq[...] += _tn(xb, dqcb)
        _, xhat, rstd = _rms(h_ref[...], g_ref[...])
        dh, dg_rows = _rms_bwd(_nt(dqcb, wdq_ref[...]), xhat, rstd, g_ref[...])
        dh_ref[...] = dr_ref[...] + dh
        dg_ref[...] += jnp.sum(dg_rows, axis=0, keepdims=True)

        @pl.when(i == n_i - 1)
        def _():
            dwuq_ref[...] = acc_uq[...].astype(BF16)
            dwdq_ref[...] = acc_dq[...].astype(BF16)

    rope = pl.BlockSpec((tm, ROPE_PAD), lambda i: (i, 0))
    row = pl.BlockSpec((tm, d), lambda i: (i, 0))
    vec = pl.BlockSpec((1, d), lambda i: (0, 0))
    return _call(body, name, (n_i,),
                 [pl.BlockSpec((N_HEADS, tm, QK), lambda i: (0, i, 0)), row, row, vec, row,
                  pl.BlockSpec((d, rank), lambda i: (0, 0)),
                  pl.BlockSpec((1, rank), lambda i: (0, 0)),
                  pl.BlockSpec((N_HEADS, rank, QK), lambda i: (0, 0, 0)), rope, rope],
                 (row,
                  pl.BlockSpec((N_HEADS, rank, QK), lambda i: (0, 0, 0)),
                  pl.BlockSpec((d, rank), lambda i: (0, 0)),
                  pl.BlockSpec((1, rank), lambda i: (0, 0)), vec),
                 (jax.ShapeDtypeStruct((seq, d), F32),
                  jax.ShapeDtypeStruct((N_HEADS, rank, QK), BF16),
                  jax.ShapeDtypeStruct((d, rank), BF16),
                  jax.ShapeDtypeStruct((1, rank), F32),
                  jax.ShapeDtypeStruct((1, d), F32)),
                 (dq, xn, h, g, dres, wdq, gq, wuq, cos, sin),
                 scratch=[pltpu.VMEM((N_HEADS, rank, QK), F32), pltpu.VMEM((d, rank), F32)],
                 rider=rider)


def kv_fwd(h, g, wdkv, gkv, wukv, cos, sin, name, rider=()):
    seq, d = h.shape
    tm = min(ROW_TILE, seq)
    wk = KV_RANK + ROPE_PAD

    def body(h_ref, g_ref, wdkv_ref, gkv_ref, wukv_ref, cos_ref, sin_ref, k_ref, v_ref, c_ref):
        xk = _rms(h_ref[...], g_ref[...])[0].astype(BF16)
        ckv = _nn(xk, wdkv_ref[...])
        c_kv = ckv[:, 0:KV_RANK]
        c_ref[...] = c_kv
        kr = _rope_fwd(ckv[:, KV_RANK:wk], cos_ref[...], sin_ref[...]).astype(BF16)
        ckn = _rms(c_kv, gkv_ref[...])[0].astype(BF16)
        for hd in range(N_HEADS):
            kvh = _nn(ckn, wukv_ref[hd])
            k_ref[hd, :, 0:NOPE] = kvh[:, 0:NOPE].astype(BF16)
            k_ref[hd, :, NOPE:QK] = kr
            v_ref[hd] = kvh[:, NOPE:NOPE + VDIM].astype(BF16)

    rope = pl.BlockSpec((tm, ROPE_PAD), lambda i: (i, 0))
    return _call(body, name, (seq // tm,),
                 [pl.BlockSpec((tm, d), lambda i: (i, 0)),
                  pl.BlockSpec((1, d), lambda i: (0, 0)),
                  pl.BlockSpec((d, wk), lambda i: (0, 0)),
                  pl.BlockSpec((1, KV_RANK), lambda i: (0, 0)),
                  pl.BlockSpec((N_HEADS, KV_RANK, NOPE + VDIM), lambda i: (0, 0, 0)), rope, rope],
                 (pl.BlockSpec((N_HEADS, tm, QK), lambda i: (0, i, 0)),
                  pl.BlockSpec((N_HEADS, tm, VDIM), lambda i: (0, i, 0)),
                  pl.BlockSpec((tm, KV_RANK), lambda i: (i, 0))),
                 (jax.ShapeDtypeStruct((N_HEADS, seq, QK), BF16),
                  jax.ShapeDtypeStruct((N_HEADS, seq, VDIM), BF16),
                  jax.ShapeDtypeStruct((seq, KV_RANK), F32)),
                 (h, g, wdkv, gkv, wukv, cos, sin), rider=rider)


def kv_bwd(dks, dvs, c_kv, h, g, dres, wdkv, gkv, wukv, cos, sin, name, rider=()):
    seq, d = h.shape
    tm = min(ROW_TILE, seq)
    n_i = seq // tm
    wk = KV_RANK + ROPE_PAD
    n_b = len(dks)

    def body(*refs):
        dk_refs = refs[:n_b]
        dv_refs = refs[n_b:2 * n_b]
        (c_ref, h_ref, g_ref, dr_ref, wdkv_ref, gkv_ref, wukv_ref, cos_ref, sin_ref,
         dh_ref, dwukv_ref, dwdkv_ref, dgkv_ref, dg_ref, acc_ukv, acc_dkv) = refs[2 * n_b:]
        i = pl.program_id(0)

        @pl.when(i == 0)
        def _():
            acc_ukv[...] = jnp.zeros_like(acc_ukv)
            acc_dkv[...] = jnp.zeros_like(acc_dkv)
            dgkv_ref[...] = jnp.zeros_like(dgkv_ref)
            dg_ref[...] = jnp.zeros_like(dg_ref)
        ckn, chat, crstd = _rms(c_ref[...], gkv_ref[...])
        cknb = ckn.astype(BF16)
        dckn = jnp.zeros((tm, KV_RANK), F32)
        dkr = jnp.zeros((tm, ROPE_PAD), F32)
        for hd in range(N_HEADS):
            dk = dk_refs[0][hd].astype(F32)
            dv = dv_refs[0][hd].astype(F32)
            for j in range(1, n_b):
                dk = dk + dk_refs[j][hd].astype(F32)
                dv = dv + dv_refs[j][hd].astype(F32)
            dkr = dkr + dk[:, NOPE:QK]
            dkvh = jnp.concatenate([dk[:, 0:NOPE].astype(BF16), dv.astype(BF16)], axis=1)
            dckn = dckn + _nt(dkvh, wukv_ref[hd])
            acc_ukv[hd] += _tn(cknb, dkvh)
        dc_kv, dg_rows = _rms_bwd(dckn, chat, crstd, gkv_ref[...])
        dgkv_ref[...] += jnp.sum(dg_rows, axis=0, keepdims=True)
        dkr_raw = _rope_bwd(dkr, cos_ref[...], sin_ref[...])
        dckv = jnp.concatenate([dc_kv.astype(BF16), dkr_raw.astype(BF16)], axis=1)
        xk, xhat, rstd = _rms(h_ref[...], g_ref[...])
        acc_dkv[...] += _tn(xk.astype(BF16), dckv)
        dh, dg_rows = _rms_bwd(_nt(dckv, wdkv_ref[...]), xhat, rstd, g_ref[...])
        dh_ref[...] = dr_ref[...] + dh
        dg_ref[...] += jnp.sum(dg_rows, axis=0, keepdims=True)

        @pl.when(i == n_i - 1)
        def _():
            dwukv_ref[...] = acc_ukv[...].astype(BF16)
            dwdkv_ref[...] = acc_dkv[...].astype(BF16)

    kspec = pl.BlockSpec((N_HEADS, tm, QK), lambda i: (0, i, 0))
    vspec = pl.BlockSpec((N_HEADS, tm, VDIM), lambda i: (0, i, 0))
    rope = pl.BlockSpec((tm, ROPE_PAD), lambda i: (i, 0))
    row = pl.BlockSpec((tm, d), lambda i: (i, 0))
    vec = pl.BlockSpec((1, d), lambda i: (0, 0))
    return _call(body, name, (n_i,),
                 [kspec] * n_b + [vspec] * n_b + [
                     pl.BlockSpec((tm, KV_RANK), lambda i: (i, 0)), row, vec, row,
                     pl.BlockSpec((d, wk), lambda i: (0, 0)),
                     pl.BlockSpec((1, KV_RANK), lambda i: (0, 0)),
                     pl.BlockSpec((N_HEADS, KV_RANK, NOPE + VDIM), lambda i: (0, 0, 0)), rope, rope],
                 (row,
                  pl.BlockSpec((N_HEADS, KV_RANK, NOPE + VDIM), lambda i: (0, 0, 0)),
                  pl.BlockSpec((d, wk), lambda i: (0, 0)),
                  pl.BlockSpec((1, KV_RANK), lambda i: (0, 0)), vec),
                 (jax.ShapeDtypeStruct((seq, d), F32),
                  jax.ShapeDtypeStruct((N_HEADS, KV_RANK, NOPE + VDIM), BF16),
                  jax.ShapeDtypeStruct((d, wk), BF16),
                  jax.ShapeDtypeStruct((1, KV_RANK), F32),
                  jax.ShapeDtypeStruct((1, d), F32)),
                 (*dks, *dvs, c_kv, h, g, dres, wdkv, gkv, wukv, cos, sin),
                 scratch=[pltpu.VMEM((N_HEADS, KV_RANK, NOPE + VDIM), F32), pltpu.VMEM((d, wk), F32)],
                 rider=rider)


def o_bwd(dh, o, wo, name, rider=()):
    seq, d = dh.shape
    hv = o.shape[1]
    tm = min(ROW_TILE, seq)
    n_i = seq // tm

    def body(dh_ref, o_ref, wo_ref, do_ref, dwo_ref, acc):
        i = pl.program_id(0)

        @pl.when(i == 0)
        def _():
            acc[...] = jnp.zeros_like(acc)
        dhb = dh_ref[...].astype(BF16)
        do_ref[...] = _nt(dhb, wo_ref[...]).astype(BF16)
        acc[...] += _tn(o_ref[...], dhb)

        @pl.when(i == n_i - 1)
        def _():
            dwo_ref[...] = acc[...].astype(BF16)

    return _call(body, name, (n_i,),
                 [pl.BlockSpec((tm, d), lambda i: (i, 0)),
                  pl.BlockSpec((tm, hv), lambda i: (i, 0)),
                  pl.BlockSpec((hv, d), lambda i: (0, 0))],
                 (pl.BlockSpec((tm, hv), lambda i: (i, 0)),
                  pl.BlockSpec((hv, d), lambda i: (0, 0))),
                 (jax.ShapeDtypeStruct((seq, hv), BF16), jax.ShapeDtypeStruct((hv, d), BF16)),
                 (dh, o, wo), scratch=[pltpu.VMEM((hv, d), F32)], rider=rider)


def _mask_diagonal(s):
    row = lax.broadcasted_iota(jnp.int32, s.shape, 0)
    col = lax.broadcasted_iota(jnp.int32, s.shape, 1)
    return jnp.where(col <= row, s, NEG_BIG)


def attn_fwd(q, k, v, name, rider=()):
    _, seq, _ = q.shape
    t = min(ATTN_TILE, seq // 2)
    n_pair = seq // (2 * t)

    def body(q_ref, k_ref, v_ref, o_ref, lse_ref):
        qi = pl.program_id(1)
        q_a = q_ref[0:t, :]
        q_b = q_ref[t:2 * t, :]

        def rows(j):
            return pl.ds(pl.multiple_of(j * t, t), t)

        def update(qx, kb, vb, state, diagonal=False):
            m, l, acc = state
            s = _nt(qx, kb)
            if diagonal:
                s = _mask_diagonal(s)
            m_new = jnp.maximum(m, jnp.max(s, axis=1, keepdims=True))
            p = jnp.exp(s - m_new)
            alpha = jnp.exp(m - m_new)
            l = alpha * l + jnp.sum(p, axis=1, keepdims=True)
            acc = alpha * acc + _nn(p.astype(BF16), vb)
            return m_new, l, acc

        def step(j, carry):
            both = pl.ds(pl.multiple_of(j * 2 * t, 2 * t), 2 * t)
            kb, vb = k_ref[both, :], v_ref[both, :]
            return update(q_a, kb, vb, carry[0:3]) + update(q_b, kb, vb, carry[3:6])

        init = (jnp.full((t, 1), NEG_BIG, F32), jnp.zeros((t, 1), F32), jnp.zeros((t, VDIM), F32))
        carry = lax.fori_loop(0, qi, step, init + init)
        k0, v0 = k_ref[rows(2 * qi), :], v_ref[rows(2 * qi), :]
        k1, v1 = k_ref[rows(2 * qi + 1), :], v_ref[rows(2 * qi + 1), :]
        state_a = update(q_a, k0, v0, carry[0:3], diagonal=True)
        state_b = update(q_b, k1, v1, update(q_b, k0, v0, carry[3:6]), diagonal=True)
        for half, (m, l, acc) in enumerate((state_a, state_b)):
            o_ref[half * t:(half + 1) * t, :] = (acc / l).astype(BF16)
            lse_ref[half * t:(half + 1) * t, :] = jnp.broadcast_to(m + jnp.log(l), (t, LANES))

    return _call(body, name, (N_HEADS, n_pair),
                 [pl.BlockSpec((None, 2 * t, QK), lambda h, i: (h, i, 0)),
                  pl.BlockSpec((None, seq, QK), lambda h, i: (h, 0, 0)),
                  pl.BlockSpec((None, seq, VDIM), lambda h, i: (h, 0, 0))],
                 (pl.BlockSpec((2 * t, VDIM), lambda h, i: (i, h)),
                  pl.BlockSpec((None, 2 * t, LANES), lambda h, i: (h, i, 0))),
                 (jax.ShapeDtypeStruct((seq, N_HEADS * VDIM), BF16),
                  jax.ShapeDtypeStruct((N_HEADS, seq, LANES), F32)),
                 (q, k, v), rider=rider)


def attn_bwd(q, k, v, o, do, lse, name, rider=()):
    _, seq, _ = q.shape
    t = min(ATTN_TILE, seq // 2)
    n_q = seq // t
    n_pair = n_q // 2

    def body(q_ref, k_ref, v_ref, o_ref, do_ref, lse_ref, dq_ref, dk_ref, dv_ref,
             dq_acc, dk_acc, dv_acc):
        kj = pl.program_id(1)

        @pl.when(kj == 0)
        def _():
            dq_acc[...] = jnp.zeros_like(dq_acc)
        halves = (slice(0, t), slice(t, 2 * t))

        def block(i, masks, n_rows=t):
            rows = pl.ds(pl.multiple_of(i * n_rows, n_rows), n_rows)
            qb = q_ref[rows, :]
            dob = do_ref[rows, :]
            lse_col = lse_ref[rows, 0:1]
            delta = jnp.sum(dob.astype(F32) * o_ref[rows, :].astype(F32), axis=1, keepdims=True)
            dq, out = None, {}
            for x, diagonal in enumerate(masks):
                if diagonal is None:
                    continue
                kb, vb = k_ref[halves[x], :], v_ref[halves[x], :]
                s = _nt(qb, kb)
                if diagonal:
                    s = _mask_diagonal(s)
                p = jnp.exp(s - lse_col)
                ds = (p * (_nt(dob, vb) - delta)).astype(BF16)
                out[x] = (_tn(p.astype(BF16), dob), _tn(ds, qb))
                part = _nn(ds, kb)
                dq = part if dq is None else dq + part
            dq_acc[rows, :] += dq
            return out

        first = block(2 * kj, (True, None))
        second = block(2 * kj + 1, (False, True))
        dv_acc[halves[0], :] = first[0][0] + second[0][0]
        dk_acc[halves[0], :] = first[0][1] + second[0][1]
        dv_acc[halves[1], :] = second[1][0]
        dk_acc[halves[1], :] = second[1][1]

        def step(i, carry):
            out = block(i, (False, False), n_rows=2 * t)
            for x in (0, 1):
                dv_acc[halves[x], :] += out[x][0]
                dk_acc[halves[x], :] += out[x][1]
            return carry

        lax.fori_loop(kj + 1, n_pair, step, 0)
        dk_ref[...] = dk_acc[...].astype(BF16)
        dv_ref[...] = dv_acc[...].astype(BF16)

        @pl.when(kj == n_pair - 1)
        def _():
            dq_ref[...] = dq_acc[...].astype(BF16)

    head_rows = pl.BlockSpec((seq, VDIM), lambda h, j: (0, h))
    return _call(body, name, (N_HEADS, n_pair),
                 [pl.BlockSpec((None, seq, QK), lambda h, j: (h, 0, 0)),
                  pl.BlockSpec((None, 2 * t, QK), lambda h, j: (h, j, 0)),
                  pl.BlockSpec((None, 2 * t, VDIM), lambda h, j: (h, j, 0)),
                  head_rows, head_rows,
                  pl.BlockSpec((None, seq, LANES), lambda h, j: (h, 0, 0))],
                 (pl.BlockSpec((None, seq, QK), lambda h, j: (h, 0, 0)),
                  pl.BlockSpec((None, 2 * t, QK), lambda h, j: (h, j, 0)),
                  pl.BlockSpec((None, 2 * t, VDIM), lambda h, j: (h, j, 0))),
                 (jax.ShapeDtypeStruct((N_HEADS, seq, QK), BF16),
                  jax.ShapeDtypeStruct((N_HEADS, seq, QK), BF16),
                  jax.ShapeDtypeStruct((N_HEADS, seq, VDIM), BF16)),
                 (q, k, v, o, do, lse),
                 scratch=[pltpu.VMEM((seq, QK), F32), pltpu.VMEM((2 * t, QK), F32),
                          pltpu.VMEM((2 * t, VDIM), F32)], rider=rider)


def loss_head(h, g, target, name):
    seq, d = h.shape
    tm = min(ROW_TILE, seq)

    def body(h_ref, g_ref, t_ref, l_ref, dh_ref, dg_ref):
        i = pl.program_id(0)

        @pl.when(i == 0)
        def _():
            l_ref[...] = jnp.zeros_like(l_ref)
            dg_ref[...] = jnp.zeros_like(dg_ref)
        y, xhat, rstd = _rms(h_ref[...], g_ref[...])
        diff = y - t_ref[...]
        l_ref[...] += jnp.sum(jnp.sum(diff * diff, axis=1, keepdims=True), axis=0, keepdims=True)
        dh, dg_rows = _rms_bwd(diff * (1.0 / d), xhat, rstd, g_ref[...])
        dh_ref[...] = dh
        dg_ref[...] += jnp.sum(dg_rows, axis=0, keepdims=True)

    row = pl.BlockSpec((tm, d), lambda i: (i, 0))
    vec = pl.BlockSpec((1, d), lambda i: (0, 0))
    return _call(body, name, (seq // tm,), [row, vec, row],
                 (pl.BlockSpec((1, LANES), lambda i: (0, 0)), row, vec),
                 (jax.ShapeDtypeStruct((1, LANES), F32), jax.ShapeDtypeStruct((seq, d), F32),
                  jax.ShapeDtypeStruct((1, d), F32)),
                 (h, g, target))[0]


def _pack(parts):
    rows = []
    for p in parts:
        flat = p.reshape(-1)
        n_rows = -(-flat.shape[0] // (8 * LANES)) * 8
        flat = jnp.pad(flat, (0, n_rows * LANES - flat.shape[0]))
        rows.append(flat.reshape(n_rows, LANES))
    return jnp.concatenate(rows, axis=0)


def _unpack(packed, shapes):
    lead = packed.shape[:-2]
    out, r0 = [], 0
    for shape in shapes:
        size = 1
        for s in shape:
            size *= s
        n_rows = -(-size // (8 * LANES)) * 8
        part = packed[..., r0:r0 + n_rows, :].reshape(lead + (n_rows * LANES,))
        out.append(part[..., :size].reshape(lead + tuple(shape)))
        r0 += n_rows
    return out


FWD_RIDERS = {
    "mixer_fwd0": [("ffn_w_up", 0)],
    "ffn_fwd0": [("ffn_w_down", 0), ("a_w_in", 1), ("a_w_out", 1)],
    "mixer_fwd1": [("ffn_w_up", 1)],
    "ffn_fwd1": [("ffn_w_down", 1), ("w_dkv", 0), ("w_ukv", 0), ("b_w_dq", 0), ("b_w_uq", 0)],
    "attn_fwd0": [("b_w_o", 0), ("ffn_w_up", 2), ("ffn_w_down", 2), ("b_w_dq", 1), ("b_w_uq", 1)],
    "attn_fwd1": [("b_w_o", 1), ("ffn_w_up", 3), ("ffn_w_down", 3)],
}
BWD_RIDERS = {
    "attn_bwd1": [("ffn_w_down", 3), ("ffn_w_up", 3), ("b_w_o", 1)],
    "attn_bwd0": [("ffn_w_down", 2), ("ffn_w_up", 2), ("b_w_o", 0)],
    "ffn_bwd1": [("b_w_uq", 1), ("b_w_dq", 1), ("b_w_uq", 0), ("b_w_dq", 0), ("w_ukv", 0), ("w_dkv", 0)],
    "ffn_in_bwd1": [("ffn_w_down", 1), ("ffn_w_up", 1, "pair")],
    "ffn_bwd0": [("ffn_w_up", 1, "chip"), ("a_w_in", 1), ("a_w_out", 1)],
    "ffn_in_bwd0": [("ffn_w_down", 0), ("ffn_w_up", 0, "pair")],
    "mixer_bwd0": [("ffn_w_up", 0, "chip")],
    "mixer_in_bwd0": [("a_w_out", 0), ("a_w_in", 0, "pair")],
    "adamw_a_w_out": [("a_w_in", 0, "chip")],
}


def kernel(x, a_mix_norm, a_w_in, a_conv, a_w_out, b_mix_norm, b_w_dq, b_q_norm, b_w_uq, b_w_o, kv_in_norm, w_dkv, kv_norm, w_ukv, ffn_norm, ffn_w_up, ffn_conv, ffn_w_down, final_norm, loss_target, m_a_mix_norm, m_a_w_in, m_a_conv, m_a_w_out, m_b_mix_norm, m_b_w_dq, m_b_q_norm, m_b_w_uq, m_b_w_o, m_kv_in_norm, m_w_dkv, m_kv_norm, m_w_ukv, m_ffn_norm, m_ffn_w_up, m_ffn_conv, m_ffn_w_down, m_final_norm, v_a_mix_norm, v_a_w_in, v_a_conv, v_a_w_out, v_b_mix_norm, v_b_w_dq, v_b_q_norm, v_b_w_uq, v_b_w_o, v_kv_in_norm, v_w_dkv, v_kv_norm, v_w_ukv, v_ffn_norm, v_ffn_w_up, v_ffn_conv, v_ffn_w_down, v_final_norm):
    seq, d = x.shape[1], x.shape[2]
    me = 4 * lax.axis_index("x") + 2 * lax.axis_index("y") + lax.axis_index("c")
    h0 = x.reshape(seq, d)
    target = loss_target.reshape(seq, d)
    cos, sin = _rope_tables(seq)
    rank = b_w_dq.shape[-1]
    f8 = ffn_w_up.shape[-1]
    fd = ffn_w_down.shape[1]
    dshard = a_w_out.shape[1]
    hv = N_HEADS * VDIM

    shards = {"a_w_in": a_w_in, "a_w_out": a_w_out, "b_w_dq": b_w_dq, "b_w_uq": b_w_uq,
              "b_w_o": b_w_o, "w_dkv": w_dkv[None], "w_ukv": w_ukv[None],
              "ffn_w_up": ffn_w_up, "ffn_w_down": ffn_w_down}

    def relayout(name, g):
        if name == "a_w_in":
            w = jnp.transpose(g, (1, 0, 2)).reshape(d, 3, d)
            return jnp.transpose(w, (1, 0, 2))
        if name == "a_w_out":
            return g.reshape(d, d)
        if name == "b_w_dq":
            return g.reshape(d, rank)
        if name == "b_w_o":
            return g.reshape(hv, d)
        if name == "w_dkv":
            return g.reshape(d, KV_RANK + ROPE_PAD)
        if name == "ffn_w_down":
            return g.reshape(N_DEV // 2, 2 * fd, d)
        return g

    weights = {}

    shards_bf16 = {n: w.astype(BF16) for n, w in shards.items()}
    shards_bf16["b_w_uq"] = jnp.pad(shards_bf16["b_w_uq"], ((0, 0), (0, 0), (0, QK - NOPE - ROPE)))
    shards_bf16["w_dkv"] = jnp.pad(shards_bf16["w_dkv"], ((0, 0), (0, 0), (0, ROPE_PAD - ROPE)))

    def ag_rider(host):
        return [(("ag", l), shards_bf16[n]) for n, l in FWD_RIDERS.get(host, [])]

    def ag_done(host, outs):
        for (n, l), g in zip(FWD_RIDERS.get(host, []), outs):
            weights[n, l] = relayout(n, g)

    small_shapes = [a_mix_norm.shape, a_conv.shape, ffn_conv.shape]
    first = exchange([(("ag", 0), shards_bf16["a_w_in"]), (("ag", 0), shards_bf16["a_w_out"]),
                      ("ag", _pack([a_mix_norm, a_conv, ffn_conv]))], "ag_first")
    weights["a_w_in", 0] = relayout("a_w_in", first[0])
    weights["a_w_out", 0] = relayout("a_w_out", first[1])
    s_mix, s_aconv, s_fconv = _unpack(first[2], small_shapes)
    a_gain = jnp.transpose(s_mix, (1, 0, 2)).reshape(N_A, d)
    a_cw = jnp.transpose(s_aconv, (1, 2, 0, 3)).reshape(N_A, 3, d)
    f_cw = jnp.transpose(s_fconv, (1, 0, 2, 3))

    def mixer_gain(layer):
        if layer >= DEPTH:
            return None
        return a_gain[layer][None] if layer < N_A else b_mix_norm[layer - N_A][None]

    saved = {}
    h = h0
    xn = norm_fwd(h, mixer_gain(0), "norm_first")
    kv = None
    for layer in range(DEPTH):
        saved["hm", layer], saved["xm", layer] = h, xn
        if layer < N_A:
            name = f"mixer_fwd{layer}"
            (u4, z), r = mixer_fwd(xn, weights["a_w_in", layer], a_cw[layer], name, rider=ag_rider(name))
            ag_done(name, r)
            saved["mix", layer] = (u4, z)
            name = f"mixer_out{layer}"
            (h, xn), r = proj_residual(z[None], weights["a_w_out", layer][None], h, name,
                                       g_next=ffn_norm[layer][None], rider=ag_rider(name))
            ag_done(name, r)
        else:
            j = layer - N_A
            name = f"q_fwd{j}"
            (q,), r = q_fwd(xn, weights["b_w_dq", j], b_q_norm[j][None], weights["b_w_uq", j],
                            cos, sin, name, rider=ag_rider(name))
            ag_done(name, r)
            name = f"attn_fwd{j}"
            (o, lse), r = attn_fwd(q, kv[0], kv[1], name, rider=ag_rider(name))
            ag_done(name, r)
            saved["attn", layer] = (q, o, lse)
            name = f"attn_out{j}"
            (h, xn), r = proj_residual(o[None], weights["b_w_o", j][None], h, name,
                                       g_next=ffn_norm[layer][None], rider=ag_rider(name))
            ag_done(name, r)
        saved["hf", layer], saved["xf", layer] = h, xn
        name = f"ffn_fwd{layer}"
        (up2, cv2, act), r = ffn_fwd(xn, weights["ffn_w_up", layer], f_cw[layer], name, rider=ag_rider(name))
        ag_done(name, r)
        saved["ffn", layer] = (up2, cv2, act)
        name = f"ffn_out{layer}"
        (h, xn), r = proj_residual(act, weights["ffn_w_down", layer], h, name,
                                   g_next=mixer_gain(layer + 1), rider=ag_rider(name))
        ag_done(name, r)
        if layer == N_A - 1:
            (k_all, v_all, c_kv), r = kv_fwd(h, kv_in_norm[None], weights["w_dkv", 0], kv_norm[None],
                                             weights["w_ukv", 0], cos, sin, "kv_fwd",
                                             rider=ag_rider("kv_fwd"))
            ag_done("kv_fwd", r)
            kv = (k_all, v_all, c_kv)

    sq_err, dh, d_final = loss_head(h, final_norm[None], target, "loss_head")

    grads = {}
    parts = {}

    pair_sums = {}

    def by_chip(g):
        return g.reshape((N_DEV // 2, 2) + g.shape[1:])

    def rs_rider(host):
        tasks = []
        for key in BWD_RIDERS.get(host, []):
            if len(key) == 2:
                tasks.append(("rs", grads[key]))
            elif key[2] == "pair":
                tasks.append(("rs_pair", by_chip(grads[key[:2]])))
            else:
                tasks.append(("rs_chip", pair_sums[key[:2]]))
        return tasks

    def rs_done(host, outs):
        for key, p in zip(BWD_RIDERS.get(host, []), outs):
            if len(key) == 3 and key[2] == "pair":
                pair_sums[key[:2]] = pair_sum(by_chip(grads[key[:2]]), p, f"pair_sum_{key[0]}{key[1]}")
            else:
                parts[key[:2]] = p

    d_ffn_norm = [None] * DEPTH
    d_fconv = [None] * DEPTH
    d_a_gain = [None] * N_A
    d_aconv = [None] * N_A
    d_b_gain = [None] * N_B
    d_q_gain = [None] * N_B
    dks, dvs = [], []
    for layer in reversed(range(DEPTH)):
        if layer == N_A - 1:
            hk = saved["hm", layer + 1]
            (dh, dwukv, dwdkv, d_kv_gain, d_kvin_gain), r = kv_bwd(
                dks, dvs, kv[2], hk, kv_in_norm[None], dh, weights["w_dkv", 0], kv_norm[None],
                weights["w_ukv", 0], cos, sin, "kv_bwd", rider=rs_rider("kv_bwd"))
            rs_done("kv_bwd", r)
            grads["w_ukv", 0] = dwukv
            grads["w_dkv", 0] = dwdkv[:, :KV_RANK + ROPE].reshape(N_DEV, dshard, KV_RANK + ROPE)
        up2, cv2, act = saved["ffn", layer]
        name = f"ffn_bwd{layer}"
        (dup2, dwup, dwdown, dcw), r = ffn_bwd(dh, weights["ffn_w_down", layer], up2, cv2, act,
                                               saved["xf", layer], f_cw[layer], name, rider=rs_rider(name))
        rs_done(name, r)
        grads["ffn_w_up", layer] = dwup.reshape(N_DEV, d, f8)
        grads["ffn_w_down", layer] = dwdown.reshape(N_DEV, fd, d)
        d_fconv[layer] = dcw.reshape(N_DEV, 3, f8)
        name = f"ffn_in_bwd{layer}"
        (dh, d_ffn_norm[layer]), r = proj_t_rms_bwd(dup2.reshape(N_DEV, seq, f8), weights["ffn_w_up", layer],
                                                    saved["hf", layer], ffn_norm[layer][None], dh, name,
                                                    rider=rs_rider(name))
        rs_done(name, r)
        hm, xm = saved["hm", layer], saved["xm", layer]
        if layer < N_A:
            u4, z = saved["mix", layer]
            name = f"mixer_bwd{layer}"
            (du3, dwin3, dwout, dcw), r = mixer_bwd(dh, weights["a_w_out", layer], u4, z, xm, a_cw[layer],
                                                    name, rider=rs_rider(name))
            rs_done(name, r)
            dwin = jnp.transpose(dwin3, (1, 0, 2)).reshape(d, N_DEV, 3 * d // N_DEV)
            grads["a_w_in", layer] = jnp.transpose(dwin, (1, 0, 2))
            grads["a_w_out", layer] = dwout.reshape(N_DEV, dshard, d)
            d_aconv[layer] = dcw
            name = f"mixer_in_bwd{layer}"
            extra = []
            if layer == 0:
                early_small = [
                    d_a_gain[1],
                    d_aconv[1],
                    jnp.concatenate(d_b_gain, axis=0),
                    jnp.concatenate(d_q_gain, axis=0),
                    d_kvin_gain[0],
                    d_kv_gain[0],
                    jnp.concatenate(d_ffn_norm, axis=0),
                    jnp.stack(d_fconv),
                    d_final[0],
                ]
                extra = [("ag", _pack(early_small))]
            (dh, d_a_gain[layer]), r = proj_t_rms_bwd(du3, weights["a_w_in", layer], hm, a_gain[layer][None],
                                                      dh, name, rider=rs_rider(name) + extra)
            rs_done(name, r)
            if layer == 0:
                g_early = r[-1]
        else:
            j = layer - N_A
            q, o, lse = saved["attn", layer]
            name = f"attn_out_bwd{j}"
            (do, dwo), r = o_bwd(dh, o, weights["b_w_o", j], name, rider=rs_rider(name))
            rs_done(name, r)
            grads["b_w_o", j] = dwo.reshape(N_DEV, dshard, d)
            name = f"attn_bwd{j}"
            (dq, dk, dv), r = attn_bwd(q, kv[0], kv[1], o, do, lse, name, rider=rs_rider(name))
            rs_done(name, r)
            dks.append(dk)
            dvs.append(dv)
            name = f"q_bwd{j}"
            (dh, dwuq, dwdq, d_q_gain[j], d_b_gain[j]), r = q_bwd(
                dq, xm, hm, b_mix_norm[j][None], dh, weights["b_w_dq", j], b_q_norm[j][None],
                weights["b_w_uq", j], cos, sin, name, rider=rs_rider(name))
            rs_done(name, r)
            grads["b_w_uq", j] = dwuq[:, :, :NOPE + ROPE]
            grads["b_w_dq", j] = dwdq.reshape(N_DEV, dshard, rank)
    grad_x = dh.reshape(x.shape)

    late_small = [d_a_gain[0], d_aconv[0], sq_err[:, 0:1]]
    full_shapes = [t.shape for t in early_small + late_small]
    small_pack = _pack(late_small)

    res = {}

    def update(name, n_layers, w, m, v, extra=(), transposed=False):
        view = (lambda t: jnp.transpose(t, (0, 2, 1))) if transposed else (lambda t: t)
        call = sum_adamw_transposed if transposed else sum_adamw
        shard = w.shape if w.ndim == 3 else (1,) + w.shape
        host = f"adamw_{name}"
        outs, r = call([parts[name, l] for l in range(n_layers)], view(w.reshape(shard)),
                       view(m.reshape(shard)), view(v.reshape(shard)), host,
                       rider=rs_rider(host) + list(extra))
        rs_done(host, r)
        res[name] = [view(t).reshape(w.shape) for t in outs]
        return r[len(BWD_RIDERS.get(host, [])):]

    (g_late,) = update("a_w_out", N_A, a_w_out, m_a_w_out, v_a_w_out, extra=[("ag", small_pack)])
    update("ffn_w_down", DEPTH, ffn_w_down, m_ffn_w_down, v_ffn_w_down)
    update("ffn_w_up", DEPTH, ffn_w_up, m_ffn_w_up, v_ffn_w_up, transposed=True)
    update("b_w_dq", N_B, b_w_dq, m_b_w_dq, v_b_w_dq)
    update("b_w_uq", N_B, b_w_uq, m_b_w_uq, v_b_w_uq)
    update("b_w_o", N_B, b_w_o, m_b_w_o, v_b_w_o)
    update("w_dkv", 1, w_dkv, m_w_dkv, v_w_dkv)
    update("w_ukv", 1, w_ukv, m_w_ukv, v_w_ukv)
    update("a_w_in", N_A, a_w_in, m_a_w_in, v_a_w_in)

    summed = sum_slots(jnp.concatenate([g_early, g_late], axis=1), "sum_small_grads")
    (s_a_gain1, s_aconv1, s_b_gain, s_q_gain, s_kvin, s_kvn, s_ffn_gain, s_fconv_g,
     s_final, s_a_gain0, s_aconv0, s_sq_err) = _unpack(summed, full_shapes)
    loss = s_sq_err[0, 0] * (0.5 / d)
    s_a_gain = jnp.concatenate([s_a_gain0, s_a_gain1], axis=0)
    s_aconv_g = jnp.stack([s_aconv0, s_aconv1])
    dsl = d // N_DEV
    small = [
        ("a_mix_norm", lax.dynamic_slice_in_dim(s_a_gain, me * dsl, dsl, axis=1), a_mix_norm, m_a_mix_norm, v_a_mix_norm),
        ("a_conv", lax.dynamic_slice_in_dim(s_aconv_g, me * dsl, dsl, axis=2), a_conv, m_a_conv, v_a_conv),
        ("b_mix_norm", s_b_gain, b_mix_norm, m_b_mix_norm, v_b_mix_norm),
        ("b_q_norm", s_q_gain, b_q_norm, m_b_q_norm, v_b_q_norm),
        ("kv_in_norm", s_kvin, kv_in_norm, m_kv_in_norm, v_kv_in_norm),
        ("kv_norm", s_kvn, kv_norm, m_kv_norm, v_kv_norm),
        ("ffn_norm", s_ffn_gain, ffn_norm, m_ffn_norm, v_ffn_norm),
        ("ffn_conv", lax.dynamic_index_in_dim(s_fconv_g, me, axis=1, keepdims=False), ffn_conv, m_ffn_conv, v_ffn_conv),
        ("final_norm", s_final, final_norm, m_final_norm, v_final_norm),
    ]
    shapes = [t[2].shape for t in small]
    packed = [_pack([t[k] for t in small])[None] for k in (1, 2, 3, 4)]
    outs, _ = sum_adamw([packed[0]], packed[1], packed[2], packed[3], "adamw_small")
    unpacked = [_unpack(t[0], shapes) for t in outs]
    for idx, t in enumerate(small):
        res[t[0]] = [unpacked[k][idx] for k in range(4)]

    order = ["a_mix_norm", "a_w_in", "a_conv", "a_w_out", "b_mix_norm", "b_w_dq", "b_q_norm",
             "b_w_uq", "b_w_o", "kv_in_norm", "w_dkv", "kv_norm", "w_ukv", "ffn_norm",
             "ffn_w_up", "ffn_conv", "ffn_w_down", "final_norm"]
    return (loss, grad_x, *[res[n][0] for n in order], *[res[n][1] for n in order],
            *[res[n][2] for n in order], *[res[n][3] for n in order])
```

```python
import functools

import numpy as np
import jax
import jax.numpy as jnp
from jax import lax
from jax.experimental import pallas as pl
from jax.experimental.pallas import tpu as pltpu

F32 = jnp.float32
BF16 = jnp.bfloat16

N_DEV = 8
N_HEADS = 8
NOPE = 128
ROPE = 64
ROPE_PAD = 128
QK = NOPE + ROPE_PAD
VDIM = 128
KV_RANK = 256
ROPE_THETA = 10000.0
RMS_EPS = 1e-6
ATTN_SCALE = (NOPE + ROPE) ** -0.5
N_A = 2
N_B = 2
DEPTH = 4

ADAM_LR = 0.001
ADAM_B1 = 0.9
ADAM_B2 = 0.999
ADAM_EPS = 1e-08
ADAM_WD = 0.01
ADAM_STEP = 10

V7X_VMEM_LIMIT = 56 * 1024 * 1024
BF16_SUBLANES = 16
ROW_TILE = 512
ROW_TILE_LARGE = 1024
ADAM_ROWS = 256
ATTN_TILE = 512
MIXER_CHUNK = 512
LANES = 128
NEG_BIG = -1e30
COPIES_PER_TASK = 7

MESH_ID = pl.DeviceIdType.MESH
ANY = pl.BlockSpec(memory_space=pl.ANY)


def _nt(a, b):
    return lax.dot_general(a, b, (((1,), (1,)), ((), ())), preferred_element_type=F32)


def _tn(a, b):
    return lax.dot_general(a, b, (((0,), (0,)), ((), ())), preferred_element_type=F32)


def _nn(a, b):
    return jnp.dot(a, b, preferred_element_type=F32)


def _rms(h, g):
    rstd = lax.rsqrt(jnp.mean(h * h, axis=-1, keepdims=True) + RMS_EPS)
    xhat = h * rstd
    return xhat * g, xhat, rstd


def _rms_bwd(dxn, xhat, rstd, g):
    dxhat = dxn * g
    dh = rstd * (dxhat - xhat * jnp.mean(dxhat * xhat, axis=-1, keepdims=True))
    return dh, dxn * xhat


def _shift_down(x, k, halo_rows):
    r = pltpu.roll(x, k, 0)
    row = lax.broadcasted_iota(jnp.int32, x.shape, 0)
    for t in range(k):
        r = jnp.where(row == t, halo_rows[t], r)
    return r


def _shift_up(x, k, halo_rows):
    n = x.shape[0]
    r = pltpu.roll(x, n - k, 0)
    row = lax.broadcasted_iota(jnp.int32, x.shape, 0)
    for t in range(k):
        r = jnp.where(row == n - k + t, halo_rows[t], r)
    return r


def _conv_taps(w_ref):
    return w_ref[0:1, :], w_ref[1:2, :], w_ref[2:3, :]


def _rope_swap(x):
    lane = lax.broadcasted_iota(jnp.int32, x.shape, 1)
    return jnp.where(lane < ROPE // 2, pltpu.roll(x, ROPE_PAD - ROPE // 2, 1),
                     pltpu.roll(x, ROPE // 2, 1))


def _rope_fwd(x, cos, sin):
    return x * cos + _rope_swap(x) * sin


def _rope_bwd(dy, cos, sin):
    return dy * cos - _rope_swap(dy) * sin


def _rope_tables(seq):
    inv = (1.0 / np.power(np.float32(ROPE_THETA), np.arange(0, ROPE, 2, dtype=np.float32) / np.float32(ROPE)))
    ang = np.arange(seq, dtype=np.float32)[:, None] * inv.astype(np.float32)[None, :]
    ang = ang.astype(np.float32).astype(np.float64)
    cos, sin = np.cos(ang).astype(np.float32), np.sin(ang).astype(np.float32)
    zero = np.zeros((seq, ROPE_PAD - ROPE), np.float32)
    return (jnp.asarray(np.concatenate([cos, cos, zero], axis=1)),
            jnp.asarray(np.concatenate([-sin, sin, zero], axis=1)))


def _row_tile(rows, cap, mult=8):
    best = None
    for t in range(mult, min(rows, cap) + 1, mult):
        if rows % t == 0:
            best = t
    return rows if best is None else best


class _AllGatherTask:
    def __init__(self, t, x_ref, out_ref, send_sems, recv_sems, local_sems):
        self.t, self.x_ref, self.out_ref = t, x_ref, out_ref
        self.send_sems, self.recv_sems, self.local_sems = send_sems, recv_sems, local_sems
        mx, my, mc = lax.axis_index("x"), lax.axis_index("y"), lax.axis_index("c")
        self.mc = mc
        self.me, self.sibling = (mx, my, mc), (mx, my, 1 - mc)
        self.chips = [(1 - mx, my), (mx, 1 - my), (1 - mx, 1 - my)]

    def _slot(self, px, py, pc):
        return self.out_ref.at[4 * px + 2 * py + pc]

    def _copy(self, k, block, to, src=None):
        s = COPIES_PER_TASK * self.t + k
        return pltpu.make_async_remote_copy(
            src_ref=self._slot(*block) if src is None else src, dst_ref=self._slot(*block),
            send_sem=self.send_sems.at[s], recv_sem=self.recv_sems.at[s],
            device_id=to, device_id_type=MESH_ID)

    def _mine(self):
        return pltpu.make_async_copy(self.x_ref, self._slot(*self.me), self.local_sems.at[self.t])

    def _first(self):
        out = [self._copy(0, self.me, self.sibling, src=self.x_ref)]
        out += [self._copy(1 + j, self.me, (*chip, self.mc), src=self.x_ref)
                for j, chip in enumerate(self.chips)]
        return out

    def _passed(self):
        return [self._copy(4 + j, (*chip, self.mc), self.sibling) for j, chip in enumerate(self.chips)]

    def start(self):
        self._mine().start()
        for cp in self._first():
            cp.start()

    def forward(self):
        passed = self._passed()
        for j, chip in enumerate(self.chips):
            self._copy(1 + j, (*chip, self.mc), self.me).wait_recv()
            passed[j].start()

    def finish(self):
        self._copy(0, self.sibling, self.me).wait_recv()
        for j, chip in enumerate(self.chips):
            self._copy(4 + j, (*chip, 1 - self.mc), self.me).wait_recv()
        for cp in self._first() + self._passed():
            cp.wait_send()
        self._mine().wait()


class _ReduceScatterTask:
    def __init__(self, t, g_ref, out_ref, send_sems, recv_sems, local_sems):
        self.t, self.g_ref, self.out_ref = t, g_ref, out_ref
        self.send_sems, self.recv_sems, self.local_sems = send_sems, recv_sems, local_sems
        mx, my, mc = lax.axis_index("x"), lax.axis_index("y"), lax.axis_index("c")
        self.me = 4 * mx + 2 * my + mc
        self.peers = []
        for k in range(1, N_DEV):
            px, py, pc = mx ^ ((k >> 2) & 1), my ^ ((k >> 1) & 1), mc ^ (k & 1)
            self.peers.append(((px, py, pc), 4 * px + 2 * py + pc))

    def _mine(self):
        return pltpu.make_async_copy(self.g_ref.at[self.me], self.out_ref.at[self.me],
                                     self.local_sems.at[self.t])

    def _copy(self, k, src_slot, dst_slot):
        s = COPIES_PER_TASK * self.t + k
        return pltpu.make_async_remote_copy(
            src_ref=self.g_ref.at[src_slot], dst_ref=self.out_ref.at[dst_slot],
            send_sem=self.send_sems.at[s], recv_sem=self.recv_sems.at[s],
            device_id=self.peers[k][0], device_id_type=MESH_ID)

    def start(self):
        self._mine().start()
        for k, (_, peer) in enumerate(self.peers):
            self._copy(k, peer, self.me).start()

    def forward(self):
        pass

    def finish(self):
        for k, (_, peer) in enumerate(self.peers):
            self._copy(k, self.me, peer).wait_recv()
        for k, (_, peer) in enumerate(self.peers):
            self._copy(k, peer, self.me).wait_send()
        self._mine().wait()


class _PairExchangeTask:
    def __init__(self, t, g_ref, out_ref, send_sems, recv_sems, local_sems):
        mx, my, mc = lax.axis_index("x"), lax.axis_index("y"), lax.axis_index("c")
        s = COPIES_PER_TASK * t
        self.copy = pltpu.make_async_remote_copy(
            src_ref=g_ref.at[:, 1 - mc], dst_ref=out_ref,
            send_sem=send_sems.at[s], recv_sem=recv_sems.at[s],
            device_id=(mx, my, 1 - mc), device_id_type=MESH_ID)

    def start(self):
        self.copy.start()

    def forward(self):
        pass

    def finish(self):
        self.copy.wait()


class _ChipScatterTask:
    def __init__(self, t, s_ref, out_ref, send_sems, recv_sems, local_sems):
        self.t, self.s_ref, self.out_ref = t, s_ref, out_ref
        self.send_sems, self.recv_sems, self.local_sems = send_sems, recv_sems, local_sems
        mx, my, mc = lax.axis_index("x"), lax.axis_index("y"), lax.axis_index("c")
        self.chip = 2 * mx + my
        self.peers = []
        for k in range(1, N_DEV // 2):
            px, py = mx ^ ((k >> 1) & 1), my ^ (k & 1)
            self.peers.append(((px, py, mc), 2 * px + py))

    def _mine(self):
        return pltpu.make_async_copy(self.s_ref.at[self.chip], self.out_ref.at[self.chip],
                                     self.local_sems.at[self.t])

    def _copy(self, k, src_slot, dst_slot):
        s = COPIES_PER_TASK * self.t + k
        return pltpu.make_async_remote_copy(
            src_ref=self.s_ref.at[src_slot], dst_ref=self.out_ref.at[dst_slot],
            send_sem=self.send_sems.at[s], recv_sem=self.recv_sems.at[s],
            device_id=self.peers[k][0], device_id_type=MESH_ID)

    def start(self):
        self._mine().start()
        for k, (_, peer) in enumerate(self.peers):
            self._copy(k, peer, self.chip).start()

    def forward(self):
        pass

    def finish(self):
        for k, (_, peer) in enumerate(self.peers):
            self._copy(k, self.chip, peer).wait_recv()
        for k, (_, peer) in enumerate(self.peers):
            self._copy(k, peer, self.chip).wait_send()
        self._mine().wait()


_TASKS = {"ag": _AllGatherTask, "rs": _ReduceScatterTask, "rs_pair": _PairExchangeTask,
          "rs_chip": _ChipScatterTask}


def _task_shape(kind, arr):
    if isinstance(kind, tuple):
        return jax.ShapeDtypeStruct((N_DEV,) + arr.shape[1:], arr.dtype)
    shape = {"ag": (N_DEV,) + arr.shape, "rs": arr.shape, "rs_chip": arr.shape,
             "rs_pair": arr.shape[:1] + arr.shape[2:]}[kind]
    return jax.ShapeDtypeStruct(shape, arr.dtype)


def _sem_shapes(n_tasks):
    return [pltpu.SemaphoreType.DMA((COPIES_PER_TASK * n_tasks,)),
            pltpu.SemaphoreType.DMA((COPIES_PER_TASK * n_tasks,)),
            pltpu.SemaphoreType.DMA((n_tasks,))]


def _make_tasks(rider, in_refs, out_refs, sems):
    tasks = []
    for t, (kind, _) in enumerate(rider):
        if isinstance(kind, tuple):
            tasks.append(_AllGatherTask(t, in_refs[t].at[kind[1]], out_refs[t], *sems))
        else:
            tasks.append(_TASKS[kind](t, in_refs[t], out_refs[t], *sems))
    return tasks


def exchange(rider, name):
    n = len(rider)

    def body(*refs):
        tasks = _make_tasks(rider, refs[:n], refs[n:2 * n], refs[2 * n:])
        for task in tasks:
            task.start()
        for task in tasks:
            task.forward()
        for task in tasks:
            task.finish()

    return list(pl.pallas_call(
        body, name=name, out_shape=tuple(_task_shape(k, a) for k, a in rider),
        in_specs=[ANY] * n, out_specs=(ANY,) * n, scratch_shapes=_sem_shapes(n),
    )(*[a for _, a in rider]))


def _call(body, name, grid, in_specs, out_specs, out_shape, args, scratch=(), rider=()):
    in_specs, out_specs, out_shape = list(in_specs), tuple(out_specs), tuple(out_shape)
    n_in, n_out, n_scr, n_r = len(in_specs), len(out_specs), len(scratch), len(rider)
    if n_r:
        def kern(*refs):
            ins, r_in = refs[:n_in], refs[n_in:n_in + n_r]
            o0 = n_in + n_r
            outs, r_out = refs[o0:o0 + n_out], refs[o0 + n_out:o0 + n_out + n_r]
            s0 = o0 + n_out + n_r
            scr, sems = refs[s0:s0 + n_scr], refs[s0 + n_scr:]
            step = 0
            for a, n in enumerate(grid):
                step = step * n + pl.program_id(a)
            n_steps = 1
            for n in grid:
                n_steps *= n

            @pl.when(step == 0)
            def _():
                for task in _make_tasks(rider, r_in, r_out, sems):
                    task.start()
            body(*ins, *outs, *scr)

            @pl.when(step == n_steps - 1)
            def _():
                tasks = _make_tasks(rider, r_in, r_out, sems)
                for task in tasks:
                    task.forward()
                for task in tasks:
                    task.finish()
    else:
        kern = body
    res = pl.pallas_call(
        kern, name=name, grid=grid,
        in_specs=in_specs + [ANY] * n_r, out_specs=out_specs + (ANY,) * n_r,
        out_shape=out_shape + tuple(_task_shape(k, a) for k, a in rider),
        scratch_shapes=list(scratch) + (_sem_shapes(n_r) if n_r else []),
        compiler_params=pltpu.CompilerParams(dimension_semantics=("arbitrary",) * len(grid),
                                             vmem_limit_bytes=V7X_VMEM_LIMIT),
    )(*args, *[a for _, a in rider])
    return list(res[:n_out]), list(res[n_out:])


def _adamw(g, w, m, v):
    m = ADAM_B1 * m + (1.0 - ADAM_B1) * g
    v = ADAM_B2 * v + (1.0 - ADAM_B2) * (g * g)
    m_hat = m / (1.0 - ADAM_B1 ** ADAM_STEP)
    v_hat = v / (1.0 - ADAM_B2 ** ADAM_STEP)
    delta = -ADAM_LR * (m_hat / (jnp.sqrt(v_hat) + ADAM_EPS) + ADAM_WD * w)
    return delta, m, v


def sum_adamw(parts, w, m, v, name, rider=()):
    n_l, rows, cols = w.shape
    mult = BF16_SUBLANES if parts[0].dtype == BF16 else 8
    tr = _row_tile(rows, ADAM_ROWS, mult)
    n_i = rows // tr

    def body(*refs):
        part_refs = refs[:n_l]
        w_ref, m_ref, v_ref, g_out, d_out, m_out, v_out = refs[n_l:]
        layer = pl.program_id(0)
        for k in range(n_l):
            @pl.when(layer == k)
            def _(k=k):
                g = part_refs[k][0].astype(F32)
                for s in range(1, parts[k].shape[0]):
                    g = g + part_refs[k][s].astype(F32)
                delta, m_new, v_new = _adamw(g, w_ref[...], m_ref[...], v_ref[...])
                g_out[...] = g
                d_out[...] = delta
                m_out[...] = m_new
                v_out[...] = v_new

    part_specs = [pl.BlockSpec((parts[k].shape[0], tr, cols), functools.partial(
        lambda l, i, k: (0, jnp.where(l == k, i, 0), 0), k=k)) for k in range(n_l)]
    wspec = pl.BlockSpec((None, tr, cols), lambda l, i: (l, i, 0))
    shape = jax.ShapeDtypeStruct(w.shape, F32)
    return _call(body, name, (n_l, n_i), part_specs + [wspec] * 3, (wspec,) * 4, (shape,) * 4,
                 (*parts, w, m, v), rider=rider)


def sum_adamw_transposed(parts, w_t, m_t, v_t, name, rider=()):
    n_l, cols, rows = w_t.shape
    tr = LANES
    n_i = rows // tr
    starts = list(range(0, cols - LANES + 1, LANES))
    if starts[-1] + LANES < cols:
        starts.append(cols - LANES)

    def body(*refs):
        part_refs = refs[:n_l]
        w_ref, m_ref, v_ref, g_out, d_out, m_out, v_out = refs[n_l:]
        layer = pl.program_id(0)
        for k in range(n_l):
            @pl.when(layer == k)
            def _(k=k):
                for c0 in starts:
                    piece = pl.ds(c0, LANES)
                    g = part_refs[k][0, :, piece].astype(F32)
                    for s in range(1, parts[k].shape[0]):
                        g = g + part_refs[k][s, :, piece].astype(F32)
                    g = g.T
                    delta, m_new, v_new = _adamw(g, w_ref[piece, :], m_ref[piece, :], v_ref[piece, :])
                    g_out[piece, :] = g
                    d_out[piece, :] = delta
                    m_out[piece, :] = m_new
                    v_out[piece, :] = v_new

    part_specs = [pl.BlockSpec((parts[k].shape[0], tr, cols), functools.partial(
        lambda l, i, k: (0, jnp.where(l == k, i, 0), 0), k=k)) for k in range(n_l)]
    wspec = pl.BlockSpec((None, cols, tr), lambda l, i: (l, 0, i))
    shape = jax.ShapeDtypeStruct(w_t.shape, F32)
    return _call(body, name, (n_l, n_i), part_specs + [wspec] * 3, (wspec,) * 4, (shape,) * 4,
                 (*parts, w_t, m_t, v_t), rider=rider)


def pair_sum(g4, other, name):
    n_chip, _, rows, cols = g4.shape
    tr = _row_tile(rows, 512, BF16_SUBLANES)

    def body(core_ref, g_ref, o_ref, s_ref):
        s_ref[...] = (g_ref[...].astype(F32) + o_ref[...].astype(F32)).astype(BF16)

    blk = pl.BlockSpec((None, tr, cols), lambda k, i, core: (k, i, 0))
    return pl.pallas_call(
        body, name=name, out_shape=jax.ShapeDtypeStruct((n_chip, rows, cols), g4.dtype),
        grid_spec=pltpu.PrefetchScalarGridSpec(
            num_scalar_prefetch=1, grid=(n_chip, rows // tr),
            in_specs=[pl.BlockSpec((None, None, tr, cols), lambda k, i, core: (k, core[0], i, 0)), blk],
            out_specs=blk),
        compiler_params=pltpu.CompilerParams(dimension_semantics=("arbitrary", "arbitrary"),
                                             vmem_limit_bytes=V7X_VMEM_LIMIT),
    )(lax.axis_index("c").astype(jnp.int32).reshape(1), g4, other)


def sum_slots(parts, name):
    n, rows, cols = parts.shape

    def body(p_ref, o_ref):
        acc = p_ref[0]
        for s in range(1, n):
            acc = acc + p_ref[s]
        o_ref[...] = acc

    return pl.pallas_call(
        body, name=name, out_shape=jax.ShapeDtypeStruct((rows, cols), F32),
        in_specs=[pl.BlockSpec(memory_space=pltpu.VMEM)],
        out_specs=pl.BlockSpec(memory_space=pltpu.VMEM),
    )(parts)


def norm_fwd(h, g, name):
    seq, d = h.shape
    tm = min(ROW_TILE, seq)

    def body(h_ref, g_ref, o_ref):
        o_ref[...] = _rms(h_ref[...], g_ref[...])[0].astype(BF16)

    return _call(body, name, (seq // tm,),
                 [pl.BlockSpec((tm, d), lambda i: (i, 0)), pl.BlockSpec((1, d), lambda i: (0, 0))],
                 [pl.BlockSpec((tm, d), lambda i: (i, 0))],
                 [jax.ShapeDtypeStruct((seq, d), BF16)], (h, g))[0][0]


def proj_residual(a, w, res, name, g_next=None, rider=()):
    nb, seq, kb = a.shape
    d = w.shape[-1]
    tm = min(ROW_TILE, seq)
    with_norm = g_next is not None

    def body(a_ref, w_ref, r_ref, *rest):
        acc = r_ref[...]
        for b in range(nb):
            acc = acc + _nn(a_ref[b], w_ref[b])
        if with_norm:
            g_ref, o_ref, xn_ref = rest
            xn_ref[...] = _rms(acc, g_ref[...])[0].astype(BF16)
        else:
            (o_ref,) = rest
        o_ref[...] = acc

    row = pl.BlockSpec((tm, d), lambda i: (i, 0))
    in_specs = [pl.BlockSpec((nb, tm, kb), lambda i: (0, i, 0)),
                pl.BlockSpec((nb, kb, d), lambda i: (0, 0, 0)), row]
    args = [a, w, res]
    out_specs, out_shape = [row], [jax.ShapeDtypeStruct((seq, d), F32)]
    if with_norm:
        in_specs.append(pl.BlockSpec((1, d), lambda i: (0, 0)))
        args.append(g_next)
        out_specs.append(row)
        out_shape.append(jax.ShapeDtypeStruct((seq, d), BF16))
    outs, r_outs = _call(body, name, (seq // tm,), in_specs, out_specs, out_shape, args, rider=rider)
    return (outs[0], outs[1] if with_norm else None), r_outs


def proj_residual_loss(a, w, res, g, target, name):
    nb, seq, kb = a.shape
    d = w.shape[-1]
    tm = min(ROW_TILE, seq)

    def body(a_ref, w_ref, r_ref, g_ref, t_ref, l_ref, dh_ref, dg_ref):
        i = pl.program_id(0)

        @pl.when(i == 0)
        def _():
            l_ref[...] = jnp.zeros_like(l_ref)
            dg_ref[...] = jnp.zeros_like(dg_ref)
        acc = r_ref[...]
        for b in range(nb):
            acc = acc + _nn(a_ref[b], w_ref[b])
        y, xhat, rstd = _rms(acc, g_ref[...])
        diff = y - t_ref[...]
        l_ref[...] += jnp.sum(jnp.sum(diff * diff, axis=1, keepdims=True), axis=0, keepdims=True)
        dh, dg_rows = _rms_bwd(diff * (1.0 / d), xhat, rstd, g_ref[...])
        dh_ref[...] = dh
        dg_ref[...] += jnp.sum(dg_rows, axis=0, keepdims=True)

    row = pl.BlockSpec((tm, d), lambda i: (i, 0))
    vec = pl.BlockSpec((1, d), lambda i: (0, 0))
    return _call(body, name, (seq // tm,),
                 [pl.BlockSpec((nb, tm, kb), lambda i: (0, i, 0)),
                  pl.BlockSpec((nb, kb, d), lambda i: (0, 0, 0)), row, vec, row],
                 (pl.BlockSpec((1, LANES), lambda i: (0, 0)), row, vec),
                 (jax.ShapeDtypeStruct((1, LANES), F32), jax.ShapeDtypeStruct((seq, d), F32),
                  jax.ShapeDtypeStruct((1, d), F32)),
                 (a, w, res, g, target))[0]


def proj_t_rms_bwd(du, w, h, g, dres, name, rider=()):
    nb, seq, wd = du.shape
    k = w.shape[1]
    big_weight = 2 * w.size * w.dtype.itemsize > V7X_VMEM_LIMIT // 4
    tm = min(ROW_TILE // 2 if big_weight else ROW_TILE, seq)

    def body(du_ref, w_ref, h_ref, g_ref, dr_ref, dh_ref, dg_ref):
        i = pl.program_id(0)
        dxn = _nt(du_ref[0], w_ref[0])
        for b in range(1, nb):
            dxn = dxn + _nt(du_ref[b], w_ref[b])
        _, xhat, rstd = _rms(h_ref[...], g_ref[...])
        dh, dg_rows = _rms_bwd(dxn, xhat, rstd, g_ref[...])
        dh_ref[...] = dr_ref[...] + dh

        @pl.when(i == 0)
        def _():
            dg_ref[...] = jnp.zeros_like(dg_ref)
        dg_ref[...] += jnp.sum(dg_rows, axis=0, keepdims=True)

    row = pl.BlockSpec((tm, k), lambda i: (i, 0))
    vec = pl.BlockSpec((1, k), lambda i: (0, 0))
    return _call(body, name, (seq // tm,),
                 [pl.BlockSpec((nb, tm, wd), lambda i: (0, i, 0)),
                  pl.BlockSpec((nb, k, wd), lambda i: (0, 0, 0)), row, vec, row],
                 (row, vec),
                 (jax.ShapeDtypeStruct((seq, k), F32), jax.ShapeDtypeStruct((1, k), F32)),
                 (du, w, h, g, dres), rider=rider)


def mixer_fwd(xn, win3, cw, name, rider=()):
    seq, d = xn.shape
    tm = min(ROW_TILE_LARGE, seq)
    cc = min(MIXER_CHUNK, d)
    n_c, n_i = d // cc, seq // tm

    def body(x_ref, w_ref, cw_ref, u_ref, z_ref, carry):
        i = pl.program_id(1)

        @pl.when(i == 0)
        def _():
            carry[...] = jnp.zeros_like(carry)
        xb = x_ref[...]
        b = _nn(xb, w_ref[0])
        c = _nn(xb, w_ref[1])
        hh = _nn(xb, w_ref[2])
        p = c * hh
        w0, w1, w2 = _conv_taps(cw_ref)
        p1 = _shift_down(p, 1, [carry[7:8, :]])
        p2 = _shift_down(p, 2, [carry[6:7, :], carry[7:8, :]])
        q = w0 * p2 + w1 * p1 + w2 * p
        carry[...] = p[tm - 8:tm, :]
        u_ref[0] = b.astype(BF16)
        u_ref[1] = c.astype(BF16)
        u_ref[2] = hh.astype(BF16)
        u_ref[3] = q.astype(BF16)
        z_ref[...] = (b * q).astype(BF16)

    return _call(body, name, (n_c, n_i),
                 [pl.BlockSpec((tm, d), lambda c, i: (i, 0)),
                  pl.BlockSpec((3, d, cc), lambda c, i: (0, 0, c)),
                  pl.BlockSpec((3, cc), lambda c, i: (0, c))],
                 (pl.BlockSpec((4, tm, cc), lambda c, i: (0, i, c)),
                  pl.BlockSpec((tm, cc), lambda c, i: (i, c))),
                 (jax.ShapeDtypeStruct((4, seq, d), BF16), jax.ShapeDtypeStruct((seq, d), BF16)),
                 (xn, win3, cw), scratch=[pltpu.VMEM((8, cc), F32)], rider=rider)


def mixer_bwd(dh, wout, u4, z, xn, cw, name, rider=()):
    seq, d = xn.shape
    tm = min(ROW_TILE, seq)
    cc = min(MIXER_CHUNK, d)
    n_c, n_i = d // cc, seq // tm

    def body(dh_ref, wout_ref, u_ref, z_ref, x_ref, cw_ref,
             du_ref, dwin_ref, dwout_ref, dcw_ref, acc_in, acc_out, acc_cw, carry):
        i = pl.program_id(1)

        @pl.when(i == 0)
        def _():
            acc_in[...] = jnp.zeros_like(acc_in)
            acc_out[...] = jnp.zeros_like(acc_out)
            acc_cw[...] = jnp.zeros_like(acc_cw)
            carry[...] = jnp.zeros_like(carry)
        dhb = dh_ref[...].astype(BF16)
        dz = _nt(dhb, wout_ref[...])
        acc_out[...] += _tn(z_ref[...], dhb)
        b = u_ref[0].astype(F32)
        c = u_ref[1].astype(F32)
        hh = u_ref[2].astype(F32)
        q = u_ref[3].astype(F32)
        p = c * hh
        db = dz * q
        dq = dz * b
        w0, w1, w2 = _conv_taps(cw_ref)
        dq1 = _shift_up(dq, 1, [carry[0:1, :]])
        dq2 = _shift_up(dq, 2, [carry[0:1, :], carry[1:2, :]])
        dp = w2 * dq + w1 * dq1 + w0 * dq2
        carry[...] = dq[0:8, :]
        acc_cw[0:1, :] += jnp.sum(dq2 * p, axis=0, keepdims=True)
        acc_cw[1:2, :] += jnp.sum(dq1 * p, axis=0, keepdims=True)
        acc_cw[2:3, :] += jnp.sum(dq * p, axis=0, keepdims=True)
        dbb = db.astype(BF16)
        dcb = (dp * hh).astype(BF16)
        dhhb = (dp * c).astype(BF16)
        du_ref[0] = dbb
        du_ref[1] = dcb
        du_ref[2] = dhhb
        xb = x_ref[...]
        acc_in[0] += _tn(xb, dbb)
        acc_in[1] += _tn(xb, dcb)
        acc_in[2] += _tn(xb, dhhb)

        @pl.when(i == n_i - 1)
        def _():
            dwin_ref[...] = acc_in[...].astype(BF16)
            dwout_ref[...] = acc_out[...].astype(BF16)
            dcw_ref[...] = acc_cw[0:3, :]

    rev = lambda c, i: (n_i - 1 - i, 0)
    return _call(body, name, (n_c, n_i),
                 [pl.BlockSpec((tm, d), rev),
                  pl.BlockSpec((cc, d), lambda c, i: (c, 0)),
                  pl.BlockSpec((4, tm, cc), lambda c, i: (0, n_i - 1 - i, c)),
                  pl.BlockSpec((tm, cc), lambda c, i: (n_i - 1 - i, c)),
                  pl.BlockSpec((tm, d), rev),
                  pl.BlockSpec((3, cc), lambda c, i: (0, c))],
                 (pl.BlockSpec((3, tm, cc), lambda c, i: (0, n_i - 1 - i, c)),
                  pl.BlockSpec((3, d, cc), lambda c, i: (0, 0, c)),
                  pl.BlockSpec((cc, d), lambda c, i: (c, 0)),
                  pl.BlockSpec((3, cc), lambda c, i: (0, c))),
                 (jax.ShapeDtypeStruct((3, seq, d), BF16), jax.ShapeDtypeStruct((3, d, d), BF16),
                  jax.ShapeDtypeStruct((d, d), BF16), jax.ShapeDtypeStruct((3, d), F32)),
                 (dh, wout, u4, z, xn, cw),
                 scratch=[pltpu.VMEM((3, d, cc), F32), pltpu.VMEM((cc, d), F32),
                          pltpu.VMEM((8, cc), F32), pltpu.VMEM((8, cc), F32)], rider=rider)


def _silu_parts(cg):
    sg = 1.0 / (1.0 + jnp.exp(-cg))
    return sg, cg * sg


def ffn_fwd(xn, wup, fcw, name, rider=()):
    seq, d = xn.shape
    f8 = wup.shape[-1]
    half = N_DEV // 2
    tm = min(ROW_TILE_LARGE, seq)
    n_i = seq // tm

    def body(x_ref, wg_ref, wu_ref, cg_ref, cu_ref, up_ref, cv_ref, a_ref, carry):
        i = pl.program_id(1)

        @pl.when(i == 0)
        def _():
            carry[...] = jnp.zeros_like(carry)
        xb = x_ref[...]
        conv = []
        for s, (w_ref, t_ref) in enumerate(((wg_ref, cg_ref), (wu_ref, cu_ref))):
            u = _nn(xb, w_ref[...])
            up_ref[s] = u.astype(BF16)
            w0, w1, w2 = _conv_taps(t_ref)
            u1 = _shift_down(u, 1, [carry[s, 7:8, :]])
            u2 = _shift_down(u, 2, [carry[s, 6:7, :], carry[s, 7:8, :]])
            cv = w0 * u2 + w1 * u1 + w2 * u
            cv_ref[s] = cv.astype(BF16)
            conv.append(cv)
            carry[s] = u[tm - 8:tm, :]
        _, silu = _silu_parts(conv[0])
        a_ref[...] = (silu * conv[1]).astype(BF16)

    blk = pl.BlockSpec((2, None, tm, f8), lambda c, i: (0, c, i, 0))
    big = jax.ShapeDtypeStruct((2, half, seq, f8), BF16)
    return _call(body, name, (half, n_i),
                 [pl.BlockSpec((tm, d), lambda c, i: (i, 0)),
                  pl.BlockSpec((None, d, f8), lambda c, i: (c, 0, 0)),
                  pl.BlockSpec((None, d, f8), lambda c, i: (c + half, 0, 0)),
                  pl.BlockSpec((None, 3, f8), lambda c, i: (c, 0, 0)),
                  pl.BlockSpec((None, 3, f8), lambda c, i: (c + half, 0, 0))],
                 (blk, blk, pl.BlockSpec((None, tm, f8), lambda c, i: (c, i, 0))),
                 (big, big, jax.ShapeDtypeStruct((half, seq, f8), BF16)),
                 (xn, wup, wup, fcw, fcw), scratch=[pltpu.VMEM((2, 8, f8), F32)], rider=rider)


def ffn_bwd(dh, wdown, up2, cv2, act, xn, fcw, name, rider=()):
    seq, d = xn.shape
    f8 = up2.shape[-1]
    fb = wdown.shape[1]
    half = N_DEV // 2
    tm = min(ROW_TILE, seq)
    n_i = seq // tm

    def body(dh_ref, wd_ref, up_ref, cv_ref, a_ref, x_ref, cg_ref, cu_ref,
             dup_ref, dwup_ref, dwd_ref, dcw_ref, acc_up, acc_down, acc_cw, carry):
        i = pl.program_id(1)

        @pl.when(i == 0)
        def _():
            acc_up[...] = jnp.zeros_like(acc_up)
            acc_down[...] = jnp.zeros_like(acc_down)
            acc_cw[...] = jnp.zeros_like(acc_cw)
            carry[...] = jnp.zeros_like(carry)
        dhb = dh_ref[...].astype(BF16)
        da = _nt(dhb, wd_ref[...])
        acc_down[...] += _tn(a_ref[...], dhb)
        cg = cv_ref[0].astype(F32)
        cu = cv_ref[1].astype(F32)
        sg, silu = _silu_parts(cg)
        dcg = da * cu * (sg + silu * (1.0 - sg))
        dcu = da * silu
        xb = x_ref[...]
        for s, (dc, t_ref) in enumerate(((dcg, cg_ref), (dcu, cu_ref))):
            w0, w1, w2 = _conv_taps(t_ref)
            d1 = _shift_up(dc, 1, [carry[s, 0:1, :]])
            d2 = _shift_up(dc, 2, [carry[s, 0:1, :], carry[s, 1:2, :]])
            du = (w2 * dc + w1 * d1 + w0 * d2).astype(BF16)
            carry[s] = dc[0:8, :]
            u = up_ref[s].astype(F32)
            acc_cw[s, 0:1, :] += jnp.sum(d2 * u, axis=0, keepdims=True)
            acc_cw[s, 1:2, :] += jnp.sum(d1 * u, axis=0, keepdims=True)
            acc_cw[s, 2:3, :] += jnp.sum(dc * u, axis=0, keepdims=True)
            dup_ref[s] = du
            acc_up[s] += _tn(xb, du)

        @pl.when(i == n_i - 1)
        def _():
            dwup_ref[...] = acc_up[...].astype(BF16)
            dwd_ref[...] = acc_down[...].astype(BF16)
            dcw_ref[...] = acc_cw[:, 0:3, :]

    rev = lambda c, i: (n_i - 1 - i, 0)
    blk = pl.BlockSpec((2, None, tm, f8), lambda c, i: (0, c, n_i - 1 - i, 0))
    return _call(body, name, (half, n_i),
                 [pl.BlockSpec((tm, d), rev),
                  pl.BlockSpec((None, fb, d), lambda c, i: (c, 0, 0)),
                  blk, blk,
                  pl.BlockSpec((None, tm, f8), lambda c, i: (c, n_i - 1 - i, 0)),
                  pl.BlockSpec((tm, d), rev),
                  pl.BlockSpec((None, 3, f8), lambda c, i: (c, 0, 0)),
                  pl.BlockSpec((None, 3, f8), lambda c, i: (c + half, 0, 0))],
                 (blk,
                  pl.BlockSpec((2, None, d, f8), lambda c, i: (0, c, 0, 0)),
                  pl.BlockSpec((None, fb, d), lambda c, i: (c, 0, 0)),
                  pl.BlockSpec((2, None, 3, f8), lambda c, i: (0, c, 0, 0))),
                 (jax.ShapeDtypeStruct((2, half, seq, f8), BF16),
                  jax.ShapeDtypeStruct((2, half, d, f8), BF16),
                  jax.ShapeDtypeStruct((half, fb, d), BF16),
                  jax.ShapeDtypeStruct((2, half, 3, f8), F32)),
                 (dh, wdown, up2, cv2, act, xn, fcw, fcw),
                 scratch=[pltpu.VMEM((2, d, f8), F32), pltpu.VMEM((fb, d), F32),
                          pltpu.VMEM((2, 8, f8), F32), pltpu.VMEM((2, 8, f8), F32)], rider=rider)


def q_fwd(xn, wdq, gq, wuq, cos, sin, name, rider=()):
    seq, d = xn.shape
    rank = wdq.shape[-1]
    tm = min(ROW_TILE, seq)

    def body(x_ref, wdq_ref, gq_ref, wuq_ref, cos_ref, sin_ref, q_ref):
        qc = _nn(x_ref[...], wdq_ref[...])
        qn = _rms(qc, gq_ref[...])[0].astype(BF16)
        for hd in range(N_HEADS):
            qh = _nn(qn, wuq_ref[hd])
            qr = _rope_fwd(qh[:, NOPE:QK], cos_ref[...], sin_ref[...])
            q_ref[hd, :, 0:NOPE] = (qh[:, 0:NOPE] * ATTN_SCALE).astype(BF16)
            q_ref[hd, :, NOPE:QK] = (qr * ATTN_SCALE).astype(BF16)

    rope = pl.BlockSpec((tm, ROPE_PAD), lambda i: (i, 0))
    return _call(body, name, (seq // tm,),
                 [pl.BlockSpec((tm, d), lambda i: (i, 0)),
                  pl.BlockSpec((d, rank), lambda i: (0, 0)),
                  pl.BlockSpec((1, rank), lambda i: (0, 0)),
                  pl.BlockSpec((N_HEADS, rank, QK), lambda i: (0, 0, 0)), rope, rope],
                 [pl.BlockSpec((N_HEADS, tm, QK), lambda i: (0, i, 0))],
                 [jax.ShapeDtypeStruct((N_HEADS, seq, QK), BF16)],
                 (xn, wdq, gq, wuq, cos, sin), rider=rider)


def q_bwd(dq, xn, h, g, dres, wdq, gq, wuq, cos, sin, name, rider=()):
    seq, d = xn.shape
    rank = wdq.shape[-1]
    tm = min(ROW_TILE, seq)
    n_i = seq // tm

    def body(dq_ref, x_ref, h_ref, g_ref, dr_ref, wdq_ref, gq_ref, wuq_ref, cos_ref, sin_ref,
             dh_ref, dwuq_ref, dwdq_ref, dgq_ref, dg_ref, acc_uq, acc_dq):
        i = pl.program_id(0)

        @pl.when(i == 0)
        def _():
            acc_uq[...] = jnp.zeros_like(acc_uq)
            acc_dq[...] = jnp.zeros_like(acc_dq)
            dgq_ref[...] = jnp.zeros_like(dgq_ref)
            dg_ref[...] = jnp.zeros_like(dg_ref)
        xb = x_ref[...]
        qc = _nn(xb, wdq_ref[...])
        qn, qhat, qrstd = _rms(qc, gq_ref[...])
        qnb = qn.astype(BF16)
        dqn = jnp.zeros((tm, rank), F32)
        for hd in range(N_HEADS):
            dnope = (dq_ref[hd, :, 0:NOPE].astype(F32) * ATTN_SCALE).astype(BF16)
            drope = _rope_bwd(dq_ref[hd, :, NOPE:QK].astype(F32) * ATTN_SCALE, cos_ref[...], sin_ref[...])
            draw = jnp.concatenate([dnope, drope.astype(BF16)], axis=1)
            dqn = dqn + _nt(draw, wuq_ref[hd])
            acc_uq[hd] += _tn(qnb, draw)
        dqc, dg_rows = _rms_bwd(dqn, qhat, qrstd, gq_ref[...])
        dgq_ref[...] += jnp.sum(dg_rows, axis=0, keepdims=True)
        dqcb = dqc.astype(BF16)
        acc_dq[...] += _tn(xb, dqcb)
        _, xhat, rstd = _rms(h_ref[...], g_ref[...])
        dh, dg_rows = _rms_bwd(_nt(dqcb, wdq_ref[...]), xhat, rstd, g_ref[...])
        dh_ref[...] = dr_ref[...] + dh
        dg_ref[...] += jnp.sum(dg_rows, axis=0, keepdims=True)

        @pl.when(i == n_i - 1)
        def _():
            dwuq_ref[...] = acc_uq[...].astype(BF16)
            dwdq_ref[...] = acc_dq[...].astype(BF16)

    rope = pl.BlockSpec((tm, ROPE_PAD), lambda i: (i, 0))
    row = pl.BlockSpec((tm, d), lambda i: (i, 0))
    vec = pl.BlockSpec((1, d), lambda i: (0, 0))
    return _call(body, name, (n_i,),
                 [pl.BlockSpec((N_HEADS, tm, QK), lambda i: (0, i, 0)), row, row, vec, row,
                  pl.BlockSpec((d, rank), lambda i: (0, 0)),
                  pl.BlockSpec((1, rank), lambda i: (0, 0)),
                  pl.BlockSpec((N_HEADS, rank, QK), lambda i: (0, 0, 0)), rope, rope],
                 (row,
                  pl.BlockSpec((N_HEADS, rank, QK), lambda i: (0, 0, 0)),
                  pl.BlockSpec((d, rank), lambda i: (0, 0)),
                  pl.BlockSpec((1, rank), lambda i: (0, 0)), vec),
                 (jax.ShapeDtypeStruct((seq, d), F32),
                  jax.ShapeDtypeStruct((N_HEADS, rank, QK), BF16),
                  jax.ShapeDtypeStruct((d, rank), BF16),
                  jax.ShapeDtypeStruct((1, rank), F32),
                  jax.ShapeDtypeStruct((1, d), F32)),
                 (dq, xn, h, g, dres, wdq, gq, wuq, cos, sin),
                 scratch=[pltpu.VMEM((N_HEADS, rank, QK), F32), pltpu.VMEM((d, rank), F32)],
                 rider=rider)


def kv_fwd(h, g, wdkv, gkv, wukv, cos, sin, name, rider=()):
    seq, d = h.shape
    tm = min(ROW_TILE, seq)
    wk = KV_RANK + ROPE_PAD

    def body(h_ref, g_ref, wdkv_ref, gkv_ref, wukv_ref, cos_ref, sin_ref, k_ref, v_ref, c_ref):
        xk = _rms(h_ref[...], g_ref[...])[0].astype(BF16)
        ckv = _nn(xk, wdkv_ref[...])
        c_kv = ckv[:, 0:KV_RANK]
        c_ref[...] = c_kv
        kr = _rope_fwd(ckv[:, KV_RANK:wk], cos_ref[...], sin_ref[...]).astype(BF16)
        ckn = _rms(c_kv, gkv_ref[...])[0].astype(BF16)
        for hd in range(N_HEADS):
            kvh = _nn(ckn, wukv_ref[hd])
            k_ref[hd, :, 0:NOPE] = kvh[:, 0:NOPE].astype(BF16)
            k_ref[hd, :, NOPE:QK] = kr
            v_ref[hd] = kvh[:, NOPE:NOPE + VDIM].astype(BF16)

    rope = pl.BlockSpec((tm, ROPE_PAD), lambda i: (i, 0))
    return _call(body, name, (seq // tm,),
                 [pl.BlockSpec((tm, d), lambda i: (i, 0)),
                  pl.BlockSpec((1, d), lambda i: (0, 0)),
                  pl.BlockSpec((d, wk), lambda i: (0, 0)),
                  pl.BlockSpec((1, KV_RANK), lambda i: (0, 0)),
                  pl.BlockSpec((N_HEADS, KV_RANK, NOPE + VDIM), lambda i: (0, 0, 0)), rope, rope],
                 (pl.BlockSpec((N_HEADS, tm, QK), lambda i: (0, i, 0)),
                  pl.BlockSpec((N_HEADS, tm, VDIM), lambda i: (0, i, 0)),
                  pl.BlockSpec((tm, KV_RANK), lambda i: (i, 0))),
                 (jax.ShapeDtypeStruct((N_HEADS, seq, QK), BF16),
                  jax.ShapeDtypeStruct((N_HEADS, seq, VDIM), BF16),
                  jax.ShapeDtypeStruct((seq, KV_RANK), F32)),
                 (h, g, wdkv, gkv, wukv, cos, sin), rider=rider)


def kv_bwd(dks, dvs, c_kv, h, g, dres, wdkv, gkv, wukv, cos, sin, name, rider=()):
    seq, d = h.shape
    tm = min(ROW_TILE, seq)
    n_i = seq // tm
    wk = KV_RANK + ROPE_PAD
    n_b = len(dks)

    def body(*refs):
        dk_refs = refs[:n_b]
        dv_refs = refs[n_b:2 * n_b]
        (c_ref, h_ref, g_ref, dr_ref, wdkv_ref, gkv_ref, wukv_ref, cos_ref, sin_ref,
         dh_ref, dwukv_ref, dwdkv_ref, dgkv_ref, dg_ref, acc_ukv, acc_dkv) = refs[2 * n_b:]
        i = pl.program_id(0)

        @pl.when(i == 0)
        def _():
            acc_ukv[...] = jnp.zeros_like(acc_ukv)
            acc_dkv[...] = jnp.zeros_like(acc_dkv)
            dgkv_ref[...] = jnp.zeros_like(dgkv_ref)
            dg_ref[...] = jnp.zeros_like(dg_ref)
        ckn, chat, crstd = _rms(c_ref[...], gkv_ref[...])
        cknb = ckn.astype(BF16)
        dckn = jnp.zeros((tm, KV_RANK), F32)
        dkr = jnp.zeros((tm, ROPE_PAD), F32)
        for hd in range(N_HEADS):
            dk = dk_refs[0][hd].astype(F32)
            dv = dv_refs[0][hd].astype(F32)
            for j in range(1, n_b):
                dk = dk + dk_refs[j][hd].astype(F32)
                dv = dv + dv_refs[j][hd].astype(F32)
            dkr = dkr + dk[:, NOPE:QK]
            dkvh = jnp.concatenate([dk[:, 0:NOPE].astype(BF16), dv.astype(BF16)], axis=1)
            dckn = dckn + _nt(dkvh, wukv_ref[hd])
            acc_ukv[hd] += _tn(cknb, dkvh)
        dc_kv, dg_rows = _rms_bwd(dckn, chat, crstd, gkv_ref[...])
        dgkv_ref[...] += jnp.sum(dg_rows, axis=0, keepdims=True)
        dkr_raw = _rope_bwd(dkr, cos_ref[...], sin_ref[...])
        dckv = jnp.concatenate([dc_kv.astype(BF16), dkr_raw.astype(BF16)], axis=1)
        xk, xhat, rstd = _rms(h_ref[...], g_ref[...])
        acc_dkv[...] += _tn(xk.astype(BF16), dckv)
        dh, dg_rows = _rms_bwd(_nt(dckv, wdkv_ref[...]), xhat, rstd, g_ref[...])
        dh_ref[...] = dr_ref[...] + dh
        dg_ref[...] += jnp.sum(dg_rows, axis=0, keepdims=True)

        @pl.when(i == n_i - 1)
        def _():
            dwukv_ref[...] = acc_ukv[...].astype(BF16)
            dwdkv_ref[...] = acc_dkv[...].astype(BF16)

    kspec = pl.BlockSpec((N_HEADS, tm, QK), lambda i: (0, i, 0))
    vspec = pl.BlockSpec((N_HEADS, tm, VDIM), lambda i: (0, i, 0))
    rope = pl.BlockSpec((tm, ROPE_PAD), lambda i: (i, 0))
    row = pl.BlockSpec((tm, d), lambda i: (i, 0))
    vec = pl.BlockSpec((1, d), lambda i: (0, 0))
    return _call(body, name, (n_i,),
                 [kspec] * n_b + [vspec] * n_b + [
                     pl.BlockSpec((tm, KV_RANK), lambda i: (i, 0)), row, vec, row,
                     pl.BlockSpec((d, wk), lambda i: (0, 0)),
                     pl.BlockSpec((1, KV_RANK), lambda i: (0, 0)),
                     pl.BlockSpec((N_HEADS, KV_RANK, NOPE + VDIM), lambda i: (0, 0, 0)), rope, rope],
                 (row,
                  pl.BlockSpec((N_HEADS, KV_RANK, NOPE + VDIM), lambda i: (0, 0, 0)),
                  pl.BlockSpec((d, wk), lambda i: (0, 0)),
                  pl.BlockSpec((1, KV_RANK), lambda i: (0, 0)), vec),
                 (jax.ShapeDtypeStruct((seq, d), F32),
                  jax.ShapeDtypeStruct((N_HEADS, KV_RANK, NOPE + VDIM), BF16),
                  jax.ShapeDtypeStruct((d, wk), BF16),
                  jax.ShapeDtypeStruct((1, KV_RANK), F32),
                  jax.ShapeDtypeStruct((1, d), F32)),
                 (*dks, *dvs, c_kv, h, g, dres, wdkv, gkv, wukv, cos, sin),
                 scratch=[pltpu.VMEM((N_HEADS, KV_RANK, NOPE + VDIM), F32), pltpu.VMEM((d, wk), F32)],
                 rider=rider)


def o_bwd(dh, o, wo, name, rider=()):
    seq, d = dh.shape
    hv = o.shape[1]
    tm = min(ROW_TILE, seq)
    n_i = seq // tm

    def body(dh_ref, o_ref, wo_ref, do_ref, dwo_ref, acc):
        i = pl.program_id(0)

        @pl.when(i == 0)
        def _():
            acc[...] = jnp.zeros_like(acc)
        dhb = dh_ref[...].astype(BF16)
        do_ref[...] = _nt(dhb, wo_ref[...]).astype(BF16)
        acc[...] += _tn(o_ref[...], dhb)

        @pl.when(i == n_i - 1)
        def _():
            dwo_ref[...] = acc[...].astype(BF16)

    return _call(body, name, (n_i,),
                 [pl.BlockSpec((tm, d), lambda i: (i, 0)),
                  pl.BlockSpec((tm, hv), lambda i: (i, 0)),
                  pl.BlockSpec((hv, d), lambda i: (0, 0))],
                 (pl.BlockSpec((tm, hv), lambda i: (i, 0)),
                  pl.BlockSpec((hv, d), lambda i: (0, 0))),
                 (jax.ShapeDtypeStruct((seq, hv), BF16), jax.ShapeDtypeStruct((hv, d), BF16)),
                 (dh, o, wo), scratch=[pltpu.VMEM((hv, d), F32)], rider=rider)


def _mask_diagonal(s):
    row = lax.broadcasted_iota(jnp.int32, s.shape, 0)
    col = lax.broadcasted_iota(jnp.int32, s.shape, 1)
    return jnp.where(col <= row, s, NEG_BIG)


def attn_fwd(q, k, v, name, rider=()):
    _, seq, _ = q.shape
    t = min(ATTN_TILE, seq // 2)
    n_pair = seq // (2 * t)

    def body(q_ref, k_ref, v_ref, o_ref, lse_ref):
        qi = pl.program_id(1)
        q_a = q_ref[0:t, :]
        q_b = q_ref[t:2 * t, :]

        def rows(j):
            return pl.ds(pl.multiple_of(j * t, t), t)

        def update(qx, kb, vb, state, diagonal=False):
            m, l, acc = state
            s = _nt(qx, kb)
            if diagonal:
                s = _mask_diagonal(s)
            m_new = jnp.maximum(m, jnp.max(s, axis=1, keepdims=True))
            p = jnp.exp(s - m_new)
            alpha = jnp.exp(m - m_new)
            l = alpha * l + jnp.sum(p, axis=1, keepdims=True)
            acc = alpha * acc + _nn(p.astype(BF16), vb)
            return m_new, l, acc

        def step(j, carry):
            both = pl.ds(pl.multiple_of(j * 2 * t, 2 * t), 2 * t)
            kb, vb = k_ref[both, :], v_ref[both, :]
            return update(q_a, kb, vb, carry[0:3]) + update(q_b, kb, vb, carry[3:6])

        init = (jnp.full((t, 1), NEG_BIG, F32), jnp.zeros((t, 1), F32), jnp.zeros((t, VDIM), F32))
        carry = lax.fori_loop(0, qi, step, init + init)
        k0, v0 = k_ref[rows(2 * qi), :], v_ref[rows(2 * qi), :]
        k1, v1 = k_ref[rows(2 * qi + 1), :], v_ref[rows(2 * qi + 1), :]
        state_a = update(q_a, k0, v0, carry[0:3], diagonal=True)
        state_b = update(q_b, k1, v1, update(q_b, k0, v0, carry[3:6]), diagonal=True)
        for half, (m, l, acc) in enumerate((state_a, state_b)):
            o_ref[half * t:(half + 1) * t, :] = (acc / l).astype(BF16)
            lse_ref[half * t:(half + 1) * t, :] = jnp.broadcast_to(m + jnp.log(l), (t, LANES))

    return _call(body, name, (N_HEADS, n_pair),
                 [pl.BlockSpec((None, 2 * t, QK), lambda h, i: (h, i, 0)),
                  pl.BlockSpec((None, seq, QK), lambda h, i: (h, 0, 0)),
                  pl.BlockSpec((None, seq, VDIM), lambda h, i: (h, 0, 0))],
                 (pl.BlockSpec((2 * t, VDIM), lambda h, i: (i, h)),
                  pl.BlockSpec((None, 2 * t, LANES), lambda h, i: (h, i, 0))),
                 (jax.ShapeDtypeStruct((seq, N_HEADS * VDIM), BF16),
                  jax.ShapeDtypeStruct((N_HEADS, seq, LANES), F32)),
                 (q, k, v), rider=rider)


def attn_bwd(q, k, v, o, do, lse, name, rider=()):
    _, seq, _ = q.shape
    t = min(ATTN_TILE, seq // 2)
    n_q = seq // t
    n_pair = n_q // 2

    def body(q_ref, k_ref, v_ref, o_ref, do_ref, lse_ref, dq_ref, dk_ref, dv_ref,
             dq_acc, dk_acc, dv_acc):
        kj = pl.program_id(1)

        @pl.when(kj == 0)
        def _():
            dq_acc[...] = jnp.zeros_like(dq_acc)
        halves = (slice(0, t), slice(t, 2 * t))

        def block(i, masks, n_rows=t):
            rows = pl.ds(pl.multiple_of(i * n_rows, n_rows), n_rows)
            qb = q_ref[rows, :]
            dob = do_ref[rows, :]
            lse_col = lse_ref[rows, 0:1]
            delta = jnp.sum(dob.astype(F32) * o_ref[rows, :].astype(F32), axis=1, keepdims=True)
            dq, out = None, {}
            for x, diagonal in enumerate(masks):
                if diagonal is None:
                    continue
                kb, vb = k_ref[halves[x], :], v_ref[halves[x], :]
                s = _nt(qb, kb)
                if diagonal:
                    s = _mask_diagonal(s)
                p = jnp.exp(s - lse_col)
                ds = (p * (_nt(dob, vb) - delta)).astype(BF16)
                out[x] = (_tn(p.astype(BF16), dob), _tn(ds, qb))
                part = _nn(ds, kb)
                dq = part if dq is None else dq + part
            dq_acc[rows, :] += dq
            return out

        first = block(2 * kj, (True, None))
        second = block(2 * kj + 1, (False, True))
        dv_acc[halves[0], :] = first[0][0] + second[0][0]
        dk_acc[halves[0], :] = first[0][1] + second[0][1]
        dv_acc[halves[1], :] = second[1][0]
        dk_acc[halves[1], :] = second[1][1]

        def step(i, carry):
            out = block(i, (False, False), n_rows=2 * t)
            for x in (0, 1):
                dv_acc[halves[x], :] += out[x][0]
                dk_acc[halves[x], :] += out[x][1]
            return carry

        lax.fori_loop(kj + 1, n_pair, step, 0)
        dk_ref[...] = dk_acc[...].astype(BF16)
        dv_ref[...] = dv_acc[...].astype(BF16)

        @pl.when(kj == n_pair - 1)
        def _():
            dq_ref[...] = dq_acc[...].astype(BF16)

    head_rows = pl.BlockSpec((seq, VDIM), lambda h, j: (0, h))
    return _call(body, name, (N_HEADS, n_pair),
                 [pl.BlockSpec((None, seq, QK), lambda h, j: (h, 0, 0)),
                  pl.BlockSpec((None, 2 * t, QK), lambda h, j: (h, j, 0)),
                  pl.BlockSpec((None, 2 * t, VDIM), lambda h, j: (h, j, 0)),
                  head_rows, head_rows,
                  pl.BlockSpec((None, seq, LANES), lambda h, j: (h, 0, 0))],
                 (pl.BlockSpec((None, seq, QK), lambda h, j: (h, 0, 0)),
                  pl.BlockSpec((None, 2 * t, QK), lambda h, j: (h, j, 0)),
                  pl.BlockSpec((None, 2 * t, VDIM), lambda h, j: (h, j, 0))),
                 (jax.ShapeDtypeStruct((N_HEADS, seq, QK), BF16),
                  jax.ShapeDtypeStruct((N_HEADS, seq, QK), BF16),
                  jax.ShapeDtypeStruct((N_HEADS, seq, VDIM), BF16)),
                 (q, k, v, o, do, lse),
                 scratch=[pltpu.VMEM((seq, QK), F32), pltpu.VMEM((2 * t, QK), F32),
                          pltpu.VMEM((2 * t, VDIM), F32)], rider=rider)


def _pack(parts):
    rows = []
    for p in parts:
        flat = p.reshape(-1)
        n_rows = -(-flat.shape[0] // (8 * LANES)) * 8
        flat = jnp.pad(flat, (0, n_rows * LANES - flat.shape[0]))
        rows.append(flat.reshape(n_rows, LANES))
    return jnp.concatenate(rows, axis=0)


def _unpack(packed, shapes):
    lead = packed.shape[:-2]
    out, r0 = [], 0
    for shape in shapes:
        size = 1
        for s in shape:
            size *= s
        n_rows = -(-size // (8 * LANES)) * 8
        part = packed[..., r0:r0 + n_rows, :].reshape(lead + (n_rows * LANES,))
        out.append(part[..., :size].reshape(lead + tuple(shape)))
        r0 += n_rows
    return out


FWD_RIDERS = {
    "mixer_fwd0": [("ffn_w_up", 0)],
    "ffn_fwd0": [("ffn_w_down", 0), ("a_w_in", 1), ("a_w_out", 1)],
    "mixer_fwd1": [("ffn_w_up", 1)],
    "ffn_fwd1": [("ffn_w_down", 1), ("w_dkv", 0), ("w_ukv", 0), ("b_w_dq", 0), ("b_w_uq", 0)],
    "attn_fwd0": [("b_w_o", 0), ("ffn_w_up", 2), ("ffn_w_down", 2), ("b_w_dq", 1), ("b_w_uq", 1)],
    "attn_fwd1": [("b_w_o", 1), ("ffn_w_up", 3), ("ffn_w_down", 3)],
}
BWD_RIDERS = {
    "attn_bwd1": [("ffn_w_down", 3), ("ffn_w_up", 3), ("b_w_o", 1)],
    "attn_bwd0": [("ffn_w_down", 2), ("ffn_w_up", 2), ("b_w_o", 0)],
    "ffn_bwd1": [("b_w_uq", 1), ("b_w_dq", 1), ("b_w_uq", 0), ("b_w_dq", 0), ("w_ukv", 0), ("w_dkv", 0)],
    "ffn_in_bwd1": [("ffn_w_down", 1), ("ffn_w_up", 1, "pair")],
    "ffn_bwd0": [("ffn_w_up", 1, "chip"), ("a_w_in", 1), ("a_w_out", 1)],
    "ffn_in_bwd0": [("ffn_w_down", 0), ("ffn_w_up", 0, "pair")],
    "mixer_bwd0": [("ffn_w_up", 0, "chip")],
    "mixer_in_bwd0": [("a_w_out", 0), ("a_w_in", 0, "pair")],
    "adamw_a_w_out": [("a_w_in", 0, "chip")],
}


def kernel(x, a_mix_norm, a_w_in, a_conv, a_w_out, b_mix_norm, b_w_dq, b_q_norm, b_w_uq, b_w_o, kv_in_norm, w_dkv, kv_norm, w_ukv, ffn_norm, ffn_w_up, ffn_conv, ffn_w_down, final_norm, loss_target, m_a_mix_norm, m_a_w_in, m_a_conv, m_a_w_out, m_b_mix_norm, m_b_w_dq, m_b_q_norm, m_b_w_uq, m_b_w_o, m_kv_in_norm, m_w_dkv, m_kv_norm, m_w_ukv, m_ffn_norm, m_ffn_w_up, m_ffn_conv, m_ffn_w_down, m_final_norm, v_a_mix_norm, v_a_w_in, v_a_conv, v_a_w_out, v_b_mix_norm, v_b_w_dq, v_b_q_norm, v_b_w_uq, v_b_w_o, v_kv_in_norm, v_w_dkv, v_kv_norm, v_w_ukv, v_ffn_norm, v_ffn_w_up, v_ffn_conv, v_ffn_w_down, v_final_norm):
    seq, d = x.shape[1], x.shape[2]
    me = 4 * lax.axis_index("x") + 2 * lax.axis_index("y") + lax.axis_index("c")
    h0 = x.reshape(seq, d)
    target = loss_target.reshape(seq, d)
    cos, sin = _rope_tables(seq)
    rank = b_w_dq.shape[-1]
    f8 = ffn_w_up.shape[-1]
    fd = ffn_w_down.shape[1]
    dshard = a_w_out.shape[1]
    hv = N_HEADS * VDIM

    shards = {"a_w_in": a_w_in, "a_w_out": a_w_out, "b_w_dq": b_w_dq, "b_w_uq": b_w_uq,
              "b_w_o": b_w_o, "w_dkv": w_dkv[None], "w_ukv": w_ukv[None],
              "ffn_w_up": ffn_w_up, "ffn_w_down": ffn_w_down}

    def relayout(name, g):
        if name == "a_w_in":
            w = jnp.transpose(g, (1, 0, 2)).reshape(d, 3, d)
            return jnp.transpose(w, (1, 0, 2))
        if name == "a_w_out":
            return g.reshape(d, d)
        if name == "b_w_dq":
            return g.reshape(d, rank)
        if name == "b_w_o":
            return g.reshape(hv, d)
        if name == "w_dkv":
            return g.reshape(d, KV_RANK + ROPE_PAD)
        if name == "ffn_w_down":
            return g.reshape(N_DEV // 2, 2 * fd, d)
        return g

    weights = {}

    shards_bf16 = {n: w.astype(BF16) for n, w in shards.items()}
    shards_bf16["b_w_uq"] = jnp.pad(shards_bf16["b_w_uq"], ((0, 0), (0, 0), (0, QK - NOPE - ROPE)))
    shards_bf16["w_dkv"] = jnp.pad(shards_bf16["w_dkv"], ((0, 0), (0, 0), (0, ROPE_PAD - ROPE)))

    def ag_rider(host):
        return [(("ag", l), shards_bf16[n]) for n, l in FWD_RIDERS.get(host, [])]

    def ag_done(host, outs):
        for (n, l), g in zip(FWD_RIDERS.get(host, []), outs):
            weights[n, l] = relayout(n, g)

    small_shapes = [a_mix_norm.shape, a_conv.shape, ffn_conv.shape]
    first = exchange([(("ag", 0), shards_bf16["a_w_in"]), (("ag", 0), shards_bf16["a_w_out"]),
                      ("ag", _pack([a_mix_norm, a_conv, ffn_conv]))], "ag_first")
    weights["a_w_in", 0] = relayout("a_w_in", first[0])
    weights["a_w_out", 0] = relayout("a_w_out", first[1])
    s_mix, s_aconv, s_fconv = _unpack(first[2], small_shapes)
    a_gain = jnp.transpose(s_mix, (1, 0, 2)).reshape(N_A, d)
    a_cw = jnp.transpose(s_aconv, (1, 2, 0, 3)).reshape(N_A, 3, d)
    f_cw = jnp.transpose(s_fconv, (1, 0, 2, 3))

    def mixer_gain(layer):
        if layer >= DEPTH:
            return None
        return a_gain[layer][None] if layer < N_A else b_mix_norm[layer - N_A][None]

    saved = {}
    h = h0
    xn = norm_fwd(h, mixer_gain(0), "norm_first")
    kv = None
    for layer in range(DEPTH):
        saved["hm", layer], saved["xm", layer] = h, xn
        if layer < N_A:
            name = f"mixer_fwd{layer}"
            (u4, z), r = mixer_fwd(xn, weights["a_w_in", layer], a_cw[layer], name, rider=ag_rider(name))
            ag_done(name, r)
            saved["mix", layer] = (u4, z)
            name = f"mixer_out{layer}"
            (h, xn), r = proj_residual(z[None], weights["a_w_out", layer][None], h, name,
                                       g_next=ffn_norm[layer][None], rider=ag_rider(name))
            ag_done(name, r)
        else:
            j = layer - N_A
            name = f"q_fwd{j}"
            (q,), r = q_fwd(xn, weights["b_w_dq", j], b_q_norm[j][None], weights["b_w_uq", j],
                            cos, sin, name, rider=ag_rider(name))
            ag_done(name, r)
            name = f"attn_fwd{j}"
            (o, lse), r = attn_fwd(q, kv[0], kv[1], name, rider=ag_rider(name))
            ag_done(name, r)
            saved["attn", layer] = (q, o, lse)
            name = f"attn_out{j}"
            (h, xn), r = proj_residual(o[None], weights["b_w_o", j][None], h, name,
                                       g_next=ffn_norm[layer][None], rider=ag_rider(name))
            ag_done(name, r)
        saved["hf", layer], saved["xf", layer] = h, xn
        name = f"ffn_fwd{layer}"
        (up2, cv2, act), r = ffn_fwd(xn, weights["ffn_w_up", layer], f_cw[layer], name, rider=ag_rider(name))
        ag_done(name, r)
        saved["ffn", layer] = (up2, cv2, act)
        name = f"ffn_out{layer}"
        if layer == DEPTH - 1:
            sq_err, dh, d_final = proj_residual_loss(act, weights["ffn_w_down", layer], h,
                                                     final_norm[None], target, name)
            break
        (h, xn), r = proj_residual(act, weights["ffn_w_down", layer], h, name,
                                   g_next=mixer_gain(layer + 1), rider=ag_rider(name))
        ag_done(name, r)
        if layer == N_A - 1:
            (k_all, v_all, c_kv), r = kv_fwd(h, kv_in_norm[None], weights["w_dkv", 0], kv_norm[None],
                                             weights["w_ukv", 0], cos, sin, "kv_fwd",
                                             rider=ag_rider("kv_fwd"))
            ag_done("kv_fwd", r)
            kv = (k_all, v_all, c_kv)


    grads = {}
    parts = {}

    pair_sums = {}

    def by_chip(g):
        return g.reshape((N_DEV // 2, 2) + g.shape[1:])

    def rs_rider(host):
        tasks = []
        for key in BWD_RIDERS.get(host, []):
            if len(key) == 2:
                tasks.append(("rs", grads[key]))
            elif key[2] == "pair":
                tasks.append(("rs_pair", by_chip(grads[key[:2]])))
            else:
                tasks.append(("rs_chip", pair_sums[key[:2]]))
        return tasks

    def rs_done(host, outs):
        for key, p in zip(BWD_RIDERS.get(host, []), outs):
            if len(key) == 3 and key[2] == "pair":
                pair_sums[key[:2]] = pair_sum(by_chip(grads[key[:2]]), p, f"pair_sum_{key[0]}{key[1]}")
            else:
                parts[key[:2]] = p

    d_ffn_norm = [None] * DEPTH
    d_fconv = [None] * DEPTH
    d_a_gain = [None] * N_A
    d_aconv = [None] * N_A
    d_b_gain = [None] * N_B
    d_q_gain = [None] * N_B
    dks, dvs = [], []
    for layer in reversed(range(DEPTH)):
        if layer == N_A - 1:
            hk = saved["hm", layer + 1]
            (dh, dwukv, dwdkv, d_kv_gain, d_kvin_gain), r = kv_bwd(
                dks, dvs, kv[2], hk, kv_in_norm[None], dh, weights["w_dkv", 0], kv_norm[None],
                weights["w_ukv", 0], cos, sin, "kv_bwd", rider=rs_rider("kv_bwd"))
            rs_done("kv_bwd", r)
            grads["w_ukv", 0] = dwukv
            grads["w_dkv", 0] = dwdkv[:, :KV_RANK + ROPE].reshape(N_DEV, dshard, KV_RANK + ROPE)
        up2, cv2, act = saved["ffn", layer]
        name = f"ffn_bwd{layer}"
        (dup2, dwup, dwdown, dcw), r = ffn_bwd(dh, weights["ffn_w_down", layer], up2, cv2, act,
                                               saved["xf", layer], f_cw[layer], name, rider=rs_rider(name))
        rs_done(name, r)
        grads["ffn_w_up", layer] = dwup.reshape(N_DEV, d, f8)
        grads["ffn_w_down", layer] = dwdown.reshape(N_DEV, fd, d)
        d_fconv[layer] = dcw.reshape(N_DEV, 3, f8)
        name = f"ffn_in_bwd{layer}"
        (dh, d_ffn_norm[layer]), r = proj_t_rms_bwd(dup2.reshape(N_DEV, seq, f8), weights["ffn_w_up", layer],
                                                    saved["hf", layer], ffn_norm[layer][None], dh, name,
                                                    rider=rs_rider(name))
        rs_done(name, r)
        hm, xm = saved["hm", layer], saved["xm", layer]
        if layer < N_A:
            u4, z = saved["mix", layer]
            name = f"mixer_bwd{layer}"
            (du3, dwin3, dwout, dcw), r = mixer_bwd(dh, weights["a_w_out", layer], u4, z, xm, a_cw[layer],
                                                    name, rider=rs_rider(name))
            rs_done(name, r)
            dwin = jnp.transpose(dwin3, (1, 0, 2)).reshape(d, N_DEV, 3 * d // N_DEV)
            grads["a_w_in", layer] = jnp.transpose(dwin, (1, 0, 2))
            grads["a_w_out", layer] = dwout.reshape(N_DEV, dshard, d)
            d_aconv[layer] = dcw
            name = f"mixer_in_bwd{layer}"
            extra = []
            if layer == 0:
                early_small = [
                    d_a_gain[1],
                    d_aconv[1],
                    jnp.concatenate(d_b_gain, axis=0),
                    jnp.concatenate(d_q_gain, axis=0),
                    d_kvin_gain[0],
                    d_kv_gain[0],
                    jnp.concatenate(d_ffn_norm, axis=0),
                    jnp.stack(d_fconv),
                    d_final[0],
                ]
                extra = [("ag", _pack(early_small))]
            (dh, d_a_gain[layer]), r = proj_t_rms_bwd(du3, weights["a_w_in", layer], hm, a_gain[layer][None],
                                                      dh, name, rider=rs_rider(name) + extra)
            rs_done(name, r)
            if layer == 0:
                g_early = r[-1]
        else:
            j = layer - N_A
            q, o, lse = saved["attn", layer]
            name = f"attn_out_bwd{j}"
            (do, dwo), r = o_bwd(dh, o, weights["b_w_o", j], name, rider=rs_rider(name))
            rs_done(name, r)
            grads["b_w_o", j] = dwo.reshape(N_DEV, dshard, d)
            name = f"attn_bwd{j}"
            (dq, dk, dv), r = attn_bwd(q, kv[0], kv[1], o, do, lse, name, rider=rs_rider(name))
            rs_done(name, r)
            dks.append(dk)
            dvs.append(dv)
            name = f"q_bwd{j}"
            (dh, dwuq, dwdq, d_q_gain[j], d_b_gain[j]), r = q_bwd(
                dq, xm, hm, b_mix_norm[j][None], dh, weights["b_w_dq", j], b_q_norm[j][None],
                weights["b_w_uq", j], cos, sin, name, rider=rs_rider(name))
            rs_done(name, r)
            grads["b_w_uq", j] = dwuq[:, :, :NOPE + ROPE]
            grads["b_w_dq", j] = dwdq.reshape(N_DEV, dshard, rank)
    grad_x = dh.reshape(x.shape)

    late_small = [d_a_gain[0], d_aconv[0], sq_err[:, 0:1]]
    full_shapes = [t.shape for t in early_small + late_small]
    small_pack = _pack(late_small)

    res = {}

    def update(name, n_layers, w, m, v, extra=(), transposed=False):
        view = (lambda t: jnp.transpose(t, (0, 2, 1))) if transposed else (lambda t: t)
        call = sum_adamw_transposed if transposed else sum_adamw
        shard = w.shape if w.ndim == 3 else (1,) + w.shape
        host = f"adamw_{name}"
        outs, r = call([parts[name, l] for l in range(n_layers)], view(w.reshape(shard)),
                       view(m.reshape(shard)), view(v.reshape(shard)), host,
                       rider=rs_rider(host) + list(extra))
        rs_done(host, r)
        res[name] = [view(t).reshape(w.shape) for t in outs]
        return r[len(BWD_RIDERS.get(host, [])):]

    (g_late,) = update("a_w_out", N_A, a_w_out, m_a_w_out, v_a_w_out, extra=[("ag", small_pack)])
    update("ffn_w_down", DEPTH, ffn_w_down, m_ffn_w_down, v_ffn_w_down)
    update("ffn_w_up", DEPTH, ffn_w_up, m_ffn_w_up, v_ffn_w_up, transposed=True)
    update("b_w_dq", N_B, b_w_dq, m_b_w_dq, v_b_w_dq)
    update("b_w_uq", N_B, b_w_uq, m_b_w_uq, v_b_w_uq)
    update("b_w_o", N_B, b_w_o, m_b_w_o, v_b_w_o)
    update("w_dkv", 1, w_dkv, m_w_dkv, v_w_dkv)
    update("w_ukv", 1, w_ukv, m_w_ukv, v_w_ukv)
    update("a_w_in", N_A, a_w_in, m_a_w_in, v_a_w_in)

    summed = sum_slots(jnp.concatenate([g_early, g_late], axis=1), "sum_small_grads")
    (s_a_gain1, s_aconv1, s_b_gain, s_q_gain, s_kvin, s_kvn, s_ffn_gain, s_fconv_g,
     s_final, s_a_gain0, s_aconv0, s_sq_err) = _unpack(summed, full_shapes)
    loss = s_sq_err[0, 0] * (0.5 / d)
    s_a_gain = jnp.concatenate([s_a_gain0, s_a_gain1], axis=0)
    s_aconv_g = jnp.stack([s_aconv0, s_aconv1])
    dsl = d // N_DEV
    small = [
        ("a_mix_norm", lax.dynamic_slice_in_dim(s_a_gain, me * dsl, dsl, axis=1), a_mix_norm, m_a_mix_norm, v_a_mix_norm),
        ("a_conv", lax.dynamic_slice_in_dim(s_aconv_g, me * dsl, dsl, axis=2), a_conv, m_a_conv, v_a_conv),
        ("b_mix_norm", s_b_gain, b_mix_norm, m_b_mix_norm, v_b_mix_norm),
        ("b_q_norm", s_q_gain, b_q_norm, m_b_q_norm, v_b_q_norm),
        ("kv_in_norm", s_kvin, kv_in_norm, m_kv_in_norm, v_kv_in_norm),
        ("kv_norm", s_kvn, kv_norm, m_kv_norm, v_kv_norm),
        ("ffn_norm", s_ffn_gain, ffn_norm, m_ffn_norm, v_ffn_norm),
        ("ffn_conv", lax.dynamic_index_in_dim(s_fconv_g, me, axis=1, keepdims=False), ffn_conv, m_ffn_conv, v_ffn_conv),
        ("final_norm", s_final, final_norm, m_final_norm, v_final_norm),
    ]
    shapes = [t[2].shape for t in small]
    packed = [_pack([t[k] for t in small])[None] for k in (1, 2, 3, 4)]
    outs, _ = sum_adamw([packed[0]], packed[1], packed[2], packed[3], "adamw_small")
    unpacked = [_unpack(t[0], shapes) for t in outs]
    for idx, t in enumerate(small):
        res[t[0]] = [unpacked[k][idx] for k in range(4)]

    order = ["a_mix_norm", "a_w_in", "a_conv", "a_w_out", "b_mix_norm", "b_w_dq", "b_q_norm",
             "b_w_uq", "b_w_o", "kv_in_norm", "w_dkv", "kv_norm", "w_ukv", "ffn_norm",
             "ffn_w_up", "ffn_conv", "ffn_w_down", "final_norm"]
    return (loss, grad_x, *[res[n][0] for n in order], *[res[n][1] for n in order],
            *[res[n][2] for n in order], *[res[n][3] for n in order])
```
